```python
import jax
import jax.numpy as jnp
from jax import lax

D_MODEL = 1024
BATCH = 4
SEQ = 4096
DEPTH = 1

GRID_W = 64
CTX_LEN = 256
CHUNK = 128
GMLP_HEADS = 4
GMLP_W = D_MODEL // 2
GMLP_DH = GMLP_W // GMLP_HEADS
MLSTM_HEADS = 4
MLSTM_W = D_MODEL // 2
MLSTM_DH = MLSTM_W // MLSTM_HEADS
D_MIX = GMLP_W + MLSTM_W
CONV_W = 3
N_EXPERTS = 16
EC_FACTOR = 2
D_EXPERT = 2 * D_MODEL
EPS = 1e-6

U_COL = 0
VG_COL = GMLP_W
Q_COL = 2 * GMLP_W
O_COL = Q_COL + MLSTM_W
K_COL = O_COL + MLSTM_W
PROJ_W = K_COL + 2 * MLSTM_W + 4 * MLSTM_HEADS

kernel_name = 'hybrid_gmlp_mlstm_expert_choice_dit_block'


def rms_norm(x, g):
    xf = x.astype(jnp.float32)
    y = xf * lax.rsqrt(jnp.mean(xf * xf, axis=-1, keepdims=True) + EPS)
    return (y * g).astype(x.dtype)


def layer_norm_plain(x):
    xf = x.astype(jnp.float32)
    mu = jnp.mean(xf, axis=-1, keepdims=True)
    var = jnp.mean(jnp.square(xf - mu), axis=-1, keepdims=True)
    return ((xf - mu) * lax.rsqrt(var + EPS)).astype(x.dtype)


def adaln(cond, w, b, n):
    m = jax.nn.silu(cond) @ w[:, :n * D_MODEL] + b[:n * D_MODEL]
    return jnp.split(m, n, axis=-1)


def centred_dwconv(x, w):
    k = w.shape[0]
    pad = k // 2
    n = x.shape[1]
    xp = jnp.pad(x, ((0, 0), (pad, pad), (0, 0)))
    return sum(xp[:, j:j + n] * w[j] for j in range(k))


def to_heads(a, n_heads):
    b, n, w = a.shape
    return a.reshape(b, n, n_heads, w // n_heads).transpose(0, 2, 1, 3)


def gmlp_mix(u_pre, v_pre, ws, bs, n_chunks):
    b, n, _ = u_pre.shape
    u = jax.nn.gelu(u_pre)
    v = layer_norm_plain(jax.nn.gelu(v_pre))
    v = v.reshape(b, n_chunks, CHUNK, GMLP_HEADS, GMLP_DH)
    s = jnp.einsum('hpq,bnqhd->bnphd', ws, v) + bs.T[None, None, :, :, None]
    return u * s.reshape(b, n, GMLP_W)


def zero_state(b):
    return (jnp.zeros((b, MLSTM_HEADS, MLSTM_DH, MLSTM_DH), jnp.float32),
            jnp.zeros((b, MLSTM_HEADS, MLSTM_DH), jnp.float32),
            jnp.zeros((b, MLSTM_HEADS), jnp.float32))


def mlstm_kv_gates(p_kv, conv_k, b_i, b_f):
    b, n, _ = p_kv.shape
    k = jax.nn.silu(centred_dwconv(p_kv[..., :MLSTM_W], conv_k)) * (MLSTM_DH ** -0.5)
    v = p_kv[..., MLSTM_W:2 * MLSTM_W]
    g0 = 2 * MLSTM_W
    i_pre = p_kv[..., g0:g0 + 2 * MLSTM_HEADS].astype(jnp.float32).reshape(b, n, 2, MLSTM_HEADS) + b_i
    f_pre = p_kv[..., g0 + 2 * MLSTM_HEADS:].astype(jnp.float32).reshape(b, n, 2, MLSTM_HEADS) + b_f
    log_i = jnp.transpose(i_pre, (2, 0, 3, 1))
    log_f = jax.nn.log_sigmoid(jnp.transpose(f_pre, (2, 0, 3, 1)))
    return to_heads(k, MLSTM_HEADS), to_heads(v, MLSTM_HEADS), log_i, log_f


def mlstm_scan(q, k, v, log_i, log_f, state):
    b, h, n, dh = q.shape
    nc = n // CHUNK

    def to_chunks(a):
        a = a.astype(jnp.float32)
        return jnp.moveaxis(a.reshape((b, h, nc, CHUNK) + a.shape[3:]), 2, 0)

    xs = tuple(to_chunks(a) for a in (q, k, v, log_i, log_f))
    causal = jnp.tril(jnp.ones((CHUNK, CHUNK), dtype=bool))

    def step(carry, inp):
        c_prev, n_prev, m_prev = carry
        qc, kc, vc, li, lf = inp
        bcum = jnp.cumsum(lf, axis=-1)
        g = bcum + m_prev[..., None]
        dmat = jnp.where(causal, bcum[..., :, None] - bcum[..., None, :] + li[..., None, :], -jnp.inf)
        m_t = jnp.maximum(g, jnp.max(dmat, axis=-1))
        inter = jnp.exp(g - m_t)
        s = jnp.einsum('bhtd,bhsd->bhts', qc, kc) * jnp.exp(dmat - m_t[..., None])
        num = inter[..., None] * jnp.einsum('bhvd,bhtd->bhtv', c_prev, qc) + jnp.einsum('bhts,bhsv->bhtv', s, vc)
        den = inter * jnp.einsum('bhd,bhtd->bht', n_prev, qc) + jnp.sum(s, axis=-1)
        h_out = num / jnp.maximum(jnp.abs(den), jnp.exp(-m_t))[..., None]
        b_last = bcum[..., -1]
        a = b_last[..., None] - bcum + li
        m_new = jnp.maximum(b_last + m_prev, jnp.max(a, axis=-1))
        decay = jnp.exp(b_last + m_prev - m_new)
        wk = jnp.exp(a - m_new[..., None])[..., None] * kc
        c_new = decay[..., None, None] * c_prev + jnp.einsum('bhsv,bhsd->bhvd', vc, wk)
        n_new = decay[..., None] * n_prev + jnp.sum(wk, axis=2)
        return (c_new, n_new, m_new), h_out

    state, hs = lax.scan(step, state, xs)
    return jnp.moveaxis(hs, 0, 2).reshape(b, h, n, dh), state


def mlstm_state_from_zero(k, v, log_i, log_f):
    k = k.astype(jnp.float32)
    v = v.astype(jnp.float32)
    bcum = jnp.cumsum(log_f, axis=-1)
    b_last = bcum[..., -1]
    a = b_last[..., None] - bcum + log_i
    m = jnp.maximum(b_last, jnp.max(a, axis=-1))
    wk = jnp.exp(a - m[..., None])[..., None] * k
    return (jnp.einsum('bhsv,bhsd->bhvd', v, wk), jnp.sum(wk, axis=2), m)


def head_norm(h, g):
    mu = jnp.mean(h, axis=-1, keepdims=True)
    var = jnp.mean(jnp.square(h - mu), axis=-1, keepdims=True)
    hn = (h - mu) * lax.rsqrt(var + EPS)
    b, nh, n, dh = h.shape
    return hn.transpose(0, 2, 1, 3).reshape(b, n, nh * dh) * g


def hybrid_mixer(p, n_chunks, conv_q, conv_k, b_i, b_f, ws, bs, norm_g, w_out, st_f, st_b):
    y_g = gmlp_mix(p[..., U_COL:U_COL + GMLP_W], p[..., VG_COL:VG_COL + GMLP_W], ws, bs, n_chunks)
    q = to_heads(jax.nn.silu(centred_dwconv(p[..., Q_COL:Q_COL + MLSTM_W], conv_q)), MLSTM_HEADS)
    o = jax.nn.sigmoid(p[..., O_COL:O_COL + MLSTM_W])
    k, v, log_i, log_f = mlstm_kv_gates(p[..., K_COL:], conv_k, b_i, b_f)
    h_f, st_f = mlstm_scan(q, k, v, log_i[0], log_f[0], st_f)
    h_b, st_b = mlstm_scan(jnp.flip(q, 2), jnp.flip(k, 2), jnp.flip(v, 2),
                           jnp.flip(log_i[1], 2), jnp.flip(log_f[1], 2), st_b)
    h = head_norm(h_f + jnp.flip(h_b, 2), norm_g).astype(p.dtype)
    y_m = o * h
    y = jnp.concatenate([y_g, y_m], axis=-1) @ w_out
    return y, st_f, st_b


def expert_choice_ffn(h, w_router, w_gate, w_up, w_down, capacity):
    n_tok, d = h.shape[1], h.shape[2]
    aff = jax.nn.softmax(jnp.einsum('bnd,de->bne', h, w_router).astype(jnp.float32), axis=-1)
    gate, idx = lax.top_k(jnp.swapaxes(aff, 1, 2), capacity)
    xe = jax.vmap(lambda hb, ib: hb[ib])(h, idx)
    a = jnp.einsum('becd,edf->becf', xe, w_gate)
    u = jnp.einsum('becd,edf->becf', xe, w_up)
    ye = jnp.einsum('becf,efd->becd', jax.nn.silu(a) * u, w_down) * gate[..., None].astype(h.dtype)
    scatter = lambda ib, yb: jnp.zeros((n_tok, d), yb.dtype).at[ib.reshape(-1)].add(yb.reshape(-1, d))
    return jax.vmap(scatter)(idx, ye)


def setup_inputs(seed: int = 0) -> dict:
    key = jax.random.key(seed)
    ks = jax.random.split(key, 22)
    nrm = lambda k, shape, s: jax.random.normal(k, shape, jnp.float32) * s
    f_bias = jnp.linspace(3.0, 6.0, MLSTM_HEADS, dtype=jnp.float32)
    return {
        'x': nrm(ks[0], (BATCH, SEQ, D_MODEL), 1.0),
        'c': nrm(ks[1], (BATCH, D_MODEL), 1.0),
        'ctx': nrm(ks[2], (BATCH, CTX_LEN, D_MODEL), 1.0),
        'c_ctx': nrm(ks[3], (D_MODEL,), 1.0),
        'w_mod': nrm(ks[4], (DEPTH, D_MODEL, 6 * D_MODEL), 0.5 * D_MODEL ** -0.5),
        'b_mod': nrm(ks[5], (DEPTH, 6 * D_MODEL), 0.02),
        'norm_mix_g': 1.0 + nrm(ks[6], (DEPTH, D_MODEL), 0.02),
        'w_in': nrm(ks[7], (DEPTH, D_MODEL, PROJ_W), D_MODEL ** -0.5),
        'conv_q': nrm(ks[8], (DEPTH, CONV_W, MLSTM_W), CONV_W ** -0.5),
        'conv_k': nrm(ks[9], (DEPTH, CONV_W, MLSTM_W), CONV_W ** -0.5),
        'b_igate': nrm(ks[10], (DEPTH, 2, MLSTM_HEADS), 0.1),
        'b_fgate': f_bias + nrm(ks[11], (DEPTH, 2, MLSTM_HEADS), 0.1),
        'gmlp_ws': nrm(ks[12], (DEPTH, GMLP_HEADS, CHUNK, CHUNK), CHUNK ** -0.5),
        'gmlp_bs': 1.0 + nrm(ks[13], (DEPTH, GMLP_HEADS, CHUNK), 0.02),
        'mlstm_norm_g': 1.0 + nrm(ks[14], (DEPTH, MLSTM_W), 0.02),
        'w_out': nrm(ks[15], (DEPTH, D_MIX, D_MODEL), D_MIX ** -0.5),
        'norm_ffn_g': 1.0 + nrm(ks[16], (DEPTH, D_MODEL), 0.02),
        'w_router': nrm(ks[17], (DEPTH, D_MODEL, N_EXPERTS), D_MODEL ** -0.5),
        'w_gate_e': nrm(ks[18], (DEPTH, N_EXPERTS, D_MODEL, D_EXPERT), D_MODEL ** -0.5),
        'w_up_e': nrm(ks[19], (DEPTH, N_EXPERTS, D_MODEL, D_EXPERT), D_MODEL ** -0.5),
        'w_down_e': nrm(ks[20], (DEPTH, N_EXPERTS, D_EXPERT, D_MODEL), D_EXPERT ** -0.5),
        'final_g': 1.0 + nrm(ks[21], (D_MODEL,), 0.02),
    }


def reference(x, c, ctx, c_ctx, w_mod, b_mod, norm_mix_g, w_in, conv_q, conv_k, b_igate, b_fgate,
              gmlp_ws, gmlp_bs, mlstm_norm_g, w_out, norm_ffn_g, w_router, w_gate_e, w_up_e,
              w_down_e, final_g):
    rows = x.shape[1] // GRID_W
    n_lat = rows * GRID_W
    lat_chunks = n_lat // CHUNK
    cap_lat = EC_FACTOR * n_lat // N_EXPERTS
    cap_ctx = EC_FACTOR * ctx.shape[1] // N_EXPERTS
    for l in range(DEPTH):
        h_c = rms_norm(ctx, norm_mix_g[l])
        if l == DEPTH - 1:
            sh_c, sc_c = adaln(c_ctx, w_mod[l], b_mod[l], 2)
            p_c = (h_c * (1.0 + sc_c) + sh_c) @ w_in[l][:, K_COL:]
            k_c, v_c, li_c, lf_c = mlstm_kv_gates(p_c, conv_k[l], b_igate[l], b_fgate[l])
            st_f = mlstm_state_from_zero(k_c, v_c, li_c[0], lf_c[0])
            st_b = mlstm_state_from_zero(jnp.flip(k_c, 2), jnp.flip(v_c, 2),
                                         jnp.flip(li_c[1], 2), jnp.flip(lf_c[1], 2))
        else:
            sh_c, sc_c, g_c, sh2_c, sc2_c, g2_c = adaln(c_ctx, w_mod[l], b_mod[l], 6)
            zero = zero_state(ctx.shape[0])
            y_c, st_f, st_b = hybrid_mixer((h_c * (1.0 + sc_c) + sh_c) @ w_in[l], ctx.shape[1] // CHUNK,
                                           conv_q[l], conv_k[l], b_igate[l], b_fgate[l], gmlp_ws[l],
                                           gmlp_bs[l], mlstm_norm_g[l], w_out[l], zero, zero)
            ctx = ctx + g_c * y_c
            h2_c = rms_norm(ctx, norm_ffn_g[l]) * (1.0 + sc2_c) + sh2_c
            ctx = ctx + g2_c * expert_choice_ffn(h2_c, w_router[l], w_gate_e[l], w_up_e[l], w_down_e[l], cap_ctx)
        sh, sc, g, sh2, sc2, g2 = (m[:, None, :] for m in adaln(c, w_mod[l], b_mod[l], 6))
        h = rms_norm(x, norm_mix_g[l]) * (1.0 + sc) + sh
        y, _, _ = hybrid_mixer(h @ w_in[l], lat_chunks, conv_q[l], conv_k[l], b_igate[l], b_fgate[l],
                               gmlp_ws[l], gmlp_bs[l], mlstm_norm_g[l], w_out[l], st_f, st_b)
        x = x + g * y
        h2 = rms_norm(x, norm_ffn_g[l]) * (1.0 + sc2) + sh2
        x = x + g2 * expert_choice_ffn(h2, w_router[l], w_gate_e[l], w_up_e[l], w_down_e[l], cap_lat)
    return rms_norm(x, final_g)
```

```python
import functools

import jax
import jax.numpy as jnp
from jax import lax
from jax.experimental import pallas as pl
from jax.experimental.pallas import tpu as pltpu

F32 = jnp.float32
BF16 = jnp.bfloat16

D_MODEL = 1024
GRID_W = 64
CHUNK = 128
HEADS = 4
GROUP_W = D_MODEL // 2
HEAD_DIM = GROUP_W // HEADS
N_EXPERTS = 16
EC_FACTOR = 2
D_EXPERT = 2 * D_MODEL
EPS = 1e-6
N_GATES = 4 * HEADS
GATE_PAD = 128
MOD_ROWS = 8

U_BLK, VG_BLK, Q_BLK, O_BLK, K_BLK, V_BLK = 0, 1, 2, 3, 4, 5
MAIN_W = 6 * GROUP_W

VMEM_LIMIT = 56 * 1024 * 1024


def _params(*sem):
    return pltpu.CompilerParams(dimension_semantics=sem, vmem_limit_bytes=VMEM_LIMIT)


def _dot(a, b):
    return jnp.dot(a, b, preferred_element_type=F32)


def _dot_nt(a, b):
    return lax.dot_general(a, b, (((1,), (1,)), ((), ())), preferred_element_type=F32)


def _split2(a):
    hi = a.astype(BF16)
    lo = (a - hi.astype(F32)).astype(BF16)
    return hi, lo


def _dot3(a, b):
    ah, al = _split2(a)
    bh, bl = _split2(b)
    return _dot(ah, bh) + (_dot(al, bh) + _dot(ah, bl))


def _dot_exact01(tri, x):
    x1 = x.astype(BF16)
    r1 = x - x1.astype(F32)
    x2 = r1.astype(BF16)
    x3 = (r1 - x2.astype(F32)).astype(BF16)
    return _dot(tri, x1) + (_dot(tri, x2) + _dot(tri, x3))


def _iota2(shape, dim):
    return lax.broadcasted_iota(jnp.int32, shape, dim)


def _row_to_col(row):
    n = row.shape[1] // 128
    eye = _iota2((128, 128), 0) == _iota2((128, 128), 1)
    cols = [jnp.sum(jnp.where(eye, row[:, j * 128:(j + 1) * 128], 0.0), axis=1, keepdims=True)
            for j in range(n)]
    return cols[0] if n == 1 else jnp.concatenate(cols, axis=0)


def _adaln_kernel(cond_ref, w_ref, b_ref, o_ref):
    o_ref[...] = _dot3(jax.nn.silu(cond_ref[...]), w_ref[...]) + b_ref[...]


def _adaln(cond, w, b):
    n_out = w.shape[1]
    tn = D_MODEL
    return pl.pallas_call(
        _adaln_kernel,
        grid=(n_out // tn,),
        in_specs=[pl.BlockSpec((MOD_ROWS, D_MODEL), lambda j: (0, 0)),
                  pl.BlockSpec((D_MODEL, tn), lambda j: (0, j)),
                  pl.BlockSpec((1, tn), lambda j: (0, j))],
        out_specs=pl.BlockSpec((MOD_ROWS, tn), lambda j: (0, j)),
        out_shape=jax.ShapeDtypeStruct((MOD_ROWS, n_out), F32),
        compiler_params=_params("arbitrary"),
        name="adaln",
    )(cond, w, b)


def _inproj_kernel(x_ref, mod_ref, g_ref, w_ref, wg_ref, p_ref, gate_ref, *, ctx_row):
    row = pl.program_id(0) if ctx_row is None else ctx_row
    x = x_ref[0]
    y = x * lax.rsqrt(jnp.mean(x * x, axis=-1, keepdims=True) + EPS) * g_ref[...]
    sh = mod_ref[pl.ds(row, 1), 0:D_MODEL]
    sc = mod_ref[pl.ds(row, 1), D_MODEL:2 * D_MODEL]
    h = y * (1.0 + sc) + sh
    p_ref[0] = _dot(h.astype(BF16), w_ref[...])
    gate_ref[0] = _dot3(h, wg_ref[...])


def _inproj(x, mod, g, w, wg, tm, ctx_row):
    b, n, _ = x.shape
    wn = w.shape[1]
    return pl.pallas_call(
        functools.partial(_inproj_kernel, ctx_row=ctx_row),
        grid=(b, n // tm),
        in_specs=[pl.BlockSpec((1, tm, D_MODEL), lambda i, j: (i, j, 0)),
                  pl.BlockSpec(mod.shape, lambda i, j: (0, 0)),
                  pl.BlockSpec((1, D_MODEL), lambda i, j: (0, 0)),
                  pl.BlockSpec((D_MODEL, wn), lambda i, j: (0, 0)),
                  pl.BlockSpec((D_MODEL, GATE_PAD), lambda i, j: (0, 0))],
        out_specs=[pl.BlockSpec((1, tm, wn), lambda i, j: (i, j, 0)),
                   pl.BlockSpec((1, tm, GATE_PAD), lambda i, j: (i, j, 0))],
        out_shape=[jax.ShapeDtypeStruct((b, n, wn), F32),
                   jax.ShapeDtypeStruct((b, n, GATE_PAD), F32)],
        compiler_params=_params("arbitrary", "arbitrary"),
        name="inproj",
    )(x, mod, g, w, wg)


def _conv_silu(cur_ref, prev_ref, next_ref, w_ref, chunk, n_chunks, scale):
    cur = cur_ref[0]
    prev_row = prev_ref[0, 7:8, :] * (chunk > 0).astype(F32)
    next_row = next_ref[0, 0:1, :] * (chunk < n_chunks - 1).astype(F32)
    t = _iota2((CHUNK, 1), 0)
    before = jnp.where(t == 0, prev_row, pltpu.roll(cur, 1, axis=0))
    after = jnp.where(t == CHUNK - 1, next_row, pltpu.roll(cur, CHUNK - 1, axis=0))
    y = before * w_ref[0:1, :] + cur * w_ref[1:2, :] + after * w_ref[2:3, :]
    return jax.nn.silu(y) * scale


def _mlstm_direction(d, chunk, n_chunks, q_refs, k_refs, v_ref, g_ref, convq_ref, convk_ref,
                     bias_ref, c_s, n_s, m_s, h_ref):
    with_h = h_ref is not None
    t_i = _iota2((CHUNK, CHUNK), 0)
    s_i = _iota2((CHUNK, CHUNK), 1)
    visible = (s_i <= t_i) if d == 0 else (s_i >= t_i)
    tri = jnp.where(visible, 1.0, 0.0).astype(BF16)

    k_all = _conv_silu(*k_refs, convk_ref, chunk, n_chunks, HEAD_DIM ** -0.5)
    v_all = v_ref[0]
    q_all = _conv_silu(*q_refs, convq_ref, chunk, n_chunks, 1.0) if with_h else None

    gates = g_ref[0] + bias_ref[...]
    log_f = jax.nn.log_sigmoid(gates)
    bcum = _dot_exact01(tri, log_f)
    gates_t = gates.T
    bcum_t = bcum.T
    last = CHUNK - 1 if d == 0 else 0

    for hd in range(HEADS):
        ci = d * HEADS + hd
        cf = 2 * HEADS + d * HEADS + hd
        lanes = slice(hd * HEAD_DIM, (hd + 1) * HEAD_DIM)
        li_col, li_row = gates[:, ci:ci + 1], gates_t[ci:ci + 1, :]
        bc_col, bc_row = bcum[:, cf:cf + 1], bcum_t[cf:cf + 1, :]
        b_last = bc_row[:, last:last + 1]
        c_prev = c_s[d, hd]
        n_prev = n_s[d, hd]
        m_prev = m_s[d, hd][:, 0:1]
        k = k_all[:, lanes]
        v = v_all[:, lanes]

        if with_h:
            q = q_all[:, lanes]
            qb = q.astype(BF16)
            g = bc_col + m_prev
            dmat = jnp.where(visible, bc_col - bc_row + li_row, -jnp.inf)
            m_t = jnp.maximum(g, jnp.max(dmat, axis=-1, keepdims=True))
            inter = jnp.exp(g - m_t)
            s = _dot_nt(qb, k.astype(BF16)) * jnp.exp(dmat - m_t)
            num = inter * _dot_nt(qb, c_prev.astype(BF16)) + _dot(s.astype(BF16), v.astype(BF16))
            den = (inter * jnp.sum(q * n_prev, axis=-1, keepdims=True)
                   + jnp.sum(s, axis=-1, keepdims=True))
            h_ref[0, :, lanes] = num * (1.0 / jnp.maximum(jnp.abs(den), jnp.exp(-m_t)))

        a_col = b_last - bc_col + li_col
        a_row = b_last - bc_row + li_row
        m_new = jnp.maximum(b_last + m_prev, jnp.max(a_row, axis=-1, keepdims=True))
        decay = jnp.exp(b_last + m_prev - m_new)
        wk = jnp.exp(a_col - m_new) * k
        c_s[d, hd] = decay * c_prev + _dot(v.T.astype(BF16), wk.astype(BF16))
        n_s[d, hd] = decay * n_prev + jnp.sum(wk, axis=0, keepdims=True)
        m_s[d, hd] = jnp.broadcast_to(m_new, (1, HEAD_DIM))


def _mlstm_kernel(*refs, n_chunks, with_h):
    refs = list(refs)
    take = lambda n: [refs.pop(0) for _ in range(n)]
    per_dir = []
    for _ in range(2):
        q_refs = take(3) if with_h else None
        k_refs = take(3)
        v_ref, g_ref = take(2)
        per_dir.append((q_refs, k_refs, v_ref, g_ref))
    convq_ref, convk_ref, bias_ref = take(3)
    if with_h:
        c0_ref, n0_ref, m0_ref = take(3)
        h_refs = take(2)
        c_out = n_out = m_out = None
    else:
        h_refs = [None, None]
        c_out, n_out, m_out = take(3)
    c_s, n_s, m_s = take(3)

    j = pl.program_id(1)

    @pl.when(j == 0)
    def _():
        if with_h:
            c_s[...] = c0_ref[0]
            n_s[...] = n0_ref[0]
            m_s[...] = m0_ref[0]
        else:
            c_s[...] = jnp.zeros_like(c_s)
            n_s[...] = jnp.zeros_like(n_s)
            m_s[...] = jnp.zeros_like(m_s)

    for d in range(2):
        chunk = j if d == 0 else n_chunks - 1 - j
        q_refs, k_refs, v_ref, g_ref = per_dir[d]
        _mlstm_direction(d, chunk, n_chunks, q_refs, k_refs, v_ref, g_ref, convq_ref, convk_ref,
                         bias_ref, c_s, n_s, m_s, h_refs[d])

    if not with_h:
        @pl.when(j == n_chunks - 1)
        def _():
            c_out[0] = c_s[...]
            n_out[0] = n_s[...]
            m_out[0] = m_s[...]


def _mlstm(p, gates, conv_q, conv_k, bias, blocks, state):
    b, n, _ = p.shape
    n_chunks = n // CHUNK
    q_blk, k_blk, v_blk = blocks
    with_h = q_blk is not None
    rows8 = CHUNK // 8
    last8 = n // 8 - 1

    def chunk_of(d):
        return (lambda j: j) if d == 0 else (lambda j: n_chunks - 1 - j)

    def conv_specs(d, blk):
        c = chunk_of(d)
        return [pl.BlockSpec((1, CHUNK, GROUP_W), lambda i, j: (i, c(j), blk)),
                pl.BlockSpec((1, 8, GROUP_W), lambda i, j: (i, jnp.maximum(c(j) * rows8 - 1, 0), blk)),
                pl.BlockSpec((1, 8, GROUP_W),
                             lambda i, j: (i, jnp.minimum((c(j) + 1) * rows8, last8), blk))]

    in_specs, args = [], []
    for d in range(2):
        c = chunk_of(d)
        if with_h:
            in_specs += conv_specs(d, q_blk)
            args += [p, p, p]
        in_specs += conv_specs(d, k_blk)
        args += [p, p, p]
        in_specs += [pl.BlockSpec((1, CHUNK, GROUP_W), lambda i, j, c=c: (i, c(j), v_blk)),
                     pl.BlockSpec((1, CHUNK, GATE_PAD), lambda i, j, c=c: (i, c(j), 0))]
        args += [p, gates]
    in_specs += [pl.BlockSpec((3, GROUP_W), lambda i, j: (0, 0)),
                 pl.BlockSpec((3, GROUP_W), lambda i, j: (0, 0)),
                 pl.BlockSpec((1, GATE_PAD), lambda i, j: (0, 0))]
    args += [conv_q, conv_k, bias]

    c_shape = (2, HEADS, HEAD_DIM, HEAD_DIM)
    v_shape = (2, HEADS, 1, HEAD_DIM)
    c_spec = pl.BlockSpec((1,) + c_shape, lambda i, j: (i, 0, 0, 0, 0))
    v_spec = pl.BlockSpec((1,) + v_shape, lambda i, j: (i, 0, 0, 0, 0))
    if with_h:
        in_specs += [c_spec, v_spec, v_spec]
        args += list(state)
        out_specs = [pl.BlockSpec((1, CHUNK, GROUP_W), lambda i, j: (i, j, 0)),
                     pl.BlockSpec((1, CHUNK, GROUP_W), lambda i, j: (i, n_chunks - 1 - j, 0))]
        out_shape = [jax.ShapeDtypeStruct((b, n, GROUP_W), F32)] * 2
    else:
        out_specs = [c_spec, v_spec, v_spec]
        out_shape = [jax.ShapeDtypeStruct((b,) + c_shape, F32),
                     jax.ShapeDtypeStruct((b,) + v_shape, F32),
                     jax.ShapeDtypeStruct((b,) + v_shape, F32)]
    return pl.pallas_call(
        functools.partial(_mlstm_kernel, n_chunks=n_chunks, with_h=with_h),
        grid=(b, n_chunks),
        in_specs=in_specs,
        out_specs=out_specs,
        out_shape=out_shape,
        scratch_shapes=[pltpu.VMEM(c_shape, F32), pltpu.VMEM(v_shape, F32), pltpu.VMEM(v_shape, F32)],
        compiler_params=_params("arbitrary", "arbitrary"),
        name="mlstm" if with_h else "mlstm_ctx_state",
    )(*args)


def _layer_norm(x):
    mu = jnp.mean(x, axis=-1, keepdims=True)
    var = jnp.mean(jnp.square(x - mu), axis=-1, keepdims=True)
    return (x - mu) * lax.rsqrt(var + EPS)


def _postmix_kernel(u_ref, vg_ref, o_ref, hf_ref, hb_ref, x_ref, mod_ref, ws_ref, bs_ref, ng_ref,
                    wout_ref, fg_ref, wr_ref, x1_ref, h2_ref, aff_ref, ycat_s):
    tm = x_ref.shape[1]
    b = pl.program_id(0)
    mod = lambda k: mod_ref[pl.ds(b, 1), k * D_MODEL:(k + 1) * D_MODEL]

    u = jax.nn.gelu(u_ref[0])
    v = _layer_norm(jax.nn.gelu(vg_ref[0])).astype(BF16)
    for c in range(tm // CHUNK):
        rows = slice(c * CHUNK, (c + 1) * CHUNK)
        for hd in range(HEADS):
            lanes = slice(hd * HEAD_DIM, (hd + 1) * HEAD_DIM)
            s = _dot(ws_ref[hd], v[rows, lanes]) + bs_ref[hd]
            ycat_s[rows, lanes] = (u[rows, lanes] * s).astype(BF16)

    hsum = hf_ref[0] + hb_ref[0]
    o = jax.nn.sigmoid(o_ref[0])
    for hd in range(HEADS):
        lanes = slice(hd * HEAD_DIM, (hd + 1) * HEAD_DIM)
        hn = _layer_norm(hsum[:, lanes]) * ng_ref[:, lanes]
        ycat_s[:, GROUP_W + hd * HEAD_DIM:GROUP_W + (hd + 1) * HEAD_DIM] = (o[:, lanes] * hn).astype(BF16)

    y = _dot(ycat_s[...], wout_ref[...])
    x1 = x_ref[0] + mod(2) * y
    x1_ref[0] = x1

    n2 = x1 * lax.rsqrt(jnp.mean(x1 * x1, axis=-1, keepdims=True) + EPS) * fg_ref[...]
    h2 = n2 * (1.0 + mod(4)) + mod(3)
    h2_ref[0] = h2.astype(BF16)

    logits = _dot3(h2, wr_ref[...])
    logits = jnp.where(_iota2(logits.shape, 1) < N_EXPERTS, logits, -jnp.inf)
    e = jnp.exp(logits - jnp.max(logits, axis=-1, keepdims=True))
    aff = e / jnp.sum(e, axis=-1, keepdims=True)
    aff_ref[0] = aff.T[0:N_EXPERTS, :]


def _postmix(p, hf, hb, x, mod, ws, bs, ng, wout, fg, wr, tm):
    b, n, _ = x.shape
    tok = lambda blk: pl.BlockSpec((1, tm, GROUP_W), lambda i, j: (i, j, blk))
    full = lambda a: pl.BlockSpec(a.shape, lambda i, j: (0,) * a.ndim)
    return pl.pallas_call(
        _postmix_kernel,
        grid=(b, n // tm),
        in_specs=[tok(U_BLK), tok(VG_BLK), tok(O_BLK), tok(0), tok(0),
                  pl.BlockSpec((1, tm, D_MODEL), lambda i, j: (i, j, 0)),
                  full(mod), full(ws), full(bs), full(ng), full(wout), full(fg), full(wr)],
        out_specs=[pl.BlockSpec((1, tm, D_MODEL), lambda i, j: (i, j, 0)),
                   pl.BlockSpec((1, tm, D_MODEL), lambda i, j: (i, j, 0)),
                   pl.BlockSpec((1, N_EXPERTS, tm), lambda i, j: (i, 0, j))],
        out_shape=[jax.ShapeDtypeStruct((b, n, D_MODEL), F32),
                   jax.ShapeDtypeStruct((b, n, D_MODEL), BF16),
                   jax.ShapeDtypeStruct((b, N_EXPERTS, n), F32)],
        scratch_shapes=[pltpu.VMEM((tm, D_MODEL), BF16)],
        compiler_params=_params("arbitrary", "arbitrary"),
        name="postmix",
    )(p, p, p, hf, hb, x, mod, ws, bs, ng, wout, fg, wr)


def _cumsum_lanes(x, upper):
    carry = jnp.zeros((x.shape[0], 1), F32)
    outs = []
    for j in range(x.shape[1] // 128):
        c = _dot(x[:, j * 128:(j + 1) * 128].astype(BF16), upper) + carry
        outs.append(c)
        carry = c[:, 127:128]
    return jnp.concatenate(outs, axis=1)


def _route_kernel(aff_ref, slot_ref, w_ref, *, capacity):
    aff = aff_ref[0]
    cap = float(capacity)
    thr_bits = jnp.zeros((aff.shape[0], 1), jnp.int32)
    for bit in range(30, -1, -1):
        cand = thr_bits | (1 << bit)
        cnt = jnp.sum(jnp.where(aff >= pltpu.bitcast(cand, F32), 1.0, 0.0), axis=-1, keepdims=True)
        thr_bits = jnp.where(cnt >= cap, cand, thr_bits)
    thr = pltpu.bitcast(thr_bits, F32)
    upper = jnp.where(_iota2((128, 128), 0) <= _iota2((128, 128), 1), 1.0, 0.0).astype(BF16)
    above = jnp.where(aff > thr, 1.0, 0.0)
    tied = jnp.where(aff == thr, 1.0, 0.0)
    need = cap - jnp.sum(above, axis=-1, keepdims=True)
    sel = above + tied * jnp.where(_cumsum_lanes(tied, upper) <= need, 1.0, 0.0)
    pos = _cumsum_lanes(sel, upper) - 1.0
    slot_ref[0] = jnp.where(sel > 0.0, pos, -1.0)
    w_ref[0] = jnp.where(sel > 0.0, aff, 0.0)


def _route(aff_t, capacity):
    b, e, n = aff_t.shape
    spec = pl.BlockSpec((1, e, n), lambda i: (i, 0, 0))
    return pl.pallas_call(
        functools.partial(_route_kernel, capacity=capacity),
        grid=(b,),
        in_specs=[spec],
        out_specs=[spec, spec],
        out_shape=[jax.ShapeDtypeStruct((b, e, n), F32)] * 2,
        compiler_params=_params("arbitrary"),
        name="route",
    )(aff_t)


def _gather_kernel(h2_ref, slot_ref, w_ref, xe_ref, gate_ref, *, tok_blk):
    cap = xe_ref.shape[2]
    n = h2_ref.shape[1]
    slot_id = _iota2((cap, 1), 0).astype(F32)
    xe = jnp.zeros((cap, D_MODEL), F32)
    gate = jnp.zeros((cap, 1), F32)
    for j in range(n // tok_blk):
        toks = slice(j * tok_blk, (j + 1) * tok_blk)
        hit = slot_ref[0, 0, :, toks] == slot_id
        xe = xe + _dot(jnp.where(hit, 1.0, 0.0).astype(BF16), h2_ref[0, toks, :])
        gate = gate + jnp.sum(jnp.where(hit, w_ref[0, 0, :, toks], 0.0), axis=1, keepdims=True)
    xe_ref[0, 0] = xe.astype(BF16)
    gate_ref[0, 0] = gate


def _gather(h2, slot, w, capacity, tok_blk):
    b, n, _ = h2.shape
    n_e = slot.shape[1]
    row = pl.BlockSpec((1, 1, 1, n), lambda i, e: (i, e, 0, 0))
    return pl.pallas_call(
        functools.partial(_gather_kernel, tok_blk=tok_blk),
        grid=(b, n_e),
        in_specs=[pl.BlockSpec((1, n, D_MODEL), lambda i, e: (i, 0, 0)), row, row],
        out_specs=[pl.BlockSpec((1, 1, capacity, D_MODEL), lambda i, e: (i, e, 0, 0)),
                   pl.BlockSpec((1, 1, capacity, 1), lambda i, e: (i, e, 0, 0))],
        out_shape=[jax.ShapeDtypeStruct((b, n_e, capacity, D_MODEL), BF16),
                   jax.ShapeDtypeStruct((b, n_e, capacity, 1), F32)],
        compiler_params=_params("arbitrary", "arbitrary"),
        name="gather",
    )(h2, slot, w)


def _ffn_kernel(xe_ref, gate_ref, wg_ref, wu_ref, wd_ref, ye_ref, wg_s, wu_s, wd_s, acc_s):
    ft = pl.program_id(1)
    b = pl.program_id(2)
    last = pl.num_programs(1) - 1

    @pl.when(b == 0)
    def _():
        wg_s[...] = wg_ref[0].astype(BF16)
        wu_s[...] = wu_ref[0].astype(BF16)
        wd_s[...] = wd_ref[0].astype(BF16)

    xe = xe_ref[0, 0]
    act = jax.nn.silu(_dot(xe, wg_s[...])) * _dot(xe, wu_s[...])
    part = _dot(act.astype(BF16), wd_s[...])

    @pl.when(ft == 0)
    def _():
        acc_s[b] = part

    @pl.when(jnp.logical_and(ft > 0, ft < last))
    def _():
        acc_s[b] += part

    @pl.when(ft == last)
    def _():
        ye_ref[0] = ((acc_s[b] + part) * gate_ref[0, 0]).astype(BF16)


def _ffn(xe, gate, wg, wu, wd, f_tile):
    b, n_e, cap, _ = xe.shape
    n_ft = D_EXPERT // f_tile
    assert n_ft >= 2
    out_idx = lambda e, f, i: (jnp.where(f == n_ft - 1, i, 0), e, 0)
    return pl.pallas_call(
        _ffn_kernel,
        grid=(n_e, n_ft, b),
        in_specs=[pl.BlockSpec((1, 1, cap, D_MODEL), lambda e, f, i: (i, e, 0, 0)),
                  pl.BlockSpec((1, 1, cap, 1), lambda e, f, i: (i, e, 0, 0)),
                  pl.BlockSpec((1, D_MODEL, f_tile), lambda e, f, i: (e, 0, f)),
                  pl.BlockSpec((1, D_MODEL, f_tile), lambda e, f, i: (e, 0, f)),
                  pl.BlockSpec((1, f_tile, D_MODEL), lambda e, f, i: (e, f, 0))],
        out_specs=pl.BlockSpec((1, cap, D_MODEL), out_idx),
        out_shape=jax.ShapeDtypeStruct((b, n_e * cap, D_MODEL), BF16),
        scratch_shapes=[pltpu.VMEM((D_MODEL, f_tile), BF16), pltpu.VMEM((D_MODEL, f_tile), BF16),
                        pltpu.VMEM((f_tile, D_MODEL), BF16), pltpu.VMEM((b, cap, D_MODEL), F32)],
        compiler_params=_params("arbitrary", "arbitrary", "arbitrary"),
        name="ffn",
    )(xe, gate, wg, wu, wd)


def _combine_kernel(slot_ref, ye_ref, x1_ref, mod_ref, fg_ref, o_ref, *, capacity):
    b = pl.program_id(0)
    n_e = slot_ref.shape[1]
    slot_lane = _iota2((1, capacity), 1).astype(F32)
    acc = jnp.zeros(o_ref.shape[1:], F32)
    for e in range(n_e):
        onehot = jnp.where(_row_to_col(slot_ref[0, e]) == slot_lane, 1.0, 0.0).astype(BF16)
        acc = acc + _dot(onehot, ye_ref[0, e * capacity:(e + 1) * capacity, :])
    g2 = mod_ref[pl.ds(b, 1), 5 * D_MODEL:6 * D_MODEL]
    x2 = x1_ref[0] + g2 * acc
    o_ref[0] = x2 * lax.rsqrt(jnp.mean(x2 * x2, axis=-1, keepdims=True) + EPS) * fg_ref[...]


def _combine(slot, ye, x1, mod, fg, capacity, tm):
    b, n, _ = x1.shape
    n_e = slot.shape[1]
    tok = pl.BlockSpec((1, tm, D_MODEL), lambda i, j: (i, j, 0))
    return pl.pallas_call(
        functools.partial(_combine_kernel, capacity=capacity),
        grid=(b, n // tm),
        in_specs=[pl.BlockSpec((1, n_e, 1, tm), lambda i, j: (i, 0, 0, j)),
                  pl.BlockSpec((1, n_e * capacity, D_MODEL), lambda i, j: (i, 0, 0)),
                  tok,
                  pl.BlockSpec(mod.shape, lambda i, j: (0, 0)),
                  pl.BlockSpec((1, D_MODEL), lambda i, j: (0, 0))],
        out_specs=tok,
        out_shape=jax.ShapeDtypeStruct((b, n, D_MODEL), F32),
        compiler_params=_params("arbitrary", "arbitrary"),
        name="combine",
    )(slot, ye, x1, mod, fg)


def kernel(x, c, ctx, c_ctx, w_mod, b_mod, norm_mix_g, w_in, conv_q, conv_k, b_igate, b_fgate,
           gmlp_ws, gmlp_bs, mlstm_norm_g, w_out, norm_ffn_g, w_router, w_gate_e, w_up_e,
           w_down_e, final_g):
    depth = w_mod.shape[0]
    assert depth == 1, "the context stream is only carried as mLSTM states (single layer)"
    batch, seq, _ = x.shape
    assert seq % GRID_W == 0 and seq % CHUNK == 0 and batch + 1 <= MOD_ROWS
    capacity = EC_FACTOR * seq // N_EXPERTS
    ctx_row = batch
    l = 0

    cond = jnp.concatenate([c, c_ctx[None], jnp.zeros((MOD_ROWS - batch - 1, D_MODEL), F32)], axis=0)
    mod = _adaln(cond, w_mod[l], b_mod[l][None])

    row = lambda a: a[None]
    w_main = w_in[l][:, :MAIN_W].astype(BF16)
    w_gates = jnp.pad(w_in[l][:, MAIN_W:], ((0, 0), (0, GATE_PAD - N_GATES)))
    gate_bias = jnp.pad(jnp.concatenate([b_igate[l].reshape(-1), b_fgate[l].reshape(-1)]),
                        (0, GATE_PAD - N_GATES))[None]

    p_c, gates_c = _inproj(ctx, mod, row(norm_mix_g[l]), w_main[:, K_BLK * GROUP_W:], w_gates,
                           tm=ctx.shape[1], ctx_row=ctx_row)
    state = _mlstm(p_c, gates_c, conv_q[l], conv_k[l], gate_bias, (None, 0, 1), None)

    p, gates = _inproj(x, mod, row(norm_mix_g[l]), w_main, w_gates, tm=512, ctx_row=None)
    h_f, h_b = _mlstm(p, gates, conv_q[l], conv_k[l], gate_bias, (Q_BLK, K_BLK, V_BLK), state)

    x1, h2, aff_t = _postmix(p, h_f, h_b, x, mod, gmlp_ws[l].astype(BF16), gmlp_bs[l][:, :, None],
                             row(mlstm_norm_g[l]), w_out[l].astype(BF16), row(norm_ffn_g[l]),
                             jnp.pad(w_router[l], ((0, 0), (0, 128 - N_EXPERTS))), tm=512)

    slot, gate_w = _route(aff_t, capacity)
    slot = slot[:, :, None, :]
    gate_w = gate_w[:, :, None, :]

    xe, gate = _gather(h2, slot, gate_w, capacity, tok_blk=1024)
    ye = _ffn(xe, gate, w_gate_e[l], w_up_e[l], w_down_e[l], f_tile=1024)
    return _combine(slot, ye, x1, mod, row(final_g), capacity, tm=512)
```

```python
import functools

import jax
import jax.numpy as jnp
from jax import lax
from jax.experimental import pallas as pl
from jax.experimental.pallas import tpu as pltpu

F32 = jnp.float32
BF16 = jnp.bfloat16

D_MODEL = 1024
GRID_W = 64
CHUNK = 128
HEADS = 4
GROUP_W = D_MODEL // 2
HEAD_DIM = GROUP_W // HEADS
N_EXPERTS = 16
EC_FACTOR = 2
D_EXPERT = 2 * D_MODEL
EPS = 1e-6
N_GATES = 4 * HEADS
GATE_PAD = 128
MOD_ROWS = 8

U_BLK, VG_BLK, Q_BLK, O_BLK, K_BLK, V_BLK = 0, 1, 2, 3, 4, 5
MAIN_W = 6 * GROUP_W

VMEM_LIMIT = 56 * 1024 * 1024


def _params(*sem):
    return pltpu.CompilerParams(dimension_semantics=sem, vmem_limit_bytes=VMEM_LIMIT)


def _dot(a, b):
    return jnp.dot(a, b, preferred_element_type=F32)


def _dot_nt(a, b):
    return lax.dot_general(a, b, (((1,), (1,)), ((), ())), preferred_element_type=F32)


def _split2(a):
    hi = a.astype(BF16)
    lo = (a - hi.astype(F32)).astype(BF16)
    return hi, lo


def _dot3(a, b):
    ah, al = _split2(a)
    bh, bl = _split2(b)
    return _dot(ah, bh) + (_dot(al, bh) + _dot(ah, bl))


def _dot_exact01(tri, x):
    x1 = x.astype(BF16)
    r1 = x - x1.astype(F32)
    x2 = r1.astype(BF16)
    x3 = (r1 - x2.astype(F32)).astype(BF16)
    return _dot(tri, x1) + (_dot(tri, x2) + _dot(tri, x3))


def _iota2(shape, dim):
    return lax.broadcasted_iota(jnp.int32, shape, dim)


def _row_to_col(row):
    n = row.shape[1] // 128
    eye = _iota2((128, 128), 0) == _iota2((128, 128), 1)
    cols = [jnp.sum(jnp.where(eye, row[:, j * 128:(j + 1) * 128], 0.0), axis=1, keepdims=True)
            for j in range(n)]
    return cols[0] if n == 1 else jnp.concatenate(cols, axis=0)


def _adaln_kernel(cond_ref, w_ref, b_ref, o_ref):
    o_ref[...] = _dot3(jax.nn.silu(cond_ref[...]), w_ref[...]) + b_ref[...]


def _adaln(cond, w, b):
    n_out = w.shape[1]
    tn = D_MODEL
    return pl.pallas_call(
        _adaln_kernel,
        grid=(n_out // tn,),
        in_specs=[pl.BlockSpec((MOD_ROWS, D_MODEL), lambda j: (0, 0)),
                  pl.BlockSpec((D_MODEL, tn), lambda j: (0, j)),
                  pl.BlockSpec((1, tn), lambda j: (0, j))],
        out_specs=pl.BlockSpec((MOD_ROWS, tn), lambda j: (0, j)),
        out_shape=jax.ShapeDtypeStruct((MOD_ROWS, n_out), F32),
        compiler_params=_params("arbitrary"),
        name="adaln",
    )(cond, w, b)


def _inproj_kernel(x_ref, mod_ref, g_ref, w_ref, wg_ref, p_ref, gate_ref, *, ctx_row):
    row = pl.program_id(0) if ctx_row is None else ctx_row
    x = x_ref[0]
    y = x * lax.rsqrt(jnp.mean(x * x, axis=-1, keepdims=True) + EPS) * g_ref[...]
    sh = mod_ref[pl.ds(row, 1), 0:D_MODEL]
    sc = mod_ref[pl.ds(row, 1), D_MODEL:2 * D_MODEL]
    h = y * (1.0 + sc) + sh
    p_ref[0] = _dot(h.astype(BF16), w_ref[...])
    gate_ref[0] = _dot3(h, wg_ref[...])


def _inproj(x, mod, g, w, wg, tm, ctx_row):
    b, n, _ = x.shape
    wn = w.shape[1]
    return pl.pallas_call(
        functools.partial(_inproj_kernel, ctx_row=ctx_row),
        grid=(b, n // tm),
        in_specs=[pl.BlockSpec((1, tm, D_MODEL), lambda i, j: (i, j, 0)),
                  pl.BlockSpec(mod.shape, lambda i, j: (0, 0)),
                  pl.BlockSpec((1, D_MODEL), lambda i, j: (0, 0)),
                  pl.BlockSpec((D_MODEL, wn), lambda i, j: (0, 0)),
                  pl.BlockSpec((D_MODEL, GATE_PAD), lambda i, j: (0, 0))],
        out_specs=[pl.BlockSpec((1, tm, wn), lambda i, j: (i, j, 0)),
                   pl.BlockSpec((1, tm, GATE_PAD), lambda i, j: (i, j, 0))],
        out_shape=[jax.ShapeDtypeStruct((b, n, wn), F32),
                   jax.ShapeDtypeStruct((b, n, GATE_PAD), F32)],
        compiler_params=_params("arbitrary", "arbitrary"),
        name="inproj",
    )(x, mod, g, w, wg)


def _dwconv_kernel(*refs, n_tiles, scales):
    n_in = len(scales)
    j = pl.program_id(1)
    for s, scale in enumerate(scales):
        cur_ref, prev_ref, next_ref, w_ref = refs[4 * s:4 * s + 4]
        o_ref = refs[4 * n_in + s]
        cur = cur_ref[0]
        rows = cur.shape[0]
        prev_row = prev_ref[0, 7:8, :] * (j > 0).astype(F32)
        next_row = next_ref[0, 0:1, :] * (j < n_tiles - 1).astype(F32)
        t = _iota2((rows, 1), 0)
        before = jnp.where(t == 0, prev_row, pltpu.roll(cur, 1, axis=0))
        after = jnp.where(t == rows - 1, next_row, pltpu.roll(cur, rows - 1, axis=0))
        y = before * w_ref[0:1, :] + cur * w_ref[1:2, :] + after * w_ref[2:3, :]
        o_ref[0] = (jax.nn.silu(y) * scale).astype(BF16)


def _dwconv(p, streams, tm):
    b, n, _ = p.shape
    n_tiles = n // tm
    rows8 = tm // 8
    last8 = n // 8 - 1
    in_specs, args = [], []
    for blk, w, _ in streams:
        in_specs += [pl.BlockSpec((1, tm, GROUP_W), lambda i, j, blk=blk: (i, j, blk)),
                     pl.BlockSpec((1, 8, GROUP_W),
                                  lambda i, j, blk=blk: (i, jnp.maximum(j * rows8 - 1, 0), blk)),
                     pl.BlockSpec((1, 8, GROUP_W),
                                  lambda i, j, blk=blk: (i, jnp.minimum((j + 1) * rows8, last8), blk)),
                     pl.BlockSpec((3, GROUP_W), lambda i, j: (0, 0))]
        args += [p, p, p, w]
    return pl.pallas_call(
        functools.partial(_dwconv_kernel, n_tiles=n_tiles, scales=tuple(s for _, _, s in streams)),
        grid=(b, n_tiles),
        in_specs=in_specs,
        out_specs=[pl.BlockSpec((1, tm, GROUP_W), lambda i, j: (i, j, 0))] * len(streams),
        out_shape=[jax.ShapeDtypeStruct((b, n, GROUP_W), BF16)] * len(streams),
        compiler_params=_params("arbitrary", "arbitrary"),
        name="dwconv",
    )(*args)


def _mlstm_direction(d, q_ref, k_ref, v_ref, g_ref, bias_ref, c_s, n_s, m_s, h_ref):
    with_h = h_ref is not None
    i0 = _iota2((CHUNK, CHUNK), 0)
    i1 = _iota2((CHUNK, CHUNK), 1)
    sees_ts = (i1 <= i0) if d == 0 else (i1 >= i0)
    sees_st = (i0 <= i1) if d == 0 else (i0 >= i1)
    tri = jnp.where(sees_ts, 1.0, 0.0).astype(BF16)

    gates = g_ref[0] + bias_ref[...]
    bcum = _dot_exact01(tri, jax.nn.log_sigmoid(gates))
    gates_t = gates.T
    bcum_t = bcum.T
    last = CHUNK - 1 if d == 0 else 0
    k_all = k_ref[0]
    v_all = v_ref[0]
    q_all = q_ref[0] if with_h else None

    for hd in range(HEADS):
        ci = d * HEADS + hd
        cf = 2 * HEADS + d * HEADS + hd
        lanes = slice(hd * HEAD_DIM, (hd + 1) * HEAD_DIM)
        li_row = gates_t[ci:ci + 1, :]
        bc_row = bcum_t[cf:cf + 1, :]
        b_last = bc_row[:, last:last + 1]
        c_prev = c_s[d, hd]
        n_prev = n_s[d, hd]
        m_prev = m_s[d, hd]
        k = k_all[:, lanes]
        v_t = v_all[:, lanes].T

        if with_h:
            q = q_all[:, lanes]
            lhs = jnp.concatenate([k, c_prev.astype(BF16),
                                   jnp.broadcast_to(n_prev, (16, HEAD_DIM)).astype(BF16)], axis=0)
            prod = _dot_nt(lhs, q)
            kq, cq, nq = prod[0:CHUNK], prod[CHUNK:2 * CHUNK], prod[2 * CHUNK:2 * CHUNK + 1]
            u_col = gates[:, ci:ci + 1] - bcum[:, cf:cf + 1]
            g = bc_row + m_prev
            dmat = jnp.where(sees_st, u_col + bc_row, -jnp.inf)
            m_t = jnp.maximum(g, jnp.max(dmat, axis=0, keepdims=True))
            inter = jnp.exp(g - m_t)
            s = kq * jnp.exp(dmat - m_t)
            num = inter * cq + _dot(v_t.astype(BF16), s.astype(BF16))
            den = inter * nq + jnp.sum(s, axis=0, keepdims=True)
            h_t = num * (1.0 / jnp.maximum(jnp.abs(den), jnp.exp(-m_t)))
            h_ref[0, :, lanes] = h_t.T

        a_row = b_last - bc_row + li_row
        m_new = jnp.maximum(b_last + m_prev, jnp.max(a_row, axis=-1, keepdims=True))
        decay = jnp.exp(b_last + m_prev - m_new)
        w_row = jnp.exp(a_row - m_new)
        lhs = jnp.concatenate([v_t * w_row, jnp.broadcast_to(w_row, (16, CHUNK))], axis=0)
        upd = _dot(lhs.astype(BF16), k)
        c_s[d, hd] = decay * c_prev + upd[0:HEAD_DIM]
        n_s[d, hd] = decay * n_prev + upd[HEAD_DIM:HEAD_DIM + 1]
        m_s[d, hd] = m_new


def _mlstm_kernel(*refs, n_chunks, with_h):
    refs = list(refs)
    take = lambda n: [refs.pop(0) for _ in range(n)]
    per_dir = []
    for _ in range(2):
        q_ref = take(1)[0] if with_h else None
        k_ref, v_ref, g_ref = take(3)
        per_dir.append((q_ref, k_ref, v_ref, g_ref))
    bias_ref = take(1)[0]
    if with_h:
        c0_ref, n0_ref, m0_ref = take(3)
        h_refs = take(2)
        c_out = n_out = m_out = None
    else:
        h_refs = [None, None]
        c_out, n_out, m_out = take(3)
    c_s, n_s, m_s = take(3)

    j = pl.program_id(1)

    @pl.when(j == 0)
    def _():
        if with_h:
            c_s[...] = c0_ref[0]
            n_s[...] = n0_ref[0]
            m_s[...] = m0_ref[0]
        else:
            c_s[...] = jnp.zeros_like(c_s)
            n_s[...] = jnp.zeros_like(n_s)
            m_s[...] = jnp.zeros_like(m_s)

    for d in range(2):
        q_ref, k_ref, v_ref, g_ref = per_dir[d]
        _mlstm_direction(d, q_ref, k_ref, v_ref, g_ref, bias_ref, c_s, n_s, m_s, h_refs[d])

    if not with_h:
        @pl.when(j == n_chunks - 1)
        def _():
            c_out[0] = c_s[...]
            n_out[0] = n_s[...]
            m_out[0] = m_s[...]


def _mlstm(q, k, p, v_blk, gates, bias, state):
    b, n, _ = k.shape
    n_chunks = n // CHUNK
    with_h = q is not None

    in_specs, args = [], []
    for d in range(2):
        c = (lambda j: j) if d == 0 else (lambda j: n_chunks - 1 - j)
        tok = pl.BlockSpec((1, CHUNK, GROUP_W), lambda i, j, c=c: (i, c(j), 0))
        if with_h:
            in_specs.append(tok)
            args.append(q)
        in_specs += [tok,
                     pl.BlockSpec((1, CHUNK, GROUP_W), lambda i, j, c=c: (i, c(j), v_blk)),
                     pl.BlockSpec((1, CHUNK, GATE_PAD), lambda i, j, c=c: (i, c(j), 0))]
        args += [k, p, gates]
    in_specs.append(pl.BlockSpec((1, GATE_PAD), lambda i, j: (0, 0)))
    args.append(bias)

    c_shape = (2, HEADS, HEAD_DIM, HEAD_DIM)
    v_shape = (2, HEADS, 1, HEAD_DIM)
    c_spec = pl.BlockSpec((1,) + c_shape, lambda i, j: (i, 0, 0, 0, 0))
    v_spec = pl.BlockSpec((1,) + v_shape, lambda i, j: (i, 0, 0, 0, 0))
    if with_h:
        in_specs += [c_spec, v_spec, v_spec]
        args += list(state)
        out_specs = [pl.BlockSpec((1, CHUNK, GROUP_W), lambda i, j: (i, j, 0)),
                     pl.BlockSpec((1, CHUNK, GROUP_W), lambda i, j: (i, n_chunks - 1 - j, 0))]
        out_shape = [jax.ShapeDtypeStruct((b, n, GROUP_W), F32)] * 2
    else:
        out_specs = [c_spec, v_spec, v_spec]
        out_shape = [jax.ShapeDtypeStruct((b,) + c_shape, F32),
                     jax.ShapeDtypeStruct((b,) + v_shape, F32),
                     jax.ShapeDtypeStruct((b,) + v_shape, F32)]
    return pl.pallas_call(
        functools.partial(_mlstm_kernel, n_chunks=n_chunks, with_h=with_h),
        grid=(b, n_chunks),
        in_specs=in_specs,
        out_specs=out_specs,
        out_shape=out_shape,
        scratch_shapes=[pltpu.VMEM(c_shape, F32), pltpu.VMEM(v_shape, F32), pltpu.VMEM(v_shape, F32)],
        compiler_params=_params("arbitrary", "arbitrary"),
        name="mlstm" if with_h else "mlstm_ctx_state",
    )(*args)


def _layer_norm(x):
    mu = jnp.mean(x, axis=-1, keepdims=True)
    var = jnp.mean(jnp.square(x - mu), axis=-1, keepdims=True)
    return (x - mu) * lax.rsqrt(var + EPS)


def _postmix_kernel(u_ref, vg_ref, o_ref, hf_ref, hb_ref, x_ref, mod_ref, ws_ref, bs_ref, ng_ref,
                    wout_ref, fg_ref, wr_ref, x1_ref, h2_ref, aff_ref, ycat_s):
    tm = x_ref.shape[1]
    b = pl.program_id(0)
    mod = lambda k: mod_ref[pl.ds(b, 1), k * D_MODEL:(k + 1) * D_MODEL]

    u = jax.nn.gelu(u_ref[0])
    v = _layer_norm(jax.nn.gelu(vg_ref[0])).astype(BF16)
    for c in range(tm // CHUNK):
        rows = slice(c * CHUNK, (c + 1) * CHUNK)
        for hd in range(HEADS):
            lanes = slice(hd * HEAD_DIM, (hd + 1) * HEAD_DIM)
            s = _dot(ws_ref[hd], v[rows, lanes]) + bs_ref[hd]
            ycat_s[rows, lanes] = (u[rows, lanes] * s).astype(BF16)

    hsum = hf_ref[0] + hb_ref[0]
    o = jax.nn.sigmoid(o_ref[0])
    for hd in range(HEADS):
        lanes = slice(hd * HEAD_DIM, (hd + 1) * HEAD_DIM)
        hn = _layer_norm(hsum[:, lanes]) * ng_ref[:, lanes]
        ycat_s[:, GROUP_W + hd * HEAD_DIM:GROUP_W + (hd + 1) * HEAD_DIM] = (o[:, lanes] * hn).astype(BF16)

    y = _dot(ycat_s[...], wout_ref[...])
    x1 = x_ref[0] + mod(2) * y
    x1_ref[0] = x1

    n2 = x1 * lax.rsqrt(jnp.mean(x1 * x1, axis=-1, keepdims=True) + EPS) * fg_ref[...]
    h2 = n2 * (1.0 + mod(4)) + mod(3)
    h2_ref[0] = h2.astype(BF16)

    logits = _dot3(h2, wr_ref[...])
    logits = jnp.where(_iota2(logits.shape, 1) < N_EXPERTS, logits, -jnp.inf)
    e = jnp.exp(logits - jnp.max(logits, axis=-1, keepdims=True))
    aff = e / jnp.sum(e, axis=-1, keepdims=True)
    aff_ref[0] = aff.T[0:N_EXPERTS, :]


def _postmix(p, hf, hb, x, mod, ws, bs, ng, wout, fg, wr, tm):
    b, n, _ = x.shape
    tok = lambda blk: pl.BlockSpec((1, tm, GROUP_W), lambda i, j: (i, j, blk))
    full = lambda a: pl.BlockSpec(a.shape, lambda i, j: (0,) * a.ndim)
    return pl.pallas_call(
        _postmix_kernel,
        grid=(b, n // tm),
        in_specs=[tok(U_BLK), tok(VG_BLK), tok(O_BLK), tok(0), tok(0),
                  pl.BlockSpec((1, tm, D_MODEL), lambda i, j: (i, j, 0)),
                  full(mod), full(ws), full(bs), full(ng), full(wout), full(fg), full(wr)],
        out_specs=[pl.BlockSpec((1, tm, D_MODEL), lambda i, j: (i, j, 0)),
                   pl.BlockSpec((1, tm, D_MODEL), lambda i, j: (i, j, 0)),
                   pl.BlockSpec((1, N_EXPERTS, tm), lambda i, j: (i, 0, j))],
        out_shape=[jax.ShapeDtypeStruct((b, n, D_MODEL), F32),
                   jax.ShapeDtypeStruct((b, n, D_MODEL), BF16),
                   jax.ShapeDtypeStruct((b, N_EXPERTS, n), F32)],
        scratch_shapes=[pltpu.VMEM((tm, D_MODEL), BF16)],
        compiler_params=_params("arbitrary", "arbitrary"),
        name="postmix",
    )(p, p, p, hf, hb, x, mod, ws, bs, ng, wout, fg, wr)


def _cumsum_lanes(x, upper):
    carry = jnp.zeros((x.shape[0], 1), F32)
    outs = []
    for j in range(x.shape[1] // 128):
        c = _dot(x[:, j * 128:(j + 1) * 128].astype(BF16), upper) + carry
        outs.append(c)
        carry = c[:, 127:128]
    return jnp.concatenate(outs, axis=1)


def _route_kernel(aff_ref, slot_ref, w_ref, *, capacity):
    aff = aff_ref[0]
    cap = float(capacity)
    thr_bits = jnp.zeros((aff.shape[0], 1), jnp.int32)
    for bit in range(30, -1, -1):
        cand = thr_bits | (1 << bit)
        cnt = jnp.sum(jnp.where(aff >= pltpu.bitcast(cand, F32), 1.0, 0.0), axis=-1, keepdims=True)
        thr_bits = jnp.where(cnt >= cap, cand, thr_bits)
    thr = pltpu.bitcast(thr_bits, F32)
    upper = jnp.where(_iota2((128, 128), 0) <= _iota2((128, 128), 1), 1.0, 0.0).astype(BF16)
    above = jnp.where(aff > thr, 1.0, 0.0)
    tied = jnp.where(aff == thr, 1.0, 0.0)
    need = cap - jnp.sum(above, axis=-1, keepdims=True)
    sel = above + tied * jnp.where(_cumsum_lanes(tied, upper) <= need, 1.0, 0.0)
    pos = _cumsum_lanes(sel, upper) - 1.0
    slot_ref[0] = jnp.where(sel > 0.0, pos, -1.0)
    w_ref[0] = jnp.where(sel > 0.0, aff, 0.0)


def _route(aff_t, capacity):
    b, e, n = aff_t.shape
    spec = pl.BlockSpec((1, e, n), lambda i: (i, 0, 0))
    return pl.pallas_call(
        functools.partial(_route_kernel, capacity=capacity),
        grid=(b,),
        in_specs=[spec],
        out_specs=[spec, spec],
        out_shape=[jax.ShapeDtypeStruct((b, e, n), F32)] * 2,
        compiler_params=_params("arbitrary"),
        name="route",
    )(aff_t)


def _gather_kernel(h2_ref, slot_ref, w_ref, xe_ref, gate_ref, *, tok_blk):
    cap = xe_ref.shape[2]
    n = h2_ref.shape[1]
    slot_id = _iota2((cap, 1), 0).astype(F32)
    xe = jnp.zeros((cap, D_MODEL), F32)
    gate = jnp.zeros((cap, 1), F32)
    for j in range(n // tok_blk):
        toks = slice(j * tok_blk, (j + 1) * tok_blk)
        hit = slot_ref[0, 0, :, toks] == slot_id
        xe = xe + _dot(jnp.where(hit, 1.0, 0.0).astype(BF16), h2_ref[0, toks, :])
        gate = gate + jnp.sum(jnp.where(hit, w_ref[0, 0, :, toks], 0.0), axis=1, keepdims=True)
    xe_ref[0, 0] = xe.astype(BF16)
    gate_ref[0, 0] = gate


def _gather(h2, slot, w, capacity, tok_blk):
    b, n, _ = h2.shape
    n_e = slot.shape[1]
    row = pl.BlockSpec((1, 1, 1, n), lambda i, e: (i, e, 0, 0))
    return pl.pallas_call(
        functools.partial(_gather_kernel, tok_blk=tok_blk),
        grid=(b, n_e),
        in_specs=[pl.BlockSpec((1, n, D_MODEL), lambda i, e: (i, 0, 0)), row, row],
        out_specs=[pl.BlockSpec((1, 1, capacity, D_MODEL), lambda i, e: (i, e, 0, 0)),
                   pl.BlockSpec((1, 1, capacity, 1), lambda i, e: (i, e, 0, 0))],
        out_shape=[jax.ShapeDtypeStruct((b, n_e, capacity, D_MODEL), BF16),
                   jax.ShapeDtypeStruct((b, n_e, capacity, 1), F32)],
        compiler_params=_params("arbitrary", "arbitrary"),
        name="gather",
    )(h2, slot, w)


def _ffn_kernel(xe_ref, gate_ref, wg_ref, wu_ref, wd_ref, ye_ref, wg_s, wu_s, wd_s, acc_s):
    ft = pl.program_id(1)
    b = pl.program_id(2)
    last = pl.num_programs(1) - 1

    @pl.when(b == 0)
    def _():
        wg_s[...] = wg_ref[0].astype(BF16)
        wu_s[...] = wu_ref[0].astype(BF16)
        wd_s[...] = wd_ref[0].astype(BF16)

    xe = xe_ref[0, 0]
    act = jax.nn.silu(_dot(xe, wg_s[...])) * _dot(xe, wu_s[...])
    part = _dot(act.astype(BF16), wd_s[...])

    @pl.when(ft == 0)
    def _():
        acc_s[b] = part

    @pl.when(jnp.logical_and(ft > 0, ft < last))
    def _():
        acc_s[b] += part

    @pl.when(ft == last)
    def _():
        ye_ref[0] = ((acc_s[b] + part) * gate_ref[0, 0]).astype(BF16)


def _ffn(xe, gate, wg, wu, wd, f_tile):
    b, n_e, cap, _ = xe.shape
    n_ft = D_EXPERT // f_tile
    assert n_ft >= 2
    out_idx = lambda e, f, i: (jnp.where(f == n_ft - 1, i, 0), e, 0)
    return pl.pallas_call(
        _ffn_kernel,
        grid=(n_e, n_ft, b),
        in_specs=[pl.BlockSpec((1, 1, cap, D_MODEL), lambda e, f, i: (i, e, 0, 0)),
                  pl.BlockSpec((1, 1, cap, 1), lambda e, f, i: (i, e, 0, 0)),
                  pl.BlockSpec((1, D_MODEL, f_tile), lambda e, f, i: (e, 0, f)),
                  pl.BlockSpec((1, D_MODEL, f_tile), lambda e, f, i: (e, 0, f)),
                  pl.BlockSpec((1, f_tile, D_MODEL), lambda e, f, i: (e, f, 0))],
        out_specs=pl.BlockSpec((1, cap, D_MODEL), out_idx),
        out_shape=jax.ShapeDtypeStruct((b, n_e * cap, D_MODEL), BF16),
        scratch_shapes=[pltpu.VMEM((D_MODEL, f_tile), BF16), pltpu.VMEM((D_MODEL, f_tile), BF16),
                        pltpu.VMEM((f_tile, D_MODEL), BF16), pltpu.VMEM((b, cap, D_MODEL), F32)],
        compiler_params=_params("arbitrary", "arbitrary", "arbitrary"),
        name="ffn",
    )(xe, gate, wg, wu, wd)


def _combine_kernel(slot_ref, ye_ref, x1_ref, mod_ref, fg_ref, o_ref, *, capacity):
    b = pl.program_id(0)
    n_e = slot_ref.shape[1]
    slot_lane = _iota2((1, capacity), 1).astype(F32)
    acc = jnp.zeros(o_ref.shape[1:], F32)
    for e in range(n_e):
        onehot = jnp.where(_row_to_col(slot_ref[0, e]) == slot_lane, 1.0, 0.0).astype(BF16)
        acc = acc + _dot(onehot, ye_ref[0, e * capacity:(e + 1) * capacity, :])
    g2 = mod_ref[pl.ds(b, 1), 5 * D_MODEL:6 * D_MODEL]
    x2 = x1_ref[0] + g2 * acc
    o_ref[0] = x2 * lax.rsqrt(jnp.mean(x2 * x2, axis=-1, keepdims=True) + EPS) * fg_ref[...]


def _combine(slot, ye, x1, mod, fg, capacity, tm):
    b, n, _ = x1.shape
    n_e = slot.shape[1]
    tok = pl.BlockSpec((1, tm, D_MODEL), lambda i, j: (i, j, 0))
    return pl.pallas_call(
        functools.partial(_combine_kernel, capacity=capacity),
        grid=(b, n // tm),
        in_specs=[pl.BlockSpec((1, n_e, 1, tm), lambda i, j: (i, 0, 0, j)),
                  pl.BlockSpec((1, n_e * capacity, D_MODEL), lambda i, j: (i, 0, 0)),
                  tok,
                  pl.BlockSpec(mod.shape, lambda i, j: (0, 0)),
                  pl.BlockSpec((1, D_MODEL), lambda i, j: (0, 0))],
        out_specs=tok,
        out_shape=jax.ShapeDtypeStruct((b, n, D_MODEL), F32),
        compiler_params=_params("arbitrary", "arbitrary"),
        name="combine",
    )(slot, ye, x1, mod, fg)


def kernel(x, c, ctx, c_ctx, w_mod, b_mod, norm_mix_g, w_in, conv_q, conv_k, b_igate, b_fgate,
           gmlp_ws, gmlp_bs, mlstm_norm_g, w_out, norm_ffn_g, w_router, w_gate_e, w_up_e,
           w_down_e, final_g):
    depth = w_mod.shape[0]
    assert depth == 1, "the context stream is only carried as mLSTM states (single layer)"
    batch, seq, _ = x.shape
    assert seq % GRID_W == 0 and seq % CHUNK == 0 and batch + 1 <= MOD_ROWS
    capacity = EC_FACTOR * seq // N_EXPERTS
    ctx_row = batch
    l = 0

    cond = jnp.concatenate([c, c_ctx[None], jnp.zeros((MOD_ROWS - batch - 1, D_MODEL), F32)], axis=0)
    mod = _adaln(cond, w_mod[l], b_mod[l][None])

    row = lambda a: a[None]
    w_main = w_in[l][:, :MAIN_W].astype(BF16)
    w_gates = jnp.pad(w_in[l][:, MAIN_W:], ((0, 0), (0, GATE_PAD - N_GATES)))
    gate_bias = jnp.pad(jnp.concatenate([b_igate[l].reshape(-1), b_fgate[l].reshape(-1)]),
                        (0, GATE_PAD - N_GATES))[None]

    p_c, gates_c = _inproj(ctx, mod, row(norm_mix_g[l]), w_main[:, K_BLK * GROUP_W:], w_gates,
                           tm=ctx.shape[1], ctx_row=ctx_row)
    k_scale = HEAD_DIM ** -0.5
    (k_c,) = _dwconv(p_c, [(0, conv_k[l], k_scale)], tm=ctx.shape[1])
    state = _mlstm(None, k_c, p_c, 1, gates_c, gate_bias, None)

    p, gates = _inproj(x, mod, row(norm_mix_g[l]), w_main, w_gates, tm=512, ctx_row=None)
    q_l, k_l = _dwconv(p, [(Q_BLK, conv_q[l], 1.0), (K_BLK, conv_k[l], k_scale)], tm=512)
    h_f, h_b = _mlstm(q_l, k_l, p, V_BLK, gates, gate_bias, state)

    x1, h2, aff_t = _postmix(p, h_f, h_b, x, mod, gmlp_ws[l].astype(BF16), gmlp_bs[l][:, :, None],
                             row(mlstm_norm_g[l]), w_out[l].astype(BF16), row(norm_ffn_g[l]),
                             jnp.pad(w_router[l], ((0, 0), (0, 128 - N_EXPERTS))), tm=512)

    slot, gate_w = _route(aff_t, capacity)
    slot = slot[:, :, None, :]
    gate_w = gate_w[:, :, None, :]

    xe, gate = _gather(h2, slot, gate_w, capacity, tok_blk=1024)
    ye = _ffn(xe, gate, w_gate_e[l], w_up_e[l], w_down_e[l], f_tile=1024)
    return _combine(slot, ye, x1, mod, row(final_g), capacity, tm=512)
```

```python
import functools

import jax
import jax.numpy as jnp
from jax import lax
from jax.experimental import pallas as pl
from jax.experimental.pallas import tpu as pltpu

F32 = jnp.float32
BF16 = jnp.bfloat16

D_MODEL = 1024
GRID_W = 64
CHUNK = 128
HEADS = 4
GROUP_W = D_MODEL // 2
HEAD_DIM = GROUP_W // HEADS
N_EXPERTS = 16
EC_FACTOR = 2
D_EXPERT = 2 * D_MODEL
EPS = 1e-6
N_GATES = 4 * HEADS
GATE_PAD = 128
MOD_ROWS = 8

U_BLK, VG_BLK, Q_BLK, O_BLK, K_BLK, V_BLK = 0, 1, 2, 3, 4, 5
MAIN_W = 6 * GROUP_W

VMEM_LIMIT = 56 * 1024 * 1024


def _params(*sem):
    return pltpu.CompilerParams(dimension_semantics=sem, vmem_limit_bytes=VMEM_LIMIT)


def _dot(a, b):
    return jnp.dot(a, b, preferred_element_type=F32)


def _dot_nt(a, b):
    return lax.dot_general(a, b, (((1,), (1,)), ((), ())), preferred_element_type=F32)


def _split2(a):
    hi = a.astype(BF16)
    lo = (a - hi.astype(F32)).astype(BF16)
    return hi, lo


def _dot3(a, b):
    ah, al = _split2(a)
    bh, bl = _split2(b)
    return _dot(ah, bh) + (_dot(al, bh) + _dot(ah, bl))


def _dot_exact01(tri, x):
    x1 = x.astype(BF16)
    r1 = x - x1.astype(F32)
    x2 = r1.astype(BF16)
    x3 = (r1 - x2.astype(F32)).astype(BF16)
    return _dot(tri, x1) + (_dot(tri, x2) + _dot(tri, x3))


def _iota2(shape, dim):
    return lax.broadcasted_iota(jnp.int32, shape, dim)


def _row_to_col(row):
    n = row.shape[1] // 128
    eye = _iota2((128, 128), 0) == _iota2((128, 128), 1)
    cols = [jnp.sum(jnp.where(eye, row[:, j * 128:(j + 1) * 128], 0.0), axis=1, keepdims=True)
            for j in range(n)]
    return cols[0] if n == 1 else jnp.concatenate(cols, axis=0)


def _adaln_kernel(cond_ref, w_ref, b_ref, o_ref):
    o_ref[...] = _dot3(jax.nn.silu(cond_ref[...]), w_ref[...]) + b_ref[...]


def _adaln(cond, w, b):
    n_out = w.shape[1]
    tn = D_MODEL
    return pl.pallas_call(
        _adaln_kernel,
        grid=(n_out // tn,),
        in_specs=[pl.BlockSpec((MOD_ROWS, D_MODEL), lambda j: (0, 0)),
                  pl.BlockSpec((D_MODEL, tn), lambda j: (0, j)),
                  pl.BlockSpec((1, tn), lambda j: (0, j))],
        out_specs=pl.BlockSpec((MOD_ROWS, tn), lambda j: (0, j)),
        out_shape=jax.ShapeDtypeStruct((MOD_ROWS, n_out), F32),
        compiler_params=_params("arbitrary"),
        name="adaln",
    )(cond, w, b)


def _inproj_kernel(x_ref, mod_ref, g_ref, w_ref, wg_ref, p_ref, gate_ref, *, ctx_row):
    row = pl.program_id(0) if ctx_row is None else ctx_row
    x = x_ref[0]
    y = x * lax.rsqrt(jnp.mean(x * x, axis=-1, keepdims=True) + EPS) * g_ref[...]
    sh = mod_ref[pl.ds(row, 1), 0:D_MODEL]
    sc = mod_ref[pl.ds(row, 1), D_MODEL:2 * D_MODEL]
    h = y * (1.0 + sc) + sh
    p_ref[0] = _dot(h.astype(BF16), w_ref[...])
    gate_ref[0] = _dot3(h, wg_ref[...])


def _inproj(x, mod, g, w, wg, tm, ctx_row):
    b, n, _ = x.shape
    wn = w.shape[1]
    return pl.pallas_call(
        functools.partial(_inproj_kernel, ctx_row=ctx_row),
        grid=(b, n // tm),
        in_specs=[pl.BlockSpec((1, tm, D_MODEL), lambda i, j: (i, j, 0)),
                  pl.BlockSpec(mod.shape, lambda i, j: (0, 0)),
                  pl.BlockSpec((1, D_MODEL), lambda i, j: (0, 0)),
                  pl.BlockSpec((D_MODEL, wn), lambda i, j: (0, 0)),
                  pl.BlockSpec((D_MODEL, GATE_PAD), lambda i, j: (0, 0))],
        out_specs=[pl.BlockSpec((1, tm, wn), lambda i, j: (i, j, 0)),
                   pl.BlockSpec((1, tm, GATE_PAD), lambda i, j: (i, j, 0))],
        out_shape=[jax.ShapeDtypeStruct((b, n, wn), F32),
                   jax.ShapeDtypeStruct((b, n, GATE_PAD), F32)],
        compiler_params=_params("arbitrary", "arbitrary"),
        name="inproj",
    )(x, mod, g, w, wg)


def _dwconv_kernel(*refs, n_tiles, scales):
    n_in = len(scales)
    j = pl.program_id(1)
    for s, scale in enumerate(scales):
        cur_ref, prev_ref, next_ref, w_ref = refs[4 * s:4 * s + 4]
        o_ref = refs[4 * n_in + s]
        cur = cur_ref[0]
        rows = cur.shape[0]
        prev_row = prev_ref[0, 7:8, :] * (j > 0).astype(F32)
        next_row = next_ref[0, 0:1, :] * (j < n_tiles - 1).astype(F32)
        t = _iota2((rows, 1), 0)
        before = jnp.where(t == 0, prev_row, pltpu.roll(cur, 1, axis=0))
        after = jnp.where(t == rows - 1, next_row, pltpu.roll(cur, rows - 1, axis=0))
        y = before * w_ref[0:1, :] + cur * w_ref[1:2, :] + after * w_ref[2:3, :]
        o_ref[0] = (jax.nn.silu(y) * scale).astype(BF16)


def _dwconv(p, streams, tm):
    b, n, _ = p.shape
    n_tiles = n // tm
    rows8 = tm // 8
    last8 = n // 8 - 1
    in_specs, args = [], []
    for blk, w, _ in streams:
        in_specs += [pl.BlockSpec((1, tm, GROUP_W), lambda i, j, blk=blk: (i, j, blk)),
                     pl.BlockSpec((1, 8, GROUP_W),
                                  lambda i, j, blk=blk: (i, jnp.maximum(j * rows8 - 1, 0), blk)),
                     pl.BlockSpec((1, 8, GROUP_W),
                                  lambda i, j, blk=blk: (i, jnp.minimum((j + 1) * rows8, last8), blk)),
                     pl.BlockSpec((3, GROUP_W), lambda i, j: (0, 0))]
        args += [p, p, p, w]
    return pl.pallas_call(
        functools.partial(_dwconv_kernel, n_tiles=n_tiles, scales=tuple(s for _, _, s in streams)),
        grid=(b, n_tiles),
        in_specs=in_specs,
        out_specs=[pl.BlockSpec((1, tm, GROUP_W), lambda i, j: (i, j, 0))] * len(streams),
        out_shape=[jax.ShapeDtypeStruct((b, n, GROUP_W), BF16)] * len(streams),
        compiler_params=_params("arbitrary", "arbitrary"),
        name="dwconv",
    )(*args)


def _mlstm_direction(d, q_ref, k_ref, v_ref, g_ref, bias_ref, c_s, n_s, m_s, h_ref):
    with_h = h_ref is not None
    i0 = _iota2((CHUNK, CHUNK), 0)
    i1 = _iota2((CHUNK, CHUNK), 1)
    sees_ts = (i1 <= i0) if d == 0 else (i1 >= i0)
    sees_st = (i0 <= i1) if d == 0 else (i0 >= i1)
    tri = jnp.where(sees_ts, 1.0, 0.0).astype(BF16)

    gates = g_ref[0] + bias_ref[...]
    bcum = _dot_exact01(tri, jax.nn.log_sigmoid(gates))
    gates_t = gates.T
    bcum_t = bcum.T
    last = CHUNK - 1 if d == 0 else 0
    k_all = k_ref[0]
    v_all = v_ref[0]
    q_all = q_ref[0] if with_h else None

    for hd in range(HEADS):
        ci = d * HEADS + hd
        cf = 2 * HEADS + d * HEADS + hd
        lanes = slice(hd * HEAD_DIM, (hd + 1) * HEAD_DIM)
        li_row = gates_t[ci:ci + 1, :]
        bc_row = bcum_t[cf:cf + 1, :]
        b_last = bc_row[:, last:last + 1]
        c_prev = c_s[d, hd]
        n_prev = n_s[d, hd]
        m_prev = m_s[d, hd]
        k = k_all[:, lanes]
        v_t = v_all[:, lanes].T

        if with_h:
            q = q_all[:, lanes]
            lhs = jnp.concatenate([k, c_prev.astype(BF16),
                                   jnp.broadcast_to(n_prev, (16, HEAD_DIM)).astype(BF16)], axis=0)
            prod = _dot_nt(lhs, q)
            kq, cq, nq = prod[0:CHUNK], prod[CHUNK:2 * CHUNK], prod[2 * CHUNK:2 * CHUNK + 1]
            u_col = gates[:, ci:ci + 1] - bcum[:, cf:cf + 1]
            g = bc_row + m_prev
            dmat = jnp.where(sees_st, u_col + bc_row, -jnp.inf)
            m_t = jnp.maximum(g, jnp.max(dmat, axis=0, keepdims=True))
            inter = jnp.exp(g - m_t)
            s = kq * jnp.exp(dmat - m_t)
            num = inter * cq + _dot(v_t.astype(BF16), s.astype(BF16))
            den = inter * nq + jnp.sum(s, axis=0, keepdims=True)
            h_t = num * (1.0 / jnp.maximum(jnp.abs(den), jnp.exp(-m_t)))
            h_ref[0, :, lanes] = h_t.T

        a_row = b_last - bc_row + li_row
        m_new = jnp.maximum(b_last + m_prev, jnp.max(a_row, axis=-1, keepdims=True))
        decay = jnp.exp(b_last + m_prev - m_new)
        w_row = jnp.exp(a_row - m_new)
        lhs = jnp.concatenate([v_t * w_row, jnp.broadcast_to(w_row, (16, CHUNK))], axis=0)
        upd = _dot(lhs.astype(BF16), k)
        c_s[d, hd] = decay * c_prev + upd[0:HEAD_DIM]
        n_s[d, hd] = decay * n_prev + upd[HEAD_DIM:HEAD_DIM + 1]
        m_s[d, hd] = m_new


def _mlstm_kernel(*refs, n_chunks, with_h):
    refs = list(refs)
    take = lambda n: [refs.pop(0) for _ in range(n)]
    per_dir = []
    for _ in range(2):
        q_ref = take(1)[0] if with_h else None
        k_ref, v_ref, g_ref = take(3)
        per_dir.append((q_ref, k_ref, v_ref, g_ref))
    bias_ref = take(1)[0]
    if with_h:
        c0_ref, n0_ref, m0_ref = take(3)
        h_refs = take(2)
        c_out = n_out = m_out = None
    else:
        h_refs = [None, None]
        c_out, n_out, m_out = take(3)
    c_s, n_s, m_s = take(3)

    j = pl.program_id(1)

    @pl.when(j == 0)
    def _():
        if with_h:
            c_s[...] = c0_ref[0]
            n_s[...] = n0_ref[0]
            m_s[...] = m0_ref[0]
        else:
            c_s[...] = jnp.zeros_like(c_s)
            n_s[...] = jnp.zeros_like(n_s)
            m_s[...] = jnp.zeros_like(m_s)

    for d in range(2):
        q_ref, k_ref, v_ref, g_ref = per_dir[d]
        _mlstm_direction(d, q_ref, k_ref, v_ref, g_ref, bias_ref, c_s, n_s, m_s, h_refs[d])

    if not with_h:
        @pl.when(j == n_chunks - 1)
        def _():
            c_out[0] = c_s[...]
            n_out[0] = n_s[...]
            m_out[0] = m_s[...]


def _mlstm(q, k, p, v_blk, gates, bias, state):
    b, n, _ = k.shape
    n_chunks = n // CHUNK
    with_h = q is not None

    in_specs, args = [], []
    for d in range(2):
        c = (lambda j: j) if d == 0 else (lambda j: n_chunks - 1 - j)
        tok = pl.BlockSpec((1, CHUNK, GROUP_W), lambda i, j, c=c: (i, c(j), 0))
        if with_h:
            in_specs.append(tok)
            args.append(q)
        in_specs += [tok,
                     pl.BlockSpec((1, CHUNK, GROUP_W), lambda i, j, c=c: (i, c(j), v_blk)),
                     pl.BlockSpec((1, CHUNK, GATE_PAD), lambda i, j, c=c: (i, c(j), 0))]
        args += [k, p, gates]
    in_specs.append(pl.BlockSpec((1, GATE_PAD), lambda i, j: (0, 0)))
    args.append(bias)

    c_shape = (2, HEADS, HEAD_DIM, HEAD_DIM)
    v_shape = (2, HEADS, 1, HEAD_DIM)
    c_spec = pl.BlockSpec((1,) + c_shape, lambda i, j: (i, 0, 0, 0, 0))
    v_spec = pl.BlockSpec((1,) + v_shape, lambda i, j: (i, 0, 0, 0, 0))
    if with_h:
        in_specs += [c_spec, v_spec, v_spec]
        args += list(state)
        out_specs = [pl.BlockSpec((1, CHUNK, GROUP_W), lambda i, j: (i, j, 0)),
                     pl.BlockSpec((1, CHUNK, GROUP_W), lambda i, j: (i, n_chunks - 1 - j, 0))]
        out_shape = [jax.ShapeDtypeStruct((b, n, GROUP_W), F32)] * 2
    else:
        out_specs = [c_spec, v_spec, v_spec]
        out_shape = [jax.ShapeDtypeStruct((b,) + c_shape, F32),
                     jax.ShapeDtypeStruct((b,) + v_shape, F32),
                     jax.ShapeDtypeStruct((b,) + v_shape, F32)]
    return pl.pallas_call(
        functools.partial(_mlstm_kernel, n_chunks=n_chunks, with_h=with_h),
        grid=(b, n_chunks),
        in_specs=in_specs,
        out_specs=out_specs,
        out_shape=out_shape,
        scratch_shapes=[pltpu.VMEM(c_shape, F32), pltpu.VMEM(v_shape, F32), pltpu.VMEM(v_shape, F32)],
        compiler_params=_params("arbitrary", "arbitrary"),
        name="mlstm" if with_h else "mlstm_ctx_state",
    )(*args)


def _layer_norm(x):
    mu = jnp.mean(x, axis=-1, keepdims=True)
    var = jnp.mean(jnp.square(x - mu), axis=-1, keepdims=True)
    return (x - mu) * lax.rsqrt(var + EPS)


def _postmix_kernel(u_ref, vg_ref, o_ref, hf_ref, hb_ref, x_ref, mod_ref, ws_ref, bs_ref, ng_ref,
                    wout_ref, fg_ref, wr_ref, x1_ref, h2_ref, aff_ref, ycat_s):
    tm = x_ref.shape[1]
    b = pl.program_id(0)
    mod = lambda k: mod_ref[pl.ds(b, 1), k * D_MODEL:(k + 1) * D_MODEL]

    u = jax.nn.gelu(u_ref[0])
    v = _layer_norm(jax.nn.gelu(vg_ref[0])).astype(BF16)
    for c in range(tm // CHUNK):
        rows = slice(c * CHUNK, (c + 1) * CHUNK)
        for hd in range(HEADS):
            lanes = slice(hd * HEAD_DIM, (hd + 1) * HEAD_DIM)
            s = _dot(ws_ref[hd], v[rows, lanes]) + bs_ref[hd]
            ycat_s[rows, lanes] = (u[rows, lanes] * s).astype(BF16)

    hsum = hf_ref[0] + hb_ref[0]
    o = jax.nn.sigmoid(o_ref[0])
    for hd in range(HEADS):
        lanes = slice(hd * HEAD_DIM, (hd + 1) * HEAD_DIM)
        hn = _layer_norm(hsum[:, lanes]) * ng_ref[:, lanes]
        ycat_s[:, GROUP_W + hd * HEAD_DIM:GROUP_W + (hd + 1) * HEAD_DIM] = (o[:, lanes] * hn).astype(BF16)

    y = _dot(ycat_s[...], wout_ref[...])
    x1 = x_ref[0] + mod(2) * y
    x1_ref[0] = x1

    n2 = x1 * lax.rsqrt(jnp.mean(x1 * x1, axis=-1, keepdims=True) + EPS) * fg_ref[...]
    h2 = n2 * (1.0 + mod(4)) + mod(3)
    h2_ref[0] = h2.astype(BF16)

    logits = _dot3(h2, wr_ref[...])
    logits = jnp.where(_iota2(logits.shape, 1) < N_EXPERTS, logits, -jnp.inf)
    e = jnp.exp(logits - jnp.max(logits, axis=-1, keepdims=True))
    aff = e / jnp.sum(e, axis=-1, keepdims=True)
    aff_ref[0] = aff.T[0:N_EXPERTS, :]


def _postmix(p, hf, hb, x, mod, ws, bs, ng, wout, fg, wr, tm):
    b, n, _ = x.shape
    tok = lambda blk: pl.BlockSpec((1, tm, GROUP_W), lambda i, j: (i, j, blk))
    full = lambda a: pl.BlockSpec(a.shape, lambda i, j: (0,) * a.ndim)
    return pl.pallas_call(
        _postmix_kernel,
        grid=(b, n // tm),
        in_specs=[tok(U_BLK), tok(VG_BLK), tok(O_BLK), tok(0), tok(0),
                  pl.BlockSpec((1, tm, D_MODEL), lambda i, j: (i, j, 0)),
                  full(mod), full(ws), full(bs), full(ng), full(wout), full(fg), full(wr)],
        out_specs=[pl.BlockSpec((1, tm, D_MODEL), lambda i, j: (i, j, 0)),
                   pl.BlockSpec((1, tm, D_MODEL), lambda i, j: (i, j, 0)),
                   pl.BlockSpec((1, N_EXPERTS, tm), lambda i, j: (i, 0, j))],
        out_shape=[jax.ShapeDtypeStruct((b, n, D_MODEL), F32),
                   jax.ShapeDtypeStruct((b, n, D_MODEL), BF16),
                   jax.ShapeDtypeStruct((b, N_EXPERTS, n), F32)],
        scratch_shapes=[pltpu.VMEM((tm, D_MODEL), BF16)],
        compiler_params=_params("arbitrary", "arbitrary"),
        name="postmix",
    )(p, p, p, hf, hb, x, mod, ws, bs, ng, wout, fg, wr)


def _cumsum_lanes(x, upper):
    carry = jnp.zeros((x.shape[0], 1), F32)
    outs = []
    for j in range(x.shape[1] // 128):
        c = _dot(x[:, j * 128:(j + 1) * 128].astype(BF16), upper) + carry
        outs.append(c)
        carry = c[:, 127:128]
    return jnp.concatenate(outs, axis=1)


def _route_kernel(aff_ref, slot_ref, w_ref, *, capacity):
    aff = aff_ref[0]
    cap = float(capacity)
    thr_bits = jnp.zeros((aff.shape[0], 1), jnp.int32)
    for bit in range(30, -1, -1):
        cand = thr_bits | (1 << bit)
        cnt = jnp.sum(jnp.where(aff >= pltpu.bitcast(cand, F32), 1.0, 0.0), axis=-1, keepdims=True)
        thr_bits = jnp.where(cnt >= cap, cand, thr_bits)
    thr = pltpu.bitcast(thr_bits, F32)
    upper = jnp.where(_iota2((128, 128), 0) <= _iota2((128, 128), 1), 1.0, 0.0).astype(BF16)
    above = jnp.where(aff > thr, 1.0, 0.0)
    tied = jnp.where(aff == thr, 1.0, 0.0)
    need = cap - jnp.sum(above, axis=-1, keepdims=True)
    sel = above + tied * jnp.where(_cumsum_lanes(tied, upper) <= need, 1.0, 0.0)
    pos = _cumsum_lanes(sel, upper) - 1.0
    slot_ref[0] = jnp.where(sel > 0.0, pos, -1.0)
    w_ref[0] = jnp.where(sel > 0.0, aff, 0.0)


def _route(aff_t, capacity):
    b, e, n = aff_t.shape
    spec = pl.BlockSpec((1, e, n), lambda i: (i, 0, 0))
    return pl.pallas_call(
        functools.partial(_route_kernel, capacity=capacity),
        grid=(b,),
        in_specs=[spec],
        out_specs=[spec, spec],
        out_shape=[jax.ShapeDtypeStruct((b, e, n), F32)] * 2,
        compiler_params=_params("arbitrary"),
        name="route",
    )(aff_t)


def _gather_kernel(h2_ref, slot_ref, w_ref, xe_ref, gate_ref, *, tok_blk):
    cap = xe_ref.shape[2]
    n = h2_ref.shape[1]
    slot_id = _iota2((cap, 1), 0).astype(F32)
    xe = jnp.zeros((cap, D_MODEL), F32)
    gate = jnp.zeros((cap, 1), F32)
    for j in range(n // tok_blk):
        toks = slice(j * tok_blk, (j + 1) * tok_blk)
        hit = slot_ref[0, 0, :, toks] == slot_id
        xe = xe + _dot(jnp.where(hit, 1.0, 0.0).astype(BF16), h2_ref[0, toks, :])
        gate = gate + jnp.sum(jnp.where(hit, w_ref[0, 0, :, toks], 0.0), axis=1, keepdims=True)
    xe_ref[0, 0] = xe.astype(BF16)
    gate_ref[0, 0] = gate


def _gather(h2, slot, w, capacity, tok_blk):
    b, n, _ = h2.shape
    n_e = slot.shape[1]
    row = pl.BlockSpec((1, 1, 1, n), lambda i, e: (i, e, 0, 0))
    return pl.pallas_call(
        functools.partial(_gather_kernel, tok_blk=tok_blk),
        grid=(b, n_e),
        in_specs=[pl.BlockSpec((1, n, D_MODEL), lambda i, e: (i, 0, 0)), row, row],
        out_specs=[pl.BlockSpec((1, 1, capacity, D_MODEL), lambda i, e: (i, e, 0, 0)),
                   pl.BlockSpec((1, 1, capacity, 1), lambda i, e: (i, e, 0, 0))],
        out_shape=[jax.ShapeDtypeStruct((b, n_e, capacity, D_MODEL), BF16),
                   jax.ShapeDtypeStruct((b, n_e, capacity, 1), F32)],
        compiler_params=_params("arbitrary", "arbitrary"),
        name="gather",
    )(h2, slot, w)


def _ffn_kernel(xe_ref, gate_ref, *refs, n_parts):
    wg_parts, wu_parts, wd_parts = (refs[k * n_parts:(k + 1) * n_parts] for k in range(3))
    ye_ref, wg_s, wu_s, wd_s, acc_s = refs[3 * n_parts:]
    ft = pl.program_id(1)
    b = pl.program_id(2)
    last = pl.num_programs(1) - 1

    @pl.when(b == 0)
    def _():
        for parts, dst in ((wg_parts, wg_s), (wu_parts, wu_s), (wd_parts, wd_s)):
            rows = dst.shape[0] // n_parts
            for r, part_ref in enumerate(parts):
                dst[r * rows:(r + 1) * rows, :] = part_ref[0].astype(BF16)

    xe = xe_ref[0, 0]
    act = jax.nn.silu(_dot(xe, wg_s[...])) * _dot(xe, wu_s[...])
    part = _dot(act.astype(BF16), wd_s[...])

    @pl.when(ft == 0)
    def _():
        acc_s[b] = part

    @pl.when(jnp.logical_and(ft > 0, ft < last))
    def _():
        acc_s[b] += part

    @pl.when(ft == last)
    def _():
        ye_ref[0] = ((acc_s[b] + part) * gate_ref[0, 0]).astype(BF16)


def _ffn(xe, gate, wg, wu, wd, f_tile):
    b, n_e, cap, _ = xe.shape
    n_ft = D_EXPERT // f_tile
    assert n_ft >= 2
    out_idx = lambda e, f, i: (jnp.where(f == n_ft - 1, i, 0), e, 0)

    n_parts = b
    assert D_MODEL % (8 * n_parts) == 0 and f_tile % (8 * n_parts) == 0

    def tile_of(e, f, i, r):
        t = jnp.minimum(e * n_ft + f + (i > r).astype(jnp.int32), n_e * n_ft - 1)
        return t // n_ft, t % n_ft

    def in_part(r):
        def idx(e, f, i):
            e2, f2 = tile_of(e, f, i, r)
            return e2, r, f2
        return pl.BlockSpec((1, D_MODEL // n_parts, f_tile), idx)

    def down_part(r):
        def idx(e, f, i):
            e2, f2 = tile_of(e, f, i, r)
            return e2, f2 * n_parts + r, 0
        return pl.BlockSpec((1, f_tile // n_parts, D_MODEL), idx)

    parts = range(n_parts)
    return pl.pallas_call(
        functools.partial(_ffn_kernel, n_parts=n_parts),
        grid=(n_e, n_ft, b),
        in_specs=([pl.BlockSpec((1, 1, cap, D_MODEL), lambda e, f, i: (i, e, 0, 0)),
                   pl.BlockSpec((1, 1, cap, 1), lambda e, f, i: (i, e, 0, 0))]
                  + [in_part(r) for r in parts] + [in_part(r) for r in parts]
                  + [down_part(r) for r in parts]),
        out_specs=pl.BlockSpec((1, cap, D_MODEL), out_idx),
        out_shape=jax.ShapeDtypeStruct((b, n_e * cap, D_MODEL), BF16),
        scratch_shapes=[pltpu.VMEM((D_MODEL, f_tile), BF16), pltpu.VMEM((D_MODEL, f_tile), BF16),
                        pltpu.VMEM((f_tile, D_MODEL), BF16), pltpu.VMEM((b, cap, D_MODEL), F32)],
        compiler_params=_params("arbitrary", "arbitrary", "arbitrary"),
        name="ffn",
    )(xe, gate, *([wg] * n_parts), *([wu] * n_parts), *([wd] * n_parts))


def _combine_kernel(slot_ref, ye_ref, x1_ref, mod_ref, fg_ref, o_ref, *, capacity):
    b = pl.program_id(0)
    n_e = slot_ref.shape[1]
    slot_lane = _iota2((1, capacity), 1).astype(F32)
    acc = jnp.zeros(o_ref.shape[1:], F32)
    for e in range(n_e):
        onehot = jnp.where(_row_to_col(slot_ref[0, e]) == slot_lane, 1.0, 0.0).astype(BF16)
        acc = acc + _dot(onehot, ye_ref[0, e * capacity:(e + 1) * capacity, :])
    g2 = mod_ref[pl.ds(b, 1), 5 * D_MODEL:6 * D_MODEL]
    x2 = x1_ref[0] + g2 * acc
    o_ref[0] = x2 * lax.rsqrt(jnp.mean(x2 * x2, axis=-1, keepdims=True) + EPS) * fg_ref[...]


def _combine(slot, ye, x1, mod, fg, capacity, tm):
    b, n, _ = x1.shape
    n_e = slot.shape[1]
    tok = pl.BlockSpec((1, tm, D_MODEL), lambda i, j: (i, j, 0))
    return pl.pallas_call(
        functools.partial(_combine_kernel, capacity=capacity),
        grid=(b, n // tm),
        in_specs=[pl.BlockSpec((1, n_e, 1, tm), lambda i, j: (i, 0, 0, j)),
                  pl.BlockSpec((1, n_e * capacity, D_MODEL), lambda i, j: (i, 0, 0)),
                  tok,
                  pl.BlockSpec(mod.shape, lambda i, j: (0, 0)),
                  pl.BlockSpec((1, D_MODEL), lambda i, j: (0, 0))],
        out_specs=tok,
        out_shape=jax.ShapeDtypeStruct((b, n, D_MODEL), F32),
        compiler_params=_params("arbitrary", "arbitrary"),
        name="combine",
    )(slot, ye, x1, mod, fg)


def kernel(x, c, ctx, c_ctx, w_mod, b_mod, norm_mix_g, w_in, conv_q, conv_k, b_igate, b_fgate,
           gmlp_ws, gmlp_bs, mlstm_norm_g, w_out, norm_ffn_g, w_router, w_gate_e, w_up_e,
           w_down_e, final_g):
    depth = w_mod.shape[0]
    assert depth == 1, "the context stream is only carried as mLSTM states (single layer)"
    batch, seq, _ = x.shape
    assert seq % GRID_W == 0 and seq % CHUNK == 0 and batch + 1 <= MOD_ROWS
    capacity = EC_FACTOR * seq // N_EXPERTS
    ctx_row = batch
    l = 0

    cond = jnp.concatenate([c, c_ctx[None], jnp.zeros((MOD_ROWS - batch - 1, D_MODEL), F32)], axis=0)
    mod = _adaln(cond, w_mod[l], b_mod[l][None])

    row = lambda a: a[None]
    w_main = w_in[l][:, :MAIN_W].astype(BF16)
    w_gates = jnp.pad(w_in[l][:, MAIN_W:], ((0, 0), (0, GATE_PAD - N_GATES)))
    gate_bias = jnp.pad(jnp.concatenate([b_igate[l].reshape(-1), b_fgate[l].reshape(-1)]),
                        (0, GATE_PAD - N_GATES))[None]

    p_c, gates_c = _inproj(ctx, mod, row(norm_mix_g[l]), w_main[:, K_BLK * GROUP_W:], w_gates,
                           tm=ctx.shape[1], ctx_row=ctx_row)
    k_scale = HEAD_DIM ** -0.5
    (k_c,) = _dwconv(p_c, [(0, conv_k[l], k_scale)], tm=ctx.shape[1])
    state = _mlstm(None, k_c, p_c, 1, gates_c, gate_bias, None)

    p, gates = _inproj(x, mod, row(norm_mix_g[l]), w_main, w_gates, tm=512, ctx_row=None)
    q_l, k_l = _dwconv(p, [(Q_BLK, conv_q[l], 1.0), (K_BLK, conv_k[l], k_scale)], tm=512)
    h_f, h_b = _mlstm(q_l, k_l, p, V_BLK, gates, gate_bias, state)

    x1, h2, aff_t = _postmix(p, h_f, h_b, x, mod, gmlp_ws[l].astype(BF16), gmlp_bs[l][:, :, None],
                             row(mlstm_norm_g[l]), w_out[l].astype(BF16), row(norm_ffn_g[l]),
                             jnp.pad(w_router[l], ((0, 0), (0, 128 - N_EXPERTS))), tm=512)

    slot, gate_w = _route(aff_t, capacity)
    slot = slot[:, :, None, :]
    gate_w = gate_w[:, :, None, :]

    xe, gate = _gather(h2, slot, gate_w, capacity, tok_blk=1024)
    ye = _ffn(xe, gate, w_gate_e[l], w_up_e[l], w_down_e[l], f_tile=1024)
    return _combine(slot, ye, x1, mod, row(final_g), capacity, tm=512)
```

```python
import functools

import jax
import jax.numpy as jnp
from jax import lax
from jax.experimental import pallas as pl
from jax.experimental.pallas import tpu as pltpu

F32 = jnp.float32
BF16 = jnp.bfloat16

D_MODEL = 1024
GRID_W = 64
CHUNK = 128
HEADS = 4
GROUP_W = D_MODEL // 2
HEAD_DIM = GROUP_W // HEADS
N_EXPERTS = 16
EC_FACTOR = 2
D_EXPERT = 2 * D_MODEL
EPS = 1e-6
N_GATES = 4 * HEADS
GATE_PAD = 128
MOD_ROWS = 8

U_BLK, VG_BLK, Q_BLK, O_BLK, K_BLK, V_BLK = 0, 1, 2, 3, 4, 5
MAIN_W = 6 * GROUP_W

VMEM_LIMIT = 56 * 1024 * 1024

def _params(*sem):
    return pltpu.CompilerParams(dimension_semantics=sem, vmem_limit_bytes=VMEM_LIMIT)


def _dot(a, b):
    return jnp.dot(a, b, preferred_element_type=F32)


def _dot_nt(a, b):
    return lax.dot_general(a, b, (((1,), (1,)), ((), ())), preferred_element_type=F32)


def _split2(a):
    hi = a.astype(BF16)
    lo = (a - hi.astype(F32)).astype(BF16)
    return hi, lo


def _dot3(a, b):
    ah, al = _split2(a)
    bh, bl = _split2(b)
    return _dot(ah, bh) + (_dot(al, bh) + _dot(ah, bl))


def _dot_exact01(tri, x):
    x1 = x.astype(BF16)
    r1 = x - x1.astype(F32)
    x2 = r1.astype(BF16)
    x3 = (r1 - x2.astype(F32)).astype(BF16)
    return _dot(tri, x1) + (_dot(tri, x2) + _dot(tri, x3))


def _iota2(shape, dim):
    return lax.broadcasted_iota(jnp.int32, shape, dim)


def _row_to_col(row):
    n = row.shape[1] // 128
    eye = _iota2((128, 128), 0) == _iota2((128, 128), 1)
    cols = [jnp.sum(jnp.where(eye, row[:, j * 128:(j + 1) * 128], 0.0), axis=1, keepdims=True)
            for j in range(n)]
    return cols[0] if n == 1 else jnp.concatenate(cols, axis=0)


def _adaln_kernel(cond_ref, w_ref, b_ref, o_ref):
    o_ref[...] = _dot3(jax.nn.silu(cond_ref[...]), w_ref[...]) + b_ref[...]


def _adaln(cond, w, b):
    n_out = w.shape[1]
    tn = D_MODEL
    return pl.pallas_call(
        _adaln_kernel,
        grid=(n_out // tn,),
        in_specs=[pl.BlockSpec((MOD_ROWS, D_MODEL), lambda j: (0, 0)),
                  pl.BlockSpec((D_MODEL, tn), lambda j: (0, j)),
                  pl.BlockSpec((1, tn), lambda j: (0, j))],
        out_specs=pl.BlockSpec((MOD_ROWS, tn), lambda j: (0, j)),
        out_shape=jax.ShapeDtypeStruct((MOD_ROWS, n_out), F32),
        compiler_params=_params("arbitrary"),
        name="adaln",
    )(cond, w, b)


def _inproj_kernel(x_ref, mod_ref, g_ref, w_ref, wg_ref, p_ref, gate_ref, *, ctx_row):
    row = pl.program_id(0) if ctx_row is None else ctx_row
    x = x_ref[0]
    y = x * lax.rsqrt(jnp.mean(x * x, axis=-1, keepdims=True) + EPS) * g_ref[...]
    sh = mod_ref[pl.ds(row, 1), 0:D_MODEL]
    sc = mod_ref[pl.ds(row, 1), D_MODEL:2 * D_MODEL]
    h = y * (1.0 + sc) + sh
    p_ref[0] = _dot(h.astype(BF16), w_ref[...])
    gate_ref[0] = _dot3(h, wg_ref[...])


def _inproj(x, mod, g, w, wg, tm, ctx_row):
    b, n, _ = x.shape
    wn = w.shape[1]
    return pl.pallas_call(
        functools.partial(_inproj_kernel, ctx_row=ctx_row),
        grid=(b, n // tm),
        in_specs=[pl.BlockSpec((1, tm, D_MODEL), lambda i, j: (i, j, 0)),
                  pl.BlockSpec(mod.shape, lambda i, j: (0, 0)),
                  pl.BlockSpec((1, D_MODEL), lambda i, j: (0, 0)),
                  pl.BlockSpec((D_MODEL, wn), lambda i, j: (0, 0)),
                  pl.BlockSpec((D_MODEL, GATE_PAD), lambda i, j: (0, 0))],
        out_specs=[pl.BlockSpec((1, tm, wn), lambda i, j: (i, j, 0)),
                   pl.BlockSpec((1, tm, GATE_PAD), lambda i, j: (i, j, 0))],
        out_shape=[jax.ShapeDtypeStruct((b, n, wn), F32),
                   jax.ShapeDtypeStruct((b, n, GATE_PAD), F32)],
        compiler_params=_params("arbitrary", "arbitrary"),
        name="inproj",
    )(x, mod, g, w, wg)


def _dwconv_kernel(*refs, n_tiles, scales):
    n_in = len(scales)
    j = pl.program_id(1)
    for s, scale in enumerate(scales):
        cur_ref, prev_ref, next_ref, w_ref = refs[4 * s:4 * s + 4]
        o_ref = refs[4 * n_in + s]
        cur = cur_ref[0]
        rows = cur.shape[0]
        prev_row = prev_ref[0, 7:8, :] * (j > 0).astype(F32)
        next_row = next_ref[0, 0:1, :] * (j < n_tiles - 1).astype(F32)
        t = _iota2((rows, 1), 0)
        before = jnp.where(t == 0, prev_row, pltpu.roll(cur, 1, axis=0))
        after = jnp.where(t == rows - 1, next_row, pltpu.roll(cur, rows - 1, axis=0))
        y = before * w_ref[0:1, :] + cur * w_ref[1:2, :] + after * w_ref[2:3, :]
        o_ref[0] = (jax.nn.silu(y) * scale).astype(BF16)


def _dwconv(p, streams, tm):
    b, n, _ = p.shape
    n_tiles = n // tm
    rows8 = tm // 8
    last8 = n // 8 - 1
    in_specs, args = [], []
    for blk, w, _ in streams:
        in_specs += [pl.BlockSpec((1, tm, GROUP_W), lambda i, j, blk=blk: (i, j, blk)),
                     pl.BlockSpec((1, 8, GROUP_W),
                                  lambda i, j, blk=blk: (i, jnp.maximum(j * rows8 - 1, 0), blk)),
                     pl.BlockSpec((1, 8, GROUP_W),
                                  lambda i, j, blk=blk: (i, jnp.minimum((j + 1) * rows8, last8), blk)),
                     pl.BlockSpec((3, GROUP_W), lambda i, j: (0, 0))]
        args += [p, p, p, w]
    return pl.pallas_call(
        functools.partial(_dwconv_kernel, n_tiles=n_tiles, scales=tuple(s for _, _, s in streams)),
        grid=(b, n_tiles),
        in_specs=in_specs,
        out_specs=[pl.BlockSpec((1, tm, GROUP_W), lambda i, j: (i, j, 0))] * len(streams),
        out_shape=[jax.ShapeDtypeStruct((b, n, GROUP_W), BF16)] * len(streams),
        compiler_params=_params("arbitrary", "arbitrary"),
        name="dwconv",
    )(*args)


def _mlstm_direction(d, q_ref, k_ref, v_ref, g_ref, bias_ref, c_s, n_s, m_s, h_ref):
    with_h = h_ref is not None
    i0 = _iota2((CHUNK, CHUNK), 0)
    i1 = _iota2((CHUNK, CHUNK), 1)
    sees_ts = (i1 <= i0) if d == 0 else (i1 >= i0)
    sees_st = (i0 <= i1) if d == 0 else (i0 >= i1)
    tri = jnp.where(sees_ts, 1.0, 0.0).astype(BF16)

    gates = g_ref[0] + bias_ref[...]
    bcum = _dot_exact01(tri, jax.nn.log_sigmoid(gates))
    gates_t = gates.T
    bcum_t = bcum.T
    last = CHUNK - 1 if d == 0 else 0
    k_all = k_ref[0]
    v_all = v_ref[0]
    q_all = q_ref[0] if with_h else None

    for hd in range(HEADS):
        ci = d * HEADS + hd
        cf = 2 * HEADS + d * HEADS + hd
        lanes = slice(hd * HEAD_DIM, (hd + 1) * HEAD_DIM)
        li_row = gates_t[ci:ci + 1, :]
        bc_row = bcum_t[cf:cf + 1, :]
        b_last = bc_row[:, last:last + 1]
        c_prev = c_s[d, hd]
        n_prev = n_s[d, hd]
        m_prev = m_s[d, hd]
        k = k_all[:, lanes]
        v_t = v_all[:, lanes].T

        if with_h:
            q = q_all[:, lanes]
            lhs = jnp.concatenate([k, c_prev.astype(BF16),
                                   jnp.broadcast_to(n_prev, (16, HEAD_DIM)).astype(BF16)], axis=0)
            prod = _dot_nt(lhs, q)
            kq, cq, nq = prod[0:CHUNK], prod[CHUNK:2 * CHUNK], prod[2 * CHUNK:2 * CHUNK + 1]
            u_col = gates[:, ci:ci + 1] - bcum[:, cf:cf + 1]
            g = bc_row + m_prev
            dmat = jnp.where(sees_st, u_col + bc_row, -jnp.inf)
            m_t = jnp.maximum(g, jnp.max(dmat, axis=0, keepdims=True))
            inter = jnp.exp(g - m_t)
            s = kq * jnp.exp(dmat - m_t)
            num = inter * cq + _dot(v_t.astype(BF16), s.astype(BF16))
            den = inter * nq + jnp.sum(s, axis=0, keepdims=True)
            h_t = num * (1.0 / jnp.maximum(jnp.abs(den), jnp.exp(-m_t)))
            h_ref[0, :, lanes] = h_t.T

        a_row = b_last - bc_row + li_row
        m_new = jnp.maximum(b_last + m_prev, jnp.max(a_row, axis=-1, keepdims=True))
        decay = jnp.exp(b_last + m_prev - m_new)
        w_row = jnp.exp(a_row - m_new)
        lhs = jnp.concatenate([v_t * w_row, jnp.broadcast_to(w_row, (16, CHUNK))], axis=0)
        upd = _dot(lhs.astype(BF16), k)
        c_s[d, hd] = decay * c_prev + upd[0:HEAD_DIM]
        n_s[d, hd] = decay * n_prev + upd[HEAD_DIM:HEAD_DIM + 1]
        m_s[d, hd] = m_new


def _mlstm_kernel(*refs, n_chunks, with_h):
    refs = list(refs)
    take = lambda n: [refs.pop(0) for _ in range(n)]
    per_dir = []
    for _ in range(2):
        q_ref = take(1)[0] if with_h else None
        k_ref, v_ref, g_ref = take(3)
        per_dir.append((q_ref, k_ref, v_ref, g_ref))
    bias_ref = take(1)[0]
    if with_h:
        c0_ref, n0_ref, m0_ref = take(3)
        h_refs = take(2)
        c_out = n_out = m_out = None
    else:
        h_refs = [None, None]
        c_out, n_out, m_out = take(3)
    c_s, n_s, m_s = take(3)

    j = pl.program_id(1)

    @pl.when(j == 0)
    def _():
        if with_h:
            c_s[...] = c0_ref[0]
            n_s[...] = n0_ref[0]
            m_s[...] = m0_ref[0]
        else:
            c_s[...] = jnp.zeros_like(c_s)
            n_s[...] = jnp.zeros_like(n_s)
            m_s[...] = jnp.zeros_like(m_s)

    for d in range(2):
        q_ref, k_ref, v_ref, g_ref = per_dir[d]
        _mlstm_direction(d, q_ref, k_ref, v_ref, g_ref, bias_ref, c_s, n_s, m_s, h_refs[d])

    if not with_h:
        @pl.when(j == n_chunks - 1)
        def _():
            c_out[0] = c_s[...]
            n_out[0] = n_s[...]
            m_out[0] = m_s[...]


def _mlstm(q, k, p, v_blk, gates, bias, state):
    b, n, _ = k.shape
    n_chunks = n // CHUNK
    with_h = q is not None

    in_specs, args = [], []
    for d in range(2):
        c = (lambda j: j) if d == 0 else (lambda j: n_chunks - 1 - j)
        tok = pl.BlockSpec((1, CHUNK, GROUP_W), lambda i, j, c=c: (i, c(j), 0))
        if with_h:
            in_specs.append(tok)
            args.append(q)
        in_specs += [tok,
                     pl.BlockSpec((1, CHUNK, GROUP_W), lambda i, j, c=c: (i, c(j), v_blk)),
                     pl.BlockSpec((1, CHUNK, GATE_PAD), lambda i, j, c=c: (i, c(j), 0))]
        args += [k, p, gates]
    in_specs.append(pl.BlockSpec((1, GATE_PAD), lambda i, j: (0, 0)))
    args.append(bias)

    c_shape = (2, HEADS, HEAD_DIM, HEAD_DIM)
    v_shape = (2, HEADS, 1, HEAD_DIM)
    c_spec = pl.BlockSpec((1,) + c_shape, lambda i, j: (i, 0, 0, 0, 0))
    v_spec = pl.BlockSpec((1,) + v_shape, lambda i, j: (i, 0, 0, 0, 0))
    if with_h:
        in_specs += [c_spec, v_spec, v_spec]
        args += list(state)
        out_specs = [pl.BlockSpec((1, CHUNK, GROUP_W), lambda i, j: (i, j, 0)),
                     pl.BlockSpec((1, CHUNK, GROUP_W), lambda i, j: (i, n_chunks - 1 - j, 0))]
        out_shape = [jax.ShapeDtypeStruct((b, n, GROUP_W), F32)] * 2
    else:
        out_specs = [c_spec, v_spec, v_spec]
        out_shape = [jax.ShapeDtypeStruct((b,) + c_shape, F32),
                     jax.ShapeDtypeStruct((b,) + v_shape, F32),
                     jax.ShapeDtypeStruct((b,) + v_shape, F32)]
    return pl.pallas_call(
        functools.partial(_mlstm_kernel, n_chunks=n_chunks, with_h=with_h),
        grid=(b, n_chunks),
        in_specs=in_specs,
        out_specs=out_specs,
        out_shape=out_shape,
        scratch_shapes=[pltpu.VMEM(c_shape, F32), pltpu.VMEM(v_shape, F32), pltpu.VMEM(v_shape, F32)],
        compiler_params=_params("arbitrary", "arbitrary"),
        name="mlstm" if with_h else "mlstm_ctx_state",
    )(*args)


def _layer_norm(x):
    mu = jnp.mean(x, axis=-1, keepdims=True)
    var = jnp.mean(jnp.square(x - mu), axis=-1, keepdims=True)
    return (x - mu) * lax.rsqrt(var + EPS)


def _postmix_kernel(u_ref, vg_ref, o_ref, hf_ref, hb_ref, x_ref, mod_ref, ws_ref, bs_ref, ng_ref,
                    wout_ref, fg_ref, wr_ref, x1_ref, h2_ref, aff_ref, ycat_s):
    tm = x_ref.shape[1]
    b = pl.program_id(0)
    mod = lambda k: mod_ref[pl.ds(b, 1), k * D_MODEL:(k + 1) * D_MODEL]

    u = jax.nn.gelu(u_ref[0])
    v = _layer_norm(jax.nn.gelu(vg_ref[0])).astype(BF16)
    for c in range(tm // CHUNK):
        rows = slice(c * CHUNK, (c + 1) * CHUNK)
        for hd in range(HEADS):
            lanes = slice(hd * HEAD_DIM, (hd + 1) * HEAD_DIM)
            s = _dot(ws_ref[hd], v[rows, lanes]) + bs_ref[hd]
            ycat_s[rows, lanes] = (u[rows, lanes] * s).astype(BF16)

    hsum = hf_ref[0] + hb_ref[0]
    o = jax.nn.sigmoid(o_ref[0])
    for hd in range(HEADS):
        lanes = slice(hd * HEAD_DIM, (hd + 1) * HEAD_DIM)
        hn = _layer_norm(hsum[:, lanes]) * ng_ref[:, lanes]
        ycat_s[:, GROUP_W + hd * HEAD_DIM:GROUP_W + (hd + 1) * HEAD_DIM] = (o[:, lanes] * hn).astype(BF16)

    y = _dot(ycat_s[...], wout_ref[...])
    x1 = x_ref[0] + mod(2) * y
    x1_ref[0] = x1

    n2 = x1 * lax.rsqrt(jnp.mean(x1 * x1, axis=-1, keepdims=True) + EPS) * fg_ref[...]
    h2 = n2 * (1.0 + mod(4)) + mod(3)
    h2_ref[0] = h2.astype(BF16)

    logits = _dot3(h2, wr_ref[...])
    logits = jnp.where(_iota2(logits.shape, 1) < N_EXPERTS, logits, -jnp.inf)
    e = jnp.exp(logits - jnp.max(logits, axis=-1, keepdims=True))
    aff = e / jnp.sum(e, axis=-1, keepdims=True)
    aff_ref[0] = aff.T[0:N_EXPERTS, :]


def _postmix(p, hf, hb, x, mod, ws, bs, ng, wout, fg, wr, tm):
    b, n, _ = x.shape
    tok = lambda blk: pl.BlockSpec((1, tm, GROUP_W), lambda i, j: (i, j, blk))
    full = lambda a: pl.BlockSpec(a.shape, lambda i, j: (0,) * a.ndim)
    return pl.pallas_call(
        _postmix_kernel,
        grid=(b, n // tm),
        in_specs=[tok(U_BLK), tok(VG_BLK), tok(O_BLK), tok(0), tok(0),
                  pl.BlockSpec((1, tm, D_MODEL), lambda i, j: (i, j, 0)),
                  full(mod), full(ws), full(bs), full(ng), full(wout), full(fg), full(wr)],
        out_specs=[pl.BlockSpec((1, tm, D_MODEL), lambda i, j: (i, j, 0)),
                   pl.BlockSpec((1, tm, D_MODEL), lambda i, j: (i, j, 0)),
                   pl.BlockSpec((1, N_EXPERTS, tm), lambda i, j: (i, 0, j))],
        out_shape=[jax.ShapeDtypeStruct((b, n, D_MODEL), F32),
                   jax.ShapeDtypeStruct((b, n, D_MODEL), BF16),
                   jax.ShapeDtypeStruct((b, N_EXPERTS, n), F32)],
        scratch_shapes=[pltpu.VMEM((tm, D_MODEL), BF16)],
        compiler_params=_params("arbitrary", "arbitrary"),
        name="postmix",
    )(p, p, p, hf, hb, x, mod, ws, bs, ng, wout, fg, wr)


def _cumsum_lanes(x, upper):
    carry = jnp.zeros((x.shape[0], 1), F32)
    outs = []
    for j in range(x.shape[1] // 128):
        c = _dot(x[:, j * 128:(j + 1) * 128].astype(BF16), upper) + carry
        outs.append(c)
        carry = c[:, 127:128]
    return jnp.concatenate(outs, axis=1)


def _route_kernel(aff_ref, slot_ref, w_ref, *, capacity):
    aff = aff_ref[0]
    cap = float(capacity)
    thr_bits = jnp.zeros((aff.shape[0], 1), jnp.int32)
    for bit in range(30, -1, -1):
        cand = thr_bits | (1 << bit)
        cnt = jnp.sum(jnp.where(aff >= pltpu.bitcast(cand, F32), 1.0, 0.0), axis=-1, keepdims=True)
        thr_bits = jnp.where(cnt >= cap, cand, thr_bits)
    thr = pltpu.bitcast(thr_bits, F32)
    upper = jnp.where(_iota2((128, 128), 0) <= _iota2((128, 128), 1), 1.0, 0.0).astype(BF16)
    above = jnp.where(aff > thr, 1.0, 0.0)
    tied = jnp.where(aff == thr, 1.0, 0.0)
    need = cap - jnp.sum(above, axis=-1, keepdims=True)
    sel = above + tied * jnp.where(_cumsum_lanes(tied, upper) <= need, 1.0, 0.0)
    pos = _cumsum_lanes(sel, upper) - 1.0
    slot_ref[0] = jnp.where(sel > 0.0, pos, -1.0)
    w_ref[0] = jnp.where(sel > 0.0, aff, 0.0)


def _route(aff_t, capacity):
    b, e, n = aff_t.shape
    spec = pl.BlockSpec((1, e, n), lambda i: (i, 0, 0))
    return pl.pallas_call(
        functools.partial(_route_kernel, capacity=capacity),
        grid=(b,),
        in_specs=[spec],
        out_specs=[spec, spec],
        out_shape=[jax.ShapeDtypeStruct((b, e, n), F32)] * 2,
        compiler_params=_params("arbitrary"),
        name="route",
    )(aff_t)


def _gather_kernel(h2_ref, slot_ref, w_ref, xe_ref, gate_ref, *, tok_blk):
    cap = xe_ref.shape[2]
    n = h2_ref.shape[1]
    slot_id = _iota2((cap, 1), 0).astype(F32)
    xe = jnp.zeros((cap, D_MODEL), F32)
    gate = jnp.zeros((cap, 1), F32)
    for j in range(n // tok_blk):
        toks = slice(j * tok_blk, (j + 1) * tok_blk)
        hit = slot_ref[0, 0, :, toks] == slot_id
        xe = xe + _dot(jnp.where(hit, 1.0, 0.0).astype(BF16), h2_ref[0, toks, :])
        gate = gate + jnp.sum(jnp.where(hit, w_ref[0, 0, :, toks], 0.0), axis=1, keepdims=True)
    xe_ref[0, 0] = xe.astype(BF16)
    gate_ref[0, 0] = gate


def _gather(h2, slot, w, capacity, tok_blk):
    b, n, _ = h2.shape
    n_e = slot.shape[1]
    row = pl.BlockSpec((1, 1, 1, n), lambda i, e: (i, e, 0, 0))
    return pl.pallas_call(
        functools.partial(_gather_kernel, tok_blk=tok_blk),
        grid=(b, n_e),
        in_specs=[pl.BlockSpec((1, n, D_MODEL), lambda i, e: (i, 0, 0)), row, row],
        out_specs=[pl.BlockSpec((1, 1, capacity, D_MODEL), lambda i, e: (i, e, 0, 0)),
                   pl.BlockSpec((1, 1, capacity, 1), lambda i, e: (i, e, 0, 0))],
        out_shape=[jax.ShapeDtypeStruct((b, n_e, capacity, D_MODEL), BF16),
                   jax.ShapeDtypeStruct((b, n_e, capacity, 1), F32)],
        compiler_params=_params("arbitrary", "arbitrary"),
        name="gather",
    )(h2, slot, w)


def _ffn_kernel(xe_ref, gate_ref, wg_ref, wu_ref, wd_ref, ye_ref, acc_s):
    ft = pl.program_id(1)

    @pl.when(ft == 0)
    def _():
        acc_s[...] = jnp.zeros_like(acc_s)

    wg = wg_ref[0].astype(BF16)
    wu = wu_ref[0].astype(BF16)
    wd = wd_ref[0].astype(BF16)
    for i in range(xe_ref.shape[0]):
        xe = xe_ref[i, 0]
        act = jax.nn.silu(_dot(xe, wg)) * _dot(xe, wu)
        acc_s[i] += _dot(act.astype(BF16), wd)

    @pl.when(ft == pl.num_programs(1) - 1)
    def _():
        for i in range(xe_ref.shape[0]):
            ye_ref[i] = (acc_s[i] * gate_ref[i, 0]).astype(BF16)


def _ffn(xe, gate, wg, wu, wd, f_tile):
    b, n_e, cap, _ = xe.shape
    return pl.pallas_call(
        _ffn_kernel,
        grid=(n_e, D_EXPERT // f_tile),
        in_specs=[pl.BlockSpec((b, 1, cap, D_MODEL), lambda e, f: (0, e, 0, 0)),
                  pl.BlockSpec((b, 1, cap, 1), lambda e, f: (0, e, 0, 0)),
                  pl.BlockSpec((1, D_MODEL, f_tile), lambda e, f: (e, 0, f)),
                  pl.BlockSpec((1, D_MODEL, f_tile), lambda e, f: (e, 0, f)),
                  pl.BlockSpec((1, f_tile, D_MODEL), lambda e, f: (e, f, 0))],
        out_specs=pl.BlockSpec((b, cap, D_MODEL), lambda e, f: (0, e, 0)),
        out_shape=jax.ShapeDtypeStruct((b, n_e * cap, D_MODEL), BF16),
        scratch_shapes=[pltpu.VMEM((b, cap, D_MODEL), F32)],
        compiler_params=_params("arbitrary", "arbitrary"),
        name="ffn",
    )(xe, gate, wg, wu, wd)


def _combine_kernel(slot_ref, ye_ref, x1_ref, mod_ref, fg_ref, o_ref, *, capacity):
    b = pl.program_id(0)
    n_e = slot_ref.shape[1]
    slot_lane = _iota2((1, capacity), 1).astype(F32)
    acc = jnp.zeros(o_ref.shape[1:], F32)
    for e in range(n_e):
        onehot = jnp.where(_row_to_col(slot_ref[0, e]) == slot_lane, 1.0, 0.0).astype(BF16)
        acc = acc + _dot(onehot, ye_ref[0, e * capacity:(e + 1) * capacity, :])
    g2 = mod_ref[pl.ds(b, 1), 5 * D_MODEL:6 * D_MODEL]
    x2 = x1_ref[0] + g2 * acc
    o_ref[0] = x2 * lax.rsqrt(jnp.mean(x2 * x2, axis=-1, keepdims=True) + EPS) * fg_ref[...]


def _combine(slot, ye, x1, mod, fg, capacity, tm):
    b, n, _ = x1.shape
    n_e = slot.shape[1]
    tok = pl.BlockSpec((1, tm, D_MODEL), lambda i, j: (i, j, 0))
    return pl.pallas_call(
        functools.partial(_combine_kernel, capacity=capacity),
        grid=(b, n // tm),
        in_specs=[pl.BlockSpec((1, n_e, 1, tm), lambda i, j: (i, 0, 0, j)),
                  pl.BlockSpec((1, n_e * capacity, D_MODEL), lambda i, j: (i, 0, 0)),
                  tok,
                  pl.BlockSpec(mod.shape, lambda i, j: (0, 0)),
                  pl.BlockSpec((1, D_MODEL), lambda i, j: (0, 0))],
        out_specs=tok,
        out_shape=jax.ShapeDtypeStruct((b, n, D_MODEL), F32),
        compiler_params=_params("arbitrary", "arbitrary"),
        name="combine",
    )(slot, ye, x1, mod, fg)


def kernel(x, c, ctx, c_ctx, w_mod, b_mod, norm_mix_g, w_in, conv_q, conv_k, b_igate, b_fgate,
           gmlp_ws, gmlp_bs, mlstm_norm_g, w_out, norm_ffn_g, w_router, w_gate_e, w_up_e,
           w_down_e, final_g):
    depth = w_mod.shape[0]
    assert depth == 1, "the context stream is only carried as mLSTM states (single layer)"
    batch, seq, _ = x.shape
    assert seq % GRID_W == 0 and seq % CHUNK == 0 and batch + 1 <= MOD_ROWS
    capacity = EC_FACTOR * seq // N_EXPERTS
    ctx_row = batch
    l = 0

    cond = jnp.concatenate([c, c_ctx[None], jnp.zeros((MOD_ROWS - batch - 1, D_MODEL), F32)], axis=0)
    mod = _adaln(cond, w_mod[l], b_mod[l][None])

    row = lambda a: a[None]
    w_main = w_in[l][:, :MAIN_W].astype(BF16)
    w_gates = jnp.pad(w_in[l][:, MAIN_W:], ((0, 0), (0, GATE_PAD - N_GATES)))
    gate_bias = jnp.pad(jnp.concatenate([b_igate[l].reshape(-1), b_fgate[l].reshape(-1)]),
                        (0, GATE_PAD - N_GATES))[None]

    p_c, gates_c = _inproj(ctx, mod, row(norm_mix_g[l]), w_main[:, K_BLK * GROUP_W:], w_gates,
                           tm=ctx.shape[1], ctx_row=ctx_row)
    k_scale = HEAD_DIM ** -0.5
    (k_c,) = _dwconv(p_c, [(0, conv_k[l], k_scale)], tm=ctx.shape[1])
    state = _mlstm(None, k_c, p_c, 1, gates_c, gate_bias, None)

    p, gates = _inproj(x, mod, row(norm_mix_g[l]), w_main, w_gates, tm=512, ctx_row=None)
    q_l, k_l = _dwconv(p, [(Q_BLK, conv_q[l], 1.0), (K_BLK, conv_k[l], k_scale)], tm=512)
    h_f, h_b = _mlstm(q_l, k_l, p, V_BLK, gates, gate_bias, state)

    x1, h2, aff_t = _postmix(p, h_f, h_b, x, mod, gmlp_ws[l].astype(BF16), gmlp_bs[l][:, :, None],
                             row(mlstm_norm_g[l]), w_out[l].astype(BF16), row(norm_ffn_g[l]),
                             jnp.pad(w_router[l], ((0, 0), (0, 128 - N_EXPERTS))), tm=512)

    slot, gate_w = _route(aff_t, capacity)
    slot = slot[:, :, None, :]
    gate_w = gate_w[:, :, None, :]

    xe, gate = _gather(h2, slot, gate_w, capacity, tok_blk=1024)
    ye = _ffn(xe, gate, w_gate_e[l], w_up_e[l], w_down_e[l], f_tile=512)
    return _combine(slot, ye, x1, mod, row(final_g), capacity, tm=512)
```

```python
import functools

import jax
import jax.numpy as jnp
from jax import lax
from jax.experimental import pallas as pl
from jax.experimental.pallas import tpu as pltpu

F32 = jnp.float32
BF16 = jnp.bfloat16

D_MODEL = 1024
GRID_W = 64
CHUNK = 128
HEADS = 4
GROUP_W = D_MODEL // 2
HEAD_DIM = GROUP_W // HEADS
N_EXPERTS = 16
EC_FACTOR = 2
D_EXPERT = 2 * D_MODEL
EPS = 1e-6
N_GATES = 4 * HEADS
GATE_PAD = 128
MOD_ROWS = 8

U_BLK, VG_BLK, Q_BLK, O_BLK, K_BLK, V_BLK = 0, 1, 2, 3, 4, 5
MAIN_W = 6 * GROUP_W

VMEM_LIMIT = 56 * 1024 * 1024

def _params(*sem):
    return pltpu.CompilerParams(dimension_semantics=sem, vmem_limit_bytes=VMEM_LIMIT)


def _dot(a, b):
    return jnp.dot(a, b, preferred_element_type=F32)


def _dot_nt(a, b):
    return lax.dot_general(a, b, (((1,), (1,)), ((), ())), preferred_element_type=F32)


def _split2(a):
    hi = a.astype(BF16)
    lo = (a - hi.astype(F32)).astype(BF16)
    return hi, lo


def _dot3(a, b):
    ah, al = _split2(a)
    bh, bl = _split2(b)
    return _dot(ah, bh) + (_dot(al, bh) + _dot(ah, bl))


def _dot_exact01(tri, x):
    x1 = x.astype(BF16)
    r1 = x - x1.astype(F32)
    x2 = r1.astype(BF16)
    x3 = (r1 - x2.astype(F32)).astype(BF16)
    return _dot(tri, x1) + (_dot(tri, x2) + _dot(tri, x3))


def _iota2(shape, dim):
    return lax.broadcasted_iota(jnp.int32, shape, dim)


def _row_to_col(row):
    n = row.shape[1] // 128
    eye = _iota2((128, 128), 0) == _iota2((128, 128), 1)
    cols = [jnp.sum(jnp.where(eye, row[:, j * 128:(j + 1) * 128], 0.0), axis=1, keepdims=True)
            for j in range(n)]
    return cols[0] if n == 1 else jnp.concatenate(cols, axis=0)


def _adaln_kernel(cond_ref, w_ref, b_ref, o_ref):
    o_ref[...] = _dot3(jax.nn.silu(cond_ref[...]), w_ref[...]) + b_ref[...]


def _adaln(cond, w, b):
    n_out = w.shape[1]
    tn = D_MODEL
    return pl.pallas_call(
        _adaln_kernel,
        grid=(n_out // tn,),
        in_specs=[pl.BlockSpec((MOD_ROWS, D_MODEL), lambda j: (0, 0)),
                  pl.BlockSpec((D_MODEL, tn), lambda j: (0, j)),
                  pl.BlockSpec((1, tn), lambda j: (0, j))],
        out_specs=pl.BlockSpec((MOD_ROWS, tn), lambda j: (0, j)),
        out_shape=jax.ShapeDtypeStruct((MOD_ROWS, n_out), F32),
        compiler_params=_params("arbitrary"),
        name="adaln",
    )(cond, w, b)


def _inproj_kernel(x_ref, mod_ref, g_ref, w_ref, wg_ref, p_ref, gate_ref, *, ctx_row):
    row = pl.program_id(0) if ctx_row is None else ctx_row
    x = x_ref[0]
    y = x * lax.rsqrt(jnp.mean(x * x, axis=-1, keepdims=True) + EPS) * g_ref[...]
    sh = mod_ref[pl.ds(row, 1), 0:D_MODEL]
    sc = mod_ref[pl.ds(row, 1), D_MODEL:2 * D_MODEL]
    h = y * (1.0 + sc) + sh
    p_ref[0] = _dot(h.astype(BF16), w_ref[...])
    gate_ref[0] = _dot3(h, wg_ref[...])


def _inproj(x, mod, g, w, wg, tm, ctx_row):
    b, n, _ = x.shape
    wn = w.shape[1]
    return pl.pallas_call(
        functools.partial(_inproj_kernel, ctx_row=ctx_row),
        grid=(b, n // tm),
        in_specs=[pl.BlockSpec((1, tm, D_MODEL), lambda i, j: (i, j, 0)),
                  pl.BlockSpec(mod.shape, lambda i, j: (0, 0)),
                  pl.BlockSpec((1, D_MODEL), lambda i, j: (0, 0)),
                  pl.BlockSpec((D_MODEL, wn), lambda i, j: (0, 0)),
                  pl.BlockSpec((D_MODEL, GATE_PAD), lambda i, j: (0, 0))],
        out_specs=[pl.BlockSpec((1, tm, wn), lambda i, j: (i, j, 0)),
                   pl.BlockSpec((1, tm, GATE_PAD), lambda i, j: (i, j, 0))],
        out_shape=[jax.ShapeDtypeStruct((b, n, wn), F32),
                   jax.ShapeDtypeStruct((b, n, GATE_PAD), F32)],
        compiler_params=_params("arbitrary", "arbitrary"),
        name="inproj",
    )(x, mod, g, w, wg)


def _dwconv_kernel(*refs, n_tiles, scales):
    n_in = len(scales)
    j = pl.program_id(1)
    for s, scale in enumerate(scales):
        cur_ref, prev_ref, next_ref, w_ref = refs[4 * s:4 * s + 4]
        o_ref = refs[4 * n_in + s]
        cur = cur_ref[0]
        rows = cur.shape[0]
        prev_row = prev_ref[0, 7:8, :] * (j > 0).astype(F32)
        next_row = next_ref[0, 0:1, :] * (j < n_tiles - 1).astype(F32)
        t = _iota2((rows, 1), 0)
        before = jnp.where(t == 0, prev_row, pltpu.roll(cur, 1, axis=0))
        after = jnp.where(t == rows - 1, next_row, pltpu.roll(cur, rows - 1, axis=0))
        y = before * w_ref[0:1, :] + cur * w_ref[1:2, :] + after * w_ref[2:3, :]
        o_ref[0] = (jax.nn.silu(y) * scale).astype(BF16)


def _dwconv(p, streams, tm):
    b, n, _ = p.shape
    n_tiles = n // tm
    rows8 = tm // 8
    last8 = n // 8 - 1
    in_specs, args = [], []
    for blk, w, _ in streams:
        in_specs += [pl.BlockSpec((1, tm, GROUP_W), lambda i, j, blk=blk: (i, j, blk)),
                     pl.BlockSpec((1, 8, GROUP_W),
                                  lambda i, j, blk=blk: (i, jnp.maximum(j * rows8 - 1, 0), blk)),
                     pl.BlockSpec((1, 8, GROUP_W),
                                  lambda i, j, blk=blk: (i, jnp.minimum((j + 1) * rows8, last8), blk)),
                     pl.BlockSpec((3, GROUP_W), lambda i, j: (0, 0))]
        args += [p, p, p, w]
    return pl.pallas_call(
        functools.partial(_dwconv_kernel, n_tiles=n_tiles, scales=tuple(s for _, _, s in streams)),
        grid=(b, n_tiles),
        in_specs=in_specs,
        out_specs=[pl.BlockSpec((1, tm, GROUP_W), lambda i, j: (i, j, 0))] * len(streams),
        out_shape=[jax.ShapeDtypeStruct((b, n, GROUP_W), BF16)] * len(streams),
        compiler_params=_params("arbitrary", "arbitrary"),
        name="dwconv",
    )(*args)


def _mlstm_direction(d, q_ref, k_ref, v_ref, g_ref, bias_ref, c_s, n_s, m_s, h_ref):
    with_h = h_ref is not None
    i0 = _iota2((CHUNK, CHUNK), 0)
    i1 = _iota2((CHUNK, CHUNK), 1)
    sees_ts = (i1 <= i0) if d == 0 else (i1 >= i0)
    sees_st = (i0 <= i1) if d == 0 else (i0 >= i1)
    tri = jnp.where(sees_ts, 1.0, 0.0).astype(BF16)

    gates = g_ref[0] + bias_ref[...]
    bcum = _dot_exact01(tri, jax.nn.log_sigmoid(gates))
    gates_t = gates.T
    bcum_t = bcum.T
    last = CHUNK - 1 if d == 0 else 0
    k_all = k_ref[0]
    v_all = v_ref[0]
    q_all = q_ref[0] if with_h else None

    for hd in range(HEADS):
        ci = d * HEADS + hd
        cf = 2 * HEADS + d * HEADS + hd
        lanes = slice(hd * HEAD_DIM, (hd + 1) * HEAD_DIM)
        li_row = gates_t[ci:ci + 1, :]
        bc_row = bcum_t[cf:cf + 1, :]
        b_last = bc_row[:, last:last + 1]
        c_prev = c_s[d, hd]
        n_prev = n_s[d, hd]
        m_prev = m_s[d, hd]
        k = k_all[:, lanes]
        v_t = v_all[:, lanes].T

        if with_h:
            q = q_all[:, lanes]
            lhs = jnp.concatenate([k, c_prev.astype(BF16),
                                   jnp.broadcast_to(n_prev, (16, HEAD_DIM)).astype(BF16)], axis=0)
            prod = _dot_nt(lhs, q)
            kq, cq, nq = prod[0:CHUNK], prod[CHUNK:2 * CHUNK], prod[2 * CHUNK:2 * CHUNK + 1]
            u_col = gates[:, ci:ci + 1] - bcum[:, cf:cf + 1]
            g = bc_row + m_prev
            dmat = jnp.where(sees_st, u_col + bc_row, -jnp.inf)
            m_t = jnp.maximum(g, jnp.max(dmat, axis=0, keepdims=True))
            inter = jnp.exp(g - m_t)
            s = kq * jnp.exp(dmat - m_t)
            num = inter * cq + _dot(v_t.astype(BF16), s.astype(BF16))
            den = inter * nq + jnp.sum(s, axis=0, keepdims=True)
            h_t = num * (1.0 / jnp.maximum(jnp.abs(den), jnp.exp(-m_t)))
            h_ref[0, :, lanes] = h_t.T

        a_row = b_last - bc_row + li_row
        m_new = jnp.maximum(b_last + m_prev, jnp.max(a_row, axis=-1, keepdims=True))
        decay = jnp.exp(b_last + m_prev - m_new)
        w_row = jnp.exp(a_row - m_new)
        lhs = jnp.concatenate([v_t * w_row, jnp.broadcast_to(w_row, (16, CHUNK))], axis=0)
        upd = _dot(lhs.astype(BF16), k)
        c_s[d, hd] = decay * c_prev + upd[0:HEAD_DIM]
        n_s[d, hd] = decay * n_prev + upd[HEAD_DIM:HEAD_DIM + 1]
        m_s[d, hd] = m_new


def _mlstm_kernel(*refs, n_chunks, with_h):
    refs = list(refs)
    take = lambda n: [refs.pop(0) for _ in range(n)]
    per_dir = []
    for _ in range(2):
        q_ref = take(1)[0] if with_h else None
        k_ref, v_ref, g_ref = take(3)
        per_dir.append((q_ref, k_ref, v_ref, g_ref))
    bias_ref = take(1)[0]
    if with_h:
        c0_ref, n0_ref, m0_ref = take(3)
        h_refs = take(2)
        c_out = n_out = m_out = None
    else:
        h_refs = [None, None]
        c_out, n_out, m_out = take(3)
    c_s, n_s, m_s = take(3)

    j = pl.program_id(1)

    @pl.when(j == 0)
    def _():
        if with_h:
            c_s[...] = c0_ref[0]
            n_s[...] = n0_ref[0]
            m_s[...] = m0_ref[0]
        else:
            c_s[...] = jnp.zeros_like(c_s)
            n_s[...] = jnp.zeros_like(n_s)
            m_s[...] = jnp.zeros_like(m_s)

    for d in range(2):
        q_ref, k_ref, v_ref, g_ref = per_dir[d]
        _mlstm_direction(d, q_ref, k_ref, v_ref, g_ref, bias_ref, c_s, n_s, m_s, h_refs[d])

    if not with_h:
        @pl.when(j == n_chunks - 1)
        def _():
            c_out[0] = c_s[...]
            n_out[0] = n_s[...]
            m_out[0] = m_s[...]


def _mlstm(q, k, p, v_blk, gates, bias, state):
    b, n, _ = k.shape
    n_chunks = n // CHUNK
    with_h = q is not None

    in_specs, args = [], []
    for d in range(2):
        c = (lambda j: j) if d == 0 else (lambda j: n_chunks - 1 - j)
        tok = pl.BlockSpec((1, CHUNK, GROUP_W), lambda i, j, c=c: (i, c(j), 0))
        if with_h:
            in_specs.append(tok)
            args.append(q)
        in_specs += [tok,
                     pl.BlockSpec((1, CHUNK, GROUP_W), lambda i, j, c=c: (i, c(j), v_blk)),
                     pl.BlockSpec((1, CHUNK, GATE_PAD), lambda i, j, c=c: (i, c(j), 0))]
        args += [k, p, gates]
    in_specs.append(pl.BlockSpec((1, GATE_PAD), lambda i, j: (0, 0)))
    args.append(bias)

    c_shape = (2, HEADS, HEAD_DIM, HEAD_DIM)
    v_shape = (2, HEADS, 1, HEAD_DIM)
    c_spec = pl.BlockSpec((1,) + c_shape, lambda i, j: (i, 0, 0, 0, 0))
    v_spec = pl.BlockSpec((1,) + v_shape, lambda i, j: (i, 0, 0, 0, 0))
    if with_h:
        in_specs += [c_spec, v_spec, v_spec]
        args += list(state)
        out_specs = [pl.BlockSpec((1, CHUNK, GROUP_W), lambda i, j: (i, j, 0)),
                     pl.BlockSpec((1, CHUNK, GROUP_W), lambda i, j: (i, n_chunks - 1 - j, 0))]
        out_shape = [jax.ShapeDtypeStruct((b, n, GROUP_W), F32)] * 2
    else:
        out_specs = [c_spec, v_spec, v_spec]
        out_shape = [jax.ShapeDtypeStruct((b,) + c_shape, F32),
                     jax.ShapeDtypeStruct((b,) + v_shape, F32),
                     jax.ShapeDtypeStruct((b,) + v_shape, F32)]
    return pl.pallas_call(
        functools.partial(_mlstm_kernel, n_chunks=n_chunks, with_h=with_h),
        grid=(b, n_chunks),
        in_specs=in_specs,
        out_specs=out_specs,
        out_shape=out_shape,
        scratch_shapes=[pltpu.VMEM(c_shape, F32), pltpu.VMEM(v_shape, F32), pltpu.VMEM(v_shape, F32)],
        compiler_params=_params("arbitrary", "arbitrary"),
        name="mlstm" if with_h else "mlstm_ctx_state",
    )(*args)


def _layer_norm(x):
    mu = jnp.mean(x, axis=-1, keepdims=True)
    var = jnp.mean(jnp.square(x - mu), axis=-1, keepdims=True)
    return (x - mu) * lax.rsqrt(var + EPS)


def _postmix_kernel(u_ref, vg_ref, o_ref, hf_ref, hb_ref, x_ref, mod_ref, ws_ref, bs_ref, ng_ref,
                    wout_ref, fg_ref, wr_ref, x1_ref, h2_ref, aff_ref, ycat_s):
    tm = x_ref.shape[1]
    b = pl.program_id(0)
    mod = lambda k: mod_ref[pl.ds(b, 1), k * D_MODEL:(k + 1) * D_MODEL]

    u = jax.nn.gelu(u_ref[0])
    v = _layer_norm(jax.nn.gelu(vg_ref[0])).astype(BF16)
    for c in range(tm // CHUNK):
        rows = slice(c * CHUNK, (c + 1) * CHUNK)
        for hd in range(HEADS):
            lanes = slice(hd * HEAD_DIM, (hd + 1) * HEAD_DIM)
            s = _dot(ws_ref[hd], v[rows, lanes]) + bs_ref[hd]
            ycat_s[rows, lanes] = (u[rows, lanes] * s).astype(BF16)

    hsum = hf_ref[0] + hb_ref[0]
    o = jax.nn.sigmoid(o_ref[0])
    for hd in range(HEADS):
        lanes = slice(hd * HEAD_DIM, (hd + 1) * HEAD_DIM)
        hn = _layer_norm(hsum[:, lanes]) * ng_ref[:, lanes]
        ycat_s[:, GROUP_W + hd * HEAD_DIM:GROUP_W + (hd + 1) * HEAD_DIM] = (o[:, lanes] * hn).astype(BF16)

    y = _dot(ycat_s[...], wout_ref[...])
    x1 = x_ref[0] + mod(2) * y
    x1_ref[0] = x1

    n2 = x1 * lax.rsqrt(jnp.mean(x1 * x1, axis=-1, keepdims=True) + EPS) * fg_ref[...]
    h2 = n2 * (1.0 + mod(4)) + mod(3)
    h2_ref[0] = h2.astype(BF16)

    logits = _dot3(h2, wr_ref[...])
    logits = jnp.where(_iota2(logits.shape, 1) < N_EXPERTS, logits, -jnp.inf)
    e = jnp.exp(logits - jnp.max(logits, axis=-1, keepdims=True))
    aff = e / jnp.sum(e, axis=-1, keepdims=True)
    aff_ref[0] = aff.T[0:N_EXPERTS, :]


def _postmix(p, hf, hb, x, mod, ws, bs, ng, wout, fg, wr, tm):
    b, n, _ = x.shape
    tok = lambda blk: pl.BlockSpec((1, tm, GROUP_W), lambda i, j: (i, j, blk))
    full = lambda a: pl.BlockSpec(a.shape, lambda i, j: (0,) * a.ndim)
    return pl.pallas_call(
        _postmix_kernel,
        grid=(b, n // tm),
        in_specs=[tok(U_BLK), tok(VG_BLK), tok(O_BLK), tok(0), tok(0),
                  pl.BlockSpec((1, tm, D_MODEL), lambda i, j: (i, j, 0)),
                  full(mod), full(ws), full(bs), full(ng), full(wout), full(fg), full(wr)],
        out_specs=[pl.BlockSpec((1, tm, D_MODEL), lambda i, j: (i, j, 0)),
                   pl.BlockSpec((1, tm, D_MODEL), lambda i, j: (i, j, 0)),
                   pl.BlockSpec((1, N_EXPERTS, tm), lambda i, j: (i, 0, j))],
        out_shape=[jax.ShapeDtypeStruct((b, n, D_MODEL), F32),
                   jax.ShapeDtypeStruct((b, n, D_MODEL), BF16),
                   jax.ShapeDtypeStruct((b, N_EXPERTS, n), F32)],
        scratch_shapes=[pltpu.VMEM((tm, D_MODEL), BF16)],
        compiler_params=_params("arbitrary", "arbitrary"),
        name="postmix",
    )(p, p, p, hf, hb, x, mod, ws, bs, ng, wout, fg, wr)


def _cumsum_lanes(x, upper):
    carry = jnp.zeros((x.shape[0], 1), F32)
    outs, before = [], []
    for j in range(x.shape[1] // 128):
        before.append(carry)
        c = _dot(x[:, j * 128:(j + 1) * 128].astype(BF16), upper) + carry
        outs.append(c)
        carry = c[:, 127:128]
    return jnp.concatenate(outs, axis=1), before


def _rows_to_lanes(x, fill):
    pad = jnp.full((128 - x.shape[0], 128), fill, F32)
    return jnp.concatenate([x, pad], axis=0).T


def _route_kernel(aff_ref, slot_ref, w_ref, slot_t_ref, start_t_ref, *, capacity, window):
    aff = aff_ref[0]
    cap = float(capacity)
    thr_bits = jnp.zeros((aff.shape[0], 1), jnp.int32)
    for bit in range(30, -1, -1):
        cand = thr_bits | (1 << bit)
        cnt = jnp.sum(jnp.where(aff >= pltpu.bitcast(cand, F32), 1.0, 0.0), axis=-1, keepdims=True)
        thr_bits = jnp.where(cnt >= cap, cand, thr_bits)
    thr = pltpu.bitcast(thr_bits, F32)
    upper = jnp.where(_iota2((128, 128), 0) <= _iota2((128, 128), 1), 1.0, 0.0).astype(BF16)
    above = jnp.where(aff > thr, 1.0, 0.0)
    tied = jnp.where(aff == thr, 1.0, 0.0)
    need = cap - jnp.sum(above, axis=-1, keepdims=True)
    sel = above + tied * jnp.where(_cumsum_lanes(tied, upper)[0] <= need, 1.0, 0.0)
    count, before = _cumsum_lanes(sel, upper)
    slot = jnp.where(sel > 0.0, count - 1.0, -1.0)
    slot_ref[0] = slot
    w_ref[0] = jnp.where(sel > 0.0, aff, 0.0)

    n_blocks = len(before)
    for j in range(n_blocks):
        slot_t_ref[0, j * 128:(j + 1) * 128, :] = _rows_to_lanes(slot[:, j * 128:(j + 1) * 128], -1.0)
    lane = _iota2((1, 128), 1)
    first = jnp.zeros((aff.shape[0], 128), F32)
    for j in range(n_blocks):
        first = jnp.where(lane == j, before[j], first)
    start = jnp.minimum(jnp.floor(first * (1.0 / 16.0)) * 16.0, float(capacity - window))
    start_t_ref[0] = _rows_to_lanes(start, 0.0)[0:n_blocks, :]


def _route(aff_t, capacity, window):
    b, e, n = aff_t.shape
    assert n // 128 <= 128 and e <= 128 and window <= capacity
    spec = pl.BlockSpec((1, e, n), lambda i: (i, 0, 0))
    return pl.pallas_call(
        functools.partial(_route_kernel, capacity=capacity, window=window),
        grid=(b,),
        in_specs=[spec],
        out_specs=[spec, spec,
                   pl.BlockSpec((1, n, 128), lambda i: (i, 0, 0)),
                   pl.BlockSpec((1, n // 128, 128), lambda i: (i, 0, 0))],
        out_shape=[jax.ShapeDtypeStruct((b, e, n), F32)] * 2
        + [jax.ShapeDtypeStruct((b, n, 128), F32), jax.ShapeDtypeStruct((b, n // 128, 128), F32)],
        compiler_params=_params("arbitrary"),
        name="route",
    )(aff_t)


def _gather_kernel(h2_ref, slot_ref, w_ref, xe_ref, gate_ref, *, tok_blk):
    cap = xe_ref.shape[2]
    n = h2_ref.shape[1]
    slot_id = _iota2((cap, 1), 0).astype(F32)
    xe = jnp.zeros((cap, D_MODEL), F32)
    gate = jnp.zeros((cap, 1), F32)
    for j in range(n // tok_blk):
        toks = slice(j * tok_blk, (j + 1) * tok_blk)
        hit = slot_ref[0, 0, :, toks] == slot_id
        xe = xe + _dot(jnp.where(hit, 1.0, 0.0).astype(BF16), h2_ref[0, toks, :])
        gate = gate + jnp.sum(jnp.where(hit, w_ref[0, 0, :, toks], 0.0), axis=1, keepdims=True)
    xe_ref[0, 0] = xe.astype(BF16)
    gate_ref[0, 0] = gate


def _gather(h2, slot, w, capacity, tok_blk):
    b, n, _ = h2.shape
    n_e = slot.shape[1]
    row = pl.BlockSpec((1, 1, 1, n), lambda i, e: (i, e, 0, 0))
    return pl.pallas_call(
        functools.partial(_gather_kernel, tok_blk=tok_blk),
        grid=(b, n_e),
        in_specs=[pl.BlockSpec((1, n, D_MODEL), lambda i, e: (i, 0, 0)), row, row],
        out_specs=[pl.BlockSpec((1, 1, capacity, D_MODEL), lambda i, e: (i, e, 0, 0)),
                   pl.BlockSpec((1, 1, capacity, 1), lambda i, e: (i, e, 0, 0))],
        out_shape=[jax.ShapeDtypeStruct((b, n_e, capacity, D_MODEL), BF16),
                   jax.ShapeDtypeStruct((b, n_e, capacity, 1), F32)],
        compiler_params=_params("arbitrary", "arbitrary"),
        name="gather",
    )(h2, slot, w)


def _ffn_kernel(xe_ref, gate_ref, wg_ref, wu_ref, wd_ref, ye_ref, acc_s):
    ft = pl.program_id(1)

    @pl.when(ft == 0)
    def _():
        acc_s[...] = jnp.zeros_like(acc_s)

    wg = wg_ref[0].astype(BF16)
    wu = wu_ref[0].astype(BF16)
    wd = wd_ref[0].astype(BF16)
    for i in range(xe_ref.shape[0]):
        xe = xe_ref[i, 0]
        act = jax.nn.silu(_dot(xe, wg)) * _dot(xe, wu)
        acc_s[i] += _dot(act.astype(BF16), wd)

    @pl.when(ft == pl.num_programs(1) - 1)
    def _():
        for i in range(xe_ref.shape[0]):
            ye_ref[i] = (acc_s[i] * gate_ref[i, 0]).astype(BF16)


def _ffn(xe, gate, wg, wu, wd, f_tile):
    b, n_e, cap, _ = xe.shape
    return pl.pallas_call(
        _ffn_kernel,
        grid=(n_e, D_EXPERT // f_tile),
        in_specs=[pl.BlockSpec((b, 1, cap, D_MODEL), lambda e, f: (0, e, 0, 0)),
                  pl.BlockSpec((b, 1, cap, 1), lambda e, f: (0, e, 0, 0)),
                  pl.BlockSpec((1, D_MODEL, f_tile), lambda e, f: (e, 0, f)),
                  pl.BlockSpec((1, D_MODEL, f_tile), lambda e, f: (e, 0, f)),
                  pl.BlockSpec((1, f_tile, D_MODEL), lambda e, f: (e, f, 0))],
        out_specs=pl.BlockSpec((b, cap, D_MODEL), lambda e, f: (0, e, 0)),
        out_shape=jax.ShapeDtypeStruct((b, n_e * cap, D_MODEL), BF16),
        scratch_shapes=[pltpu.VMEM((b, cap, D_MODEL), F32)],
        compiler_params=_params("arbitrary", "arbitrary"),
        name="ffn",
    )(xe, gate, wg, wu, wd)


def _combine_kernel(start_ref, slot_t_ref, start_t_ref, ye_ref, x1_ref, mod_ref, fg_ref, o_ref, acc_s,
                    *, capacity, window):
    b = pl.program_id(0)
    j = pl.program_id(1)
    n_sub = o_ref.shape[1] // 128
    k_total = N_EXPERTS * window
    lane = _iota2((1, 128), 1).astype(F32)
    for sb in range(n_sub):
        blk = j * n_sub + sb
        slot_t = slot_t_ref[0, sb * 128:(sb + 1) * 128, :]
        k_pos = jnp.where(slot_t >= 0.0, slot_t - start_t_ref[0, pl.ds(blk, 1), :] + lane * float(window), -1.0)
        cols = []
        for c in range(k_total // 128):
            k_lane = lane + float(128 * c)
            hit = jnp.zeros((128, 128), F32)
            for e in range((128 * c) // window, (128 * c + 127) // window + 1):
                hit = jnp.where(k_pos[:, e:e + 1] == k_lane, 1.0, hit)
            cols.append(hit.astype(BF16))
        onehot = jnp.concatenate(cols, axis=1)
        rows = [ye_ref[0, pl.ds(pl.multiple_of(e * capacity + start_ref[b, blk, e], 16), window), :]
                for e in range(N_EXPERTS)]
        acc_s[sb * 128:(sb + 1) * 128, :] = _dot(onehot, jnp.concatenate(rows, axis=0))
    g2 = mod_ref[pl.ds(b, 1), 5 * D_MODEL:6 * D_MODEL]
    x2 = x1_ref[0] + g2 * acc_s[...]
    o_ref[0] = x2 * lax.rsqrt(jnp.mean(x2 * x2, axis=-1, keepdims=True) + EPS) * fg_ref[...]


def _combine(start, slot_t, start_t, ye, x1, mod, fg, capacity, window, tm):
    b, n, _ = x1.shape
    assert (N_EXPERTS * window) % 128 == 0 and window % 16 == 0 and capacity % 16 == 0
    tok = pl.BlockSpec((1, tm, D_MODEL), lambda i, j, s: (i, j, 0))
    return pl.pallas_call(
        functools.partial(_combine_kernel, capacity=capacity, window=window),
        grid_spec=pltpu.PrefetchScalarGridSpec(
            num_scalar_prefetch=1,
            grid=(b, n // tm),
            in_specs=[pl.BlockSpec((1, tm, 128), lambda i, j, s: (i, j, 0)),
                      pl.BlockSpec((1, n // 128, 128), lambda i, j, s: (i, 0, 0)),
                      pl.BlockSpec((1, N_EXPERTS * capacity, D_MODEL), lambda i, j, s: (i, 0, 0)),
                      tok,
                      pl.BlockSpec(mod.shape, lambda i, j, s: (0, 0)),
                      pl.BlockSpec((1, D_MODEL), lambda i, j, s: (0, 0))],
            out_specs=tok,
            scratch_shapes=[pltpu.VMEM((tm, D_MODEL), F32)]),
        out_shape=jax.ShapeDtypeStruct((b, n, D_MODEL), F32),
        compiler_params=_params("arbitrary", "arbitrary"),
        name="combine",
    )(start, slot_t, start_t, ye, x1, mod, fg)


def kernel(x, c, ctx, c_ctx, w_mod, b_mod, norm_mix_g, w_in, conv_q, conv_k, b_igate, b_fgate,
           gmlp_ws, gmlp_bs, mlstm_norm_g, w_out, norm_ffn_g, w_router, w_gate_e, w_up_e,
           w_down_e, final_g):
    depth = w_mod.shape[0]
    assert depth == 1, "the context stream is only carried as mLSTM states (single layer)"
    batch, seq, _ = x.shape
    assert seq % GRID_W == 0 and seq % CHUNK == 0 and batch + 1 <= MOD_ROWS
    capacity = EC_FACTOR * seq // N_EXPERTS
    ctx_row = batch
    l = 0

    cond = jnp.concatenate([c, c_ctx[None], jnp.zeros((MOD_ROWS - batch - 1, D_MODEL), F32)], axis=0)
    mod = _adaln(cond, w_mod[l], b_mod[l][None])

    row = lambda a: a[None]
    w_main = w_in[l][:, :MAIN_W].astype(BF16)
    w_gates = jnp.pad(w_in[l][:, MAIN_W:], ((0, 0), (0, GATE_PAD - N_GATES)))
    gate_bias = jnp.pad(jnp.concatenate([b_igate[l].reshape(-1), b_fgate[l].reshape(-1)]),
                        (0, GATE_PAD - N_GATES))[None]

    p_c, gates_c = _inproj(ctx, mod, row(norm_mix_g[l]), w_main[:, K_BLK * GROUP_W:], w_gates,
                           tm=ctx.shape[1], ctx_row=ctx_row)
    k_scale = HEAD_DIM ** -0.5
    (k_c,) = _dwconv(p_c, [(0, conv_k[l], k_scale)], tm=ctx.shape[1])
    state = _mlstm(None, k_c, p_c, 1, gates_c, gate_bias, None)

    p, gates = _inproj(x, mod, row(norm_mix_g[l]), w_main, w_gates, tm=512, ctx_row=None)
    q_l, k_l = _dwconv(p, [(Q_BLK, conv_q[l], 1.0), (K_BLK, conv_k[l], k_scale)], tm=512)
    h_f, h_b = _mlstm(q_l, k_l, p, V_BLK, gates, gate_bias, state)

    x1, h2, aff_t = _postmix(p, h_f, h_b, x, mod, gmlp_ws[l].astype(BF16), gmlp_bs[l][:, :, None],
                             row(mlstm_norm_g[l]), w_out[l].astype(BF16), row(norm_ffn_g[l]),
                             jnp.pad(w_router[l], ((0, 0), (0, 128 - N_EXPERTS))), tm=512)

    window = CHUNK + 16
    slot, gate_w, slot_t, start_t = _route(aff_t, capacity, window)
    slot = slot[:, :, None, :]
    gate_w = gate_w[:, :, None, :]

    xe, gate = _gather(h2, slot, gate_w, capacity, tok_blk=1024)
    ye = _ffn(xe, gate, w_gate_e[l], w_up_e[l], w_down_e[l], f_tile=512)
    start = start_t[:, :, :N_EXPERTS].astype(jnp.int32)
    return _combine(start, slot_t, start_t, ye, x1, mod, row(final_g), capacity, window, tm=512)
```

```python
import functools

import jax
import jax.numpy as jnp
from jax import lax
from jax.experimental import pallas as pl
from jax.experimental.pallas import tpu as pltpu

F32 = jnp.float32
BF16 = jnp.bfloat16

D_MODEL = 1024
GRID_W = 64
CHUNK = 128
HEADS = 4
GROUP_W = D_MODEL // 2
HEAD_DIM = GROUP_W // HEADS
N_EXPERTS = 16
EC_FACTOR = 2
D_EXPERT = 2 * D_MODEL
EPS = 1e-6
N_GATES = 4 * HEADS
GATE_PAD = 128
MOD_ROWS = 8

U_BLK, VG_BLK, Q_BLK, O_BLK, K_BLK, V_BLK = 0, 1, 2, 3, 4, 5
MAIN_W = 6 * GROUP_W

VMEM_LIMIT = 56 * 1024 * 1024

def _params(*sem):
    return pltpu.CompilerParams(dimension_semantics=sem, vmem_limit_bytes=VMEM_LIMIT)


def _dot(a, b):
    return jnp.dot(a, b, preferred_element_type=F32)


def _dot_nt(a, b):
    return lax.dot_general(a, b, (((1,), (1,)), ((), ())), preferred_element_type=F32)


def _split2(a):
    hi = a.astype(BF16)
    lo = (a - hi.astype(F32)).astype(BF16)
    return hi, lo


def _dot3(a, b):
    ah, al = _split2(a)
    bh, bl = _split2(b)
    return _dot(ah, bh) + (_dot(al, bh) + _dot(ah, bl))


def _dot_exact01(tri, x):
    x1 = x.astype(BF16)
    r1 = x - x1.astype(F32)
    x2 = r1.astype(BF16)
    x3 = (r1 - x2.astype(F32)).astype(BF16)
    return _dot(tri, x1) + (_dot(tri, x2) + _dot(tri, x3))


def _iota2(shape, dim):
    return lax.broadcasted_iota(jnp.int32, shape, dim)


def _row_to_col(row):
    n = row.shape[1] // 128
    eye = _iota2((128, 128), 0) == _iota2((128, 128), 1)
    cols = [jnp.sum(jnp.where(eye, row[:, j * 128:(j + 1) * 128], 0.0), axis=1, keepdims=True)
            for j in range(n)]
    return cols[0] if n == 1 else jnp.concatenate(cols, axis=0)


def _adaln_kernel(cond_ref, w_ref, b_ref, o_ref):
    o_ref[...] = _dot3(jax.nn.silu(cond_ref[...]), w_ref[...]) + b_ref[...]


def _adaln(cond, w, b):
    n_out = w.shape[1]
    tn = D_MODEL
    return pl.pallas_call(
        _adaln_kernel,
        grid=(n_out // tn,),
        in_specs=[pl.BlockSpec((MOD_ROWS, D_MODEL), lambda j: (0, 0)),
                  pl.BlockSpec((D_MODEL, tn), lambda j: (0, j)),
                  pl.BlockSpec((1, tn), lambda j: (0, j))],
        out_specs=pl.BlockSpec((MOD_ROWS, tn), lambda j: (0, j)),
        out_shape=jax.ShapeDtypeStruct((MOD_ROWS, n_out), F32),
        compiler_params=_params("arbitrary"),
        name="adaln",
    )(cond, w, b)


def _inproj_kernel(x_ref, mod_ref, g_ref, w_ref, wg_ref, p_ref, gate_ref, *, ctx_row):
    row = pl.program_id(0) if ctx_row is None else ctx_row
    x = x_ref[0]
    y = x * lax.rsqrt(jnp.mean(x * x, axis=-1, keepdims=True) + EPS) * g_ref[...]
    sh = mod_ref[pl.ds(row, 1), 0:D_MODEL]
    sc = mod_ref[pl.ds(row, 1), D_MODEL:2 * D_MODEL]
    h = y * (1.0 + sc) + sh
    p_ref[0] = _dot(h.astype(BF16), w_ref[...])
    gate_ref[0] = _dot3(h, wg_ref[...])


def _inproj(x, mod, g, w, wg, tm, ctx_row):
    b, n, _ = x.shape
    wn = w.shape[1]
    return pl.pallas_call(
        functools.partial(_inproj_kernel, ctx_row=ctx_row),
        grid=(b, n // tm),
        in_specs=[pl.BlockSpec((1, tm, D_MODEL), lambda i, j: (i, j, 0)),
                  pl.BlockSpec(mod.shape, lambda i, j: (0, 0)),
                  pl.BlockSpec((1, D_MODEL), lambda i, j: (0, 0)),
                  pl.BlockSpec((D_MODEL, wn), lambda i, j: (0, 0)),
                  pl.BlockSpec((D_MODEL, GATE_PAD), lambda i, j: (0, 0))],
        out_specs=[pl.BlockSpec((1, tm, wn), lambda i, j: (i, j, 0)),
                   pl.BlockSpec((1, tm, GATE_PAD), lambda i, j: (i, j, 0))],
        out_shape=[jax.ShapeDtypeStruct((b, n, wn), F32),
                   jax.ShapeDtypeStruct((b, n, GATE_PAD), F32)],
        compiler_params=_params("arbitrary", "arbitrary"),
        name="inproj",
    )(x, mod, g, w, wg)


def _dwconv_kernel(*refs, n_tiles, scales):
    n_in = len(scales)
    j = pl.program_id(1)
    for s, scale in enumerate(scales):
        cur_ref, prev_ref, next_ref, w_ref = refs[4 * s:4 * s + 4]
        o_ref = refs[4 * n_in + s]
        cur = cur_ref[0]
        rows = cur.shape[0]
        prev_row = prev_ref[0, 7:8, :] * (j > 0).astype(F32)
        next_row = next_ref[0, 0:1, :] * (j < n_tiles - 1).astype(F32)
        t = _iota2((rows, 1), 0)
        before = jnp.where(t == 0, prev_row, pltpu.roll(cur, 1, axis=0))
        after = jnp.where(t == rows - 1, next_row, pltpu.roll(cur, rows - 1, axis=0))
        y = before * w_ref[0:1, :] + cur * w_ref[1:2, :] + after * w_ref[2:3, :]
        o_ref[0] = (jax.nn.silu(y) * scale).astype(BF16)


def _dwconv(p, streams, tm):
    b, n, _ = p.shape
    n_tiles = n // tm
    rows8 = tm // 8
    last8 = n // 8 - 1
    in_specs, args = [], []
    for blk, w, _ in streams:
        in_specs += [pl.BlockSpec((1, tm, GROUP_W), lambda i, j, blk=blk: (i, j, blk)),
                     pl.BlockSpec((1, 8, GROUP_W),
                                  lambda i, j, blk=blk: (i, jnp.maximum(j * rows8 - 1, 0), blk)),
                     pl.BlockSpec((1, 8, GROUP_W),
                                  lambda i, j, blk=blk: (i, jnp.minimum((j + 1) * rows8, last8), blk)),
                     pl.BlockSpec((3, GROUP_W), lambda i, j: (0, 0))]
        args += [p, p, p, w]
    return pl.pallas_call(
        functools.partial(_dwconv_kernel, n_tiles=n_tiles, scales=tuple(s for _, _, s in streams)),
        grid=(b, n_tiles),
        in_specs=in_specs,
        out_specs=[pl.BlockSpec((1, tm, GROUP_W), lambda i, j: (i, j, 0))] * len(streams),
        out_shape=[jax.ShapeDtypeStruct((b, n, GROUP_W), BF16)] * len(streams),
        compiler_params=_params("arbitrary", "arbitrary"),
        name="dwconv",
    )(*args)


def _mlstm_chunk(per_dir, bias_ref, c_s, n_s, m_s, h_refs):
    with_h = h_refs[0] is not None
    i0 = _iota2((CHUNK, CHUNK), 0)
    i1 = _iota2((CHUNK, CHUNK), 1)

    units = []
    for d, (q_ref, k_ref, v_ref, g_ref) in enumerate(per_dir):
        sees_ts = (i1 <= i0) if d == 0 else (i1 >= i0)
        sees_st = (i0 <= i1) if d == 0 else (i0 >= i1)
        tri = jnp.where(sees_ts, 1.0, 0.0).astype(BF16)
        gates = g_ref[0] + bias_ref[...]
        bcum = _dot_exact01(tri, jax.nn.log_sigmoid(gates))
        for hd in range(HEADS):
            lanes = slice(hd * HEAD_DIM, (hd + 1) * HEAD_DIM)
            c_prev = c_s[d, hd]
            n_prev = n_s[d, hd]
            k = k_ref[0, :, lanes]
            st = dict(d=d, hd=hd, lanes=lanes, c_prev=c_prev, n_prev=n_prev, k=k, sees_st=sees_st,
                      gates=gates, bcum=bcum, v=v_ref[0, :, lanes])
            if with_h:
                lhs = jnp.concatenate([k, c_prev.astype(BF16),
                                       jnp.broadcast_to(n_prev, (16, HEAD_DIM)).astype(BF16)], axis=0)
                st["prod"] = _dot_nt(lhs, q_ref[0, :, lanes])
            units.append(st)

    rows = {}
    for st in units:
        d, hd = st["d"], st["hd"]
        if d not in rows:
            rows[d] = (st["gates"].T, st["bcum"].T)
        gates_t, bcum_t = rows[d]
        ci = d * HEADS + hd
        cf = 2 * HEADS + d * HEADS + hd
        last = CHUNK - 1 if d == 0 else 0
        li_row = gates_t[ci:ci + 1, :]
        bc_row = bcum_t[cf:cf + 1, :]
        b_last = bc_row[:, last:last + 1]
        m_prev = m_s[d, hd]
        v_t = st["v"].T
        a_row = b_last - bc_row + li_row
        m_new = jnp.maximum(b_last + m_prev, jnp.max(a_row, axis=-1, keepdims=True))
        w_row = jnp.exp(a_row - m_new)
        lhs = jnp.concatenate([v_t * w_row, jnp.broadcast_to(w_row, (16, CHUNK))], axis=0)
        st["upd"] = _dot(lhs.astype(BF16), st["k"])
        st.update(bc_row=bc_row, m_prev=m_prev, m_new=m_new, v_t=v_t,
                  decay=jnp.exp(b_last + m_prev - m_new),
                  u_col=st["gates"][:, ci:ci + 1] - st["bcum"][:, cf:cf + 1])

    if with_h:
        for st in units:
            prod, bc_row = st["prod"], st["bc_row"]
            g = bc_row + st["m_prev"]
            dmat = jnp.where(st["sees_st"], st["u_col"] + bc_row, -jnp.inf)
            m_t = jnp.maximum(g, jnp.max(dmat, axis=0, keepdims=True))
            inter = jnp.exp(g - m_t)
            s = prod[0:CHUNK] * jnp.exp(dmat - m_t)
            st["pv"] = _dot(st["v_t"].astype(BF16), s.astype(BF16))
            st["num0"] = inter * prod[CHUNK:2 * CHUNK]
            den = inter * prod[2 * CHUNK:2 * CHUNK + 1] + jnp.sum(s, axis=0, keepdims=True)
            st["scale"] = 1.0 / jnp.maximum(jnp.abs(den), jnp.exp(-m_t))

    for st in units:
        d, hd = st["d"], st["hd"]
        if with_h:
            h_refs[d][0, :, st["lanes"]] = ((st["num0"] + st["pv"]) * st["scale"]).T
        c_s[d, hd] = st["decay"] * st["c_prev"] + st["upd"][0:HEAD_DIM]
        n_s[d, hd] = st["decay"] * st["n_prev"] + st["upd"][HEAD_DIM:HEAD_DIM + 1]
        m_s[d, hd] = st["m_new"]


def _mlstm_kernel(*refs, n_chunks, with_h):
    refs = list(refs)
    take = lambda n: [refs.pop(0) for _ in range(n)]
    per_dir = []
    for _ in range(2):
        q_ref = take(1)[0] if with_h else None
        k_ref, v_ref, g_ref = take(3)
        per_dir.append((q_ref, k_ref, v_ref, g_ref))
    bias_ref = take(1)[0]
    if with_h:
        c0_ref, n0_ref, m0_ref = take(3)
        h_refs = take(2)
        c_out = n_out = m_out = None
    else:
        h_refs = [None, None]
        c_out, n_out, m_out = take(3)
    c_s, n_s, m_s = take(3)

    j = pl.program_id(1)

    @pl.when(j == 0)
    def _():
        if with_h:
            c_s[...] = c0_ref[0]
            n_s[...] = n0_ref[0]
            m_s[...] = m0_ref[0]
        else:
            c_s[...] = jnp.zeros_like(c_s)
            n_s[...] = jnp.zeros_like(n_s)
            m_s[...] = jnp.zeros_like(m_s)

    _mlstm_chunk(per_dir, bias_ref, c_s, n_s, m_s, h_refs)

    if not with_h:
        @pl.when(j == n_chunks - 1)
        def _():
            c_out[0] = c_s[...]
            n_out[0] = n_s[...]
            m_out[0] = m_s[...]


def _mlstm(q, k, p, v_blk, gates, bias, state):
    b, n, _ = k.shape
    n_chunks = n // CHUNK
    with_h = q is not None

    in_specs, args = [], []
    for d in range(2):
        c = (lambda j: j) if d == 0 else (lambda j: n_chunks - 1 - j)
        tok = pl.BlockSpec((1, CHUNK, GROUP_W), lambda i, j, c=c: (i, c(j), 0))
        if with_h:
            in_specs.append(tok)
            args.append(q)
        in_specs += [tok,
                     pl.BlockSpec((1, CHUNK, GROUP_W), lambda i, j, c=c: (i, c(j), v_blk)),
                     pl.BlockSpec((1, CHUNK, GATE_PAD), lambda i, j, c=c: (i, c(j), 0))]
        args += [k, p, gates]
    in_specs.append(pl.BlockSpec((1, GATE_PAD), lambda i, j: (0, 0)))
    args.append(bias)

    c_shape = (2, HEADS, HEAD_DIM, HEAD_DIM)
    v_shape = (2, HEADS, 1, HEAD_DIM)
    c_spec = pl.BlockSpec((1,) + c_shape, lambda i, j: (i, 0, 0, 0, 0))
    v_spec = pl.BlockSpec((1,) + v_shape, lambda i, j: (i, 0, 0, 0, 0))
    if with_h:
        in_specs += [c_spec, v_spec, v_spec]
        args += list(state)
        out_specs = [pl.BlockSpec((1, CHUNK, GROUP_W), lambda i, j: (i, j, 0)),
                     pl.BlockSpec((1, CHUNK, GROUP_W), lambda i, j: (i, n_chunks - 1 - j, 0))]
        out_shape = [jax.ShapeDtypeStruct((b, n, GROUP_W), F32)] * 2
    else:
        out_specs = [c_spec, v_spec, v_spec]
        out_shape = [jax.ShapeDtypeStruct((b,) + c_shape, F32),
                     jax.ShapeDtypeStruct((b,) + v_shape, F32),
                     jax.ShapeDtypeStruct((b,) + v_shape, F32)]
    return pl.pallas_call(
        functools.partial(_mlstm_kernel, n_chunks=n_chunks, with_h=with_h),
        grid=(b, n_chunks),
        in_specs=in_specs,
        out_specs=out_specs,
        out_shape=out_shape,
        scratch_shapes=[pltpu.VMEM(c_shape, F32), pltpu.VMEM(v_shape, F32), pltpu.VMEM(v_shape, F32)],
        compiler_params=_params("arbitrary", "arbitrary"),
        name="mlstm" if with_h else "mlstm_ctx_state",
    )(*args)


def _layer_norm(x):
    mu = jnp.mean(x, axis=-1, keepdims=True)
    var = jnp.mean(jnp.square(x - mu), axis=-1, keepdims=True)
    return (x - mu) * lax.rsqrt(var + EPS)


def _postmix_kernel(u_ref, vg_ref, o_ref, hf_ref, hb_ref, x_ref, mod_ref, ws_ref, bs_ref, ng_ref,
                    wout_ref, fg_ref, wr_ref, x1_ref, h2_ref, aff_ref, ycat_s):
    tm = x_ref.shape[1]
    b = pl.program_id(0)
    mod = lambda k: mod_ref[pl.ds(b, 1), k * D_MODEL:(k + 1) * D_MODEL]

    u = jax.nn.gelu(u_ref[0])
    v = _layer_norm(jax.nn.gelu(vg_ref[0])).astype(BF16)
    for c in range(tm // CHUNK):
        rows = slice(c * CHUNK, (c + 1) * CHUNK)
        for hd in range(HEADS):
            lanes = slice(hd * HEAD_DIM, (hd + 1) * HEAD_DIM)
            s = _dot(ws_ref[hd], v[rows, lanes]) + bs_ref[hd]
            ycat_s[rows, lanes] = (u[rows, lanes] * s).astype(BF16)

    hsum = hf_ref[0] + hb_ref[0]
    o = jax.nn.sigmoid(o_ref[0])
    for hd in range(HEADS):
        lanes = slice(hd * HEAD_DIM, (hd + 1) * HEAD_DIM)
        hn = _layer_norm(hsum[:, lanes]) * ng_ref[:, lanes]
        ycat_s[:, GROUP_W + hd * HEAD_DIM:GROUP_W + (hd + 1) * HEAD_DIM] = (o[:, lanes] * hn).astype(BF16)

    y = _dot(ycat_s[...], wout_ref[...])
    x1 = x_ref[0] + mod(2) * y
    x1_ref[0] = x1

    n2 = x1 * lax.rsqrt(jnp.mean(x1 * x1, axis=-1, keepdims=True) + EPS) * fg_ref[...]
    h2 = n2 * (1.0 + mod(4)) + mod(3)
    h2_ref[0] = h2.astype(BF16)

    logits = _dot3(h2, wr_ref[...])
    logits = jnp.where(_iota2(logits.shape, 1) < N_EXPERTS, logits, -jnp.inf)
    e = jnp.exp(logits - jnp.max(logits, axis=-1, keepdims=True))
    aff = e / jnp.sum(e, axis=-1, keepdims=True)
    aff_ref[0] = aff.T[0:N_EXPERTS, :]


def _postmix(p, hf, hb, x, mod, ws, bs, ng, wout, fg, wr, tm):
    b, n, _ = x.shape
    tok = lambda blk: pl.BlockSpec((1, tm, GROUP_W), lambda i, j: (i, j, blk))
    full = lambda a: pl.BlockSpec(a.shape, lambda i, j: (0,) * a.ndim)
    return pl.pallas_call(
        _postmix_kernel,
        grid=(b, n // tm),
        in_specs=[tok(U_BLK), tok(VG_BLK), tok(O_BLK), tok(0), tok(0),
                  pl.BlockSpec((1, tm, D_MODEL), lambda i, j: (i, j, 0)),
                  full(mod), full(ws), full(bs), full(ng), full(wout), full(fg), full(wr)],
        out_specs=[pl.BlockSpec((1, tm, D_MODEL), lambda i, j: (i, j, 0)),
                   pl.BlockSpec((1, tm, D_MODEL), lambda i, j: (i, j, 0)),
                   pl.BlockSpec((1, N_EXPERTS, tm), lambda i, j: (i, 0, j))],
        out_shape=[jax.ShapeDtypeStruct((b, n, D_MODEL), F32),
                   jax.ShapeDtypeStruct((b, n, D_MODEL), BF16),
                   jax.ShapeDtypeStruct((b, N_EXPERTS, n), F32)],
        scratch_shapes=[pltpu.VMEM((tm, D_MODEL), BF16)],
        compiler_params=_params("arbitrary", "arbitrary"),
        name="postmix",
    )(p, p, p, hf, hb, x, mod, ws, bs, ng, wout, fg, wr)


def _cumsum_lanes(x, upper):
    carry = jnp.zeros((x.shape[0], 1), F32)
    outs, before = [], []
    for j in range(x.shape[1] // 128):
        before.append(carry)
        c = _dot(x[:, j * 128:(j + 1) * 128].astype(BF16), upper) + carry
        outs.append(c)
        carry = c[:, 127:128]
    return jnp.concatenate(outs, axis=1), before


def _rows_to_lanes(x, fill):
    pad = jnp.full((128 - x.shape[0], 128), fill, F32)
    return jnp.concatenate([x, pad], axis=0).T


def _route_kernel(aff_ref, slot_ref, w_ref, slot_t_ref, start_t_ref, *, capacity, window):
    aff = aff_ref[0]
    cap = float(capacity)
    thr_bits = jnp.zeros((aff.shape[0], 1), jnp.int32)
    for bit in range(30, -1, -1):
        cand = thr_bits | (1 << bit)
        cnt = jnp.sum(jnp.where(aff >= pltpu.bitcast(cand, F32), 1.0, 0.0), axis=-1, keepdims=True)
        thr_bits = jnp.where(cnt >= cap, cand, thr_bits)
    thr = pltpu.bitcast(thr_bits, F32)
    upper = jnp.where(_iota2((128, 128), 0) <= _iota2((128, 128), 1), 1.0, 0.0).astype(BF16)
    above = jnp.where(aff > thr, 1.0, 0.0)
    tied = jnp.where(aff == thr, 1.0, 0.0)
    need = cap - jnp.sum(above, axis=-1, keepdims=True)
    sel = above + tied * jnp.where(_cumsum_lanes(tied, upper)[0] <= need, 1.0, 0.0)
    count, before = _cumsum_lanes(sel, upper)
    slot = jnp.where(sel > 0.0, count - 1.0, -1.0)
    slot_ref[0] = slot
    w_ref[0] = jnp.where(sel > 0.0, aff, 0.0)

    n_blocks = len(before)
    for j in range(n_blocks):
        slot_t_ref[0, j * 128:(j + 1) * 128, :] = _rows_to_lanes(slot[:, j * 128:(j + 1) * 128], -1.0)
    lane = _iota2((1, 128), 1)
    first = jnp.zeros((aff.shape[0], 128), F32)
    for j in range(n_blocks):
        first = jnp.where(lane == j, before[j], first)
    start = jnp.minimum(jnp.floor(first * (1.0 / 16.0)) * 16.0, float(capacity - window))
    start_t_ref[0] = _rows_to_lanes(start, 0.0)[0:n_blocks, :]


def _route(aff_t, capacity, window):
    b, e, n = aff_t.shape
    assert n // 128 <= 128 and e <= 128 and window <= capacity
    spec = pl.BlockSpec((1, e, n), lambda i: (i, 0, 0))
    return pl.pallas_call(
        functools.partial(_route_kernel, capacity=capacity, window=window),
        grid=(b,),
        in_specs=[spec],
        out_specs=[spec, spec,
                   pl.BlockSpec((1, n, 128), lambda i: (i, 0, 0)),
                   pl.BlockSpec((1, n // 128, 128), lambda i: (i, 0, 0))],
        out_shape=[jax.ShapeDtypeStruct((b, e, n), F32)] * 2
        + [jax.ShapeDtypeStruct((b, n, 128), F32), jax.ShapeDtypeStruct((b, n // 128, 128), F32)],
        compiler_params=_params("arbitrary"),
        name="route",
    )(aff_t)


def _gather_kernel(h2_ref, slot_ref, w_ref, xe_ref, gate_ref, *, tok_blk):
    cap = xe_ref.shape[2]
    n = h2_ref.shape[1]
    slot_id = _iota2((cap, 1), 0).astype(F32)
    xe = jnp.zeros((cap, D_MODEL), F32)
    gate = jnp.zeros((cap, 1), F32)
    for j in range(n // tok_blk):
        toks = slice(j * tok_blk, (j + 1) * tok_blk)
        hit = slot_ref[0, 0, :, toks] == slot_id
        xe = xe + _dot(jnp.where(hit, 1.0, 0.0).astype(BF16), h2_ref[0, toks, :])
        gate = gate + jnp.sum(jnp.where(hit, w_ref[0, 0, :, toks], 0.0), axis=1, keepdims=True)
    xe_ref[0, 0] = xe.astype(BF16)
    gate_ref[0, 0] = gate


def _gather(h2, slot, w, capacity, tok_blk):
    b, n, _ = h2.shape
    n_e = slot.shape[1]
    row = pl.BlockSpec((1, 1, 1, n), lambda i, e: (i, e, 0, 0))
    return pl.pallas_call(
        functools.partial(_gather_kernel, tok_blk=tok_blk),
        grid=(b, n_e),
        in_specs=[pl.BlockSpec((1, n, D_MODEL), lambda i, e: (i, 0, 0)), row, row],
        out_specs=[pl.BlockSpec((1, 1, capacity, D_MODEL), lambda i, e: (i, e, 0, 0)),
                   pl.BlockSpec((1, 1, capacity, 1), lambda i, e: (i, e, 0, 0))],
        out_shape=[jax.ShapeDtypeStruct((b, n_e, capacity, D_MODEL), BF16),
                   jax.ShapeDtypeStruct((b, n_e, capacity, 1), F32)],
        compiler_params=_params("arbitrary", "arbitrary"),
        name="gather",
    )(h2, slot, w)


def _ffn_kernel(xe_ref, gate_ref, wg_ref, wu_ref, wd_ref, ye_ref, acc_s):
    ft = pl.program_id(1)

    @pl.when(ft == 0)
    def _():
        acc_s[...] = jnp.zeros_like(acc_s)

    wg = wg_ref[0].astype(BF16)
    wu = wu_ref[0].astype(BF16)
    wd = wd_ref[0].astype(BF16)
    for i in range(xe_ref.shape[0]):
        xe = xe_ref[i, 0]
        act = jax.nn.silu(_dot(xe, wg)) * _dot(xe, wu)
        acc_s[i] += _dot(act.astype(BF16), wd)

    @pl.when(ft == pl.num_programs(1) - 1)
    def _():
        for i in range(xe_ref.shape[0]):
            ye_ref[i] = (acc_s[i] * gate_ref[i, 0]).astype(BF16)


def _ffn(xe, gate, wg, wu, wd, f_tile):
    b, n_e, cap, _ = xe.shape
    return pl.pallas_call(
        _ffn_kernel,
        grid=(n_e, D_EXPERT // f_tile),
        in_specs=[pl.BlockSpec((b, 1, cap, D_MODEL), lambda e, f: (0, e, 0, 0)),
                  pl.BlockSpec((b, 1, cap, 1), lambda e, f: (0, e, 0, 0)),
                  pl.BlockSpec((1, D_MODEL, f_tile), lambda e, f: (e, 0, f)),
                  pl.BlockSpec((1, D_MODEL, f_tile), lambda e, f: (e, 0, f)),
                  pl.BlockSpec((1, f_tile, D_MODEL), lambda e, f: (e, f, 0))],
        out_specs=pl.BlockSpec((b, cap, D_MODEL), lambda e, f: (0, e, 0)),
        out_shape=jax.ShapeDtypeStruct((b, n_e * cap, D_MODEL), BF16),
        scratch_shapes=[pltpu.VMEM((b, cap, D_MODEL), F32)],
        compiler_params=_params("arbitrary", "arbitrary"),
        name="ffn",
    )(xe, gate, wg, wu, wd)


def _combine_kernel(start_ref, slot_t_ref, start_t_ref, ye_ref, x1_ref, mod_ref, fg_ref, o_ref, acc_s,
                    *, capacity, window):
    b = pl.program_id(0)
    j = pl.program_id(1)
    n_sub = o_ref.shape[1] // 128
    k_total = N_EXPERTS * window
    lane = _iota2((1, 128), 1).astype(F32)
    for sb in range(n_sub):
        blk = j * n_sub + sb
        slot_t = slot_t_ref[0, sb * 128:(sb + 1) * 128, :]
        k_pos = jnp.where(slot_t >= 0.0, slot_t - start_t_ref[0, pl.ds(blk, 1), :] + lane * float(window), -1.0)
        cols = []
        for c in range(k_total // 128):
            k_lane = lane + float(128 * c)
            hit = jnp.zeros((128, 128), F32)
            for e in range((128 * c) // window, (128 * c + 127) // window + 1):
                hit = jnp.where(k_pos[:, e:e + 1] == k_lane, 1.0, hit)
            cols.append(hit.astype(BF16))
        onehot = jnp.concatenate(cols, axis=1)
        rows = [ye_ref[0, pl.ds(pl.multiple_of(e * capacity + start_ref[b, blk, e], 16), window), :]
                for e in range(N_EXPERTS)]
        acc_s[sb * 128:(sb + 1) * 128, :] = _dot(onehot, jnp.concatenate(rows, axis=0))
    g2 = mod_ref[pl.ds(b, 1), 5 * D_MODEL:6 * D_MODEL]
    x2 = x1_ref[0] + g2 * acc_s[...]
    o_ref[0] = x2 * lax.rsqrt(jnp.mean(x2 * x2, axis=-1, keepdims=True) + EPS) * fg_ref[...]


def _combine(start, slot_t, start_t, ye, x1, mod, fg, capacity, window, tm):
    b, n, _ = x1.shape
    assert (N_EXPERTS * window) % 128 == 0 and window % 16 == 0 and capacity % 16 == 0
    tok = pl.BlockSpec((1, tm, D_MODEL), lambda i, j, s: (i, j, 0))
    return pl.pallas_call(
        functools.partial(_combine_kernel, capacity=capacity, window=window),
        grid_spec=pltpu.PrefetchScalarGridSpec(
            num_scalar_prefetch=1,
            grid=(b, n // tm),
            in_specs=[pl.BlockSpec((1, tm, 128), lambda i, j, s: (i, j, 0)),
                      pl.BlockSpec((1, n // 128, 128), lambda i, j, s: (i, 0, 0)),
                      pl.BlockSpec((1, N_EXPERTS * capacity, D_MODEL), lambda i, j, s: (i, 0, 0)),
                      tok,
                      pl.BlockSpec(mod.shape, lambda i, j, s: (0, 0)),
                      pl.BlockSpec((1, D_MODEL), lambda i, j, s: (0, 0))],
            out_specs=tok,
            scratch_shapes=[pltpu.VMEM((tm, D_MODEL), F32)]),
        out_shape=jax.ShapeDtypeStruct((b, n, D_MODEL), F32),
        compiler_params=_params("arbitrary", "arbitrary"),
        name="combine",
    )(start, slot_t, start_t, ye, x1, mod, fg)


def kernel(x, c, ctx, c_ctx, w_mod, b_mod, norm_mix_g, w_in, conv_q, conv_k, b_igate, b_fgate,
           gmlp_ws, gmlp_bs, mlstm_norm_g, w_out, norm_ffn_g, w_router, w_gate_e, w_up_e,
           w_down_e, final_g):
    depth = w_mod.shape[0]
    assert depth == 1, "the context stream is only carried as mLSTM states (single layer)"
    batch, seq, _ = x.shape
    assert seq % GRID_W == 0 and seq % CHUNK == 0 and batch + 1 <= MOD_ROWS
    capacity = EC_FACTOR * seq // N_EXPERTS
    ctx_row = batch
    l = 0

    cond = jnp.concatenate([c, c_ctx[None], jnp.zeros((MOD_ROWS - batch - 1, D_MODEL), F32)], axis=0)
    mod = _adaln(cond, w_mod[l], b_mod[l][None])

    row = lambda a: a[None]
    w_main = w_in[l][:, :MAIN_W].astype(BF16)
    w_gates = jnp.pad(w_in[l][:, MAIN_W:], ((0, 0), (0, GATE_PAD - N_GATES)))
    gate_bias = jnp.pad(jnp.concatenate([b_igate[l].reshape(-1), b_fgate[l].reshape(-1)]),
                        (0, GATE_PAD - N_GATES))[None]

    p_c, gates_c = _inproj(ctx, mod, row(norm_mix_g[l]), w_main[:, K_BLK * GROUP_W:], w_gates,
                           tm=ctx.shape[1], ctx_row=ctx_row)
    k_scale = HEAD_DIM ** -0.5
    (k_c,) = _dwconv(p_c, [(0, conv_k[l], k_scale)], tm=ctx.shape[1])
    state = _mlstm(None, k_c, p_c, 1, gates_c, gate_bias, None)

    p, gates = _inproj(x, mod, row(norm_mix_g[l]), w_main, w_gates, tm=512, ctx_row=None)
    q_l, k_l = _dwconv(p, [(Q_BLK, conv_q[l], 1.0), (K_BLK, conv_k[l], k_scale)], tm=512)
    h_f, h_b = _mlstm(q_l, k_l, p, V_BLK, gates, gate_bias, state)

    x1, h2, aff_t = _postmix(p, h_f, h_b, x, mod, gmlp_ws[l].astype(BF16), gmlp_bs[l][:, :, None],
                             row(mlstm_norm_g[l]), w_out[l].astype(BF16), row(norm_ffn_g[l]),
                             jnp.pad(w_router[l], ((0, 0), (0, 128 - N_EXPERTS))), tm=512)

    window = CHUNK + 16
    slot, gate_w, slot_t, start_t = _route(aff_t, capacity, window)
    slot = slot[:, :, None, :]
    gate_w = gate_w[:, :, None, :]

    xe, gate = _gather(h2, slot, gate_w, capacity, tok_blk=1024)
    ye = _ffn(xe, gate, w_gate_e[l], w_up_e[l], w_down_e[l], f_tile=512)
    start = start_t[:, :, :N_EXPERTS].astype(jnp.int32)
    return _combine(start, slot_t, start_t, ye, x1, mod, row(final_g), capacity, window, tm=512)
```

```python
import functools

import jax
import jax.numpy as jnp
from jax import lax
from jax.experimental import pallas as pl
from jax.experimental.pallas import tpu as pltpu

F32 = jnp.float32
BF16 = jnp.bfloat16

D_MODEL = 1024
GRID_W = 64
CHUNK = 128
HEADS = 4
GROUP_W = D_MODEL // 2
HEAD_DIM = GROUP_W // HEADS
N_EXPERTS = 16
EC_FACTOR = 2
D_EXPERT = 2 * D_MODEL
EPS = 1e-6
N_GATES = 4 * HEADS
GATE_PAD = 128
MOD_ROWS = 8

U_BLK, VG_BLK, Q_BLK, O_BLK, K_BLK, V_BLK = 0, 1, 2, 3, 4, 5
MAIN_W = 6 * GROUP_W

VMEM_LIMIT = 56 * 1024 * 1024
ROW_GROUP = 2 * CHUNK

def _params(*sem):
    return pltpu.CompilerParams(dimension_semantics=sem, vmem_limit_bytes=VMEM_LIMIT)


def _dot(a, b):
    return jnp.dot(a, b, preferred_element_type=F32)


def _dot_nt(a, b):
    return lax.dot_general(a, b, (((1,), (1,)), ((), ())), preferred_element_type=F32)


def _split2(a):
    hi = a.astype(BF16)
    lo = (a - hi.astype(F32)).astype(BF16)
    return hi, lo


def _dot3(a, b):
    ah, al = _split2(a)
    bh, bl = _split2(b)
    return _dot(ah, bh) + (_dot(al, bh) + _dot(ah, bl))


def _dot_exact01(tri, x):
    x1 = x.astype(BF16)
    r1 = x - x1.astype(F32)
    x2 = r1.astype(BF16)
    x3 = (r1 - x2.astype(F32)).astype(BF16)
    return _dot(tri, x1) + (_dot(tri, x2) + _dot(tri, x3))


def _iota2(shape, dim):
    return lax.broadcasted_iota(jnp.int32, shape, dim)


def _row_to_col(row):
    n = row.shape[1] // 128
    eye = _iota2((128, 128), 0) == _iota2((128, 128), 1)
    cols = [jnp.sum(jnp.where(eye, row[:, j * 128:(j + 1) * 128], 0.0), axis=1, keepdims=True)
            for j in range(n)]
    return cols[0] if n == 1 else jnp.concatenate(cols, axis=0)


def _adaln_kernel(cond_ref, w_ref, b_ref, o_ref):
    o_ref[...] = _dot3(jax.nn.silu(cond_ref[...]), w_ref[...]) + b_ref[...]


def _adaln(cond, w, b):
    n_out = w.shape[1]
    tn = D_MODEL
    return pl.pallas_call(
        _adaln_kernel,
        grid=(n_out // tn,),
        in_specs=[pl.BlockSpec((MOD_ROWS, D_MODEL), lambda j: (0, 0)),
                  pl.BlockSpec((D_MODEL, tn), lambda j: (0, j)),
                  pl.BlockSpec((1, tn), lambda j: (0, j))],
        out_specs=pl.BlockSpec((MOD_ROWS, tn), lambda j: (0, j)),
        out_shape=jax.ShapeDtypeStruct((MOD_ROWS, n_out), F32),
        compiler_params=_params("arbitrary"),
        name="adaln",
    )(cond, w, b)


def _inproj_kernel(x_ref, mod_ref, g_ref, w_ref, wg_ref, p_ref, gate_ref, *, ctx_row):
    row = pl.program_id(0) if ctx_row is None else ctx_row
    sh = mod_ref[pl.ds(row, 1), 0:D_MODEL]
    sc = mod_ref[pl.ds(row, 1), D_MODEL:2 * D_MODEL]
    wg_hi, wg_lo = _split2(wg_ref[...])
    for r in range(x_ref.shape[1] // ROW_GROUP):
        rows = slice(r * ROW_GROUP, (r + 1) * ROW_GROUP)
        x = x_ref[0, rows, :]
        y = x * lax.rsqrt(jnp.mean(x * x, axis=-1, keepdims=True) + EPS) * g_ref[...]
        h = y * (1.0 + sc) + sh
        h_hi, h_lo = _split2(h)
        p_ref[0, rows, :] = _dot(h_hi, w_ref[...])
        gate_ref[0, rows, :] = _dot(h_hi, wg_hi) + (_dot(h_lo, wg_hi) + _dot(h_hi, wg_lo))


def _inproj(x, mod, g, w, wg, tm, ctx_row):
    b, n, _ = x.shape
    wn = w.shape[1]
    return pl.pallas_call(
        functools.partial(_inproj_kernel, ctx_row=ctx_row),
        grid=(b, n // tm),
        in_specs=[pl.BlockSpec((1, tm, D_MODEL), lambda i, j: (i, j, 0)),
                  pl.BlockSpec(mod.shape, lambda i, j: (0, 0)),
                  pl.BlockSpec((1, D_MODEL), lambda i, j: (0, 0)),
                  pl.BlockSpec((D_MODEL, wn), lambda i, j: (0, 0)),
                  pl.BlockSpec((D_MODEL, GATE_PAD), lambda i, j: (0, 0))],
        out_specs=[pl.BlockSpec((1, tm, wn), lambda i, j: (i, j, 0)),
                   pl.BlockSpec((1, tm, GATE_PAD), lambda i, j: (i, j, 0))],
        out_shape=[jax.ShapeDtypeStruct((b, n, wn), F32),
                   jax.ShapeDtypeStruct((b, n, GATE_PAD), F32)],
        compiler_params=_params("arbitrary", "arbitrary"),
        name="inproj",
    )(x, mod, g, w, wg)


def _dwconv_kernel(*refs, n_tiles, scales):
    n_in = len(scales)
    j = pl.program_id(1)
    for s, scale in enumerate(scales):
        cur_ref, prev_ref, next_ref, w_ref = refs[4 * s:4 * s + 4]
        o_ref = refs[4 * n_in + s]
        cur = cur_ref[0]
        rows = cur.shape[0]
        prev_row = prev_ref[0, 7:8, :] * (j > 0).astype(F32)
        next_row = next_ref[0, 0:1, :] * (j < n_tiles - 1).astype(F32)
        t = _iota2((rows, 1), 0)
        before = jnp.where(t == 0, prev_row, pltpu.roll(cur, 1, axis=0))
        after = jnp.where(t == rows - 1, next_row, pltpu.roll(cur, rows - 1, axis=0))
        y = before * w_ref[0:1, :] + cur * w_ref[1:2, :] + after * w_ref[2:3, :]
        o_ref[0] = (jax.nn.silu(y) * scale).astype(BF16)


def _dwconv(p, streams, tm):
    b, n, _ = p.shape
    n_tiles = n // tm
    rows8 = tm // 8
    last8 = n // 8 - 1
    in_specs, args = [], []
    for blk, w, _ in streams:
        in_specs += [pl.BlockSpec((1, tm, GROUP_W), lambda i, j, blk=blk: (i, j, blk)),
                     pl.BlockSpec((1, 8, GROUP_W),
                                  lambda i, j, blk=blk: (i, jnp.maximum(j * rows8 - 1, 0), blk)),
                     pl.BlockSpec((1, 8, GROUP_W),
                                  lambda i, j, blk=blk: (i, jnp.minimum((j + 1) * rows8, last8), blk)),
                     pl.BlockSpec((3, GROUP_W), lambda i, j: (0, 0))]
        args += [p, p, p, w]
    return pl.pallas_call(
        functools.partial(_dwconv_kernel, n_tiles=n_tiles, scales=tuple(s for _, _, s in streams)),
        grid=(b, n_tiles),
        in_specs=in_specs,
        out_specs=[pl.BlockSpec((1, tm, GROUP_W), lambda i, j: (i, j, 0))] * len(streams),
        out_shape=[jax.ShapeDtypeStruct((b, n, GROUP_W), BF16)] * len(streams),
        compiler_params=_params("arbitrary", "arbitrary"),
        name="dwconv",
    )(*args)


def _mlstm_chunk(per_dir, bias_ref, c_s, n_s, m_s, h_refs):
    with_h = h_refs[0] is not None
    i0 = _iota2((CHUNK, CHUNK), 0)
    i1 = _iota2((CHUNK, CHUNK), 1)

    units = []
    for d, (q_ref, k_ref, v_ref, g_ref) in enumerate(per_dir):
        sees_ts = (i1 <= i0) if d == 0 else (i1 >= i0)
        sees_st = (i0 <= i1) if d == 0 else (i0 >= i1)
        tri = jnp.where(sees_ts, 1.0, 0.0).astype(BF16)
        gates = g_ref[0] + bias_ref[...]
        bcum = _dot_exact01(tri, jax.nn.log_sigmoid(gates))
        for hd in range(HEADS):
            lanes = slice(hd * HEAD_DIM, (hd + 1) * HEAD_DIM)
            c_prev = c_s[d, hd]
            n_prev = n_s[d, hd]
            k = k_ref[0, :, lanes]
            st = dict(d=d, hd=hd, lanes=lanes, c_prev=c_prev, n_prev=n_prev, k=k, sees_st=sees_st,
                      gates=gates, bcum=bcum, v=v_ref[0, :, lanes])
            if with_h:
                lhs = jnp.concatenate([k, c_prev.astype(BF16),
                                       jnp.broadcast_to(n_prev, (16, HEAD_DIM)).astype(BF16)], axis=0)
                st["prod"] = _dot_nt(lhs, q_ref[0, :, lanes])
            units.append(st)

    rows = {}
    for st in units:
        d, hd = st["d"], st["hd"]
        if d not in rows:
            rows[d] = (st["gates"].T, st["bcum"].T)
        gates_t, bcum_t = rows[d]
        ci = d * HEADS + hd
        cf = 2 * HEADS + d * HEADS + hd
        last = CHUNK - 1 if d == 0 else 0
        li_row = gates_t[ci:ci + 1, :]
        bc_row = bcum_t[cf:cf + 1, :]
        b_last = bc_row[:, last:last + 1]
        m_prev = m_s[d, hd]
        v_t = st["v"].T
        a_row = b_last - bc_row + li_row
        m_new = jnp.maximum(b_last + m_prev, jnp.max(a_row, axis=-1, keepdims=True))
        w_row = jnp.exp(a_row - m_new)
        lhs = jnp.concatenate([v_t * w_row, jnp.broadcast_to(w_row, (16, CHUNK))], axis=0)
        st["upd"] = _dot(lhs.astype(BF16), st["k"])
        st.update(bc_row=bc_row, m_prev=m_prev, m_new=m_new, v_t=v_t,
                  decay=jnp.exp(b_last + m_prev - m_new),
                  u_col=st["gates"][:, ci:ci + 1] - st["bcum"][:, cf:cf + 1])

    if with_h:
        for st in units:
            prod, bc_row = st["prod"], st["bc_row"]
            g = bc_row + st["m_prev"]
            dmat = jnp.where(st["sees_st"], st["u_col"] + bc_row, -jnp.inf)
            m_t = jnp.maximum(g, jnp.max(dmat, axis=0, keepdims=True))
            inter = jnp.exp(g - m_t)
            s = prod[0:CHUNK] * jnp.exp(dmat - m_t)
            st["pv"] = _dot(st["v_t"].astype(BF16), s.astype(BF16))
            st["num0"] = inter * prod[CHUNK:2 * CHUNK]
            den = inter * prod[2 * CHUNK:2 * CHUNK + 1] + jnp.sum(s, axis=0, keepdims=True)
            st["scale"] = 1.0 / jnp.maximum(jnp.abs(den), jnp.exp(-m_t))

    for st in units:
        d, hd = st["d"], st["hd"]
        if with_h:
            h_refs[d][0, :, st["lanes"]] = ((st["num0"] + st["pv"]) * st["scale"]).T
        c_s[d, hd] = st["decay"] * st["c_prev"] + st["upd"][0:HEAD_DIM]
        n_s[d, hd] = st["decay"] * st["n_prev"] + st["upd"][HEAD_DIM:HEAD_DIM + 1]
        m_s[d, hd] = st["m_new"]


def _mlstm_kernel(*refs, n_chunks, with_h):
    refs = list(refs)
    take = lambda n: [refs.pop(0) for _ in range(n)]
    per_dir = []
    for _ in range(2):
        q_ref = take(1)[0] if with_h else None
        k_ref, v_ref, g_ref = take(3)
        per_dir.append((q_ref, k_ref, v_ref, g_ref))
    bias_ref = take(1)[0]
    if with_h:
        c0_ref, n0_ref, m0_ref = take(3)
        h_refs = take(2)
        c_out = n_out = m_out = None
    else:
        h_refs = [None, None]
        c_out, n_out, m_out = take(3)
    c_s, n_s, m_s = take(3)

    j = pl.program_id(1)

    @pl.when(j == 0)
    def _():
        if with_h:
            c_s[...] = c0_ref[0]
            n_s[...] = n0_ref[0]
            m_s[...] = m0_ref[0]
        else:
            c_s[...] = jnp.zeros_like(c_s)
            n_s[...] = jnp.zeros_like(n_s)
            m_s[...] = jnp.zeros_like(m_s)

    _mlstm_chunk(per_dir, bias_ref, c_s, n_s, m_s, h_refs)

    if not with_h:
        @pl.when(j == n_chunks - 1)
        def _():
            c_out[0] = c_s[...]
            n_out[0] = n_s[...]
            m_out[0] = m_s[...]


def _mlstm(q, k, p, v_blk, gates, bias, state):
    b, n, _ = k.shape
    n_chunks = n // CHUNK
    with_h = q is not None

    in_specs, args = [], []
    for d in range(2):
        c = (lambda j: j) if d == 0 else (lambda j: n_chunks - 1 - j)
        tok = pl.BlockSpec((1, CHUNK, GROUP_W), lambda i, j, c=c: (i, c(j), 0))
        if with_h:
            in_specs.append(tok)
            args.append(q)
        in_specs += [tok,
                     pl.BlockSpec((1, CHUNK, GROUP_W), lambda i, j, c=c: (i, c(j), v_blk)),
                     pl.BlockSpec((1, CHUNK, GATE_PAD), lambda i, j, c=c: (i, c(j), 0))]
        args += [k, p, gates]
    in_specs.append(pl.BlockSpec((1, GATE_PAD), lambda i, j: (0, 0)))
    args.append(bias)

    c_shape = (2, HEADS, HEAD_DIM, HEAD_DIM)
    v_shape = (2, HEADS, 1, HEAD_DIM)
    c_spec = pl.BlockSpec((1,) + c_shape, lambda i, j: (i, 0, 0, 0, 0))
    v_spec = pl.BlockSpec((1,) + v_shape, lambda i, j: (i, 0, 0, 0, 0))
    if with_h:
        in_specs += [c_spec, v_spec, v_spec]
        args += list(state)
        out_specs = [pl.BlockSpec((1, CHUNK, GROUP_W), lambda i, j: (i, j, 0)),
                     pl.BlockSpec((1, CHUNK, GROUP_W), lambda i, j: (i, n_chunks - 1 - j, 0))]
        out_shape = [jax.ShapeDtypeStruct((b, n, GROUP_W), F32)] * 2
    else:
        out_specs = [c_spec, v_spec, v_spec]
        out_shape = [jax.ShapeDtypeStruct((b,) + c_shape, F32),
                     jax.ShapeDtypeStruct((b,) + v_shape, F32),
                     jax.ShapeDtypeStruct((b,) + v_shape, F32)]
    return pl.pallas_call(
        functools.partial(_mlstm_kernel, n_chunks=n_chunks, with_h=with_h),
        grid=(b, n_chunks),
        in_specs=in_specs,
        out_specs=out_specs,
        out_shape=out_shape,
        scratch_shapes=[pltpu.VMEM(c_shape, F32), pltpu.VMEM(v_shape, F32), pltpu.VMEM(v_shape, F32)],
        compiler_params=_params("arbitrary", "arbitrary"),
        name="mlstm" if with_h else "mlstm_ctx_state",
    )(*args)


def _layer_norm(x):
    mu = jnp.mean(x, axis=-1, keepdims=True)
    var = jnp.mean(jnp.square(x - mu), axis=-1, keepdims=True)
    return (x - mu) * lax.rsqrt(var + EPS)


def _postmix_kernel(u_ref, vg_ref, o_ref, hf_ref, hb_ref, x_ref, mod_ref, ws_ref, bs_ref, ng_ref,
                    wout_ref, fg_ref, wr_ref, x1_ref, h2_ref, aff_ref, ycat_s):
    tm = x_ref.shape[1]
    b = pl.program_id(0)
    mod = lambda k: mod_ref[pl.ds(b, 1), k * D_MODEL:(k + 1) * D_MODEL]
    wr_hi, wr_lo = _split2(wr_ref[...])
    expert_lane = _iota2((ROW_GROUP, 128), 1) < N_EXPERTS

    for r in range(tm // ROW_GROUP):
        rows = slice(r * ROW_GROUP, (r + 1) * ROW_GROUP)

        for c in range(r * ROW_GROUP // CHUNK, (r + 1) * ROW_GROUP // CHUNK):
            crows = slice(c * CHUNK, (c + 1) * CHUNK)
            u = jax.nn.gelu(u_ref[0, crows, :])
            v = _layer_norm(jax.nn.gelu(vg_ref[0, crows, :])).astype(BF16)
            for hd in range(HEADS):
                lanes = slice(hd * HEAD_DIM, (hd + 1) * HEAD_DIM)
                s = _dot(ws_ref[hd], v[:, lanes]) + bs_ref[hd]
                ycat_s[crows, lanes] = (u[:, lanes] * s).astype(BF16)

        hsum = hf_ref[0, rows, :] + hb_ref[0, rows, :]
        o = jax.nn.sigmoid(o_ref[0, rows, :])
        for hd in range(HEADS):
            lanes = slice(hd * HEAD_DIM, (hd + 1) * HEAD_DIM)
            hn = _layer_norm(hsum[:, lanes]) * ng_ref[:, lanes]
            ycat_s[rows, GROUP_W + hd * HEAD_DIM:GROUP_W + (hd + 1) * HEAD_DIM] = (
                o[:, lanes] * hn).astype(BF16)

        y = _dot(ycat_s[rows, :], wout_ref[...])
        x1 = x_ref[0, rows, :] + mod(2) * y
        x1_ref[0, rows, :] = x1

        n2 = x1 * lax.rsqrt(jnp.mean(x1 * x1, axis=-1, keepdims=True) + EPS) * fg_ref[...]
        h2 = n2 * (1.0 + mod(4)) + mod(3)
        h2_hi, h2_lo = _split2(h2)
        h2_ref[0, rows, :] = h2_hi

        logits = _dot(h2_hi, wr_hi) + (_dot(h2_lo, wr_hi) + _dot(h2_hi, wr_lo))
        logits = jnp.where(expert_lane, logits, -jnp.inf)
        e = jnp.exp(logits - jnp.max(logits, axis=-1, keepdims=True))
        aff = e / jnp.sum(e, axis=-1, keepdims=True)
        aff_ref[0, :, rows] = aff.T[0:N_EXPERTS, :]


def _postmix(p, hf, hb, x, mod, ws, bs, ng, wout, fg, wr, tm):
    b, n, _ = x.shape
    tok = lambda blk: pl.BlockSpec((1, tm, GROUP_W), lambda i, j: (i, j, blk))
    full = lambda a: pl.BlockSpec(a.shape, lambda i, j: (0,) * a.ndim)
    return pl.pallas_call(
        _postmix_kernel,
        grid=(b, n // tm),
        in_specs=[tok(U_BLK), tok(VG_BLK), tok(O_BLK), tok(0), tok(0),
                  pl.BlockSpec((1, tm, D_MODEL), lambda i, j: (i, j, 0)),
                  full(mod), full(ws), full(bs), full(ng), full(wout), full(fg), full(wr)],
        out_specs=[pl.BlockSpec((1, tm, D_MODEL), lambda i, j: (i, j, 0)),
                   pl.BlockSpec((1, tm, D_MODEL), lambda i, j: (i, j, 0)),
                   pl.BlockSpec((1, N_EXPERTS, tm), lambda i, j: (i, 0, j))],
        out_shape=[jax.ShapeDtypeStruct((b, n, D_MODEL), F32),
                   jax.ShapeDtypeStruct((b, n, D_MODEL), BF16),
                   jax.ShapeDtypeStruct((b, N_EXPERTS, n), F32)],
        scratch_shapes=[pltpu.VMEM((tm, D_MODEL), BF16)],
        compiler_params=_params("arbitrary", "arbitrary"),
        name="postmix",
    )(p, p, p, hf, hb, x, mod, ws, bs, ng, wout, fg, wr)


def _cumsum_lanes(x, upper):
    carry = jnp.zeros((x.shape[0], 1), F32)
    outs, before = [], []
    for j in range(x.shape[1] // 128):
        before.append(carry)
        c = _dot(x[:, j * 128:(j + 1) * 128].astype(BF16), upper) + carry
        outs.append(c)
        carry = c[:, 127:128]
    return jnp.concatenate(outs, axis=1), before


def _rows_to_lanes(x, fill):
    pad = jnp.full((128 - x.shape[0], 128), fill, F32)
    return jnp.concatenate([x, pad], axis=0).T


def _route_kernel(aff_ref, slot_ref, w_ref, slot_t_ref, start_t_ref, *, capacity, window):
    aff = aff_ref[0]
    cap = float(capacity)
    thr_bits = jnp.zeros((aff.shape[0], 1), jnp.int32)
    for bit in range(30, -1, -1):
        cand = thr_bits | (1 << bit)
        cnt = jnp.sum(jnp.where(aff >= pltpu.bitcast(cand, F32), 1.0, 0.0), axis=-1, keepdims=True)
        thr_bits = jnp.where(cnt >= cap, cand, thr_bits)
    thr = pltpu.bitcast(thr_bits, F32)
    upper = jnp.where(_iota2((128, 128), 0) <= _iota2((128, 128), 1), 1.0, 0.0).astype(BF16)
    above = jnp.where(aff > thr, 1.0, 0.0)
    tied = jnp.where(aff == thr, 1.0, 0.0)
    need = cap - jnp.sum(above, axis=-1, keepdims=True)
    sel = above + tied * jnp.where(_cumsum_lanes(tied, upper)[0] <= need, 1.0, 0.0)
    count, before = _cumsum_lanes(sel, upper)
    slot = jnp.where(sel > 0.0, count - 1.0, -1.0)
    slot_ref[0] = slot
    w_ref[0] = jnp.where(sel > 0.0, aff, 0.0)

    n_blocks = len(before)
    for j in range(n_blocks):
        slot_t_ref[0, j * 128:(j + 1) * 128, :] = _rows_to_lanes(slot[:, j * 128:(j + 1) * 128], -1.0)
    lane = _iota2((1, 128), 1)
    first = jnp.zeros((aff.shape[0], 128), F32)
    for j in range(n_blocks):
        first = jnp.where(lane == j, before[j], first)
    start = jnp.minimum(jnp.floor(first * (1.0 / 16.0)) * 16.0, float(capacity - window))
    start_t_ref[0] = _rows_to_lanes(start, 0.0)[0:n_blocks, :]


def _route(aff_t, capacity, window):
    b, e, n = aff_t.shape
    assert n // 128 <= 128 and e <= 128 and window <= capacity
    spec = pl.BlockSpec((1, e, n), lambda i: (i, 0, 0))
    return pl.pallas_call(
        functools.partial(_route_kernel, capacity=capacity, window=window),
        grid=(b,),
        in_specs=[spec],
        out_specs=[spec, spec,
                   pl.BlockSpec((1, n, 128), lambda i: (i, 0, 0)),
                   pl.BlockSpec((1, n // 128, 128), lambda i: (i, 0, 0))],
        out_shape=[jax.ShapeDtypeStruct((b, e, n), F32)] * 2
        + [jax.ShapeDtypeStruct((b, n, 128), F32), jax.ShapeDtypeStruct((b, n // 128, 128), F32)],
        compiler_params=_params("arbitrary"),
        name="route",
    )(aff_t)


def _gather_kernel(h2_ref, slot_ref, w_ref, xe_ref, gate_ref, *, tok_blk):
    cap = xe_ref.shape[2]
    n = h2_ref.shape[1]
    slot_id = _iota2((cap, 1), 0).astype(F32)
    xe = jnp.zeros((cap, D_MODEL), F32)
    gate = jnp.zeros((cap, 1), F32)
    for j in range(n // tok_blk):
        toks = slice(j * tok_blk, (j + 1) * tok_blk)
        hit = slot_ref[0, 0, :, toks] == slot_id
        xe = xe + _dot(jnp.where(hit, 1.0, 0.0).astype(BF16), h2_ref[0, toks, :])
        gate = gate + jnp.sum(jnp.where(hit, w_ref[0, 0, :, toks], 0.0), axis=1, keepdims=True)
    xe_ref[0, 0] = xe.astype(BF16)
    gate_ref[0, 0] = gate


def _gather(h2, slot, w, capacity, tok_blk):
    b, n, _ = h2.shape
    n_e = slot.shape[1]
    row = pl.BlockSpec((1, 1, 1, n), lambda i, e: (i, e, 0, 0))
    return pl.pallas_call(
        functools.partial(_gather_kernel, tok_blk=tok_blk),
        grid=(b, n_e),
        in_specs=[pl.BlockSpec((1, n, D_MODEL), lambda i, e: (i, 0, 0)), row, row],
        out_specs=[pl.BlockSpec((1, 1, capacity, D_MODEL), lambda i, e: (i, e, 0, 0)),
                   pl.BlockSpec((1, 1, capacity, 1), lambda i, e: (i, e, 0, 0))],
        out_shape=[jax.ShapeDtypeStruct((b, n_e, capacity, D_MODEL), BF16),
                   jax.ShapeDtypeStruct((b, n_e, capacity, 1), F32)],
        compiler_params=_params("arbitrary", "arbitrary"),
        name="gather",
    )(h2, slot, w)


def _ffn_kernel(xe_ref, gate_ref, wg_ref, wu_ref, wd_ref, ye_ref, acc_s):
    ft = pl.program_id(1)

    @pl.when(ft == 0)
    def _():
        acc_s[...] = jnp.zeros_like(acc_s)

    wg = wg_ref[0].astype(BF16)
    wu = wu_ref[0].astype(BF16)
    wd = wd_ref[0].astype(BF16)
    for i in range(xe_ref.shape[0]):
        xe = xe_ref[i, 0]
        act = jax.nn.silu(_dot(xe, wg)) * _dot(xe, wu)
        acc_s[i] += _dot(act.astype(BF16), wd)

    @pl.when(ft == pl.num_programs(1) - 1)
    def _():
        for i in range(xe_ref.shape[0]):
            ye_ref[i] = (acc_s[i] * gate_ref[i, 0]).astype(BF16)


def _ffn(xe, gate, wg, wu, wd, f_tile):
    b, n_e, cap, _ = xe.shape
    return pl.pallas_call(
        _ffn_kernel,
        grid=(n_e, D_EXPERT // f_tile),
        in_specs=[pl.BlockSpec((b, 1, cap, D_MODEL), lambda e, f: (0, e, 0, 0)),
                  pl.BlockSpec((b, 1, cap, 1), lambda e, f: (0, e, 0, 0)),
                  pl.BlockSpec((1, D_MODEL, f_tile), lambda e, f: (e, 0, f)),
                  pl.BlockSpec((1, D_MODEL, f_tile), lambda e, f: (e, 0, f)),
                  pl.BlockSpec((1, f_tile, D_MODEL), lambda e, f: (e, f, 0))],
        out_specs=pl.BlockSpec((b, cap, D_MODEL), lambda e, f: (0, e, 0)),
        out_shape=jax.ShapeDtypeStruct((b, n_e * cap, D_MODEL), BF16),
        scratch_shapes=[pltpu.VMEM((b, cap, D_MODEL), F32)],
        compiler_params=_params("arbitrary", "arbitrary"),
        name="ffn",
    )(xe, gate, wg, wu, wd)


def _combine_kernel(start_ref, slot_t_ref, start_t_ref, ye_ref, x1_ref, mod_ref, fg_ref, o_ref, acc_s,
                    *, capacity, window):
    b = pl.program_id(0)
    j = pl.program_id(1)
    n_sub = o_ref.shape[1] // 128
    k_total = N_EXPERTS * window
    lane = _iota2((1, 128), 1).astype(F32)
    for sb in range(n_sub):
        blk = j * n_sub + sb
        slot_t = slot_t_ref[0, sb * 128:(sb + 1) * 128, :]
        k_pos = jnp.where(slot_t >= 0.0, slot_t - start_t_ref[0, pl.ds(blk, 1), :] + lane * float(window), -1.0)
        cols = []
        for c in range(k_total // 128):
            k_lane = lane + float(128 * c)
            hit = jnp.zeros((128, 128), F32)
            for e in range((128 * c) // window, (128 * c + 127) // window + 1):
                hit = jnp.where(k_pos[:, e:e + 1] == k_lane, 1.0, hit)
            cols.append(hit.astype(BF16))
        onehot = jnp.concatenate(cols, axis=1)
        rows = [ye_ref[0, pl.ds(pl.multiple_of(e * capacity + start_ref[b, blk, e], 16), window), :]
                for e in range(N_EXPERTS)]
        acc_s[sb * 128:(sb + 1) * 128, :] = _dot(onehot, jnp.concatenate(rows, axis=0))
    g2 = mod_ref[pl.ds(b, 1), 5 * D_MODEL:6 * D_MODEL]
    x2 = x1_ref[0] + g2 * acc_s[...]
    o_ref[0] = x2 * lax.rsqrt(jnp.mean(x2 * x2, axis=-1, keepdims=True) + EPS) * fg_ref[...]


def _combine(start, slot_t, start_t, ye, x1, mod, fg, capacity, window, tm):
    b, n, _ = x1.shape
    assert (N_EXPERTS * window) % 128 == 0 and window % 16 == 0 and capacity % 16 == 0
    tok = pl.BlockSpec((1, tm, D_MODEL), lambda i, j, s: (i, j, 0))
    return pl.pallas_call(
        functools.partial(_combine_kernel, capacity=capacity, window=window),
        grid_spec=pltpu.PrefetchScalarGridSpec(
            num_scalar_prefetch=1,
            grid=(b, n // tm),
            in_specs=[pl.BlockSpec((1, tm, 128), lambda i, j, s: (i, j, 0)),
                      pl.BlockSpec((1, n // 128, 128), lambda i, j, s: (i, 0, 0)),
                      pl.BlockSpec((1, N_EXPERTS * capacity, D_MODEL), lambda i, j, s: (i, 0, 0)),
                      tok,
                      pl.BlockSpec(mod.shape, lambda i, j, s: (0, 0)),
                      pl.BlockSpec((1, D_MODEL), lambda i, j, s: (0, 0))],
            out_specs=tok,
            scratch_shapes=[pltpu.VMEM((tm, D_MODEL), F32)]),
        out_shape=jax.ShapeDtypeStruct((b, n, D_MODEL), F32),
        compiler_params=_params("arbitrary", "arbitrary"),
        name="combine",
    )(start, slot_t, start_t, ye, x1, mod, fg)


def kernel(x, c, ctx, c_ctx, w_mod, b_mod, norm_mix_g, w_in, conv_q, conv_k, b_igate, b_fgate,
           gmlp_ws, gmlp_bs, mlstm_norm_g, w_out, norm_ffn_g, w_router, w_gate_e, w_up_e,
           w_down_e, final_g):
    depth = w_mod.shape[0]
    assert depth == 1, "the context stream is only carried as mLSTM states (single layer)"
    batch, seq, _ = x.shape
    assert seq % GRID_W == 0 and seq % CHUNK == 0 and batch + 1 <= MOD_ROWS
    capacity = EC_FACTOR * seq // N_EXPERTS
    ctx_row = batch
    l = 0

    cond = jnp.concatenate([c, c_ctx[None], jnp.zeros((MOD_ROWS - batch - 1, D_MODEL), F32)], axis=0)
    mod = _adaln(cond, w_mod[l], b_mod[l][None])

    row = lambda a: a[None]
    w_main = w_in[l][:, :MAIN_W].astype(BF16)
    w_gates = jnp.pad(w_in[l][:, MAIN_W:], ((0, 0), (0, GATE_PAD - N_GATES)))
    gate_bias = jnp.pad(jnp.concatenate([b_igate[l].reshape(-1), b_fgate[l].reshape(-1)]),
                        (0, GATE_PAD - N_GATES))[None]

    p_c, gates_c = _inproj(ctx, mod, row(norm_mix_g[l]), w_main[:, K_BLK * GROUP_W:], w_gates,
                           tm=ctx.shape[1], ctx_row=ctx_row)
    k_scale = HEAD_DIM ** -0.5
    (k_c,) = _dwconv(p_c, [(0, conv_k[l], k_scale)], tm=ctx.shape[1])
    state = _mlstm(None, k_c, p_c, 1, gates_c, gate_bias, None)

    p, gates = _inproj(x, mod, row(norm_mix_g[l]), w_main, w_gates, tm=512, ctx_row=None)
    q_l, k_l = _dwconv(p, [(Q_BLK, conv_q[l], 1.0), (K_BLK, conv_k[l], k_scale)], tm=512)
    h_f, h_b = _mlstm(q_l, k_l, p, V_BLK, gates, gate_bias, state)

    x1, h2, aff_t = _postmix(p, h_f, h_b, x, mod, gmlp_ws[l].astype(BF16), gmlp_bs[l][:, :, None],
                             row(mlstm_norm_g[l]), w_out[l].astype(BF16), row(norm_ffn_g[l]),
                             jnp.pad(w_router[l], ((0, 0), (0, 128 - N_EXPERTS))), tm=512)

    window = CHUNK + 16
    slot, gate_w, slot_t, start_t = _route(aff_t, capacity, window)
    slot = slot[:, :, None, :]
    gate_w = gate_w[:, :, None, :]

    xe, gate = _gather(h2, slot, gate_w, capacity, tok_blk=1024)
    ye = _ffn(xe, gate, w_gate_e[l], w_up_e[l], w_down_e[l], f_tile=512)
    start = start_t[:, :, :N_EXPERTS].astype(jnp.int32)
    return _combine(start, slot_t, start_t, ye, x1, mod, row(final_g), capacity, window, tm=512)
```

```python
import functools

import jax
import jax.numpy as jnp
from jax import lax
from jax.experimental import pallas as pl
from jax.experimental.pallas import tpu as pltpu

F32 = jnp.float32
BF16 = jnp.bfloat16

D_MODEL = 1024
GRID_W = 64
CHUNK = 128
HEADS = 4
GROUP_W = D_MODEL // 2
HEAD_DIM = GROUP_W // HEADS
N_EXPERTS = 16
EC_FACTOR = 2
D_EXPERT = 2 * D_MODEL
EPS = 1e-6
N_GATES = 4 * HEADS
GATE_PAD = 128
MOD_ROWS = 8

U_BLK, VG_BLK, Q_BLK, O_BLK, K_BLK, V_BLK = 0, 1, 2, 3, 4, 5
MAIN_W = 6 * GROUP_W

VMEM_LIMIT = 56 * 1024 * 1024
ROW_GROUP = 2 * CHUNK

def _params(*sem):
    return pltpu.CompilerParams(dimension_semantics=sem, vmem_limit_bytes=VMEM_LIMIT)


def _dot(a, b):
    return jnp.dot(a, b, preferred_element_type=F32)


def _dot_nt(a, b):
    return lax.dot_general(a, b, (((1,), (1,)), ((), ())), preferred_element_type=F32)


def _split2(a):
    hi = a.astype(BF16)
    lo = (a - hi.astype(F32)).astype(BF16)
    return hi, lo


def _dot3(a, b):
    ah, al = _split2(a)
    bh, bl = _split2(b)
    return _dot(ah, bh) + (_dot(al, bh) + _dot(ah, bl))


def _dot_exact01(tri, x):
    x1 = x.astype(BF16)
    r1 = x - x1.astype(F32)
    x2 = r1.astype(BF16)
    x3 = (r1 - x2.astype(F32)).astype(BF16)
    return _dot(tri, x1) + (_dot(tri, x2) + _dot(tri, x3))


def _iota2(shape, dim):
    return lax.broadcasted_iota(jnp.int32, shape, dim)


def _row_to_col(row):
    n = row.shape[1] // 128
    eye = _iota2((128, 128), 0) == _iota2((128, 128), 1)
    cols = [jnp.sum(jnp.where(eye, row[:, j * 128:(j + 1) * 128], 0.0), axis=1, keepdims=True)
            for j in range(n)]
    return cols[0] if n == 1 else jnp.concatenate(cols, axis=0)


def _adaln_kernel(cond_ref, w_ref, b_ref, o_ref):
    o_ref[...] = _dot3(jax.nn.silu(cond_ref[...]), w_ref[...]) + b_ref[...]


def _adaln(cond, w, b):
    n_out = w.shape[1]
    tn = D_MODEL
    return pl.pallas_call(
        _adaln_kernel,
        grid=(n_out // tn,),
        in_specs=[pl.BlockSpec((MOD_ROWS, D_MODEL), lambda j: (0, 0)),
                  pl.BlockSpec((D_MODEL, tn), lambda j: (0, j)),
                  pl.BlockSpec((1, tn), lambda j: (0, j))],
        out_specs=pl.BlockSpec((MOD_ROWS, tn), lambda j: (0, j)),
        out_shape=jax.ShapeDtypeStruct((MOD_ROWS, n_out), F32),
        compiler_params=_params("arbitrary"),
        name="adaln",
    )(cond, w, b)


def _inproj_kernel(x_ref, mod_ref, g_ref, w_ref, wg_ref, p_ref, gate_ref, *, ctx_row):
    row = pl.program_id(0) if ctx_row is None else ctx_row
    sh = mod_ref[pl.ds(row, 1), 0:D_MODEL]
    sc = mod_ref[pl.ds(row, 1), D_MODEL:2 * D_MODEL]
    for r in range(x_ref.shape[1] // ROW_GROUP):
        rows = slice(r * ROW_GROUP, (r + 1) * ROW_GROUP)
        x = x_ref[0, rows, :]
        y = x * lax.rsqrt(jnp.mean(x * x, axis=-1, keepdims=True) + EPS) * g_ref[...]
        h = (y * (1.0 + sc) + sh).astype(BF16)
        p_ref[0, rows, :] = _dot(h, w_ref[...])
        gate_ref[0, rows, :] = _dot(h, wg_ref[...])


def _inproj(x, mod, g, w, wg, tm, ctx_row):
    b, n, _ = x.shape
    wn = w.shape[1]
    return pl.pallas_call(
        functools.partial(_inproj_kernel, ctx_row=ctx_row),
        grid=(b, n // tm),
        in_specs=[pl.BlockSpec((1, tm, D_MODEL), lambda i, j: (i, j, 0)),
                  pl.BlockSpec(mod.shape, lambda i, j: (0, 0)),
                  pl.BlockSpec((1, D_MODEL), lambda i, j: (0, 0)),
                  pl.BlockSpec((D_MODEL, wn), lambda i, j: (0, 0)),
                  pl.BlockSpec((D_MODEL, GATE_PAD), lambda i, j: (0, 0))],
        out_specs=[pl.BlockSpec((1, tm, wn), lambda i, j: (i, j, 0)),
                   pl.BlockSpec((1, tm, GATE_PAD), lambda i, j: (i, j, 0))],
        out_shape=[jax.ShapeDtypeStruct((b, n, wn), F32),
                   jax.ShapeDtypeStruct((b, n, GATE_PAD), F32)],
        compiler_params=_params("arbitrary", "arbitrary"),
        name="inproj",
    )(x, mod, g, w, wg)


def _dwconv_kernel(*refs, n_tiles, scales):
    n_in = len(scales)
    j = pl.program_id(1)
    for s, scale in enumerate(scales):
        cur_ref, prev_ref, next_ref, w_ref = refs[4 * s:4 * s + 4]
        o_ref = refs[4 * n_in + s]
        cur = cur_ref[0]
        rows = cur.shape[0]
        prev_row = prev_ref[0, 7:8, :] * (j > 0).astype(F32)
        next_row = next_ref[0, 0:1, :] * (j < n_tiles - 1).astype(F32)
        t = _iota2((rows, 1), 0)
        before = jnp.where(t == 0, prev_row, pltpu.roll(cur, 1, axis=0))
        after = jnp.where(t == rows - 1, next_row, pltpu.roll(cur, rows - 1, axis=0))
        y = before * w_ref[0:1, :] + cur * w_ref[1:2, :] + after * w_ref[2:3, :]
        o_ref[0] = (jax.nn.silu(y) * scale).astype(BF16)


def _dwconv(p, streams, tm):
    b, n, _ = p.shape
    n_tiles = n // tm
    rows8 = tm // 8
    last8 = n // 8 - 1
    in_specs, args = [], []
    for blk, w, _ in streams:
        in_specs += [pl.BlockSpec((1, tm, GROUP_W), lambda i, j, blk=blk: (i, j, blk)),
                     pl.BlockSpec((1, 8, GROUP_W),
                                  lambda i, j, blk=blk: (i, jnp.maximum(j * rows8 - 1, 0), blk)),
                     pl.BlockSpec((1, 8, GROUP_W),
                                  lambda i, j, blk=blk: (i, jnp.minimum((j + 1) * rows8, last8), blk)),
                     pl.BlockSpec((3, GROUP_W), lambda i, j: (0, 0))]
        args += [p, p, p, w]
    return pl.pallas_call(
        functools.partial(_dwconv_kernel, n_tiles=n_tiles, scales=tuple(s for _, _, s in streams)),
        grid=(b, n_tiles),
        in_specs=in_specs,
        out_specs=[pl.BlockSpec((1, tm, GROUP_W), lambda i, j: (i, j, 0))] * len(streams),
        out_shape=[jax.ShapeDtypeStruct((b, n, GROUP_W), BF16)] * len(streams),
        compiler_params=_params("arbitrary", "arbitrary"),
        name="dwconv",
    )(*args)


def _mlstm_chunk(per_dir, bias_ref, c_s, n_s, m_s, h_refs):
    with_h = h_refs[0] is not None
    i0 = _iota2((CHUNK, CHUNK), 0)
    i1 = _iota2((CHUNK, CHUNK), 1)

    units = []
    for d, (q_ref, k_ref, v_ref, g_ref) in enumerate(per_dir):
        sees_ts = (i1 <= i0) if d == 0 else (i1 >= i0)
        sees_st = (i0 <= i1) if d == 0 else (i0 >= i1)
        tri = jnp.where(sees_ts, 1.0, 0.0).astype(BF16)
        gates = g_ref[0] + bias_ref[...]
        bcum = _dot_exact01(tri, jax.nn.log_sigmoid(gates))
        for hd in range(HEADS):
            lanes = slice(hd * HEAD_DIM, (hd + 1) * HEAD_DIM)
            c_prev = c_s[d, hd]
            n_prev = n_s[d, hd]
            k = k_ref[0, :, lanes]
            st = dict(d=d, hd=hd, lanes=lanes, c_prev=c_prev, n_prev=n_prev, k=k, sees_st=sees_st,
                      gates=gates, bcum=bcum, v=v_ref[0, :, lanes])
            if with_h:
                lhs = jnp.concatenate([k, c_prev.astype(BF16),
                                       jnp.broadcast_to(n_prev, (16, HEAD_DIM)).astype(BF16)], axis=0)
                st["prod"] = _dot_nt(lhs, q_ref[0, :, lanes])
            units.append(st)

    rows = {}
    for st in units:
        d, hd = st["d"], st["hd"]
        if d not in rows:
            rows[d] = (st["gates"].T, st["bcum"].T)
        gates_t, bcum_t = rows[d]
        ci = d * HEADS + hd
        cf = 2 * HEADS + d * HEADS + hd
        last = CHUNK - 1 if d == 0 else 0
        li_row = gates_t[ci:ci + 1, :]
        bc_row = bcum_t[cf:cf + 1, :]
        b_last = bc_row[:, last:last + 1]
        m_prev = m_s[d, hd]
        v_t = st["v"].T
        a_row = b_last - bc_row + li_row
        m_new = jnp.maximum(b_last + m_prev, jnp.max(a_row, axis=-1, keepdims=True))
        w_row = jnp.exp(a_row - m_new)
        lhs = jnp.concatenate([v_t * w_row, jnp.broadcast_to(w_row, (16, CHUNK))], axis=0)
        st["upd"] = _dot(lhs.astype(BF16), st["k"])
        st.update(bc_row=bc_row, m_prev=m_prev, m_new=m_new, v_t=v_t,
                  decay=jnp.exp(b_last + m_prev - m_new),
                  u_col=st["gates"][:, ci:ci + 1] - st["bcum"][:, cf:cf + 1])

    if with_h:
        for st in units:
            prod, bc_row = st["prod"], st["bc_row"]
            g = bc_row + st["m_prev"]
            dmat = jnp.where(st["sees_st"], st["u_col"] + bc_row, -jnp.inf)
            m_t = jnp.maximum(g, jnp.max(dmat, axis=0, keepdims=True))
            inter = jnp.exp(g - m_t)
            s = prod[0:CHUNK] * jnp.exp(dmat - m_t)
            st["pv"] = _dot(st["v_t"].astype(BF16), s.astype(BF16))
            st["num0"] = inter * prod[CHUNK:2 * CHUNK]
            den = inter * prod[2 * CHUNK:2 * CHUNK + 1] + jnp.sum(s, axis=0, keepdims=True)
            st["scale"] = 1.0 / jnp.maximum(jnp.abs(den), jnp.exp(-m_t))

    for st in units:
        d, hd = st["d"], st["hd"]
        if with_h:
            h_refs[d][0, :, st["lanes"]] = ((st["num0"] + st["pv"]) * st["scale"]).T
        c_s[d, hd] = st["decay"] * st["c_prev"] + st["upd"][0:HEAD_DIM]
        n_s[d, hd] = st["decay"] * st["n_prev"] + st["upd"][HEAD_DIM:HEAD_DIM + 1]
        m_s[d, hd] = st["m_new"]


def _mlstm_kernel(*refs, n_chunks, with_h):
    refs = list(refs)
    take = lambda n: [refs.pop(0) for _ in range(n)]
    per_dir = []
    for _ in range(2):
        q_ref = take(1)[0] if with_h else None
        k_ref, v_ref, g_ref = take(3)
        per_dir.append((q_ref, k_ref, v_ref, g_ref))
    bias_ref = take(1)[0]
    if with_h:
        c0_ref, n0_ref, m0_ref = take(3)
        h_refs = take(2)
        c_out = n_out = m_out = None
    else:
        h_refs = [None, None]
        c_out, n_out, m_out = take(3)
    c_s, n_s, m_s = take(3)

    j = pl.program_id(1)

    @pl.when(j == 0)
    def _():
        if with_h:
            c_s[...] = c0_ref[0]
            n_s[...] = n0_ref[0]
            m_s[...] = m0_ref[0]
        else:
            c_s[...] = jnp.zeros_like(c_s)
            n_s[...] = jnp.zeros_like(n_s)
            m_s[...] = jnp.zeros_like(m_s)

    _mlstm_chunk(per_dir, bias_ref, c_s, n_s, m_s, h_refs)

    if not with_h:
        @pl.when(j == n_chunks - 1)
        def _():
            c_out[0] = c_s[...]
            n_out[0] = n_s[...]
            m_out[0] = m_s[...]


def _mlstm(q, k, p, v_blk, gates, bias, state):
    b, n, _ = k.shape
    n_chunks = n // CHUNK
    with_h = q is not None

    in_specs, args = [], []
    for d in range(2):
        c = (lambda j: j) if d == 0 else (lambda j: n_chunks - 1 - j)
        tok = pl.BlockSpec((1, CHUNK, GROUP_W), lambda i, j, c=c: (i, c(j), 0))
        if with_h:
            in_specs.append(tok)
            args.append(q)
        in_specs += [tok,
                     pl.BlockSpec((1, CHUNK, GROUP_W), lambda i, j, c=c: (i, c(j), v_blk)),
                     pl.BlockSpec((1, CHUNK, GATE_PAD), lambda i, j, c=c: (i, c(j), 0))]
        args += [k, p, gates]
    in_specs.append(pl.BlockSpec((1, GATE_PAD), lambda i, j: (0, 0)))
    args.append(bias)

    c_shape = (2, HEADS, HEAD_DIM, HEAD_DIM)
    v_shape = (2, HEADS, 1, HEAD_DIM)
    c_spec = pl.BlockSpec((1,) + c_shape, lambda i, j: (i, 0, 0, 0, 0))
    v_spec = pl.BlockSpec((1,) + v_shape, lambda i, j: (i, 0, 0, 0, 0))
    if with_h:
        in_specs += [c_spec, v_spec, v_spec]
        args += list(state)
        out_specs = [pl.BlockSpec((1, CHUNK, GROUP_W), lambda i, j: (i, j, 0)),
                     pl.BlockSpec((1, CHUNK, GROUP_W), lambda i, j: (i, n_chunks - 1 - j, 0))]
        out_shape = [jax.ShapeDtypeStruct((b, n, GROUP_W), F32)] * 2
    else:
        out_specs = [c_spec, v_spec, v_spec]
        out_shape = [jax.ShapeDtypeStruct((b,) + c_shape, F32),
                     jax.ShapeDtypeStruct((b,) + v_shape, F32),
                     jax.ShapeDtypeStruct((b,) + v_shape, F32)]
    return pl.pallas_call(
        functools.partial(_mlstm_kernel, n_chunks=n_chunks, with_h=with_h),
        grid=(b, n_chunks),
        in_specs=in_specs,
        out_specs=out_specs,
        out_shape=out_shape,
        scratch_shapes=[pltpu.VMEM(c_shape, F32), pltpu.VMEM(v_shape, F32), pltpu.VMEM(v_shape, F32)],
        compiler_params=_params("arbitrary", "arbitrary"),
        name="mlstm" if with_h else "mlstm_ctx_state",
    )(*args)


def _layer_norm(x):
    mu = jnp.mean(x, axis=-1, keepdims=True)
    var = jnp.mean(jnp.square(x - mu), axis=-1, keepdims=True)
    return (x - mu) * lax.rsqrt(var + EPS)


def _postmix_kernel(u_ref, vg_ref, o_ref, hf_ref, hb_ref, x_ref, mod_ref, ws_ref, bs_ref, ng_ref,
                    wout_ref, fg_ref, wr_ref, x1_ref, h2_ref, aff_ref, ycat_s):
    tm = x_ref.shape[1]
    b = pl.program_id(0)
    mod = lambda k: mod_ref[pl.ds(b, 1), k * D_MODEL:(k + 1) * D_MODEL]
    wr_hi, wr_lo = _split2(wr_ref[...])
    expert_lane = _iota2((ROW_GROUP, 128), 1) < N_EXPERTS

    for r in range(tm // ROW_GROUP):
        rows = slice(r * ROW_GROUP, (r + 1) * ROW_GROUP)

        for c in range(r * ROW_GROUP // CHUNK, (r + 1) * ROW_GROUP // CHUNK):
            crows = slice(c * CHUNK, (c + 1) * CHUNK)
            u = jax.nn.gelu(u_ref[0, crows, :])
            v = _layer_norm(jax.nn.gelu(vg_ref[0, crows, :])).astype(BF16)
            for hd in range(HEADS):
                lanes = slice(hd * HEAD_DIM, (hd + 1) * HEAD_DIM)
                s = _dot(ws_ref[hd], v[:, lanes]) + bs_ref[hd]
                ycat_s[crows, lanes] = (u[:, lanes] * s).astype(BF16)

        hsum = hf_ref[0, rows, :] + hb_ref[0, rows, :]
        o = jax.nn.sigmoid(o_ref[0, rows, :])
        for hd in range(HEADS):
            lanes = slice(hd * HEAD_DIM, (hd + 1) * HEAD_DIM)
            hn = _layer_norm(hsum[:, lanes]) * ng_ref[:, lanes]
            ycat_s[rows, GROUP_W + hd * HEAD_DIM:GROUP_W + (hd + 1) * HEAD_DIM] = (
                o[:, lanes] * hn).astype(BF16)

        y = _dot(ycat_s[rows, :], wout_ref[...])
        x1 = x_ref[0, rows, :] + mod(2) * y
        x1_ref[0, rows, :] = x1

        n2 = x1 * lax.rsqrt(jnp.mean(x1 * x1, axis=-1, keepdims=True) + EPS) * fg_ref[...]
        h2 = n2 * (1.0 + mod(4)) + mod(3)
        h2_hi, h2_lo = _split2(h2)
        h2_ref[0, rows, :] = h2_hi

        logits = _dot(h2_hi, wr_hi) + (_dot(h2_lo, wr_hi) + _dot(h2_hi, wr_lo))
        logits = jnp.where(expert_lane, logits, -jnp.inf)
        e = jnp.exp(logits - jnp.max(logits, axis=-1, keepdims=True))
        aff = e / jnp.sum(e, axis=-1, keepdims=True)
        aff_ref[0, :, rows] = aff.T[0:N_EXPERTS, :]


def _postmix(p, hf, hb, x, mod, ws, bs, ng, wout, fg, wr, tm):
    b, n, _ = x.shape
    tok = lambda blk: pl.BlockSpec((1, tm, GROUP_W), lambda i, j: (i, j, blk))
    full = lambda a: pl.BlockSpec(a.shape, lambda i, j: (0,) * a.ndim)
    return pl.pallas_call(
        _postmix_kernel,
        grid=(b, n // tm),
        in_specs=[tok(U_BLK), tok(VG_BLK), tok(O_BLK), tok(0), tok(0),
                  pl.BlockSpec((1, tm, D_MODEL), lambda i, j: (i, j, 0)),
                  full(mod), full(ws), full(bs), full(ng), full(wout), full(fg), full(wr)],
        out_specs=[pl.BlockSpec((1, tm, D_MODEL), lambda i, j: (i, j, 0)),
                   pl.BlockSpec((1, tm, D_MODEL), lambda i, j: (i, j, 0)),
                   pl.BlockSpec((1, N_EXPERTS, tm), lambda i, j: (i, 0, j))],
        out_shape=[jax.ShapeDtypeStruct((b, n, D_MODEL), F32),
                   jax.ShapeDtypeStruct((b, n, D_MODEL), BF16),
                   jax.ShapeDtypeStruct((b, N_EXPERTS, n), F32)],
        scratch_shapes=[pltpu.VMEM((tm, D_MODEL), BF16)],
        compiler_params=_params("arbitrary", "arbitrary"),
        name="postmix",
    )(p, p, p, hf, hb, x, mod, ws, bs, ng, wout, fg, wr)


def _cumsum_lanes(x, upper):
    carry = jnp.zeros((x.shape[0], 1), F32)
    outs, before = [], []
    for j in range(x.shape[1] // 128):
        before.append(carry)
        c = _dot(x[:, j * 128:(j + 1) * 128].astype(BF16), upper) + carry
        outs.append(c)
        carry = c[:, 127:128]
    return jnp.concatenate(outs, axis=1), before


def _rows_to_lanes(x, fill):
    pad = jnp.full((128 - x.shape[0], 128), fill, F32)
    return jnp.concatenate([x, pad], axis=0).T


def _route_kernel(aff_ref, slot_ref, w_ref, slot_t_ref, start_t_ref, *, capacity, window):
    aff = aff_ref[0]
    cap = float(capacity)
    thr_bits = jnp.zeros((aff.shape[0], 1), jnp.int32)
    for bit in range(30, -1, -1):
        cand = thr_bits | (1 << bit)
        cnt = jnp.sum(jnp.where(aff >= pltpu.bitcast(cand, F32), 1.0, 0.0), axis=-1, keepdims=True)
        thr_bits = jnp.where(cnt >= cap, cand, thr_bits)
    thr = pltpu.bitcast(thr_bits, F32)
    upper = jnp.where(_iota2((128, 128), 0) <= _iota2((128, 128), 1), 1.0, 0.0).astype(BF16)
    above = jnp.where(aff > thr, 1.0, 0.0)
    tied = jnp.where(aff == thr, 1.0, 0.0)
    need = cap - jnp.sum(above, axis=-1, keepdims=True)
    sel = above + tied * jnp.where(_cumsum_lanes(tied, upper)[0] <= need, 1.0, 0.0)
    count, before = _cumsum_lanes(sel, upper)
    slot = jnp.where(sel > 0.0, count - 1.0, -1.0)
    slot_ref[0] = slot
    w_ref[0] = jnp.where(sel > 0.0, aff, 0.0)

    n_blocks = len(before)
    for j in range(n_blocks):
        slot_t_ref[0, j * 128:(j + 1) * 128, :] = _rows_to_lanes(slot[:, j * 128:(j + 1) * 128], -1.0)
    lane = _iota2((1, 128), 1)
    first = jnp.zeros((aff.shape[0], 128), F32)
    for j in range(n_blocks):
        first = jnp.where(lane == j, before[j], first)
    start = jnp.minimum(jnp.floor(first * (1.0 / 16.0)) * 16.0, float(capacity - window))
    start_t_ref[0] = _rows_to_lanes(start, 0.0)[0:n_blocks, :]


def _route(aff_t, capacity, window):
    b, e, n = aff_t.shape
    assert n // 128 <= 128 and e <= 128 and window <= capacity
    spec = pl.BlockSpec((1, e, n), lambda i: (i, 0, 0))
    return pl.pallas_call(
        functools.partial(_route_kernel, capacity=capacity, window=window),
        grid=(b,),
        in_specs=[spec],
        out_specs=[spec, spec,
                   pl.BlockSpec((1, n, 128), lambda i: (i, 0, 0)),
                   pl.BlockSpec((1, n // 128, 128), lambda i: (i, 0, 0))],
        out_shape=[jax.ShapeDtypeStruct((b, e, n), F32)] * 2
        + [jax.ShapeDtypeStruct((b, n, 128), F32), jax.ShapeDtypeStruct((b, n // 128, 128), F32)],
        compiler_params=_params("arbitrary"),
        name="route",
    )(aff_t)


def _gather_kernel(h2_ref, slot_ref, w_ref, xe_ref, gate_ref, *, tok_blk):
    cap = xe_ref.shape[2]
    n = h2_ref.shape[1]
    slot_id = _iota2((cap, 1), 0).astype(F32)
    xe = jnp.zeros((cap, D_MODEL), F32)
    gate = jnp.zeros((cap, 1), F32)
    for j in range(n // tok_blk):
        toks = slice(j * tok_blk, (j + 1) * tok_blk)
        hit = slot_ref[0, 0, :, toks] == slot_id
        xe = xe + _dot(jnp.where(hit, 1.0, 0.0).astype(BF16), h2_ref[0, toks, :])
        gate = gate + jnp.sum(jnp.where(hit, w_ref[0, 0, :, toks], 0.0), axis=1, keepdims=True)
    xe_ref[0, 0] = xe.astype(BF16)
    gate_ref[0, 0] = gate


def _gather(h2, slot, w, capacity, tok_blk):
    b, n, _ = h2.shape
    n_e = slot.shape[1]
    row = pl.BlockSpec((1, 1, 1, n), lambda i, e: (i, e, 0, 0))
    return pl.pallas_call(
        functools.partial(_gather_kernel, tok_blk=tok_blk),
        grid=(b, n_e),
        in_specs=[pl.BlockSpec((1, n, D_MODEL), lambda i, e: (i, 0, 0)), row, row],
        out_specs=[pl.BlockSpec((1, 1, capacity, D_MODEL), lambda i, e: (i, e, 0, 0)),
                   pl.BlockSpec((1, 1, capacity, 1), lambda i, e: (i, e, 0, 0))],
        out_shape=[jax.ShapeDtypeStruct((b, n_e, capacity, D_MODEL), BF16),
                   jax.ShapeDtypeStruct((b, n_e, capacity, 1), F32)],
        compiler_params=_params("arbitrary", "arbitrary"),
        name="gather",
    )(h2, slot, w)


def _ffn_kernel(xe_ref, gate_ref, wg_ref, wu_ref, wd_ref, ye_ref, acc_s):
    ft = pl.program_id(1)
    last = pl.num_programs(1) - 1

    def step(first, final):
        wg = wg_ref[0].astype(BF16)
        wu = wu_ref[0].astype(BF16)
        wd = wd_ref[0].astype(BF16)
        for i in range(xe_ref.shape[0]):
            xe = xe_ref[i, 0]
            act = jax.nn.silu(_dot(xe, wg)) * _dot(xe, wu)
            part = _dot(act.astype(BF16), wd)
            total = part if first else acc_s[i] + part
            if final:
                ye_ref[i] = (total * gate_ref[i, 0]).astype(BF16)
            else:
                acc_s[i] = total

    pl.when(ft == 0)(lambda: step(True, False))
    pl.when(jnp.logical_and(ft > 0, ft < last))(lambda: step(False, False))
    pl.when(ft == last)(lambda: step(False, True))


def _ffn(xe, gate, wg, wu, wd, f_tile):
    b, n_e, cap, _ = xe.shape
    assert D_EXPERT // f_tile >= 2
    return pl.pallas_call(
        _ffn_kernel,
        grid=(n_e, D_EXPERT // f_tile),
        in_specs=[pl.BlockSpec((b, 1, cap, D_MODEL), lambda e, f: (0, e, 0, 0)),
                  pl.BlockSpec((b, 1, cap, 1), lambda e, f: (0, e, 0, 0)),
                  pl.BlockSpec((1, D_MODEL, f_tile), lambda e, f: (e, 0, f)),
                  pl.BlockSpec((1, D_MODEL, f_tile), lambda e, f: (e, 0, f)),
                  pl.BlockSpec((1, f_tile, D_MODEL), lambda e, f: (e, f, 0))],
        out_specs=pl.BlockSpec((b, cap, D_MODEL), lambda e, f: (0, e, 0)),
        out_shape=jax.ShapeDtypeStruct((b, n_e * cap, D_MODEL), BF16),
        scratch_shapes=[pltpu.VMEM((b, cap, D_MODEL), F32)],
        compiler_params=_params("arbitrary", "arbitrary"),
        name="ffn",
    )(xe, gate, wg, wu, wd)


def _combine_kernel(start_ref, slot_t_ref, start_t_ref, ye_ref, x1_ref, mod_ref, fg_ref, o_ref, acc_s,
                    *, capacity, window):
    b = pl.program_id(0)
    j = pl.program_id(1)
    n_sub = o_ref.shape[1] // 128
    k_total = N_EXPERTS * window
    lane = _iota2((1, 128), 1).astype(F32)
    for sb in range(n_sub):
        blk = j * n_sub + sb
        slot_t = slot_t_ref[0, sb * 128:(sb + 1) * 128, :]
        k_pos = jnp.where(slot_t >= 0.0, slot_t - start_t_ref[0, pl.ds(blk, 1), :] + lane * float(window), -1.0)
        cols = []
        for c in range(k_total // 128):
            k_lane = lane + float(128 * c)
            hit = jnp.zeros((128, 128), F32)
            for e in range((128 * c) // window, (128 * c + 127) // window + 1):
                hit = jnp.where(k_pos[:, e:e + 1] == k_lane, 1.0, hit)
            cols.append(hit.astype(BF16))
        onehot = jnp.concatenate(cols, axis=1)
        rows = [ye_ref[0, pl.ds(pl.multiple_of(e * capacity + start_ref[b, blk, e], 16), window), :]
                for e in range(N_EXPERTS)]
        acc_s[sb * 128:(sb + 1) * 128, :] = _dot(onehot, jnp.concatenate(rows, axis=0))
    g2 = mod_ref[pl.ds(b, 1), 5 * D_MODEL:6 * D_MODEL]
    x2 = x1_ref[0] + g2 * acc_s[...]
    o_ref[0] = x2 * lax.rsqrt(jnp.mean(x2 * x2, axis=-1, keepdims=True) + EPS) * fg_ref[...]


def _combine(start, slot_t, start_t, ye, x1, mod, fg, capacity, window, tm):
    b, n, _ = x1.shape
    assert (N_EXPERTS * window) % 128 == 0 and window % 16 == 0 and capacity % 16 == 0
    tok = pl.BlockSpec((1, tm, D_MODEL), lambda i, j, s: (i, j, 0))
    return pl.pallas_call(
        functools.partial(_combine_kernel, capacity=capacity, window=window),
        grid_spec=pltpu.PrefetchScalarGridSpec(
            num_scalar_prefetch=1,
            grid=(b, n // tm),
            in_specs=[pl.BlockSpec((1, tm, 128), lambda i, j, s: (i, j, 0)),
                      pl.BlockSpec((1, n // 128, 128), lambda i, j, s: (i, 0, 0)),
                      pl.BlockSpec((1, N_EXPERTS * capacity, D_MODEL), lambda i, j, s: (i, 0, 0)),
                      tok,
                      pl.BlockSpec(mod.shape, lambda i, j, s: (0, 0)),
                      pl.BlockSpec((1, D_MODEL), lambda i, j, s: (0, 0))],
            out_specs=tok,
            scratch_shapes=[pltpu.VMEM((tm, D_MODEL), F32)]),
        out_shape=jax.ShapeDtypeStruct((b, n, D_MODEL), F32),
        compiler_params=_params("arbitrary", "arbitrary"),
        name="combine",
    )(start, slot_t, start_t, ye, x1, mod, fg)


def kernel(x, c, ctx, c_ctx, w_mod, b_mod, norm_mix_g, w_in, conv_q, conv_k, b_igate, b_fgate,
           gmlp_ws, gmlp_bs, mlstm_norm_g, w_out, norm_ffn_g, w_router, w_gate_e, w_up_e,
           w_down_e, final_g):
    depth = w_mod.shape[0]
    assert depth == 1, "the context stream is only carried as mLSTM states (single layer)"
    batch, seq, _ = x.shape
    assert seq % GRID_W == 0 and seq % CHUNK == 0 and batch + 1 <= MOD_ROWS
    capacity = EC_FACTOR * seq // N_EXPERTS
    ctx_row = batch
    l = 0

    cond = jnp.concatenate([c, c_ctx[None], jnp.zeros((MOD_ROWS - batch - 1, D_MODEL), F32)], axis=0)
    mod = _adaln(cond, w_mod[l], b_mod[l][None])

    row = lambda a: a[None]
    w_main = w_in[l][:, :MAIN_W].astype(BF16)
    w_gates = jnp.pad(w_in[l][:, MAIN_W:], ((0, 0), (0, GATE_PAD - N_GATES))).astype(BF16)
    gate_bias = jnp.pad(jnp.concatenate([b_igate[l].reshape(-1), b_fgate[l].reshape(-1)]),
                        (0, GATE_PAD - N_GATES))[None]

    p_c, gates_c = _inproj(ctx, mod, row(norm_mix_g[l]), w_main[:, K_BLK * GROUP_W:], w_gates,
                           tm=ctx.shape[1], ctx_row=ctx_row)
    k_scale = HEAD_DIM ** -0.5
    (k_c,) = _dwconv(p_c, [(0, conv_k[l], k_scale)], tm=ctx.shape[1])
    state = _mlstm(None, k_c, p_c, 1, gates_c, gate_bias, None)

    p, gates = _inproj(x, mod, row(norm_mix_g[l]), w_main, w_gates, tm=512, ctx_row=None)
    q_l, k_l = _dwconv(p, [(Q_BLK, conv_q[l], 1.0), (K_BLK, conv_k[l], k_scale)], tm=512)
    h_f, h_b = _mlstm(q_l, k_l, p, V_BLK, gates, gate_bias, state)

    x1, h2, aff_t = _postmix(p, h_f, h_b, x, mod, gmlp_ws[l].astype(BF16), gmlp_bs[l][:, :, None],
                             row(mlstm_norm_g[l]), w_out[l].astype(BF16), row(norm_ffn_g[l]),
                             jnp.pad(w_router[l], ((0, 0), (0, 128 - N_EXPERTS))), tm=512)

    window = CHUNK + 16
    slot, gate_w, slot_t, start_t = _route(aff_t, capacity, window)
    slot = slot[:, :, None, :]
    gate_w = gate_w[:, :, None, :]

    xe, gate = _gather(h2, slot, gate_w, capacity, tok_blk=1024)
    ye = _ffn(xe, gate, w_gate_e[l], w_up_e[l], w_down_e[l], f_tile=512)
    start = start_t[:, :, :N_EXPERTS].astype(jnp.int32)
    return _combine(start, slot_t, start_t, ye, x1, mod, row(final_g), capacity, window, tm=512)
```

```python
import functools

import jax
import jax.numpy as jnp
from jax import lax
from jax.experimental import pallas as pl
from jax.experimental.pallas import tpu as pltpu

F32 = jnp.float32
BF16 = jnp.bfloat16

D_MODEL = 1024
GRID_W = 64
CHUNK = 128
HEADS = 4
GROUP_W = D_MODEL // 2
HEAD_DIM = GROUP_W // HEADS
N_EXPERTS = 16
EC_FACTOR = 2
D_EXPERT = 2 * D_MODEL
EPS = 1e-6
N_GATES = 4 * HEADS
GATE_PAD = 128
MOD_ROWS = 8

U_BLK, VG_BLK, Q_BLK, O_BLK, K_BLK, V_BLK = 0, 1, 2, 3, 4, 5
MAIN_W = 6 * GROUP_W

VMEM_LIMIT = 56 * 1024 * 1024
ROW_GROUP = 2 * CHUNK

def _params(*sem):
    return pltpu.CompilerParams(dimension_semantics=sem, vmem_limit_bytes=VMEM_LIMIT)


def _dot(a, b):
    return jnp.dot(a, b, preferred_element_type=F32)


def _dot_nt(a, b):
    return lax.dot_general(a, b, (((1,), (1,)), ((), ())), preferred_element_type=F32)


def _split2(a):
    hi = a.astype(BF16)
    lo = (a - hi.astype(F32)).astype(BF16)
    return hi, lo


def _dot3(a, b):
    ah, al = _split2(a)
    bh, bl = _split2(b)
    return _dot(ah, bh) + (_dot(al, bh) + _dot(ah, bl))


def _dot_exact01(tri, x):
    x1 = x.astype(BF16)
    r1 = x - x1.astype(F32)
    x2 = r1.astype(BF16)
    x3 = (r1 - x2.astype(F32)).astype(BF16)
    return _dot(tri, x1) + (_dot(tri, x2) + _dot(tri, x3))


def _iota2(shape, dim):
    return lax.broadcasted_iota(jnp.int32, shape, dim)


def _row_to_col(row):
    n = row.shape[1] // 128
    eye = _iota2((128, 128), 0) == _iota2((128, 128), 1)
    cols = [jnp.sum(jnp.where(eye, row[:, j * 128:(j + 1) * 128], 0.0), axis=1, keepdims=True)
            for j in range(n)]
    return cols[0] if n == 1 else jnp.concatenate(cols, axis=0)


def _adaln_kernel(cond_ref, w_ref, b_ref, o_ref):
    o_ref[...] = _dot3(jax.nn.silu(cond_ref[...]), w_ref[...]) + b_ref[...]


def _adaln(cond, w, b):
    n_out = w.shape[1]
    tn = D_MODEL
    return pl.pallas_call(
        _adaln_kernel,
        grid=(n_out // tn,),
        in_specs=[pl.BlockSpec((MOD_ROWS, D_MODEL), lambda j: (0, 0)),
                  pl.BlockSpec((D_MODEL, tn), lambda j: (0, j)),
                  pl.BlockSpec((1, tn), lambda j: (0, j))],
        out_specs=pl.BlockSpec((MOD_ROWS, tn), lambda j: (0, j)),
        out_shape=jax.ShapeDtypeStruct((MOD_ROWS, n_out), F32),
        compiler_params=_params("arbitrary"),
        name="adaln",
    )(cond, w, b)


def _inproj_kernel(x_ref, mod_ref, g_ref, w_ref, wg_ref, p_ref, gate_ref, *, ctx_row):
    row = pl.program_id(0) if ctx_row is None else ctx_row
    sh = mod_ref[pl.ds(row, 1), 0:D_MODEL]
    sc = mod_ref[pl.ds(row, 1), D_MODEL:2 * D_MODEL]
    for r in range(x_ref.shape[1] // ROW_GROUP):
        rows = slice(r * ROW_GROUP, (r + 1) * ROW_GROUP)
        x = x_ref[0, rows, :]
        y = x * lax.rsqrt(jnp.mean(x * x, axis=-1, keepdims=True) + EPS) * g_ref[...]
        h = (y * (1.0 + sc) + sh).astype(BF16)
        p_ref[0, rows, :] = _dot(h, w_ref[...])
        gate_ref[0, rows, :] = _dot(h, wg_ref[...])


def _inproj(x, mod, g, w, wg, tm, ctx_row):
    b, n, _ = x.shape
    wn = w.shape[1]
    return pl.pallas_call(
        functools.partial(_inproj_kernel, ctx_row=ctx_row),
        grid=(b, n // tm),
        in_specs=[pl.BlockSpec((1, tm, D_MODEL), lambda i, j: (i, j, 0)),
                  pl.BlockSpec(mod.shape, lambda i, j: (0, 0)),
                  pl.BlockSpec((1, D_MODEL), lambda i, j: (0, 0)),
                  pl.BlockSpec((D_MODEL, wn), lambda i, j: (0, 0)),
                  pl.BlockSpec((D_MODEL, GATE_PAD), lambda i, j: (0, 0))],
        out_specs=[pl.BlockSpec((1, tm, wn), lambda i, j: (i, j, 0)),
                   pl.BlockSpec((1, tm, GATE_PAD), lambda i, j: (i, j, 0))],
        out_shape=[jax.ShapeDtypeStruct((b, n, wn), F32),
                   jax.ShapeDtypeStruct((b, n, GATE_PAD), F32)],
        compiler_params=_params("arbitrary", "arbitrary"),
        name="inproj",
    )(x, mod, g, w, wg)


def _dwconv_kernel(*refs, n_tiles, scales):
    n_in = len(scales)
    j = pl.program_id(1)
    for s, scale in enumerate(scales):
        cur_ref, prev_ref, next_ref, w_ref = refs[4 * s:4 * s + 4]
        o_ref = refs[4 * n_in + s]
        cur = cur_ref[0]
        rows = cur.shape[0]
        prev_row = prev_ref[0, 7:8, :] * (j > 0).astype(F32)
        next_row = next_ref[0, 0:1, :] * (j < n_tiles - 1).astype(F32)
        t = _iota2((rows, 1), 0)
        before = jnp.where(t == 0, prev_row, pltpu.roll(cur, 1, axis=0))
        after = jnp.where(t == rows - 1, next_row, pltpu.roll(cur, rows - 1, axis=0))
        y = before * w_ref[0:1, :] + cur * w_ref[1:2, :] + after * w_ref[2:3, :]
        o_ref[0] = (jax.nn.silu(y) * scale).astype(BF16)


def _dwconv(p, streams, tm):
    b, n, _ = p.shape
    n_tiles = n // tm
    rows8 = tm // 8
    last8 = n // 8 - 1
    in_specs, args = [], []
    for blk, w, _ in streams:
        in_specs += [pl.BlockSpec((1, tm, GROUP_W), lambda i, j, blk=blk: (i, j, blk)),
                     pl.BlockSpec((1, 8, GROUP_W),
                                  lambda i, j, blk=blk: (i, jnp.maximum(j * rows8 - 1, 0), blk)),
                     pl.BlockSpec((1, 8, GROUP_W),
                                  lambda i, j, blk=blk: (i, jnp.minimum((j + 1) * rows8, last8), blk)),
                     pl.BlockSpec((3, GROUP_W), lambda i, j: (0, 0))]
        args += [p, p, p, w]
    return pl.pallas_call(
        functools.partial(_dwconv_kernel, n_tiles=n_tiles, scales=tuple(s for _, _, s in streams)),
        grid=(b, n_tiles),
        in_specs=in_specs,
        out_specs=[pl.BlockSpec((1, tm, GROUP_W), lambda i, j: (i, j, 0))] * len(streams),
        out_shape=[jax.ShapeDtypeStruct((b, n, GROUP_W), BF16)] * len(streams),
        compiler_params=_params("arbitrary", "arbitrary"),
        name="dwconv",
    )(*args)


def _mlstm_chunk(per_dir, bias_ref, c_s, n_s, m_s, h_refs):
    with_h = h_refs[0] is not None
    i0 = _iota2((CHUNK, CHUNK), 0)
    i1 = _iota2((CHUNK, CHUNK), 1)

    units = []
    for d, (q_ref, k_ref, v_ref, g_ref) in enumerate(per_dir):
        sees_ts = (i1 <= i0) if d == 0 else (i1 >= i0)
        sees_st = (i0 <= i1) if d == 0 else (i0 >= i1)
        tri = jnp.where(sees_ts, 1.0, 0.0).astype(BF16)
        gates = g_ref[0] + bias_ref[...]
        bcum = _dot_exact01(tri, jax.nn.log_sigmoid(gates))
        for hd in range(HEADS):
            lanes = slice(hd * HEAD_DIM, (hd + 1) * HEAD_DIM)
            c_prev = c_s[d, hd]
            n_prev = n_s[d, hd]
            k = k_ref[0, :, lanes]
            st = dict(d=d, hd=hd, lanes=lanes, c_prev=c_prev, n_prev=n_prev, k=k, sees_st=sees_st,
                      gates=gates, bcum=bcum, v=v_ref[0, :, lanes])
            if with_h:
                lhs = jnp.concatenate([k, c_prev.astype(BF16),
                                       jnp.broadcast_to(n_prev, (16, HEAD_DIM)).astype(BF16)], axis=0)
                st["prod"] = _dot_nt(lhs, q_ref[0, :, lanes])
            units.append(st)

    rows = {}
    for st in units:
        d, hd = st["d"], st["hd"]
        if d not in rows:
            rows[d] = (st["gates"].T, st["bcum"].T)
        gates_t, bcum_t = rows[d]
        ci = d * HEADS + hd
        cf = 2 * HEADS + d * HEADS + hd
        last = CHUNK - 1 if d == 0 else 0
        li_row = gates_t[ci:ci + 1, :]
        bc_row = bcum_t[cf:cf + 1, :]
        b_last = bc_row[:, last:last + 1]
        m_prev = m_s[d, hd]
        v_t = st["v"].T
        a_row = b_last - bc_row + li_row
        m_new = jnp.maximum(b_last + m_prev, jnp.max(a_row, axis=-1, keepdims=True))
        w_row = jnp.exp(a_row - m_new)
        lhs = jnp.concatenate([v_t * w_row, jnp.broadcast_to(w_row, (16, CHUNK))], axis=0)
        st["upd"] = _dot(lhs.astype(BF16), st["k"])
        st.update(bc_row=bc_row, m_prev=m_prev, m_new=m_new, v_t=v_t,
                  decay=jnp.exp(b_last + m_prev - m_new),
                  u_col=st["gates"][:, ci:ci + 1] - st["bcum"][:, cf:cf + 1])

    if with_h:
        for st in units:
            prod, bc_row = st["prod"], st["bc_row"]
            g = bc_row + st["m_prev"]
            dmat = jnp.where(st["sees_st"], st["u_col"] + bc_row, -jnp.inf)
            m_t = jnp.maximum(g, jnp.max(dmat, axis=0, keepdims=True))
            inter = jnp.exp(g - m_t)
            s = prod[0:CHUNK] * jnp.exp(dmat - m_t)
            st["pv"] = _dot(st["v_t"].astype(BF16), s.astype(BF16))
            st["num0"] = inter * prod[CHUNK:2 * CHUNK]
            den = inter * prod[2 * CHUNK:2 * CHUNK + 1] + jnp.sum(s, axis=0, keepdims=True)
            st["scale"] = 1.0 / jnp.maximum(jnp.abs(den), jnp.exp(-m_t))

    for st in units:
        d, hd = st["d"], st["hd"]
        if with_h:
            h_refs[d][0, :, st["lanes"]] = ((st["num0"] + st["pv"]) * st["scale"]).T
        c_s[d, hd] = st["decay"] * st["c_prev"] + st["upd"][0:HEAD_DIM]
        n_s[d, hd] = st["decay"] * st["n_prev"] + st["upd"][HEAD_DIM:HEAD_DIM + 1]
        m_s[d, hd] = st["m_new"]


def _mlstm_kernel(*refs, n_chunks, with_h):
    refs = list(refs)
    take = lambda n: [refs.pop(0) for _ in range(n)]
    per_dir = []
    for _ in range(2):
        q_ref = take(1)[0] if with_h else None
        k_ref, v_ref, g_ref = take(3)
        per_dir.append((q_ref, k_ref, v_ref, g_ref))
    bias_ref = take(1)[0]
    if with_h:
        c0_ref, n0_ref, m0_ref = take(3)
        h_refs = take(2)
        c_out = n_out = m_out = None
    else:
        h_refs = [None, None]
        c_out, n_out, m_out = take(3)
    c_s, n_s, m_s = take(3)

    j = pl.program_id(1)

    @pl.when(j == 0)
    def _():
        if with_h:
            c_s[...] = c0_ref[0]
            n_s[...] = n0_ref[0]
            m_s[...] = m0_ref[0]
        else:
            c_s[...] = jnp.zeros_like(c_s)
            n_s[...] = jnp.zeros_like(n_s)
            m_s[...] = jnp.zeros_like(m_s)

    _mlstm_chunk(per_dir, bias_ref, c_s, n_s, m_s, h_refs)

    if not with_h:
        @pl.when(j == n_chunks - 1)
        def _():
            c_out[0] = c_s[...]
            n_out[0] = n_s[...]
            m_out[0] = m_s[...]


def _mlstm(q, k, p, v_blk, gates, bias, state):
    b, n, _ = k.shape
    n_chunks = n // CHUNK
    with_h = q is not None

    in_specs, args = [], []
    for d in range(2):
        c = (lambda j: j) if d == 0 else (lambda j: n_chunks - 1 - j)
        tok = pl.BlockSpec((1, CHUNK, GROUP_W), lambda i, j, c=c: (i, c(j), 0))
        if with_h:
            in_specs.append(tok)
            args.append(q)
        in_specs += [tok,
                     pl.BlockSpec((1, CHUNK, GROUP_W), lambda i, j, c=c: (i, c(j), v_blk)),
                     pl.BlockSpec((1, CHUNK, GATE_PAD), lambda i, j, c=c: (i, c(j), 0))]
        args += [k, p, gates]
    in_specs.append(pl.BlockSpec((1, GATE_PAD), lambda i, j: (0, 0)))
    args.append(bias)

    c_shape = (2, HEADS, HEAD_DIM, HEAD_DIM)
    v_shape = (2, HEADS, 1, HEAD_DIM)
    c_spec = pl.BlockSpec((1,) + c_shape, lambda i, j: (i, 0, 0, 0, 0))
    v_spec = pl.BlockSpec((1,) + v_shape, lambda i, j: (i, 0, 0, 0, 0))
    if with_h:
        in_specs += [c_spec, v_spec, v_spec]
        args += list(state)
        out_specs = [pl.BlockSpec((1, CHUNK, GROUP_W), lambda i, j: (i, j, 0)),
                     pl.BlockSpec((1, CHUNK, GROUP_W), lambda i, j: (i, n_chunks - 1 - j, 0))]
        out_shape = [jax.ShapeDtypeStruct((b, n, GROUP_W), F32)] * 2
    else:
        out_specs = [c_spec, v_spec, v_spec]
        out_shape = [jax.ShapeDtypeStruct((b,) + c_shape, F32),
                     jax.ShapeDtypeStruct((b,) + v_shape, F32),
                     jax.ShapeDtypeStruct((b,) + v_shape, F32)]
    return pl.pallas_call(
        functools.partial(_mlstm_kernel, n_chunks=n_chunks, with_h=with_h),
        grid=(b, n_chunks),
        in_specs=in_specs,
        out_specs=out_specs,
        out_shape=out_shape,
        scratch_shapes=[pltpu.VMEM(c_shape, F32), pltpu.VMEM(v_shape, F32), pltpu.VMEM(v_shape, F32)],
        compiler_params=_params("arbitrary", "arbitrary"),
        name="mlstm" if with_h else "mlstm_ctx_state",
    )(*args)


def _layer_norm(x):
    mu = jnp.mean(x, axis=-1, keepdims=True)
    var = jnp.mean(jnp.square(x - mu), axis=-1, keepdims=True)
    return (x - mu) * lax.rsqrt(var + EPS)


def _postmix_kernel(u_ref, vg_ref, o_ref, hf_ref, hb_ref, x_ref, mod_ref, ws_ref, bs_ref, ng_ref,
                    wout_ref, fg_ref, wr_ref, x1_ref, h2_ref, aff_ref, ycat_s):
    tm = x_ref.shape[1]
    b = pl.program_id(0)
    mod = lambda k: mod_ref[pl.ds(b, 1), k * D_MODEL:(k + 1) * D_MODEL]
    wr_hi, wr_lo = _split2(wr_ref[...])
    expert_lane = _iota2((ROW_GROUP, 128), 1) < N_EXPERTS

    for r in range(tm // ROW_GROUP):
        rows = slice(r * ROW_GROUP, (r + 1) * ROW_GROUP)

        for c in range(r * ROW_GROUP // CHUNK, (r + 1) * ROW_GROUP // CHUNK):
            crows = slice(c * CHUNK, (c + 1) * CHUNK)
            u = jax.nn.gelu(u_ref[0, crows, :])
            v = _layer_norm(jax.nn.gelu(vg_ref[0, crows, :])).astype(BF16)
            for hd in range(HEADS):
                lanes = slice(hd * HEAD_DIM, (hd + 1) * HEAD_DIM)
                s = _dot(ws_ref[hd], v[:, lanes]) + bs_ref[hd]
                ycat_s[crows, lanes] = (u[:, lanes] * s).astype(BF16)

        hsum = hf_ref[0, rows, :] + hb_ref[0, rows, :]
        o = jax.nn.sigmoid(o_ref[0, rows, :])
        for hd in range(HEADS):
            lanes = slice(hd * HEAD_DIM, (hd + 1) * HEAD_DIM)
            hn = _layer_norm(hsum[:, lanes]) * ng_ref[:, lanes]
            ycat_s[rows, GROUP_W + hd * HEAD_DIM:GROUP_W + (hd + 1) * HEAD_DIM] = (
                o[:, lanes] * hn).astype(BF16)

        y = _dot(ycat_s[rows, :], wout_ref[...])
        x1 = x_ref[0, rows, :] + mod(2) * y
        x1_ref[0, rows, :] = x1

        n2 = x1 * lax.rsqrt(jnp.mean(x1 * x1, axis=-1, keepdims=True) + EPS) * fg_ref[...]
        h2 = n2 * (1.0 + mod(4)) + mod(3)
        h2_hi, h2_lo = _split2(h2)
        h2_ref[0, rows, :] = h2_hi

        logits = _dot(h2_hi, wr_hi) + (_dot(h2_lo, wr_hi) + _dot(h2_hi, wr_lo))
        logits = jnp.where(expert_lane, logits, -jnp.inf)
        e = jnp.exp(logits - jnp.max(logits, axis=-1, keepdims=True))
        aff = e / jnp.sum(e, axis=-1, keepdims=True)
        aff_ref[0, :, rows] = aff.T[0:N_EXPERTS, :]


def _postmix(p, hf, hb, x, mod, ws, bs, ng, wout, fg, wr, tm):
    b, n, _ = x.shape
    tok = lambda blk: pl.BlockSpec((1, tm, GROUP_W), lambda i, j: (i, j, blk))
    full = lambda a: pl.BlockSpec(a.shape, lambda i, j: (0,) * a.ndim)
    return pl.pallas_call(
        _postmix_kernel,
        grid=(b, n // tm),
        in_specs=[tok(U_BLK), tok(VG_BLK), tok(O_BLK), tok(0), tok(0),
                  pl.BlockSpec((1, tm, D_MODEL), lambda i, j: (i, j, 0)),
                  full(mod), full(ws), full(bs), full(ng), full(wout), full(fg), full(wr)],
        out_specs=[pl.BlockSpec((1, tm, D_MODEL), lambda i, j: (i, j, 0)),
                   pl.BlockSpec((1, tm, D_MODEL), lambda i, j: (i, j, 0)),
                   pl.BlockSpec((1, N_EXPERTS, tm), lambda i, j: (i, 0, j))],
        out_shape=[jax.ShapeDtypeStruct((b, n, D_MODEL), F32),
                   jax.ShapeDtypeStruct((b, n, D_MODEL), BF16),
                   jax.ShapeDtypeStruct((b, N_EXPERTS, n), F32)],
        scratch_shapes=[pltpu.VMEM((tm, D_MODEL), BF16)],
        compiler_params=_params("arbitrary", "arbitrary"),
        name="postmix",
    )(p, p, p, hf, hb, x, mod, ws, bs, ng, wout, fg, wr)


def _cumsum_lanes(x, upper):
    carry = jnp.zeros((x.shape[0], 1), F32)
    outs, before = [], []
    for j in range(x.shape[1] // 128):
        before.append(carry)
        c = _dot(x[:, j * 128:(j + 1) * 128].astype(BF16), upper) + carry
        outs.append(c)
        carry = c[:, 127:128]
    return jnp.concatenate(outs, axis=1), before


def _rows_to_lanes(x, fill):
    pad = jnp.full((128 - x.shape[0], 128), fill, F32)
    return jnp.concatenate([x, pad], axis=0).T


def _route_kernel(aff_ref, slot_ref, w_ref, slot_t_ref, start_t_ref, first_ref, *, capacity, window):
    aff = aff_ref[0]
    cap = float(capacity)
    thr_bits = jnp.zeros((aff.shape[0], 1), jnp.int32)
    for bit in range(30, -1, -1):
        cand = thr_bits | (1 << bit)
        cnt = jnp.sum(jnp.where(aff >= pltpu.bitcast(cand, F32), 1.0, 0.0), axis=-1, keepdims=True)
        thr_bits = jnp.where(cnt >= cap, cand, thr_bits)
    thr = pltpu.bitcast(thr_bits, F32)
    upper = jnp.where(_iota2((128, 128), 0) <= _iota2((128, 128), 1), 1.0, 0.0).astype(BF16)
    above = jnp.where(aff > thr, 1.0, 0.0)
    tied = jnp.where(aff == thr, 1.0, 0.0)
    need = cap - jnp.sum(above, axis=-1, keepdims=True)
    sel = above + tied * jnp.where(_cumsum_lanes(tied, upper)[0] <= need, 1.0, 0.0)
    count, before = _cumsum_lanes(sel, upper)
    slot = jnp.where(sel > 0.0, count - 1.0, -1.0)
    slot_ref[0] = slot
    w_ref[0] = jnp.where(sel > 0.0, aff, 0.0)

    n_blocks = len(before)
    for j in range(n_blocks):
        slot_t_ref[0, j * 128:(j + 1) * 128, :] = _rows_to_lanes(slot[:, j * 128:(j + 1) * 128], -1.0)
    lane = _iota2((1, 128), 1)
    first = jnp.zeros((aff.shape[0], 128), F32)
    for j in range(n_blocks):
        first = jnp.where(lane == j, before[j], first)
    first_ref[0] = first
    start = jnp.minimum(jnp.floor(first * (1.0 / 16.0)) * 16.0, float(capacity - window))
    start_t_ref[0] = _rows_to_lanes(start, 0.0)[0:n_blocks, :]


def _route(aff_t, capacity, window):
    b, e, n = aff_t.shape
    assert n // 128 <= 128 and e <= 128 and window <= capacity
    spec = pl.BlockSpec((1, e, n), lambda i: (i, 0, 0))
    return pl.pallas_call(
        functools.partial(_route_kernel, capacity=capacity, window=window),
        grid=(b,),
        in_specs=[spec],
        out_specs=[spec, spec,
                   pl.BlockSpec((1, n, 128), lambda i: (i, 0, 0)),
                   pl.BlockSpec((1, n // 128, 128), lambda i: (i, 0, 0)),
                   pl.BlockSpec((1, e, 128), lambda i: (i, 0, 0))],
        out_shape=[jax.ShapeDtypeStruct((b, e, n), F32)] * 2
        + [jax.ShapeDtypeStruct((b, n, 128), F32), jax.ShapeDtypeStruct((b, n // 128, 128), F32),
           jax.ShapeDtypeStruct((b, e, 128), F32)],
        compiler_params=_params("arbitrary"),
        name="route",
    )(aff_t)


GATHER_TILE = 256
GATHER_ROWS = 128
GATHER_GROUP = 4


def _gather_kernel(before_ref, fits_ref, h2_ref, slot_ref, w_ref, xe_ref, gate_ref, xe_s, gate_s):
    b = pl.program_id(0)
    g = pl.program_id(1)
    n_g, cap = xe_ref.shape[1], xe_ref.shape[2]
    n = h2_ref.shape[1]
    n_tiles = n // GATHER_TILE

    @pl.when(fits_ref[b, g] != 0)
    def _():
        xe_s[:, 0:16, :] = jnp.zeros((n_g, 16, D_MODEL), BF16)
        gate_s[...] = jnp.zeros_like(gate_s)
        row_id = _iota2((GATHER_ROWS, 1), 0).astype(F32)
        for j in range(n_tiles):
            starts, onehots = [], []
            for e in range(n_g):
                start = pl.multiple_of((before_ref[b, g * n_g + e, j] // 16) * 16, 16)
                hit = (slot_ref[0, e, j:j + 1, :] - start.astype(F32)) == row_id
                onehots.append(jnp.where(hit, 1.0, 0.0).astype(BF16))
                gate_s[e, pl.ds(start, GATHER_ROWS), :] += jnp.sum(
                    jnp.where(hit, w_ref[0, e, j:j + 1, :], 0.0), axis=1, keepdims=True)
                starts.append(start)
            rows = _dot(jnp.concatenate(onehots, axis=0),
                        h2_ref[0, j * GATHER_TILE:(j + 1) * GATHER_TILE, :]).astype(BF16)
            for e, start in enumerate(starts):
                r0 = e * GATHER_ROWS
                xe_s[e, pl.ds(start, 16), :] += rows[r0:r0 + 16]
                xe_s[e, pl.ds(start + 16, GATHER_ROWS - 16), :] = rows[r0 + 16:r0 + GATHER_ROWS]
        xe_ref[0] = xe_s[:, 0:cap, :]
        gate_ref[0] = gate_s[:, 0:cap, :]

    @pl.when(fits_ref[b, g] == 0)
    def _():
        slot_id = _iota2((cap, 1), 0).astype(F32)
        for e in range(n_g):
            xe = jnp.zeros((cap, D_MODEL), F32)
            gate = jnp.zeros((cap, 1), F32)
            for j in range(n_tiles):
                hit = slot_ref[0, e, j:j + 1, :] == slot_id
                xe = xe + _dot(jnp.where(hit, 1.0, 0.0).astype(BF16),
                               h2_ref[0, j * GATHER_TILE:(j + 1) * GATHER_TILE, :])
                gate = gate + jnp.sum(jnp.where(hit, w_ref[0, e, j:j + 1, :], 0.0), axis=1,
                                      keepdims=True)
            xe_ref[0, e] = xe.astype(BF16)
            gate_ref[0, e] = gate


def _gather(h2, slot, w, first, capacity):
    b, n, _ = h2.shape
    n_e = slot.shape[1]
    n_tiles = n // GATHER_TILE
    assert n_e % GATHER_GROUP == 0 and GATHER_TILE % 128 == 0 and capacity % 16 == 0
    before = first[:, :, 0:n // 128:GATHER_TILE // 128].astype(jnp.int32)
    after = jnp.concatenate([before[:, :, 1:], jnp.full((b, n_e, 1), capacity, jnp.int32)], axis=2)
    fits = jnp.all((after - (before // 16) * 16) < GATHER_ROWS, axis=2)
    fits = jnp.all(fits.reshape(b, n_e // GATHER_GROUP, GATHER_GROUP), axis=2).astype(jnp.int32)
    tiles = lambda a: a.reshape(b, n_e, n_tiles, GATHER_TILE)
    rows = pl.BlockSpec((1, GATHER_GROUP, n_tiles, GATHER_TILE), lambda i, g, *_: (i, g, 0, 0))
    return pl.pallas_call(
        _gather_kernel,
        grid_spec=pltpu.PrefetchScalarGridSpec(
            num_scalar_prefetch=2,
            grid=(b, n_e // GATHER_GROUP),
            in_specs=[pl.BlockSpec((1, n, D_MODEL), lambda i, g, *_: (i, 0, 0)), rows, rows],
            out_specs=[pl.BlockSpec((1, GATHER_GROUP, capacity, D_MODEL), lambda i, g, *_: (i, g, 0, 0)),
                       pl.BlockSpec((1, GATHER_GROUP, capacity, 1), lambda i, g, *_: (i, g, 0, 0))],
            scratch_shapes=[pltpu.VMEM((GATHER_GROUP, capacity + GATHER_ROWS, D_MODEL), BF16),
                            pltpu.VMEM((GATHER_GROUP, capacity + GATHER_ROWS, 1), F32)]),
        out_shape=[jax.ShapeDtypeStruct((b, n_e, capacity, D_MODEL), BF16),
                   jax.ShapeDtypeStruct((b, n_e, capacity, 1), F32)],
        compiler_params=_params("arbitrary", "arbitrary"),
        name="gather",
    )(before, fits, h2, tiles(slot), tiles(w))


def _ffn_kernel(xe_ref, gate_ref, wg_ref, wu_ref, wd_ref, ye_ref, acc_s):
    ft = pl.program_id(1)
    last = pl.num_programs(1) - 1

    def step(first, final):
        wg = wg_ref[0].astype(BF16)
        wu = wu_ref[0].astype(BF16)
        wd = wd_ref[0].astype(BF16)
        for i in range(xe_ref.shape[0]):
            xe = xe_ref[i, 0]
            act = jax.nn.silu(_dot(xe, wg)) * _dot(xe, wu)
            part = _dot(act.astype(BF16), wd)
            total = part if first else acc_s[i] + part
            if final:
                ye_ref[i] = (total * gate_ref[i, 0]).astype(BF16)
            else:
                acc_s[i] = total

    pl.when(ft == 0)(lambda: step(True, False))
    pl.when(jnp.logical_and(ft > 0, ft < last))(lambda: step(False, False))
    pl.when(ft == last)(lambda: step(False, True))


def _ffn(xe, gate, wg, wu, wd, f_tile):
    b, n_e, cap, _ = xe.shape
    assert D_EXPERT // f_tile >= 2
    return pl.pallas_call(
        _ffn_kernel,
        grid=(n_e, D_EXPERT // f_tile),
        in_specs=[pl.BlockSpec((b, 1, cap, D_MODEL), lambda e, f: (0, e, 0, 0)),
                  pl.BlockSpec((b, 1, cap, 1), lambda e, f: (0, e, 0, 0)),
                  pl.BlockSpec((1, D_MODEL, f_tile), lambda e, f: (e, 0, f)),
                  pl.BlockSpec((1, D_MODEL, f_tile), lambda e, f: (e, 0, f)),
                  pl.BlockSpec((1, f_tile, D_MODEL), lambda e, f: (e, f, 0))],
        out_specs=pl.BlockSpec((b, cap, D_MODEL), lambda e, f: (0, e, 0)),
        out_shape=jax.ShapeDtypeStruct((b, n_e * cap, D_MODEL), BF16),
        scratch_shapes=[pltpu.VMEM((b, cap, D_MODEL), F32)],
        compiler_params=_params("arbitrary", "arbitrary"),
        name="ffn",
    )(xe, gate, wg, wu, wd)


def _combine_kernel(start_ref, slot_t_ref, start_t_ref, ye_ref, x1_ref, mod_ref, fg_ref, o_ref, acc_s,
                    *, capacity, window):
    b = pl.program_id(0)
    j = pl.program_id(1)
    n_sub = o_ref.shape[1] // 128
    k_total = N_EXPERTS * window
    lane = _iota2((1, 128), 1).astype(F32)
    for sb in range(n_sub):
        blk = j * n_sub + sb
        slot_t = slot_t_ref[0, sb * 128:(sb + 1) * 128, :]
        k_pos = jnp.where(slot_t >= 0.0, slot_t - start_t_ref[0, pl.ds(blk, 1), :] + lane * float(window), -1.0)
        cols = []
        for c in range(k_total // 128):
            k_lane = lane + float(128 * c)
            hit = jnp.zeros((128, 128), F32)
            for e in range((128 * c) // window, (128 * c + 127) // window + 1):
                hit = jnp.where(k_pos[:, e:e + 1] == k_lane, 1.0, hit)
            cols.append(hit.astype(BF16))
        onehot = jnp.concatenate(cols, axis=1)
        rows = [ye_ref[0, pl.ds(pl.multiple_of(e * capacity + start_ref[b, blk, e], 16), window), :]
                for e in range(N_EXPERTS)]
        acc_s[sb * 128:(sb + 1) * 128, :] = _dot(onehot, jnp.concatenate(rows, axis=0))
    g2 = mod_ref[pl.ds(b, 1), 5 * D_MODEL:6 * D_MODEL]
    x2 = x1_ref[0] + g2 * acc_s[...]
    o_ref[0] = x2 * lax.rsqrt(jnp.mean(x2 * x2, axis=-1, keepdims=True) + EPS) * fg_ref[...]


def _combine(start, slot_t, start_t, ye, x1, mod, fg, capacity, window, tm):
    b, n, _ = x1.shape
    assert (N_EXPERTS * window) % 128 == 0 and window % 16 == 0 and capacity % 16 == 0
    tok = pl.BlockSpec((1, tm, D_MODEL), lambda i, j, s: (i, j, 0))
    return pl.pallas_call(
        functools.partial(_combine_kernel, capacity=capacity, window=window),
        grid_spec=pltpu.PrefetchScalarGridSpec(
            num_scalar_prefetch=1,
            grid=(b, n // tm),
            in_specs=[pl.BlockSpec((1, tm, 128), lambda i, j, s: (i, j, 0)),
                      pl.BlockSpec((1, n // 128, 128), lambda i, j, s: (i, 0, 0)),
                      pl.BlockSpec((1, N_EXPERTS * capacity, D_MODEL), lambda i, j, s: (i, 0, 0)),
                      tok,
                      pl.BlockSpec(mod.shape, lambda i, j, s: (0, 0)),
                      pl.BlockSpec((1, D_MODEL), lambda i, j, s: (0, 0))],
            out_specs=tok,
            scratch_shapes=[pltpu.VMEM((tm, D_MODEL), F32)]),
        out_shape=jax.ShapeDtypeStruct((b, n, D_MODEL), F32),
        compiler_params=_params("arbitrary", "arbitrary"),
        name="combine",
    )(start, slot_t, start_t, ye, x1, mod, fg)


def kernel(x, c, ctx, c_ctx, w_mod, b_mod, norm_mix_g, w_in, conv_q, conv_k, b_igate, b_fgate,
           gmlp_ws, gmlp_bs, mlstm_norm_g, w_out, norm_ffn_g, w_router, w_gate_e, w_up_e,
           w_down_e, final_g):
    depth = w_mod.shape[0]
    assert depth == 1, "the context stream is only carried as mLSTM states (single layer)"
    batch, seq, _ = x.shape
    assert seq % GRID_W == 0 and seq % CHUNK == 0 and batch + 1 <= MOD_ROWS
    capacity = EC_FACTOR * seq // N_EXPERTS
    ctx_row = batch
    l = 0

    cond = jnp.concatenate([c, c_ctx[None], jnp.zeros((MOD_ROWS - batch - 1, D_MODEL), F32)], axis=0)
    mod = _adaln(cond, w_mod[l], b_mod[l][None])

    row = lambda a: a[None]
    w_main = w_in[l][:, :MAIN_W].astype(BF16)
    w_gates = jnp.pad(w_in[l][:, MAIN_W:], ((0, 0), (0, GATE_PAD - N_GATES))).astype(BF16)
    gate_bias = jnp.pad(jnp.concatenate([b_igate[l].reshape(-1), b_fgate[l].reshape(-1)]),
                        (0, GATE_PAD - N_GATES))[None]

    p_c, gates_c = _inproj(ctx, mod, row(norm_mix_g[l]), w_main[:, K_BLK * GROUP_W:], w_gates,
                           tm=ctx.shape[1], ctx_row=ctx_row)
    k_scale = HEAD_DIM ** -0.5
    (k_c,) = _dwconv(p_c, [(0, conv_k[l], k_scale)], tm=ctx.shape[1])
    state = _mlstm(None, k_c, p_c, 1, gates_c, gate_bias, None)

    p, gates = _inproj(x, mod, row(norm_mix_g[l]), w_main, w_gates, tm=512, ctx_row=None)
    q_l, k_l = _dwconv(p, [(Q_BLK, conv_q[l], 1.0), (K_BLK, conv_k[l], k_scale)], tm=512)
    h_f, h_b = _mlstm(q_l, k_l, p, V_BLK, gates, gate_bias, state)

    x1, h2, aff_t = _postmix(p, h_f, h_b, x, mod, gmlp_ws[l].astype(BF16), gmlp_bs[l][:, :, None],
                             row(mlstm_norm_g[l]), w_out[l].astype(BF16), row(norm_ffn_g[l]),
                             jnp.pad(w_router[l], ((0, 0), (0, 128 - N_EXPERTS))), tm=512)

    window = CHUNK + 16
    slot, gate_w, slot_t, start_t, first = _route(aff_t, capacity, window)

    xe, gate = _gather(h2, slot, gate_w, first, capacity)
    ye = _ffn(xe, gate, w_gate_e[l], w_up_e[l], w_down_e[l], f_tile=512)
    start = start_t[:, :, :N_EXPERTS].astype(jnp.int32)
    return _combine(start, slot_t, start_t, ye, x1, mod, row(final_g), capacity, window, tm=512)
```

```python
import functools

import jax
import jax.numpy as jnp
from jax import lax
from jax.experimental import pallas as pl
from jax.experimental.pallas import tpu as pltpu

F32 = jnp.float32
BF16 = jnp.bfloat16

D_MODEL = 1024
GRID_W = 64
CHUNK = 128
HEADS = 4
GROUP_W = D_MODEL // 2
HEAD_DIM = GROUP_W // HEADS
N_EXPERTS = 16
EC_FACTOR = 2
D_EXPERT = 2 * D_MODEL
EPS = 1e-6
N_GATES = 4 * HEADS
GATE_PAD = 128
MOD_ROWS = 8

U_BLK, VG_BLK, Q_BLK, O_BLK, K_BLK, V_BLK = 0, 1, 2, 3, 4, 5
MAIN_W = 6 * GROUP_W

VMEM_LIMIT = 56 * 1024 * 1024
ROW_GROUP = 2 * CHUNK
MLSTM_SAMPLES = 4

def _params(*sem):
    return pltpu.CompilerParams(dimension_semantics=sem, vmem_limit_bytes=VMEM_LIMIT)


def _dot(a, b):
    return jnp.dot(a, b, preferred_element_type=F32)


def _dot_nt(a, b):
    return lax.dot_general(a, b, (((1,), (1,)), ((), ())), preferred_element_type=F32)


def _split2(a):
    hi = a.astype(BF16)
    lo = (a - hi.astype(F32)).astype(BF16)
    return hi, lo


def _dot3(a, b):
    ah, al = _split2(a)
    bh, bl = _split2(b)
    return _dot(ah, bh) + (_dot(al, bh) + _dot(ah, bl))


def _dot_exact01(tri, x):
    x1 = x.astype(BF16)
    r1 = x - x1.astype(F32)
    x2 = r1.astype(BF16)
    x3 = (r1 - x2.astype(F32)).astype(BF16)
    return _dot(tri, x1) + (_dot(tri, x2) + _dot(tri, x3))


def _iota2(shape, dim):
    return lax.broadcasted_iota(jnp.int32, shape, dim)


def _row_to_col(row):
    n = row.shape[1] // 128
    eye = _iota2((128, 128), 0) == _iota2((128, 128), 1)
    cols = [jnp.sum(jnp.where(eye, row[:, j * 128:(j + 1) * 128], 0.0), axis=1, keepdims=True)
            for j in range(n)]
    return cols[0] if n == 1 else jnp.concatenate(cols, axis=0)


def _adaln_kernel(cond_ref, w_ref, b_ref, o_ref):
    o_ref[...] = _dot3(jax.nn.silu(cond_ref[...]), w_ref[...]) + b_ref[...]


def _adaln(cond, w, b):
    n_out = w.shape[1]
    tn = D_MODEL
    return pl.pallas_call(
        _adaln_kernel,
        grid=(n_out // tn,),
        in_specs=[pl.BlockSpec((MOD_ROWS, D_MODEL), lambda j: (0, 0)),
                  pl.BlockSpec((D_MODEL, tn), lambda j: (0, j)),
                  pl.BlockSpec((1, tn), lambda j: (0, j))],
        out_specs=pl.BlockSpec((MOD_ROWS, tn), lambda j: (0, j)),
        out_shape=jax.ShapeDtypeStruct((MOD_ROWS, n_out), F32),
        compiler_params=_params("arbitrary"),
        name="adaln",
    )(cond, w, b)


def _inproj_kernel(x_ref, mod_ref, g_ref, w_ref, wg_ref, p_ref, gate_ref, *, ctx_row):
    row = pl.program_id(0) if ctx_row is None else ctx_row
    sh = mod_ref[pl.ds(row, 1), 0:D_MODEL]
    sc = mod_ref[pl.ds(row, 1), D_MODEL:2 * D_MODEL]
    for r in range(x_ref.shape[1] // ROW_GROUP):
        rows = slice(r * ROW_GROUP, (r + 1) * ROW_GROUP)
        x = x_ref[0, rows, :]
        y = x * lax.rsqrt(jnp.mean(x * x, axis=-1, keepdims=True) + EPS) * g_ref[...]
        h = (y * (1.0 + sc) + sh).astype(BF16)
        p_ref[0, rows, :] = _dot(h, w_ref[...])
        gate_ref[0, rows, :] = _dot(h, wg_ref[...])


def _inproj(x, mod, g, w, wg, tm, ctx_row):
    b, n, _ = x.shape
    wn = w.shape[1]
    return pl.pallas_call(
        functools.partial(_inproj_kernel, ctx_row=ctx_row),
        grid=(b, n // tm),
        in_specs=[pl.BlockSpec((1, tm, D_MODEL), lambda i, j: (i, j, 0)),
                  pl.BlockSpec(mod.shape, lambda i, j: (0, 0)),
                  pl.BlockSpec((1, D_MODEL), lambda i, j: (0, 0)),
                  pl.BlockSpec((D_MODEL, wn), lambda i, j: (0, 0)),
                  pl.BlockSpec((D_MODEL, GATE_PAD), lambda i, j: (0, 0))],
        out_specs=[pl.BlockSpec((1, tm, wn), lambda i, j: (i, j, 0)),
                   pl.BlockSpec((1, tm, GATE_PAD), lambda i, j: (i, j, 0))],
        out_shape=[jax.ShapeDtypeStruct((b, n, wn), F32),
                   jax.ShapeDtypeStruct((b, n, GATE_PAD), F32)],
        compiler_params=_params("arbitrary", "arbitrary"),
        name="inproj",
    )(x, mod, g, w, wg)


def _dwconv_kernel(*refs, n_tiles, scales):
    n_in = len(scales)
    j = pl.program_id(1)
    for s, scale in enumerate(scales):
        cur_ref, prev_ref, next_ref, w_ref = refs[4 * s:4 * s + 4]
        o_ref = refs[4 * n_in + s]
        cur = cur_ref[0]
        rows = cur.shape[0]
        prev_row = prev_ref[0, 7:8, :] * (j > 0).astype(F32)
        next_row = next_ref[0, 0:1, :] * (j < n_tiles - 1).astype(F32)
        t = _iota2((rows, 1), 0)
        before = jnp.where(t == 0, prev_row, pltpu.roll(cur, 1, axis=0))
        after = jnp.where(t == rows - 1, next_row, pltpu.roll(cur, rows - 1, axis=0))
        y = before * w_ref[0:1, :] + cur * w_ref[1:2, :] + after * w_ref[2:3, :]
        o_ref[0] = (jax.nn.silu(y) * scale).astype(BF16)


def _dwconv(p, streams, tm):
    b, n, _ = p.shape
    n_tiles = n // tm
    rows8 = tm // 8
    last8 = n // 8 - 1
    in_specs, args = [], []
    for blk, w, _ in streams:
        in_specs += [pl.BlockSpec((1, tm, GROUP_W), lambda i, j, blk=blk: (i, j, blk)),
                     pl.BlockSpec((1, 8, GROUP_W),
                                  lambda i, j, blk=blk: (i, jnp.maximum(j * rows8 - 1, 0), blk)),
                     pl.BlockSpec((1, 8, GROUP_W),
                                  lambda i, j, blk=blk: (i, jnp.minimum((j + 1) * rows8, last8), blk)),
                     pl.BlockSpec((3, GROUP_W), lambda i, j: (0, 0))]
        args += [p, p, p, w]
    return pl.pallas_call(
        functools.partial(_dwconv_kernel, n_tiles=n_tiles, scales=tuple(s for _, _, s in streams)),
        grid=(b, n_tiles),
        in_specs=in_specs,
        out_specs=[pl.BlockSpec((1, tm, GROUP_W), lambda i, j: (i, j, 0))] * len(streams),
        out_shape=[jax.ShapeDtypeStruct((b, n, GROUP_W), BF16)] * len(streams),
        compiler_params=_params("arbitrary", "arbitrary"),
        name="dwconv",
    )(*args)


def _mlstm_chunk(per_dir, bias_ref, c_s, n_s, m_s, h_refs):
    with_h = h_refs[0] is not None
    i0 = _iota2((CHUNK, CHUNK), 0)
    i1 = _iota2((CHUNK, CHUNK), 1)

    units = []
    for smp, d in [(smp, d) for smp in range(c_s.shape[0]) for d in range(2)]:
        q_ref, k_ref, v_ref, g_ref = per_dir[d]
        sees_ts = (i1 <= i0) if d == 0 else (i1 >= i0)
        sees_st = (i0 <= i1) if d == 0 else (i0 >= i1)
        tri = jnp.where(sees_ts, 1.0, 0.0).astype(BF16)
        gates = g_ref[smp] + bias_ref[...]
        bcum = _dot_exact01(tri, jax.nn.log_sigmoid(gates))
        for hd in range(HEADS):
            lanes = slice(hd * HEAD_DIM, (hd + 1) * HEAD_DIM)
            c_prev = c_s[smp, d, hd]
            n_prev = n_s[smp, d, hd]
            k = k_ref[smp, :, lanes]
            st = dict(smp=smp, d=d, hd=hd, lanes=lanes, c_prev=c_prev, n_prev=n_prev, k=k,
                      sees_st=sees_st, gates=gates, bcum=bcum, v=v_ref[smp, :, lanes])
            if with_h:
                lhs = jnp.concatenate([k, c_prev.astype(BF16),
                                       jnp.broadcast_to(n_prev, (16, HEAD_DIM)).astype(BF16)], axis=0)
                st["prod"] = _dot_nt(lhs, q_ref[smp, :, lanes])
            units.append(st)

    rows = {}
    for st in units:
        smp, d, hd = st["smp"], st["d"], st["hd"]
        if (smp, d) not in rows:
            rows[smp, d] = (st["gates"].T, st["bcum"].T)
        gates_t, bcum_t = rows[smp, d]
        ci = d * HEADS + hd
        cf = 2 * HEADS + d * HEADS + hd
        last = CHUNK - 1 if d == 0 else 0
        li_row = gates_t[ci:ci + 1, :]
        bc_row = bcum_t[cf:cf + 1, :]
        b_last = bc_row[:, last:last + 1]
        m_prev = m_s[smp, d, hd]
        v_t = st["v"].T
        a_row = b_last - bc_row + li_row
        m_new = jnp.maximum(b_last + m_prev, jnp.max(a_row, axis=-1, keepdims=True))
        w_row = jnp.exp(a_row - m_new)
        lhs = jnp.concatenate([v_t * w_row, jnp.broadcast_to(w_row, (16, CHUNK))], axis=0)
        st["upd"] = _dot(lhs.astype(BF16), st["k"])
        st.update(bc_row=bc_row, m_prev=m_prev, m_new=m_new, v_t=v_t,
                  decay=jnp.exp(b_last + m_prev - m_new),
                  u_col=st["gates"][:, ci:ci + 1] - st["bcum"][:, cf:cf + 1])

    if with_h:
        for st in units:
            prod, bc_row = st["prod"], st["bc_row"]
            g = bc_row + st["m_prev"]
            dmat = jnp.where(st["sees_st"], st["u_col"] + bc_row, -jnp.inf)
            m_t = jnp.maximum(g, jnp.max(dmat, axis=0, keepdims=True))
            inter = jnp.exp(g - m_t)
            s = prod[0:CHUNK] * jnp.exp(dmat - m_t)
            st["pv"] = _dot(st["v_t"].astype(BF16), s.astype(BF16))
            st["num0"] = inter * prod[CHUNK:2 * CHUNK]
            den = inter * prod[2 * CHUNK:2 * CHUNK + 1] + jnp.sum(s, axis=0, keepdims=True)
            st["scale"] = 1.0 / jnp.maximum(jnp.abs(den), jnp.exp(-m_t))

    for st in units:
        smp, d, hd = st["smp"], st["d"], st["hd"]
        if with_h:
            h_refs[d][smp, :, st["lanes"]] = ((st["num0"] + st["pv"]) * st["scale"]).T
        c_s[smp, d, hd] = st["decay"] * st["c_prev"] + st["upd"][0:HEAD_DIM]
        n_s[smp, d, hd] = st["decay"] * st["n_prev"] + st["upd"][HEAD_DIM:HEAD_DIM + 1]
        m_s[smp, d, hd] = st["m_new"]


def _mlstm_kernel(*refs, n_chunks, with_h):
    refs = list(refs)
    take = lambda n: [refs.pop(0) for _ in range(n)]
    per_dir = []
    for _ in range(2):
        q_ref = take(1)[0] if with_h else None
        k_ref, v_ref, g_ref = take(3)
        per_dir.append((q_ref, k_ref, v_ref, g_ref))
    bias_ref = take(1)[0]
    if with_h:
        c0_ref, n0_ref, m0_ref = take(3)
        h_refs = take(2)
        c_out = n_out = m_out = None
    else:
        h_refs = [None, None]
        c_out, n_out, m_out = take(3)
    c_s, n_s, m_s = take(3)

    j = pl.program_id(1)

    @pl.when(j == 0)
    def _():
        if with_h:
            c_s[...] = c0_ref[...]
            n_s[...] = n0_ref[...]
            m_s[...] = m0_ref[...]
        else:
            c_s[...] = jnp.zeros_like(c_s)
            n_s[...] = jnp.zeros_like(n_s)
            m_s[...] = jnp.zeros_like(m_s)

    _mlstm_chunk(per_dir, bias_ref, c_s, n_s, m_s, h_refs)

    if not with_h:
        @pl.when(j == n_chunks - 1)
        def _():
            c_out[...] = c_s[...]
            n_out[...] = n_s[...]
            m_out[...] = m_s[...]


def _mlstm(q, k, p, v_blk, gates, bias, state):
    b, n, _ = k.shape
    n_chunks = n // CHUNK
    with_h = q is not None
    ns = MLSTM_SAMPLES if b % MLSTM_SAMPLES == 0 else 1

    in_specs, args = [], []
    for d in range(2):
        c = (lambda j: j) if d == 0 else (lambda j: n_chunks - 1 - j)
        tok = pl.BlockSpec((ns, CHUNK, GROUP_W), lambda i, j, c=c: (i, c(j), 0))
        if with_h:
            in_specs.append(tok)
            args.append(q)
        in_specs += [tok,
                     pl.BlockSpec((ns, CHUNK, GROUP_W), lambda i, j, c=c: (i, c(j), v_blk)),
                     pl.BlockSpec((ns, CHUNK, GATE_PAD), lambda i, j, c=c: (i, c(j), 0))]
        args += [k, p, gates]
    in_specs.append(pl.BlockSpec((1, GATE_PAD), lambda i, j: (0, 0)))
    args.append(bias)

    c_shape = (2, HEADS, HEAD_DIM, HEAD_DIM)
    v_shape = (2, HEADS, 1, HEAD_DIM)
    c_spec = pl.BlockSpec((ns,) + c_shape, lambda i, j: (i, 0, 0, 0, 0))
    v_spec = pl.BlockSpec((ns,) + v_shape, lambda i, j: (i, 0, 0, 0, 0))
    if with_h:
        in_specs += [c_spec, v_spec, v_spec]
        args += list(state)
        out_specs = [pl.BlockSpec((ns, CHUNK, GROUP_W), lambda i, j: (i, j, 0)),
                     pl.BlockSpec((ns, CHUNK, GROUP_W), lambda i, j: (i, n_chunks - 1 - j, 0))]
        out_shape = [jax.ShapeDtypeStruct((b, n, GROUP_W), F32)] * 2
    else:
        out_specs = [c_spec, v_spec, v_spec]
        out_shape = [jax.ShapeDtypeStruct((b,) + c_shape, F32),
                     jax.ShapeDtypeStruct((b,) + v_shape, F32),
                     jax.ShapeDtypeStruct((b,) + v_shape, F32)]
    return pl.pallas_call(
        functools.partial(_mlstm_kernel, n_chunks=n_chunks, with_h=with_h),
        grid=(b // ns, n_chunks),
        in_specs=in_specs,
        out_specs=out_specs,
        out_shape=out_shape,
        scratch_shapes=[pltpu.VMEM((ns,) + c_shape, F32), pltpu.VMEM((ns,) + v_shape, F32),
                        pltpu.VMEM((ns,) + v_shape, F32)],
        compiler_params=_params("arbitrary", "arbitrary"),
        name="mlstm" if with_h else "mlstm_ctx_state",
    )(*args)


def _layer_norm(x):
    mu = jnp.mean(x, axis=-1, keepdims=True)
    var = jnp.mean(jnp.square(x - mu), axis=-1, keepdims=True)
    return (x - mu) * lax.rsqrt(var + EPS)


def _postmix_kernel(u_ref, vg_ref, o_ref, hf_ref, hb_ref, x_ref, mod_ref, ws_ref, bs_ref, ng_ref,
                    wout_ref, fg_ref, wr_ref, x1_ref, h2_ref, aff_ref, ycat_s):
    tm = x_ref.shape[1]
    b = pl.program_id(0)
    mod = lambda k: mod_ref[pl.ds(b, 1), k * D_MODEL:(k + 1) * D_MODEL]
    wr_hi, wr_lo = _split2(wr_ref[...])
    expert_lane = _iota2((ROW_GROUP, 128), 1) < N_EXPERTS

    for r in range(tm // ROW_GROUP):
        rows = slice(r * ROW_GROUP, (r + 1) * ROW_GROUP)

        for c in range(r * ROW_GROUP // CHUNK, (r + 1) * ROW_GROUP // CHUNK):
            crows = slice(c * CHUNK, (c + 1) * CHUNK)
            u = jax.nn.gelu(u_ref[0, crows, :])
            v = _layer_norm(jax.nn.gelu(vg_ref[0, crows, :])).astype(BF16)
            for hd in range(HEADS):
                lanes = slice(hd * HEAD_DIM, (hd + 1) * HEAD_DIM)
                s = _dot(ws_ref[hd], v[:, lanes]) + bs_ref[hd]
                ycat_s[crows, lanes] = (u[:, lanes] * s).astype(BF16)

        hsum = hf_ref[0, rows, :] + hb_ref[0, rows, :]
        o = jax.nn.sigmoid(o_ref[0, rows, :])
        for hd in range(HEADS):
            lanes = slice(hd * HEAD_DIM, (hd + 1) * HEAD_DIM)
            hn = _layer_norm(hsum[:, lanes]) * ng_ref[:, lanes]
            ycat_s[rows, GROUP_W + hd * HEAD_DIM:GROUP_W + (hd + 1) * HEAD_DIM] = (
                o[:, lanes] * hn).astype(BF16)

        y = _dot(ycat_s[rows, :], wout_ref[...])
        x1 = x_ref[0, rows, :] + mod(2) * y
        x1_ref[0, rows, :] = x1

        n2 = x1 * lax.rsqrt(jnp.mean(x1 * x1, axis=-1, keepdims=True) + EPS) * fg_ref[...]
        h2 = n2 * (1.0 + mod(4)) + mod(3)
        h2_hi, h2_lo = _split2(h2)
        h2_ref[0, rows, :] = h2_hi

        logits = _dot(h2_hi, wr_hi) + (_dot(h2_lo, wr_hi) + _dot(h2_hi, wr_lo))
        logits = jnp.where(expert_lane, logits, -jnp.inf)
        e = jnp.exp(logits - jnp.max(logits, axis=-1, keepdims=True))
        aff = e / jnp.sum(e, axis=-1, keepdims=True)
        aff_ref[0, :, rows] = aff.T[0:N_EXPERTS, :]


def _postmix(p, hf, hb, x, mod, ws, bs, ng, wout, fg, wr, tm):
    b, n, _ = x.shape
    tok = lambda blk: pl.BlockSpec((1, tm, GROUP_W), lambda i, j: (i, j, blk))
    full = lambda a: pl.BlockSpec(a.shape, lambda i, j: (0,) * a.ndim)
    return pl.pallas_call(
        _postmix_kernel,
        grid=(b, n // tm),
        in_specs=[tok(U_BLK), tok(VG_BLK), tok(O_BLK), tok(0), tok(0),
                  pl.BlockSpec((1, tm, D_MODEL), lambda i, j: (i, j, 0)),
                  full(mod), full(ws), full(bs), full(ng), full(wout), full(fg), full(wr)],
        out_specs=[pl.BlockSpec((1, tm, D_MODEL), lambda i, j: (i, j, 0)),
                   pl.BlockSpec((1, tm, D_MODEL), lambda i, j: (i, j, 0)),
                   pl.BlockSpec((1, N_EXPERTS, tm), lambda i, j: (i, 0, j))],
        out_shape=[jax.ShapeDtypeStruct((b, n, D_MODEL), F32),
                   jax.ShapeDtypeStruct((b, n, D_MODEL), BF16),
                   jax.ShapeDtypeStruct((b, N_EXPERTS, n), F32)],
        scratch_shapes=[pltpu.VMEM((tm, D_MODEL), BF16)],
        compiler_params=_params("arbitrary", "arbitrary"),
        name="postmix",
    )(p, p, p, hf, hb, x, mod, ws, bs, ng, wout, fg, wr)


def _cumsum_lanes(x, upper):
    carry = jnp.zeros((x.shape[0], 1), F32)
    outs, before = [], []
    for j in range(x.shape[1] // 128):
        before.append(carry)
        c = _dot(x[:, j * 128:(j + 1) * 128].astype(BF16), upper) + carry
        outs.append(c)
        carry = c[:, 127:128]
    return jnp.concatenate(outs, axis=1), before


def _rows_to_lanes(x, fill):
    pad = jnp.full((128 - x.shape[0], 128), fill, F32)
    return jnp.concatenate([x, pad], axis=0).T


def _route_kernel(aff_ref, slot_ref, w_ref, slot_t_ref, start_t_ref, first_ref, *, capacity, window):
    aff = aff_ref[0]
    cap = float(capacity)
    thr_bits = jnp.zeros((aff.shape[0], 1), jnp.int32)
    for bit in range(30, -1, -1):
        cand = thr_bits | (1 << bit)
        cnt = jnp.sum(jnp.where(aff >= pltpu.bitcast(cand, F32), 1.0, 0.0), axis=-1, keepdims=True)
        thr_bits = jnp.where(cnt >= cap, cand, thr_bits)
    thr = pltpu.bitcast(thr_bits, F32)
    upper = jnp.where(_iota2((128, 128), 0) <= _iota2((128, 128), 1), 1.0, 0.0).astype(BF16)
    above = jnp.where(aff > thr, 1.0, 0.0)
    tied = jnp.where(aff == thr, 1.0, 0.0)
    need = cap - jnp.sum(above, axis=-1, keepdims=True)
    sel = above + tied * jnp.where(_cumsum_lanes(tied, upper)[0] <= need, 1.0, 0.0)
    count, before = _cumsum_lanes(sel, upper)
    slot = jnp.where(sel > 0.0, count - 1.0, -1.0)
    slot_ref[0] = slot
    w_ref[0] = jnp.where(sel > 0.0, aff, 0.0)

    n_blocks = len(before)
    for j in range(n_blocks):
        slot_t_ref[0, j * 128:(j + 1) * 128, :] = _rows_to_lanes(slot[:, j * 128:(j + 1) * 128], -1.0)
    lane = _iota2((1, 128), 1)
    first = jnp.zeros((aff.shape[0], 128), F32)
    for j in range(n_blocks):
        first = jnp.where(lane == j, before[j], first)
    first_ref[0] = first
    start = jnp.minimum(jnp.floor(first * (1.0 / 16.0)) * 16.0, float(capacity - window))
    start_t_ref[0] = _rows_to_lanes(start, 0.0)[0:n_blocks, :]


def _route(aff_t, capacity, window):
    b, e, n = aff_t.shape
    assert n // 128 <= 128 and e <= 128 and window <= capacity
    spec = pl.BlockSpec((1, e, n), lambda i: (i, 0, 0))
    return pl.pallas_call(
        functools.partial(_route_kernel, capacity=capacity, window=window),
        grid=(b,),
        in_specs=[spec],
        out_specs=[spec, spec,
                   pl.BlockSpec((1, n, 128), lambda i: (i, 0, 0)),
                   pl.BlockSpec((1, n // 128, 128), lambda i: (i, 0, 0)),
                   pl.BlockSpec((1, e, 128), lambda i: (i, 0, 0))],
        out_shape=[jax.ShapeDtypeStruct((b, e, n), F32)] * 2
        + [jax.ShapeDtypeStruct((b, n, 128), F32), jax.ShapeDtypeStruct((b, n // 128, 128), F32),
           jax.ShapeDtypeStruct((b, e, 128), F32)],
        compiler_params=_params("arbitrary"),
        name="route",
    )(aff_t)


GATHER_TILE = 256
GATHER_ROWS = 128
GATHER_GROUP = 4


def _gather_kernel(before_ref, fits_ref, h2_ref, slot_ref, w_ref, xe_ref, gate_ref, xe_s, gate_s):
    b = pl.program_id(0)
    g = pl.program_id(1)
    n_g, cap = xe_ref.shape[1], xe_ref.shape[2]
    n = h2_ref.shape[1]
    n_tiles = n // GATHER_TILE

    @pl.when(fits_ref[b, g] != 0)
    def _():
        xe_s[:, 0:16, :] = jnp.zeros((n_g, 16, D_MODEL), BF16)
        gate_s[...] = jnp.zeros_like(gate_s)
        row_id = _iota2((GATHER_ROWS, 1), 0).astype(F32)
        for j in range(n_tiles):
            starts, onehots = [], []
            for e in range(n_g):
                start = pl.multiple_of((before_ref[b, g * n_g + e, j] // 16) * 16, 16)
                hit = (slot_ref[0, e, j:j + 1, :] - start.astype(F32)) == row_id
                onehots.append(jnp.where(hit, 1.0, 0.0).astype(BF16))
                gate_s[e, pl.ds(start, GATHER_ROWS), :] += jnp.sum(
                    jnp.where(hit, w_ref[0, e, j:j + 1, :], 0.0), axis=1, keepdims=True)
                starts.append(start)
            rows = _dot(jnp.concatenate(onehots, axis=0),
                        h2_ref[0, j * GATHER_TILE:(j + 1) * GATHER_TILE, :]).astype(BF16)
            for e, start in enumerate(starts):
                r0 = e * GATHER_ROWS
                xe_s[e, pl.ds(start, 16), :] += rows[r0:r0 + 16]
                xe_s[e, pl.ds(start + 16, GATHER_ROWS - 16), :] = rows[r0 + 16:r0 + GATHER_ROWS]
        xe_ref[0] = xe_s[:, 0:cap, :]
        gate_ref[0] = gate_s[:, 0:cap, :]

    @pl.when(fits_ref[b, g] == 0)
    def _():
        slot_id = _iota2((cap, 1), 0).astype(F32)
        for e in range(n_g):
            xe = jnp.zeros((cap, D_MODEL), F32)
            gate = jnp.zeros((cap, 1), F32)
            for j in range(n_tiles):
                hit = slot_ref[0, e, j:j + 1, :] == slot_id
                xe = xe + _dot(jnp.where(hit, 1.0, 0.0).astype(BF16),
                               h2_ref[0, j * GATHER_TILE:(j + 1) * GATHER_TILE, :])
                gate = gate + jnp.sum(jnp.where(hit, w_ref[0, e, j:j + 1, :], 0.0), axis=1,
                                      keepdims=True)
            xe_ref[0, e] = xe.astype(BF16)
            gate_ref[0, e] = gate


def _gather(h2, slot, w, first, capacity):
    b, n, _ = h2.shape
    n_e = slot.shape[1]
    n_tiles = n // GATHER_TILE
    assert n_e % GATHER_GROUP == 0 and GATHER_TILE % 128 == 0 and capacity % 16 == 0
    before = first[:, :, 0:n // 128:GATHER_TILE // 128].astype(jnp.int32)
    after = jnp.concatenate([before[:, :, 1:], jnp.full((b, n_e, 1), capacity, jnp.int32)], axis=2)
    fits = jnp.all((after - (before // 16) * 16) < GATHER_ROWS, axis=2)
    fits = jnp.all(fits.reshape(b, n_e // GATHER_GROUP, GATHER_GROUP), axis=2).astype(jnp.int32)
    tiles = lambda a: a.reshape(b, n_e, n_tiles, GATHER_TILE)
    rows = pl.BlockSpec((1, GATHER_GROUP, n_tiles, GATHER_TILE), lambda i, g, *_: (i, g, 0, 0))
    return pl.pallas_call(
        _gather_kernel,
        grid_spec=pltpu.PrefetchScalarGridSpec(
            num_scalar_prefetch=2,
            grid=(b, n_e // GATHER_GROUP),
            in_specs=[pl.BlockSpec((1, n, D_MODEL), lambda i, g, *_: (i, 0, 0)), rows, rows],
            out_specs=[pl.BlockSpec((1, GATHER_GROUP, capacity, D_MODEL), lambda i, g, *_: (i, g, 0, 0)),
                       pl.BlockSpec((1, GATHER_GROUP, capacity, 1), lambda i, g, *_: (i, g, 0, 0))],
            scratch_shapes=[pltpu.VMEM((GATHER_GROUP, capacity + GATHER_ROWS, D_MODEL), BF16),
                            pltpu.VMEM((GATHER_GROUP, capacity + GATHER_ROWS, 1), F32)]),
        out_shape=[jax.ShapeDtypeStruct((b, n_e, capacity, D_MODEL), BF16),
                   jax.ShapeDtypeStruct((b, n_e, capacity, 1), F32)],
        compiler_params=_params("arbitrary", "arbitrary"),
        name="gather",
    )(before, fits, h2, tiles(slot), tiles(w))


def _ffn_kernel(xe_ref, gate_ref, wg_ref, wu_ref, wd_ref, ye_ref, acc_s):
    ft = pl.program_id(1)
    last = pl.num_programs(1) - 1

    def step(first, final):
        wg = wg_ref[0].astype(BF16)
        wu = wu_ref[0].astype(BF16)
        wd = wd_ref[0].astype(BF16)
        for i in range(xe_ref.shape[0]):
            xe = xe_ref[i, 0]
            act = jax.nn.silu(_dot(xe, wg)) * _dot(xe, wu)
            part = _dot(act.astype(BF16), wd)
            total = part if first else acc_s[i] + part
            if final:
                ye_ref[i] = (total * gate_ref[i, 0]).astype(BF16)
            else:
                acc_s[i] = total

    pl.when(ft == 0)(lambda: step(True, False))
    pl.when(jnp.logical_and(ft > 0, ft < last))(lambda: step(False, False))
    pl.when(ft == last)(lambda: step(False, True))


def _ffn(xe, gate, wg, wu, wd, f_tile):
    b, n_e, cap, _ = xe.shape
    assert D_EXPERT // f_tile >= 2
    return pl.pallas_call(
        _ffn_kernel,
        grid=(n_e, D_EXPERT // f_tile),
        in_specs=[pl.BlockSpec((b, 1, cap, D_MODEL), lambda e, f: (0, e, 0, 0)),
                  pl.BlockSpec((b, 1, cap, 1), lambda e, f: (0, e, 0, 0)),
                  pl.BlockSpec((1, D_MODEL, f_tile), lambda e, f: (e, 0, f)),
                  pl.BlockSpec((1, D_MODEL, f_tile), lambda e, f: (e, 0, f)),
                  pl.BlockSpec((1, f_tile, D_MODEL), lambda e, f: (e, f, 0))],
        out_specs=pl.BlockSpec((b, cap, D_MODEL), lambda e, f: (0, e, 0)),
        out_shape=jax.ShapeDtypeStruct((b, n_e * cap, D_MODEL), BF16),
        scratch_shapes=[pltpu.VMEM((b, cap, D_MODEL), F32)],
        compiler_params=_params("arbitrary", "arbitrary"),
        name="ffn",
    )(xe, gate, wg, wu, wd)


def _combine_kernel(start_ref, slot_t_ref, start_t_ref, ye_ref, x1_ref, mod_ref, fg_ref, o_ref, acc_s,
                    *, capacity, window):
    b = pl.program_id(0)
    j = pl.program_id(1)
    n_sub = o_ref.shape[1] // 128
    k_total = N_EXPERTS * window
    lane = _iota2((1, 128), 1).astype(F32)
    for sb in range(n_sub):
        blk = j * n_sub + sb
        slot_t = slot_t_ref[0, sb * 128:(sb + 1) * 128, :]
        k_pos = jnp.where(slot_t >= 0.0, slot_t - start_t_ref[0, pl.ds(blk, 1), :] + lane * float(window), -1.0)
        cols = []
        for c in range(k_total // 128):
            k_lane = lane + float(128 * c)
            hit = jnp.zeros((128, 128), F32)
            for e in range((128 * c) // window, (128 * c + 127) // window + 1):
                hit = jnp.where(k_pos[:, e:e + 1] == k_lane, 1.0, hit)
            cols.append(hit.astype(BF16))
        onehot = jnp.concatenate(cols, axis=1)
        rows = [ye_ref[0, pl.ds(pl.multiple_of(e * capacity + start_ref[b, blk, e], 16), window), :]
                for e in range(N_EXPERTS)]
        acc_s[sb * 128:(sb + 1) * 128, :] = _dot(onehot, jnp.concatenate(rows, axis=0))
    g2 = mod_ref[pl.ds(b, 1), 5 * D_MODEL:6 * D_MODEL]
    x2 = x1_ref[0] + g2 * acc_s[...]
    o_ref[0] = x2 * lax.rsqrt(jnp.mean(x2 * x2, axis=-1, keepdims=True) + EPS) * fg_ref[...]


def _combine(start, slot_t, start_t, ye, x1, mod, fg, capacity, window, tm):
    b, n, _ = x1.shape
    assert (N_EXPERTS * window) % 128 == 0 and window % 16 == 0 and capacity % 16 == 0
    tok = pl.BlockSpec((1, tm, D_MODEL), lambda i, j, s: (i, j, 0))
    return pl.pallas_call(
        functools.partial(_combine_kernel, capacity=capacity, window=window),
        grid_spec=pltpu.PrefetchScalarGridSpec(
            num_scalar_prefetch=1,
            grid=(b, n // tm),
            in_specs=[pl.BlockSpec((1, tm, 128), lambda i, j, s: (i, j, 0)),
                      pl.BlockSpec((1, n // 128, 128), lambda i, j, s: (i, 0, 0)),
                      pl.BlockSpec((1, N_EXPERTS * capacity, D_MODEL), lambda i, j, s: (i, 0, 0)),
                      tok,
                      pl.BlockSpec(mod.shape, lambda i, j, s: (0, 0)),
                      pl.BlockSpec((1, D_MODEL), lambda i, j, s: (0, 0))],
            out_specs=tok,
            scratch_shapes=[pltpu.VMEM((tm, D_MODEL), F32)]),
        out_shape=jax.ShapeDtypeStruct((b, n, D_MODEL), F32),
        compiler_params=_params("arbitrary", "arbitrary"),
        name="combine",
    )(start, slot_t, start_t, ye, x1, mod, fg)


def kernel(x, c, ctx, c_ctx, w_mod, b_mod, norm_mix_g, w_in, conv_q, conv_k, b_igate, b_fgate,
           gmlp_ws, gmlp_bs, mlstm_norm_g, w_out, norm_ffn_g, w_router, w_gate_e, w_up_e,
           w_down_e, final_g):
    depth = w_mod.shape[0]
    assert depth == 1, "the context stream is only carried as mLSTM states (single layer)"
    batch, seq, _ = x.shape
    assert seq % GRID_W == 0 and seq % CHUNK == 0 and batch + 1 <= MOD_ROWS
    capacity = EC_FACTOR * seq // N_EXPERTS
    ctx_row = batch
    l = 0

    cond = jnp.concatenate([c, c_ctx[None], jnp.zeros((MOD_ROWS - batch - 1, D_MODEL), F32)], axis=0)
    mod = _adaln(cond, w_mod[l], b_mod[l][None])

    row = lambda a: a[None]
    w_main = w_in[l][:, :MAIN_W].astype(BF16)
    w_gates = jnp.pad(w_in[l][:, MAIN_W:], ((0, 0), (0, GATE_PAD - N_GATES))).astype(BF16)
    gate_bias = jnp.pad(jnp.concatenate([b_igate[l].reshape(-1), b_fgate[l].reshape(-1)]),
                        (0, GATE_PAD - N_GATES))[None]

    p_c, gates_c = _inproj(ctx, mod, row(norm_mix_g[l]), w_main[:, K_BLK * GROUP_W:], w_gates,
                           tm=ctx.shape[1], ctx_row=ctx_row)
    k_scale = HEAD_DIM ** -0.5
    (k_c,) = _dwconv(p_c, [(0, conv_k[l], k_scale)], tm=ctx.shape[1])
    state = _mlstm(None, k_c, p_c, 1, gates_c, gate_bias, None)

    p, gates = _inproj(x, mod, row(norm_mix_g[l]), w_main, w_gates, tm=512, ctx_row=None)
    q_l, k_l = _dwconv(p, [(Q_BLK, conv_q[l], 1.0), (K_BLK, conv_k[l], k_scale)], tm=512)
    h_f, h_b = _mlstm(q_l, k_l, p, V_BLK, gates, gate_bias, state)

    x1, h2, aff_t = _postmix(p, h_f, h_b, x, mod, gmlp_ws[l].astype(BF16), gmlp_bs[l][:, :, None],
                             row(mlstm_norm_g[l]), w_out[l].astype(BF16), row(norm_ffn_g[l]),
                             jnp.pad(w_router[l], ((0, 0), (0, 128 - N_EXPERTS))), tm=512)

    window = CHUNK + 16
    slot, gate_w, slot_t, start_t, first = _route(aff_t, capacity, window)

    xe, gate = _gather(h2, slot, gate_w, first, capacity)
    ye = _ffn(xe, gate, w_gate_e[l], w_up_e[l], w_down_e[l], f_tile=512)
    start = start_t[:, :, :N_EXPERTS].astype(jnp.int32)
    return _combine(start, slot_t, start_t, ye, x1, mod, row(final_g), capacity, window, tm=512)
```

```python
import functools

import jax
import jax.numpy as jnp
from jax import lax
from jax.experimental import pallas as pl
from jax.experimental.pallas import tpu as pltpu

F32 = jnp.float32
BF16 = jnp.bfloat16

D_MODEL = 1024
GRID_W = 64
CHUNK = 128
HEADS = 4
GROUP_W = D_MODEL // 2
HEAD_DIM = GROUP_W // HEADS
N_EXPERTS = 16
EC_FACTOR = 2
D_EXPERT = 2 * D_MODEL
EPS = 1e-6
N_GATES = 4 * HEADS
GATE_PAD = 128
MOD_ROWS = 8

U_BLK, VG_BLK, Q_BLK, O_BLK, K_BLK, V_BLK = 0, 1, 2, 3, 4, 5
MAIN_W = 6 * GROUP_W

VMEM_LIMIT = 56 * 1024 * 1024
ROW_GROUP = 2 * CHUNK
MLSTM_SAMPLES = 4

def _params(*sem):
    return pltpu.CompilerParams(dimension_semantics=sem, vmem_limit_bytes=VMEM_LIMIT)


def _dot(a, b):
    return jnp.dot(a, b, preferred_element_type=F32)


def _dot_nt(a, b):
    return lax.dot_general(a, b, (((1,), (1,)), ((), ())), preferred_element_type=F32)


def _split2(a):
    hi = a.astype(BF16)
    lo = (a - hi.astype(F32)).astype(BF16)
    return hi, lo


def _dot3(a, b):
    ah, al = _split2(a)
    bh, bl = _split2(b)
    return _dot(ah, bh) + (_dot(al, bh) + _dot(ah, bl))


def _dot_exact01(tri, x):
    x1 = x.astype(BF16)
    r1 = x - x1.astype(F32)
    x2 = r1.astype(BF16)
    x3 = (r1 - x2.astype(F32)).astype(BF16)
    return _dot(tri, x1) + (_dot(tri, x2) + _dot(tri, x3))


def _iota2(shape, dim):
    return lax.broadcasted_iota(jnp.int32, shape, dim)


def _row_to_col(row):
    n = row.shape[1] // 128
    eye = _iota2((128, 128), 0) == _iota2((128, 128), 1)
    cols = [jnp.sum(jnp.where(eye, row[:, j * 128:(j + 1) * 128], 0.0), axis=1, keepdims=True)
            for j in range(n)]
    return cols[0] if n == 1 else jnp.concatenate(cols, axis=0)


def _adaln_kernel(cond_ref, w_ref, b_ref, o_ref):
    o_ref[...] = _dot3(jax.nn.silu(cond_ref[...]), w_ref[...]) + b_ref[...]


def _adaln(cond, w, b):
    n_out = w.shape[1]
    tn = D_MODEL
    return pl.pallas_call(
        _adaln_kernel,
        grid=(n_out // tn,),
        in_specs=[pl.BlockSpec((MOD_ROWS, D_MODEL), lambda j: (0, 0)),
                  pl.BlockSpec((D_MODEL, tn), lambda j: (0, j)),
                  pl.BlockSpec((1, tn), lambda j: (0, j))],
        out_specs=pl.BlockSpec((MOD_ROWS, tn), lambda j: (0, j)),
        out_shape=jax.ShapeDtypeStruct((MOD_ROWS, n_out), F32),
        compiler_params=_params("arbitrary"),
        name="adaln",
    )(cond, w, b)


def _inproj_kernel(x_ref, mod_ref, g_ref, w_ref, wg_ref, p_ref, gate_ref, *, ctx_row):
    row = pl.program_id(0) if ctx_row is None else ctx_row
    sh = mod_ref[pl.ds(row, 1), 0:D_MODEL]
    sc = mod_ref[pl.ds(row, 1), D_MODEL:2 * D_MODEL]
    for r in range(x_ref.shape[1] // ROW_GROUP):
        rows = slice(r * ROW_GROUP, (r + 1) * ROW_GROUP)
        x = x_ref[0, rows, :]
        y = x * lax.rsqrt(jnp.mean(x * x, axis=-1, keepdims=True) + EPS) * g_ref[...]
        h = (y * (1.0 + sc) + sh).astype(BF16)
        p_ref[0, rows, :] = _dot(h, w_ref[...])
        gate_ref[0, rows, :] = _dot(h, wg_ref[...])


def _inproj(x, mod, g, w, wg, tm, ctx_row):
    b, n, _ = x.shape
    wn = w.shape[1]
    return pl.pallas_call(
        functools.partial(_inproj_kernel, ctx_row=ctx_row),
        grid=(b, n // tm),
        in_specs=[pl.BlockSpec((1, tm, D_MODEL), lambda i, j: (i, j, 0)),
                  pl.BlockSpec(mod.shape, lambda i, j: (0, 0)),
                  pl.BlockSpec((1, D_MODEL), lambda i, j: (0, 0)),
                  pl.BlockSpec((D_MODEL, wn), lambda i, j: (0, 0)),
                  pl.BlockSpec((D_MODEL, GATE_PAD), lambda i, j: (0, 0))],
        out_specs=[pl.BlockSpec((1, tm, wn), lambda i, j: (i, j, 0)),
                   pl.BlockSpec((1, tm, GATE_PAD), lambda i, j: (i, j, 0))],
        out_shape=[jax.ShapeDtypeStruct((b, n, wn), F32),
                   jax.ShapeDtypeStruct((b, n, GATE_PAD), F32)],
        compiler_params=_params("arbitrary", "arbitrary"),
        name="inproj",
    )(x, mod, g, w, wg)


def _dwconv_kernel(*refs, n_tiles, scales):
    n_in = len(scales)
    j = pl.program_id(1)
    for s, scale in enumerate(scales):
        cur_ref, prev_ref, next_ref, w_ref = refs[4 * s:4 * s + 4]
        o_ref = refs[4 * n_in + s]
        cur = cur_ref[0]
        rows = cur.shape[0]
        prev_row = prev_ref[0, 7:8, :] * (j > 0).astype(F32)
        next_row = next_ref[0, 0:1, :] * (j < n_tiles - 1).astype(F32)
        t = _iota2((rows, 1), 0)
        before = jnp.where(t == 0, prev_row, pltpu.roll(cur, 1, axis=0))
        after = jnp.where(t == rows - 1, next_row, pltpu.roll(cur, rows - 1, axis=0))
        y = before * w_ref[0:1, :] + cur * w_ref[1:2, :] + after * w_ref[2:3, :]
        o_ref[0] = (jax.nn.silu(y) * scale).astype(BF16)


def _dwconv(p, streams, tm):
    b, n, _ = p.shape
    n_tiles = n // tm
    rows8 = tm // 8
    last8 = n // 8 - 1
    in_specs, args = [], []
    for blk, w, _ in streams:
        in_specs += [pl.BlockSpec((1, tm, GROUP_W), lambda i, j, blk=blk: (i, j, blk)),
                     pl.BlockSpec((1, 8, GROUP_W),
                                  lambda i, j, blk=blk: (i, jnp.maximum(j * rows8 - 1, 0), blk)),
                     pl.BlockSpec((1, 8, GROUP_W),
                                  lambda i, j, blk=blk: (i, jnp.minimum((j + 1) * rows8, last8), blk)),
                     pl.BlockSpec((3, GROUP_W), lambda i, j: (0, 0))]
        args += [p, p, p, w]
    return pl.pallas_call(
        functools.partial(_dwconv_kernel, n_tiles=n_tiles, scales=tuple(s for _, _, s in streams)),
        grid=(b, n_tiles),
        in_specs=in_specs,
        out_specs=[pl.BlockSpec((1, tm, GROUP_W), lambda i, j: (i, j, 0))] * len(streams),
        out_shape=[jax.ShapeDtypeStruct((b, n, GROUP_W), BF16)] * len(streams),
        compiler_params=_params("arbitrary", "arbitrary"),
        name="dwconv",
    )(*args)


def _mlstm_chunk(per_dir, bias_ref, c_s, n_s, m_s, h_refs):
    with_h = h_refs[0] is not None
    i0 = _iota2((CHUNK, CHUNK), 0)
    i1 = _iota2((CHUNK, CHUNK), 1)

    units = []
    for smp, d in [(smp, d) for smp in range(c_s.shape[0]) for d in range(2)]:
        q_ref, k_ref, v_ref, g_ref = per_dir[d]
        sees_ts = (i1 <= i0) if d == 0 else (i1 >= i0)
        sees_st = (i0 <= i1) if d == 0 else (i0 >= i1)
        tri = jnp.where(sees_ts, 1.0, 0.0).astype(BF16)
        gates = g_ref[smp] + bias_ref[...]
        bcum = _dot_exact01(tri, jax.nn.log_sigmoid(gates))
        for hd in range(HEADS):
            lanes = slice(hd * HEAD_DIM, (hd + 1) * HEAD_DIM)
            c_prev = c_s[smp, d, hd]
            n_prev = n_s[smp, d, hd]
            k = k_ref[smp, :, lanes]
            st = dict(smp=smp, d=d, hd=hd, lanes=lanes, c_prev=c_prev, n_prev=n_prev, k=k,
                      sees_st=sees_st, gates=gates, bcum=bcum, v=v_ref[smp, :, lanes])
            if with_h:
                lhs = jnp.concatenate([k, c_prev.astype(BF16),
                                       jnp.broadcast_to(n_prev, (16, HEAD_DIM)).astype(BF16)], axis=0)
                st["prod"] = _dot_nt(lhs, q_ref[smp, :, lanes])
            units.append(st)

    rows = {}
    for st in units:
        smp, d, hd = st["smp"], st["d"], st["hd"]
        if (smp, d) not in rows:
            rows[smp, d] = (st["gates"].T, st["bcum"].T)
        gates_t, bcum_t = rows[smp, d]
        ci = d * HEADS + hd
        cf = 2 * HEADS + d * HEADS + hd
        last = CHUNK - 1 if d == 0 else 0
        li_row = gates_t[ci:ci + 1, :]
        bc_row = bcum_t[cf:cf + 1, :]
        b_last = bc_row[:, last:last + 1]
        m_prev = m_s[smp, d, hd]
        v_t = st["v"].T
        a_row = b_last - bc_row + li_row
        m_new = jnp.maximum(b_last + m_prev, jnp.max(a_row, axis=-1, keepdims=True))
        w_row = jnp.exp(a_row - m_new)
        lhs = jnp.concatenate([v_t * w_row, jnp.broadcast_to(w_row, (16, CHUNK))], axis=0)
        st["upd"] = _dot(lhs.astype(BF16), st["k"])
        st.update(bc_row=bc_row, m_prev=m_prev, m_new=m_new, v_t=v_t,
                  decay=jnp.exp(b_last + m_prev - m_new),
                  u_col=st["gates"][:, ci:ci + 1] - st["bcum"][:, cf:cf + 1])

    if with_h:
        for st in units:
            prod, bc_row = st["prod"], st["bc_row"]
            g = bc_row + st["m_prev"]
            dmat = jnp.where(st["sees_st"], st["u_col"] + bc_row, -jnp.inf)
            m_t = jnp.maximum(g, jnp.max(dmat, axis=0, keepdims=True))
            inter = jnp.exp(g - m_t)
            s = prod[0:CHUNK] * jnp.exp(dmat - m_t)
            st["pv"] = _dot(st["v_t"].astype(BF16), s.astype(BF16))
            st["num0"] = inter * prod[CHUNK:2 * CHUNK]
            den = inter * prod[2 * CHUNK:2 * CHUNK + 1] + jnp.sum(s, axis=0, keepdims=True)
            st["scale"] = 1.0 / jnp.maximum(jnp.abs(den), jnp.exp(-m_t))

    for st in units:
        smp, d, hd = st["smp"], st["d"], st["hd"]
        if with_h:
            h_refs[d][smp, :, st["lanes"]] = ((st["num0"] + st["pv"]) * st["scale"]).T
        c_s[smp, d, hd] = st["decay"] * st["c_prev"] + st["upd"][0:HEAD_DIM]
        n_s[smp, d, hd] = st["decay"] * st["n_prev"] + st["upd"][HEAD_DIM:HEAD_DIM + 1]
        m_s[smp, d, hd] = st["m_new"]


def _mlstm_kernel(*refs, n_chunks, with_h):
    refs = list(refs)
    take = lambda n: [refs.pop(0) for _ in range(n)]
    per_dir = []
    for _ in range(2):
        q_ref = take(1)[0] if with_h else None
        k_ref, v_ref, g_ref = take(3)
        per_dir.append((q_ref, k_ref, v_ref, g_ref))
    bias_ref = take(1)[0]
    if with_h:
        c0_ref, n0_ref, m0_ref = take(3)
        h_refs = take(2)
        c_out = n_out = m_out = None
    else:
        h_refs = [None, None]
        c_out, n_out, m_out = take(3)
    c_s, n_s, m_s = take(3)

    j = pl.program_id(1)

    @pl.when(j == 0)
    def _():
        if with_h:
            c_s[...] = c0_ref[...]
            n_s[...] = n0_ref[...]
            m_s[...] = m0_ref[...]
        else:
            c_s[...] = jnp.zeros_like(c_s)
            n_s[...] = jnp.zeros_like(n_s)
            m_s[...] = jnp.zeros_like(m_s)

    _mlstm_chunk(per_dir, bias_ref, c_s, n_s, m_s, h_refs)

    if not with_h:
        @pl.when(j == n_chunks - 1)
        def _():
            c_out[...] = c_s[...]
            n_out[...] = n_s[...]
            m_out[...] = m_s[...]


def _mlstm(q, k, p, v_blk, gates, bias, state):
    b, n, _ = k.shape
    n_chunks = n // CHUNK
    with_h = q is not None
    ns = MLSTM_SAMPLES if b % MLSTM_SAMPLES == 0 else 1

    in_specs, args = [], []
    for d in range(2):
        c = (lambda j: j) if d == 0 else (lambda j: n_chunks - 1 - j)
        tok = pl.BlockSpec((ns, CHUNK, GROUP_W), lambda i, j, c=c: (i, c(j), 0))
        if with_h:
            in_specs.append(tok)
            args.append(q)
        in_specs += [tok,
                     pl.BlockSpec((ns, CHUNK, GROUP_W), lambda i, j, c=c: (i, c(j), v_blk)),
                     pl.BlockSpec((ns, CHUNK, GATE_PAD), lambda i, j, c=c: (i, c(j), 0))]
        args += [k, p, gates]
    in_specs.append(pl.BlockSpec((1, GATE_PAD), lambda i, j: (0, 0)))
    args.append(bias)

    c_shape = (2, HEADS, HEAD_DIM, HEAD_DIM)
    v_shape = (2, HEADS, 1, HEAD_DIM)
    c_spec = pl.BlockSpec((ns,) + c_shape, lambda i, j: (i, 0, 0, 0, 0))
    v_spec = pl.BlockSpec((ns,) + v_shape, lambda i, j: (i, 0, 0, 0, 0))
    if with_h:
        in_specs += [c_spec, v_spec, v_spec]
        args += list(state)
        out_specs = [pl.BlockSpec((ns, CHUNK, GROUP_W), lambda i, j: (i, j, 0)),
                     pl.BlockSpec((ns, CHUNK, GROUP_W), lambda i, j: (i, n_chunks - 1 - j, 0))]
        out_shape = [jax.ShapeDtypeStruct((b, n, GROUP_W), F32)] * 2
    else:
        out_specs = [c_spec, v_spec, v_spec]
        out_shape = [jax.ShapeDtypeStruct((b,) + c_shape, F32),
                     jax.ShapeDtypeStruct((b,) + v_shape, F32),
                     jax.ShapeDtypeStruct((b,) + v_shape, F32)]
    return pl.pallas_call(
        functools.partial(_mlstm_kernel, n_chunks=n_chunks, with_h=with_h),
        grid=(b // ns, n_chunks),
        in_specs=in_specs,
        out_specs=out_specs,
        out_shape=out_shape,
        scratch_shapes=[pltpu.VMEM((ns,) + c_shape, F32), pltpu.VMEM((ns,) + v_shape, F32),
                        pltpu.VMEM((ns,) + v_shape, F32)],
        compiler_params=_params("arbitrary", "arbitrary"),
        name="mlstm" if with_h else "mlstm_ctx_state",
    )(*args)


def _layer_norm(x):
    mu = jnp.mean(x, axis=-1, keepdims=True)
    var = jnp.mean(jnp.square(x - mu), axis=-1, keepdims=True)
    return (x - mu) * lax.rsqrt(var + EPS)


def _postmix_kernel(u_ref, vg_ref, o_ref, hf_ref, hb_ref, x_ref, mod_ref, ws_ref, bs_ref, ng_ref,
                    wout_ref, fg_ref, wr_ref, x1_ref, h2_ref, aff_ref, ycat_s):
    tm = x_ref.shape[1]
    b = pl.program_id(0)
    mod = lambda k: mod_ref[pl.ds(b, 1), k * D_MODEL:(k + 1) * D_MODEL]
    wr_hi, wr_lo = _split2(wr_ref[...])
    expert_lane = _iota2((ROW_GROUP, 128), 1) < N_EXPERTS

    for r in range(tm // ROW_GROUP):
        rows = slice(r * ROW_GROUP, (r + 1) * ROW_GROUP)

        for c in range(r * ROW_GROUP // CHUNK, (r + 1) * ROW_GROUP // CHUNK):
            crows = slice(c * CHUNK, (c + 1) * CHUNK)
            u = jax.nn.gelu(u_ref[0, crows, :])
            v = _layer_norm(jax.nn.gelu(vg_ref[0, crows, :])).astype(BF16)
            for hd in range(HEADS):
                lanes = slice(hd * HEAD_DIM, (hd + 1) * HEAD_DIM)
                s = _dot(ws_ref[hd], v[:, lanes]) + bs_ref[hd]
                ycat_s[crows, lanes] = (u[:, lanes] * s).astype(BF16)

        hsum = hf_ref[0, rows, :] + hb_ref[0, rows, :]
        o = jax.nn.sigmoid(o_ref[0, rows, :])
        for hd in range(HEADS):
            lanes = slice(hd * HEAD_DIM, (hd + 1) * HEAD_DIM)
            hn = _layer_norm(hsum[:, lanes]) * ng_ref[:, lanes]
            ycat_s[rows, GROUP_W + hd * HEAD_DIM:GROUP_W + (hd + 1) * HEAD_DIM] = (
                o[:, lanes] * hn).astype(BF16)

        y = _dot(ycat_s[rows, :], wout_ref[...])
        x1 = x_ref[0, rows, :] + mod(2) * y
        x1_ref[0, rows, :] = x1

        n2 = x1 * lax.rsqrt(jnp.mean(x1 * x1, axis=-1, keepdims=True) + EPS) * fg_ref[...]
        h2 = n2 * (1.0 + mod(4)) + mod(3)
        h2_hi, h2_lo = _split2(h2)
        h2_ref[0, rows, :] = h2_hi

        logits = _dot(h2_hi, wr_hi) + (_dot(h2_lo, wr_hi) + _dot(h2_hi, wr_lo))
        logits = jnp.where(expert_lane, logits, -jnp.inf)
        e = jnp.exp(logits - jnp.max(logits, axis=-1, keepdims=True))
        aff = e / jnp.sum(e, axis=-1, keepdims=True)
        aff_ref[0, :, rows] = aff.T[0:N_EXPERTS, :]


def _postmix(p, hf, hb, x, mod, ws, bs, ng, wout, fg, wr, tm):
    b, n, _ = x.shape
    tok = lambda blk: pl.BlockSpec((1, tm, GROUP_W), lambda i, j: (i, j, blk))
    full = lambda a: pl.BlockSpec(a.shape, lambda i, j: (0,) * a.ndim)
    return pl.pallas_call(
        _postmix_kernel,
        grid=(b, n // tm),
        in_specs=[tok(U_BLK), tok(VG_BLK), tok(O_BLK), tok(0), tok(0),
                  pl.BlockSpec((1, tm, D_MODEL), lambda i, j: (i, j, 0)),
                  full(mod), full(ws), full(bs), full(ng), full(wout), full(fg), full(wr)],
        out_specs=[pl.BlockSpec((1, tm, D_MODEL), lambda i, j: (i, j, 0)),
                   pl.BlockSpec((1, tm, D_MODEL), lambda i, j: (i, j, 0)),
                   pl.BlockSpec((1, N_EXPERTS, tm), lambda i, j: (i, 0, j))],
        out_shape=[jax.ShapeDtypeStruct((b, n, D_MODEL), F32),
                   jax.ShapeDtypeStruct((b, n, D_MODEL), BF16),
                   jax.ShapeDtypeStruct((b, N_EXPERTS, n), F32)],
        scratch_shapes=[pltpu.VMEM((tm, D_MODEL), BF16)],
        compiler_params=_params("arbitrary", "arbitrary"),
        name="postmix",
    )(p, p, p, hf, hb, x, mod, ws, bs, ng, wout, fg, wr)


def _cumsum_lanes(x, upper):
    carry = jnp.zeros((x.shape[0], 1), F32)
    outs, before = [], []
    for j in range(x.shape[1] // 128):
        before.append(carry)
        c = _dot(x[:, j * 128:(j + 1) * 128].astype(BF16), upper) + carry
        outs.append(c)
        carry = c[:, 127:128]
    return jnp.concatenate(outs, axis=1), before


def _rows_to_lanes(x, fill):
    pad = jnp.full((128 - x.shape[0], 128), fill, F32)
    return jnp.concatenate([x, pad], axis=0).T


def _route_kernel(aff_ref, slot_ref, w_ref, slot_t_ref, first_t_ref, first_ref, *, capacity):
    aff = aff_ref[0]
    cap = float(capacity)
    thr_bits = jnp.zeros((aff.shape[0], 1), jnp.int32)
    for bit in range(30, -1, -1):
        cand = thr_bits | (1 << bit)
        cnt = jnp.sum(jnp.where(aff >= pltpu.bitcast(cand, F32), 1.0, 0.0), axis=-1, keepdims=True)
        thr_bits = jnp.where(cnt >= cap, cand, thr_bits)
    thr = pltpu.bitcast(thr_bits, F32)
    upper = jnp.where(_iota2((128, 128), 0) <= _iota2((128, 128), 1), 1.0, 0.0).astype(BF16)
    above = jnp.where(aff > thr, 1.0, 0.0)
    tied = jnp.where(aff == thr, 1.0, 0.0)
    need = cap - jnp.sum(above, axis=-1, keepdims=True)
    sel = above + tied * jnp.where(_cumsum_lanes(tied, upper)[0] <= need, 1.0, 0.0)
    count, before = _cumsum_lanes(sel, upper)
    slot = jnp.where(sel > 0.0, count - 1.0, -1.0)
    slot_ref[0] = slot
    w_ref[0] = jnp.where(sel > 0.0, aff, 0.0)

    n_blocks = len(before)
    for j in range(n_blocks):
        slot_t_ref[0, j * 128:(j + 1) * 128, :] = _rows_to_lanes(slot[:, j * 128:(j + 1) * 128], -1.0)
    lane = _iota2((1, 128), 1)
    first = jnp.zeros((aff.shape[0], 128), F32)
    for j in range(n_blocks):
        first = jnp.where(lane == j, before[j], first)
    first_ref[0] = first
    first_t_ref[0] = _rows_to_lanes(first, 0.0)[0:n_blocks, :]


def _route(aff_t, capacity):
    b, e, n = aff_t.shape
    assert n // 128 <= 128 and e <= 128
    spec = pl.BlockSpec((1, e, n), lambda i: (i, 0, 0))
    return pl.pallas_call(
        functools.partial(_route_kernel, capacity=capacity),
        grid=(b,),
        in_specs=[spec],
        out_specs=[spec, spec,
                   pl.BlockSpec((1, n, 128), lambda i: (i, 0, 0)),
                   pl.BlockSpec((1, n // 128, 128), lambda i: (i, 0, 0)),
                   pl.BlockSpec((1, e, 128), lambda i: (i, 0, 0))],
        out_shape=[jax.ShapeDtypeStruct((b, e, n), F32)] * 2
        + [jax.ShapeDtypeStruct((b, n, 128), F32), jax.ShapeDtypeStruct((b, n // 128, 128), F32),
           jax.ShapeDtypeStruct((b, e, 128), F32)],
        compiler_params=_params("arbitrary"),
        name="route",
    )(aff_t)


GATHER_TILE = 256
GATHER_ROWS = 128
GATHER_GROUP = 4


def _gather_kernel(before_ref, fits_ref, h2_ref, slot_ref, w_ref, xe_ref, gate_ref, xe_s, gate_s):
    b = pl.program_id(0)
    g = pl.program_id(1)
    n_g, cap = xe_ref.shape[1], xe_ref.shape[2]
    n = h2_ref.shape[1]
    n_tiles = n // GATHER_TILE

    @pl.when(fits_ref[b, g] != 0)
    def _():
        xe_s[:, 0:16, :] = jnp.zeros((n_g, 16, D_MODEL), BF16)
        gate_s[...] = jnp.zeros_like(gate_s)
        row_id = _iota2((GATHER_ROWS, 1), 0).astype(F32)
        for j in range(n_tiles):
            starts, onehots = [], []
            for e in range(n_g):
                start = pl.multiple_of((before_ref[b, g * n_g + e, j] // 16) * 16, 16)
                hit = (slot_ref[0, e, j:j + 1, :] - start.astype(F32)) == row_id
                onehots.append(jnp.where(hit, 1.0, 0.0).astype(BF16))
                gate_s[e, pl.ds(start, GATHER_ROWS), :] += jnp.sum(
                    jnp.where(hit, w_ref[0, e, j:j + 1, :], 0.0), axis=1, keepdims=True)
                starts.append(start)
            rows = _dot(jnp.concatenate(onehots, axis=0),
                        h2_ref[0, j * GATHER_TILE:(j + 1) * GATHER_TILE, :]).astype(BF16)
            for e, start in enumerate(starts):
                r0 = e * GATHER_ROWS
                xe_s[e, pl.ds(start, 16), :] += rows[r0:r0 + 16]
                xe_s[e, pl.ds(start + 16, GATHER_ROWS - 16), :] = rows[r0 + 16:r0 + GATHER_ROWS]
        xe_ref[0] = xe_s[:, 0:cap, :]
        gate_ref[0] = gate_s[:, 0:cap, :]

    @pl.when(fits_ref[b, g] == 0)
    def _():
        slot_id = _iota2((cap, 1), 0).astype(F32)
        for e in range(n_g):
            xe = jnp.zeros((cap, D_MODEL), F32)
            gate = jnp.zeros((cap, 1), F32)
            for j in range(n_tiles):
                hit = slot_ref[0, e, j:j + 1, :] == slot_id
                xe = xe + _dot(jnp.where(hit, 1.0, 0.0).astype(BF16),
                               h2_ref[0, j * GATHER_TILE:(j + 1) * GATHER_TILE, :])
                gate = gate + jnp.sum(jnp.where(hit, w_ref[0, e, j:j + 1, :], 0.0), axis=1,
                                      keepdims=True)
            xe_ref[0, e] = xe.astype(BF16)
            gate_ref[0, e] = gate


def _gather(h2, slot, w, first, capacity):
    b, n, _ = h2.shape
    n_e = slot.shape[1]
    n_tiles = n // GATHER_TILE
    assert n_e % GATHER_GROUP == 0 and GATHER_TILE % 128 == 0 and capacity % 16 == 0
    before = first[:, :, 0:n // 128:GATHER_TILE // 128].astype(jnp.int32)
    after = jnp.concatenate([before[:, :, 1:], jnp.full((b, n_e, 1), capacity, jnp.int32)], axis=2)
    fits = jnp.all((after - (before // 16) * 16) < GATHER_ROWS, axis=2)
    fits = jnp.all(fits.reshape(b, n_e // GATHER_GROUP, GATHER_GROUP), axis=2).astype(jnp.int32)
    tiles = lambda a: a.reshape(b, n_e, n_tiles, GATHER_TILE)
    rows = pl.BlockSpec((1, GATHER_GROUP, n_tiles, GATHER_TILE), lambda i, g, *_: (i, g, 0, 0))
    return pl.pallas_call(
        _gather_kernel,
        grid_spec=pltpu.PrefetchScalarGridSpec(
            num_scalar_prefetch=2,
            grid=(b, n_e // GATHER_GROUP),
            in_specs=[pl.BlockSpec((1, n, D_MODEL), lambda i, g, *_: (i, 0, 0)), rows, rows],
            out_specs=[pl.BlockSpec((1, GATHER_GROUP, capacity, D_MODEL), lambda i, g, *_: (i, g, 0, 0)),
                       pl.BlockSpec((1, GATHER_GROUP, capacity, 1), lambda i, g, *_: (i, g, 0, 0))],
            scratch_shapes=[pltpu.VMEM((GATHER_GROUP, capacity + GATHER_ROWS, D_MODEL), BF16),
                            pltpu.VMEM((GATHER_GROUP, capacity + GATHER_ROWS, 1), F32)]),
        out_shape=[jax.ShapeDtypeStruct((b, n_e, capacity, D_MODEL), BF16),
                   jax.ShapeDtypeStruct((b, n_e, capacity, 1), F32)],
        compiler_params=_params("arbitrary", "arbitrary"),
        name="gather",
    )(before, fits, h2, tiles(slot), tiles(w))


def _ffn_kernel(xe_ref, gate_ref, wg_ref, wu_ref, wd_ref, ye_ref, acc_s):
    ft = pl.program_id(1)
    last = pl.num_programs(1) - 1

    def step(first, final):
        wg = wg_ref[0].astype(BF16)
        wu = wu_ref[0].astype(BF16)
        wd = wd_ref[0].astype(BF16)
        for i in range(xe_ref.shape[0]):
            xe = xe_ref[i, 0]
            act = jax.nn.silu(_dot(xe, wg)) * _dot(xe, wu)
            part = _dot(act.astype(BF16), wd)
            total = part if first else acc_s[i] + part
            if final:
                ye_ref[i] = (total * gate_ref[i, 0]).astype(BF16)
            else:
                acc_s[i] = total

    pl.when(ft == 0)(lambda: step(True, False))
    pl.when(jnp.logical_and(ft > 0, ft < last))(lambda: step(False, False))
    pl.when(ft == last)(lambda: step(False, True))


def _ffn(xe, gate, wg, wu, wd, f_tile):
    b, n_e, cap, _ = xe.shape
    assert D_EXPERT // f_tile >= 2
    return pl.pallas_call(
        _ffn_kernel,
        grid=(n_e, D_EXPERT // f_tile),
        in_specs=[pl.BlockSpec((b, 1, cap, D_MODEL), lambda e, f: (0, e, 0, 0)),
                  pl.BlockSpec((b, 1, cap, 1), lambda e, f: (0, e, 0, 0)),
                  pl.BlockSpec((1, D_MODEL, f_tile), lambda e, f: (e, 0, f)),
                  pl.BlockSpec((1, D_MODEL, f_tile), lambda e, f: (e, 0, f)),
                  pl.BlockSpec((1, f_tile, D_MODEL), lambda e, f: (e, f, 0))],
        out_specs=pl.BlockSpec((b, cap, D_MODEL), lambda e, f: (0, e, 0)),
        out_shape=jax.ShapeDtypeStruct((b, n_e * cap, D_MODEL), BF16),
        scratch_shapes=[pltpu.VMEM((b, cap, D_MODEL), F32)],
        compiler_params=_params("arbitrary", "arbitrary"),
        name="ffn",
    )(xe, gate, wg, wu, wd)


COMBINE_WIDE = CHUNK + 16
COMBINE_NARROW = 48


def _combine_kernel(first_ref, narrow_ref, slot_t_ref, first_t_ref, ye_ref, x1_ref, mod_ref, fg_ref,
                    o_ref, acc_s, *, capacity):
    b = pl.program_id(0)
    j = pl.program_id(1)
    n_sub = o_ref.shape[1] // 128
    lane = _iota2((1, 128), 1).astype(F32)

    def scatter(window):
        k_total = N_EXPERTS * window
        for sb in range(n_sub):
            blk = j * n_sub + sb
            slot_t = slot_t_ref[0, sb * 128:(sb + 1) * 128, :]
            start_row = jnp.minimum(jnp.floor(first_t_ref[0, pl.ds(blk, 1), :] * (1.0 / 16.0)) * 16.0,
                                    float(capacity - window))
            k_pos = jnp.where(slot_t >= 0.0, slot_t - start_row + lane * float(window), -1.0)
            cols = []
            for c in range(k_total // 128):
                k_lane = lane + float(128 * c)
                hit = jnp.zeros((128, 128), F32)
                for e in range((128 * c) // window, (128 * c + 127) // window + 1):
                    hit = jnp.where(k_pos[:, e:e + 1] == k_lane, 1.0, hit)
                cols.append(hit.astype(BF16))
            onehot = jnp.concatenate(cols, axis=1)
            rows = []
            for e in range(N_EXPERTS):
                start = jnp.minimum((first_ref[b, blk, e] // 16) * 16, capacity - window)
                rows.append(ye_ref[0, pl.ds(pl.multiple_of(e * capacity + start, 16), window), :])
            acc_s[sb * 128:(sb + 1) * 128, :] = _dot(onehot, jnp.concatenate(rows, axis=0))

    pl.when(narrow_ref[b, j] != 0)(lambda: scatter(COMBINE_NARROW))
    pl.when(narrow_ref[b, j] == 0)(lambda: scatter(COMBINE_WIDE))

    g2 = mod_ref[pl.ds(b, 1), 5 * D_MODEL:6 * D_MODEL]
    x2 = x1_ref[0] + g2 * acc_s[...]
    o_ref[0] = x2 * lax.rsqrt(jnp.mean(x2 * x2, axis=-1, keepdims=True) + EPS) * fg_ref[...]


def _combine(slot_t, first_t, ye, x1, mod, fg, capacity, tm):
    b, n, _ = x1.shape
    n_blocks = n // 128
    for window in (COMBINE_WIDE, COMBINE_NARROW):
        assert (N_EXPERTS * window) % 128 == 0 and window % 16 == 0 and window <= capacity
    assert COMBINE_WIDE >= 128 + 15 and capacity % 16 == 0
    first = first_t[:, :, :N_EXPERTS].astype(jnp.int32)
    after = jnp.concatenate([first[:, 1:], jnp.full((b, 1, N_EXPERTS), capacity, jnp.int32)], axis=1)
    narrow = jnp.all(after - (first // 16) * 16 <= COMBINE_NARROW, axis=2)
    narrow = jnp.all(narrow.reshape(b, n // tm, tm // 128), axis=2).astype(jnp.int32)
    tok = pl.BlockSpec((1, tm, D_MODEL), lambda i, j, *_: (i, j, 0))
    return pl.pallas_call(
        functools.partial(_combine_kernel, capacity=capacity),
        grid_spec=pltpu.PrefetchScalarGridSpec(
            num_scalar_prefetch=2,
            grid=(b, n // tm),
            in_specs=[pl.BlockSpec((1, tm, 128), lambda i, j, *_: (i, j, 0)),
                      pl.BlockSpec((1, n_blocks, 128), lambda i, j, *_: (i, 0, 0)),
                      pl.BlockSpec((1, N_EXPERTS * capacity, D_MODEL), lambda i, j, *_: (i, 0, 0)),
                      tok,
                      pl.BlockSpec(mod.shape, lambda i, j, *_: (0, 0)),
                      pl.BlockSpec((1, D_MODEL), lambda i, j, *_: (0, 0))],
            out_specs=tok,
            scratch_shapes=[pltpu.VMEM((tm, D_MODEL), F32)]),
        out_shape=jax.ShapeDtypeStruct((b, n, D_MODEL), F32),
        compiler_params=_params("arbitrary", "arbitrary"),
        name="combine",
    )(first, narrow, slot_t, first_t, ye, x1, mod, fg)


def kernel(x, c, ctx, c_ctx, w_mod, b_mod, norm_mix_g, w_in, conv_q, conv_k, b_igate, b_fgate,
           gmlp_ws, gmlp_bs, mlstm_norm_g, w_out, norm_ffn_g, w_router, w_gate_e, w_up_e,
           w_down_e, final_g):
    depth = w_mod.shape[0]
    assert depth == 1, "the context stream is only carried as mLSTM states (single layer)"
    batch, seq, _ = x.shape
    assert seq % GRID_W == 0 and seq % CHUNK == 0 and batch + 1 <= MOD_ROWS
    capacity = EC_FACTOR * seq // N_EXPERTS
    ctx_row = batch
    l = 0

    cond = jnp.concatenate([c, c_ctx[None], jnp.zeros((MOD_ROWS - batch - 1, D_MODEL), F32)], axis=0)
    mod = _adaln(cond, w_mod[l], b_mod[l][None])

    row = lambda a: a[None]
    w_main = w_in[l][:, :MAIN_W].astype(BF16)
    w_gates = jnp.pad(w_in[l][:, MAIN_W:], ((0, 0), (0, GATE_PAD - N_GATES))).astype(BF16)
    gate_bias = jnp.pad(jnp.concatenate([b_igate[l].reshape(-1), b_fgate[l].reshape(-1)]),
                        (0, GATE_PAD - N_GATES))[None]

    p_c, gates_c = _inproj(ctx, mod, row(norm_mix_g[l]), w_main[:, K_BLK * GROUP_W:], w_gates,
                           tm=ctx.shape[1], ctx_row=ctx_row)
    k_scale = HEAD_DIM ** -0.5
    (k_c,) = _dwconv(p_c, [(0, conv_k[l], k_scale)], tm=ctx.shape[1])
    state = _mlstm(None, k_c, p_c, 1, gates_c, gate_bias, None)

    p, gates = _inproj(x, mod, row(norm_mix_g[l]), w_main, w_gates, tm=512, ctx_row=None)
    q_l, k_l = _dwconv(p, [(Q_BLK, conv_q[l], 1.0), (K_BLK, conv_k[l], k_scale)], tm=512)
    h_f, h_b = _mlstm(q_l, k_l, p, V_BLK, gates, gate_bias, state)

    x1, h2, aff_t = _postmix(p, h_f, h_b, x, mod, gmlp_ws[l].astype(BF16), gmlp_bs[l][:, :, None],
                             row(mlstm_norm_g[l]), w_out[l].astype(BF16), row(norm_ffn_g[l]),
                             jnp.pad(w_router[l], ((0, 0), (0, 128 - N_EXPERTS))), tm=512)

    slot, gate_w, slot_t, first_t, first = _route(aff_t, capacity)

    xe, gate = _gather(h2, slot, gate_w, first, capacity)
    ye = _ffn(xe, gate, w_gate_e[l], w_up_e[l], w_down_e[l], f_tile=512)
    return _combine(slot_t, first_t, ye, x1, mod, row(final_g), capacity, tm=512)
```

```python
import functools

import jax
import jax.numpy as jnp
from jax import lax
from jax.experimental import pallas as pl
from jax.experimental.pallas import tpu as pltpu

F32 = jnp.float32
BF16 = jnp.bfloat16

D_MODEL = 1024
GRID_W = 64
CHUNK = 128
HEADS = 4
GROUP_W = D_MODEL // 2
HEAD_DIM = GROUP_W // HEADS
N_EXPERTS = 16
EC_FACTOR = 2
D_EXPERT = 2 * D_MODEL
EPS = 1e-6
N_GATES = 4 * HEADS
GATE_PAD = 128
MOD_ROWS = 8

U_BLK, VG_BLK, Q_BLK, O_BLK, K_BLK, V_BLK = 0, 1, 2, 3, 4, 5
MAIN_W = 6 * GROUP_W
P_U, P_VG, P_O, P_V = 0, 1, 2, 3

VMEM_LIMIT = 56 * 1024 * 1024
ROW_GROUP = 2 * CHUNK
MLSTM_SAMPLES = 4

def _params(*sem):
    return pltpu.CompilerParams(dimension_semantics=sem, vmem_limit_bytes=VMEM_LIMIT)


def _dot(a, b):
    return jnp.dot(a, b, preferred_element_type=F32)


def _dot_nt(a, b):
    return lax.dot_general(a, b, (((1,), (1,)), ((), ())), preferred_element_type=F32)


def _split2(a):
    hi = a.astype(BF16)
    lo = (a - hi.astype(F32)).astype(BF16)
    return hi, lo


def _dot3(a, b):
    ah, al = _split2(a)
    bh, bl = _split2(b)
    return _dot(ah, bh) + (_dot(al, bh) + _dot(ah, bl))


def _dot_exact01(tri, x):
    x1 = x.astype(BF16)
    r1 = x - x1.astype(F32)
    x2 = r1.astype(BF16)
    x3 = (r1 - x2.astype(F32)).astype(BF16)
    return _dot(tri, x1) + (_dot(tri, x2) + _dot(tri, x3))


def _iota2(shape, dim):
    return lax.broadcasted_iota(jnp.int32, shape, dim)


def _row_to_col(row):
    n = row.shape[1] // 128
    eye = _iota2((128, 128), 0) == _iota2((128, 128), 1)
    cols = [jnp.sum(jnp.where(eye, row[:, j * 128:(j + 1) * 128], 0.0), axis=1, keepdims=True)
            for j in range(n)]
    return cols[0] if n == 1 else jnp.concatenate(cols, axis=0)


def _adaln_kernel(cond_ref, w_ref, b_ref, o_ref):
    o_ref[...] = _dot3(jax.nn.silu(cond_ref[...]), w_ref[...]) + b_ref[...]


def _adaln(cond, w, b):
    n_out = w.shape[1]
    tn = D_MODEL
    return pl.pallas_call(
        _adaln_kernel,
        grid=(n_out // tn,),
        in_specs=[pl.BlockSpec((MOD_ROWS, D_MODEL), lambda j: (0, 0)),
                  pl.BlockSpec((D_MODEL, tn), lambda j: (0, j)),
                  pl.BlockSpec((1, tn), lambda j: (0, j))],
        out_specs=pl.BlockSpec((MOD_ROWS, tn), lambda j: (0, j)),
        out_shape=jax.ShapeDtypeStruct((MOD_ROWS, n_out), F32),
        compiler_params=_params("arbitrary"),
        name="adaln",
    )(cond, w, b)


def _inproj_kernel(x_ref, xprev_ref, xnext_ref, mod_ref, g_ref, wc_ref, taps_ref, w_ref, wg_ref, *outs,
                   ctx_row, scales):
    conv_refs, (p_ref, gate_ref) = outs[:len(scales)], outs[len(scales):]
    row = pl.program_id(0) if ctx_row is None else ctx_row
    j = pl.program_id(1)
    sh = mod_ref[pl.ds(row, 1), 0:D_MODEL]
    sc = mod_ref[pl.ds(row, 1), D_MODEL:2 * D_MODEL]

    def modulated(x):
        y = x * lax.rsqrt(jnp.mean(x * x, axis=-1, keepdims=True) + EPS) * g_ref[...]
        return y * (1.0 + sc) + sh

    n_groups = x_ref.shape[1] // ROW_GROUP
    before = modulated(xprev_ref[0]) * (j > 0).astype(F32)
    for r in range(n_groups):
        rows = slice(r * ROW_GROUP, (r + 1) * ROW_GROUP)
        h = modulated(x_ref[0, rows, :])
        if r == n_groups - 1:
            after = modulated(xnext_ref[0]) * (j < pl.num_programs(1) - 1).astype(F32)
        else:
            after = modulated(x_ref[0, (r + 1) * ROW_GROUP:(r + 1) * ROW_GROUP + 8, :])
        ext = jnp.concatenate([before, h, after], axis=0).astype(BF16)
        before = h[ROW_GROUP - 8:, :]
        y = _dot(ext, wc_ref[...])
        hb = h.astype(BF16)
        p_ref[0, rows, :] = _dot(hb, w_ref[...])
        gate_ref[0, rows, :] = _dot(hb, wg_ref[...])
        n_ext = ext.shape[0]
        y_before = pltpu.roll(y, 1, axis=0)[8:8 + ROW_GROUP]
        y_after = pltpu.roll(y, n_ext - 1, axis=0)[8:8 + ROW_GROUP]
        y_here = y[8:8 + ROW_GROUP]
        for s, scale in enumerate(scales):
            lanes = slice(s * GROUP_W, (s + 1) * GROUP_W)
            z = (y_before[:, lanes] * taps_ref[s, 0:1, :] + y_here[:, lanes] * taps_ref[s, 1:2, :]
                 + y_after[:, lanes] * taps_ref[s, 2:3, :])
            conv_refs[s][0, rows, :] = (jax.nn.silu(z) * scale).astype(BF16)


def _inproj(x, mod, g, w_conv, taps, scales, w, wg, tm, ctx_row):
    b, n, _ = x.shape
    wn = w.shape[1]
    assert tm % ROW_GROUP == 0 and n % tm == 0
    rows8 = tm // 8
    last8 = n // 8 - 1
    full = lambda a: pl.BlockSpec(a.shape, lambda i, j: (0,) * a.ndim)
    conv_spec = pl.BlockSpec((1, tm, GROUP_W), lambda i, j: (i, j, 0))
    return pl.pallas_call(
        functools.partial(_inproj_kernel, ctx_row=ctx_row, scales=tuple(scales)),
        grid=(b, n // tm),
        in_specs=[pl.BlockSpec((1, tm, D_MODEL), lambda i, j: (i, j, 0)),
                  pl.BlockSpec((1, 8, D_MODEL), lambda i, j: (i, jnp.maximum(j * rows8 - 1, 0), 0)),
                  pl.BlockSpec((1, 8, D_MODEL), lambda i, j: (i, jnp.minimum((j + 1) * rows8, last8), 0)),
                  full(mod), full(g), full(w_conv), full(taps), full(w), full(wg)],
        out_specs=[conv_spec] * len(scales)
        + [pl.BlockSpec((1, tm, wn), lambda i, j: (i, j, 0)),
           pl.BlockSpec((1, tm, GATE_PAD), lambda i, j: (i, j, 0))],
        out_shape=[jax.ShapeDtypeStruct((b, n, GROUP_W), BF16)] * len(scales)
        + [jax.ShapeDtypeStruct((b, n, wn), F32), jax.ShapeDtypeStruct((b, n, GATE_PAD), F32)],
        compiler_params=_params("arbitrary", "arbitrary"),
        name="inproj",
    )(x, x, x, mod, g, w_conv, taps, w, wg)


def _mlstm_chunk(per_dir, bias_ref, c_s, n_s, m_s, h_refs):
    with_h = h_refs[0] is not None
    i0 = _iota2((CHUNK, CHUNK), 0)
    i1 = _iota2((CHUNK, CHUNK), 1)

    units = []
    for smp, d in [(smp, d) for smp in range(c_s.shape[0]) for d in range(2)]:
        q_ref, k_ref, v_ref, g_ref = per_dir[d]
        sees_ts = (i1 <= i0) if d == 0 else (i1 >= i0)
        sees_st = (i0 <= i1) if d == 0 else (i0 >= i1)
        tri = jnp.where(sees_ts, 1.0, 0.0).astype(BF16)
        gates = g_ref[smp] + bias_ref[...]
        bcum = _dot_exact01(tri, jax.nn.log_sigmoid(gates))
        for hd in range(HEADS):
            lanes = slice(hd * HEAD_DIM, (hd + 1) * HEAD_DIM)
            c_prev = c_s[smp, d, hd]
            n_prev = n_s[smp, d, hd]
            k = k_ref[smp, :, lanes]
            st = dict(smp=smp, d=d, hd=hd, lanes=lanes, c_prev=c_prev, n_prev=n_prev, k=k,
                      sees_st=sees_st, gates=gates, bcum=bcum, v=v_ref[smp, :, lanes])
            if with_h:
                lhs = jnp.concatenate([k, c_prev.astype(BF16),
                                       jnp.broadcast_to(n_prev, (16, HEAD_DIM)).astype(BF16)], axis=0)
                st["prod"] = _dot_nt(lhs, q_ref[smp, :, lanes])
            units.append(st)

    rows = {}
    for st in units:
        smp, d, hd = st["smp"], st["d"], st["hd"]
        if (smp, d) not in rows:
            rows[smp, d] = (st["gates"].T, st["bcum"].T)
        gates_t, bcum_t = rows[smp, d]
        ci = d * HEADS + hd
        cf = 2 * HEADS + d * HEADS + hd
        last = CHUNK - 1 if d == 0 else 0
        li_row = gates_t[ci:ci + 1, :]
        bc_row = bcum_t[cf:cf + 1, :]
        b_last = bc_row[:, last:last + 1]
        m_prev = m_s[smp, d, hd]
        v_t = st["v"].T
        a_row = b_last - bc_row + li_row
        m_new = jnp.maximum(b_last + m_prev, jnp.max(a_row, axis=-1, keepdims=True))
        w_row = jnp.exp(a_row - m_new)
        lhs = jnp.concatenate([v_t * w_row, jnp.broadcast_to(w_row, (16, CHUNK))], axis=0)
        st["upd"] = _dot(lhs.astype(BF16), st["k"])
        st.update(bc_row=bc_row, m_prev=m_prev, m_new=m_new, v_t=v_t,
                  decay=jnp.exp(b_last + m_prev - m_new),
                  u_col=st["gates"][:, ci:ci + 1] - st["bcum"][:, cf:cf + 1])

    if with_h:
        for st in units:
            prod, bc_row = st["prod"], st["bc_row"]
            g = bc_row + st["m_prev"]
            dmat = jnp.where(st["sees_st"], st["u_col"] + bc_row, -jnp.inf)
            m_t = jnp.maximum(g, jnp.max(dmat, axis=0, keepdims=True))
            inter = jnp.exp(g - m_t)
            s = prod[0:CHUNK] * jnp.exp(dmat - m_t)
            st["pv"] = _dot(st["v_t"].astype(BF16), s.astype(BF16))
            st["num0"] = inter * prod[CHUNK:2 * CHUNK]
            den = inter * prod[2 * CHUNK:2 * CHUNK + 1] + jnp.sum(s, axis=0, keepdims=True)
            st["scale"] = 1.0 / jnp.maximum(jnp.abs(den), jnp.exp(-m_t))

    for st in units:
        smp, d, hd = st["smp"], st["d"], st["hd"]
        if with_h:
            h_refs[d][smp, :, st["lanes"]] = ((st["num0"] + st["pv"]) * st["scale"]).T
        c_s[smp, d, hd] = st["decay"] * st["c_prev"] + st["upd"][0:HEAD_DIM]
        n_s[smp, d, hd] = st["decay"] * st["n_prev"] + st["upd"][HEAD_DIM:HEAD_DIM + 1]
        m_s[smp, d, hd] = st["m_new"]


def _mlstm_kernel(*refs, n_chunks, with_h):
    refs = list(refs)
    take = lambda n: [refs.pop(0) for _ in range(n)]
    per_dir = []
    for _ in range(2):
        q_ref = take(1)[0] if with_h else None
        k_ref, v_ref, g_ref = take(3)
        per_dir.append((q_ref, k_ref, v_ref, g_ref))
    bias_ref = take(1)[0]
    if with_h:
        c0_ref, n0_ref, m0_ref = take(3)
        h_refs = take(2)
        c_out = n_out = m_out = None
    else:
        h_refs = [None, None]
        c_out, n_out, m_out = take(3)
    c_s, n_s, m_s = take(3)

    j = pl.program_id(1)

    @pl.when(j == 0)
    def _():
        if with_h:
            c_s[...] = c0_ref[...]
            n_s[...] = n0_ref[...]
            m_s[...] = m0_ref[...]
        else:
            c_s[...] = jnp.zeros_like(c_s)
            n_s[...] = jnp.zeros_like(n_s)
            m_s[...] = jnp.zeros_like(m_s)

    _mlstm_chunk(per_dir, bias_ref, c_s, n_s, m_s, h_refs)

    if not with_h:
        @pl.when(j == n_chunks - 1)
        def _():
            c_out[...] = c_s[...]
            n_out[...] = n_s[...]
            m_out[...] = m_s[...]


def _mlstm(q, k, p, v_blk, gates, bias, state):
    b, n, _ = k.shape
    n_chunks = n // CHUNK
    with_h = q is not None
    ns = MLSTM_SAMPLES if b % MLSTM_SAMPLES == 0 else 1

    in_specs, args = [], []
    for d in range(2):
        c = (lambda j: j) if d == 0 else (lambda j: n_chunks - 1 - j)
        tok = pl.BlockSpec((ns, CHUNK, GROUP_W), lambda i, j, c=c: (i, c(j), 0))
        if with_h:
            in_specs.append(tok)
            args.append(q)
        in_specs += [tok,
                     pl.BlockSpec((ns, CHUNK, GROUP_W), lambda i, j, c=c: (i, c(j), v_blk)),
                     pl.BlockSpec((ns, CHUNK, GATE_PAD), lambda i, j, c=c: (i, c(j), 0))]
        args += [k, p, gates]
    in_specs.append(pl.BlockSpec((1, GATE_PAD), lambda i, j: (0, 0)))
    args.append(bias)

    c_shape = (2, HEADS, HEAD_DIM, HEAD_DIM)
    v_shape = (2, HEADS, 1, HEAD_DIM)
    c_spec = pl.BlockSpec((ns,) + c_shape, lambda i, j: (i, 0, 0, 0, 0))
    v_spec = pl.BlockSpec((ns,) + v_shape, lambda i, j: (i, 0, 0, 0, 0))
    if with_h:
        in_specs += [c_spec, v_spec, v_spec]
        args += list(state)
        out_specs = [pl.BlockSpec((ns, CHUNK, GROUP_W), lambda i, j: (i, j, 0)),
                     pl.BlockSpec((ns, CHUNK, GROUP_W), lambda i, j: (i, n_chunks - 1 - j, 0))]
        out_shape = [jax.ShapeDtypeStruct((b, n, GROUP_W), F32)] * 2
    else:
        out_specs = [c_spec, v_spec, v_spec]
        out_shape = [jax.ShapeDtypeStruct((b,) + c_shape, F32),
                     jax.ShapeDtypeStruct((b,) + v_shape, F32),
                     jax.ShapeDtypeStruct((b,) + v_shape, F32)]
    return pl.pallas_call(
        functools.partial(_mlstm_kernel, n_chunks=n_chunks, with_h=with_h),
        grid=(b // ns, n_chunks),
        in_specs=in_specs,
        out_specs=out_specs,
        out_shape=out_shape,
        scratch_shapes=[pltpu.VMEM((ns,) + c_shape, F32), pltpu.VMEM((ns,) + v_shape, F32),
                        pltpu.VMEM((ns,) + v_shape, F32)],
        compiler_params=_params("arbitrary", "arbitrary"),
        name="mlstm" if with_h else "mlstm_ctx_state",
    )(*args)


def _layer_norm(x):
    mu = jnp.mean(x, axis=-1, keepdims=True)
    var = jnp.mean(jnp.square(x - mu), axis=-1, keepdims=True)
    return (x - mu) * lax.rsqrt(var + EPS)


def _postmix_kernel(u_ref, vg_ref, o_ref, hf_ref, hb_ref, x_ref, mod_ref, ws_ref, bs_ref, ng_ref,
                    wout_ref, fg_ref, wr_ref, x1_ref, h2_ref, aff_ref, ycat_s):
    tm = x_ref.shape[1]
    b = pl.program_id(0)
    mod = lambda k: mod_ref[pl.ds(b, 1), k * D_MODEL:(k + 1) * D_MODEL]
    wr_hi, wr_lo = _split2(wr_ref[...])
    expert_lane = _iota2((ROW_GROUP, 128), 1) < N_EXPERTS

    for r in range(tm // ROW_GROUP):
        rows = slice(r * ROW_GROUP, (r + 1) * ROW_GROUP)

        for c in range(r * ROW_GROUP // CHUNK, (r + 1) * ROW_GROUP // CHUNK):
            crows = slice(c * CHUNK, (c + 1) * CHUNK)
            u = jax.nn.gelu(u_ref[0, crows, :])
            v = _layer_norm(jax.nn.gelu(vg_ref[0, crows, :])).astype(BF16)
            for hd in range(HEADS):
                lanes = slice(hd * HEAD_DIM, (hd + 1) * HEAD_DIM)
                s = _dot(ws_ref[hd], v[:, lanes]) + bs_ref[hd]
                ycat_s[crows, lanes] = (u[:, lanes] * s).astype(BF16)

        hsum = hf_ref[0, rows, :] + hb_ref[0, rows, :]
        o = jax.nn.sigmoid(o_ref[0, rows, :])
        for hd in range(HEADS):
            lanes = slice(hd * HEAD_DIM, (hd + 1) * HEAD_DIM)
            hn = _layer_norm(hsum[:, lanes]) * ng_ref[:, lanes]
            ycat_s[rows, GROUP_W + hd * HEAD_DIM:GROUP_W + (hd + 1) * HEAD_DIM] = (
                o[:, lanes] * hn).astype(BF16)

        y = _dot(ycat_s[rows, :], wout_ref[...])
        x1 = x_ref[0, rows, :] + mod(2) * y
        x1_ref[0, rows, :] = x1

        n2 = x1 * lax.rsqrt(jnp.mean(x1 * x1, axis=-1, keepdims=True) + EPS) * fg_ref[...]
        h2 = n2 * (1.0 + mod(4)) + mod(3)
        h2_hi, h2_lo = _split2(h2)
        h2_ref[0, rows, :] = h2_hi

        logits = _dot(h2_hi, wr_hi) + (_dot(h2_lo, wr_hi) + _dot(h2_hi, wr_lo))
        logits = jnp.where(expert_lane, logits, -jnp.inf)
        e = jnp.exp(logits - jnp.max(logits, axis=-1, keepdims=True))
        aff = e / jnp.sum(e, axis=-1, keepdims=True)
        aff_ref[0, :, rows] = aff.T[0:N_EXPERTS, :]


def _postmix(p, hf, hb, x, mod, ws, bs, ng, wout, fg, wr, tm):
    b, n, _ = x.shape
    tok = lambda blk: pl.BlockSpec((1, tm, GROUP_W), lambda i, j: (i, j, blk))
    full = lambda a: pl.BlockSpec(a.shape, lambda i, j: (0,) * a.ndim)
    return pl.pallas_call(
        _postmix_kernel,
        grid=(b, n // tm),
        in_specs=[tok(P_U), tok(P_VG), tok(P_O), tok(0), tok(0),
                  pl.BlockSpec((1, tm, D_MODEL), lambda i, j: (i, j, 0)),
                  full(mod), full(ws), full(bs), full(ng), full(wout), full(fg), full(wr)],
        out_specs=[pl.BlockSpec((1, tm, D_MODEL), lambda i, j: (i, j, 0)),
                   pl.BlockSpec((1, tm, D_MODEL), lambda i, j: (i, j, 0)),
                   pl.BlockSpec((1, N_EXPERTS, tm), lambda i, j: (i, 0, j))],
        out_shape=[jax.ShapeDtypeStruct((b, n, D_MODEL), F32),
                   jax.ShapeDtypeStruct((b, n, D_MODEL), BF16),
                   jax.ShapeDtypeStruct((b, N_EXPERTS, n), F32)],
        scratch_shapes=[pltpu.VMEM((tm, D_MODEL), BF16)],
        compiler_params=_params("arbitrary", "arbitrary"),
        name="postmix",
    )(p, p, p, hf, hb, x, mod, ws, bs, ng, wout, fg, wr)


def _cumsum_lanes(x, upper):
    carry = jnp.zeros((x.shape[0], 1), F32)
    outs, before = [], []
    for j in range(x.shape[1] // 128):
        before.append(carry)
        c = _dot(x[:, j * 128:(j + 1) * 128].astype(BF16), upper) + carry
        outs.append(c)
        carry = c[:, 127:128]
    return jnp.concatenate(outs, axis=1), before


def _rows_to_lanes(x, fill):
    pad = jnp.full((128 - x.shape[0], 128), fill, F32)
    return jnp.concatenate([x, pad], axis=0).T


def _route_kernel(aff_ref, slot_ref, w_ref, slot_t_ref, first_t_ref, first_ref, *, capacity):
    aff = aff_ref[0]
    cap = float(capacity)
    thr_bits = jnp.zeros((aff.shape[0], 1), jnp.int32)
    for bit in range(30, -1, -1):
        cand = thr_bits | (1 << bit)
        cnt = jnp.sum(jnp.where(aff >= pltpu.bitcast(cand, F32), 1.0, 0.0), axis=-1, keepdims=True)
        thr_bits = jnp.where(cnt >= cap, cand, thr_bits)
    thr = pltpu.bitcast(thr_bits, F32)
    upper = jnp.where(_iota2((128, 128), 0) <= _iota2((128, 128), 1), 1.0, 0.0).astype(BF16)
    above = jnp.where(aff > thr, 1.0, 0.0)
    tied = jnp.where(aff == thr, 1.0, 0.0)
    need = cap - jnp.sum(above, axis=-1, keepdims=True)
    sel = above + tied * jnp.where(_cumsum_lanes(tied, upper)[0] <= need, 1.0, 0.0)
    count, before = _cumsum_lanes(sel, upper)
    slot = jnp.where(sel > 0.0, count - 1.0, -1.0)
    slot_ref[0] = slot
    w_ref[0] = jnp.where(sel > 0.0, aff, 0.0)

    n_blocks = len(before)
    for j in range(n_blocks):
        slot_t_ref[0, j * 128:(j + 1) * 128, :] = _rows_to_lanes(slot[:, j * 128:(j + 1) * 128], -1.0)
    lane = _iota2((1, 128), 1)
    first = jnp.zeros((aff.shape[0], 128), F32)
    for j in range(n_blocks):
        first = jnp.where(lane == j, before[j], first)
    first_ref[0] = first
    first_t_ref[0] = _rows_to_lanes(first, 0.0)[0:n_blocks, :]


def _route(aff_t, capacity):
    b, e, n = aff_t.shape
    assert n // 128 <= 128 and e <= 128
    spec = pl.BlockSpec((1, e, n), lambda i: (i, 0, 0))
    return pl.pallas_call(
        functools.partial(_route_kernel, capacity=capacity),
        grid=(b,),
        in_specs=[spec],
        out_specs=[spec, spec,
                   pl.BlockSpec((1, n, 128), lambda i: (i, 0, 0)),
                   pl.BlockSpec((1, n // 128, 128), lambda i: (i, 0, 0)),
                   pl.BlockSpec((1, e, 128), lambda i: (i, 0, 0))],
        out_shape=[jax.ShapeDtypeStruct((b, e, n), F32)] * 2
        + [jax.ShapeDtypeStruct((b, n, 128), F32), jax.ShapeDtypeStruct((b, n // 128, 128), F32),
           jax.ShapeDtypeStruct((b, e, 128), F32)],
        compiler_params=_params("arbitrary"),
        name="route",
    )(aff_t)


GATHER_TILE = 256
GATHER_ROWS = 128
GATHER_GROUP = 4


def _gather_kernel(before_ref, fits_ref, h2_ref, slot_ref, w_ref, xe_ref, gate_ref, xe_s, gate_s):
    b = pl.program_id(0)
    g = pl.program_id(1)
    n_g, cap = xe_ref.shape[1], xe_ref.shape[2]
    n = h2_ref.shape[1]
    n_tiles = n // GATHER_TILE

    @pl.when(fits_ref[b, g] != 0)
    def _():
        xe_s[:, 0:16, :] = jnp.zeros((n_g, 16, D_MODEL), BF16)
        gate_s[...] = jnp.zeros_like(gate_s)
        row_id = _iota2((GATHER_ROWS, 1), 0).astype(F32)
        for j in range(n_tiles):
            starts, onehots = [], []
            for e in range(n_g):
                start = pl.multiple_of((before_ref[b, g * n_g + e, j] // 16) * 16, 16)
                hit = (slot_ref[0, e, j:j + 1, :] - start.astype(F32)) == row_id
                onehots.append(jnp.where(hit, 1.0, 0.0).astype(BF16))
                gate_s[e, pl.ds(start, GATHER_ROWS), :] += jnp.sum(
                    jnp.where(hit, w_ref[0, e, j:j + 1, :], 0.0), axis=1, keepdims=True)
                starts.append(start)
            rows = _dot(jnp.concatenate(onehots, axis=0),
                        h2_ref[0, j * GATHER_TILE:(j + 1) * GATHER_TILE, :]).astype(BF16)
            for e, start in enumerate(starts):
                r0 = e * GATHER_ROWS
                xe_s[e, pl.ds(start, 16), :] += rows[r0:r0 + 16]
                xe_s[e, pl.ds(start + 16, GATHER_ROWS - 16), :] = rows[r0 + 16:r0 + GATHER_ROWS]
        xe_ref[0] = xe_s[:, 0:cap, :]
        gate_ref[0] = gate_s[:, 0:cap, :]

    @pl.when(fits_ref[b, g] == 0)
    def _():
        slot_id = _iota2((cap, 1), 0).astype(F32)
        for e in range(n_g):
            xe = jnp.zeros((cap, D_MODEL), F32)
            gate = jnp.zeros((cap, 1), F32)
            for j in range(n_tiles):
                hit = slot_ref[0, e, j:j + 1, :] == slot_id
                xe = xe + _dot(jnp.where(hit, 1.0, 0.0).astype(BF16),
                               h2_ref[0, j * GATHER_TILE:(j + 1) * GATHER_TILE, :])
                gate = gate + jnp.sum(jnp.where(hit, w_ref[0, e, j:j + 1, :], 0.0), axis=1,
                                      keepdims=True)
            xe_ref[0, e] = xe.astype(BF16)
            gate_ref[0, e] = gate


def _gather(h2, slot, w, first, capacity):
    b, n, _ = h2.shape
    n_e = slot.shape[1]
    n_tiles = n // GATHER_TILE
    assert n_e % GATHER_GROUP == 0 and GATHER_TILE % 128 == 0 and capacity % 16 == 0
    before = first[:, :, 0:n // 128:GATHER_TILE // 128].astype(jnp.int32)
    after = jnp.concatenate([before[:, :, 1:], jnp.full((b, n_e, 1), capacity, jnp.int32)], axis=2)
    fits = jnp.all((after - (before // 16) * 16) < GATHER_ROWS, axis=2)
    fits = jnp.all(fits.reshape(b, n_e // GATHER_GROUP, GATHER_GROUP), axis=2).astype(jnp.int32)
    tiles = lambda a: a.reshape(b, n_e, n_tiles, GATHER_TILE)
    rows = pl.BlockSpec((1, GATHER_GROUP, n_tiles, GATHER_TILE), lambda i, g, *_: (i, g, 0, 0))
    return pl.pallas_call(
        _gather_kernel,
        grid_spec=pltpu.PrefetchScalarGridSpec(
            num_scalar_prefetch=2,
            grid=(b, n_e // GATHER_GROUP),
            in_specs=[pl.BlockSpec((1, n, D_MODEL), lambda i, g, *_: (i, 0, 0)), rows, rows],
            out_specs=[pl.BlockSpec((1, GATHER_GROUP, capacity, D_MODEL), lambda i, g, *_: (i, g, 0, 0)),
                       pl.BlockSpec((1, GATHER_GROUP, capacity, 1), lambda i, g, *_: (i, g, 0, 0))],
            scratch_shapes=[pltpu.VMEM((GATHER_GROUP, capacity + GATHER_ROWS, D_MODEL), BF16),
                            pltpu.VMEM((GATHER_GROUP, capacity + GATHER_ROWS, 1), F32)]),
        out_shape=[jax.ShapeDtypeStruct((b, n_e, capacity, D_MODEL), BF16),
                   jax.ShapeDtypeStruct((b, n_e, capacity, 1), F32)],
        compiler_params=_params("arbitrary", "arbitrary"),
        name="gather",
    )(before, fits, h2, tiles(slot), tiles(w))


def _ffn_kernel(xe_ref, gate_ref, wg_ref, wu_ref, wd_ref, ye_ref, acc_s):
    ft = pl.program_id(1)
    last = pl.num_programs(1) - 1

    def step(first, final):
        wg = wg_ref[0].astype(BF16)
        wu = wu_ref[0].astype(BF16)
        wd = wd_ref[0].astype(BF16)
        for i in range(xe_ref.shape[0]):
            xe = xe_ref[i, 0]
            act = jax.nn.silu(_dot(xe, wg)) * _dot(xe, wu)
            part = _dot(act.astype(BF16), wd)
            total = part if first else acc_s[i] + part
            if final:
                ye_ref[i] = (total * gate_ref[i, 0]).astype(BF16)
            else:
                acc_s[i] = total

    pl.when(ft == 0)(lambda: step(True, False))
    pl.when(jnp.logical_and(ft > 0, ft < last))(lambda: step(False, False))
    pl.when(ft == last)(lambda: step(False, True))


def _ffn(xe, gate, wg, wu, wd, f_tile):
    b, n_e, cap, _ = xe.shape
    assert D_EXPERT // f_tile >= 2
    return pl.pallas_call(
        _ffn_kernel,
        grid=(n_e, D_EXPERT // f_tile),
        in_specs=[pl.BlockSpec((b, 1, cap, D_MODEL), lambda e, f: (0, e, 0, 0)),
                  pl.BlockSpec((b, 1, cap, 1), lambda e, f: (0, e, 0, 0)),
                  pl.BlockSpec((1, D_MODEL, f_tile), lambda e, f: (e, 0, f)),
                  pl.BlockSpec((1, D_MODEL, f_tile), lambda e, f: (e, 0, f)),
                  pl.BlockSpec((1, f_tile, D_MODEL), lambda e, f: (e, f, 0))],
        out_specs=pl.BlockSpec((b, cap, D_MODEL), lambda e, f: (0, e, 0)),
        out_shape=jax.ShapeDtypeStruct((b, n_e * cap, D_MODEL), BF16),
        scratch_shapes=[pltpu.VMEM((b, cap, D_MODEL), F32)],
        compiler_params=_params("arbitrary", "arbitrary"),
        name="ffn",
    )(xe, gate, wg, wu, wd)


COMBINE_WIDE = CHUNK + 16
COMBINE_NARROW = 48


def _combine_kernel(first_ref, narrow_ref, slot_t_ref, first_t_ref, ye_ref, x1_ref, mod_ref, fg_ref,
                    o_ref, acc_s, *, capacity):
    b = pl.program_id(0)
    j = pl.program_id(1)
    n_sub = o_ref.shape[1] // 128
    lane = _iota2((1, 128), 1).astype(F32)

    def scatter(window):
        k_total = N_EXPERTS * window
        for sb in range(n_sub):
            blk = j * n_sub + sb
            slot_t = slot_t_ref[0, sb * 128:(sb + 1) * 128, :]
            start_row = jnp.minimum(jnp.floor(first_t_ref[0, pl.ds(blk, 1), :] * (1.0 / 16.0)) * 16.0,
                                    float(capacity - window))
            k_pos = jnp.where(slot_t >= 0.0, slot_t - start_row + lane * float(window), -1.0)
            cols = []
            for c in range(k_total // 128):
                k_lane = lane + float(128 * c)
                hit = jnp.zeros((128, 128), F32)
                for e in range((128 * c) // window, (128 * c + 127) // window + 1):
                    hit = jnp.where(k_pos[:, e:e + 1] == k_lane, 1.0, hit)
                cols.append(hit.astype(BF16))
            onehot = jnp.concatenate(cols, axis=1)
            rows = []
            for e in range(N_EXPERTS):
                start = jnp.minimum((first_ref[b, blk, e] // 16) * 16, capacity - window)
                rows.append(ye_ref[0, pl.ds(pl.multiple_of(e * capacity + start, 16), window), :])
            acc_s[sb * 128:(sb + 1) * 128, :] = _dot(onehot, jnp.concatenate(rows, axis=0))

    pl.when(narrow_ref[b, j] != 0)(lambda: scatter(COMBINE_NARROW))
    pl.when(narrow_ref[b, j] == 0)(lambda: scatter(COMBINE_WIDE))

    g2 = mod_ref[pl.ds(b, 1), 5 * D_MODEL:6 * D_MODEL]
    x2 = x1_ref[0] + g2 * acc_s[...]
    o_ref[0] = x2 * lax.rsqrt(jnp.mean(x2 * x2, axis=-1, keepdims=True) + EPS) * fg_ref[...]


def _combine(slot_t, first_t, ye, x1, mod, fg, capacity, tm):
    b, n, _ = x1.shape
    n_blocks = n // 128
    for window in (COMBINE_WIDE, COMBINE_NARROW):
        assert (N_EXPERTS * window) % 128 == 0 and window % 16 == 0 and window <= capacity
    assert COMBINE_WIDE >= 128 + 15 and capacity % 16 == 0
    first = first_t[:, :, :N_EXPERTS].astype(jnp.int32)
    after = jnp.concatenate([first[:, 1:], jnp.full((b, 1, N_EXPERTS), capacity, jnp.int32)], axis=1)
    narrow = jnp.all(after - (first // 16) * 16 <= COMBINE_NARROW, axis=2)
    narrow = jnp.all(narrow.reshape(b, n // tm, tm // 128), axis=2).astype(jnp.int32)
    tok = pl.BlockSpec((1, tm, D_MODEL), lambda i, j, *_: (i, j, 0))
    return pl.pallas_call(
        functools.partial(_combine_kernel, capacity=capacity),
        grid_spec=pltpu.PrefetchScalarGridSpec(
            num_scalar_prefetch=2,
            grid=(b, n // tm),
            in_specs=[pl.BlockSpec((1, tm, 128), lambda i, j, *_: (i, j, 0)),
                      pl.BlockSpec((1, n_blocks, 128), lambda i, j, *_: (i, 0, 0)),
                      pl.BlockSpec((1, N_EXPERTS * capacity, D_MODEL), lambda i, j, *_: (i, 0, 0)),
                      tok,
                      pl.BlockSpec(mod.shape, lambda i, j, *_: (0, 0)),
                      pl.BlockSpec((1, D_MODEL), lambda i, j, *_: (0, 0))],
            out_specs=tok,
            scratch_shapes=[pltpu.VMEM((tm, D_MODEL), F32)]),
        out_shape=jax.ShapeDtypeStruct((b, n, D_MODEL), F32),
        compiler_params=_params("arbitrary", "arbitrary"),
        name="combine",
    )(first, narrow, slot_t, first_t, ye, x1, mod, fg)


def kernel(x, c, ctx, c_ctx, w_mod, b_mod, norm_mix_g, w_in, conv_q, conv_k, b_igate, b_fgate,
           gmlp_ws, gmlp_bs, mlstm_norm_g, w_out, norm_ffn_g, w_router, w_gate_e, w_up_e,
           w_down_e, final_g):
    depth = w_mod.shape[0]
    assert depth == 1, "the context stream is only carried as mLSTM states (single layer)"
    batch, seq, _ = x.shape
    assert seq % GRID_W == 0 and seq % CHUNK == 0 and batch + 1 <= MOD_ROWS
    capacity = EC_FACTOR * seq // N_EXPERTS
    ctx_row = batch
    l = 0

    cond = jnp.concatenate([c, c_ctx[None], jnp.zeros((MOD_ROWS - batch - 1, D_MODEL), F32)], axis=0)
    mod = _adaln(cond, w_mod[l], b_mod[l][None])

    row = lambda a: a[None]
    w_bf = w_in[l].astype(BF16)
    cols = lambda *blks: jnp.concatenate([w_bf[:, k * GROUP_W:(k + 1) * GROUP_W] for k in blks], axis=1)
    w_gates = jnp.pad(w_bf[:, MAIN_W:], ((0, 0), (0, GATE_PAD - N_GATES)))
    gate_bias = jnp.pad(jnp.concatenate([b_igate[l].reshape(-1), b_fgate[l].reshape(-1)]),
                        (0, GATE_PAD - N_GATES))[None]
    k_scale = HEAD_DIM ** -0.5

    k_c, v_c, gates_c = _inproj(ctx, mod, row(norm_mix_g[l]), cols(K_BLK), conv_k[l][None], (k_scale,),
                                cols(V_BLK), w_gates, tm=ctx.shape[1], ctx_row=ctx_row)
    state = _mlstm(None, k_c, v_c, 0, gates_c, gate_bias, None)

    q_l, k_l, p, gates = _inproj(x, mod, row(norm_mix_g[l]), cols(Q_BLK, K_BLK),
                                 jnp.stack([conv_q[l], conv_k[l]]), (1.0, k_scale),
                                 cols(U_BLK, VG_BLK, O_BLK, V_BLK), w_gates, tm=512, ctx_row=None)
    h_f, h_b = _mlstm(q_l, k_l, p, P_V, gates, gate_bias, state)

    x1, h2, aff_t = _postmix(p, h_f, h_b, x, mod, gmlp_ws[l].astype(BF16), gmlp_bs[l][:, :, None],
                             row(mlstm_norm_g[l]), w_out[l].astype(BF16), row(norm_ffn_g[l]),
                             jnp.pad(w_router[l], ((0, 0), (0, 128 - N_EXPERTS))), tm=512)

    slot, gate_w, slot_t, first_t, first = _route(aff_t, capacity)

    xe, gate = _gather(h2, slot, gate_w, first, capacity)
    ye = _ffn(xe, gate, w_gate_e[l], w_up_e[l], w_down_e[l], f_tile=512)
    return _combine(slot_t, first_t, ye, x1, mod, row(final_g), capacity, tm=512)
```

```python
import functools

import jax
import jax.numpy as jnp
from jax import lax
from jax.experimental import pallas as pl
from jax.experimental.pallas import tpu as pltpu

F32 = jnp.float32
BF16 = jnp.bfloat16

D_MODEL = 1024
GRID_W = 64
CHUNK = 128
HEADS = 4
GROUP_W = D_MODEL // 2
HEAD_DIM = GROUP_W // HEADS
N_EXPERTS = 16
EC_FACTOR = 2
D_EXPERT = 2 * D_MODEL
EPS = 1e-6
N_GATES = 4 * HEADS
GATE_PAD = 128
MOD_ROWS = 8

U_BLK, VG_BLK, Q_BLK, O_BLK, K_BLK, V_BLK = 0, 1, 2, 3, 4, 5
MAIN_W = 6 * GROUP_W
P_U, P_VG, P_O, P_V = 0, 1, 2, 3

VMEM_LIMIT = 56 * 1024 * 1024
ROW_GROUP = 2 * CHUNK
MLSTM_SAMPLES = 4

def _params(*sem):
    return pltpu.CompilerParams(dimension_semantics=sem, vmem_limit_bytes=VMEM_LIMIT)


def _dot(a, b):
    return jnp.dot(a, b, preferred_element_type=F32)


def _dot_nt(a, b):
    return lax.dot_general(a, b, (((1,), (1,)), ((), ())), preferred_element_type=F32)


def _split2(a):
    hi = a.astype(BF16)
    lo = (a - hi.astype(F32)).astype(BF16)
    return hi, lo


def _dot3(a, b):
    ah, al = _split2(a)
    bh, bl = _split2(b)
    return _dot(ah, bh) + (_dot(al, bh) + _dot(ah, bl))


def _dot_exact01(tri, x):
    x1 = x.astype(BF16)
    r1 = x - x1.astype(F32)
    x2 = r1.astype(BF16)
    x3 = (r1 - x2.astype(F32)).astype(BF16)
    return _dot(tri, x1) + (_dot(tri, x2) + _dot(tri, x3))


def _iota2(shape, dim):
    return lax.broadcasted_iota(jnp.int32, shape, dim)


def _row_to_col(row):
    n = row.shape[1] // 128
    eye = _iota2((128, 128), 0) == _iota2((128, 128), 1)
    cols = [jnp.sum(jnp.where(eye, row[:, j * 128:(j + 1) * 128], 0.0), axis=1, keepdims=True)
            for j in range(n)]
    return cols[0] if n == 1 else jnp.concatenate(cols, axis=0)


def _adaln_kernel(cond_ref, w_ref, b_ref, o_ref):
    o_ref[...] = _dot3(jax.nn.silu(cond_ref[...]), w_ref[...]) + b_ref[...]


def _adaln(cond, w, b):
    n_out = w.shape[1]
    tn = D_MODEL
    return pl.pallas_call(
        _adaln_kernel,
        grid=(n_out // tn,),
        in_specs=[pl.BlockSpec((MOD_ROWS, D_MODEL), lambda j: (0, 0)),
                  pl.BlockSpec((D_MODEL, tn), lambda j: (0, j)),
                  pl.BlockSpec((1, tn), lambda j: (0, j))],
        out_specs=pl.BlockSpec((MOD_ROWS, tn), lambda j: (0, j)),
        out_shape=jax.ShapeDtypeStruct((MOD_ROWS, n_out), F32),
        compiler_params=_params("arbitrary"),
        name="adaln",
    )(cond, w, b)


def _inproj_kernel(x_ref, xprev_ref, xnext_ref, mod_ref, g_ref, wc_ref, taps_ref, w_ref, wg_ref, *outs,
                   ctx_row, scales):
    conv_refs, (p_ref, gate_ref) = outs[:len(scales)], outs[len(scales):]
    row = pl.program_id(0) if ctx_row is None else ctx_row
    j = pl.program_id(1)
    sh = mod_ref[pl.ds(row, 1), 0:D_MODEL]
    sc = mod_ref[pl.ds(row, 1), D_MODEL:2 * D_MODEL]

    def modulated(x):
        y = x * lax.rsqrt(jnp.mean(x * x, axis=-1, keepdims=True) + EPS) * g_ref[...]
        return y * (1.0 + sc) + sh

    n_groups = x_ref.shape[1] // ROW_GROUP
    before = modulated(xprev_ref[0]) * (j > 0).astype(F32)
    for r in range(n_groups):
        rows = slice(r * ROW_GROUP, (r + 1) * ROW_GROUP)
        h = modulated(x_ref[0, rows, :])
        if r == n_groups - 1:
            after = modulated(xnext_ref[0]) * (j < pl.num_programs(1) - 1).astype(F32)
        else:
            after = modulated(x_ref[0, (r + 1) * ROW_GROUP:(r + 1) * ROW_GROUP + 8, :])
        ext = jnp.concatenate([before, h, after], axis=0).astype(BF16)
        before = h[ROW_GROUP - 8:, :]
        y = _dot(ext, wc_ref[...])
        hb = h.astype(BF16)
        p_ref[0, rows, :] = _dot(hb, w_ref[...])
        gate_ref[0, rows, :] = _dot(hb, wg_ref[...])
        n_ext = ext.shape[0]
        y_before = pltpu.roll(y, 1, axis=0)[8:8 + ROW_GROUP]
        y_after = pltpu.roll(y, n_ext - 1, axis=0)[8:8 + ROW_GROUP]
        y_here = y[8:8 + ROW_GROUP]
        for s, scale in enumerate(scales):
            lanes = slice(s * GROUP_W, (s + 1) * GROUP_W)
            z = (y_before[:, lanes] * taps_ref[s, 0:1, :] + y_here[:, lanes] * taps_ref[s, 1:2, :]
                 + y_after[:, lanes] * taps_ref[s, 2:3, :])
            conv_refs[s][0, rows, :] = (jax.nn.silu(z) * scale).astype(BF16)


def _inproj(x, mod, g, w_conv, taps, scales, w, wg, tm, ctx_row):
    b, n, _ = x.shape
    wn = w.shape[1]
    assert tm % ROW_GROUP == 0 and n % tm == 0
    rows8 = tm // 8
    last8 = n // 8 - 1
    full = lambda a: pl.BlockSpec(a.shape, lambda i, j: (0,) * a.ndim)
    conv_spec = pl.BlockSpec((1, tm, GROUP_W), lambda i, j: (i, j, 0))
    return pl.pallas_call(
        functools.partial(_inproj_kernel, ctx_row=ctx_row, scales=tuple(scales)),
        grid=(b, n // tm),
        in_specs=[pl.BlockSpec((1, tm, D_MODEL), lambda i, j: (i, j, 0)),
                  pl.BlockSpec((1, 8, D_MODEL), lambda i, j: (i, jnp.maximum(j * rows8 - 1, 0), 0)),
                  pl.BlockSpec((1, 8, D_MODEL), lambda i, j: (i, jnp.minimum((j + 1) * rows8, last8), 0)),
                  full(mod), full(g), full(w_conv), full(taps), full(w), full(wg)],
        out_specs=[conv_spec] * len(scales)
        + [pl.BlockSpec((1, tm, wn), lambda i, j: (i, j, 0)),
           pl.BlockSpec((1, tm, GATE_PAD), lambda i, j: (i, j, 0))],
        out_shape=[jax.ShapeDtypeStruct((b, n, GROUP_W), BF16)] * len(scales)
        + [jax.ShapeDtypeStruct((b, n, wn), F32), jax.ShapeDtypeStruct((b, n, GATE_PAD), F32)],
        compiler_params=_params("arbitrary", "arbitrary"),
        name="inproj",
    )(x, x, x, mod, g, w_conv, taps, w, wg)


def _mlstm_chunk(per_dir, bias_ref, c_s, n_s, m_s, h_refs):
    with_h = h_refs[0] is not None
    i0 = _iota2((CHUNK, CHUNK), 0)
    i1 = _iota2((CHUNK, CHUNK), 1)

    units = []
    for smp, d in [(smp, d) for smp in range(c_s.shape[0]) for d in range(2)]:
        q_ref, k_ref, v_ref, g_ref = per_dir[d]
        sees_ts = (i1 <= i0) if d == 0 else (i1 >= i0)
        sees_st = (i0 <= i1) if d == 0 else (i0 >= i1)
        tri = jnp.where(sees_ts, 1.0, 0.0).astype(BF16)
        gates = g_ref[smp] + bias_ref[...]
        bcum = _dot_exact01(tri, jax.nn.log_sigmoid(gates))
        for hd in range(HEADS):
            lanes = slice(hd * HEAD_DIM, (hd + 1) * HEAD_DIM)
            c_prev = c_s[smp, d, hd]
            n_prev = n_s[smp, d, hd]
            k = k_ref[smp, :, lanes]
            st = dict(smp=smp, d=d, hd=hd, lanes=lanes, c_prev=c_prev, n_prev=n_prev, k=k,
                      sees_st=sees_st, gates=gates, bcum=bcum, v=v_ref[smp, :, lanes])
            if with_h:
                lhs = jnp.concatenate([k, c_prev.astype(BF16),
                                       jnp.broadcast_to(n_prev, (16, HEAD_DIM)).astype(BF16)], axis=0)
                st["prod"] = _dot_nt(lhs, q_ref[smp, :, lanes])
            units.append(st)

    rows = {}
    for st in units:
        smp, d, hd = st["smp"], st["d"], st["hd"]
        if (smp, d) not in rows:
            rows[smp, d] = (st["gates"].T, st["bcum"].T)
        gates_t, bcum_t = rows[smp, d]
        ci = d * HEADS + hd
        cf = 2 * HEADS + d * HEADS + hd
        last = CHUNK - 1 if d == 0 else 0
        li_row = gates_t[ci:ci + 1, :]
        bc_row = bcum_t[cf:cf + 1, :]
        b_last = bc_row[:, last:last + 1]
        m_prev = m_s[smp, d, hd]
        v_t = st["v"].T
        a_row = b_last - bc_row + li_row
        m_new = jnp.maximum(b_last + m_prev, jnp.max(a_row, axis=-1, keepdims=True))
        w_row = jnp.exp(a_row - m_new)
        lhs = jnp.concatenate([v_t * w_row, jnp.broadcast_to(w_row, (16, CHUNK))], axis=0)
        st["upd"] = _dot(lhs.astype(BF16), st["k"])
        st.update(bc_row=bc_row, m_prev=m_prev, m_new=m_new, v_t=v_t,
                  decay=jnp.exp(b_last + m_prev - m_new),
                  u_col=st["gates"][:, ci:ci + 1] - st["bcum"][:, cf:cf + 1])

    if with_h:
        for st in units:
            prod, bc_row = st["prod"], st["bc_row"]
            g = bc_row + st["m_prev"]
            dmat = jnp.where(st["sees_st"], st["u_col"] + bc_row, -jnp.inf)
            m_t = jnp.maximum(g, jnp.max(dmat, axis=0, keepdims=True))
            inter = jnp.exp(g - m_t)
            s = prod[0:CHUNK] * jnp.exp(dmat - m_t)
            st["pv"] = _dot(st["v_t"].astype(BF16), s.astype(BF16))
            st["num0"] = inter * prod[CHUNK:2 * CHUNK]
            den = inter * prod[2 * CHUNK:2 * CHUNK + 1] + jnp.sum(s, axis=0, keepdims=True)
            st["scale"] = 1.0 / jnp.maximum(jnp.abs(den), jnp.exp(-m_t))

    for st in units:
        smp, d, hd = st["smp"], st["d"], st["hd"]
        if with_h:
            h_refs[d][smp, :, st["lanes"]] = ((st["num0"] + st["pv"]) * st["scale"]).T
        c_s[smp, d, hd] = st["decay"] * st["c_prev"] + st["upd"][0:HEAD_DIM]
        n_s[smp, d, hd] = st["decay"] * st["n_prev"] + st["upd"][HEAD_DIM:HEAD_DIM + 1]
        m_s[smp, d, hd] = st["m_new"]


def _mlstm_kernel(*refs, n_chunks, with_h):
    refs = list(refs)
    take = lambda n: [refs.pop(0) for _ in range(n)]
    per_dir = []
    for _ in range(2):
        q_ref = take(1)[0] if with_h else None
        k_ref, v_ref, g_ref = take(3)
        per_dir.append((q_ref, k_ref, v_ref, g_ref))
    bias_ref = take(1)[0]
    if with_h:
        c0_ref, n0_ref, m0_ref = take(3)
        h_refs = take(2)
        c_out = n_out = m_out = None
    else:
        h_refs = [None, None]
        c_out, n_out, m_out = take(3)
    c_s, n_s, m_s = take(3)

    j = pl.program_id(1)

    @pl.when(j == 0)
    def _():
        if with_h:
            c_s[...] = c0_ref[...]
            n_s[...] = n0_ref[...]
            m_s[...] = m0_ref[...]
        else:
            c_s[...] = jnp.zeros_like(c_s)
            n_s[...] = jnp.zeros_like(n_s)
            m_s[...] = jnp.zeros_like(m_s)

    _mlstm_chunk(per_dir, bias_ref, c_s, n_s, m_s, h_refs)

    if not with_h:
        @pl.when(j == n_chunks - 1)
        def _():
            c_out[...] = c_s[...]
            n_out[...] = n_s[...]
            m_out[...] = m_s[...]


def _mlstm(q, k, p, v_blk, gates, bias, state):
    b, n, _ = k.shape
    n_chunks = n // CHUNK
    with_h = q is not None
    ns = MLSTM_SAMPLES if b % MLSTM_SAMPLES == 0 else 1

    in_specs, args = [], []
    for d in range(2):
        c = (lambda j: j) if d == 0 else (lambda j: n_chunks - 1 - j)
        tok = pl.BlockSpec((ns, CHUNK, GROUP_W), lambda i, j, c=c: (i, c(j), 0))
        if with_h:
            in_specs.append(tok)
            args.append(q)
        in_specs += [tok,
                     pl.BlockSpec((ns, CHUNK, GROUP_W), lambda i, j, c=c: (i, c(j), v_blk)),
                     pl.BlockSpec((ns, CHUNK, GATE_PAD), lambda i, j, c=c: (i, c(j), 0))]
        args += [k, p, gates]
    in_specs.append(pl.BlockSpec((1, GATE_PAD), lambda i, j: (0, 0)))
    args.append(bias)

    c_shape = (2, HEADS, HEAD_DIM, HEAD_DIM)
    v_shape = (2, HEADS, 1, HEAD_DIM)
    c_spec = pl.BlockSpec((ns,) + c_shape, lambda i, j: (i, 0, 0, 0, 0))
    v_spec = pl.BlockSpec((ns,) + v_shape, lambda i, j: (i, 0, 0, 0, 0))
    if with_h:
        in_specs += [c_spec, v_spec, v_spec]
        args += list(state)
        out_specs = [pl.BlockSpec((ns, CHUNK, GROUP_W), lambda i, j: (i, j, 0)),
                     pl.BlockSpec((ns, CHUNK, GROUP_W), lambda i, j: (i, n_chunks - 1 - j, 0))]
        out_shape = [jax.ShapeDtypeStruct((b, n, GROUP_W), F32)] * 2
    else:
        out_specs = [c_spec, v_spec, v_spec]
        out_shape = [jax.ShapeDtypeStruct((b,) + c_shape, F32),
                     jax.ShapeDtypeStruct((b,) + v_shape, F32),
                     jax.ShapeDtypeStruct((b,) + v_shape, F32)]
    return pl.pallas_call(
        functools.partial(_mlstm_kernel, n_chunks=n_chunks, with_h=with_h),
        grid=(b // ns, n_chunks),
        in_specs=in_specs,
        out_specs=out_specs,
        out_shape=out_shape,
        scratch_shapes=[pltpu.VMEM((ns,) + c_shape, F32), pltpu.VMEM((ns,) + v_shape, F32),
                        pltpu.VMEM((ns,) + v_shape, F32)],
        compiler_params=_params("arbitrary", "arbitrary"),
        name="mlstm" if with_h else "mlstm_ctx_state",
    )(*args)


def _layer_norm(x):
    mu = jnp.mean(x, axis=-1, keepdims=True)
    var = jnp.mean(jnp.square(x - mu), axis=-1, keepdims=True)
    return (x - mu) * lax.rsqrt(var + EPS)


def _postmix_kernel(u_ref, vg_ref, o_ref, hf_ref, hb_ref, x_ref, mod_ref, ws_ref, bs_ref, ng_ref,
                    wout_ref, fg_ref, wr_ref, x1_ref, h2_ref, aff_ref, ycat_s):
    tm = x_ref.shape[1]
    b = pl.program_id(0)
    mod = lambda k: mod_ref[pl.ds(b, 1), k * D_MODEL:(k + 1) * D_MODEL]
    wr_hi, wr_lo = _split2(wr_ref[...])
    expert_lane = _iota2((ROW_GROUP, 128), 1) < N_EXPERTS

    for r in range(tm // ROW_GROUP):
        rows = slice(r * ROW_GROUP, (r + 1) * ROW_GROUP)

        for c in range(r * ROW_GROUP // CHUNK, (r + 1) * ROW_GROUP // CHUNK):
            crows = slice(c * CHUNK, (c + 1) * CHUNK)
            u = jax.nn.gelu(u_ref[0, crows, :])
            v = _layer_norm(jax.nn.gelu(vg_ref[0, crows, :])).astype(BF16)
            for hd in range(HEADS):
                lanes = slice(hd * HEAD_DIM, (hd + 1) * HEAD_DIM)
                s = _dot(ws_ref[hd], v[:, lanes]) + bs_ref[hd]
                ycat_s[crows, lanes] = (u[:, lanes] * s).astype(BF16)

        hsum = hf_ref[0, rows, :] + hb_ref[0, rows, :]
        o = jax.nn.sigmoid(o_ref[0, rows, :])
        for hd in range(HEADS):
            lanes = slice(hd * HEAD_DIM, (hd + 1) * HEAD_DIM)
            hn = _layer_norm(hsum[:, lanes]) * ng_ref[:, lanes]
            ycat_s[rows, GROUP_W + hd * HEAD_DIM:GROUP_W + (hd + 1) * HEAD_DIM] = (
                o[:, lanes] * hn).astype(BF16)

        y = _dot(ycat_s[rows, :], wout_ref[...])
        x1 = x_ref[0, rows, :] + mod(2) * y
        x1_ref[0, rows, :] = x1

        n2 = x1 * lax.rsqrt(jnp.mean(x1 * x1, axis=-1, keepdims=True) + EPS) * fg_ref[...]
        h2 = n2 * (1.0 + mod(4)) + mod(3)
        h2_hi, h2_lo = _split2(h2)
        h2_ref[0, rows, :] = h2_hi

        logits = _dot(h2_hi, wr_hi) + (_dot(h2_lo, wr_hi) + _dot(h2_hi, wr_lo))
        logits = jnp.where(expert_lane, logits, -jnp.inf)
        e = jnp.exp(logits - jnp.max(logits, axis=-1, keepdims=True))
        aff = e / jnp.sum(e, axis=-1, keepdims=True)
        aff_ref[0, :, rows] = aff.T[0:N_EXPERTS, :]


def _postmix(p, hf, hb, x, mod, ws, bs, ng, wout, fg, wr, tm):
    b, n, _ = x.shape
    tok = lambda blk: pl.BlockSpec((1, tm, GROUP_W), lambda i, j: (i, j, blk))
    full = lambda a: pl.BlockSpec(a.shape, lambda i, j: (0,) * a.ndim)
    return pl.pallas_call(
        _postmix_kernel,
        grid=(b, n // tm),
        in_specs=[tok(P_U), tok(P_VG), tok(P_O), tok(0), tok(0),
                  pl.BlockSpec((1, tm, D_MODEL), lambda i, j: (i, j, 0)),
                  full(mod), full(ws), full(bs), full(ng), full(wout), full(fg), full(wr)],
        out_specs=[pl.BlockSpec((1, tm, D_MODEL), lambda i, j: (i, j, 0)),
                   pl.BlockSpec((1, tm, D_MODEL), lambda i, j: (i, j, 0)),
                   pl.BlockSpec((1, N_EXPERTS, tm), lambda i, j: (i, 0, j))],
        out_shape=[jax.ShapeDtypeStruct((b, n, D_MODEL), F32),
                   jax.ShapeDtypeStruct((b, n, D_MODEL), BF16),
                   jax.ShapeDtypeStruct((b, N_EXPERTS, n), F32)],
        scratch_shapes=[pltpu.VMEM((tm, D_MODEL), BF16)],
        compiler_params=_params("arbitrary", "arbitrary"),
        name="postmix",
    )(p, p, p, hf, hb, x, mod, ws, bs, ng, wout, fg, wr)


def _cumsum_lanes(x, upper):
    carry = jnp.zeros((x.shape[0], 1), F32)
    outs, before = [], []
    for j in range(x.shape[1] // 128):
        before.append(carry)
        c = _dot(x[:, j * 128:(j + 1) * 128].astype(BF16), upper) + carry
        outs.append(c)
        carry = c[:, 127:128]
    return jnp.concatenate(outs, axis=1), before


def _rows_to_lanes(x, fill):
    pad = jnp.full((128 - x.shape[0], 128), fill, F32)
    return jnp.concatenate([x, pad], axis=0).T


def _route_kernel(aff_ref, slot_ref, w_ref, slot_t_ref, first_t_ref, first_ref, *, capacity, n_e):
    aff = aff_ref[...]
    cap = float(capacity)
    thr_bits = jnp.zeros((aff.shape[0], 1), jnp.int32)
    for bit in range(30, -1, -1):
        cand = thr_bits | (1 << bit)
        cnt = jnp.sum(jnp.where(aff >= pltpu.bitcast(cand, F32), 1.0, 0.0), axis=-1, keepdims=True)
        thr_bits = jnp.where(cnt >= cap, cand, thr_bits)
    thr = pltpu.bitcast(thr_bits, F32)
    upper = jnp.where(_iota2((128, 128), 0) <= _iota2((128, 128), 1), 1.0, 0.0).astype(BF16)
    above = jnp.where(aff > thr, 1.0, 0.0)
    tied = jnp.where(aff == thr, 1.0, 0.0)
    need = cap - jnp.sum(above, axis=-1, keepdims=True)
    sel = above + tied * jnp.where(_cumsum_lanes(tied, upper)[0] <= need, 1.0, 0.0)
    count, before = _cumsum_lanes(sel, upper)
    slot = jnp.where(sel > 0.0, count - 1.0, -1.0)
    slot_ref[...] = slot
    w_ref[...] = jnp.where(sel > 0.0, aff, 0.0)

    n_blocks = len(before)
    lane = _iota2((1, 128), 1)
    first = jnp.zeros((aff.shape[0], 128), F32)
    for j in range(n_blocks):
        first = jnp.where(lane == j, before[j], first)
    first_ref[...] = first
    for smp in range(aff.shape[0] // n_e):
        rows = slice(smp * n_e, (smp + 1) * n_e)
        for j in range(n_blocks):
            slot_t_ref[smp, j * 128:(j + 1) * 128, :] = _rows_to_lanes(
                slot[rows, j * 128:(j + 1) * 128], -1.0)
        first_t_ref[smp] = _rows_to_lanes(first[rows, :], 0.0)[0:n_blocks, :]


def _route(aff_t, capacity):
    b, e, n = aff_t.shape
    assert n // 128 <= 128 and e <= 128 and e % 8 == 0
    rows = pl.BlockSpec((b * e, n), lambda i: (0, 0))
    slot, w, slot_t, first_t, first = pl.pallas_call(
        functools.partial(_route_kernel, capacity=capacity, n_e=e),
        grid=(1,),
        in_specs=[rows],
        out_specs=[rows, rows,
                   pl.BlockSpec((b, n, 128), lambda i: (0, 0, 0)),
                   pl.BlockSpec((b, n // 128, 128), lambda i: (0, 0, 0)),
                   pl.BlockSpec((b * e, 128), lambda i: (0, 0))],
        out_shape=[jax.ShapeDtypeStruct((b * e, n), F32)] * 2
        + [jax.ShapeDtypeStruct((b, n, 128), F32), jax.ShapeDtypeStruct((b, n // 128, 128), F32),
           jax.ShapeDtypeStruct((b * e, 128), F32)],
        compiler_params=_params("arbitrary"),
        name="route",
    )(aff_t.reshape(b * e, n))
    return slot.reshape(b, e, n), w.reshape(b, e, n), slot_t, first_t, first.reshape(b, e, 128)


GATHER_TILE = 256
GATHER_ROWS = 128
GATHER_GROUP = 4


def _gather_kernel(before_ref, fits_ref, h2_ref, slot_ref, w_ref, xe_ref, gate_ref, xe_s, gate_s):
    b = pl.program_id(0)
    g = pl.program_id(1)
    n_g, cap = xe_ref.shape[1], xe_ref.shape[2]
    n = h2_ref.shape[1]
    n_tiles = n // GATHER_TILE

    @pl.when(fits_ref[b, g] != 0)
    def _():
        xe_s[:, 0:16, :] = jnp.zeros((n_g, 16, D_MODEL), BF16)
        gate_s[...] = jnp.zeros_like(gate_s)
        row_id = _iota2((GATHER_ROWS, 1), 0).astype(F32)
        for j in range(n_tiles):
            starts, onehots = [], []
            for e in range(n_g):
                start = pl.multiple_of((before_ref[b, g * n_g + e, j] // 16) * 16, 16)
                hit = (slot_ref[0, e, j:j + 1, :] - start.astype(F32)) == row_id
                onehots.append(jnp.where(hit, 1.0, 0.0).astype(BF16))
                gate_s[e, pl.ds(start, GATHER_ROWS), :] += jnp.sum(
                    jnp.where(hit, w_ref[0, e, j:j + 1, :], 0.0), axis=1, keepdims=True)
                starts.append(start)
            rows = _dot(jnp.concatenate(onehots, axis=0),
                        h2_ref[0, j * GATHER_TILE:(j + 1) * GATHER_TILE, :]).astype(BF16)
            for e, start in enumerate(starts):
                r0 = e * GATHER_ROWS
                xe_s[e, pl.ds(start, 16), :] += rows[r0:r0 + 16]
                xe_s[e, pl.ds(start + 16, GATHER_ROWS - 16), :] = rows[r0 + 16:r0 + GATHER_ROWS]
        xe_ref[0] = xe_s[:, 0:cap, :]
        gate_ref[0] = gate_s[:, 0:cap, :]

    @pl.when(fits_ref[b, g] == 0)
    def _():
        slot_id = _iota2((cap, 1), 0).astype(F32)
        for e in range(n_g):
            xe = jnp.zeros((cap, D_MODEL), F32)
            gate = jnp.zeros((cap, 1), F32)
            for j in range(n_tiles):
                hit = slot_ref[0, e, j:j + 1, :] == slot_id
                xe = xe + _dot(jnp.where(hit, 1.0, 0.0).astype(BF16),
                               h2_ref[0, j * GATHER_TILE:(j + 1) * GATHER_TILE, :])
                gate = gate + jnp.sum(jnp.where(hit, w_ref[0, e, j:j + 1, :], 0.0), axis=1,
                                      keepdims=True)
            xe_ref[0, e] = xe.astype(BF16)
            gate_ref[0, e] = gate


def _gather(h2, slot, w, first, capacity):
    b, n, _ = h2.shape
    n_e = slot.shape[1]
    n_tiles = n // GATHER_TILE
    assert n_e % GATHER_GROUP == 0 and GATHER_TILE % 128 == 0 and capacity % 16 == 0
    before = first[:, :, 0:n // 128:GATHER_TILE // 128].astype(jnp.int32)
    after = jnp.concatenate([before[:, :, 1:], jnp.full((b, n_e, 1), capacity, jnp.int32)], axis=2)
    fits = jnp.all((after - (before // 16) * 16) < GATHER_ROWS, axis=2)
    fits = jnp.all(fits.reshape(b, n_e // GATHER_GROUP, GATHER_GROUP), axis=2).astype(jnp.int32)
    tiles = lambda a: a.reshape(b, n_e, n_tiles, GATHER_TILE)
    rows = pl.BlockSpec((1, GATHER_GROUP, n_tiles, GATHER_TILE), lambda i, g, *_: (i, g, 0, 0))
    return pl.pallas_call(
        _gather_kernel,
        grid_spec=pltpu.PrefetchScalarGridSpec(
            num_scalar_prefetch=2,
            grid=(b, n_e // GATHER_GROUP),
            in_specs=[pl.BlockSpec((1, n, D_MODEL), lambda i, g, *_: (i, 0, 0)), rows, rows],
            out_specs=[pl.BlockSpec((1, GATHER_GROUP, capacity, D_MODEL), lambda i, g, *_: (i, g, 0, 0)),
                       pl.BlockSpec((1, GATHER_GROUP, capacity, 1), lambda i, g, *_: (i, g, 0, 0))],
            scratch_shapes=[pltpu.VMEM((GATHER_GROUP, capacity + GATHER_ROWS, D_MODEL), BF16),
                            pltpu.VMEM((GATHER_GROUP, capacity + GATHER_ROWS, 1), F32)]),
        out_shape=[jax.ShapeDtypeStruct((b, n_e, capacity, D_MODEL), BF16),
                   jax.ShapeDtypeStruct((b, n_e, capacity, 1), F32)],
        compiler_params=_params("arbitrary", "arbitrary"),
        name="gather",
    )(before, fits, h2, tiles(slot), tiles(w))


def _ffn_kernel(xe_ref, gate_ref, wg_ref, wu_ref, wd_ref, ye_ref, acc_s):
    ft = pl.program_id(1)
    last = pl.num_programs(1) - 1

    def step(first, final):
        wg = wg_ref[0].astype(BF16)
        wu = wu_ref[0].astype(BF16)
        wd = wd_ref[0].astype(BF16)
        for i in range(xe_ref.shape[0]):
            xe = xe_ref[i, 0]
            act = jax.nn.silu(_dot(xe, wg)) * _dot(xe, wu)
            part = _dot(act.astype(BF16), wd)
            total = part if first else acc_s[i] + part
            if final:
                ye_ref[i] = (total * gate_ref[i, 0]).astype(BF16)
            else:
                acc_s[i] = total

    pl.when(ft == 0)(lambda: step(True, False))
    pl.when(jnp.logical_and(ft > 0, ft < last))(lambda: step(False, False))
    pl.when(ft == last)(lambda: step(False, True))


def _ffn(xe, gate, wg, wu, wd, f_tile):
    b, n_e, cap, _ = xe.shape
    assert D_EXPERT // f_tile >= 2
    return pl.pallas_call(
        _ffn_kernel,
        grid=(n_e, D_EXPERT // f_tile),
        in_specs=[pl.BlockSpec((b, 1, cap, D_MODEL), lambda e, f: (0, e, 0, 0)),
                  pl.BlockSpec((b, 1, cap, 1), lambda e, f: (0, e, 0, 0)),
                  pl.BlockSpec((1, D_MODEL, f_tile), lambda e, f: (e, 0, f)),
                  pl.BlockSpec((1, D_MODEL, f_tile), lambda e, f: (e, 0, f)),
                  pl.BlockSpec((1, f_tile, D_MODEL), lambda e, f: (e, f, 0))],
        out_specs=pl.BlockSpec((b, cap, D_MODEL), lambda e, f: (0, e, 0)),
        out_shape=jax.ShapeDtypeStruct((b, n_e * cap, D_MODEL), BF16),
        scratch_shapes=[pltpu.VMEM((b, cap, D_MODEL), F32)],
        compiler_params=_params("arbitrary", "arbitrary"),
        name="ffn",
    )(xe, gate, wg, wu, wd)


COMBINE_WIDE = CHUNK + 16
COMBINE_NARROW = 48


def _combine_kernel(first_ref, narrow_ref, slot_t_ref, first_t_ref, ye_ref, x1_ref, mod_ref, fg_ref,
                    o_ref, acc_s, *, capacity):
    b = pl.program_id(0)
    j = pl.program_id(1)
    n_sub = o_ref.shape[1] // 128
    lane = _iota2((1, 128), 1).astype(F32)

    def scatter(window):
        k_total = N_EXPERTS * window
        for sb in range(n_sub):
            blk = j * n_sub + sb
            slot_t = slot_t_ref[0, sb * 128:(sb + 1) * 128, :]
            start_row = jnp.minimum(jnp.floor(first_t_ref[0, pl.ds(blk, 1), :] * (1.0 / 16.0)) * 16.0,
                                    float(capacity - window))
            k_pos = jnp.where(slot_t >= 0.0, slot_t - start_row + lane * float(window), -1.0)
            cols = []
            for c in range(k_total // 128):
                k_lane = lane + float(128 * c)
                hit = jnp.zeros((128, 128), F32)
                for e in range((128 * c) // window, (128 * c + 127) // window + 1):
                    hit = jnp.where(k_pos[:, e:e + 1] == k_lane, 1.0, hit)
                cols.append(hit.astype(BF16))
            onehot = jnp.concatenate(cols, axis=1)
            rows = []
            for e in range(N_EXPERTS):
                start = jnp.minimum((first_ref[b, blk, e] // 16) * 16, capacity - window)
                rows.append(ye_ref[0, pl.ds(pl.multiple_of(e * capacity + start, 16), window), :])
            acc_s[sb * 128:(sb + 1) * 128, :] = _dot(onehot, jnp.concatenate(rows, axis=0))

    pl.when(narrow_ref[b, j] != 0)(lambda: scatter(COMBINE_NARROW))
    pl.when(narrow_ref[b, j] == 0)(lambda: scatter(COMBINE_WIDE))

    g2 = mod_ref[pl.ds(b, 1), 5 * D_MODEL:6 * D_MODEL]
    x2 = x1_ref[0] + g2 * acc_s[...]
    o_ref[0] = x2 * lax.rsqrt(jnp.mean(x2 * x2, axis=-1, keepdims=True) + EPS) * fg_ref[...]


def _combine(slot_t, first_t, ye, x1, mod, fg, capacity, tm):
    b, n, _ = x1.shape
    n_blocks = n // 128
    for window in (COMBINE_WIDE, COMBINE_NARROW):
        assert (N_EXPERTS * window) % 128 == 0 and window % 16 == 0 and window <= capacity
    assert COMBINE_WIDE >= 128 + 15 and capacity % 16 == 0
    first = first_t[:, :, :N_EXPERTS].astype(jnp.int32)
    after = jnp.concatenate([first[:, 1:], jnp.full((b, 1, N_EXPERTS), capacity, jnp.int32)], axis=1)
    narrow = jnp.all(after - (first // 16) * 16 <= COMBINE_NARROW, axis=2)
    narrow = jnp.all(narrow.reshape(b, n // tm, tm // 128), axis=2).astype(jnp.int32)
    tok = pl.BlockSpec((1, tm, D_MODEL), lambda i, j, *_: (i, j, 0))
    return pl.pallas_call(
        functools.partial(_combine_kernel, capacity=capacity),
        grid_spec=pltpu.PrefetchScalarGridSpec(
            num_scalar_prefetch=2,
            grid=(b, n // tm),
            in_specs=[pl.BlockSpec((1, tm, 128), lambda i, j, *_: (i, j, 0)),
                      pl.BlockSpec((1, n_blocks, 128), lambda i, j, *_: (i, 0, 0)),
                      pl.BlockSpec((1, N_EXPERTS * capacity, D_MODEL), lambda i, j, *_: (i, 0, 0)),
                      tok,
                      pl.BlockSpec(mod.shape, lambda i, j, *_: (0, 0)),
                      pl.BlockSpec((1, D_MODEL), lambda i, j, *_: (0, 0))],
            out_specs=tok,
            scratch_shapes=[pltpu.VMEM((tm, D_MODEL), F32)]),
        out_shape=jax.ShapeDtypeStruct((b, n, D_MODEL), F32),
        compiler_params=_params("arbitrary", "arbitrary"),
        name="combine",
    )(first, narrow, slot_t, first_t, ye, x1, mod, fg)


def kernel(x, c, ctx, c_ctx, w_mod, b_mod, norm_mix_g, w_in, conv_q, conv_k, b_igate, b_fgate,
           gmlp_ws, gmlp_bs, mlstm_norm_g, w_out, norm_ffn_g, w_router, w_gate_e, w_up_e,
           w_down_e, final_g):
    depth = w_mod.shape[0]
    assert depth == 1, "the context stream is only carried as mLSTM states (single layer)"
    batch, seq, _ = x.shape
    assert seq % GRID_W == 0 and seq % CHUNK == 0 and batch + 1 <= MOD_ROWS
    capacity = EC_FACTOR * seq // N_EXPERTS
    ctx_row = batch
    l = 0

    cond = jnp.concatenate([c, c_ctx[None], jnp.zeros((MOD_ROWS - batch - 1, D_MODEL), F32)], axis=0)
    mod = _adaln(cond, w_mod[l], b_mod[l][None])

    row = lambda a: a[None]
    w_bf = w_in[l].astype(BF16)
    cols = lambda *blks: jnp.concatenate([w_bf[:, k * GROUP_W:(k + 1) * GROUP_W] for k in blks], axis=1)
    w_gates = jnp.pad(w_bf[:, MAIN_W:], ((0, 0), (0, GATE_PAD - N_GATES)))
    gate_bias = jnp.pad(jnp.concatenate([b_igate[l].reshape(-1), b_fgate[l].reshape(-1)]),
                        (0, GATE_PAD - N_GATES))[None]
    k_scale = HEAD_DIM ** -0.5

    k_c, v_c, gates_c = _inproj(ctx, mod, row(norm_mix_g[l]), cols(K_BLK), conv_k[l][None], (k_scale,),
                                cols(V_BLK), w_gates, tm=ctx.shape[1], ctx_row=ctx_row)
    state = _mlstm(None, k_c, v_c, 0, gates_c, gate_bias, None)

    q_l, k_l, p, gates = _inproj(x, mod, row(norm_mix_g[l]), cols(Q_BLK, K_BLK),
                                 jnp.stack([conv_q[l], conv_k[l]]), (1.0, k_scale),
                                 cols(U_BLK, VG_BLK, O_BLK, V_BLK), w_gates, tm=512, ctx_row=None)
    h_f, h_b = _mlstm(q_l, k_l, p, P_V, gates, gate_bias, state)

    x1, h2, aff_t = _postmix(p, h_f, h_b, x, mod, gmlp_ws[l].astype(BF16), gmlp_bs[l][:, :, None],
                             row(mlstm_norm_g[l]), w_out[l].astype(BF16), row(norm_ffn_g[l]),
                             jnp.pad(w_router[l], ((0, 0), (0, 128 - N_EXPERTS))), tm=512)

    slot, gate_w, slot_t, first_t, first = _route(aff_t, capacity)

    xe, gate = _gather(h2, slot, gate_w, first, capacity)
    ye = _ffn(xe, gate, w_gate_e[l], w_up_e[l], w_down_e[l], f_tile=512)
    return _combine(slot_t, first_t, ye, x1, mod, row(final_g), capacity, tm=512)
```

```python
import functools

import jax
import jax.numpy as jnp
from jax import lax
from jax.experimental import pallas as pl
from jax.experimental.pallas import tpu as pltpu

F32 = jnp.float32
BF16 = jnp.bfloat16

D_MODEL = 1024
GRID_W = 64
CHUNK = 128
HEADS = 4
GROUP_W = D_MODEL // 2
HEAD_DIM = GROUP_W // HEADS
N_EXPERTS = 16
EC_FACTOR = 2
D_EXPERT = 2 * D_MODEL
EPS = 1e-6
N_GATES = 4 * HEADS
GATE_PAD = 128
MOD_ROWS = 8

U_BLK, VG_BLK, Q_BLK, O_BLK, K_BLK, V_BLK = 0, 1, 2, 3, 4, 5
MAIN_W = 6 * GROUP_W
P_U, P_VG, P_O, P_V = 0, 1, 2, 3

VMEM_LIMIT = 56 * 1024 * 1024
ROW_GROUP = 2 * CHUNK
MLSTM_SAMPLES = 4

def _params(*sem):
    return pltpu.CompilerParams(dimension_semantics=sem, vmem_limit_bytes=VMEM_LIMIT)


def _dot(a, b):
    return jnp.dot(a, b, preferred_element_type=F32)


def _dot_nt(a, b):
    return lax.dot_general(a, b, (((1,), (1,)), ((), ())), preferred_element_type=F32)


def _split2(a):
    hi = a.astype(BF16)
    lo = (a - hi.astype(F32)).astype(BF16)
    return hi, lo


def _dot3(a, b):
    ah, al = _split2(a)
    bh, bl = _split2(b)
    return _dot(ah, bh) + (_dot(al, bh) + _dot(ah, bl))


def _dot_exact01(tri, x):
    x1 = x.astype(BF16)
    r1 = x - x1.astype(F32)
    x2 = r1.astype(BF16)
    x3 = (r1 - x2.astype(F32)).astype(BF16)
    return _dot(tri, x1) + (_dot(tri, x2) + _dot(tri, x3))


def _iota2(shape, dim):
    return lax.broadcasted_iota(jnp.int32, shape, dim)


def _row_to_col(row):
    n = row.shape[1] // 128
    eye = _iota2((128, 128), 0) == _iota2((128, 128), 1)
    cols = [jnp.sum(jnp.where(eye, row[:, j * 128:(j + 1) * 128], 0.0), axis=1, keepdims=True)
            for j in range(n)]
    return cols[0] if n == 1 else jnp.concatenate(cols, axis=0)


def _adaln_kernel(cond_ref, w_ref, b_ref, o_ref):
    o_ref[...] = _dot3(jax.nn.silu(cond_ref[...]), w_ref[...]) + b_ref[...]


def _adaln(cond, w, b):
    n_out = w.shape[1]
    tn = D_MODEL
    return pl.pallas_call(
        _adaln_kernel,
        grid=(n_out // tn,),
        in_specs=[pl.BlockSpec((MOD_ROWS, D_MODEL), lambda j: (0, 0)),
                  pl.BlockSpec((D_MODEL, tn), lambda j: (0, j)),
                  pl.BlockSpec((1, tn), lambda j: (0, j))],
        out_specs=pl.BlockSpec((MOD_ROWS, tn), lambda j: (0, j)),
        out_shape=jax.ShapeDtypeStruct((MOD_ROWS, n_out), F32),
        compiler_params=_params("arbitrary"),
        name="adaln",
    )(cond, w, b)


def _inproj_kernel(x_ref, xprev_ref, xnext_ref, mod_ref, g_ref, taps_ref, w_ref, *outs, ctx_row, scales):
    conv_refs, (p_ref, gate_ref) = outs[:len(scales)], outs[len(scales):]
    row = pl.program_id(0) if ctx_row is None else ctx_row
    j = pl.program_id(1)
    sh = mod_ref[pl.ds(row, 1), 0:D_MODEL]
    sc = mod_ref[pl.ds(row, 1), D_MODEL:2 * D_MODEL]

    def modulated(x):
        y = x * lax.rsqrt(jnp.mean(x * x, axis=-1, keepdims=True) + EPS) * g_ref[...]
        return y * (1.0 + sc) + sh

    n_groups = x_ref.shape[1] // ROW_GROUP
    before = modulated(xprev_ref[0]) * (j > 0).astype(F32)
    for r in range(n_groups):
        rows = slice(r * ROW_GROUP, (r + 1) * ROW_GROUP)
        h = modulated(x_ref[0, rows, :])
        if r == n_groups - 1:
            after = modulated(xnext_ref[0]) * (j < pl.num_programs(1) - 1).astype(F32)
        else:
            after = modulated(x_ref[0, (r + 1) * ROW_GROUP:(r + 1) * ROW_GROUP + 8, :])
        ext = jnp.concatenate([before, h, after], axis=0).astype(BF16)
        before = h[ROW_GROUP - 8:, :]
        y = _dot(ext, w_ref[...])
        n_ext, n_conv = ext.shape[0], len(scales) * GROUP_W
        p_ref[0, rows, :] = y[8:8 + ROW_GROUP, n_conv:n_conv + p_ref.shape[2]]
        gate_ref[0, rows, :] = y[8:8 + ROW_GROUP, n_conv + p_ref.shape[2]:]
        y_before = pltpu.roll(y, 1, axis=0)[8:8 + ROW_GROUP]
        y_after = pltpu.roll(y, n_ext - 1, axis=0)[8:8 + ROW_GROUP]
        y_here = y[8:8 + ROW_GROUP]
        for s, scale in enumerate(scales):
            lanes = slice(s * GROUP_W, (s + 1) * GROUP_W)
            z = (y_before[:, lanes] * taps_ref[s, 0:1, :] + y_here[:, lanes] * taps_ref[s, 1:2, :]
                 + y_after[:, lanes] * taps_ref[s, 2:3, :])
            conv_refs[s][0, rows, :] = (jax.nn.silu(z) * scale).astype(BF16)


def _inproj(x, mod, g, w, taps, scales, tm, ctx_row):
    b, n, _ = x.shape
    wn = w.shape[1] - len(scales) * GROUP_W - GATE_PAD
    assert tm % ROW_GROUP == 0 and n % tm == 0 and wn % 128 == 0
    rows8 = tm // 8
    last8 = n // 8 - 1
    full = lambda a: pl.BlockSpec(a.shape, lambda i, j: (0,) * a.ndim)
    conv_spec = pl.BlockSpec((1, tm, GROUP_W), lambda i, j: (i, j, 0))
    return pl.pallas_call(
        functools.partial(_inproj_kernel, ctx_row=ctx_row, scales=tuple(scales)),
        grid=(b, n // tm),
        in_specs=[pl.BlockSpec((1, tm, D_MODEL), lambda i, j: (i, j, 0)),
                  pl.BlockSpec((1, 8, D_MODEL), lambda i, j: (i, jnp.maximum(j * rows8 - 1, 0), 0)),
                  pl.BlockSpec((1, 8, D_MODEL), lambda i, j: (i, jnp.minimum((j + 1) * rows8, last8), 0)),
                  full(mod), full(g), full(taps), full(w)],
        out_specs=[conv_spec] * len(scales)
        + [pl.BlockSpec((1, tm, wn), lambda i, j: (i, j, 0)),
           pl.BlockSpec((1, tm, GATE_PAD), lambda i, j: (i, j, 0))],
        out_shape=[jax.ShapeDtypeStruct((b, n, GROUP_W), BF16)] * len(scales)
        + [jax.ShapeDtypeStruct((b, n, wn), F32), jax.ShapeDtypeStruct((b, n, GATE_PAD), F32)],
        compiler_params=_params("arbitrary", "arbitrary"),
        name="inproj",
    )(x, x, x, mod, g, taps, w)


def _mlstm_chunk(per_dir, bias_ref, c_s, n_s, m_s, h_refs):
    with_h = h_refs[0] is not None
    i0 = _iota2((CHUNK, CHUNK), 0)
    i1 = _iota2((CHUNK, CHUNK), 1)

    units = []
    for smp, d in [(smp, d) for smp in range(c_s.shape[0]) for d in range(2)]:
        q_ref, k_ref, v_ref, g_ref = per_dir[d]
        sees_ts = (i1 <= i0) if d == 0 else (i1 >= i0)
        sees_st = (i0 <= i1) if d == 0 else (i0 >= i1)
        tri = jnp.where(sees_ts, 1.0, 0.0).astype(BF16)
        gates = g_ref[smp] + bias_ref[...]
        bcum = _dot_exact01(tri, jax.nn.log_sigmoid(gates))
        for hd in range(HEADS):
            lanes = slice(hd * HEAD_DIM, (hd + 1) * HEAD_DIM)
            c_prev = c_s[smp, d, hd]
            n_prev = n_s[smp, d, hd]
            k = k_ref[smp, :, lanes]
            st = dict(smp=smp, d=d, hd=hd, lanes=lanes, c_prev=c_prev, n_prev=n_prev, k=k,
                      sees_st=sees_st, gates=gates, bcum=bcum, v=v_ref[smp, :, lanes])
            if with_h:
                lhs = jnp.concatenate([k, c_prev.astype(BF16),
                                       jnp.broadcast_to(n_prev, (16, HEAD_DIM)).astype(BF16)], axis=0)
                st["prod"] = _dot_nt(lhs, q_ref[smp, :, lanes])
            units.append(st)

    rows = {}
    for st in units:
        smp, d, hd = st["smp"], st["d"], st["hd"]
        if (smp, d) not in rows:
            rows[smp, d] = (st["gates"].T, st["bcum"].T)
        gates_t, bcum_t = rows[smp, d]
        ci = d * HEADS + hd
        cf = 2 * HEADS + d * HEADS + hd
        last = CHUNK - 1 if d == 0 else 0
        li_row = gates_t[ci:ci + 1, :]
        bc_row = bcum_t[cf:cf + 1, :]
        b_last = bc_row[:, last:last + 1]
        m_prev = m_s[smp, d, hd]
        v_t = st["v"].T
        a_row = b_last - bc_row + li_row
        m_new = jnp.maximum(b_last + m_prev, jnp.max(a_row, axis=-1, keepdims=True))
        w_row = jnp.exp(a_row - m_new)
        lhs = jnp.concatenate([v_t * w_row, jnp.broadcast_to(w_row, (16, CHUNK))], axis=0)
        st["upd"] = _dot(lhs.astype(BF16), st["k"])
        st.update(bc_row=bc_row, m_prev=m_prev, m_new=m_new, v_t=v_t,
                  decay=jnp.exp(b_last + m_prev - m_new),
                  u_col=st["gates"][:, ci:ci + 1] - st["bcum"][:, cf:cf + 1])

    if with_h:
        for st in units:
            prod, bc_row = st["prod"], st["bc_row"]
            g = bc_row + st["m_prev"]
            dmat = jnp.where(st["sees_st"], st["u_col"] + bc_row, -jnp.inf)
            m_t = jnp.maximum(g, jnp.max(dmat, axis=0, keepdims=True))
            inter = jnp.exp(g - m_t)
            s = prod[0:CHUNK] * jnp.exp(dmat - m_t)
            st["pv"] = _dot(st["v_t"].astype(BF16), s.astype(BF16))
            st["num0"] = inter * prod[CHUNK:2 * CHUNK]
            den = inter * prod[2 * CHUNK:2 * CHUNK + 1] + jnp.sum(s, axis=0, keepdims=True)
            st["scale"] = 1.0 / jnp.maximum(jnp.abs(den), jnp.exp(-m_t))

    for st in units:
        smp, d, hd = st["smp"], st["d"], st["hd"]
        if with_h:
            h_refs[d][smp, :, st["lanes"]] = ((st["num0"] + st["pv"]) * st["scale"]).T
        c_s[smp, d, hd] = st["decay"] * st["c_prev"] + st["upd"][0:HEAD_DIM]
        n_s[smp, d, hd] = st["decay"] * st["n_prev"] + st["upd"][HEAD_DIM:HEAD_DIM + 1]
        m_s[smp, d, hd] = st["m_new"]


def _mlstm_kernel(*refs, n_chunks, with_h):
    refs = list(refs)
    take = lambda n: [refs.pop(0) for _ in range(n)]
    per_dir = []
    for _ in range(2):
        q_ref = take(1)[0] if with_h else None
        k_ref, v_ref, g_ref = take(3)
        per_dir.append((q_ref, k_ref, v_ref, g_ref))
    bias_ref = take(1)[0]
    if with_h:
        c0_ref, n0_ref, m0_ref = take(3)
        h_refs = take(2)
        c_out = n_out = m_out = None
    else:
        h_refs = [None, None]
        c_out, n_out, m_out = take(3)
    c_s, n_s, m_s = take(3)

    j = pl.program_id(1)

    @pl.when(j == 0)
    def _():
        if with_h:
            c_s[...] = c0_ref[...]
            n_s[...] = n0_ref[...]
            m_s[...] = m0_ref[...]
        else:
            c_s[...] = jnp.zeros_like(c_s)
            n_s[...] = jnp.zeros_like(n_s)
            m_s[...] = jnp.zeros_like(m_s)

    _mlstm_chunk(per_dir, bias_ref, c_s, n_s, m_s, h_refs)

    if not with_h:
        @pl.when(j == n_chunks - 1)
        def _():
            c_out[...] = c_s[...]
            n_out[...] = n_s[...]
            m_out[...] = m_s[...]


def _mlstm(q, k, p, v_blk, gates, bias, state):
    b, n, _ = k.shape
    n_chunks = n // CHUNK
    with_h = q is not None
    ns = MLSTM_SAMPLES if b % MLSTM_SAMPLES == 0 else 1

    in_specs, args = [], []
    for d in range(2):
        c = (lambda j: j) if d == 0 else (lambda j: n_chunks - 1 - j)
        tok = pl.BlockSpec((ns, CHUNK, GROUP_W), lambda i, j, c=c: (i, c(j), 0))
        if with_h:
            in_specs.append(tok)
            args.append(q)
        in_specs += [tok,
                     pl.BlockSpec((ns, CHUNK, GROUP_W), lambda i, j, c=c: (i, c(j), v_blk)),
                     pl.BlockSpec((ns, CHUNK, GATE_PAD), lambda i, j, c=c: (i, c(j), 0))]
        args += [k, p, gates]
    in_specs.append(pl.BlockSpec((1, GATE_PAD), lambda i, j: (0, 0)))
    args.append(bias)

    c_shape = (2, HEADS, HEAD_DIM, HEAD_DIM)
    v_shape = (2, HEADS, 1, HEAD_DIM)
    c_spec = pl.BlockSpec((ns,) + c_shape, lambda i, j: (i, 0, 0, 0, 0))
    v_spec = pl.BlockSpec((ns,) + v_shape, lambda i, j: (i, 0, 0, 0, 0))
    if with_h:
        in_specs += [c_spec, v_spec, v_spec]
        args += list(state)
        out_specs = [pl.BlockSpec((ns, CHUNK, GROUP_W), lambda i, j: (i, j, 0)),
                     pl.BlockSpec((ns, CHUNK, GROUP_W), lambda i, j: (i, n_chunks - 1 - j, 0))]
        out_shape = [jax.ShapeDtypeStruct((b, n, GROUP_W), F32)] * 2
    else:
        out_specs = [c_spec, v_spec, v_spec]
        out_shape = [jax.ShapeDtypeStruct((b,) + c_shape, F32),
                     jax.ShapeDtypeStruct((b,) + v_shape, F32),
                     jax.ShapeDtypeStruct((b,) + v_shape, F32)]
    return pl.pallas_call(
        functools.partial(_mlstm_kernel, n_chunks=n_chunks, with_h=with_h),
        grid=(b // ns, n_chunks),
        in_specs=in_specs,
        out_specs=out_specs,
        out_shape=out_shape,
        scratch_shapes=[pltpu.VMEM((ns,) + c_shape, F32), pltpu.VMEM((ns,) + v_shape, F32),
                        pltpu.VMEM((ns,) + v_shape, F32)],
        compiler_params=_params("arbitrary", "arbitrary"),
        name="mlstm" if with_h else "mlstm_ctx_state",
    )(*args)


def _layer_norm(x):
    mu = jnp.mean(x, axis=-1, keepdims=True)
    var = jnp.mean(jnp.square(x - mu), axis=-1, keepdims=True)
    return (x - mu) * lax.rsqrt(var + EPS)


def _postmix_kernel(u_ref, vg_ref, o_ref, hf_ref, hb_ref, x_ref, mod_ref, ws_ref, bs_ref, ng_ref,
                    wout_ref, fg_ref, wr_ref, x1_ref, h2_ref, aff_ref, ycat_s):
    tm = x_ref.shape[1]
    b = pl.program_id(0)
    mod = lambda k: mod_ref[pl.ds(b, 1), k * D_MODEL:(k + 1) * D_MODEL]
    wr_hi, wr_lo = _split2(wr_ref[...])
    expert_lane = _iota2((ROW_GROUP, 128), 1) < N_EXPERTS

    for r in range(tm // ROW_GROUP):
        rows = slice(r * ROW_GROUP, (r + 1) * ROW_GROUP)

        for c in range(r * ROW_GROUP // CHUNK, (r + 1) * ROW_GROUP // CHUNK):
            crows = slice(c * CHUNK, (c + 1) * CHUNK)
            u = jax.nn.gelu(u_ref[0, crows, :])
            v = _layer_norm(jax.nn.gelu(vg_ref[0, crows, :])).astype(BF16)
            for hd in range(HEADS):
                lanes = slice(hd * HEAD_DIM, (hd + 1) * HEAD_DIM)
                s = _dot(ws_ref[hd], v[:, lanes]) + bs_ref[hd]
                ycat_s[crows, lanes] = (u[:, lanes] * s).astype(BF16)

        hsum = hf_ref[0, rows, :] + hb_ref[0, rows, :]
        o = jax.nn.sigmoid(o_ref[0, rows, :])
        for hd in range(HEADS):
            lanes = slice(hd * HEAD_DIM, (hd + 1) * HEAD_DIM)
            hn = _layer_norm(hsum[:, lanes]) * ng_ref[:, lanes]
            ycat_s[rows, GROUP_W + hd * HEAD_DIM:GROUP_W + (hd + 1) * HEAD_DIM] = (
                o[:, lanes] * hn).astype(BF16)

        y = _dot(ycat_s[rows, :], wout_ref[...])
        x1 = x_ref[0, rows, :] + mod(2) * y
        x1_ref[0, rows, :] = x1

        n2 = x1 * lax.rsqrt(jnp.mean(x1 * x1, axis=-1, keepdims=True) + EPS) * fg_ref[...]
        h2 = n2 * (1.0 + mod(4)) + mod(3)
        h2_hi, h2_lo = _split2(h2)
        h2_ref[0, rows, :] = h2_hi

        logits = _dot(h2_hi, wr_hi) + (_dot(h2_lo, wr_hi) + _dot(h2_hi, wr_lo))
        logits = jnp.where(expert_lane, logits, -jnp.inf)
        e = jnp.exp(logits - jnp.max(logits, axis=-1, keepdims=True))
        aff = e / jnp.sum(e, axis=-1, keepdims=True)
        aff_ref[0, :, rows] = aff.T[0:N_EXPERTS, :]


def _postmix(p, hf, hb, x, mod, ws, bs, ng, wout, fg, wr, tm):
    b, n, _ = x.shape
    tok = lambda blk: pl.BlockSpec((1, tm, GROUP_W), lambda i, j: (i, j, blk))
    full = lambda a: pl.BlockSpec(a.shape, lambda i, j: (0,) * a.ndim)
    return pl.pallas_call(
        _postmix_kernel,
        grid=(b, n // tm),
        in_specs=[tok(P_U), tok(P_VG), tok(P_O), tok(0), tok(0),
                  pl.BlockSpec((1, tm, D_MODEL), lambda i, j: (i, j, 0)),
                  full(mod), full(ws), full(bs), full(ng), full(wout), full(fg), full(wr)],
        out_specs=[pl.BlockSpec((1, tm, D_MODEL), lambda i, j: (i, j, 0)),
                   pl.BlockSpec((1, tm, D_MODEL), lambda i, j: (i, j, 0)),
                   pl.BlockSpec((1, N_EXPERTS, tm), lambda i, j: (i, 0, j))],
        out_shape=[jax.ShapeDtypeStruct((b, n, D_MODEL), F32),
                   jax.ShapeDtypeStruct((b, n, D_MODEL), BF16),
                   jax.ShapeDtypeStruct((b, N_EXPERTS, n), F32)],
        scratch_shapes=[pltpu.VMEM((tm, D_MODEL), BF16)],
        compiler_params=_params("arbitrary", "arbitrary"),
        name="postmix",
    )(p, p, p, hf, hb, x, mod, ws, bs, ng, wout, fg, wr)


def _cumsum_lanes(x, upper):
    carry = jnp.zeros((x.shape[0], 1), F32)
    outs, before = [], []
    for j in range(x.shape[1] // 128):
        before.append(carry)
        c = _dot(x[:, j * 128:(j + 1) * 128].astype(BF16), upper) + carry
        outs.append(c)
        carry = c[:, 127:128]
    return jnp.concatenate(outs, axis=1), before


def _rows_to_lanes(x, fill):
    pad = jnp.full((128 - x.shape[0], 128), fill, F32)
    return jnp.concatenate([x, pad], axis=0).T


def _route_kernel(aff_ref, slot_ref, w_ref, slot_t_ref, first_t_ref, first_ref, *, capacity, n_e):
    aff = aff_ref[...]
    cap = float(capacity)
    thr_bits = jnp.zeros((aff.shape[0], 1), jnp.int32)
    for bit in range(30, -1, -1):
        cand = thr_bits | (1 << bit)
        cnt = jnp.sum(jnp.where(aff >= pltpu.bitcast(cand, F32), 1.0, 0.0), axis=-1, keepdims=True)
        thr_bits = jnp.where(cnt >= cap, cand, thr_bits)
    thr = pltpu.bitcast(thr_bits, F32)
    upper = jnp.where(_iota2((128, 128), 0) <= _iota2((128, 128), 1), 1.0, 0.0).astype(BF16)
    above = jnp.where(aff > thr, 1.0, 0.0)
    tied = jnp.where(aff == thr, 1.0, 0.0)
    need = cap - jnp.sum(above, axis=-1, keepdims=True)
    sel = above + tied * jnp.where(_cumsum_lanes(tied, upper)[0] <= need, 1.0, 0.0)
    count, before = _cumsum_lanes(sel, upper)
    slot = jnp.where(sel > 0.0, count - 1.0, -1.0)
    slot_ref[...] = slot
    w_ref[...] = jnp.where(sel > 0.0, aff, 0.0)

    n_blocks = len(before)
    lane = _iota2((1, 128), 1)
    first = jnp.zeros((aff.shape[0], 128), F32)
    for j in range(n_blocks):
        first = jnp.where(lane == j, before[j], first)
    first_ref[...] = first
    for smp in range(aff.shape[0] // n_e):
        rows = slice(smp * n_e, (smp + 1) * n_e)
        for j in range(n_blocks):
            slot_t_ref[smp, j * 128:(j + 1) * 128, :] = _rows_to_lanes(
                slot[rows, j * 128:(j + 1) * 128], -1.0)
        first_t_ref[smp] = _rows_to_lanes(first[rows, :], 0.0)[0:n_blocks, :]


def _route(aff_t, capacity):
    b, e, n = aff_t.shape
    assert n // 128 <= 128 and e <= 128 and e % 8 == 0
    rows = pl.BlockSpec((b * e, n), lambda i: (0, 0))
    slot, w, slot_t, first_t, first = pl.pallas_call(
        functools.partial(_route_kernel, capacity=capacity, n_e=e),
        grid=(1,),
        in_specs=[rows],
        out_specs=[rows, rows,
                   pl.BlockSpec((b, n, 128), lambda i: (0, 0, 0)),
                   pl.BlockSpec((b, n // 128, 128), lambda i: (0, 0, 0)),
                   pl.BlockSpec((b * e, 128), lambda i: (0, 0))],
        out_shape=[jax.ShapeDtypeStruct((b * e, n), F32)] * 2
        + [jax.ShapeDtypeStruct((b, n, 128), F32), jax.ShapeDtypeStruct((b, n // 128, 128), F32),
           jax.ShapeDtypeStruct((b * e, 128), F32)],
        compiler_params=_params("arbitrary"),
        name="route",
    )(aff_t.reshape(b * e, n))
    return slot.reshape(b, e, n), w.reshape(b, e, n), slot_t, first_t, first.reshape(b, e, 128)


GATHER_TILE = 256
GATHER_ROWS = 128
GATHER_GROUP = 4


def _gather_kernel(before_ref, fits_ref, h2_ref, slot_ref, w_ref, xe_ref, gate_ref, xe_s, gate_s):
    b = pl.program_id(0)
    g = pl.program_id(1)
    n_g, cap = xe_ref.shape[1], xe_ref.shape[2]
    n = h2_ref.shape[1]
    n_tiles = n // GATHER_TILE

    @pl.when(fits_ref[b, g] != 0)
    def _():
        xe_s[:, 0:16, :] = jnp.zeros((n_g, 16, D_MODEL), BF16)
        gate_s[...] = jnp.zeros_like(gate_s)
        row_id = _iota2((GATHER_ROWS, 1), 0).astype(F32)
        for j in range(n_tiles):
            starts, onehots = [], []
            for e in range(n_g):
                start = pl.multiple_of((before_ref[b, g * n_g + e, j] // 16) * 16, 16)
                hit = (slot_ref[0, e, j:j + 1, :] - start.astype(F32)) == row_id
                onehots.append(jnp.where(hit, 1.0, 0.0).astype(BF16))
                gate_s[e, pl.ds(start, GATHER_ROWS), :] += jnp.sum(
                    jnp.where(hit, w_ref[0, e, j:j + 1, :], 0.0), axis=1, keepdims=True)
                starts.append(start)
            rows = _dot(jnp.concatenate(onehots, axis=0),
                        h2_ref[0, j * GATHER_TILE:(j + 1) * GATHER_TILE, :]).astype(BF16)
            for e, start in enumerate(starts):
                r0 = e * GATHER_ROWS
                xe_s[e, pl.ds(start, 16), :] += rows[r0:r0 + 16]
                xe_s[e, pl.ds(start + 16, GATHER_ROWS - 16), :] = rows[r0 + 16:r0 + GATHER_ROWS]
        xe_ref[0] = xe_s[:, 0:cap, :]
        gate_ref[0] = gate_s[:, 0:cap, :]

    @pl.when(fits_ref[b, g] == 0)
    def _():
        slot_id = _iota2((cap, 1), 0).astype(F32)
        for e in range(n_g):
            xe = jnp.zeros((cap, D_MODEL), F32)
            gate = jnp.zeros((cap, 1), F32)
            for j in range(n_tiles):
                hit = slot_ref[0, e, j:j + 1, :] == slot_id
                xe = xe + _dot(jnp.where(hit, 1.0, 0.0).astype(BF16),
                               h2_ref[0, j * GATHER_TILE:(j + 1) * GATHER_TILE, :])
                gate = gate + jnp.sum(jnp.where(hit, w_ref[0, e, j:j + 1, :], 0.0), axis=1,
                                      keepdims=True)
            xe_ref[0, e] = xe.astype(BF16)
            gate_ref[0, e] = gate


def _gather(h2, slot, w, first, capacity):
    b, n, _ = h2.shape
    n_e = slot.shape[1]
    n_tiles = n // GATHER_TILE
    assert n_e % GATHER_GROUP == 0 and GATHER_TILE % 128 == 0 and capacity % 16 == 0
    before = first[:, :, 0:n // 128:GATHER_TILE // 128].astype(jnp.int32)
    after = jnp.concatenate([before[:, :, 1:], jnp.full((b, n_e, 1), capacity, jnp.int32)], axis=2)
    fits = jnp.all((after - (before // 16) * 16) < GATHER_ROWS, axis=2)
    fits = jnp.all(fits.reshape(b, n_e // GATHER_GROUP, GATHER_GROUP), axis=2).astype(jnp.int32)
    tiles = lambda a: a.reshape(b, n_e, n_tiles, GATHER_TILE)
    rows = pl.BlockSpec((1, GATHER_GROUP, n_tiles, GATHER_TILE), lambda i, g, *_: (i, g, 0, 0))
    return pl.pallas_call(
        _gather_kernel,
        grid_spec=pltpu.PrefetchScalarGridSpec(
            num_scalar_prefetch=2,
            grid=(b, n_e // GATHER_GROUP),
            in_specs=[pl.BlockSpec((1, n, D_MODEL), lambda i, g, *_: (i, 0, 0)), rows, rows],
            out_specs=[pl.BlockSpec((1, GATHER_GROUP, capacity, D_MODEL), lambda i, g, *_: (i, g, 0, 0)),
                       pl.BlockSpec((1, GATHER_GROUP, capacity, 1), lambda i, g, *_: (i, g, 0, 0))],
            scratch_shapes=[pltpu.VMEM((GATHER_GROUP, capacity + GATHER_ROWS, D_MODEL), BF16),
                            pltpu.VMEM((GATHER_GROUP, capacity + GATHER_ROWS, 1), F32)]),
        out_shape=[jax.ShapeDtypeStruct((b, n_e, capacity, D_MODEL), BF16),
                   jax.ShapeDtypeStruct((b, n_e, capacity, 1), F32)],
        compiler_params=_params("arbitrary", "arbitrary"),
        name="gather",
    )(before, fits, h2, tiles(slot), tiles(w))


def _ffn_kernel(xe_ref, gate_ref, wg_ref, wu_ref, wd_ref, ye_ref, acc_s):
    ft = pl.program_id(1)
    last = pl.num_programs(1) - 1

    def step(first, final):
        wg = wg_ref[0].astype(BF16)
        wu = wu_ref[0].astype(BF16)
        wd = wd_ref[0].astype(BF16)
        for i in range(xe_ref.shape[0]):
            xe = xe_ref[i, 0]
            act = jax.nn.silu(_dot(xe, wg)) * _dot(xe, wu)
            part = _dot(act.astype(BF16), wd)
            total = part if first else acc_s[i] + part
            if final:
                ye_ref[i] = (total * gate_ref[i, 0]).astype(BF16)
            else:
                acc_s[i] = total

    pl.when(ft == 0)(lambda: step(True, False))
    pl.when(jnp.logical_and(ft > 0, ft < last))(lambda: step(False, False))
    pl.when(ft == last)(lambda: step(False, True))


def _ffn(xe, gate, wg, wu, wd, f_tile):
    b, n_e, cap, _ = xe.shape
    assert D_EXPERT // f_tile >= 2
    return pl.pallas_call(
        _ffn_kernel,
        grid=(n_e, D_EXPERT // f_tile),
        in_specs=[pl.BlockSpec((b, 1, cap, D_MODEL), lambda e, f: (0, e, 0, 0)),
                  pl.BlockSpec((b, 1, cap, 1), lambda e, f: (0, e, 0, 0)),
                  pl.BlockSpec((1, D_MODEL, f_tile), lambda e, f: (e, 0, f)),
                  pl.BlockSpec((1, D_MODEL, f_tile), lambda e, f: (e, 0, f)),
                  pl.BlockSpec((1, f_tile, D_MODEL), lambda e, f: (e, f, 0))],
        out_specs=pl.BlockSpec((b, cap, D_MODEL), lambda e, f: (0, e, 0)),
        out_shape=jax.ShapeDtypeStruct((b, n_e * cap, D_MODEL), BF16),
        scratch_shapes=[pltpu.VMEM((b, cap, D_MODEL), F32)],
        compiler_params=_params("arbitrary", "arbitrary"),
        name="ffn",
    )(xe, gate, wg, wu, wd)


COMBINE_WIDE = CHUNK + 16
COMBINE_NARROW = 48


def _combine_kernel(first_ref, narrow_ref, slot_t_ref, first_t_ref, ye_ref, x1_ref, mod_ref, fg_ref,
                    o_ref, acc_s, *, capacity):
    b = pl.program_id(0)
    j = pl.program_id(1)
    n_sub = o_ref.shape[1] // 128
    lane = _iota2((1, 128), 1).astype(F32)

    def scatter(window):
        k_total = N_EXPERTS * window
        for sb in range(n_sub):
            blk = j * n_sub + sb
            slot_t = slot_t_ref[0, sb * 128:(sb + 1) * 128, :]
            start_row = jnp.minimum(jnp.floor(first_t_ref[0, pl.ds(blk, 1), :] * (1.0 / 16.0)) * 16.0,
                                    float(capacity - window))
            k_pos = jnp.where(slot_t >= 0.0, slot_t - start_row + lane * float(window), -1.0)
            cols = []
            for c in range(k_total // 128):
                k_lane = lane + float(128 * c)
                hit = jnp.zeros((128, 128), F32)
                for e in range((128 * c) // window, (128 * c + 127) // window + 1):
                    hit = jnp.where(k_pos[:, e:e + 1] == k_lane, 1.0, hit)
                cols.append(hit.astype(BF16))
            onehot = jnp.concatenate(cols, axis=1)
            rows = []
            for e in range(N_EXPERTS):
                start = jnp.minimum((first_ref[b, blk, e] // 16) * 16, capacity - window)
                rows.append(ye_ref[0, pl.ds(pl.multiple_of(e * capacity + start, 16), window), :])
            acc_s[sb * 128:(sb + 1) * 128, :] = _dot(onehot, jnp.concatenate(rows, axis=0))

    pl.when(narrow_ref[b, j] != 0)(lambda: scatter(COMBINE_NARROW))
    pl.when(narrow_ref[b, j] == 0)(lambda: scatter(COMBINE_WIDE))

    g2 = mod_ref[pl.ds(b, 1), 5 * D_MODEL:6 * D_MODEL]
    x2 = x1_ref[0] + g2 * acc_s[...]
    o_ref[0] = x2 * lax.rsqrt(jnp.mean(x2 * x2, axis=-1, keepdims=True) + EPS) * fg_ref[...]


def _combine(slot_t, first_t, ye, x1, mod, fg, capacity, tm):
    b, n, _ = x1.shape
    n_blocks = n // 128
    for window in (COMBINE_WIDE, COMBINE_NARROW):
        assert (N_EXPERTS * window) % 128 == 0 and window % 16 == 0 and window <= capacity
    assert COMBINE_WIDE >= 128 + 15 and capacity % 16 == 0
    first = first_t[:, :, :N_EXPERTS].astype(jnp.int32)
    after = jnp.concatenate([first[:, 1:], jnp.full((b, 1, N_EXPERTS), capacity, jnp.int32)], axis=1)
    narrow = jnp.all(after - (first // 16) * 16 <= COMBINE_NARROW, axis=2)
    narrow = jnp.all(narrow.reshape(b, n // tm, tm // 128), axis=2).astype(jnp.int32)
    tok = pl.BlockSpec((1, tm, D_MODEL), lambda i, j, *_: (i, j, 0))
    return pl.pallas_call(
        functools.partial(_combine_kernel, capacity=capacity),
        grid_spec=pltpu.PrefetchScalarGridSpec(
            num_scalar_prefetch=2,
            grid=(b, n // tm),
            in_specs=[pl.BlockSpec((1, tm, 128), lambda i, j, *_: (i, j, 0)),
                      pl.BlockSpec((1, n_blocks, 128), lambda i, j, *_: (i, 0, 0)),
                      pl.BlockSpec((1, N_EXPERTS * capacity, D_MODEL), lambda i, j, *_: (i, 0, 0)),
                      tok,
                      pl.BlockSpec(mod.shape, lambda i, j, *_: (0, 0)),
                      pl.BlockSpec((1, D_MODEL), lambda i, j, *_: (0, 0))],
            out_specs=tok,
            scratch_shapes=[pltpu.VMEM((tm, D_MODEL), F32)]),
        out_shape=jax.ShapeDtypeStruct((b, n, D_MODEL), F32),
        compiler_params=_params("arbitrary", "arbitrary"),
        name="combine",
    )(first, narrow, slot_t, first_t, ye, x1, mod, fg)


def kernel(x, c, ctx, c_ctx, w_mod, b_mod, norm_mix_g, w_in, conv_q, conv_k, b_igate, b_fgate,
           gmlp_ws, gmlp_bs, mlstm_norm_g, w_out, norm_ffn_g, w_router, w_gate_e, w_up_e,
           w_down_e, final_g):
    depth = w_mod.shape[0]
    assert depth == 1, "the context stream is only carried as mLSTM states (single layer)"
    batch, seq, _ = x.shape
    assert seq % GRID_W == 0 and seq % CHUNK == 0 and batch + 1 <= MOD_ROWS
    capacity = EC_FACTOR * seq // N_EXPERTS
    ctx_row = batch
    l = 0

    cond = jnp.concatenate([c, c_ctx[None], jnp.zeros((MOD_ROWS - batch - 1, D_MODEL), F32)], axis=0)
    mod = _adaln(cond, w_mod[l], b_mod[l][None])

    row = lambda a: a[None]
    w_bf = w_in[l].astype(BF16)
    w_gates = jnp.pad(w_bf[:, MAIN_W:], ((0, 0), (0, GATE_PAD - N_GATES)))
    cols = lambda *blks: jnp.concatenate(
        [w_bf[:, k * GROUP_W:(k + 1) * GROUP_W] for k in blks] + [w_gates], axis=1)
    gate_bias = jnp.pad(jnp.concatenate([b_igate[l].reshape(-1), b_fgate[l].reshape(-1)]),
                        (0, GATE_PAD - N_GATES))[None]
    k_scale = HEAD_DIM ** -0.5

    k_c, v_c, gates_c = _inproj(ctx, mod, row(norm_mix_g[l]), cols(K_BLK, V_BLK), conv_k[l][None],
                                (k_scale,), tm=ctx.shape[1], ctx_row=ctx_row)
    state = _mlstm(None, k_c, v_c, 0, gates_c, gate_bias, None)

    q_l, k_l, p, gates = _inproj(x, mod, row(norm_mix_g[l]),
                                 cols(Q_BLK, K_BLK, U_BLK, VG_BLK, O_BLK, V_BLK),
                                 jnp.stack([conv_q[l], conv_k[l]]), (1.0, k_scale), tm=512, ctx_row=None)
    h_f, h_b = _mlstm(q_l, k_l, p, P_V, gates, gate_bias, state)

    x1, h2, aff_t = _postmix(p, h_f, h_b, x, mod, gmlp_ws[l].astype(BF16), gmlp_bs[l][:, :, None],
                             row(mlstm_norm_g[l]), w_out[l].astype(BF16), row(norm_ffn_g[l]),
                             jnp.pad(w_router[l], ((0, 0), (0, 128 - N_EXPERTS))), tm=512)

    slot, gate_w, slot_t, first_t, first = _route(aff_t, capacity)

    xe, gate = _gather(h2, slot, gate_w, first, capacity)
    ye = _ffn(xe, gate, w_gate_e[l], w_up_e[l], w_down_e[l], f_tile=512)
    return _combine(slot_t, first_t, ye, x1, mod, row(final_g), capacity, tm=512)
```

```python
import functools

import jax
import jax.numpy as jnp
from jax import lax
from jax.experimental import pallas as pl
from jax.experimental.pallas import tpu as pltpu

F32 = jnp.float32
BF16 = jnp.bfloat16

D_MODEL = 1024
GRID_W = 64
CHUNK = 128
HEADS = 4
GROUP_W = D_MODEL // 2
HEAD_DIM = GROUP_W // HEADS
N_EXPERTS = 16
EC_FACTOR = 2
D_EXPERT = 2 * D_MODEL
EPS = 1e-6
N_GATES = 4 * HEADS
GATE_PAD = 128
MOD_ROWS = 8

U_BLK, VG_BLK, Q_BLK, O_BLK, K_BLK, V_BLK = 0, 1, 2, 3, 4, 5
MAIN_W = 6 * GROUP_W
P_U, P_VG, P_O, P_V = 0, 1, 2, 3

VMEM_LIMIT = 56 * 1024 * 1024
ROW_GROUP = 2 * CHUNK
MLSTM_SAMPLES = 4

def _params(*sem):
    return pltpu.CompilerParams(dimension_semantics=sem, vmem_limit_bytes=VMEM_LIMIT)


def _dot(a, b):
    return jnp.dot(a, b, preferred_element_type=F32)


def _dot_nt(a, b):
    return lax.dot_general(a, b, (((1,), (1,)), ((), ())), preferred_element_type=F32)


def _split2(a):
    hi = a.astype(BF16)
    lo = (a - hi.astype(F32)).astype(BF16)
    return hi, lo


def _dot3(a, b):
    ah, al = _split2(a)
    bh, bl = _split2(b)
    return _dot(ah, bh) + (_dot(al, bh) + _dot(ah, bl))


def _dot_exact01(tri, x):
    x1 = x.astype(BF16)
    r1 = x - x1.astype(F32)
    x2 = r1.astype(BF16)
    x3 = (r1 - x2.astype(F32)).astype(BF16)
    return _dot(tri, x1) + (_dot(tri, x2) + _dot(tri, x3))


def _iota2(shape, dim):
    return lax.broadcasted_iota(jnp.int32, shape, dim)


def _row_to_col(row):
    n = row.shape[1] // 128
    eye = _iota2((128, 128), 0) == _iota2((128, 128), 1)
    cols = [jnp.sum(jnp.where(eye, row[:, j * 128:(j + 1) * 128], 0.0), axis=1, keepdims=True)
            for j in range(n)]
    return cols[0] if n == 1 else jnp.concatenate(cols, axis=0)


def _adaln_kernel(cond_ref, w_ref, b_ref, o_ref):
    o_ref[...] = _dot3(jax.nn.silu(cond_ref[...]), w_ref[...]) + b_ref[...]


def _adaln(cond, w, b):
    n_out = w.shape[1]
    tn = 2 * D_MODEL
    assert n_out % tn == 0
    return pl.pallas_call(
        _adaln_kernel,
        grid=(n_out // tn,),
        in_specs=[pl.BlockSpec((MOD_ROWS, D_MODEL), lambda j: (0, 0)),
                  pl.BlockSpec((D_MODEL, tn), lambda j: (0, j)),
                  pl.BlockSpec((1, tn), lambda j: (0, j))],
        out_specs=pl.BlockSpec((MOD_ROWS, tn), lambda j: (0, j)),
        out_shape=jax.ShapeDtypeStruct((MOD_ROWS, n_out), F32),
        compiler_params=_params("arbitrary"),
        name="adaln",
    )(cond, w, b)


def _inproj_kernel(x_ref, xprev_ref, xnext_ref, mod_ref, g_ref, taps_ref, w_ref, *outs, ctx_row, scales):
    conv_refs, (p_ref, gate_ref) = outs[:len(scales)], outs[len(scales):]
    row = pl.program_id(0) if ctx_row is None else ctx_row
    j = pl.program_id(1)
    sh = mod_ref[pl.ds(row, 1), 0:D_MODEL]
    sc = mod_ref[pl.ds(row, 1), D_MODEL:2 * D_MODEL]

    def modulated(x):
        y = x * lax.rsqrt(jnp.mean(x * x, axis=-1, keepdims=True) + EPS) * g_ref[...]
        return y * (1.0 + sc) + sh

    n_groups = x_ref.shape[1] // ROW_GROUP
    before = modulated(xprev_ref[0]) * (j > 0).astype(F32)
    for r in range(n_groups):
        rows = slice(r * ROW_GROUP, (r + 1) * ROW_GROUP)
        h = modulated(x_ref[0, rows, :])
        if r == n_groups - 1:
            after = modulated(xnext_ref[0]) * (j < pl.num_programs(1) - 1).astype(F32)
        else:
            after = modulated(x_ref[0, (r + 1) * ROW_GROUP:(r + 1) * ROW_GROUP + 8, :])
        ext = jnp.concatenate([before, h, after], axis=0).astype(BF16)
        before = h[ROW_GROUP - 8:, :]
        y = _dot(ext, w_ref[...])
        n_ext, n_conv = ext.shape[0], len(scales) * GROUP_W
        p_ref[0, rows, :] = y[8:8 + ROW_GROUP, n_conv:n_conv + p_ref.shape[2]]
        gate_ref[0, rows, :] = y[8:8 + ROW_GROUP, n_conv + p_ref.shape[2]:]
        y_before = pltpu.roll(y, 1, axis=0)[8:8 + ROW_GROUP]
        y_after = pltpu.roll(y, n_ext - 1, axis=0)[8:8 + ROW_GROUP]
        y_here = y[8:8 + ROW_GROUP]
        for s, scale in enumerate(scales):
            lanes = slice(s * GROUP_W, (s + 1) * GROUP_W)
            z = (y_before[:, lanes] * taps_ref[s, 0:1, :] + y_here[:, lanes] * taps_ref[s, 1:2, :]
                 + y_after[:, lanes] * taps_ref[s, 2:3, :])
            conv_refs[s][0, rows, :] = (jax.nn.silu(z) * scale).astype(BF16)


def _inproj(x, mod, g, w, taps, scales, tm, ctx_row):
    b, n, _ = x.shape
    wn = w.shape[1] - len(scales) * GROUP_W - GATE_PAD
    assert tm % ROW_GROUP == 0 and n % tm == 0 and wn % 128 == 0
    rows8 = tm // 8
    last8 = n // 8 - 1
    full = lambda a: pl.BlockSpec(a.shape, lambda i, j: (0,) * a.ndim)
    conv_spec = pl.BlockSpec((1, tm, GROUP_W), lambda i, j: (i, j, 0))
    return pl.pallas_call(
        functools.partial(_inproj_kernel, ctx_row=ctx_row, scales=tuple(scales)),
        grid=(b, n // tm),
        in_specs=[pl.BlockSpec((1, tm, D_MODEL), lambda i, j: (i, j, 0)),
                  pl.BlockSpec((1, 8, D_MODEL), lambda i, j: (i, jnp.maximum(j * rows8 - 1, 0), 0)),
                  pl.BlockSpec((1, 8, D_MODEL), lambda i, j: (i, jnp.minimum((j + 1) * rows8, last8), 0)),
                  full(mod), full(g), full(taps), full(w)],
        out_specs=[conv_spec] * len(scales)
        + [pl.BlockSpec((1, tm, wn), lambda i, j: (i, j, 0)),
           pl.BlockSpec((1, tm, GATE_PAD), lambda i, j: (i, j, 0))],
        out_shape=[jax.ShapeDtypeStruct((b, n, GROUP_W), BF16)] * len(scales)
        + [jax.ShapeDtypeStruct((b, n, wn), F32), jax.ShapeDtypeStruct((b, n, GATE_PAD), F32)],
        compiler_params=_params("arbitrary", "arbitrary"),
        name="inproj",
    )(x, x, x, mod, g, taps, w)


def _mlstm_chunk(per_dir, bias_ref, c_s, n_s, m_s, h_refs):
    with_h = h_refs[0] is not None
    i0 = _iota2((CHUNK, CHUNK), 0)
    i1 = _iota2((CHUNK, CHUNK), 1)

    units = []
    for smp, d in [(smp, d) for smp in range(c_s.shape[0]) for d in range(2)]:
        q_ref, k_ref, v_ref, g_ref = per_dir[d]
        sees_ts = (i1 <= i0) if d == 0 else (i1 >= i0)
        sees_st = (i0 <= i1) if d == 0 else (i0 >= i1)
        tri = jnp.where(sees_ts, 1.0, 0.0).astype(BF16)
        gates = g_ref[smp] + bias_ref[...]
        bcum = _dot_exact01(tri, jax.nn.log_sigmoid(gates))
        for hd in range(HEADS):
            lanes = slice(hd * HEAD_DIM, (hd + 1) * HEAD_DIM)
            c_prev = c_s[smp, d, hd]
            n_prev = n_s[smp, d, hd]
            k = k_ref[smp, :, lanes]
            st = dict(smp=smp, d=d, hd=hd, lanes=lanes, c_prev=c_prev, n_prev=n_prev, k=k,
                      sees_st=sees_st, gates=gates, bcum=bcum, v=v_ref[smp, :, lanes])
            if with_h:
                lhs = jnp.concatenate([k, c_prev.astype(BF16),
                                       jnp.broadcast_to(n_prev, (16, HEAD_DIM)).astype(BF16)], axis=0)
                st["prod"] = _dot_nt(lhs, q_ref[smp, :, lanes])
            units.append(st)

    rows = {}
    for st in units:
        smp, d, hd = st["smp"], st["d"], st["hd"]
        if (smp, d) not in rows:
            rows[smp, d] = (st["gates"].T, st["bcum"].T)
        gates_t, bcum_t = rows[smp, d]
        ci = d * HEADS + hd
        cf = 2 * HEADS + d * HEADS + hd
        last = CHUNK - 1 if d == 0 else 0
        li_row = gates_t[ci:ci + 1, :]
        bc_row = bcum_t[cf:cf + 1, :]
        b_last = bc_row[:, last:last + 1]
        m_prev = m_s[smp, d, hd]
        v_t = st["v"].T
        a_row = b_last - bc_row + li_row
        m_new = jnp.maximum(b_last + m_prev, jnp.max(a_row, axis=-1, keepdims=True))
        w_row = jnp.exp(a_row - m_new)
        lhs = jnp.concatenate([v_t * w_row, jnp.broadcast_to(w_row, (16, CHUNK))], axis=0)
        st["upd"] = _dot(lhs.astype(BF16), st["k"])
        st.update(bc_row=bc_row, m_prev=m_prev, m_new=m_new, v_t=v_t,
                  decay=jnp.exp(b_last + m_prev - m_new),
                  u_col=st["gates"][:, ci:ci + 1] - st["bcum"][:, cf:cf + 1])

    if with_h:
        for st in units:
            prod, bc_row = st["prod"], st["bc_row"]
            g = bc_row + st["m_prev"]
            dmat = jnp.where(st["sees_st"], st["u_col"] + bc_row, -jnp.inf)
            m_t = jnp.maximum(g, jnp.max(dmat, axis=0, keepdims=True))
            inter = jnp.exp(g - m_t)
            s = prod[0:CHUNK] * jnp.exp(dmat - m_t)
            st["pv"] = _dot(st["v_t"].astype(BF16), s.astype(BF16))
            st["num0"] = inter * prod[CHUNK:2 * CHUNK]
            den = inter * prod[2 * CHUNK:2 * CHUNK + 1] + jnp.sum(s, axis=0, keepdims=True)
            st["scale"] = 1.0 / jnp.maximum(jnp.abs(den), jnp.exp(-m_t))

    for st in units:
        smp, d, hd = st["smp"], st["d"], st["hd"]
        if with_h:
            h_refs[d][smp, :, st["lanes"]] = ((st["num0"] + st["pv"]) * st["scale"]).T
        c_s[smp, d, hd] = st["decay"] * st["c_prev"] + st["upd"][0:HEAD_DIM]
        n_s[smp, d, hd] = st["decay"] * st["n_prev"] + st["upd"][HEAD_DIM:HEAD_DIM + 1]
        m_s[smp, d, hd] = st["m_new"]


def _mlstm_kernel(*refs, n_chunks, with_h):
    refs = list(refs)
    take = lambda n: [refs.pop(0) for _ in range(n)]
    per_dir = []
    for _ in range(2):
        q_ref = take(1)[0] if with_h else None
        k_ref, v_ref, g_ref = take(3)
        per_dir.append((q_ref, k_ref, v_ref, g_ref))
    bias_ref = take(1)[0]
    if with_h:
        c0_ref, n0_ref, m0_ref = take(3)
        h_refs = take(2)
        c_out = n_out = m_out = None
    else:
        h_refs = [None, None]
        c_out, n_out, m_out = take(3)
    c_s, n_s, m_s = take(3)

    j = pl.program_id(1)

    @pl.when(j == 0)
    def _():
        if with_h:
            c_s[...] = c0_ref[...]
            n_s[...] = n0_ref[...]
            m_s[...] = m0_ref[...]
        else:
            c_s[...] = jnp.zeros_like(c_s)
            n_s[...] = jnp.zeros_like(n_s)
            m_s[...] = jnp.zeros_like(m_s)

    _mlstm_chunk(per_dir, bias_ref, c_s, n_s, m_s, h_refs)

    if not with_h:
        @pl.when(j == n_chunks - 1)
        def _():
            c_out[...] = c_s[...]
            n_out[...] = n_s[...]
            m_out[...] = m_s[...]


def _mlstm(q, k, p, v_blk, gates, bias, state):
    b, n, _ = k.shape
    n_chunks = n // CHUNK
    with_h = q is not None
    ns = MLSTM_SAMPLES if b % MLSTM_SAMPLES == 0 else 1

    in_specs, args = [], []
    for d in range(2):
        c = (lambda j: j) if d == 0 else (lambda j: n_chunks - 1 - j)
        tok = pl.BlockSpec((ns, CHUNK, GROUP_W), lambda i, j, c=c: (i, c(j), 0))
        if with_h:
            in_specs.append(tok)
            args.append(q)
        in_specs += [tok,
                     pl.BlockSpec((ns, CHUNK, GROUP_W), lambda i, j, c=c: (i, c(j), v_blk)),
                     pl.BlockSpec((ns, CHUNK, GATE_PAD), lambda i, j, c=c: (i, c(j), 0))]
        args += [k, p, gates]
    in_specs.append(pl.BlockSpec((1, GATE_PAD), lambda i, j: (0, 0)))
    args.append(bias)

    c_shape = (2, HEADS, HEAD_DIM, HEAD_DIM)
    v_shape = (2, HEADS, 1, HEAD_DIM)
    c_spec = pl.BlockSpec((ns,) + c_shape, lambda i, j: (i, 0, 0, 0, 0))
    v_spec = pl.BlockSpec((ns,) + v_shape, lambda i, j: (i, 0, 0, 0, 0))
    if with_h:
        in_specs += [c_spec, v_spec, v_spec]
        args += list(state)
        out_specs = [pl.BlockSpec((ns, CHUNK, GROUP_W), lambda i, j: (i, j, 0)),
                     pl.BlockSpec((ns, CHUNK, GROUP_W), lambda i, j: (i, n_chunks - 1 - j, 0))]
        out_shape = [jax.ShapeDtypeStruct((b, n, GROUP_W), F32)] * 2
    else:
        out_specs = [c_spec, v_spec, v_spec]
        out_shape = [jax.ShapeDtypeStruct((b,) + c_shape, F32),
                     jax.ShapeDtypeStruct((b,) + v_shape, F32),
                     jax.ShapeDtypeStruct((b,) + v_shape, F32)]
    return pl.pallas_call(
        functools.partial(_mlstm_kernel, n_chunks=n_chunks, with_h=with_h),
        grid=(b // ns, n_chunks),
        in_specs=in_specs,
        out_specs=out_specs,
        out_shape=out_shape,
        scratch_shapes=[pltpu.VMEM((ns,) + c_shape, F32), pltpu.VMEM((ns,) + v_shape, F32),
                        pltpu.VMEM((ns,) + v_shape, F32)],
        compiler_params=_params("arbitrary", "arbitrary"),
        name="mlstm" if with_h else "mlstm_ctx_state",
    )(*args)


def _layer_norm(x):
    mu = jnp.mean(x, axis=-1, keepdims=True)
    var = jnp.mean(jnp.square(x - mu), axis=-1, keepdims=True)
    return (x - mu) * lax.rsqrt(var + EPS)


def _postmix_kernel(u_ref, vg_ref, o_ref, hf_ref, hb_ref, x_ref, mod_ref, ws_ref, bs_ref, ng_ref,
                    wout_ref, fg_ref, wr_ref, x1_ref, h2_ref, aff_ref, ycat_s):
    tm = x_ref.shape[1]
    b = pl.program_id(0)
    mod = lambda k: mod_ref[pl.ds(b, 1), k * D_MODEL:(k + 1) * D_MODEL]
    wr_hi, wr_lo = _split2(wr_ref[...])
    expert_lane = _iota2((ROW_GROUP, 128), 1) < N_EXPERTS

    for r in range(tm // ROW_GROUP):
        rows = slice(r * ROW_GROUP, (r + 1) * ROW_GROUP)

        for c in range(r * ROW_GROUP // CHUNK, (r + 1) * ROW_GROUP // CHUNK):
            crows = slice(c * CHUNK, (c + 1) * CHUNK)
            u = jax.nn.gelu(u_ref[0, crows, :])
            v = _layer_norm(jax.nn.gelu(vg_ref[0, crows, :])).astype(BF16)
            for hd in range(HEADS):
                lanes = slice(hd * HEAD_DIM, (hd + 1) * HEAD_DIM)
                s = _dot(ws_ref[hd], v[:, lanes]) + bs_ref[hd]
                ycat_s[crows, lanes] = (u[:, lanes] * s).astype(BF16)

        hsum = hf_ref[0, rows, :] + hb_ref[0, rows, :]
        o = jax.nn.sigmoid(o_ref[0, rows, :])
        for hd in range(HEADS):
            lanes = slice(hd * HEAD_DIM, (hd + 1) * HEAD_DIM)
            hn = _layer_norm(hsum[:, lanes]) * ng_ref[:, lanes]
            ycat_s[rows, GROUP_W + hd * HEAD_DIM:GROUP_W + (hd + 1) * HEAD_DIM] = (
                o[:, lanes] * hn).astype(BF16)

        y = _dot(ycat_s[rows, :], wout_ref[...])
        x1 = x_ref[0, rows, :] + mod(2) * y
        x1_ref[0, rows, :] = x1

        n2 = x1 * lax.rsqrt(jnp.mean(x1 * x1, axis=-1, keepdims=True) + EPS) * fg_ref[...]
        h2 = n2 * (1.0 + mod(4)) + mod(3)
        h2_hi, h2_lo = _split2(h2)
        h2_ref[0, rows, :] = h2_hi

        logits = _dot(h2_hi, wr_hi) + (_dot(h2_lo, wr_hi) + _dot(h2_hi, wr_lo))
        logits = jnp.where(expert_lane, logits, -jnp.inf)
        e = jnp.exp(logits - jnp.max(logits, axis=-1, keepdims=True))
        aff = e / jnp.sum(e, axis=-1, keepdims=True)
        aff_ref[0, :, rows] = aff.T[0:N_EXPERTS, :]


def _postmix(p, hf, hb, x, mod, ws, bs, ng, wout, fg, wr, tm):
    b, n, _ = x.shape
    tok = lambda blk: pl.BlockSpec((1, tm, GROUP_W), lambda i, j: (i, j, blk))
    full = lambda a: pl.BlockSpec(a.shape, lambda i, j: (0,) * a.ndim)
    return pl.pallas_call(
        _postmix_kernel,
        grid=(b, n // tm),
        in_specs=[tok(P_U), tok(P_VG), tok(P_O), tok(0), tok(0),
                  pl.BlockSpec((1, tm, D_MODEL), lambda i, j: (i, j, 0)),
                  full(mod), full(ws), full(bs), full(ng), full(wout), full(fg), full(wr)],
        out_specs=[pl.BlockSpec((1, tm, D_MODEL), lambda i, j: (i, j, 0)),
                   pl.BlockSpec((1, tm, D_MODEL), lambda i, j: (i, j, 0)),
                   pl.BlockSpec((1, N_EXPERTS, tm), lambda i, j: (i, 0, j))],
        out_shape=[jax.ShapeDtypeStruct((b, n, D_MODEL), F32),
                   jax.ShapeDtypeStruct((b, n, D_MODEL), BF16),
                   jax.ShapeDtypeStruct((b, N_EXPERTS, n), F32)],
        scratch_shapes=[pltpu.VMEM((tm, D_MODEL), BF16)],
        compiler_params=_params("arbitrary", "arbitrary"),
        name="postmix",
    )(p, p, p, hf, hb, x, mod, ws, bs, ng, wout, fg, wr)


def _cumsum_lanes(x, upper):
    carry = jnp.zeros((x.shape[0], 1), F32)
    outs, before = [], []
    for j in range(x.shape[1] // 128):
        before.append(carry)
        c = _dot(x[:, j * 128:(j + 1) * 128].astype(BF16), upper) + carry
        outs.append(c)
        carry = c[:, 127:128]
    return jnp.concatenate(outs, axis=1), before


def _rows_to_lanes(x, fill):
    pad = jnp.full((128 - x.shape[0], 128), fill, F32)
    return jnp.concatenate([x, pad], axis=0).T


def _route_kernel(aff_ref, slot_ref, w_ref, slot_t_ref, first_t_ref, first_ref, *, capacity, n_e):
    aff = aff_ref[...]
    cap = float(capacity)
    thr_bits = jnp.zeros((aff.shape[0], 1), jnp.int32)
    for bit in range(30, -1, -1):
        cand = thr_bits | (1 << bit)
        cnt = jnp.sum(jnp.where(aff >= pltpu.bitcast(cand, F32), 1.0, 0.0), axis=-1, keepdims=True)
        thr_bits = jnp.where(cnt >= cap, cand, thr_bits)
    thr = pltpu.bitcast(thr_bits, F32)
    upper = jnp.where(_iota2((128, 128), 0) <= _iota2((128, 128), 1), 1.0, 0.0).astype(BF16)
    above = jnp.where(aff > thr, 1.0, 0.0)
    tied = jnp.where(aff == thr, 1.0, 0.0)
    need = cap - jnp.sum(above, axis=-1, keepdims=True)
    sel = above + tied * jnp.where(_cumsum_lanes(tied, upper)[0] <= need, 1.0, 0.0)
    count, before = _cumsum_lanes(sel, upper)
    slot = jnp.where(sel > 0.0, count - 1.0, -1.0)
    slot_ref[...] = slot
    w_ref[...] = jnp.where(sel > 0.0, aff, 0.0)

    n_blocks = len(before)
    lane = _iota2((1, 128), 1)
    first = jnp.zeros((aff.shape[0], 128), F32)
    for j in range(n_blocks):
        first = jnp.where(lane == j, before[j], first)
    first_ref[...] = first
    for smp in range(aff.shape[0] // n_e):
        rows = slice(smp * n_e, (smp + 1) * n_e)
        for j in range(n_blocks):
            slot_t_ref[smp, j * 128:(j + 1) * 128, :] = _rows_to_lanes(
                slot[rows, j * 128:(j + 1) * 128], -1.0)
        first_t_ref[smp] = _rows_to_lanes(first[rows, :], 0.0)[0:n_blocks, :]


def _route(aff_t, capacity):
    b, e, n = aff_t.shape
    assert n // 128 <= 128 and e <= 128 and e % 8 == 0
    rows = pl.BlockSpec((b * e, n), lambda i: (0, 0))
    slot, w, slot_t, first_t, first = pl.pallas_call(
        functools.partial(_route_kernel, capacity=capacity, n_e=e),
        grid=(1,),
        in_specs=[rows],
        out_specs=[rows, rows,
                   pl.BlockSpec((b, n, 128), lambda i: (0, 0, 0)),
                   pl.BlockSpec((b, n // 128, 128), lambda i: (0, 0, 0)),
                   pl.BlockSpec((b * e, 128), lambda i: (0, 0))],
        out_shape=[jax.ShapeDtypeStruct((b * e, n), F32)] * 2
        + [jax.ShapeDtypeStruct((b, n, 128), F32), jax.ShapeDtypeStruct((b, n // 128, 128), F32),
           jax.ShapeDtypeStruct((b * e, 128), F32)],
        compiler_params=_params("arbitrary"),
        name="route",
    )(aff_t.reshape(b * e, n))
    return slot.reshape(b, e, n), w.reshape(b, e, n), slot_t, first_t, first.reshape(b, e, 128)


GATHER_TILE = 256
GATHER_ROWS = 128
GATHER_GROUP = 4


def _gather_kernel(before_ref, fits_ref, h2_ref, slot_ref, w_ref, xe_ref, gate_ref, xe_s, gate_s):
    b = pl.program_id(0)
    g = pl.program_id(1)
    n_g, cap = xe_ref.shape[1], xe_ref.shape[2]
    n = h2_ref.shape[1]
    n_tiles = n // GATHER_TILE

    @pl.when(fits_ref[b, g] != 0)
    def _():
        xe_s[:, 0:16, :] = jnp.zeros((n_g, 16, D_MODEL), BF16)
        gate_s[...] = jnp.zeros_like(gate_s)
        row_id = _iota2((GATHER_ROWS, 1), 0).astype(F32)
        for j in range(n_tiles):
            starts, onehots = [], []
            for e in range(n_g):
                start = pl.multiple_of((before_ref[b, g * n_g + e, j] // 16) * 16, 16)
                hit = (slot_ref[0, e, j:j + 1, :] - start.astype(F32)) == row_id
                onehots.append(jnp.where(hit, 1.0, 0.0).astype(BF16))
                gate_s[e, pl.ds(start, GATHER_ROWS), :] += jnp.sum(
                    jnp.where(hit, w_ref[0, e, j:j + 1, :], 0.0), axis=1, keepdims=True)
                starts.append(start)
            rows = _dot(jnp.concatenate(onehots, axis=0),
                        h2_ref[0, j * GATHER_TILE:(j + 1) * GATHER_TILE, :]).astype(BF16)
            for e, start in enumerate(starts):
                r0 = e * GATHER_ROWS
                xe_s[e, pl.ds(start, 16), :] += rows[r0:r0 + 16]
                xe_s[e, pl.ds(start + 16, GATHER_ROWS - 16), :] = rows[r0 + 16:r0 + GATHER_ROWS]
        xe_ref[0] = xe_s[:, 0:cap, :]
        gate_ref[0] = gate_s[:, 0:cap, :]

    @pl.when(fits_ref[b, g] == 0)
    def _():
        slot_id = _iota2((cap, 1), 0).astype(F32)
        for e in range(n_g):
            xe = jnp.zeros((cap, D_MODEL), F32)
            gate = jnp.zeros((cap, 1), F32)
            for j in range(n_tiles):
                hit = slot_ref[0, e, j:j + 1, :] == slot_id
                xe = xe + _dot(jnp.where(hit, 1.0, 0.0).astype(BF16),
                               h2_ref[0, j * GATHER_TILE:(j + 1) * GATHER_TILE, :])
                gate = gate + jnp.sum(jnp.where(hit, w_ref[0, e, j:j + 1, :], 0.0), axis=1,
                                      keepdims=True)
            xe_ref[0, e] = xe.astype(BF16)
            gate_ref[0, e] = gate


def _gather(h2, slot, w, first, capacity):
    b, n, _ = h2.shape
    n_e = slot.shape[1]
    n_tiles = n // GATHER_TILE
    assert n_e % GATHER_GROUP == 0 and GATHER_TILE % 128 == 0 and capacity % 16 == 0
    before = first[:, :, 0:n // 128:GATHER_TILE // 128].astype(jnp.int32)
    after = jnp.concatenate([before[:, :, 1:], jnp.full((b, n_e, 1), capacity, jnp.int32)], axis=2)
    fits = jnp.all((after - (before // 16) * 16) < GATHER_ROWS, axis=2)
    fits = jnp.all(fits.reshape(b, n_e // GATHER_GROUP, GATHER_GROUP), axis=2).astype(jnp.int32)
    tiles = lambda a: a.reshape(b, n_e, n_tiles, GATHER_TILE)
    rows = pl.BlockSpec((1, GATHER_GROUP, n_tiles, GATHER_TILE), lambda i, g, *_: (i, g, 0, 0))
    return pl.pallas_call(
        _gather_kernel,
        grid_spec=pltpu.PrefetchScalarGridSpec(
            num_scalar_prefetch=2,
            grid=(b, n_e // GATHER_GROUP),
            in_specs=[pl.BlockSpec((1, n, D_MODEL), lambda i, g, *_: (i, 0, 0)), rows, rows],
            out_specs=[pl.BlockSpec((1, GATHER_GROUP, capacity, D_MODEL), lambda i, g, *_: (i, g, 0, 0)),
                       pl.BlockSpec((1, GATHER_GROUP, capacity, 1), lambda i, g, *_: (i, g, 0, 0))],
            scratch_shapes=[pltpu.VMEM((GATHER_GROUP, capacity + GATHER_ROWS, D_MODEL), BF16),
                            pltpu.VMEM((GATHER_GROUP, capacity + GATHER_ROWS, 1), F32)]),
        out_shape=[jax.ShapeDtypeStruct((b, n_e, capacity, D_MODEL), BF16),
                   jax.ShapeDtypeStruct((b, n_e, capacity, 1), F32)],
        compiler_params=_params("arbitrary", "arbitrary"),
        name="gather",
    )(before, fits, h2, tiles(slot), tiles(w))


def _ffn_kernel(xe_ref, gate_ref, wg_ref, wu_ref, wd_ref, ye_ref, acc_s):
    ft = pl.program_id(1)
    last = pl.num_programs(1) - 1

    def step(first, final):
        wg = wg_ref[0].astype(BF16)
        wu = wu_ref[0].astype(BF16)
        wd = wd_ref[0].astype(BF16)
        for i in range(xe_ref.shape[0]):
            xe = xe_ref[i, 0]
            act = jax.nn.silu(_dot(xe, wg)) * _dot(xe, wu)
            part = _dot(act.astype(BF16), wd)
            total = part if first else acc_s[i] + part
            if final:
                ye_ref[i] = (total * gate_ref[i, 0]).astype(BF16)
            else:
                acc_s[i] = total

    pl.when(ft == 0)(lambda: step(True, False))
    pl.when(jnp.logical_and(ft > 0, ft < last))(lambda: step(False, False))
    pl.when(ft == last)(lambda: step(False, True))


def _ffn(xe, gate, wg, wu, wd, f_tile):
    b, n_e, cap, _ = xe.shape
    assert D_EXPERT // f_tile >= 2
    return pl.pallas_call(
        _ffn_kernel,
        grid=(n_e, D_EXPERT // f_tile),
        in_specs=[pl.BlockSpec((b, 1, cap, D_MODEL), lambda e, f: (0, e, 0, 0)),
                  pl.BlockSpec((b, 1, cap, 1), lambda e, f: (0, e, 0, 0)),
                  pl.BlockSpec((1, D_MODEL, f_tile), lambda e, f: (e, 0, f)),
                  pl.BlockSpec((1, D_MODEL, f_tile), lambda e, f: (e, 0, f)),
                  pl.BlockSpec((1, f_tile, D_MODEL), lambda e, f: (e, f, 0))],
        out_specs=pl.BlockSpec((b, cap, D_MODEL), lambda e, f: (0, e, 0)),
        out_shape=jax.ShapeDtypeStruct((b, n_e * cap, D_MODEL), BF16),
        scratch_shapes=[pltpu.VMEM((b, cap, D_MODEL), F32)],
        compiler_params=_params("arbitrary", "arbitrary"),
        name="ffn",
    )(xe, gate, wg, wu, wd)


COMBINE_WIDE = CHUNK + 16
COMBINE_NARROW = 48


def _combine_kernel(first_ref, narrow_ref, slot_t_ref, first_t_ref, ye_ref, x1_ref, mod_ref, fg_ref,
                    o_ref, acc_s, *, capacity):
    b = pl.program_id(0)
    j = pl.program_id(1)
    n_sub = o_ref.shape[1] // 128
    lane = _iota2((1, 128), 1).astype(F32)

    def scatter(window):
        k_total = N_EXPERTS * window
        for sb in range(n_sub):
            blk = j * n_sub + sb
            slot_t = slot_t_ref[0, sb * 128:(sb + 1) * 128, :]
            start_row = jnp.minimum(jnp.floor(first_t_ref[0, pl.ds(blk, 1), :] * (1.0 / 16.0)) * 16.0,
                                    float(capacity - window))
            k_pos = jnp.where(slot_t >= 0.0, slot_t - start_row + lane * float(window), -1.0)
            cols = []
            for c in range(k_total // 128):
                k_lane = lane + float(128 * c)
                hit = jnp.zeros((128, 128), F32)
                for e in range((128 * c) // window, (128 * c + 127) // window + 1):
                    hit = jnp.where(k_pos[:, e:e + 1] == k_lane, 1.0, hit)
                cols.append(hit.astype(BF16))
            onehot = jnp.concatenate(cols, axis=1)
            rows = []
            for e in range(N_EXPERTS):
                start = jnp.minimum((first_ref[b, blk, e] // 16) * 16, capacity - window)
                rows.append(ye_ref[0, pl.ds(pl.multiple_of(e * capacity + start, 16), window), :])
            acc_s[sb * 128:(sb + 1) * 128, :] = _dot(onehot, jnp.concatenate(rows, axis=0))

    pl.when(narrow_ref[b, j] != 0)(lambda: scatter(COMBINE_NARROW))
    pl.when(narrow_ref[b, j] == 0)(lambda: scatter(COMBINE_WIDE))

    g2 = mod_ref[pl.ds(b, 1), 5 * D_MODEL:6 * D_MODEL]
    x2 = x1_ref[0] + g2 * acc_s[...]
    o_ref[0] = x2 * lax.rsqrt(jnp.mean(x2 * x2, axis=-1, keepdims=True) + EPS) * fg_ref[...]


def _combine(slot_t, first_t, ye, x1, mod, fg, capacity, tm):
    b, n, _ = x1.shape
    n_blocks = n // 128
    for window in (COMBINE_WIDE, COMBINE_NARROW):
        assert (N_EXPERTS * window) % 128 == 0 and window % 16 == 0 and window <= capacity
    assert COMBINE_WIDE >= 128 + 15 and capacity % 16 == 0
    first = first_t[:, :, :N_EXPERTS].astype(jnp.int32)
    after = jnp.concatenate([first[:, 1:], jnp.full((b, 1, N_EXPERTS), capacity, jnp.int32)], axis=1)
    narrow = jnp.all(after - (first // 16) * 16 <= COMBINE_NARROW, axis=2)
    narrow = jnp.all(narrow.reshape(b, n // tm, tm // 128), axis=2).astype(jnp.int32)
    tok = pl.BlockSpec((1, tm, D_MODEL), lambda i, j, *_: (i, j, 0))
    return pl.pallas_call(
        functools.partial(_combine_kernel, capacity=capacity),
        grid_spec=pltpu.PrefetchScalarGridSpec(
            num_scalar_prefetch=2,
            grid=(b, n // tm),
            in_specs=[pl.BlockSpec((1, tm, 128), lambda i, j, *_: (i, j, 0)),
                      pl.BlockSpec((1, n_blocks, 128), lambda i, j, *_: (i, 0, 0)),
                      pl.BlockSpec((1, N_EXPERTS * capacity, D_MODEL), lambda i, j, *_: (i, 0, 0)),
                      tok,
                      pl.BlockSpec(mod.shape, lambda i, j, *_: (0, 0)),
                      pl.BlockSpec((1, D_MODEL), lambda i, j, *_: (0, 0))],
            out_specs=tok,
            scratch_shapes=[pltpu.VMEM((tm, D_MODEL), F32)]),
        out_shape=jax.ShapeDtypeStruct((b, n, D_MODEL), F32),
        compiler_params=_params("arbitrary", "arbitrary"),
        name="combine",
    )(first, narrow, slot_t, first_t, ye, x1, mod, fg)


def kernel(x, c, ctx, c_ctx, w_mod, b_mod, norm_mix_g, w_in, conv_q, conv_k, b_igate, b_fgate,
           gmlp_ws, gmlp_bs, mlstm_norm_g, w_out, norm_ffn_g, w_router, w_gate_e, w_up_e,
           w_down_e, final_g):
    depth = w_mod.shape[0]
    assert depth == 1, "the context stream is only carried as mLSTM states (single layer)"
    batch, seq, _ = x.shape
    assert seq % GRID_W == 0 and seq % CHUNK == 0 and batch + 1 <= MOD_ROWS
    capacity = EC_FACTOR * seq // N_EXPERTS
    ctx_row = batch
    l = 0

    cond = jnp.concatenate([c, c_ctx[None], jnp.zeros((MOD_ROWS - batch - 1, D_MODEL), F32)], axis=0)
    mod = _adaln(cond, w_mod[l], b_mod[l][None])

    row = lambda a: a[None]
    w_gates = jnp.pad(w_in[l][:, MAIN_W:], ((0, 0), (0, GATE_PAD - N_GATES)))
    cols = lambda *blks: jnp.concatenate(
        [w_in[l][:, k * GROUP_W:(k + 1) * GROUP_W] for k in blks] + [w_gates], axis=1).astype(BF16)
    gate_bias = jnp.pad(jnp.concatenate([b_igate[l].reshape(-1), b_fgate[l].reshape(-1)]),
                        (0, GATE_PAD - N_GATES))[None]
    k_scale = HEAD_DIM ** -0.5

    k_c, v_c, gates_c = _inproj(ctx, mod, row(norm_mix_g[l]), cols(K_BLK, V_BLK), conv_k[l][None],
                                (k_scale,), tm=ctx.shape[1], ctx_row=ctx_row)
    state = _mlstm(None, k_c, v_c, 0, gates_c, gate_bias, None)

    q_l, k_l, p, gates = _inproj(x, mod, row(norm_mix_g[l]),
                                 cols(Q_BLK, K_BLK, U_BLK, VG_BLK, O_BLK, V_BLK),
                                 jnp.stack([conv_q[l], conv_k[l]]), (1.0, k_scale), tm=512, ctx_row=None)
    h_f, h_b = _mlstm(q_l, k_l, p, P_V, gates, gate_bias, state)

    x1, h2, aff_t = _postmix(p, h_f, h_b, x, mod, gmlp_ws[l].astype(BF16), gmlp_bs[l][:, :, None],
                             row(mlstm_norm_g[l]), w_out[l].astype(BF16), row(norm_ffn_g[l]),
                             jnp.pad(w_router[l], ((0, 0), (0, 128 - N_EXPERTS))), tm=512)

    slot, gate_w, slot_t, first_t, first = _route(aff_t, capacity)

    xe, gate = _gather(h2, slot, gate_w, first, capacity)
    ye = _ffn(xe, gate, w_gate_e[l], w_up_e[l], w_down_e[l], f_tile=512)
    return _combine(slot_t, first_t, ye, x1, mod, row(final_g), capacity, tm=512)
```

```python
import functools

import jax
import jax.numpy as jnp
from jax import lax
from jax.experimental import pallas as pl
from jax.experimental.pallas import tpu as pltpu

F32 = jnp.float32
BF16 = jnp.bfloat16

D_MODEL = 1024
GRID_W = 64
CHUNK = 128
HEADS = 4
GROUP_W = D_MODEL // 2
HEAD_DIM = GROUP_W // HEADS
N_EXPERTS = 16
EC_FACTOR = 2
D_EXPERT = 2 * D_MODEL
EPS = 1e-6
N_GATES = 4 * HEADS
GATE_PAD = 128
MOD_ROWS = 8

U_BLK, VG_BLK, Q_BLK, O_BLK, K_BLK, V_BLK = 0, 1, 2, 3, 4, 5
MAIN_W = 6 * GROUP_W
P_U, P_VG, P_O, P_V = 0, 1, 2, 3

VMEM_LIMIT = 56 * 1024 * 1024
ROW_GROUP = 2 * CHUNK
MLSTM_SAMPLES = 4

def _params(*sem):
    return pltpu.CompilerParams(dimension_semantics=sem, vmem_limit_bytes=VMEM_LIMIT)


def _dot(a, b):
    return jnp.dot(a, b, preferred_element_type=F32)


def _dot_nt(a, b):
    return lax.dot_general(a, b, (((1,), (1,)), ((), ())), preferred_element_type=F32)


def _split2(a):
    hi = a.astype(BF16)
    lo = (a - hi.astype(F32)).astype(BF16)
    return hi, lo


def _dot3(a, b):
    ah, al = _split2(a)
    bh, bl = _split2(b)
    return _dot(ah, bh) + (_dot(al, bh) + _dot(ah, bl))


def _dot_exact01(tri, x):
    x1 = x.astype(BF16)
    r1 = x - x1.astype(F32)
    x2 = r1.astype(BF16)
    x3 = (r1 - x2.astype(F32)).astype(BF16)
    return _dot(tri, x1) + (_dot(tri, x2) + _dot(tri, x3))


def _iota2(shape, dim):
    return lax.broadcasted_iota(jnp.int32, shape, dim)


def _adaln_kernel(cond_ref, w_ref, b_ref, o_ref):
    o_ref[...] = _dot3(jax.nn.silu(cond_ref[...]), w_ref[...]) + b_ref[...]


def _adaln(cond, w, b):
    n_out = w.shape[1]
    tn = 2 * D_MODEL
    assert n_out % tn == 0
    return pl.pallas_call(
        _adaln_kernel,
        grid=(n_out // tn,),
        in_specs=[pl.BlockSpec((MOD_ROWS, D_MODEL), lambda j: (0, 0)),
                  pl.BlockSpec((D_MODEL, tn), lambda j: (0, j)),
                  pl.BlockSpec((1, tn), lambda j: (0, j))],
        out_specs=pl.BlockSpec((MOD_ROWS, tn), lambda j: (0, j)),
        out_shape=jax.ShapeDtypeStruct((MOD_ROWS, n_out), F32),
        compiler_params=_params("arbitrary"),
        name="adaln",
    )(cond, w, b)


def _inproj_kernel(x_ref, xprev_ref, xnext_ref, mod_ref, g_ref, taps_ref, w_ref, *outs, ctx_row, scales):
    conv_refs, (p_ref, gate_ref) = outs[:len(scales)], outs[len(scales):]
    row = pl.program_id(0) if ctx_row is None else ctx_row
    j = pl.program_id(1)
    sh = mod_ref[pl.ds(row, 1), 0:D_MODEL]
    sc = mod_ref[pl.ds(row, 1), D_MODEL:2 * D_MODEL]

    def modulated(x):
        y = x * lax.rsqrt(jnp.mean(x * x, axis=-1, keepdims=True) + EPS) * g_ref[...]
        return y * (1.0 + sc) + sh

    n_groups = x_ref.shape[1] // ROW_GROUP
    before = modulated(xprev_ref[0]) * (j > 0).astype(F32)
    for r in range(n_groups):
        rows = slice(r * ROW_GROUP, (r + 1) * ROW_GROUP)
        h = modulated(x_ref[0, rows, :])
        if r == n_groups - 1:
            after = modulated(xnext_ref[0]) * (j < pl.num_programs(1) - 1).astype(F32)
        else:
            after = modulated(x_ref[0, (r + 1) * ROW_GROUP:(r + 1) * ROW_GROUP + 8, :])
        ext = jnp.concatenate([before, h, after], axis=0).astype(BF16)
        before = h[ROW_GROUP - 8:, :]
        y = _dot(ext, w_ref[...])
        n_ext, n_conv = ext.shape[0], len(scales) * GROUP_W
        p_ref[0, rows, :] = y[8:8 + ROW_GROUP, n_conv:n_conv + p_ref.shape[2]]
        gate_ref[0, rows, :] = y[8:8 + ROW_GROUP, n_conv + p_ref.shape[2]:]
        y_before = pltpu.roll(y, 1, axis=0)[8:8 + ROW_GROUP]
        y_after = pltpu.roll(y, n_ext - 1, axis=0)[8:8 + ROW_GROUP]
        y_here = y[8:8 + ROW_GROUP]
        for s, scale in enumerate(scales):
            lanes = slice(s * GROUP_W, (s + 1) * GROUP_W)
            z = (y_before[:, lanes] * taps_ref[s, 0:1, :] + y_here[:, lanes] * taps_ref[s, 1:2, :]
                 + y_after[:, lanes] * taps_ref[s, 2:3, :])
            conv_refs[s][0, rows, :] = (jax.nn.silu(z) * scale).astype(BF16)


def _inproj(x, mod, g, w, taps, scales, tm, ctx_row):
    b, n, _ = x.shape
    wn = w.shape[1] - len(scales) * GROUP_W - GATE_PAD
    assert tm % ROW_GROUP == 0 and n % tm == 0 and wn % 128 == 0
    rows8 = tm // 8
    last8 = n // 8 - 1
    full = lambda a: pl.BlockSpec(a.shape, lambda i, j: (0,) * a.ndim)
    conv_spec = pl.BlockSpec((1, tm, GROUP_W), lambda i, j: (i, j, 0))
    return pl.pallas_call(
        functools.partial(_inproj_kernel, ctx_row=ctx_row, scales=tuple(scales)),
        grid=(b, n // tm),
        in_specs=[pl.BlockSpec((1, tm, D_MODEL), lambda i, j: (i, j, 0)),
                  pl.BlockSpec((1, 8, D_MODEL), lambda i, j: (i, jnp.maximum(j * rows8 - 1, 0), 0)),
                  pl.BlockSpec((1, 8, D_MODEL), lambda i, j: (i, jnp.minimum((j + 1) * rows8, last8), 0)),
                  full(mod), full(g), full(taps), full(w)],
        out_specs=[conv_spec] * len(scales)
        + [pl.BlockSpec((1, tm, wn), lambda i, j: (i, j, 0)),
           pl.BlockSpec((1, tm, GATE_PAD), lambda i, j: (i, j, 0))],
        out_shape=[jax.ShapeDtypeStruct((b, n, GROUP_W), BF16)] * len(scales)
        + [jax.ShapeDtypeStruct((b, n, wn), F32), jax.ShapeDtypeStruct((b, n, GATE_PAD), F32)],
        compiler_params=_params("arbitrary", "arbitrary"),
        name="inproj",
    )(x, x, x, mod, g, taps, w)


def _mlstm_chunk(per_dir, bias_ref, c_s, n_s, m_s, h_refs):
    with_h = h_refs[0] is not None
    i0 = _iota2((CHUNK, CHUNK), 0)
    i1 = _iota2((CHUNK, CHUNK), 1)

    units = []
    for smp, d in [(smp, d) for smp in range(c_s.shape[0]) for d in range(2)]:
        q_ref, k_ref, v_ref, g_ref = per_dir[d]
        sees_ts = (i1 <= i0) if d == 0 else (i1 >= i0)
        sees_st = (i0 <= i1) if d == 0 else (i0 >= i1)
        tri = jnp.where(sees_ts, 1.0, 0.0).astype(BF16)
        gates = g_ref[smp] + bias_ref[...]
        bcum = _dot_exact01(tri, jax.nn.log_sigmoid(gates))
        for hd in range(HEADS):
            lanes = slice(hd * HEAD_DIM, (hd + 1) * HEAD_DIM)
            c_prev = c_s[smp, d, hd]
            n_prev = n_s[smp, d, hd]
            k = k_ref[smp, :, lanes]
            st = dict(smp=smp, d=d, hd=hd, lanes=lanes, c_prev=c_prev, n_prev=n_prev, k=k,
                      sees_st=sees_st, gates=gates, bcum=bcum, v=v_ref[smp, :, lanes])
            if with_h:
                lhs = jnp.concatenate([k, c_prev.astype(BF16),
                                       jnp.broadcast_to(n_prev, (16, HEAD_DIM)).astype(BF16)], axis=0)
                st["prod"] = _dot_nt(lhs, q_ref[smp, :, lanes])
            units.append(st)

    rows = {}
    for st in units:
        smp, d, hd = st["smp"], st["d"], st["hd"]
        if (smp, d) not in rows:
            rows[smp, d] = (st["gates"].T, st["bcum"].T)
        gates_t, bcum_t = rows[smp, d]
        ci = d * HEADS + hd
        cf = 2 * HEADS + d * HEADS + hd
        last = CHUNK - 1 if d == 0 else 0
        li_row = gates_t[ci:ci + 1, :]
        bc_row = bcum_t[cf:cf + 1, :]
        b_last = bc_row[:, last:last + 1]
        m_prev = m_s[smp, d, hd]
        v_t = st["v"].T
        a_row = b_last - bc_row + li_row
        m_new = jnp.maximum(b_last + m_prev, jnp.max(a_row, axis=-1, keepdims=True))
        w_row = jnp.exp(a_row - m_new)
        lhs = jnp.concatenate([v_t * w_row, jnp.broadcast_to(w_row, (16, CHUNK))], axis=0)
        st["upd"] = _dot(lhs.astype(BF16), st["k"])
        st.update(bc_row=bc_row, m_prev=m_prev, m_new=m_new, v_t=v_t,
                  decay=jnp.exp(b_last + m_prev - m_new),
                  u_col=st["gates"][:, ci:ci + 1] - st["bcum"][:, cf:cf + 1])

    if with_h:
        for st in units:
            prod, bc_row = st["prod"], st["bc_row"]
            g = bc_row + st["m_prev"]
            dmat = jnp.where(st["sees_st"], st["u_col"] + bc_row, -jnp.inf)
            m_t = jnp.maximum(g, jnp.max(dmat, axis=0, keepdims=True))
            inter = jnp.exp(g - m_t)
            s = prod[0:CHUNK] * jnp.exp(dmat - m_t)
            st["pv"] = _dot(st["v_t"].astype(BF16), s.astype(BF16))
            st["num0"] = inter * prod[CHUNK:2 * CHUNK]
            den = inter * prod[2 * CHUNK:2 * CHUNK + 1] + jnp.sum(s, axis=0, keepdims=True)
            st["scale"] = 1.0 / jnp.maximum(jnp.abs(den), jnp.exp(-m_t))

    for st in units:
        smp, d, hd = st["smp"], st["d"], st["hd"]
        if with_h:
            h_refs[d][smp, :, st["lanes"]] = ((st["num0"] + st["pv"]) * st["scale"]).T
        c_s[smp, d, hd] = st["decay"] * st["c_prev"] + st["upd"][0:HEAD_DIM]
        n_s[smp, d, hd] = st["decay"] * st["n_prev"] + st["upd"][HEAD_DIM:HEAD_DIM + 1]
        m_s[smp, d, hd] = st["m_new"]


def _mlstm_kernel(*refs, n_chunks, with_h):
    refs = list(refs)
    take = lambda n: [refs.pop(0) for _ in range(n)]
    per_dir = []
    for _ in range(2):
        q_ref = take(1)[0] if with_h else None
        k_ref, v_ref, g_ref = take(3)
        per_dir.append((q_ref, k_ref, v_ref, g_ref))
    bias_ref = take(1)[0]
    if with_h:
        c0_ref, n0_ref, m0_ref = take(3)
        h_refs = take(2)
        c_out = n_out = m_out = None
    else:
        h_refs = [None, None]
        c_out, n_out, m_out = take(3)
    c_s, n_s, m_s = take(3)

    j = pl.program_id(1)

    @pl.when(j == 0)
    def _():
        if with_h:
            c_s[...] = c0_ref[...]
            n_s[...] = n0_ref[...]
            m_s[...] = m0_ref[...]
        else:
            c_s[...] = jnp.zeros_like(c_s)
            n_s[...] = jnp.zeros_like(n_s)
            m_s[...] = jnp.zeros_like(m_s)

    _mlstm_chunk(per_dir, bias_ref, c_s, n_s, m_s, h_refs)

    if not with_h:
        @pl.when(j == n_chunks - 1)
        def _():
            c_out[...] = c_s[...]
            n_out[...] = n_s[...]
            m_out[...] = m_s[...]


def _mlstm(q, k, p, v_blk, gates, bias, state):
    b, n, _ = k.shape
    n_chunks = n // CHUNK
    with_h = q is not None
    ns = MLSTM_SAMPLES if b % MLSTM_SAMPLES == 0 else 1

    in_specs, args = [], []
    for d in range(2):
        c = (lambda j: j) if d == 0 else (lambda j: n_chunks - 1 - j)
        tok = pl.BlockSpec((ns, CHUNK, GROUP_W), lambda i, j, c=c: (i, c(j), 0))
        if with_h:
            in_specs.append(tok)
            args.append(q)
        in_specs += [tok,
                     pl.BlockSpec((ns, CHUNK, GROUP_W), lambda i, j, c=c: (i, c(j), v_blk)),
                     pl.BlockSpec((ns, CHUNK, GATE_PAD), lambda i, j, c=c: (i, c(j), 0))]
        args += [k, p, gates]
    in_specs.append(pl.BlockSpec((1, GATE_PAD), lambda i, j: (0, 0)))
    args.append(bias)

    c_shape = (2, HEADS, HEAD_DIM, HEAD_DIM)
    v_shape = (2, HEADS, 1, HEAD_DIM)
    c_spec = pl.BlockSpec((ns,) + c_shape, lambda i, j: (i, 0, 0, 0, 0))
    v_spec = pl.BlockSpec((ns,) + v_shape, lambda i, j: (i, 0, 0, 0, 0))
    if with_h:
        in_specs += [c_spec, v_spec, v_spec]
        args += list(state)
        out_specs = [pl.BlockSpec((ns, CHUNK, GROUP_W), lambda i, j: (i, j, 0)),
                     pl.BlockSpec((ns, CHUNK, GROUP_W), lambda i, j: (i, n_chunks - 1 - j, 0))]
        out_shape = [jax.ShapeDtypeStruct((b, n, GROUP_W), F32)] * 2
    else:
        out_specs = [c_spec, v_spec, v_spec]
        out_shape = [jax.ShapeDtypeStruct((b,) + c_shape, F32),
                     jax.ShapeDtypeStruct((b,) + v_shape, F32),
                     jax.ShapeDtypeStruct((b,) + v_shape, F32)]
    return pl.pallas_call(
        functools.partial(_mlstm_kernel, n_chunks=n_chunks, with_h=with_h),
        grid=(b // ns, n_chunks),
        in_specs=in_specs,
        out_specs=out_specs,
        out_shape=out_shape,
        scratch_shapes=[pltpu.VMEM((ns,) + c_shape, F32), pltpu.VMEM((ns,) + v_shape, F32),
                        pltpu.VMEM((ns,) + v_shape, F32)],
        compiler_params=_params("arbitrary", "arbitrary"),
        name="mlstm" if with_h else "mlstm_ctx_state",
    )(*args)


def _layer_norm(x):
    mu = jnp.mean(x, axis=-1, keepdims=True)
    var = jnp.mean(jnp.square(x - mu), axis=-1, keepdims=True)
    return (x - mu) * lax.rsqrt(var + EPS)


def _postmix_kernel(u_ref, vg_ref, o_ref, hf_ref, hb_ref, x_ref, mod_ref, ws_ref, bs_ref, ng_ref,
                    wout_ref, fg_ref, wr_ref, x1_ref, h2_ref, aff_ref, ycat_s):
    tm = x_ref.shape[1]
    b = pl.program_id(0)
    mod = lambda k: mod_ref[pl.ds(b, 1), k * D_MODEL:(k + 1) * D_MODEL]
    wr_hi, wr_lo = _split2(wr_ref[...])
    expert_lane = _iota2((ROW_GROUP, 128), 1) < N_EXPERTS

    for r in range(tm // ROW_GROUP):
        rows = slice(r * ROW_GROUP, (r + 1) * ROW_GROUP)

        for c in range(r * ROW_GROUP // CHUNK, (r + 1) * ROW_GROUP // CHUNK):
            crows = slice(c * CHUNK, (c + 1) * CHUNK)
            u = jax.nn.gelu(u_ref[0, crows, :])
            v = _layer_norm(jax.nn.gelu(vg_ref[0, crows, :])).astype(BF16)
            for hd in range(HEADS):
                lanes = slice(hd * HEAD_DIM, (hd + 1) * HEAD_DIM)
                s = _dot(ws_ref[hd], v[:, lanes]) + bs_ref[hd]
                ycat_s[crows, lanes] = (u[:, lanes] * s).astype(BF16)

        hsum = hf_ref[0, rows, :] + hb_ref[0, rows, :]
        o = jax.nn.sigmoid(o_ref[0, rows, :])
        for hd in range(HEADS):
            lanes = slice(hd * HEAD_DIM, (hd + 1) * HEAD_DIM)
            hn = _layer_norm(hsum[:, lanes]) * ng_ref[:, lanes]
            ycat_s[rows, GROUP_W + hd * HEAD_DIM:GROUP_W + (hd + 1) * HEAD_DIM] = (
                o[:, lanes] * hn).astype(BF16)

        y = _dot(ycat_s[rows, :], wout_ref[...])
        x1 = x_ref[0, rows, :] + mod(2) * y
        x1_ref[0, rows, :] = x1

        n2 = x1 * lax.rsqrt(jnp.mean(x1 * x1, axis=-1, keepdims=True) + EPS) * fg_ref[...]
        h2 = n2 * (1.0 + mod(4)) + mod(3)
        h2_hi, h2_lo = _split2(h2)
        h2_ref[0, rows, :] = h2_hi

        logits = _dot(h2_hi, wr_hi) + (_dot(h2_lo, wr_hi) + _dot(h2_hi, wr_lo))
        logits = jnp.where(expert_lane, logits, -jnp.inf)
        e = jnp.exp(logits - jnp.max(logits, axis=-1, keepdims=True))
        aff = e / jnp.sum(e, axis=-1, keepdims=True)
        aff_ref[0, :, rows] = aff.T[0:N_EXPERTS, :]


def _postmix(p, hf, hb, x, mod, ws, bs, ng, wout, fg, wr, tm):
    b, n, _ = x.shape
    tok = lambda blk: pl.BlockSpec((1, tm, GROUP_W), lambda i, j: (i, j, blk))
    full = lambda a: pl.BlockSpec(a.shape, lambda i, j: (0,) * a.ndim)
    return pl.pallas_call(
        _postmix_kernel,
        grid=(b, n // tm),
        in_specs=[tok(P_U), tok(P_VG), tok(P_O), tok(0), tok(0),
                  pl.BlockSpec((1, tm, D_MODEL), lambda i, j: (i, j, 0)),
                  full(mod), full(ws), full(bs), full(ng), full(wout), full(fg), full(wr)],
        out_specs=[pl.BlockSpec((1, tm, D_MODEL), lambda i, j: (i, j, 0)),
                   pl.BlockSpec((1, tm, D_MODEL), lambda i, j: (i, j, 0)),
                   pl.BlockSpec((1, N_EXPERTS, tm), lambda i, j: (i, 0, j))],
        out_shape=[jax.ShapeDtypeStruct((b, n, D_MODEL), F32),
                   jax.ShapeDtypeStruct((b, n, D_MODEL), BF16),
                   jax.ShapeDtypeStruct((b, N_EXPERTS, n), F32)],
        scratch_shapes=[pltpu.VMEM((tm, D_MODEL), BF16)],
        compiler_params=_params("arbitrary", "arbitrary"),
        name="postmix",
    )(p, p, p, hf, hb, x, mod, ws, bs, ng, wout, fg, wr)


def _cumsum_lanes(x, upper):
    carry = jnp.zeros((x.shape[0], 1), F32)
    outs, before = [], []
    for j in range(x.shape[1] // 128):
        before.append(carry)
        c = _dot(x[:, j * 128:(j + 1) * 128].astype(BF16), upper) + carry
        outs.append(c)
        carry = c[:, 127:128]
    return jnp.concatenate(outs, axis=1), before


def _rows_to_lanes(x, fill):
    pad = jnp.full((128 - x.shape[0], 128), fill, F32)
    return jnp.concatenate([x, pad], axis=0).T


def _route_kernel(aff_ref, slot_ref, w_ref, slot_t_ref, first_t_ref, first_ref, *, capacity, n_e):
    aff = aff_ref[...]
    cap = float(capacity)
    thr_bits = jnp.zeros((aff.shape[0], 1), jnp.int32)
    for bit in range(30, -1, -1):
        cand = thr_bits | (1 << bit)
        cnt = jnp.sum(jnp.where(aff >= pltpu.bitcast(cand, F32), 1.0, 0.0), axis=-1, keepdims=True)
        thr_bits = jnp.where(cnt >= cap, cand, thr_bits)
    thr = pltpu.bitcast(thr_bits, F32)
    upper = jnp.where(_iota2((128, 128), 0) <= _iota2((128, 128), 1), 1.0, 0.0).astype(BF16)
    above = jnp.where(aff > thr, 1.0, 0.0)
    tied = jnp.where(aff == thr, 1.0, 0.0)
    need = cap - jnp.sum(above, axis=-1, keepdims=True)
    sel = above + tied * jnp.where(_cumsum_lanes(tied, upper)[0] <= need, 1.0, 0.0)
    count, before = _cumsum_lanes(sel, upper)
    slot = jnp.where(sel > 0.0, count - 1.0, -1.0)
    slot_ref[...] = slot
    w_ref[...] = jnp.where(sel > 0.0, aff, 0.0)

    n_blocks = len(before)
    lane = _iota2((1, 128), 1)
    first = jnp.zeros((aff.shape[0], 128), F32)
    for j in range(n_blocks):
        first = jnp.where(lane == j, before[j], first)
    first_ref[...] = first
    for smp in range(aff.shape[0] // n_e):
        rows = slice(smp * n_e, (smp + 1) * n_e)
        for j in range(n_blocks):
            slot_t_ref[smp, j * 128:(j + 1) * 128, :] = _rows_to_lanes(
                slot[rows, j * 128:(j + 1) * 128], -1.0)
        first_t_ref[smp] = _rows_to_lanes(first[rows, :], 0.0)[0:n_blocks, :]


def _route(aff_t, capacity):
    b, e, n = aff_t.shape
    assert n // 128 <= 128 and e <= 128 and e % 8 == 0
    rows = pl.BlockSpec((b * e, n), lambda i: (0, 0))
    slot, w, slot_t, first_t, first = pl.pallas_call(
        functools.partial(_route_kernel, capacity=capacity, n_e=e),
        grid=(1,),
        in_specs=[rows],
        out_specs=[rows, rows,
                   pl.BlockSpec((b, n, 128), lambda i: (0, 0, 0)),
                   pl.BlockSpec((b, n // 128, 128), lambda i: (0, 0, 0)),
                   pl.BlockSpec((b * e, 128), lambda i: (0, 0))],
        out_shape=[jax.ShapeDtypeStruct((b * e, n), F32)] * 2
        + [jax.ShapeDtypeStruct((b, n, 128), F32), jax.ShapeDtypeStruct((b, n // 128, 128), F32),
           jax.ShapeDtypeStruct((b * e, 128), F32)],
        compiler_params=_params("arbitrary"),
        name="route",
    )(aff_t.reshape(b * e, n))
    return slot.reshape(b, e, n), w.reshape(b, e, n), slot_t, first_t, first.reshape(b, e, 128)


GATHER_TILE = 256
GATHER_ROWS = (64, 128)
GATHER_GROUP = 4


def _gather_kernel(before_ref, fits_ref, h2_ref, slot_ref, w_ref, xe_ref, gate_ref, xe_s, gate_s):
    b = pl.program_id(0)
    g = pl.program_id(1)
    n_g, cap = xe_ref.shape[1], xe_ref.shape[2]
    n = h2_ref.shape[1]
    n_tiles = n // GATHER_TILE

    def windowed(n_rows):
        xe_s[:, 0:16, :] = jnp.zeros((n_g, 16, D_MODEL), BF16)
        gate_s[...] = jnp.zeros_like(gate_s)
        row_id = _iota2((n_rows, 1), 0).astype(F32)
        for j in range(n_tiles):
            starts, onehots = [], []
            for e in range(n_g):
                start = pl.multiple_of((before_ref[b, g * n_g + e, j] // 16) * 16, 16)
                hit = (slot_ref[0, e, j:j + 1, :] - start.astype(F32)) == row_id
                onehots.append(jnp.where(hit, 1.0, 0.0).astype(BF16))
                gate_s[e, pl.ds(start, n_rows), :] += jnp.sum(
                    jnp.where(hit, w_ref[0, e, j:j + 1, :], 0.0), axis=1, keepdims=True)
                starts.append(start)
            rows = _dot(jnp.concatenate(onehots, axis=0),
                        h2_ref[0, j * GATHER_TILE:(j + 1) * GATHER_TILE, :]).astype(BF16)
            for e, start in enumerate(starts):
                r0 = e * n_rows
                xe_s[e, pl.ds(start, 16), :] += rows[r0:r0 + 16]
                xe_s[e, pl.ds(start + 16, n_rows - 16), :] = rows[r0 + 16:r0 + n_rows]
        xe_ref[0] = xe_s[:, 0:cap, :]
        gate_ref[0] = gate_s[:, 0:cap, :]

    for level, n_rows in enumerate(GATHER_ROWS):
        pl.when(fits_ref[b, g] == level)(functools.partial(windowed, n_rows))

    @pl.when(fits_ref[b, g] == len(GATHER_ROWS))
    def _():
        slot_id = _iota2((cap, 1), 0).astype(F32)
        for e in range(n_g):
            xe = jnp.zeros((cap, D_MODEL), F32)
            gate = jnp.zeros((cap, 1), F32)
            for j in range(n_tiles):
                hit = slot_ref[0, e, j:j + 1, :] == slot_id
                xe = xe + _dot(jnp.where(hit, 1.0, 0.0).astype(BF16),
                               h2_ref[0, j * GATHER_TILE:(j + 1) * GATHER_TILE, :])
                gate = gate + jnp.sum(jnp.where(hit, w_ref[0, e, j:j + 1, :], 0.0), axis=1,
                                      keepdims=True)
            xe_ref[0, e] = xe.astype(BF16)
            gate_ref[0, e] = gate


def _gather(h2, slot, w, first, capacity):
    b, n, _ = h2.shape
    n_e = slot.shape[1]
    n_tiles = n // GATHER_TILE
    assert n_e % GATHER_GROUP == 0 and GATHER_TILE % 128 == 0 and capacity % 16 == 0
    assert all(r % 16 == 0 and r >= 32 for r in GATHER_ROWS) and list(GATHER_ROWS) == sorted(GATHER_ROWS)
    before = first[:, :, 0:n // 128:GATHER_TILE // 128].astype(jnp.int32)
    after = jnp.concatenate([before[:, :, 1:], jnp.full((b, n_e, 1), capacity, jnp.int32)], axis=2)
    extent = jnp.max(after - (before // 16) * 16, axis=2)
    extent = jnp.max(extent.reshape(b, n_e // GATHER_GROUP, GATHER_GROUP), axis=2)
    fits = sum((extent >= n_rows).astype(jnp.int32) for n_rows in GATHER_ROWS)
    tiles = lambda a: a.reshape(b, n_e, n_tiles, GATHER_TILE)
    rows = pl.BlockSpec((1, GATHER_GROUP, n_tiles, GATHER_TILE), lambda i, g, *_: (i, g, 0, 0))
    return pl.pallas_call(
        _gather_kernel,
        grid_spec=pltpu.PrefetchScalarGridSpec(
            num_scalar_prefetch=2,
            grid=(b, n_e // GATHER_GROUP),
            in_specs=[pl.BlockSpec((1, n, D_MODEL), lambda i, g, *_: (i, 0, 0)), rows, rows],
            out_specs=[pl.BlockSpec((1, GATHER_GROUP, capacity, D_MODEL), lambda i, g, *_: (i, g, 0, 0)),
                       pl.BlockSpec((1, GATHER_GROUP, capacity, 1), lambda i, g, *_: (i, g, 0, 0))],
            scratch_shapes=[pltpu.VMEM((GATHER_GROUP, capacity + max(GATHER_ROWS), D_MODEL), BF16),
                            pltpu.VMEM((GATHER_GROUP, capacity + max(GATHER_ROWS), 1), F32)]),
        out_shape=[jax.ShapeDtypeStruct((b, n_e, capacity, D_MODEL), BF16),
                   jax.ShapeDtypeStruct((b, n_e, capacity, 1), F32)],
        compiler_params=_params("arbitrary", "arbitrary"),
        name="gather",
    )(before, fits, h2, tiles(slot), tiles(w))


def _ffn_kernel(xe_ref, gate_ref, wg_ref, wu_ref, wd_ref, ye_ref, acc_s):
    ft = pl.program_id(1)
    last = pl.num_programs(1) - 1

    def step(first, final):
        wg = wg_ref[0].astype(BF16)
        wu = wu_ref[0].astype(BF16)
        wd = wd_ref[0].astype(BF16)
        for i in range(xe_ref.shape[0]):
            xe = xe_ref[i, 0]
            act = jax.nn.silu(_dot(xe, wg)) * _dot(xe, wu)
            part = _dot(act.astype(BF16), wd)
            total = part if first else acc_s[i] + part
            if final:
                ye_ref[i] = (total * gate_ref[i, 0]).astype(BF16)
            else:
                acc_s[i] = total

    pl.when(ft == 0)(lambda: step(True, False))
    pl.when(jnp.logical_and(ft > 0, ft < last))(lambda: step(False, False))
    pl.when(ft == last)(lambda: step(False, True))


def _ffn(xe, gate, wg, wu, wd, f_tile):
    b, n_e, cap, _ = xe.shape
    assert D_EXPERT // f_tile >= 2
    return pl.pallas_call(
        _ffn_kernel,
        grid=(n_e, D_EXPERT // f_tile),
        in_specs=[pl.BlockSpec((b, 1, cap, D_MODEL), lambda e, f: (0, e, 0, 0)),
                  pl.BlockSpec((b, 1, cap, 1), lambda e, f: (0, e, 0, 0)),
                  pl.BlockSpec((1, D_MODEL, f_tile), lambda e, f: (e, 0, f)),
                  pl.BlockSpec((1, D_MODEL, f_tile), lambda e, f: (e, 0, f)),
                  pl.BlockSpec((1, f_tile, D_MODEL), lambda e, f: (e, f, 0))],
        out_specs=pl.BlockSpec((b, cap, D_MODEL), lambda e, f: (0, e, 0)),
        out_shape=jax.ShapeDtypeStruct((b, n_e * cap, D_MODEL), BF16),
        scratch_shapes=[pltpu.VMEM((b, cap, D_MODEL), F32)],
        compiler_params=_params("arbitrary", "arbitrary"),
        name="ffn",
    )(xe, gate, wg, wu, wd)


COMBINE_WIDE = CHUNK + 16
COMBINE_NARROW = 48


def _combine_kernel(first_ref, narrow_ref, slot_t_ref, first_t_ref, ye_ref, x1_ref, mod_ref, fg_ref,
                    o_ref, acc_s, *, capacity):
    b = pl.program_id(0)
    j = pl.program_id(1)
    n_sub = o_ref.shape[1] // 128
    lane = _iota2((1, 128), 1).astype(F32)

    def scatter(window):
        k_total = N_EXPERTS * window
        for sb in range(n_sub):
            blk = j * n_sub + sb
            slot_t = slot_t_ref[0, sb * 128:(sb + 1) * 128, :]
            start_row = jnp.minimum(jnp.floor(first_t_ref[0, pl.ds(blk, 1), :] * (1.0 / 16.0)) * 16.0,
                                    float(capacity - window))
            k_pos = jnp.where(slot_t >= 0.0, slot_t - start_row + lane * float(window), -1.0)
            cols = []
            for c in range(k_total // 128):
                k_lane = lane + float(128 * c)
                hit = jnp.zeros((128, 128), F32)
                for e in range((128 * c) // window, (128 * c + 127) // window + 1):
                    hit = jnp.where(k_pos[:, e:e + 1] == k_lane, 1.0, hit)
                cols.append(hit.astype(BF16))
            onehot = jnp.concatenate(cols, axis=1)
            rows = []
            for e in range(N_EXPERTS):
                start = jnp.minimum((first_ref[b, blk, e] // 16) * 16, capacity - window)
                rows.append(ye_ref[0, pl.ds(pl.multiple_of(e * capacity + start, 16), window), :])
            acc_s[sb * 128:(sb + 1) * 128, :] = _dot(onehot, jnp.concatenate(rows, axis=0))

    pl.when(narrow_ref[b, j] != 0)(lambda: scatter(COMBINE_NARROW))
    pl.when(narrow_ref[b, j] == 0)(lambda: scatter(COMBINE_WIDE))

    g2 = mod_ref[pl.ds(b, 1), 5 * D_MODEL:6 * D_MODEL]
    x2 = x1_ref[0] + g2 * acc_s[...]
    o_ref[0] = x2 * lax.rsqrt(jnp.mean(x2 * x2, axis=-1, keepdims=True) + EPS) * fg_ref[...]


def _combine(slot_t, first_t, ye, x1, mod, fg, capacity, tm):
    b, n, _ = x1.shape
    n_blocks = n // 128
    for window in (COMBINE_WIDE, COMBINE_NARROW):
        assert (N_EXPERTS * window) % 128 == 0 and window % 16 == 0 and window <= capacity
    assert COMBINE_WIDE >= 128 + 15 and capacity % 16 == 0
    first = first_t[:, :, :N_EXPERTS].astype(jnp.int32)
    after = jnp.concatenate([first[:, 1:], jnp.full((b, 1, N_EXPERTS), capacity, jnp.int32)], axis=1)
    narrow = jnp.all(after - (first // 16) * 16 <= COMBINE_NARROW, axis=2)
    narrow = jnp.all(narrow.reshape(b, n // tm, tm // 128), axis=2).astype(jnp.int32)
    tok = pl.BlockSpec((1, tm, D_MODEL), lambda i, j, *_: (i, j, 0))
    return pl.pallas_call(
        functools.partial(_combine_kernel, capacity=capacity),
        grid_spec=pltpu.PrefetchScalarGridSpec(
            num_scalar_prefetch=2,
            grid=(b, n // tm),
            in_specs=[pl.BlockSpec((1, tm, 128), lambda i, j, *_: (i, j, 0)),
                      pl.BlockSpec((1, n_blocks, 128), lambda i, j, *_: (i, 0, 0)),
                      pl.BlockSpec((1, N_EXPERTS * capacity, D_MODEL), lambda i, j, *_: (i, 0, 0)),
                      tok,
                      pl.BlockSpec(mod.shape, lambda i, j, *_: (0, 0)),
                      pl.BlockSpec((1, D_MODEL), lambda i, j, *_: (0, 0))],
            out_specs=tok,
            scratch_shapes=[pltpu.VMEM((tm, D_MODEL), F32)]),
        out_shape=jax.ShapeDtypeStruct((b, n, D_MODEL), F32),
        compiler_params=_params("arbitrary", "arbitrary"),
        name="combine",
    )(first, narrow, slot_t, first_t, ye, x1, mod, fg)


def kernel(x, c, ctx, c_ctx, w_mod, b_mod, norm_mix_g, w_in, conv_q, conv_k, b_igate, b_fgate,
           gmlp_ws, gmlp_bs, mlstm_norm_g, w_out, norm_ffn_g, w_router, w_gate_e, w_up_e,
           w_down_e, final_g):
    depth = w_mod.shape[0]
    assert depth == 1, "the context stream is only carried as mLSTM states (single layer)"
    batch, seq, _ = x.shape
    assert seq % GRID_W == 0 and seq % CHUNK == 0 and batch + 1 <= MOD_ROWS
    capacity = EC_FACTOR * seq // N_EXPERTS
    ctx_row = batch
    l = 0

    cond = jnp.concatenate([c, c_ctx[None], jnp.zeros((MOD_ROWS - batch - 1, D_MODEL), F32)], axis=0)
    mod = _adaln(cond, w_mod[l], b_mod[l][None])

    row = lambda a: a[None]
    w_gates = jnp.pad(w_in[l][:, MAIN_W:], ((0, 0), (0, GATE_PAD - N_GATES)))
    cols = lambda *blks: jnp.concatenate(
        [w_in[l][:, k * GROUP_W:(k + 1) * GROUP_W] for k in blks] + [w_gates], axis=1).astype(BF16)
    gate_bias = jnp.pad(jnp.concatenate([b_igate[l].reshape(-1), b_fgate[l].reshape(-1)]),
                        (0, GATE_PAD - N_GATES))[None]
    k_scale = HEAD_DIM ** -0.5

    k_c, v_c, gates_c = _inproj(ctx, mod, row(norm_mix_g[l]), cols(K_BLK, V_BLK), conv_k[l][None],
                                (k_scale,), tm=ctx.shape[1], ctx_row=ctx_row)
    state = _mlstm(None, k_c, v_c, 0, gates_c, gate_bias, None)

    q_l, k_l, p, gates = _inproj(x, mod, row(norm_mix_g[l]),
                                 cols(Q_BLK, K_BLK, U_BLK, VG_BLK, O_BLK, V_BLK),
                                 jnp.stack([conv_q[l], conv_k[l]]), (1.0, k_scale), tm=512, ctx_row=None)
    h_f, h_b = _mlstm(q_l, k_l, p, P_V, gates, gate_bias, state)

    x1, h2, aff_t = _postmix(p, h_f, h_b, x, mod, gmlp_ws[l].astype(BF16), gmlp_bs[l][:, :, None],
                             row(mlstm_norm_g[l]), w_out[l].astype(BF16), row(norm_ffn_g[l]),
                             jnp.pad(w_router[l], ((0, 0), (0, 128 - N_EXPERTS))), tm=512)

    slot, gate_w, slot_t, first_t, first = _route(aff_t, capacity)

    xe, gate = _gather(h2, slot, gate_w, first, capacity)
    ye = _ffn(xe, gate, w_gate_e[l], w_up_e[l], w_down_e[l], f_tile=512)
    return _combine(slot_t, first_t, ye, x1, mod, row(final_g), capacity, tm=512)
```

```python
import functools

import jax
import jax.numpy as jnp
from jax import lax
from jax.experimental import pallas as pl
from jax.experimental.pallas import tpu as pltpu

F32 = jnp.float32
BF16 = jnp.bfloat16

D_MODEL = 1024
GRID_W = 64
CHUNK = 128
HEADS = 4
GROUP_W = D_MODEL // 2
HEAD_DIM = GROUP_W // HEADS
N_EXPERTS = 16
EC_FACTOR = 2
D_EXPERT = 2 * D_MODEL
EPS = 1e-6
N_GATES = 4 * HEADS
GATE_PAD = 128
MOD_ROWS = 8

U_BLK, VG_BLK, Q_BLK, O_BLK, K_BLK, V_BLK = 0, 1, 2, 3, 4, 5
MAIN_W = 6 * GROUP_W
P_U, P_VG, P_O, P_V = 0, 1, 2, 3

VMEM_LIMIT = 56 * 1024 * 1024
ROW_GROUP = 2 * CHUNK
MLSTM_SAMPLES = 4


def _params(*sem):
    return pltpu.CompilerParams(dimension_semantics=sem, vmem_limit_bytes=VMEM_LIMIT)


def _dot(a, b):
    return jnp.dot(a, b, preferred_element_type=F32)


def _dot_nt(a, b):
    return lax.dot_general(a, b, (((1,), (1,)), ((), ())), preferred_element_type=F32)


def _split2(a):
    hi = a.astype(BF16)
    lo = (a - hi.astype(F32)).astype(BF16)
    return hi, lo


def _dot3(a, b):
    ah, al = _split2(a)
    bh, bl = _split2(b)
    return _dot(ah, bh) + (_dot(al, bh) + _dot(ah, bl))


def _dot_exact01(tri, x):
    x1 = x.astype(BF16)
    r1 = x - x1.astype(F32)
    x2 = r1.astype(BF16)
    x3 = (r1 - x2.astype(F32)).astype(BF16)
    return _dot(tri, x1) + (_dot(tri, x2) + _dot(tri, x3))


def _iota2(shape, dim):
    return lax.broadcasted_iota(jnp.int32, shape, dim)


def _adaln_kernel(cond_ref, w_ref, b_ref, o_ref):
    o_ref[...] = _dot3(jax.nn.silu(cond_ref[...]), w_ref[...]) + b_ref[...]


def _adaln(cond, w, b):
    n_out = w.shape[1]
    tn = D_MODEL
    assert n_out % tn == 0
    return pl.pallas_call(
        _adaln_kernel,
        grid=(n_out // tn,),
        in_specs=[pl.BlockSpec((MOD_ROWS, D_MODEL), lambda j: (0, 0)),
                  pl.BlockSpec((D_MODEL, tn), lambda j: (0, j)),
                  pl.BlockSpec((1, tn), lambda j: (0, j))],
        out_specs=pl.BlockSpec((MOD_ROWS, tn), lambda j: (0, j)),
        out_shape=jax.ShapeDtypeStruct((MOD_ROWS, n_out), F32),
        compiler_params=_params("arbitrary"),
        name="adaln",
    )(cond, w, b)


def _inproj_kernel(x_ref, xprev_ref, xnext_ref, mod_ref, g_ref, taps_ref, w_ref, *outs, ctx_row, scales,
                   plain_acts):
    conv_refs, (p_ref, gate_ref) = outs[:len(scales)], outs[len(scales):]
    row = pl.program_id(0) if ctx_row is None else ctx_row
    j = pl.program_id(1)
    sh = mod_ref[pl.ds(row, 1), 0:D_MODEL]
    sc = mod_ref[pl.ds(row, 1), D_MODEL:2 * D_MODEL]

    def modulated(x):
        y = x * lax.rsqrt(jnp.mean(x * x, axis=-1, keepdims=True) + EPS) * g_ref[...]
        return y * (1.0 + sc) + sh

    n_groups = x_ref.shape[1] // ROW_GROUP
    before = modulated(xprev_ref[0]) * (j > 0).astype(F32)
    for r in range(n_groups):
        rows = slice(r * ROW_GROUP, (r + 1) * ROW_GROUP)
        h = modulated(x_ref[0, rows, :])
        if r == n_groups - 1:
            after = modulated(xnext_ref[0]) * (j < pl.num_programs(1) - 1).astype(F32)
        else:
            after = modulated(x_ref[0, (r + 1) * ROW_GROUP:(r + 1) * ROW_GROUP + 8, :])
        ext = jnp.concatenate([before, h, after], axis=0).astype(BF16)
        before = h[ROW_GROUP - 8:, :]
        y = _dot(ext, w_ref[...])
        n_ext, n_conv = ext.shape[0], len(scales) * GROUP_W
        y_here = y[8:8 + ROW_GROUP]
        for k, act in enumerate(plain_acts):
            lanes = slice(n_conv + k * GROUP_W, n_conv + (k + 1) * GROUP_W)
            p_ref[0, rows, k * GROUP_W:(k + 1) * GROUP_W] = act(y_here[:, lanes])
        gate_ref[0, rows, :] = y_here[:, n_conv + p_ref.shape[2]:]
        y_conv = y[:, 0:n_conv]
        y_before = pltpu.roll(y_conv, 1, axis=0)[8:8 + ROW_GROUP]
        y_after = pltpu.roll(y_conv, n_ext - 1, axis=0)[8:8 + ROW_GROUP]
        for s, scale in enumerate(scales):
            lanes = slice(s * GROUP_W, (s + 1) * GROUP_W)
            z = (y_before[:, lanes] * taps_ref[s, 0:1, :] + y_here[:, lanes] * taps_ref[s, 1:2, :]
                 + y_after[:, lanes] * taps_ref[s, 2:3, :])
            conv_refs[s][0, rows, :] = (jax.nn.silu(z) * scale).astype(BF16)


def _inproj(x, mod, g, w, taps, scales, plain_acts, tm, ctx_row):
    b, n, _ = x.shape
    wn = len(plain_acts) * GROUP_W
    assert w.shape[1] == len(scales) * GROUP_W + wn + GATE_PAD
    assert tm % ROW_GROUP == 0 and n % tm == 0 and wn % 128 == 0
    rows8 = tm // 8
    last8 = n // 8 - 1
    full = lambda a: pl.BlockSpec(a.shape, lambda i, j: (0,) * a.ndim)
    conv_spec = pl.BlockSpec((1, tm, GROUP_W), lambda i, j: (i, j, 0))
    return pl.pallas_call(
        functools.partial(_inproj_kernel, ctx_row=ctx_row, scales=tuple(scales),
                          plain_acts=tuple(plain_acts)),
        grid=(b, n // tm),
        in_specs=[pl.BlockSpec((1, tm, D_MODEL), lambda i, j: (i, j, 0)),
                  pl.BlockSpec((1, 8, D_MODEL), lambda i, j: (i, jnp.maximum(j * rows8 - 1, 0), 0)),
                  pl.BlockSpec((1, 8, D_MODEL), lambda i, j: (i, jnp.minimum((j + 1) * rows8, last8), 0)),
                  full(mod), full(g), full(taps), full(w)],
        out_specs=[conv_spec] * len(scales)
        + [pl.BlockSpec((1, tm, wn), lambda i, j: (i, j, 0)),
           pl.BlockSpec((1, tm, GATE_PAD), lambda i, j: (i, j, 0))],
        out_shape=[jax.ShapeDtypeStruct((b, n, GROUP_W), BF16)] * len(scales)
        + [jax.ShapeDtypeStruct((b, n, wn), F32), jax.ShapeDtypeStruct((b, n, GATE_PAD), F32)],
        compiler_params=_params("arbitrary", "arbitrary"),
        name="inproj",
    )(x, x, x, mod, g, taps, w)


def _mlstm_chunk(per_dir, bias_ref, c_s, n_s, m_s, h_refs):
    with_h = h_refs[0] is not None
    i0 = _iota2((CHUNK, CHUNK), 0)
    i1 = _iota2((CHUNK, CHUNK), 1)

    units = []
    for smp, d in [(smp, d) for smp in range(c_s.shape[0]) for d in range(2)]:
        q_ref, k_ref, v_ref, g_ref = per_dir[d]
        sees_ts = (i1 <= i0) if d == 0 else (i1 >= i0)
        sees_st = (i0 <= i1) if d == 0 else (i0 >= i1)
        tri = jnp.where(sees_ts, 1.0, 0.0).astype(BF16)
        gates = g_ref[smp] + bias_ref[...]
        bcum = _dot_exact01(tri, jax.nn.log_sigmoid(gates))
        for hd in range(HEADS):
            lanes = slice(hd * HEAD_DIM, (hd + 1) * HEAD_DIM)
            c_prev = c_s[smp, d, hd]
            n_prev = n_s[smp, d, hd]
            k = k_ref[smp, :, lanes]
            st = dict(smp=smp, d=d, hd=hd, lanes=lanes, c_prev=c_prev, n_prev=n_prev, k=k,
                      sees_st=sees_st, gates=gates, bcum=bcum, v=v_ref[smp, :, lanes])
            if with_h:
                lhs = jnp.concatenate([k, c_prev.astype(BF16),
                                       jnp.broadcast_to(n_prev, (16, HEAD_DIM)).astype(BF16)], axis=0)
                st["prod"] = _dot_nt(lhs, q_ref[smp, :, lanes])
            units.append(st)

    rows = {}
    for st in units:
        smp, d, hd = st["smp"], st["d"], st["hd"]
        if (smp, d) not in rows:
            rows[smp, d] = (st["gates"].T, st["bcum"].T)
        gates_t, bcum_t = rows[smp, d]
        ci = d * HEADS + hd
        cf = 2 * HEADS + d * HEADS + hd
        last = CHUNK - 1 if d == 0 else 0
        li_row = gates_t[ci:ci + 1, :]
        bc_row = bcum_t[cf:cf + 1, :]
        b_last = bc_row[:, last:last + 1]
        m_prev = m_s[smp, d, hd]
        v_t = st["v"].T
        a_row = b_last - bc_row + li_row
        m_new = jnp.maximum(b_last + m_prev, jnp.max(a_row, axis=-1, keepdims=True))
        w_row = jnp.exp(a_row - m_new)
        lhs = jnp.concatenate([v_t * w_row, jnp.broadcast_to(w_row, (16, CHUNK))], axis=0)
        st["upd"] = _dot(lhs.astype(BF16), st["k"])
        st.update(bc_row=bc_row, m_prev=m_prev, m_new=m_new, v_t=v_t,
                  decay=jnp.exp(b_last + m_prev - m_new),
                  u_col=st["gates"][:, ci:ci + 1] - st["bcum"][:, cf:cf + 1])

    if with_h:
        for st in units:
            prod, bc_row = st["prod"], st["bc_row"]
            g = bc_row + st["m_prev"]
            dmat = jnp.where(st["sees_st"], st["u_col"] + bc_row, -jnp.inf)
            m_t = jnp.maximum(g, jnp.max(dmat, axis=0, keepdims=True))
            inter = jnp.exp(g - m_t)
            s = prod[0:CHUNK] * jnp.exp(dmat - m_t)
            st["pv"] = _dot(st["v_t"].astype(BF16), s.astype(BF16))
            st["num0"] = inter * prod[CHUNK:2 * CHUNK]
            den = inter * prod[2 * CHUNK:2 * CHUNK + 1] + jnp.sum(s, axis=0, keepdims=True)
            st["scale"] = 1.0 / jnp.maximum(jnp.abs(den), jnp.exp(-m_t))

    for st in units:
        smp, d, hd = st["smp"], st["d"], st["hd"]
        if with_h:
            h_refs[d][smp, :, st["lanes"]] = ((st["num0"] + st["pv"]) * st["scale"]).T
        c_s[smp, d, hd] = st["decay"] * st["c_prev"] + st["upd"][0:HEAD_DIM]
        n_s[smp, d, hd] = st["decay"] * st["n_prev"] + st["upd"][HEAD_DIM:HEAD_DIM + 1]
        m_s[smp, d, hd] = st["m_new"]


def _mlstm_kernel(*refs, n_chunks, with_h):
    refs = list(refs)
    take = lambda n: [refs.pop(0) for _ in range(n)]
    per_dir = []
    for _ in range(2):
        q_ref = take(1)[0] if with_h else None
        k_ref, v_ref, g_ref = take(3)
        per_dir.append((q_ref, k_ref, v_ref, g_ref))
    bias_ref = take(1)[0]
    if with_h:
        c0_ref, n0_ref, m0_ref = take(3)
        h_refs = take(2)
        c_out = n_out = m_out = None
    else:
        h_refs = [None, None]
        c_out, n_out, m_out = take(3)
    c_s, n_s, m_s = take(3)

    j = pl.program_id(1)

    @pl.when(j == 0)
    def _():
        if with_h:
            c_s[...] = c0_ref[...]
            n_s[...] = n0_ref[...]
            m_s[...] = m0_ref[...]
        else:
            c_s[...] = jnp.zeros_like(c_s)
            n_s[...] = jnp.zeros_like(n_s)
            m_s[...] = jnp.zeros_like(m_s)

    _mlstm_chunk(per_dir, bias_ref, c_s, n_s, m_s, h_refs)

    if not with_h:
        @pl.when(j == n_chunks - 1)
        def _():
            c_out[...] = c_s[...]
            n_out[...] = n_s[...]
            m_out[...] = m_s[...]


def _mlstm(q, k, p, v_blk, gates, bias, state):
    b, n, _ = k.shape
    n_chunks = n // CHUNK
    with_h = q is not None
    ns = MLSTM_SAMPLES if b % MLSTM_SAMPLES == 0 else 1

    in_specs, args = [], []
    for d in range(2):
        c = (lambda j: j) if d == 0 else (lambda j: n_chunks - 1 - j)
        tok = pl.BlockSpec((ns, CHUNK, GROUP_W), lambda i, j, c=c: (i, c(j), 0))
        if with_h:
            in_specs.append(tok)
            args.append(q)
        in_specs += [tok,
                     pl.BlockSpec((ns, CHUNK, GROUP_W), lambda i, j, c=c: (i, c(j), v_blk)),
                     pl.BlockSpec((ns, CHUNK, GATE_PAD), lambda i, j, c=c: (i, c(j), 0))]
        args += [k, p, gates]
    in_specs.append(pl.BlockSpec((1, GATE_PAD), lambda i, j: (0, 0)))
    args.append(bias)

    c_shape = (2, HEADS, HEAD_DIM, HEAD_DIM)
    v_shape = (2, HEADS, 1, HEAD_DIM)
    c_spec = pl.BlockSpec((ns,) + c_shape, lambda i, j: (i, 0, 0, 0, 0))
    v_spec = pl.BlockSpec((ns,) + v_shape, lambda i, j: (i, 0, 0, 0, 0))
    if with_h:
        in_specs += [c_spec, v_spec, v_spec]
        args += list(state)
        out_specs = [pl.BlockSpec((ns, CHUNK, GROUP_W), lambda i, j: (i, j, 0)),
                     pl.BlockSpec((ns, CHUNK, GROUP_W), lambda i, j: (i, n_chunks - 1 - j, 0))]
        out_shape = [jax.ShapeDtypeStruct((b, n, GROUP_W), F32)] * 2
    else:
        out_specs = [c_spec, v_spec, v_spec]
        out_shape = [jax.ShapeDtypeStruct((b,) + c_shape, F32),
                     jax.ShapeDtypeStruct((b,) + v_shape, F32),
                     jax.ShapeDtypeStruct((b,) + v_shape, F32)]
    return pl.pallas_call(
        functools.partial(_mlstm_kernel, n_chunks=n_chunks, with_h=with_h),
        grid=(b // ns, n_chunks),
        in_specs=in_specs,
        out_specs=out_specs,
        out_shape=out_shape,
        scratch_shapes=[pltpu.VMEM((ns,) + c_shape, F32), pltpu.VMEM((ns,) + v_shape, F32),
                        pltpu.VMEM((ns,) + v_shape, F32)],
        compiler_params=_params("arbitrary", "arbitrary"),
        name="mlstm" if with_h else "mlstm_ctx_state",
    )(*args)


def _layer_norm(x):
    mu = jnp.mean(x, axis=-1, keepdims=True)
    var = jnp.mean(jnp.square(x - mu), axis=-1, keepdims=True)
    return (x - mu) * lax.rsqrt(var + EPS)


def _postmix_kernel(u_ref, vg_ref, o_ref, hf_ref, hb_ref, x_ref, mod_ref, ws_ref, bs_ref, ng_ref,
                    wout_ref, fg_ref, wr_ref, x1_ref, h2_ref, aff_ref, ycat_s):
    tm = x_ref.shape[1]
    b = pl.program_id(0)
    mod = lambda k: mod_ref[pl.ds(b, 1), k * D_MODEL:(k + 1) * D_MODEL]
    wr_hi, wr_lo = _split2(wr_ref[...])
    expert_lane = _iota2((ROW_GROUP, 128), 1) < N_EXPERTS

    for r in range(tm // ROW_GROUP):
        rows = slice(r * ROW_GROUP, (r + 1) * ROW_GROUP)

        for c in range(r * ROW_GROUP // CHUNK, (r + 1) * ROW_GROUP // CHUNK):
            crows = slice(c * CHUNK, (c + 1) * CHUNK)
            u = u_ref[0, crows, :]
            v = vg_ref[0, crows, :].astype(BF16)
            for hd in range(HEADS):
                lanes = slice(hd * HEAD_DIM, (hd + 1) * HEAD_DIM)
                s = _dot(ws_ref[hd], v[:, lanes]) + bs_ref[hd]
                ycat_s[crows, lanes] = (u[:, lanes] * s).astype(BF16)

        hsum = hf_ref[0, rows, :] + hb_ref[0, rows, :]
        o = o_ref[0, rows, :]
        for hd in range(HEADS):
            lanes = slice(hd * HEAD_DIM, (hd + 1) * HEAD_DIM)
            hn = _layer_norm(hsum[:, lanes]) * ng_ref[:, lanes]
            ycat_s[rows, GROUP_W + hd * HEAD_DIM:GROUP_W + (hd + 1) * HEAD_DIM] = (
                o[:, lanes] * hn).astype(BF16)

        y = _dot(ycat_s[rows, :], wout_ref[...])
        x1 = x_ref[0, rows, :] + mod(2) * y
        x1_ref[0, rows, :] = x1

        n2 = x1 * lax.rsqrt(jnp.mean(x1 * x1, axis=-1, keepdims=True) + EPS) * fg_ref[...]
        h2 = n2 * (1.0 + mod(4)) + mod(3)
        h2_hi, h2_lo = _split2(h2)
        h2_ref[0, rows, :] = h2_hi

        logits = _dot(h2_hi, wr_hi) + (_dot(h2_lo, wr_hi) + _dot(h2_hi, wr_lo))
        logits = jnp.where(expert_lane, logits, -jnp.inf)
        e = jnp.exp(logits - jnp.max(logits, axis=-1, keepdims=True))
        aff = e / jnp.sum(e, axis=-1, keepdims=True)
        aff_ref[0, :, rows] = aff.T[0:N_EXPERTS, :]


def _postmix(p, hf, hb, x, mod, ws, bs, ng, wout, fg, wr, tm):
    b, n, _ = x.shape
    tok = lambda blk: pl.BlockSpec((1, tm, GROUP_W), lambda i, j: (i, j, blk))
    full = lambda a: pl.BlockSpec(a.shape, lambda i, j: (0,) * a.ndim)
    return pl.pallas_call(
        _postmix_kernel,
        grid=(b, n // tm),
        in_specs=[tok(P_U), tok(P_VG), tok(P_O), tok(0), tok(0),
                  pl.BlockSpec((1, tm, D_MODEL), lambda i, j: (i, j, 0)),
                  full(mod), full(ws), full(bs), full(ng), full(wout), full(fg), full(wr)],
        out_specs=[pl.BlockSpec((1, tm, D_MODEL), lambda i, j: (i, j, 0)),
                   pl.BlockSpec((1, tm, D_MODEL), lambda i, j: (i, j, 0)),
                   pl.BlockSpec((1, N_EXPERTS, tm), lambda i, j: (i, 0, j))],
        out_shape=[jax.ShapeDtypeStruct((b, n, D_MODEL), F32),
                   jax.ShapeDtypeStruct((b, n, D_MODEL), BF16),
                   jax.ShapeDtypeStruct((b, N_EXPERTS, n), F32)],
        scratch_shapes=[pltpu.VMEM((tm, D_MODEL), BF16)],
        compiler_params=_params("arbitrary", "arbitrary"),
        name="postmix",
    )(p, p, p, hf, hb, x, mod, ws, bs, ng, wout, fg, wr)


def _cumsum_lanes(x, upper):
    carry = jnp.zeros((x.shape[0], 1), F32)
    outs, before = [], []
    for j in range(x.shape[1] // 128):
        before.append(carry)
        c = _dot(x[:, j * 128:(j + 1) * 128].astype(BF16), upper) + carry
        outs.append(c)
        carry = c[:, 127:128]
    return jnp.concatenate(outs, axis=1), before


def _rows_to_lanes(x, fill):
    pad = jnp.full((128 - x.shape[0], 128), fill, F32)
    return jnp.concatenate([x, pad], axis=0).T


def _route_kernel(aff_ref, slot_ref, w_ref, slot_t_ref, first_t_ref, first_ref, *, capacity, n_e):
    aff = aff_ref[...]
    cap = float(capacity)
    thr_bits = jnp.zeros((aff.shape[0], 1), jnp.int32)
    for bit in range(30, -1, -1):
        cand = thr_bits | (1 << bit)
        cnt = jnp.sum(jnp.where(aff >= pltpu.bitcast(cand, F32), 1.0, 0.0), axis=-1, keepdims=True)
        thr_bits = jnp.where(cnt >= cap, cand, thr_bits)
    thr = pltpu.bitcast(thr_bits, F32)
    upper = jnp.where(_iota2((128, 128), 0) <= _iota2((128, 128), 1), 1.0, 0.0).astype(BF16)
    above = jnp.where(aff > thr, 1.0, 0.0)
    tied = jnp.where(aff == thr, 1.0, 0.0)
    need = cap - jnp.sum(above, axis=-1, keepdims=True)
    sel = above + tied * jnp.where(_cumsum_lanes(tied, upper)[0] <= need, 1.0, 0.0)
    count, before = _cumsum_lanes(sel, upper)
    slot = jnp.where(sel > 0.0, count - 1.0, -1.0)
    slot_ref[...] = slot
    w_ref[...] = jnp.where(sel > 0.0, aff, 0.0)

    n_blocks = len(before)
    lane = _iota2((1, 128), 1)
    first = jnp.zeros((aff.shape[0], 128), F32)
    for j in range(n_blocks):
        first = jnp.where(lane == j, before[j], first)
    first_ref[...] = first
    for smp in range(aff.shape[0] // n_e):
        rows = slice(smp * n_e, (smp + 1) * n_e)
        for j in range(n_blocks):
            slot_t_ref[smp, j * 128:(j + 1) * 128, :] = _rows_to_lanes(
                slot[rows, j * 128:(j + 1) * 128], -1.0)
        first_t_ref[smp] = _rows_to_lanes(first[rows, :], 0.0)[0:n_blocks, :]


def _route(aff_t, capacity):
    b, e, n = aff_t.shape
    assert n // 128 <= 128 and e <= 128 and e % 8 == 0
    rows = pl.BlockSpec((b * e, n), lambda i: (0, 0))
    slot, w, slot_t, first_t, first = pl.pallas_call(
        functools.partial(_route_kernel, capacity=capacity, n_e=e),
        grid=(1,),
        in_specs=[rows],
        out_specs=[rows, rows,
                   pl.BlockSpec((b, n, 128), lambda i: (0, 0, 0)),
                   pl.BlockSpec((b, n // 128, 128), lambda i: (0, 0, 0)),
                   pl.BlockSpec((b * e, 128), lambda i: (0, 0))],
        out_shape=[jax.ShapeDtypeStruct((b * e, n), F32)] * 2
        + [jax.ShapeDtypeStruct((b, n, 128), F32), jax.ShapeDtypeStruct((b, n // 128, 128), F32),
           jax.ShapeDtypeStruct((b * e, 128), F32)],
        compiler_params=_params("arbitrary"),
        name="route",
    )(aff_t.reshape(b * e, n))
    return slot.reshape(b, e, n), w.reshape(b, e, n), slot_t, first_t, first.reshape(b, e, 128)


GATHER_TILE = 256
GATHER_ROWS = (64, 128)
GATHER_GROUP = 4


def _gather_kernel(before_ref, fits_ref, h2_ref, slot_ref, w_ref, xe_ref, gate_ref, xe_s, gate_s):
    b = pl.program_id(0)
    g = pl.program_id(1)
    n_g, cap = xe_ref.shape[1], xe_ref.shape[2]
    n = h2_ref.shape[1]
    n_tiles = n // GATHER_TILE

    def windowed(n_rows):
        xe_s[:, 0:16, :] = jnp.zeros((n_g, 16, D_MODEL), BF16)
        gate_s[...] = jnp.zeros_like(gate_s)
        row_id = _iota2((n_rows, 1), 0).astype(F32)
        for j in range(n_tiles):
            starts, onehots = [], []
            for e in range(n_g):
                start = pl.multiple_of((before_ref[b, g * n_g + e, j] // 16) * 16, 16)
                hit = (slot_ref[0, e, j:j + 1, :] - start.astype(F32)) == row_id
                onehots.append(jnp.where(hit, 1.0, 0.0).astype(BF16))
                gate_s[e, pl.ds(start, n_rows), :] += jnp.sum(
                    jnp.where(hit, w_ref[0, e, j:j + 1, :], 0.0), axis=1, keepdims=True)
                starts.append(start)
            rows = _dot(jnp.concatenate(onehots, axis=0),
                        h2_ref[0, j * GATHER_TILE:(j + 1) * GATHER_TILE, :]).astype(BF16)
            for e, start in enumerate(starts):
                r0 = e * n_rows
                xe_s[e, pl.ds(start, 16), :] += rows[r0:r0 + 16]
                xe_s[e, pl.ds(start + 16, n_rows - 16), :] = rows[r0 + 16:r0 + n_rows]
        xe_ref[0] = xe_s[:, 0:cap, :]
        gate_ref[0] = gate_s[:, 0:cap, :]

    for level, n_rows in enumerate(GATHER_ROWS):
        pl.when(fits_ref[b, g] == level)(functools.partial(windowed, n_rows))

    @pl.when(fits_ref[b, g] == len(GATHER_ROWS))
    def _():
        slot_id = _iota2((cap, 1), 0).astype(F32)
        for e in range(n_g):
            xe = jnp.zeros((cap, D_MODEL), F32)
            gate = jnp.zeros((cap, 1), F32)
            for j in range(n_tiles):
                hit = slot_ref[0, e, j:j + 1, :] == slot_id
                xe = xe + _dot(jnp.where(hit, 1.0, 0.0).astype(BF16),
                               h2_ref[0, j * GATHER_TILE:(j + 1) * GATHER_TILE, :])
                gate = gate + jnp.sum(jnp.where(hit, w_ref[0, e, j:j + 1, :], 0.0), axis=1,
                                      keepdims=True)
            xe_ref[0, e] = xe.astype(BF16)
            gate_ref[0, e] = gate


def _gather(h2, slot, w, first, capacity):
    b, n, _ = h2.shape
    n_e = slot.shape[1]
    n_tiles = n // GATHER_TILE
    assert n_e % GATHER_GROUP == 0 and GATHER_TILE % 128 == 0 and capacity % 16 == 0
    assert all(r % 16 == 0 and r >= 32 for r in GATHER_ROWS) and list(GATHER_ROWS) == sorted(GATHER_ROWS)
    before = first[:, :, 0:n // 128:GATHER_TILE // 128].astype(jnp.int32)
    after = jnp.concatenate([before[:, :, 1:], jnp.full((b, n_e, 1), capacity, jnp.int32)], axis=2)
    extent = jnp.max(after - (before // 16) * 16, axis=2)
    extent = jnp.max(extent.reshape(b, n_e // GATHER_GROUP, GATHER_GROUP), axis=2)
    fits = sum((extent >= n_rows).astype(jnp.int32) for n_rows in GATHER_ROWS)
    tiles = lambda a: a.reshape(b, n_e, n_tiles, GATHER_TILE)
    rows = pl.BlockSpec((1, GATHER_GROUP, n_tiles, GATHER_TILE), lambda i, g, *_: (i, g, 0, 0))
    return pl.pallas_call(
        _gather_kernel,
        grid_spec=pltpu.PrefetchScalarGridSpec(
            num_scalar_prefetch=2,
            grid=(b, n_e // GATHER_GROUP),
            in_specs=[pl.BlockSpec((1, n, D_MODEL), lambda i, g, *_: (i, 0, 0)), rows, rows],
            out_specs=[pl.BlockSpec((1, GATHER_GROUP, capacity, D_MODEL), lambda i, g, *_: (i, g, 0, 0)),
                       pl.BlockSpec((1, GATHER_GROUP, capacity, 1), lambda i, g, *_: (i, g, 0, 0))],
            scratch_shapes=[pltpu.VMEM((GATHER_GROUP, capacity + max(GATHER_ROWS), D_MODEL), BF16),
                            pltpu.VMEM((GATHER_GROUP, capacity + max(GATHER_ROWS), 1), F32)]),
        out_shape=[jax.ShapeDtypeStruct((b, n_e, capacity, D_MODEL), BF16),
                   jax.ShapeDtypeStruct((b, n_e, capacity, 1), F32)],
        compiler_params=_params("arbitrary", "arbitrary"),
        name="gather",
    )(before, fits, h2, tiles(slot), tiles(w))


def _ffn_kernel(xe_ref, gate_ref, wg_ref, wu_ref, wd_ref, ye_ref, acc_s):
    ft = pl.program_id(1)
    last = pl.num_programs(1) - 1

    def step(first, final):
        wg = wg_ref[0].astype(BF16)
        wu = wu_ref[0].astype(BF16)
        wd = wd_ref[0].astype(BF16)
        for i in range(xe_ref.shape[0]):
            xe = xe_ref[i, 0]
            act = jax.nn.silu(_dot(xe, wg)) * _dot(xe, wu)
            part = _dot(act.astype(BF16), wd)
            total = part if first else acc_s[i] + part
            if final:
                ye_ref[i] = (total * gate_ref[i, 0]).astype(BF16)
            else:
                acc_s[i] = total

    pl.when(ft == 0)(lambda: step(True, False))
    pl.when(jnp.logical_and(ft > 0, ft < last))(lambda: step(False, False))
    pl.when(ft == last)(lambda: step(False, True))


def _ffn(xe, gate, wg, wu, wd, f_tile):
    b, n_e, cap, _ = xe.shape
    assert D_EXPERT // f_tile >= 2
    return pl.pallas_call(
        _ffn_kernel,
        grid=(n_e, D_EXPERT // f_tile),
        in_specs=[pl.BlockSpec((b, 1, cap, D_MODEL), lambda e, f: (0, e, 0, 0)),
                  pl.BlockSpec((b, 1, cap, 1), lambda e, f: (0, e, 0, 0)),
                  pl.BlockSpec((1, D_MODEL, f_tile), lambda e, f: (e, 0, f)),
                  pl.BlockSpec((1, D_MODEL, f_tile), lambda e, f: (e, 0, f)),
                  pl.BlockSpec((1, f_tile, D_MODEL), lambda e, f: (e, f, 0))],
        out_specs=pl.BlockSpec((b, cap, D_MODEL), lambda e, f: (0, e, 0)),
        out_shape=jax.ShapeDtypeStruct((b, n_e * cap, D_MODEL), BF16),
        scratch_shapes=[pltpu.VMEM((b, cap, D_MODEL), F32)],
        compiler_params=_params("arbitrary", "arbitrary"),
        name="ffn",
    )(xe, gate, wg, wu, wd)


COMBINE_WIDE = CHUNK + 16
COMBINE_NARROW = 48


def _combine_kernel(first_ref, narrow_ref, slot_t_ref, first_t_ref, ye_ref, x1_ref, mod_ref, fg_ref,
                    o_ref, acc_s, *, capacity):
    b = pl.program_id(0)
    j = pl.program_id(1)
    n_sub = o_ref.shape[1] // 128
    lane = _iota2((1, 128), 1).astype(F32)

    def scatter(window):
        k_total = N_EXPERTS * window
        for sb in range(n_sub):
            blk = j * n_sub + sb
            slot_t = slot_t_ref[0, sb * 128:(sb + 1) * 128, :]
            start_row = jnp.minimum(jnp.floor(first_t_ref[0, pl.ds(blk, 1), :] * (1.0 / 16.0)) * 16.0,
                                    float(capacity - window))
            k_pos = jnp.where(slot_t >= 0.0, slot_t - start_row + lane * float(window), -1.0)
            cols = []
            for c in range(k_total // 128):
                k_lane = lane + float(128 * c)
                hit = jnp.zeros((128, 128), F32)
                for e in range((128 * c) // window, (128 * c + 127) // window + 1):
                    hit = jnp.where(k_pos[:, e:e + 1] == k_lane, 1.0, hit)
                cols.append(hit.astype(BF16))
            onehot = jnp.concatenate(cols, axis=1)
            rows = []
            for e in range(N_EXPERTS):
                start = jnp.minimum((first_ref[b, blk, e] // 16) * 16, capacity - window)
                rows.append(ye_ref[0, pl.ds(pl.multiple_of(e * capacity + start, 16), window), :])
            acc_s[sb * 128:(sb + 1) * 128, :] = _dot(onehot, jnp.concatenate(rows, axis=0))

    pl.when(narrow_ref[b, j] != 0)(lambda: scatter(COMBINE_NARROW))
    pl.when(narrow_ref[b, j] == 0)(lambda: scatter(COMBINE_WIDE))

    g2 = mod_ref[pl.ds(b, 1), 5 * D_MODEL:6 * D_MODEL]
    x2 = x1_ref[0] + g2 * acc_s[...]
    o_ref[0] = x2 * lax.rsqrt(jnp.mean(x2 * x2, axis=-1, keepdims=True) + EPS) * fg_ref[...]


def _combine(slot_t, first_t, ye, x1, mod, fg, capacity, tm):
    b, n, _ = x1.shape
    n_blocks = n // 128
    for window in (COMBINE_WIDE, COMBINE_NARROW):
        assert (N_EXPERTS * window) % 128 == 0 and window % 16 == 0 and window <= capacity
    assert COMBINE_WIDE >= 128 + 15 and capacity % 16 == 0
    first = first_t[:, :, :N_EXPERTS].astype(jnp.int32)
    after = jnp.concatenate([first[:, 1:], jnp.full((b, 1, N_EXPERTS), capacity, jnp.int32)], axis=1)
    narrow = jnp.all(after - (first // 16) * 16 <= COMBINE_NARROW, axis=2)
    narrow = jnp.all(narrow.reshape(b, n // tm, tm // 128), axis=2).astype(jnp.int32)
    tok = pl.BlockSpec((1, tm, D_MODEL), lambda i, j, *_: (i, j, 0))
    return pl.pallas_call(
        functools.partial(_combine_kernel, capacity=capacity),
        grid_spec=pltpu.PrefetchScalarGridSpec(
            num_scalar_prefetch=2,
            grid=(b, n // tm),
            in_specs=[pl.BlockSpec((1, tm, 128), lambda i, j, *_: (i, j, 0)),
                      pl.BlockSpec((1, n_blocks, 128), lambda i, j, *_: (i, 0, 0)),
                      pl.BlockSpec((1, N_EXPERTS * capacity, D_MODEL), lambda i, j, *_: (i, 0, 0)),
                      tok,
                      pl.BlockSpec(mod.shape, lambda i, j, *_: (0, 0)),
                      pl.BlockSpec((1, D_MODEL), lambda i, j, *_: (0, 0))],
            out_specs=tok,
            scratch_shapes=[pltpu.VMEM((tm, D_MODEL), F32)]),
        out_shape=jax.ShapeDtypeStruct((b, n, D_MODEL), F32),
        compiler_params=_params("arbitrary", "arbitrary"),
        name="combine",
    )(first, narrow, slot_t, first_t, ye, x1, mod, fg)


def kernel(x, c, ctx, c_ctx, w_mod, b_mod, norm_mix_g, w_in, conv_q, conv_k, b_igate, b_fgate,
           gmlp_ws, gmlp_bs, mlstm_norm_g, w_out, norm_ffn_g, w_router, w_gate_e, w_up_e,
           w_down_e, final_g):
    depth = w_mod.shape[0]
    assert depth == 1, "the context stream is only carried as mLSTM states (single layer)"
    batch, seq, _ = x.shape
    assert seq % GRID_W == 0 and seq % CHUNK == 0 and batch + 1 <= MOD_ROWS
    capacity = EC_FACTOR * seq // N_EXPERTS
    ctx_row = batch
    l = 0

    cond = jnp.concatenate([c, c_ctx[None], jnp.zeros((MOD_ROWS - batch - 1, D_MODEL), F32)], axis=0)
    mod = _adaln(cond, w_mod[l], b_mod[l][None])

    row = lambda a: a[None]
    w_gates = jnp.pad(w_in[l][:, MAIN_W:], ((0, 0), (0, GATE_PAD - N_GATES)))
    cols = lambda *blks: jnp.concatenate(
        [w_in[l][:, k * GROUP_W:(k + 1) * GROUP_W] for k in blks] + [w_gates], axis=1).astype(BF16)
    gate_bias = jnp.pad(jnp.concatenate([b_igate[l].reshape(-1), b_fgate[l].reshape(-1)]),
                        (0, GATE_PAD - N_GATES))[None]
    k_scale = HEAD_DIM ** -0.5

    identity = lambda a: a
    k_c, v_c, gates_c = _inproj(ctx, mod, row(norm_mix_g[l]), cols(K_BLK, V_BLK), conv_k[l][None],
                                (k_scale,), (identity,), tm=ctx.shape[1], ctx_row=ctx_row)
    state = _mlstm(None, k_c, v_c, 0, gates_c, gate_bias, None)

    q_l, k_l, p, gates = _inproj(x, mod, row(norm_mix_g[l]),
                                 cols(Q_BLK, K_BLK, U_BLK, VG_BLK, O_BLK, V_BLK),
                                 jnp.stack([conv_q[l], conv_k[l]]), (1.0, k_scale),
                                 (jax.nn.gelu, lambda a: _layer_norm(jax.nn.gelu(a)), jax.nn.sigmoid, identity),
                                 tm=512, ctx_row=None)
    h_f, h_b = _mlstm(q_l, k_l, p, P_V, gates, gate_bias, state)

    x1, h2, aff_t = _postmix(p, h_f, h_b, x, mod, gmlp_ws[l].astype(BF16), gmlp_bs[l][:, :, None],
                             row(mlstm_norm_g[l]), w_out[l].astype(BF16), row(norm_ffn_g[l]),
                             jnp.pad(w_router[l], ((0, 0), (0, 128 - N_EXPERTS))), tm=512)

    slot, gate_w, slot_t, first_t, first = _route(aff_t, capacity)

    xe, gate = _gather(h2, slot, gate_w, first, capacity)
    ye = _ffn(xe, gate, w_gate_e[l], w_up_e[l], w_down_e[l], f_tile=512)
    return _combine(slot_t, first_t, ye, x1, mod, row(final_g), capacity, tm=512)
```

```python
import functools

import jax
import jax.numpy as jnp
from jax import lax
from jax.experimental import pallas as pl
from jax.experimental.pallas import tpu as pltpu

F32 = jnp.float32
BF16 = jnp.bfloat16

D_MODEL = 1024
GRID_W = 64
CHUNK = 128
HEADS = 4
GROUP_W = D_MODEL // 2
HEAD_DIM = GROUP_W // HEADS
N_EXPERTS = 16
EC_FACTOR = 2
D_EXPERT = 2 * D_MODEL
EPS = 1e-6
N_GATES = 4 * HEADS
GATE_PAD = 128
MOD_ROWS = 8

U_BLK, VG_BLK, Q_BLK, O_BLK, K_BLK, V_BLK = 0, 1, 2, 3, 4, 5
MAIN_W = 6 * GROUP_W
P_U, P_VG, P_O, P_V = 0, 1, 2, 3

VMEM_LIMIT = 56 * 1024 * 1024
ROW_GROUP = 2 * CHUNK
MLSTM_SAMPLES = 4


def _params(*sem):
    return pltpu.CompilerParams(dimension_semantics=sem, vmem_limit_bytes=VMEM_LIMIT)


def _dot(a, b):
    return jnp.dot(a, b, preferred_element_type=F32)


def _dot_nt(a, b):
    return lax.dot_general(a, b, (((1,), (1,)), ((), ())), preferred_element_type=F32)


def _split2(a):
    hi = a.astype(BF16)
    lo = (a - hi.astype(F32)).astype(BF16)
    return hi, lo


def _dot3(a, b):
    ah, al = _split2(a)
    bh, bl = _split2(b)
    return _dot(ah, bh) + (_dot(al, bh) + _dot(ah, bl))


def _dot_exact01(tri, x):
    x1 = x.astype(BF16)
    r1 = x - x1.astype(F32)
    x2 = r1.astype(BF16)
    x3 = (r1 - x2.astype(F32)).astype(BF16)
    return _dot(tri, x1) + (_dot(tri, x2) + _dot(tri, x3))


def _iota2(shape, dim):
    return lax.broadcasted_iota(jnp.int32, shape, dim)


def _cols_to_rows(col):
    eye = _iota2((128, 128), 0) == _iota2((128, 128), 1)
    return jnp.concatenate(
        [jnp.sum(jnp.where(eye, col[r * 128:(r + 1) * 128, :], 0.0), axis=0, keepdims=True)
         for r in range(col.shape[0] // 128)], axis=0)


def _rows_to_cols(rows):
    eye = _iota2((128, 128), 0) == _iota2((128, 128), 1)
    return jnp.concatenate(
        [jnp.sum(jnp.where(eye, rows[r:r + 1, :], 0.0), axis=1, keepdims=True)
         for r in range(rows.shape[0])], axis=0)


def _adaln_kernel(cond_ref, w_ref, b_ref, o_ref):
    o_ref[...] = _dot3(jax.nn.silu(cond_ref[...]), w_ref[...]) + b_ref[...]


def _adaln(cond, w, b):
    n_out = w.shape[1]
    tn = D_MODEL
    assert n_out % tn == 0
    return pl.pallas_call(
        _adaln_kernel,
        grid=(n_out // tn,),
        in_specs=[pl.BlockSpec((MOD_ROWS, D_MODEL), lambda j: (0, 0)),
                  pl.BlockSpec((D_MODEL, tn), lambda j: (0, j)),
                  pl.BlockSpec((1, tn), lambda j: (0, j))],
        out_specs=pl.BlockSpec((MOD_ROWS, tn), lambda j: (0, j)),
        out_shape=jax.ShapeDtypeStruct((MOD_ROWS, n_out), F32),
        compiler_params=_params("arbitrary"),
        name="adaln",
    )(cond, w, b)


def _inproj_kernel(x_ref, xprev_ref, xnext_ref, mod_ref, g_ref, taps_ref, w_ref, *outs, ctx_row, scales,
                   plain_acts):
    conv_refs, (p_ref, gate_ref) = outs[:len(scales)], outs[len(scales):]
    row = pl.program_id(0) if ctx_row is None else ctx_row
    j = pl.program_id(1)
    sh = mod_ref[pl.ds(row, 1), 0:D_MODEL]
    sc = mod_ref[pl.ds(row, 1), D_MODEL:2 * D_MODEL]

    def modulated(x):
        y = x * lax.rsqrt(jnp.mean(x * x, axis=-1, keepdims=True) + EPS) * g_ref[...]
        return y * (1.0 + sc) + sh

    n_groups = x_ref.shape[1] // ROW_GROUP
    before = modulated(xprev_ref[0]) * (j > 0).astype(F32)
    for r in range(n_groups):
        rows = slice(r * ROW_GROUP, (r + 1) * ROW_GROUP)
        h = modulated(x_ref[0, rows, :])
        if r == n_groups - 1:
            after = modulated(xnext_ref[0]) * (j < pl.num_programs(1) - 1).astype(F32)
        else:
            after = modulated(x_ref[0, (r + 1) * ROW_GROUP:(r + 1) * ROW_GROUP + 8, :])
        ext = jnp.concatenate([before, h, after], axis=0).astype(BF16)
        before = h[ROW_GROUP - 8:, :]
        y = _dot(ext, w_ref[...])
        n_ext, n_conv = ext.shape[0], len(scales) * GROUP_W
        y_here = y[8:8 + ROW_GROUP]
        for k, act in enumerate(plain_acts):
            lanes = slice(n_conv + k * GROUP_W, n_conv + (k + 1) * GROUP_W)
            p_ref[0, rows, k * GROUP_W:(k + 1) * GROUP_W] = act(y_here[:, lanes])
        gate_ref[0, rows, :] = y_here[:, n_conv + p_ref.shape[2]:]
        y_conv = y[:, 0:n_conv]
        y_before = pltpu.roll(y_conv, 1, axis=0)[8:8 + ROW_GROUP]
        y_after = pltpu.roll(y_conv, n_ext - 1, axis=0)[8:8 + ROW_GROUP]
        for s, scale in enumerate(scales):
            lanes = slice(s * GROUP_W, (s + 1) * GROUP_W)
            z = (y_before[:, lanes] * taps_ref[s, 0:1, :] + y_here[:, lanes] * taps_ref[s, 1:2, :]
                 + y_after[:, lanes] * taps_ref[s, 2:3, :])
            conv_refs[s][0, rows, :] = (jax.nn.silu(z) * scale).astype(BF16)


def _inproj(x, mod, g, w, taps, scales, plain_acts, tm, ctx_row):
    b, n, _ = x.shape
    wn = len(plain_acts) * GROUP_W
    assert w.shape[1] == len(scales) * GROUP_W + wn + GATE_PAD
    assert tm % ROW_GROUP == 0 and n % tm == 0 and wn % 128 == 0
    rows8 = tm // 8
    last8 = n // 8 - 1
    full = lambda a: pl.BlockSpec(a.shape, lambda i, j: (0,) * a.ndim)
    conv_spec = pl.BlockSpec((1, tm, GROUP_W), lambda i, j: (i, j, 0))
    return pl.pallas_call(
        functools.partial(_inproj_kernel, ctx_row=ctx_row, scales=tuple(scales),
                          plain_acts=tuple(plain_acts)),
        grid=(b, n // tm),
        in_specs=[pl.BlockSpec((1, tm, D_MODEL), lambda i, j: (i, j, 0)),
                  pl.BlockSpec((1, 8, D_MODEL), lambda i, j: (i, jnp.maximum(j * rows8 - 1, 0), 0)),
                  pl.BlockSpec((1, 8, D_MODEL), lambda i, j: (i, jnp.minimum((j + 1) * rows8, last8), 0)),
                  full(mod), full(g), full(taps), full(w)],
        out_specs=[conv_spec] * len(scales)
        + [pl.BlockSpec((1, tm, wn), lambda i, j: (i, j, 0)),
           pl.BlockSpec((1, tm, GATE_PAD), lambda i, j: (i, j, 0))],
        out_shape=[jax.ShapeDtypeStruct((b, n, GROUP_W), BF16)] * len(scales)
        + [jax.ShapeDtypeStruct((b, n, wn), F32), jax.ShapeDtypeStruct((b, n, GATE_PAD), F32)],
        compiler_params=_params("arbitrary", "arbitrary"),
        name="inproj",
    )(x, x, x, mod, g, taps, w)


def _mlstm_chunk(per_dir, bias_ref, c_s, n_s, m_s, h_refs):
    with_h = h_refs[0] is not None
    i0 = _iota2((CHUNK, CHUNK), 0)
    i1 = _iota2((CHUNK, CHUNK), 1)

    units = []
    for smp, d in [(smp, d) for smp in range(c_s.shape[0]) for d in range(2)]:
        q_ref, k_ref, v_ref, g_ref = per_dir[d]
        sees_ts = (i1 <= i0) if d == 0 else (i1 >= i0)
        sees_st = (i0 <= i1) if d == 0 else (i0 >= i1)
        tri = jnp.where(sees_ts, 1.0, 0.0).astype(BF16)
        gates = g_ref[smp] + bias_ref[...]
        bcum = _dot_exact01(tri, jax.nn.log_sigmoid(gates))
        for hd in range(HEADS):
            lanes = slice(hd * HEAD_DIM, (hd + 1) * HEAD_DIM)
            c_prev = c_s[smp, d, hd]
            n_prev = n_s[smp, d, hd]
            k = k_ref[smp, :, lanes]
            st = dict(smp=smp, d=d, hd=hd, lanes=lanes, c_prev=c_prev, n_prev=n_prev, k=k,
                      sees_st=sees_st, gates=gates, bcum=bcum, v=v_ref[smp, :, lanes])
            if with_h:
                lhs = jnp.concatenate([k, c_prev.astype(BF16),
                                       jnp.broadcast_to(n_prev, (16, HEAD_DIM)).astype(BF16)], axis=0)
                st["prod"] = _dot_nt(lhs, q_ref[smp, :, lanes])
            units.append(st)

    rows = {}
    for st in units:
        smp, d, hd = st["smp"], st["d"], st["hd"]
        if (smp, d) not in rows:
            rows[smp, d] = (st["gates"].T, st["bcum"].T)
        gates_t, bcum_t = rows[smp, d]
        ci = d * HEADS + hd
        cf = 2 * HEADS + d * HEADS + hd
        last = CHUNK - 1 if d == 0 else 0
        li_row = gates_t[ci:ci + 1, :]
        bc_row = bcum_t[cf:cf + 1, :]
        b_last = bc_row[:, last:last + 1]
        m_prev = m_s[smp, d, hd]
        v_t = st["v"].T
        a_row = b_last - bc_row + li_row
        m_new = jnp.maximum(b_last + m_prev, jnp.max(a_row, axis=-1, keepdims=True))
        w_row = jnp.exp(a_row - m_new)
        lhs = jnp.concatenate([v_t * w_row, jnp.broadcast_to(w_row, (16, CHUNK))], axis=0)
        st["upd"] = _dot(lhs.astype(BF16), st["k"])
        st.update(bc_row=bc_row, m_prev=m_prev, m_new=m_new, v_t=v_t,
                  decay=jnp.exp(b_last + m_prev - m_new),
                  u_col=st["gates"][:, ci:ci + 1] - st["bcum"][:, cf:cf + 1])

    if with_h:
        for st in units:
            prod, bc_row = st["prod"], st["bc_row"]
            g = bc_row + st["m_prev"]
            dmat = jnp.where(st["sees_st"], st["u_col"] + bc_row, -jnp.inf)
            m_t = jnp.maximum(g, jnp.max(dmat, axis=0, keepdims=True))
            inter = jnp.exp(g - m_t)
            s = prod[0:CHUNK] * jnp.exp(dmat - m_t)
            st["pv"] = _dot(st["v_t"].astype(BF16), s.astype(BF16))
            st["num0"] = inter * prod[CHUNK:2 * CHUNK]
            den = inter * prod[2 * CHUNK:2 * CHUNK + 1] + jnp.sum(s, axis=0, keepdims=True)
            st["scale"] = 1.0 / jnp.maximum(jnp.abs(den), jnp.exp(-m_t))

    for st in units:
        smp, d, hd = st["smp"], st["d"], st["hd"]
        if with_h:
            h_refs[d][smp, :, st["lanes"]] = ((st["num0"] + st["pv"]) * st["scale"]).T
        c_s[smp, d, hd] = st["decay"] * st["c_prev"] + st["upd"][0:HEAD_DIM]
        n_s[smp, d, hd] = st["decay"] * st["n_prev"] + st["upd"][HEAD_DIM:HEAD_DIM + 1]
        m_s[smp, d, hd] = st["m_new"]


def _mlstm_kernel(*refs, n_chunks, with_h):
    refs = list(refs)
    take = lambda n: [refs.pop(0) for _ in range(n)]
    per_dir = []
    for _ in range(2):
        q_ref = take(1)[0] if with_h else None
        k_ref, v_ref, g_ref = take(3)
        per_dir.append((q_ref, k_ref, v_ref, g_ref))
    bias_ref = take(1)[0]
    if with_h:
        c0_ref, n0_ref, m0_ref = take(3)
        h_refs = take(2)
        c_out = n_out = m_out = None
    else:
        h_refs = [None, None]
        c_out, n_out, m_out = take(3)
    c_s, n_s, m_s = take(3)

    j = pl.program_id(1)

    @pl.when(j == 0)
    def _():
        if with_h:
            c_s[...] = c0_ref[...]
            n_s[...] = n0_ref[...]
            m_s[...] = m0_ref[...]
        else:
            c_s[...] = jnp.zeros_like(c_s)
            n_s[...] = jnp.zeros_like(n_s)
            m_s[...] = jnp.zeros_like(m_s)

    _mlstm_chunk(per_dir, bias_ref, c_s, n_s, m_s, h_refs)

    if not with_h:
        @pl.when(j == n_chunks - 1)
        def _():
            c_out[...] = c_s[...]
            n_out[...] = n_s[...]
            m_out[...] = m_s[...]


def _mlstm(q, k, p, v_blk, gates, bias, state):
    b, n, _ = k.shape
    n_chunks = n // CHUNK
    with_h = q is not None
    ns = MLSTM_SAMPLES if b % MLSTM_SAMPLES == 0 else 1

    in_specs, args = [], []
    for d in range(2):
        c = (lambda j: j) if d == 0 else (lambda j: n_chunks - 1 - j)
        tok = pl.BlockSpec((ns, CHUNK, GROUP_W), lambda i, j, c=c: (i, c(j), 0))
        if with_h:
            in_specs.append(tok)
            args.append(q)
        in_specs += [tok,
                     pl.BlockSpec((ns, CHUNK, GROUP_W), lambda i, j, c=c: (i, c(j), v_blk)),
                     pl.BlockSpec((ns, CHUNK, GATE_PAD), lambda i, j, c=c: (i, c(j), 0))]
        args += [k, p, gates]
    in_specs.append(pl.BlockSpec((1, GATE_PAD), lambda i, j: (0, 0)))
    args.append(bias)

    c_shape = (2, HEADS, HEAD_DIM, HEAD_DIM)
    v_shape = (2, HEADS, 1, HEAD_DIM)
    c_spec = pl.BlockSpec((ns,) + c_shape, lambda i, j: (i, 0, 0, 0, 0))
    v_spec = pl.BlockSpec((ns,) + v_shape, lambda i, j: (i, 0, 0, 0, 0))
    if with_h:
        in_specs += [c_spec, v_spec, v_spec]
        args += list(state)
        out_specs = [pl.BlockSpec((ns, CHUNK, GROUP_W), lambda i, j: (i, j, 0)),
                     pl.BlockSpec((ns, CHUNK, GROUP_W), lambda i, j: (i, n_chunks - 1 - j, 0))]
        out_shape = [jax.ShapeDtypeStruct((b, n, GROUP_W), F32)] * 2
    else:
        out_specs = [c_spec, v_spec, v_spec]
        out_shape = [jax.ShapeDtypeStruct((b,) + c_shape, F32),
                     jax.ShapeDtypeStruct((b,) + v_shape, F32),
                     jax.ShapeDtypeStruct((b,) + v_shape, F32)]
    return pl.pallas_call(
        functools.partial(_mlstm_kernel, n_chunks=n_chunks, with_h=with_h),
        grid=(b // ns, n_chunks),
        in_specs=in_specs,
        out_specs=out_specs,
        out_shape=out_shape,
        scratch_shapes=[pltpu.VMEM((ns,) + c_shape, F32), pltpu.VMEM((ns,) + v_shape, F32),
                        pltpu.VMEM((ns,) + v_shape, F32)],
        compiler_params=_params("arbitrary", "arbitrary"),
        name="mlstm" if with_h else "mlstm_ctx_state",
    )(*args)


def _layer_norm(x):
    mu = jnp.mean(x, axis=-1, keepdims=True)
    var = jnp.mean(jnp.square(x - mu), axis=-1, keepdims=True)
    return (x - mu) * lax.rsqrt(var + EPS)


def _postmix_kernel(u_ref, vg_ref, o_ref, hf_ref, hb_ref, x_ref, mod_ref, ws_ref, bs_ref, ng_ref,
                    wout_ref, fg_ref, wr_ref, x1_ref, h2_ref, aff_ref, ycat_s):
    tm = x_ref.shape[1]
    b = pl.program_id(0)
    mod = lambda k: mod_ref[pl.ds(b, 1), k * D_MODEL:(k + 1) * D_MODEL]
    wr_hi, wr_lo = _split2(wr_ref[...])
    expert_lane = _iota2((ROW_GROUP, 128), 1) < N_EXPERTS

    for r in range(tm // ROW_GROUP):
        rows = slice(r * ROW_GROUP, (r + 1) * ROW_GROUP)

        for c in range(r * ROW_GROUP // CHUNK, (r + 1) * ROW_GROUP // CHUNK):
            crows = slice(c * CHUNK, (c + 1) * CHUNK)
            u = u_ref[0, crows, :]
            v = vg_ref[0, crows, :].astype(BF16)
            for hd in range(HEADS):
                lanes = slice(hd * HEAD_DIM, (hd + 1) * HEAD_DIM)
                s = _dot(ws_ref[hd], v[:, lanes]) + bs_ref[hd]
                ycat_s[crows, lanes] = (u[:, lanes] * s).astype(BF16)

        hsum = hf_ref[0, rows, :] + hb_ref[0, rows, :]
        o = o_ref[0, rows, :]
        for hd in range(HEADS):
            lanes = slice(hd * HEAD_DIM, (hd + 1) * HEAD_DIM)
            hn = _layer_norm(hsum[:, lanes]) * ng_ref[:, lanes]
            ycat_s[rows, GROUP_W + hd * HEAD_DIM:GROUP_W + (hd + 1) * HEAD_DIM] = (
                o[:, lanes] * hn).astype(BF16)

        y = _dot(ycat_s[rows, :], wout_ref[...])
        x1 = x_ref[0, rows, :] + mod(2) * y
        x1_ref[0, rows, :] = x1

        n2 = x1 * lax.rsqrt(jnp.mean(x1 * x1, axis=-1, keepdims=True) + EPS) * fg_ref[...]
        h2 = n2 * (1.0 + mod(4)) + mod(3)
        h2_hi, h2_lo = _split2(h2)
        h2_ref[0, rows, :] = h2_hi

        logits = _dot(h2_hi, wr_hi) + (_dot(h2_lo, wr_hi) + _dot(h2_hi, wr_lo))
        logits = jnp.where(expert_lane, logits, -jnp.inf)
        e = jnp.exp(logits - jnp.max(logits, axis=-1, keepdims=True))
        aff = e / jnp.sum(e, axis=-1, keepdims=True)
        aff_ref[0, :, rows] = aff.T[0:N_EXPERTS, :]


def _postmix(p, hf, hb, x, mod, ws, bs, ng, wout, fg, wr, tm):
    b, n, _ = x.shape
    tok = lambda blk: pl.BlockSpec((1, tm, GROUP_W), lambda i, j: (i, j, blk))
    full = lambda a: pl.BlockSpec(a.shape, lambda i, j: (0,) * a.ndim)
    return pl.pallas_call(
        _postmix_kernel,
        grid=(b, n // tm),
        in_specs=[tok(P_U), tok(P_VG), tok(P_O), tok(0), tok(0),
                  pl.BlockSpec((1, tm, D_MODEL), lambda i, j: (i, j, 0)),
                  full(mod), full(ws), full(bs), full(ng), full(wout), full(fg), full(wr)],
        out_specs=[pl.BlockSpec((1, tm, D_MODEL), lambda i, j: (i, j, 0)),
                   pl.BlockSpec((1, tm, D_MODEL), lambda i, j: (i, j, 0)),
                   pl.BlockSpec((1, N_EXPERTS, tm), lambda i, j: (i, 0, j))],
        out_shape=[jax.ShapeDtypeStruct((b, n, D_MODEL), F32),
                   jax.ShapeDtypeStruct((b, n, D_MODEL), BF16),
                   jax.ShapeDtypeStruct((b, N_EXPERTS, n), F32)],
        scratch_shapes=[pltpu.VMEM((tm, D_MODEL), BF16)],
        compiler_params=_params("arbitrary", "arbitrary"),
        name="postmix",
    )(p, p, p, hf, hb, x, mod, ws, bs, ng, wout, fg, wr)


def _cumsum_lanes(x, upper):
    carry = jnp.zeros((x.shape[0], 1), F32)
    outs, before = [], []
    for j in range(x.shape[1] // 128):
        before.append(carry)
        c = _dot(x[:, j * 128:(j + 1) * 128].astype(BF16), upper) + carry
        outs.append(c)
        carry = c[:, 127:128]
    return jnp.concatenate(outs, axis=1), before


def _rows_to_lanes(x, fill):
    pad = jnp.full((128 - x.shape[0], 128), fill, F32)
    return jnp.concatenate([x, pad], axis=0).T


def _route_kernel(aff_ref, slot_ref, w_ref, slot_t_ref, first_t_ref, first_ref, *, capacity, n_e):
    aff = aff_ref[...]
    cap = float(capacity)
    thr_bits = jnp.zeros((aff.shape[0], 1), jnp.int32)
    for bit in range(30, -1, -1):
        cand = thr_bits | (1 << bit)
        cnt = jnp.sum(jnp.where(aff >= pltpu.bitcast(cand, F32), 1.0, 0.0), axis=-1, keepdims=True)
        thr_bits = jnp.where(cnt >= cap, cand, thr_bits)
    thr = pltpu.bitcast(thr_bits, F32)
    upper = jnp.where(_iota2((128, 128), 0) <= _iota2((128, 128), 1), 1.0, 0.0).astype(BF16)
    above = jnp.where(aff > thr, 1.0, 0.0)
    tied = jnp.where(aff == thr, 1.0, 0.0)
    need = cap - jnp.sum(above, axis=-1, keepdims=True)
    sel = above + tied * jnp.where(_cumsum_lanes(tied, upper)[0] <= need, 1.0, 0.0)
    count, before = _cumsum_lanes(sel, upper)
    slot = jnp.where(sel > 0.0, count - 1.0, -1.0)
    slot_ref[...] = slot
    w_ref[...] = jnp.where(sel > 0.0, aff, 0.0)

    n_blocks = len(before)
    lane = _iota2((1, 128), 1)
    first = jnp.zeros((aff.shape[0], 128), F32)
    for j in range(n_blocks):
        first = jnp.where(lane == j, before[j], first)
    first_ref[...] = first
    for smp in range(aff.shape[0] // n_e):
        rows = slice(smp * n_e, (smp + 1) * n_e)
        for j in range(n_blocks):
            slot_t_ref[smp, j * 128:(j + 1) * 128, :] = _rows_to_lanes(
                slot[rows, j * 128:(j + 1) * 128], -1.0)
        first_t_ref[smp] = _rows_to_lanes(first[rows, :], 0.0)[0:n_blocks, :]


def _route(aff_t, capacity):
    b, e, n = aff_t.shape
    assert n // 128 <= 128 and e <= 128 and e % 8 == 0
    rows = pl.BlockSpec((b * e, n), lambda i: (0, 0))
    slot, w, slot_t, first_t, first = pl.pallas_call(
        functools.partial(_route_kernel, capacity=capacity, n_e=e),
        grid=(1,),
        in_specs=[rows],
        out_specs=[rows, rows,
                   pl.BlockSpec((b, n, 128), lambda i: (0, 0, 0)),
                   pl.BlockSpec((b, n // 128, 128), lambda i: (0, 0, 0)),
                   pl.BlockSpec((b * e, 128), lambda i: (0, 0))],
        out_shape=[jax.ShapeDtypeStruct((b * e, n), F32)] * 2
        + [jax.ShapeDtypeStruct((b, n, 128), F32), jax.ShapeDtypeStruct((b, n // 128, 128), F32),
           jax.ShapeDtypeStruct((b * e, 128), F32)],
        compiler_params=_params("arbitrary"),
        name="route",
    )(aff_t.reshape(b * e, n))
    return slot.reshape(b, e, n), w.reshape(b, e, n), slot_t, first_t, first.reshape(b, e, 128)


GATHER_TILE = 256
GATHER_ROWS = (64, 128)
GATHER_GROUP = 4


def _gather_kernel(before_ref, fits_ref, h2_ref, slot_ref, w_ref, xe_ref, gate_ref, xe_s, gate_s):
    b = pl.program_id(0)
    g = pl.program_id(1)
    n_g, cap = xe_ref.shape[1], xe_ref.shape[2]
    n = h2_ref.shape[1]
    n_tiles = n // GATHER_TILE

    def windowed(n_rows):
        xe_s[:, 0:16, :] = jnp.zeros((n_g, 16, D_MODEL), BF16)
        gate_s[...] = jnp.zeros_like(gate_s)
        row_id = _iota2((n_rows, 1), 0).astype(F32)
        for j in range(n_tiles):
            starts, onehots = [], []
            for e in range(n_g):
                start = pl.multiple_of((before_ref[b, g * n_g + e, j] // 16) * 16, 16)
                hit = (slot_ref[0, e, j:j + 1, :] - start.astype(F32)) == row_id
                onehots.append(jnp.where(hit, 1.0, 0.0).astype(BF16))
                gate_s[e, pl.ds(start, n_rows), :] += jnp.sum(
                    jnp.where(hit, w_ref[0, e, j:j + 1, :], 0.0), axis=1, keepdims=True)
                starts.append(start)
            rows = _dot(jnp.concatenate(onehots, axis=0),
                        h2_ref[0, j * GATHER_TILE:(j + 1) * GATHER_TILE, :]).astype(BF16)
            for e, start in enumerate(starts):
                r0 = e * n_rows
                xe_s[e, pl.ds(start, 16), :] += rows[r0:r0 + 16]
                xe_s[e, pl.ds(start + 16, n_rows - 16), :] = rows[r0 + 16:r0 + n_rows]
        xe_ref[0] = xe_s[:, 0:cap, :]
        for e in range(n_g):
            gate_ref[0, e] = _cols_to_rows(gate_s[e, 0:cap, :])

    for level, n_rows in enumerate(GATHER_ROWS):
        pl.when(fits_ref[b, g] == level)(functools.partial(windowed, n_rows))

    @pl.when(fits_ref[b, g] == len(GATHER_ROWS))
    def _():
        slot_id = _iota2((cap, 1), 0).astype(F32)
        for e in range(n_g):
            xe = jnp.zeros((cap, D_MODEL), F32)
            gate = jnp.zeros((cap, 1), F32)
            for j in range(n_tiles):
                hit = slot_ref[0, e, j:j + 1, :] == slot_id
                xe = xe + _dot(jnp.where(hit, 1.0, 0.0).astype(BF16),
                               h2_ref[0, j * GATHER_TILE:(j + 1) * GATHER_TILE, :])
                gate = gate + jnp.sum(jnp.where(hit, w_ref[0, e, j:j + 1, :], 0.0), axis=1,
                                      keepdims=True)
            xe_ref[0, e] = xe.astype(BF16)
            gate_ref[0, e] = _cols_to_rows(gate)


def _gather(h2, slot, w, first, capacity):
    b, n, _ = h2.shape
    n_e = slot.shape[1]
    n_tiles = n // GATHER_TILE
    assert n_e % GATHER_GROUP == 0 and GATHER_TILE % 128 == 0 and capacity % 128 == 0
    assert all(r % 16 == 0 and r >= 32 for r in GATHER_ROWS) and list(GATHER_ROWS) == sorted(GATHER_ROWS)
    before = first[:, :, 0:n // 128:GATHER_TILE // 128].astype(jnp.int32)
    after = jnp.concatenate([before[:, :, 1:], jnp.full((b, n_e, 1), capacity, jnp.int32)], axis=2)
    extent = jnp.max(after - (before // 16) * 16, axis=2)
    extent = jnp.max(extent.reshape(b, n_e // GATHER_GROUP, GATHER_GROUP), axis=2)
    fits = sum((extent >= n_rows).astype(jnp.int32) for n_rows in GATHER_ROWS)
    tiles = lambda a: a.reshape(b, n_e, n_tiles, GATHER_TILE)
    rows = pl.BlockSpec((1, GATHER_GROUP, n_tiles, GATHER_TILE), lambda i, g, *_: (i, g, 0, 0))
    return pl.pallas_call(
        _gather_kernel,
        grid_spec=pltpu.PrefetchScalarGridSpec(
            num_scalar_prefetch=2,
            grid=(b, n_e // GATHER_GROUP),
            in_specs=[pl.BlockSpec((1, n, D_MODEL), lambda i, g, *_: (i, 0, 0)), rows, rows],
            out_specs=[pl.BlockSpec((1, GATHER_GROUP, capacity, D_MODEL), lambda i, g, *_: (i, g, 0, 0)),
                       pl.BlockSpec((1, GATHER_GROUP, capacity // 128, 128), lambda i, g, *_: (i, g, 0, 0))],
            scratch_shapes=[pltpu.VMEM((GATHER_GROUP, capacity + max(GATHER_ROWS), D_MODEL), BF16),
                            pltpu.VMEM((GATHER_GROUP, capacity + max(GATHER_ROWS), 1), F32)]),
        out_shape=[jax.ShapeDtypeStruct((b, n_e, capacity, D_MODEL), BF16),
                   jax.ShapeDtypeStruct((b, n_e, capacity // 128, 128), F32)],
        compiler_params=_params("arbitrary", "arbitrary"),
        name="gather",
    )(before, fits, h2, tiles(slot), tiles(w))


def _ffn_kernel(xe_ref, gate_ref, wg_ref, wu_ref, wd_ref, ye_ref, acc_s):
    ft = pl.program_id(1)
    last = pl.num_programs(1) - 1

    def step(first, final):
        wg = wg_ref[0].astype(BF16)
        wu = wu_ref[0].astype(BF16)
        wd = wd_ref[0].astype(BF16)
        for i in range(xe_ref.shape[0]):
            xe = xe_ref[i, 0]
            act = jax.nn.silu(_dot(xe, wg)) * _dot(xe, wu)
            part = _dot(act.astype(BF16), wd)
            total = part if first else acc_s[i] + part
            if final:
                ye_ref[i] = (total * _rows_to_cols(gate_ref[i, 0])).astype(BF16)
            else:
                acc_s[i] = total

    pl.when(ft == 0)(lambda: step(True, False))
    pl.when(jnp.logical_and(ft > 0, ft < last))(lambda: step(False, False))
    pl.when(ft == last)(lambda: step(False, True))


def _ffn(xe, gate, wg, wu, wd, f_tile):
    b, n_e, cap, _ = xe.shape
    assert D_EXPERT // f_tile >= 2
    return pl.pallas_call(
        _ffn_kernel,
        grid=(n_e, D_EXPERT // f_tile),
        in_specs=[pl.BlockSpec((b, 1, cap, D_MODEL), lambda e, f: (0, e, 0, 0)),
                  pl.BlockSpec((b, 1, cap // 128, 128), lambda e, f: (0, e, 0, 0)),
                  pl.BlockSpec((1, D_MODEL, f_tile), lambda e, f: (e, 0, f)),
                  pl.BlockSpec((1, D_MODEL, f_tile), lambda e, f: (e, 0, f)),
                  pl.BlockSpec((1, f_tile, D_MODEL), lambda e, f: (e, f, 0))],
        out_specs=pl.BlockSpec((b, cap, D_MODEL), lambda e, f: (0, e, 0)),
        out_shape=jax.ShapeDtypeStruct((b, n_e * cap, D_MODEL), BF16),
        scratch_shapes=[pltpu.VMEM((b, cap, D_MODEL), F32)],
        compiler_params=_params("arbitrary", "arbitrary"),
        name="ffn",
    )(xe, gate, wg, wu, wd)


COMBINE_WIDE = CHUNK + 16
COMBINE_NARROW = 48


def _combine_kernel(first_ref, narrow_ref, slot_t_ref, first_t_ref, ye_ref, x1_ref, mod_ref, fg_ref,
                    o_ref, acc_s, *, capacity):
    b = pl.program_id(0)
    j = pl.program_id(1)
    n_sub = o_ref.shape[1] // 128
    lane = _iota2((1, 128), 1).astype(F32)

    def scatter(window):
        k_total = N_EXPERTS * window
        for sb in range(n_sub):
            blk = j * n_sub + sb
            slot_t = slot_t_ref[0, sb * 128:(sb + 1) * 128, :]
            start_row = jnp.minimum(jnp.floor(first_t_ref[0, pl.ds(blk, 1), :] * (1.0 / 16.0)) * 16.0,
                                    float(capacity - window))
            k_pos = jnp.where(slot_t >= 0.0, slot_t - start_row + lane * float(window), -1.0)
            cols = []
            for c in range(k_total // 128):
                k_lane = lane + float(128 * c)
                hit = jnp.zeros((128, 128), F32)
                for e in range((128 * c) // window, (128 * c + 127) // window + 1):
                    hit = jnp.where(k_pos[:, e:e + 1] == k_lane, 1.0, hit)
                cols.append(hit.astype(BF16))
            onehot = jnp.concatenate(cols, axis=1)
            rows = []
            for e in range(N_EXPERTS):
                start = jnp.minimum((first_ref[b, blk, e] // 16) * 16, capacity - window)
                rows.append(ye_ref[0, pl.ds(pl.multiple_of(e * capacity + start, 16), window), :])
            acc_s[sb * 128:(sb + 1) * 128, :] = _dot(onehot, jnp.concatenate(rows, axis=0))

    pl.when(narrow_ref[b, j] != 0)(lambda: scatter(COMBINE_NARROW))
    pl.when(narrow_ref[b, j] == 0)(lambda: scatter(COMBINE_WIDE))

    g2 = mod_ref[pl.ds(b, 1), 5 * D_MODEL:6 * D_MODEL]
    x2 = x1_ref[0] + g2 * acc_s[...]
    o_ref[0] = x2 * lax.rsqrt(jnp.mean(x2 * x2, axis=-1, keepdims=True) + EPS) * fg_ref[...]


def _combine(slot_t, first_t, ye, x1, mod, fg, capacity, tm):
    b, n, _ = x1.shape
    n_blocks = n // 128
    for window in (COMBINE_WIDE, COMBINE_NARROW):
        assert (N_EXPERTS * window) % 128 == 0 and window % 16 == 0 and window <= capacity
    assert COMBINE_WIDE >= 128 + 15 and capacity % 16 == 0
    first = first_t[:, :, :N_EXPERTS].astype(jnp.int32)
    after = jnp.concatenate([first[:, 1:], jnp.full((b, 1, N_EXPERTS), capacity, jnp.int32)], axis=1)
    narrow = jnp.all(after - (first // 16) * 16 <= COMBINE_NARROW, axis=2)
    narrow = jnp.all(narrow.reshape(b, n // tm, tm // 128), axis=2).astype(jnp.int32)
    tok = pl.BlockSpec((1, tm, D_MODEL), lambda i, j, *_: (i, j, 0))
    return pl.pallas_call(
        functools.partial(_combine_kernel, capacity=capacity),
        grid_spec=pltpu.PrefetchScalarGridSpec(
            num_scalar_prefetch=2,
            grid=(b, n // tm),
            in_specs=[pl.BlockSpec((1, tm, 128), lambda i, j, *_: (i, j, 0)),
                      pl.BlockSpec((1, n_blocks, 128), lambda i, j, *_: (i, 0, 0)),
                      pl.BlockSpec((1, N_EXPERTS * capacity, D_MODEL), lambda i, j, *_: (i, 0, 0)),
                      tok,
                      pl.BlockSpec(mod.shape, lambda i, j, *_: (0, 0)),
                      pl.BlockSpec((1, D_MODEL), lambda i, j, *_: (0, 0))],
            out_specs=tok,
            scratch_shapes=[pltpu.VMEM((tm, D_MODEL), F32)]),
        out_shape=jax.ShapeDtypeStruct((b, n, D_MODEL), F32),
        compiler_params=_params("arbitrary", "arbitrary"),
        name="combine",
    )(first, narrow, slot_t, first_t, ye, x1, mod, fg)


def kernel(x, c, ctx, c_ctx, w_mod, b_mod, norm_mix_g, w_in, conv_q, conv_k, b_igate, b_fgate,
           gmlp_ws, gmlp_bs, mlstm_norm_g, w_out, norm_ffn_g, w_router, w_gate_e, w_up_e,
           w_down_e, final_g):
    depth = w_mod.shape[0]
    assert depth == 1, "the context stream is only carried as mLSTM states (single layer)"
    batch, seq, _ = x.shape
    assert seq % GRID_W == 0 and seq % CHUNK == 0 and batch + 1 <= MOD_ROWS
    capacity = EC_FACTOR * seq // N_EXPERTS
    ctx_row = batch
    l = 0

    cond = jnp.concatenate([c, c_ctx[None], jnp.zeros((MOD_ROWS - batch - 1, D_MODEL), F32)], axis=0)
    mod = _adaln(cond, w_mod[l], b_mod[l][None])

    row = lambda a: a[None]
    w_gates = jnp.pad(w_in[l][:, MAIN_W:], ((0, 0), (0, GATE_PAD - N_GATES)))
    cols = lambda *blks: jnp.concatenate(
        [w_in[l][:, k * GROUP_W:(k + 1) * GROUP_W] for k in blks] + [w_gates], axis=1).astype(BF16)
    gate_bias = jnp.pad(jnp.concatenate([b_igate[l].reshape(-1), b_fgate[l].reshape(-1)]),
                        (0, GATE_PAD - N_GATES))[None]
    k_scale = HEAD_DIM ** -0.5

    identity = lambda a: a
    k_c, v_c, gates_c = _inproj(ctx, mod, row(norm_mix_g[l]), cols(K_BLK, V_BLK), conv_k[l][None],
                                (k_scale,), (identity,), tm=ctx.shape[1], ctx_row=ctx_row)
    state = _mlstm(None, k_c, v_c, 0, gates_c, gate_bias, None)

    q_l, k_l, p, gates = _inproj(x, mod, row(norm_mix_g[l]),
                                 cols(Q_BLK, K_BLK, U_BLK, VG_BLK, O_BLK, V_BLK),
                                 jnp.stack([conv_q[l], conv_k[l]]), (1.0, k_scale),
                                 (jax.nn.gelu, lambda a: _layer_norm(jax.nn.gelu(a)), jax.nn.sigmoid, identity),
                                 tm=512, ctx_row=None)
    h_f, h_b = _mlstm(q_l, k_l, p, P_V, gates, gate_bias, state)

    x1, h2, aff_t = _postmix(p, h_f, h_b, x, mod, gmlp_ws[l].astype(BF16), gmlp_bs[l][:, :, None],
                             row(mlstm_norm_g[l]), w_out[l].astype(BF16), row(norm_ffn_g[l]),
                             jnp.pad(w_router[l], ((0, 0), (0, 128 - N_EXPERTS))), tm=512)

    slot, gate_w, slot_t, first_t, first = _route(aff_t, capacity)

    xe, gate = _gather(h2, slot, gate_w, first, capacity)
    ye = _ffn(xe, gate, w_gate_e[l], w_up_e[l], w_down_e[l], f_tile=512)
    return _combine(slot_t, first_t, ye, x1, mod, row(final_g), capacity, tm=512)
```

```python
import functools

import jax
import jax.numpy as jnp
from jax import lax
from jax.experimental import pallas as pl
from jax.experimental.pallas import tpu as pltpu

F32 = jnp.float32
BF16 = jnp.bfloat16

D_MODEL = 1024
GRID_W = 64
CHUNK = 128
HEADS = 4
GROUP_W = D_MODEL // 2
HEAD_DIM = GROUP_W // HEADS
N_EXPERTS = 16
EC_FACTOR = 2
D_EXPERT = 2 * D_MODEL
EPS = 1e-6
N_GATES = 4 * HEADS
GATE_PAD = 128
MOD_ROWS = 8

U_BLK, VG_BLK, Q_BLK, O_BLK, K_BLK, V_BLK = 0, 1, 2, 3, 4, 5
MAIN_W = 6 * GROUP_W

VMEM_LIMIT = 56 * 1024 * 1024
ROW_GROUP = 2 * CHUNK
MLSTM_SAMPLES = 4


def _params(*sem):
    return pltpu.CompilerParams(dimension_semantics=sem, vmem_limit_bytes=VMEM_LIMIT)


def _dot(a, b):
    return jnp.dot(a, b, preferred_element_type=F32)


def _dot_nt(a, b):
    return lax.dot_general(a, b, (((1,), (1,)), ((), ())), preferred_element_type=F32)


def _split2(a):
    hi = a.astype(BF16)
    lo = (a - hi.astype(F32)).astype(BF16)
    return hi, lo


def _dot3(a, b):
    ah, al = _split2(a)
    bh, bl = _split2(b)
    return _dot(ah, bh) + (_dot(al, bh) + _dot(ah, bl))


def _dot_exact01(tri, x):
    x1 = x.astype(BF16)
    r1 = x - x1.astype(F32)
    x2 = r1.astype(BF16)
    x3 = (r1 - x2.astype(F32)).astype(BF16)
    return _dot(tri, x1) + (_dot(tri, x2) + _dot(tri, x3))


def _iota2(shape, dim):
    return lax.broadcasted_iota(jnp.int32, shape, dim)


def _cols_to_rows(col):
    eye = _iota2((128, 128), 0) == _iota2((128, 128), 1)
    return jnp.concatenate(
        [jnp.sum(jnp.where(eye, col[r * 128:(r + 1) * 128, :], 0.0), axis=0, keepdims=True)
         for r in range(col.shape[0] // 128)], axis=0)


def _rows_to_cols(rows):
    eye = _iota2((128, 128), 0) == _iota2((128, 128), 1)
    return jnp.concatenate(
        [jnp.sum(jnp.where(eye, rows[r:r + 1, :], 0.0), axis=1, keepdims=True)
         for r in range(rows.shape[0])], axis=0)


def _adaln_kernel(cond_ref, w_ref, b_ref, o_ref):
    o_ref[...] = _dot3(jax.nn.silu(cond_ref[...]), w_ref[...]) + b_ref[...]


def _adaln(cond, w, b):
    n_out = w.shape[1]
    tn = D_MODEL
    assert n_out % tn == 0
    return pl.pallas_call(
        _adaln_kernel,
        grid=(n_out // tn,),
        in_specs=[pl.BlockSpec((MOD_ROWS, D_MODEL), lambda j: (0, 0)),
                  pl.BlockSpec((D_MODEL, tn), lambda j: (0, j)),
                  pl.BlockSpec((1, tn), lambda j: (0, j))],
        out_specs=pl.BlockSpec((MOD_ROWS, tn), lambda j: (0, j)),
        out_shape=jax.ShapeDtypeStruct((MOD_ROWS, n_out), F32),
        compiler_params=_params("arbitrary"),
        name="adaln",
    )(cond, w, b)


def _inproj_kernel(x_ref, xprev_ref, xnext_ref, mod_ref, g_ref, taps_ref, w_ref, *outs, ctx_row, scales,
                   plain_acts):
    conv_refs, plain_refs, gate_ref = outs[:len(scales)], outs[len(scales):-1], outs[-1]
    row = pl.program_id(0) if ctx_row is None else ctx_row
    j = pl.program_id(1)
    sh = mod_ref[pl.ds(row, 1), 0:D_MODEL]
    sc = mod_ref[pl.ds(row, 1), D_MODEL:2 * D_MODEL]

    def modulated(x):
        y = x * lax.rsqrt(jnp.mean(x * x, axis=-1, keepdims=True) + EPS) * g_ref[...]
        return y * (1.0 + sc) + sh

    n_groups = x_ref.shape[1] // ROW_GROUP
    before = modulated(xprev_ref[0]) * (j > 0).astype(F32)
    for r in range(n_groups):
        rows = slice(r * ROW_GROUP, (r + 1) * ROW_GROUP)
        h = modulated(x_ref[0, rows, :])
        if r == n_groups - 1:
            after = modulated(xnext_ref[0]) * (j < pl.num_programs(1) - 1).astype(F32)
        else:
            after = modulated(x_ref[0, (r + 1) * ROW_GROUP:(r + 1) * ROW_GROUP + 8, :])
        ext = jnp.concatenate([before, h, after], axis=0).astype(BF16)
        before = h[ROW_GROUP - 8:, :]
        y = _dot(ext, w_ref[...])
        n_ext, n_conv = ext.shape[0], len(scales) * GROUP_W
        y_here = y[8:8 + ROW_GROUP]
        for k, (act, o_ref) in enumerate(zip(plain_acts, plain_refs)):
            lanes = slice(n_conv + k * GROUP_W, n_conv + (k + 1) * GROUP_W)
            o_ref[0, rows, :] = act(y_here[:, lanes]).astype(o_ref.dtype)
        gate_ref[0, rows, :] = y_here[:, n_conv + len(plain_acts) * GROUP_W:]
        y_conv = y[:, 0:n_conv]
        y_before = pltpu.roll(y_conv, 1, axis=0)[8:8 + ROW_GROUP]
        y_after = pltpu.roll(y_conv, n_ext - 1, axis=0)[8:8 + ROW_GROUP]
        for s, scale in enumerate(scales):
            lanes = slice(s * GROUP_W, (s + 1) * GROUP_W)
            z = (y_before[:, lanes] * taps_ref[s, 0:1, :] + y_here[:, lanes] * taps_ref[s, 1:2, :]
                 + y_after[:, lanes] * taps_ref[s, 2:3, :])
            conv_refs[s][0, rows, :] = (jax.nn.silu(z) * scale).astype(BF16)


def _inproj(x, mod, g, w, taps, scales, plain_acts, tm, ctx_row):
    b, n, _ = x.shape
    assert w.shape[1] == (len(scales) + len(plain_acts)) * GROUP_W + GATE_PAD
    assert tm % ROW_GROUP == 0 and n % tm == 0
    rows8 = tm // 8
    last8 = n // 8 - 1
    full = lambda a: pl.BlockSpec(a.shape, lambda i, j: (0,) * a.ndim)
    group_spec = pl.BlockSpec((1, tm, GROUP_W), lambda i, j: (i, j, 0))
    return pl.pallas_call(
        functools.partial(_inproj_kernel, ctx_row=ctx_row, scales=tuple(scales),
                          plain_acts=tuple(act for act, _ in plain_acts)),
        grid=(b, n // tm),
        in_specs=[pl.BlockSpec((1, tm, D_MODEL), lambda i, j: (i, j, 0)),
                  pl.BlockSpec((1, 8, D_MODEL), lambda i, j: (i, jnp.maximum(j * rows8 - 1, 0), 0)),
                  pl.BlockSpec((1, 8, D_MODEL), lambda i, j: (i, jnp.minimum((j + 1) * rows8, last8), 0)),
                  full(mod), full(g), full(taps), full(w)],
        out_specs=[group_spec] * (len(scales) + len(plain_acts))
        + [pl.BlockSpec((1, tm, GATE_PAD), lambda i, j: (i, j, 0))],
        out_shape=[jax.ShapeDtypeStruct((b, n, GROUP_W), BF16)] * len(scales)
        + [jax.ShapeDtypeStruct((b, n, GROUP_W), dtype) for _, dtype in plain_acts]
        + [jax.ShapeDtypeStruct((b, n, GATE_PAD), F32)],
        compiler_params=_params("arbitrary", "arbitrary"),
        name="inproj",
    )(x, x, x, mod, g, taps, w)


def _mlstm_chunk(per_dir, bias_ref, c_s, n_s, m_s, h_refs):
    with_h = h_refs[0] is not None
    i0 = _iota2((CHUNK, CHUNK), 0)
    i1 = _iota2((CHUNK, CHUNK), 1)

    units = []
    for smp, d in [(smp, d) for smp in range(c_s.shape[0]) for d in range(2)]:
        q_ref, k_ref, v_ref, g_ref = per_dir[d]
        sees_ts = (i1 <= i0) if d == 0 else (i1 >= i0)
        sees_st = (i0 <= i1) if d == 0 else (i0 >= i1)
        tri = jnp.where(sees_ts, 1.0, 0.0).astype(BF16)
        gates = g_ref[smp] + bias_ref[...]
        bcum = _dot_exact01(tri, jax.nn.log_sigmoid(gates))
        for hd in range(HEADS):
            lanes = slice(hd * HEAD_DIM, (hd + 1) * HEAD_DIM)
            c_prev = c_s[smp, d, hd]
            n_prev = n_s[smp, d, hd]
            k = k_ref[smp, :, lanes]
            st = dict(smp=smp, d=d, hd=hd, lanes=lanes, c_prev=c_prev, n_prev=n_prev, k=k,
                      sees_st=sees_st, gates=gates, bcum=bcum, v=v_ref[smp, :, lanes])
            if with_h:
                lhs = jnp.concatenate([k, c_prev.astype(BF16),
                                       jnp.broadcast_to(n_prev, (16, HEAD_DIM)).astype(BF16)], axis=0)
                st["prod"] = _dot_nt(lhs, q_ref[smp, :, lanes])
            units.append(st)

    rows = {}
    for st in units:
        smp, d, hd = st["smp"], st["d"], st["hd"]
        if (smp, d) not in rows:
            rows[smp, d] = (st["gates"].T, st["bcum"].T)
        gates_t, bcum_t = rows[smp, d]
        ci = d * HEADS + hd
        cf = 2 * HEADS + d * HEADS + hd
        last = CHUNK - 1 if d == 0 else 0
        li_row = gates_t[ci:ci + 1, :]
        bc_row = bcum_t[cf:cf + 1, :]
        b_last = bc_row[:, last:last + 1]
        m_prev = m_s[smp, d, hd]
        v_t = st["v"].T
        a_row = b_last - bc_row + li_row
        m_new = jnp.maximum(b_last + m_prev, jnp.max(a_row, axis=-1, keepdims=True))
        w_row = jnp.exp(a_row - m_new)
        lhs = jnp.concatenate([v_t * w_row, jnp.broadcast_to(w_row, (16, CHUNK))], axis=0)
        st["upd"] = _dot(lhs.astype(BF16), st["k"])
        st.update(bc_row=bc_row, m_prev=m_prev, m_new=m_new, v_t=v_t,
                  decay=jnp.exp(b_last + m_prev - m_new),
                  u_col=st["gates"][:, ci:ci + 1] - st["bcum"][:, cf:cf + 1])

    if with_h:
        for st in units:
            prod, bc_row = st["prod"], st["bc_row"]
            g = bc_row + st["m_prev"]
            dmat = jnp.where(st["sees_st"], st["u_col"] + bc_row, -jnp.inf)
            m_t = jnp.maximum(g, jnp.max(dmat, axis=0, keepdims=True))
            inter = jnp.exp(g - m_t)
            s = prod[0:CHUNK] * jnp.exp(dmat - m_t)
            st["pv"] = _dot(st["v_t"].astype(BF16), s.astype(BF16))
            st["num0"] = inter * prod[CHUNK:2 * CHUNK]
            den = inter * prod[2 * CHUNK:2 * CHUNK + 1] + jnp.sum(s, axis=0, keepdims=True)
            st["scale"] = 1.0 / jnp.maximum(jnp.abs(den), jnp.exp(-m_t))

    for st in units:
        smp, d, hd = st["smp"], st["d"], st["hd"]
        if with_h:
            h_refs[d][smp, :, st["lanes"]] = ((st["num0"] + st["pv"]) * st["scale"]).T
        c_s[smp, d, hd] = st["decay"] * st["c_prev"] + st["upd"][0:HEAD_DIM]
        n_s[smp, d, hd] = st["decay"] * st["n_prev"] + st["upd"][HEAD_DIM:HEAD_DIM + 1]
        m_s[smp, d, hd] = st["m_new"]


def _mlstm_kernel(*refs, n_chunks, with_h):
    refs = list(refs)
    take = lambda n: [refs.pop(0) for _ in range(n)]
    per_dir = []
    for _ in range(2):
        q_ref = take(1)[0] if with_h else None
        k_ref, v_ref, g_ref = take(3)
        per_dir.append((q_ref, k_ref, v_ref, g_ref))
    bias_ref = take(1)[0]
    if with_h:
        c0_ref, n0_ref, m0_ref = take(3)
        h_refs = take(2)
        c_out = n_out = m_out = None
    else:
        h_refs = [None, None]
        c_out, n_out, m_out = take(3)
    c_s, n_s, m_s = take(3)

    j = pl.program_id(1)

    @pl.when(j == 0)
    def _():
        if with_h:
            c_s[...] = c0_ref[...]
            n_s[...] = n0_ref[...]
            m_s[...] = m0_ref[...]
        else:
            c_s[...] = jnp.zeros_like(c_s)
            n_s[...] = jnp.zeros_like(n_s)
            m_s[...] = jnp.zeros_like(m_s)

    _mlstm_chunk(per_dir, bias_ref, c_s, n_s, m_s, h_refs)

    if not with_h:
        @pl.when(j == n_chunks - 1)
        def _():
            c_out[...] = c_s[...]
            n_out[...] = n_s[...]
            m_out[...] = m_s[...]


def _mlstm(q, k, v, gates, bias, state):
    b, n, _ = k.shape
    n_chunks = n // CHUNK
    with_h = q is not None
    ns = MLSTM_SAMPLES if b % MLSTM_SAMPLES == 0 else 1

    in_specs, args = [], []
    for d in range(2):
        c = (lambda j: j) if d == 0 else (lambda j: n_chunks - 1 - j)
        tok = pl.BlockSpec((ns, CHUNK, GROUP_W), lambda i, j, c=c: (i, c(j), 0))
        if with_h:
            in_specs.append(tok)
            args.append(q)
        in_specs += [tok, tok, pl.BlockSpec((ns, CHUNK, GATE_PAD), lambda i, j, c=c: (i, c(j), 0))]
        args += [k, v, gates]
    in_specs.append(pl.BlockSpec((1, GATE_PAD), lambda i, j: (0, 0)))
    args.append(bias)

    c_shape = (2, HEADS, HEAD_DIM, HEAD_DIM)
    v_shape = (2, HEADS, 1, HEAD_DIM)
    c_spec = pl.BlockSpec((ns,) + c_shape, lambda i, j: (i, 0, 0, 0, 0))
    v_spec = pl.BlockSpec((ns,) + v_shape, lambda i, j: (i, 0, 0, 0, 0))
    if with_h:
        in_specs += [c_spec, v_spec, v_spec]
        args += list(state)
        out_specs = [pl.BlockSpec((ns, CHUNK, GROUP_W), lambda i, j: (i, j, 0)),
                     pl.BlockSpec((ns, CHUNK, GROUP_W), lambda i, j: (i, n_chunks - 1 - j, 0))]
        out_shape = [jax.ShapeDtypeStruct((b, n, GROUP_W), F32)] * 2
    else:
        out_specs = [c_spec, v_spec, v_spec]
        out_shape = [jax.ShapeDtypeStruct((b,) + c_shape, F32),
                     jax.ShapeDtypeStruct((b,) + v_shape, F32),
                     jax.ShapeDtypeStruct((b,) + v_shape, F32)]
    return pl.pallas_call(
        functools.partial(_mlstm_kernel, n_chunks=n_chunks, with_h=with_h),
        grid=(b // ns, n_chunks),
        in_specs=in_specs,
        out_specs=out_specs,
        out_shape=out_shape,
        scratch_shapes=[pltpu.VMEM((ns,) + c_shape, F32), pltpu.VMEM((ns,) + v_shape, F32),
                        pltpu.VMEM((ns,) + v_shape, F32)],
        compiler_params=_params("arbitrary", "arbitrary"),
        name="mlstm" if with_h else "mlstm_ctx_state",
    )(*args)


def _layer_norm(x):
    mu = jnp.mean(x, axis=-1, keepdims=True)
    var = jnp.mean(jnp.square(x - mu), axis=-1, keepdims=True)
    return (x - mu) * lax.rsqrt(var + EPS)


def _postmix_kernel(u_ref, vg_ref, o_ref, hf_ref, hb_ref, x_ref, mod_ref, ws_ref, bs_ref, ng_ref,
                    wout_ref, fg_ref, wr_ref, x1_ref, h2_ref, aff_ref, ycat_s):
    tm = x_ref.shape[1]
    b = pl.program_id(0)
    mod = lambda k: mod_ref[pl.ds(b, 1), k * D_MODEL:(k + 1) * D_MODEL]
    wr_hi, wr_lo = _split2(wr_ref[...])
    expert_lane = _iota2((ROW_GROUP, 128), 1) < N_EXPERTS

    for r in range(tm // ROW_GROUP):
        rows = slice(r * ROW_GROUP, (r + 1) * ROW_GROUP)

        for c in range(r * ROW_GROUP // CHUNK, (r + 1) * ROW_GROUP // CHUNK):
            crows = slice(c * CHUNK, (c + 1) * CHUNK)
            u = u_ref[0, crows, :].astype(F32)
            v = vg_ref[0, crows, :]
            for hd in range(HEADS):
                lanes = slice(hd * HEAD_DIM, (hd + 1) * HEAD_DIM)
                s = _dot(ws_ref[hd], v[:, lanes]) + bs_ref[hd]
                ycat_s[crows, lanes] = (u[:, lanes] * s).astype(BF16)

        hsum = hf_ref[0, rows, :] + hb_ref[0, rows, :]
        o = o_ref[0, rows, :].astype(F32)
        for hd in range(HEADS):
            lanes = slice(hd * HEAD_DIM, (hd + 1) * HEAD_DIM)
            hn = _layer_norm(hsum[:, lanes]) * ng_ref[:, lanes]
            ycat_s[rows, GROUP_W + hd * HEAD_DIM:GROUP_W + (hd + 1) * HEAD_DIM] = (
                o[:, lanes] * hn).astype(BF16)

        y = _dot(ycat_s[rows, :], wout_ref[...])
        x1 = x_ref[0, rows, :] + mod(2) * y
        x1_ref[0, rows, :] = x1

        n2 = x1 * lax.rsqrt(jnp.mean(x1 * x1, axis=-1, keepdims=True) + EPS) * fg_ref[...]
        h2 = n2 * (1.0 + mod(4)) + mod(3)
        h2_hi, h2_lo = _split2(h2)
        h2_ref[0, rows, :] = h2_hi

        logits = _dot(h2_hi, wr_hi) + (_dot(h2_lo, wr_hi) + _dot(h2_hi, wr_lo))
        logits = jnp.where(expert_lane, logits, -jnp.inf)
        e = jnp.exp(logits - jnp.max(logits, axis=-1, keepdims=True))
        aff = e / jnp.sum(e, axis=-1, keepdims=True)
        aff_ref[0, :, rows] = aff.T[0:N_EXPERTS, :]


def _postmix(u, vg, o, hf, hb, x, mod, ws, bs, ng, wout, fg, wr, tm):
    b, n, _ = x.shape
    tok = pl.BlockSpec((1, tm, GROUP_W), lambda i, j: (i, j, 0))
    full = lambda a: pl.BlockSpec(a.shape, lambda i, j: (0,) * a.ndim)
    return pl.pallas_call(
        _postmix_kernel,
        grid=(b, n // tm),
        in_specs=[tok, tok, tok, tok, tok,
                  pl.BlockSpec((1, tm, D_MODEL), lambda i, j: (i, j, 0)),
                  full(mod), full(ws), full(bs), full(ng), full(wout), full(fg), full(wr)],
        out_specs=[pl.BlockSpec((1, tm, D_MODEL), lambda i, j: (i, j, 0)),
                   pl.BlockSpec((1, tm, D_MODEL), lambda i, j: (i, j, 0)),
                   pl.BlockSpec((1, N_EXPERTS, tm), lambda i, j: (i, 0, j))],
        out_shape=[jax.ShapeDtypeStruct((b, n, D_MODEL), F32),
                   jax.ShapeDtypeStruct((b, n, D_MODEL), BF16),
                   jax.ShapeDtypeStruct((b, N_EXPERTS, n), F32)],
        scratch_shapes=[pltpu.VMEM((tm, D_MODEL), BF16)],
        compiler_params=_params("arbitrary", "arbitrary"),
        name="postmix",
    )(u, vg, o, hf, hb, x, mod, ws, bs, ng, wout, fg, wr)


def _cumsum_lanes(x, upper):
    carry = jnp.zeros((x.shape[0], 1), F32)
    outs, before = [], []
    for j in range(x.shape[1] // 128):
        before.append(carry)
        c = _dot(x[:, j * 128:(j + 1) * 128].astype(BF16), upper) + carry
        outs.append(c)
        carry = c[:, 127:128]
    return jnp.concatenate(outs, axis=1), before


def _rows_to_lanes(x, fill):
    pad = jnp.full((128 - x.shape[0], 128), fill, F32)
    return jnp.concatenate([x, pad], axis=0).T


def _route_kernel(aff_ref, slot_ref, w_ref, slot_t_ref, first_t_ref, first_ref, *, capacity, n_e):
    aff = aff_ref[...]
    cap = float(capacity)
    thr_bits = jnp.zeros((aff.shape[0], 1), jnp.int32)
    for bit in range(30, -1, -1):
        cand = thr_bits | (1 << bit)
        cnt = jnp.sum(jnp.where(aff >= pltpu.bitcast(cand, F32), 1.0, 0.0), axis=-1, keepdims=True)
        thr_bits = jnp.where(cnt >= cap, cand, thr_bits)
    thr = pltpu.bitcast(thr_bits, F32)
    upper = jnp.where(_iota2((128, 128), 0) <= _iota2((128, 128), 1), 1.0, 0.0).astype(BF16)
    above = jnp.where(aff > thr, 1.0, 0.0)
    tied = jnp.where(aff == thr, 1.0, 0.0)
    need = cap - jnp.sum(above, axis=-1, keepdims=True)
    sel = above + tied * jnp.where(_cumsum_lanes(tied, upper)[0] <= need, 1.0, 0.0)
    count, before = _cumsum_lanes(sel, upper)
    slot = jnp.where(sel > 0.0, count - 1.0, -1.0)
    slot_ref[...] = slot
    w_ref[...] = jnp.where(sel > 0.0, aff, 0.0)

    n_blocks = len(before)
    lane = _iota2((1, 128), 1)
    first = jnp.zeros((aff.shape[0], 128), F32)
    for j in range(n_blocks):
        first = jnp.where(lane == j, before[j], first)
    first_ref[...] = first
    for smp in range(aff.shape[0] // n_e):
        rows = slice(smp * n_e, (smp + 1) * n_e)
        for j in range(n_blocks):
            slot_t_ref[smp, j * 128:(j + 1) * 128, :] = _rows_to_lanes(
                slot[rows, j * 128:(j + 1) * 128], -1.0)
        first_t_ref[smp] = _rows_to_lanes(first[rows, :], 0.0)[0:n_blocks, :]


def _route(aff_t, capacity):
    b, e, n = aff_t.shape
    assert n // 128 <= 128 and e <= 128 and e % 8 == 0
    rows = pl.BlockSpec((b * e, n), lambda i: (0, 0))
    slot, w, slot_t, first_t, first = pl.pallas_call(
        functools.partial(_route_kernel, capacity=capacity, n_e=e),
        grid=(1,),
        in_specs=[rows],
        out_specs=[rows, rows,
                   pl.BlockSpec((b, n, 128), lambda i: (0, 0, 0)),
                   pl.BlockSpec((b, n // 128, 128), lambda i: (0, 0, 0)),
                   pl.BlockSpec((b * e, 128), lambda i: (0, 0))],
        out_shape=[jax.ShapeDtypeStruct((b * e, n), F32)] * 2
        + [jax.ShapeDtypeStruct((b, n, 128), F32), jax.ShapeDtypeStruct((b, n // 128, 128), F32),
           jax.ShapeDtypeStruct((b * e, 128), F32)],
        compiler_params=_params("arbitrary"),
        name="route",
    )(aff_t.reshape(b * e, n))
    return slot.reshape(b, e, n), w.reshape(b, e, n), slot_t, first_t, first.reshape(b, e, 128)


GATHER_TILE = 256
GATHER_ROWS = (64, 128)
GATHER_GROUP = 4


def _gather_kernel(before_ref, fits_ref, h2_ref, slot_ref, w_ref, xe_ref, gate_ref, xe_s, gate_s):
    b = pl.program_id(0)
    g = pl.program_id(1)
    n_g, cap = xe_ref.shape[1], xe_ref.shape[2]
    n = h2_ref.shape[1]
    n_tiles = n // GATHER_TILE

    def windowed(n_rows):
        xe_s[:, 0:16, :] = jnp.zeros((n_g, 16, D_MODEL), BF16)
        gate_s[...] = jnp.zeros_like(gate_s)
        row_id = _iota2((n_rows, 1), 0).astype(F32)
        for j in range(n_tiles):
            starts, onehots = [], []
            for e in range(n_g):
                start = pl.multiple_of((before_ref[b, g * n_g + e, j] // 16) * 16, 16)
                hit = (slot_ref[0, e, j:j + 1, :] - start.astype(F32)) == row_id
                onehots.append(jnp.where(hit, 1.0, 0.0).astype(BF16))
                gate_s[e, pl.ds(start, n_rows), :] += jnp.sum(
                    jnp.where(hit, w_ref[0, e, j:j + 1, :], 0.0), axis=1, keepdims=True)
                starts.append(start)
            rows = _dot(jnp.concatenate(onehots, axis=0),
                        h2_ref[0, j * GATHER_TILE:(j + 1) * GATHER_TILE, :]).astype(BF16)
            for e, start in enumerate(starts):
                r0 = e * n_rows
                xe_s[e, pl.ds(start, 16), :] += rows[r0:r0 + 16]
                xe_s[e, pl.ds(start + 16, n_rows - 16), :] = rows[r0 + 16:r0 + n_rows]
        xe_ref[0] = xe_s[:, 0:cap, :]
        for e in range(n_g):
            gate_ref[0, e] = _cols_to_rows(gate_s[e, 0:cap, :])

    for level, n_rows in enumerate(GATHER_ROWS):
        pl.when(fits_ref[b, g] == level)(functools.partial(windowed, n_rows))

    @pl.when(fits_ref[b, g] == len(GATHER_ROWS))
    def _():
        slot_id = _iota2((cap, 1), 0).astype(F32)
        for e in range(n_g):
            xe = jnp.zeros((cap, D_MODEL), F32)
            gate = jnp.zeros((cap, 1), F32)
            for j in range(n_tiles):
                hit = slot_ref[0, e, j:j + 1, :] == slot_id
                xe = xe + _dot(jnp.where(hit, 1.0, 0.0).astype(BF16),
                               h2_ref[0, j * GATHER_TILE:(j + 1) * GATHER_TILE, :])
                gate = gate + jnp.sum(jnp.where(hit, w_ref[0, e, j:j + 1, :], 0.0), axis=1,
                                      keepdims=True)
            xe_ref[0, e] = xe.astype(BF16)
            gate_ref[0, e] = _cols_to_rows(gate)


def _gather(h2, slot, w, first, capacity):
    b, n, _ = h2.shape
    n_e = slot.shape[1]
    n_tiles = n // GATHER_TILE
    assert n_e % GATHER_GROUP == 0 and GATHER_TILE % 128 == 0 and capacity % 128 == 0
    assert all(r % 16 == 0 and r >= 32 for r in GATHER_ROWS) and list(GATHER_ROWS) == sorted(GATHER_ROWS)
    before = first[:, :, 0:n // 128:GATHER_TILE // 128].astype(jnp.int32)
    after = jnp.concatenate([before[:, :, 1:], jnp.full((b, n_e, 1), capacity, jnp.int32)], axis=2)
    extent = jnp.max(after - (before // 16) * 16, axis=2)
    extent = jnp.max(extent.reshape(b, n_e // GATHER_GROUP, GATHER_GROUP), axis=2)
    fits = sum((extent >= n_rows).astype(jnp.int32) for n_rows in GATHER_ROWS)
    tiles = lambda a: a.reshape(b, n_e, n_tiles, GATHER_TILE)
    rows = pl.BlockSpec((1, GATHER_GROUP, n_tiles, GATHER_TILE), lambda i, g, *_: (i, g, 0, 0))
    return pl.pallas_call(
        _gather_kernel,
        grid_spec=pltpu.PrefetchScalarGridSpec(
            num_scalar_prefetch=2,
            grid=(b, n_e // GATHER_GROUP),
            in_specs=[pl.BlockSpec((1, n, D_MODEL), lambda i, g, *_: (i, 0, 0)), rows, rows],
            out_specs=[pl.BlockSpec((1, GATHER_GROUP, capacity, D_MODEL), lambda i, g, *_: (i, g, 0, 0)),
                       pl.BlockSpec((1, GATHER_GROUP, capacity // 128, 128), lambda i, g, *_: (i, g, 0, 0))],
            scratch_shapes=[pltpu.VMEM((GATHER_GROUP, capacity + max(GATHER_ROWS), D_MODEL), BF16),
                            pltpu.VMEM((GATHER_GROUP, capacity + max(GATHER_ROWS), 1), F32)]),
        out_shape=[jax.ShapeDtypeStruct((b, n_e, capacity, D_MODEL), BF16),
                   jax.ShapeDtypeStruct((b, n_e, capacity // 128, 128), F32)],
        compiler_params=_params("arbitrary", "arbitrary"),
        name="gather",
    )(before, fits, h2, tiles(slot), tiles(w))


def _ffn_kernel(xe_ref, gate_ref, wg_ref, wu_ref, wd_ref, ye_ref, acc_s):
    ft = pl.program_id(1)
    last = pl.num_programs(1) - 1

    def step(first, final):
        wg = wg_ref[0].astype(BF16)
        wu = wu_ref[0].astype(BF16)
        wd = wd_ref[0].astype(BF16)
        for i in range(xe_ref.shape[0]):
            xe = xe_ref[i, 0]
            act = jax.nn.silu(_dot(xe, wg)) * _dot(xe, wu)
            part = _dot(act.astype(BF16), wd)
            total = part if first else acc_s[i] + part
            if final:
                ye_ref[i] = (total * _rows_to_cols(gate_ref[i, 0])).astype(BF16)
            else:
                acc_s[i] = total

    pl.when(ft == 0)(lambda: step(True, False))
    pl.when(jnp.logical_and(ft > 0, ft < last))(lambda: step(False, False))
    pl.when(ft == last)(lambda: step(False, True))


def _ffn(xe, gate, wg, wu, wd, f_tile):
    b, n_e, cap, _ = xe.shape
    assert D_EXPERT // f_tile >= 2
    return pl.pallas_call(
        _ffn_kernel,
        grid=(n_e, D_EXPERT // f_tile),
        in_specs=[pl.BlockSpec((b, 1, cap, D_MODEL), lambda e, f: (0, e, 0, 0)),
                  pl.BlockSpec((b, 1, cap // 128, 128), lambda e, f: (0, e, 0, 0)),
                  pl.BlockSpec((1, D_MODEL, f_tile), lambda e, f: (e, 0, f)),
                  pl.BlockSpec((1, D_MODEL, f_tile), lambda e, f: (e, 0, f)),
                  pl.BlockSpec((1, f_tile, D_MODEL), lambda e, f: (e, f, 0))],
        out_specs=pl.BlockSpec((b, cap, D_MODEL), lambda e, f: (0, e, 0)),
        out_shape=jax.ShapeDtypeStruct((b, n_e * cap, D_MODEL), BF16),
        scratch_shapes=[pltpu.VMEM((b, cap, D_MODEL), F32)],
        compiler_params=_params("arbitrary", "arbitrary"),
        name="ffn",
    )(xe, gate, wg, wu, wd)


COMBINE_WIDE = CHUNK + 16
COMBINE_NARROW = 48


def _combine_kernel(first_ref, narrow_ref, slot_t_ref, first_t_ref, ye_ref, x1_ref, mod_ref, fg_ref,
                    o_ref, acc_s, *, capacity):
    b = pl.program_id(0)
    j = pl.program_id(1)
    n_sub = o_ref.shape[1] // 128
    lane = _iota2((1, 128), 1).astype(F32)

    def scatter(window):
        k_total = N_EXPERTS * window
        for sb in range(n_sub):
            blk = j * n_sub + sb
            slot_t = slot_t_ref[0, sb * 128:(sb + 1) * 128, :]
            start_row = jnp.minimum(jnp.floor(first_t_ref[0, pl.ds(blk, 1), :] * (1.0 / 16.0)) * 16.0,
                                    float(capacity - window))
            k_pos = jnp.where(slot_t >= 0.0, slot_t - start_row + lane * float(window), -1.0)
            cols = []
            for c in range(k_total // 128):
                k_lane = lane + float(128 * c)
                hit = jnp.zeros((128, 128), F32)
                for e in range((128 * c) // window, (128 * c + 127) // window + 1):
                    hit = jnp.where(k_pos[:, e:e + 1] == k_lane, 1.0, hit)
                cols.append(hit.astype(BF16))
            onehot = jnp.concatenate(cols, axis=1)
            rows = []
            for e in range(N_EXPERTS):
                start = jnp.minimum((first_ref[b, blk, e] // 16) * 16, capacity - window)
                rows.append(ye_ref[0, pl.ds(pl.multiple_of(e * capacity + start, 16), window), :])
            acc_s[sb * 128:(sb + 1) * 128, :] = _dot(onehot, jnp.concatenate(rows, axis=0))

    pl.when(narrow_ref[b, j] != 0)(lambda: scatter(COMBINE_NARROW))
    pl.when(narrow_ref[b, j] == 0)(lambda: scatter(COMBINE_WIDE))

    g2 = mod_ref[pl.ds(b, 1), 5 * D_MODEL:6 * D_MODEL]
    x2 = x1_ref[0] + g2 * acc_s[...]
    o_ref[0] = x2 * lax.rsqrt(jnp.mean(x2 * x2, axis=-1, keepdims=True) + EPS) * fg_ref[...]


def _combine(slot_t, first_t, ye, x1, mod, fg, capacity, tm):
    b, n, _ = x1.shape
    n_blocks = n // 128
    for window in (COMBINE_WIDE, COMBINE_NARROW):
        assert (N_EXPERTS * window) % 128 == 0 and window % 16 == 0 and window <= capacity
    assert COMBINE_WIDE >= 128 + 15 and capacity % 16 == 0
    first = first_t[:, :, :N_EXPERTS].astype(jnp.int32)
    after = jnp.concatenate([first[:, 1:], jnp.full((b, 1, N_EXPERTS), capacity, jnp.int32)], axis=1)
    narrow = jnp.all(after - (first // 16) * 16 <= COMBINE_NARROW, axis=2)
    narrow = jnp.all(narrow.reshape(b, n // tm, tm // 128), axis=2).astype(jnp.int32)
    tok = pl.BlockSpec((1, tm, D_MODEL), lambda i, j, *_: (i, j, 0))
    return pl.pallas_call(
        functools.partial(_combine_kernel, capacity=capacity),
        grid_spec=pltpu.PrefetchScalarGridSpec(
            num_scalar_prefetch=2,
            grid=(b, n // tm),
            in_specs=[pl.BlockSpec((1, tm, 128), lambda i, j, *_: (i, j, 0)),
                      pl.BlockSpec((1, n_blocks, 128), lambda i, j, *_: (i, 0, 0)),
                      pl.BlockSpec((1, N_EXPERTS * capacity, D_MODEL), lambda i, j, *_: (i, 0, 0)),
                      tok,
                      pl.BlockSpec(mod.shape, lambda i, j, *_: (0, 0)),
                      pl.BlockSpec((1, D_MODEL), lambda i, j, *_: (0, 0))],
            out_specs=tok,
            scratch_shapes=[pltpu.VMEM((tm, D_MODEL), F32)]),
        out_shape=jax.ShapeDtypeStruct((b, n, D_MODEL), F32),
        compiler_params=_params("arbitrary", "arbitrary"),
        name="combine",
    )(first, narrow, slot_t, first_t, ye, x1, mod, fg)


def kernel(x, c, ctx, c_ctx, w_mod, b_mod, norm_mix_g, w_in, conv_q, conv_k, b_igate, b_fgate,
           gmlp_ws, gmlp_bs, mlstm_norm_g, w_out, norm_ffn_g, w_router, w_gate_e, w_up_e,
           w_down_e, final_g):
    depth = w_mod.shape[0]
    assert depth == 1, "the context stream is only carried as mLSTM states (single layer)"
    batch, seq, _ = x.shape
    assert seq % GRID_W == 0 and seq % CHUNK == 0 and batch + 1 <= MOD_ROWS
    capacity = EC_FACTOR * seq // N_EXPERTS
    ctx_row = batch
    l = 0

    cond = jnp.concatenate([c, c_ctx[None], jnp.zeros((MOD_ROWS - batch - 1, D_MODEL), F32)], axis=0)
    mod = _adaln(cond, w_mod[l], b_mod[l][None])

    row = lambda a: a[None]
    w_gates = jnp.pad(w_in[l][:, MAIN_W:], ((0, 0), (0, GATE_PAD - N_GATES)))
    cols = lambda *blks: jnp.concatenate(
        [w_in[l][:, k * GROUP_W:(k + 1) * GROUP_W] for k in blks] + [w_gates], axis=1).astype(BF16)
    gate_bias = jnp.pad(jnp.concatenate([b_igate[l].reshape(-1), b_fgate[l].reshape(-1)]),
                        (0, GATE_PAD - N_GATES))[None]
    k_scale = HEAD_DIM ** -0.5

    value = (lambda a: a, F32)
    k_c, v_c, gates_c = _inproj(ctx, mod, row(norm_mix_g[l]), cols(K_BLK, V_BLK), conv_k[l][None],
                                (k_scale,), (value,), tm=ctx.shape[1], ctx_row=ctx_row)
    state = _mlstm(None, k_c, v_c, gates_c, gate_bias, None)

    q_l, k_l, u_l, vg_l, o_l, v_l, gates = _inproj(
        x, mod, row(norm_mix_g[l]), cols(Q_BLK, K_BLK, U_BLK, VG_BLK, O_BLK, V_BLK),
        jnp.stack([conv_q[l], conv_k[l]]), (1.0, k_scale),
        ((jax.nn.gelu, BF16), (lambda a: _layer_norm(jax.nn.gelu(a)), BF16), (jax.nn.sigmoid, BF16), value),
        tm=512, ctx_row=None)
    h_f, h_b = _mlstm(q_l, k_l, v_l, gates, gate_bias, state)

    x1, h2, aff_t = _postmix(u_l, vg_l, o_l, h_f, h_b, x, mod, gmlp_ws[l].astype(BF16), gmlp_bs[l][:, :, None],
                             row(mlstm_norm_g[l]), w_out[l].astype(BF16), row(norm_ffn_g[l]),
                             jnp.pad(w_router[l], ((0, 0), (0, 128 - N_EXPERTS))), tm=512)

    slot, gate_w, slot_t, first_t, first = _route(aff_t, capacity)

    xe, gate = _gather(h2, slot, gate_w, first, capacity)
    ye = _ffn(xe, gate, w_gate_e[l], w_up_e[l], w_down_e[l], f_tile=512)
    return _combine(slot_t, first_t, ye, x1, mod, row(final_g), capacity, tm=512)
```

```python
import functools

import jax
import jax.numpy as jnp
from jax import lax
from jax.experimental import pallas as pl
from jax.experimental.pallas import tpu as pltpu

F32 = jnp.float32
BF16 = jnp.bfloat16

LANES = 128
BF16_ROWS = 16
VMEM_LIMIT = 56 * 1024 * 1024

D_MODEL = 1024
GRID_W = 64
CHUNK = 128
HEADS = 4
GROUP_W = D_MODEL // 2
HEAD_DIM = GROUP_W // HEADS
N_EXPERTS = 16
EC_FACTOR = 2
D_EXPERT = 2 * D_MODEL
EPS = 1e-6
N_GATES = 4 * HEADS
GATE_PAD = LANES
MOD_ROWS = 8

U_BLK, VG_BLK, Q_BLK, O_BLK, K_BLK, V_BLK = 0, 1, 2, 3, 4, 5
MAIN_W = 6 * GROUP_W
P_U, P_VG, P_O, P_V = 0, 1, 2, 3

ROW_GROUP = 2 * CHUNK
MLSTM_SAMPLES = 4


def _params(*sem):
    return pltpu.CompilerParams(dimension_semantics=sem, vmem_limit_bytes=VMEM_LIMIT)


def _dot(a, b):
    return jnp.dot(a, b, preferred_element_type=F32)


def _dot_nt(a, b):
    return lax.dot_general(a, b, (((1,), (1,)), ((), ())), preferred_element_type=F32)


def _split2(a):
    hi = a.astype(BF16)
    lo = (a - hi.astype(F32)).astype(BF16)
    return hi, lo


def _dot3(a, b):
    ah, al = _split2(a)
    bh, bl = _split2(b)
    return _dot(ah, bh) + (_dot(al, bh) + _dot(ah, bl))


def _dot_exact01(tri, x):
    x1 = x.astype(BF16)
    r1 = x - x1.astype(F32)
    x2 = r1.astype(BF16)
    x3 = (r1 - x2.astype(F32)).astype(BF16)
    return _dot(tri, x1) + (_dot(tri, x2) + _dot(tri, x3))


def _iota2(shape, dim):
    return lax.broadcasted_iota(jnp.int32, shape, dim)


def _eye():
    return _iota2((LANES, LANES), 0) == _iota2((LANES, LANES), 1)


def _cols_to_rows(col):
    return jnp.concatenate(
        [jnp.sum(jnp.where(_eye(), col[r * LANES:(r + 1) * LANES, :], 0.0), axis=0, keepdims=True)
         for r in range(col.shape[0] // LANES)], axis=0)


def _rows_to_cols(rows):
    return jnp.concatenate(
        [jnp.sum(jnp.where(_eye(), rows[r:r + 1, :], 0.0), axis=1, keepdims=True)
         for r in range(rows.shape[0])], axis=0)


def _adaln_kernel(cond_ref, w_ref, b_ref, o_ref):
    o_ref[...] = _dot3(jax.nn.silu(cond_ref[...]), w_ref[...]) + b_ref[...]


def _adaln(cond, w, b):
    n_out = w.shape[1]
    tn = D_MODEL
    assert n_out % tn == 0
    return pl.pallas_call(
        _adaln_kernel,
        grid=(n_out // tn,),
        in_specs=[pl.BlockSpec((MOD_ROWS, D_MODEL), lambda j: (0, 0)),
                  pl.BlockSpec((D_MODEL, tn), lambda j: (0, j)),
                  pl.BlockSpec((1, tn), lambda j: (0, j))],
        out_specs=pl.BlockSpec((MOD_ROWS, tn), lambda j: (0, j)),
        out_shape=jax.ShapeDtypeStruct((MOD_ROWS, n_out), F32),
        compiler_params=_params("arbitrary"),
        name="adaln",
    )(cond, w, b)


def _inproj_kernel(x_ref, xprev_ref, xnext_ref, mod_ref, g_ref, taps_ref, w_ref, *outs, ctx_row, scales,
                   plain_acts):
    conv_refs, (p_ref, gate_ref) = outs[:len(scales)], outs[len(scales):]
    row = pl.program_id(0) if ctx_row is None else ctx_row
    j = pl.program_id(1)
    sh = mod_ref[pl.ds(row, 1), 0:D_MODEL]
    sc = mod_ref[pl.ds(row, 1), D_MODEL:2 * D_MODEL]

    def modulated(x):
        y = x * lax.rsqrt(jnp.mean(x * x, axis=-1, keepdims=True) + EPS) * g_ref[...]
        return y * (1.0 + sc) + sh

    n_groups = x_ref.shape[1] // ROW_GROUP
    before = modulated(xprev_ref[0]) * (j > 0).astype(F32)
    for r in range(n_groups):
        rows = slice(r * ROW_GROUP, (r + 1) * ROW_GROUP)
        h = modulated(x_ref[0, rows, :])
        if r == n_groups - 1:
            after = modulated(xnext_ref[0]) * (j < pl.num_programs(1) - 1).astype(F32)
        else:
            after = modulated(x_ref[0, (r + 1) * ROW_GROUP:(r + 1) * ROW_GROUP + 8, :])
        ext = jnp.concatenate([before, h, after], axis=0).astype(BF16)
        before = h[ROW_GROUP - 8:, :]
        y = _dot(ext, w_ref[...])
        n_ext, n_conv = ext.shape[0], len(scales) * GROUP_W
        y_here = y[8:8 + ROW_GROUP]
        for k, act in enumerate(plain_acts):
            lanes = slice(n_conv + k * GROUP_W, n_conv + (k + 1) * GROUP_W)
            p_ref[0, rows, k * GROUP_W:(k + 1) * GROUP_W] = act(y_here[:, lanes])
        gate_ref[0, rows, :] = y_here[:, n_conv + p_ref.shape[2]:]
        y_conv = y[:, 0:n_conv]
        y_before = pltpu.roll(y_conv, 1, axis=0)[8:8 + ROW_GROUP]
        y_after = pltpu.roll(y_conv, n_ext - 1, axis=0)[8:8 + ROW_GROUP]
        for s, scale in enumerate(scales):
            lanes = slice(s * GROUP_W, (s + 1) * GROUP_W)
            z = (y_before[:, lanes] * taps_ref[s, 0:1, :] + y_here[:, lanes] * taps_ref[s, 1:2, :]
                 + y_after[:, lanes] * taps_ref[s, 2:3, :])
            conv_refs[s][0, rows, :] = (jax.nn.silu(z) * scale).astype(BF16)


def _inproj(x, mod, g, w, taps, scales, plain_acts, tm, ctx_row):
    b, n, _ = x.shape
    wn = len(plain_acts) * GROUP_W
    assert w.shape[1] == len(scales) * GROUP_W + wn + GATE_PAD
    assert tm % ROW_GROUP == 0 and n % tm == 0 and wn % LANES == 0
    rows8 = tm // 8
    last8 = n // 8 - 1
    full = lambda a: pl.BlockSpec(a.shape, lambda i, j: (0,) * a.ndim)
    conv_spec = pl.BlockSpec((1, tm, GROUP_W), lambda i, j: (i, j, 0))
    return pl.pallas_call(
        functools.partial(_inproj_kernel, ctx_row=ctx_row, scales=tuple(scales),
                          plain_acts=tuple(plain_acts)),
        grid=(b, n // tm),
        in_specs=[pl.BlockSpec((1, tm, D_MODEL), lambda i, j: (i, j, 0)),
                  pl.BlockSpec((1, 8, D_MODEL), lambda i, j: (i, jnp.maximum(j * rows8 - 1, 0), 0)),
                  pl.BlockSpec((1, 8, D_MODEL), lambda i, j: (i, jnp.minimum((j + 1) * rows8, last8), 0)),
                  full(mod), full(g), full(taps), full(w)],
        out_specs=[conv_spec] * len(scales)
        + [pl.BlockSpec((1, tm, wn), lambda i, j: (i, j, 0)),
           pl.BlockSpec((1, tm, GATE_PAD), lambda i, j: (i, j, 0))],
        out_shape=[jax.ShapeDtypeStruct((b, n, GROUP_W), BF16)] * len(scales)
        + [jax.ShapeDtypeStruct((b, n, wn), F32), jax.ShapeDtypeStruct((b, n, GATE_PAD), F32)],
        compiler_params=_params("arbitrary", "arbitrary"),
        name="inproj",
    )(x, x, x, mod, g, taps, w)


def _mlstm_chunk(per_dir, bias_ref, c_s, n_s, m_s, h_refs):
    with_h = h_refs[0] is not None
    i0 = _iota2((CHUNK, CHUNK), 0)
    i1 = _iota2((CHUNK, CHUNK), 1)

    units = []
    for smp, d in [(smp, d) for smp in range(c_s.shape[0]) for d in range(2)]:
        q_ref, k_ref, v_ref, g_ref = per_dir[d]
        sees_ts = (i1 <= i0) if d == 0 else (i1 >= i0)
        sees_st = (i0 <= i1) if d == 0 else (i0 >= i1)
        tri = jnp.where(sees_ts, 1.0, 0.0).astype(BF16)
        gates = g_ref[smp] + bias_ref[...]
        bcum = _dot_exact01(tri, jax.nn.log_sigmoid(gates))
        for hd in range(HEADS):
            lanes = slice(hd * HEAD_DIM, (hd + 1) * HEAD_DIM)
            c_prev = c_s[smp, d, hd]
            n_prev = n_s[smp, d, hd]
            k = k_ref[smp, :, lanes]
            st = dict(smp=smp, d=d, hd=hd, lanes=lanes, c_prev=c_prev, n_prev=n_prev, k=k,
                      sees_st=sees_st, gates=gates, bcum=bcum, v=v_ref[smp, :, lanes])
            if with_h:
                lhs = jnp.concatenate([k, c_prev.astype(BF16),
                                       jnp.broadcast_to(n_prev, (BF16_ROWS, HEAD_DIM)).astype(BF16)], axis=0)
                st["prod"] = _dot_nt(lhs, q_ref[smp, :, lanes])
            units.append(st)

    rows = {}
    for st in units:
        smp, d, hd = st["smp"], st["d"], st["hd"]
        if (smp, d) not in rows:
            rows[smp, d] = (st["gates"].T, st["bcum"].T)
        gates_t, bcum_t = rows[smp, d]
        ci = d * HEADS + hd
        cf = 2 * HEADS + d * HEADS + hd
        last = CHUNK - 1 if d == 0 else 0
        li_row = gates_t[ci:ci + 1, :]
        bc_row = bcum_t[cf:cf + 1, :]
        b_last = bc_row[:, last:last + 1]
        m_prev = m_s[smp, d, hd]
        v_t = st["v"].T
        a_row = b_last - bc_row + li_row
        m_new = jnp.maximum(b_last + m_prev, jnp.max(a_row, axis=-1, keepdims=True))
        w_row = jnp.exp(a_row - m_new)
        lhs = jnp.concatenate([v_t * w_row, jnp.broadcast_to(w_row, (BF16_ROWS, CHUNK))], axis=0)
        st["upd"] = _dot(lhs.astype(BF16), st["k"])
        st.update(bc_row=bc_row, m_prev=m_prev, m_new=m_new, v_t=v_t,
                  decay=jnp.exp(b_last + m_prev - m_new),
                  u_col=st["gates"][:, ci:ci + 1] - st["bcum"][:, cf:cf + 1])

    if with_h:
        for st in units:
            prod, bc_row = st["prod"], st["bc_row"]
            g = bc_row + st["m_prev"]
            dmat = jnp.where(st["sees_st"], st["u_col"] + bc_row, -jnp.inf)
            m_t = jnp.maximum(g, jnp.max(dmat, axis=0, keepdims=True))
            inter = jnp.exp(g - m_t)
            s = prod[0:CHUNK] * jnp.exp(dmat - m_t)
            st["pv"] = _dot(st["v_t"].astype(BF16), s.astype(BF16))
            st["num0"] = inter * prod[CHUNK:2 * CHUNK]
            den = inter * prod[2 * CHUNK:2 * CHUNK + 1] + jnp.sum(s, axis=0, keepdims=True)
            st["scale"] = 1.0 / jnp.maximum(jnp.abs(den), jnp.exp(-m_t))

    for st in units:
        smp, d, hd = st["smp"], st["d"], st["hd"]
        if with_h:
            h_refs[d][smp, :, st["lanes"]] = ((st["num0"] + st["pv"]) * st["scale"]).T
        c_s[smp, d, hd] = st["decay"] * st["c_prev"] + st["upd"][0:HEAD_DIM]
        n_s[smp, d, hd] = st["decay"] * st["n_prev"] + st["upd"][HEAD_DIM:HEAD_DIM + 1]
        m_s[smp, d, hd] = st["m_new"]


def _mlstm_kernel(*refs, n_chunks, with_h):
    refs = list(refs)
    take = lambda n: [refs.pop(0) for _ in range(n)]
    per_dir = []
    for _ in range(2):
        q_ref = take(1)[0] if with_h else None
        k_ref, v_ref, g_ref = take(3)
        per_dir.append((q_ref, k_ref, v_ref, g_ref))
    bias_ref = take(1)[0]
    if with_h:
        c0_ref, n0_ref, m0_ref = take(3)
        h_refs = take(2)
        c_out = n_out = m_out = None
    else:
        h_refs = [None, None]
        c_out, n_out, m_out = take(3)
    c_s, n_s, m_s = take(3)

    j = pl.program_id(1)

    @pl.when(j == 0)
    def _():
        if with_h:
            c_s[...] = c0_ref[...]
            n_s[...] = n0_ref[...]
            m_s[...] = m0_ref[...]
        else:
            c_s[...] = jnp.zeros_like(c_s)
            n_s[...] = jnp.zeros_like(n_s)
            m_s[...] = jnp.zeros_like(m_s)

    _mlstm_chunk(per_dir, bias_ref, c_s, n_s, m_s, h_refs)

    if not with_h:
        @pl.when(j == n_chunks - 1)
        def _():
            c_out[...] = c_s[...]
            n_out[...] = n_s[...]
            m_out[...] = m_s[...]


def _mlstm(q, k, p, v_blk, gates, bias, state):
    b, n, _ = k.shape
    n_chunks = n // CHUNK
    with_h = q is not None
    ns = MLSTM_SAMPLES if b % MLSTM_SAMPLES == 0 else 1

    in_specs, args = [], []
    for d in range(2):
        c = (lambda j: j) if d == 0 else (lambda j: n_chunks - 1 - j)
        tok = pl.BlockSpec((ns, CHUNK, GROUP_W), lambda i, j, c=c: (i, c(j), 0))
        if with_h:
            in_specs.append(tok)
            args.append(q)
        in_specs += [tok,
                     pl.BlockSpec((ns, CHUNK, GROUP_W), lambda i, j, c=c: (i, c(j), v_blk)),
                     pl.BlockSpec((ns, CHUNK, GATE_PAD), lambda i, j, c=c: (i, c(j), 0))]
        args += [k, p, gates]
    in_specs.append(pl.BlockSpec((1, GATE_PAD), lambda i, j: (0, 0)))
    args.append(bias)

    c_shape = (2, HEADS, HEAD_DIM, HEAD_DIM)
    v_shape = (2, HEADS, 1, HEAD_DIM)
    c_spec = pl.BlockSpec((ns,) + c_shape, lambda i, j: (i, 0, 0, 0, 0))
    v_spec = pl.BlockSpec((ns,) + v_shape, lambda i, j: (i, 0, 0, 0, 0))
    if with_h:
        in_specs += [c_spec, v_spec, v_spec]
        args += list(state)
        out_specs = [pl.BlockSpec((ns, CHUNK, GROUP_W), lambda i, j: (i, j, 0)),
                     pl.BlockSpec((ns, CHUNK, GROUP_W), lambda i, j: (i, n_chunks - 1 - j, 0))]
        out_shape = [jax.ShapeDtypeStruct((b, n, GROUP_W), F32)] * 2
    else:
        out_specs = [c_spec, v_spec, v_spec]
        out_shape = [jax.ShapeDtypeStruct((b,) + c_shape, F32),
                     jax.ShapeDtypeStruct((b,) + v_shape, F32),
                     jax.ShapeDtypeStruct((b,) + v_shape, F32)]
    return pl.pallas_call(
        functools.partial(_mlstm_kernel, n_chunks=n_chunks, with_h=with_h),
        grid=(b // ns, n_chunks),
        in_specs=in_specs,
        out_specs=out_specs,
        out_shape=out_shape,
        scratch_shapes=[pltpu.VMEM((ns,) + c_shape, F32), pltpu.VMEM((ns,) + v_shape, F32),
                        pltpu.VMEM((ns,) + v_shape, F32)],
        compiler_params=_params("arbitrary", "arbitrary"),
        name="mlstm" if with_h else "mlstm_ctx_state",
    )(*args)


def _layer_norm(x):
    mu = jnp.mean(x, axis=-1, keepdims=True)
    var = jnp.mean(jnp.square(x - mu), axis=-1, keepdims=True)
    return (x - mu) * lax.rsqrt(var + EPS)


def _postmix_kernel(u_ref, vg_ref, o_ref, hf_ref, hb_ref, x_ref, mod_ref, ws_ref, bs_ref, ng_ref,
                    wout_ref, fg_ref, wr_ref, x1_ref, h2_ref, aff_ref, ycat_s):
    tm = x_ref.shape[1]
    b = pl.program_id(0)
    mod = lambda k: mod_ref[pl.ds(b, 1), k * D_MODEL:(k + 1) * D_MODEL]
    wr_hi, wr_lo = _split2(wr_ref[...])
    expert_lane = _iota2((ROW_GROUP, LANES), 1) < N_EXPERTS

    for r in range(tm // ROW_GROUP):
        rows = slice(r * ROW_GROUP, (r + 1) * ROW_GROUP)

        for c in range(r * ROW_GROUP // CHUNK, (r + 1) * ROW_GROUP // CHUNK):
            crows = slice(c * CHUNK, (c + 1) * CHUNK)
            u = u_ref[0, crows, :]
            v = vg_ref[0, crows, :].astype(BF16)
            for hd in range(HEADS):
                lanes = slice(hd * HEAD_DIM, (hd + 1) * HEAD_DIM)
                s = _dot(ws_ref[hd], v[:, lanes]) + bs_ref[hd]
                ycat_s[crows, lanes] = (u[:, lanes] * s).astype(BF16)

        hsum = hf_ref[0, rows, :] + hb_ref[0, rows, :]
        o = o_ref[0, rows, :]
        for hd in range(HEADS):
            lanes = slice(hd * HEAD_DIM, (hd + 1) * HEAD_DIM)
            hn = _layer_norm(hsum[:, lanes]) * ng_ref[:, lanes]
            ycat_s[rows, GROUP_W + hd * HEAD_DIM:GROUP_W + (hd + 1) * HEAD_DIM] = (
                o[:, lanes] * hn).astype(BF16)

        y = _dot(ycat_s[rows, :], wout_ref[...])
        x1 = x_ref[0, rows, :] + mod(2) * y
        x1_ref[0, rows, :] = x1

        n2 = x1 * lax.rsqrt(jnp.mean(x1 * x1, axis=-1, keepdims=True) + EPS) * fg_ref[...]
        h2 = n2 * (1.0 + mod(4)) + mod(3)
        h2_hi, h2_lo = _split2(h2)
        h2_ref[0, rows, :] = h2_hi

        logits = _dot(h2_hi, wr_hi) + (_dot(h2_lo, wr_hi) + _dot(h2_hi, wr_lo))
        logits = jnp.where(expert_lane, logits, -jnp.inf)
        e = jnp.exp(logits - jnp.max(logits, axis=-1, keepdims=True))
        aff = e / jnp.sum(e, axis=-1, keepdims=True)
        aff_ref[0, :, rows] = aff.T[0:N_EXPERTS, :]


def _postmix(p, hf, hb, x, mod, ws, bs, ng, wout, fg, wr, tm):
    b, n, _ = x.shape
    tok = lambda blk: pl.BlockSpec((1, tm, GROUP_W), lambda i, j: (i, j, blk))
    full = lambda a: pl.BlockSpec(a.shape, lambda i, j: (0,) * a.ndim)
    return pl.pallas_call(
        _postmix_kernel,
        grid=(b, n // tm),
        in_specs=[tok(P_U), tok(P_VG), tok(P_O), tok(0), tok(0),
                  pl.BlockSpec((1, tm, D_MODEL), lambda i, j: (i, j, 0)),
                  full(mod), full(ws), full(bs), full(ng), full(wout), full(fg), full(wr)],
        out_specs=[pl.BlockSpec((1, tm, D_MODEL), lambda i, j: (i, j, 0)),
                   pl.BlockSpec((1, tm, D_MODEL), lambda i, j: (i, j, 0)),
                   pl.BlockSpec((1, N_EXPERTS, tm), lambda i, j: (i, 0, j))],
        out_shape=[jax.ShapeDtypeStruct((b, n, D_MODEL), F32),
                   jax.ShapeDtypeStruct((b, n, D_MODEL), BF16),
                   jax.ShapeDtypeStruct((b, N_EXPERTS, n), F32)],
        scratch_shapes=[pltpu.VMEM((tm, D_MODEL), BF16)],
        compiler_params=_params("arbitrary", "arbitrary"),
        name="postmix",
    )(p, p, p, hf, hb, x, mod, ws, bs, ng, wout, fg, wr)


def _cumsum_lanes(x, upper):
    carry = jnp.zeros((x.shape[0], 1), F32)
    outs, before = [], []
    for j in range(x.shape[1] // LANES):
        before.append(carry)
        c = _dot(x[:, j * LANES:(j + 1) * LANES].astype(BF16), upper) + carry
        outs.append(c)
        carry = c[:, LANES - 1:LANES]
    return jnp.concatenate(outs, axis=1), before


def _rows_to_lanes(x, fill):
    pad = jnp.full((LANES - x.shape[0], LANES), fill, F32)
    return jnp.concatenate([x, pad], axis=0).T


def _route_kernel(aff_ref, slot_ref, w_ref, slot_t_ref, first_t_ref, first_ref, *, capacity, n_e):
    aff = aff_ref[...]
    cap = float(capacity)
    thr_bits = jnp.zeros((aff.shape[0], 1), jnp.int32)
    for bit in range(30, -1, -1):
        cand = thr_bits | (1 << bit)
        cnt = jnp.sum(jnp.where(aff >= pltpu.bitcast(cand, F32), 1.0, 0.0), axis=-1, keepdims=True)
        thr_bits = jnp.where(cnt >= cap, cand, thr_bits)
    thr = pltpu.bitcast(thr_bits, F32)
    upper = jnp.where(_iota2((LANES, LANES), 0) <= _iota2((LANES, LANES), 1), 1.0, 0.0).astype(BF16)
    above = jnp.where(aff > thr, 1.0, 0.0)
    tied = jnp.where(aff == thr, 1.0, 0.0)
    need = cap - jnp.sum(above, axis=-1, keepdims=True)
    sel = above + tied * jnp.where(_cumsum_lanes(tied, upper)[0] <= need, 1.0, 0.0)
    count, before = _cumsum_lanes(sel, upper)
    slot = jnp.where(sel > 0.0, count - 1.0, -1.0)
    slot_ref[...] = slot
    w_ref[...] = jnp.where(sel > 0.0, aff, 0.0)

    n_blocks = len(before)
    lane = _iota2((1, LANES), 1)
    first = jnp.zeros((aff.shape[0], LANES), F32)
    for j in range(n_blocks):
        first = jnp.where(lane == j, before[j], first)
    first_ref[...] = first
    for smp in range(aff.shape[0] // n_e):
        rows = slice(smp * n_e, (smp + 1) * n_e)
        for j in range(n_blocks):
            slot_t_ref[smp, j * LANES:(j + 1) * LANES, :] = _rows_to_lanes(
                slot[rows, j * LANES:(j + 1) * LANES], -1.0)
        first_t_ref[smp] = _rows_to_lanes(first[rows, :], 0.0)[0:n_blocks, :]


def _route(aff_t, capacity):
    b, e, n = aff_t.shape
    assert n // LANES <= LANES and e <= LANES and e % 8 == 0
    rows = pl.BlockSpec((b * e, n), lambda i: (0, 0))
    slot, w, slot_t, first_t, first = pl.pallas_call(
        functools.partial(_route_kernel, capacity=capacity, n_e=e),
        grid=(1,),
        in_specs=[rows],
        out_specs=[rows, rows,
                   pl.BlockSpec((b, n, LANES), lambda i: (0, 0, 0)),
                   pl.BlockSpec((b, n // LANES, LANES), lambda i: (0, 0, 0)),
                   pl.BlockSpec((b * e, LANES), lambda i: (0, 0))],
        out_shape=[jax.ShapeDtypeStruct((b * e, n), F32)] * 2
        + [jax.ShapeDtypeStruct((b, n, LANES), F32), jax.ShapeDtypeStruct((b, n // LANES, LANES), F32),
           jax.ShapeDtypeStruct((b * e, LANES), F32)],
        compiler_params=_params("arbitrary"),
        name="route",
    )(aff_t.reshape(b * e, n))
    return slot.reshape(b, e, n), w.reshape(b, e, n), slot_t, first_t, first.reshape(b, e, LANES)


GATHER_TILE = 256
GATHER_ROWS = (64, 128)
GATHER_GROUP = 4


def _gather_kernel(before_ref, fits_ref, h2_ref, slot_ref, w_ref, xe_ref, gate_ref, xe_s, gate_s):
    b = pl.program_id(0)
    g = pl.program_id(1)
    n_g, cap = xe_ref.shape[1], xe_ref.shape[2]
    n = h2_ref.shape[1]
    n_tiles = n // GATHER_TILE
    head = BF16_ROWS

    def windowed(n_rows):
        xe_s[:, 0:head, :] = jnp.zeros((n_g, head, D_MODEL), BF16)
        gate_s[...] = jnp.zeros_like(gate_s)
        row_id = _iota2((n_rows, 1), 0).astype(F32)
        for j in range(n_tiles):
            starts, onehots = [], []
            for e in range(n_g):
                start = pl.multiple_of((before_ref[b, g * n_g + e, j] // head) * head, head)
                hit = (slot_ref[0, e, j:j + 1, :] - start.astype(F32)) == row_id
                onehots.append(jnp.where(hit, 1.0, 0.0).astype(BF16))
                gate_s[e, pl.ds(start, n_rows), :] += jnp.sum(
                    jnp.where(hit, w_ref[0, e, j:j + 1, :], 0.0), axis=1, keepdims=True)
                starts.append(start)
            rows = _dot(jnp.concatenate(onehots, axis=0),
                        h2_ref[0, j * GATHER_TILE:(j + 1) * GATHER_TILE, :]).astype(BF16)
            for e, start in enumerate(starts):
                r0 = e * n_rows
                xe_s[e, pl.ds(start, head), :] += rows[r0:r0 + head]
                xe_s[e, pl.ds(start + head, n_rows - head), :] = rows[r0 + head:r0 + n_rows]
        xe_ref[0] = xe_s[:, 0:cap, :]
        for e in range(n_g):
            gate_ref[0, e] = _cols_to_rows(gate_s[e, 0:cap, :])

    for level, n_rows in enumerate(GATHER_ROWS):
        pl.when(fits_ref[b, g] == level)(functools.partial(windowed, n_rows))

    @pl.when(fits_ref[b, g] == len(GATHER_ROWS))
    def _():
        slot_id = _iota2((cap, 1), 0).astype(F32)
        for e in range(n_g):
            xe = jnp.zeros((cap, D_MODEL), F32)
            gate = jnp.zeros((cap, 1), F32)
            for j in range(n_tiles):
                hit = slot_ref[0, e, j:j + 1, :] == slot_id
                xe = xe + _dot(jnp.where(hit, 1.0, 0.0).astype(BF16),
                               h2_ref[0, j * GATHER_TILE:(j + 1) * GATHER_TILE, :])
                gate = gate + jnp.sum(jnp.where(hit, w_ref[0, e, j:j + 1, :], 0.0), axis=1,
                                      keepdims=True)
            xe_ref[0, e] = xe.astype(BF16)
            gate_ref[0, e] = _cols_to_rows(gate)


def _gather(h2, slot, w, first, capacity):
    b, n, _ = h2.shape
    n_e = slot.shape[1]
    n_tiles = n // GATHER_TILE
    assert n_e % GATHER_GROUP == 0 and GATHER_TILE % LANES == 0 and capacity % LANES == 0
    assert all(r % BF16_ROWS == 0 and r >= 2 * BF16_ROWS for r in GATHER_ROWS)
    assert list(GATHER_ROWS) == sorted(GATHER_ROWS)
    before = first[:, :, 0:n // LANES:GATHER_TILE // LANES].astype(jnp.int32)
    after = jnp.concatenate([before[:, :, 1:], jnp.full((b, n_e, 1), capacity, jnp.int32)], axis=2)
    extent = jnp.max(after - (before // BF16_ROWS) * BF16_ROWS, axis=2)
    extent = jnp.max(extent.reshape(b, n_e // GATHER_GROUP, GATHER_GROUP), axis=2)
    fits = sum((extent >= n_rows).astype(jnp.int32) for n_rows in GATHER_ROWS)
    tiles = lambda a: a.reshape(b, n_e, n_tiles, GATHER_TILE)
    rows = pl.BlockSpec((1, GATHER_GROUP, n_tiles, GATHER_TILE), lambda i, g, *_: (i, g, 0, 0))
    return pl.pallas_call(
        _gather_kernel,
        grid_spec=pltpu.PrefetchScalarGridSpec(
            num_scalar_prefetch=2,
            grid=(b, n_e // GATHER_GROUP),
            in_specs=[pl.BlockSpec((1, n, D_MODEL), lambda i, g, *_: (i, 0, 0)), rows, rows],
            out_specs=[pl.BlockSpec((1, GATHER_GROUP, capacity, D_MODEL), lambda i, g, *_: (i, g, 0, 0)),
                       pl.BlockSpec((1, GATHER_GROUP, capacity // LANES, LANES),
                                    lambda i, g, *_: (i, g, 0, 0))],
            scratch_shapes=[pltpu.VMEM((GATHER_GROUP, capacity + max(GATHER_ROWS), D_MODEL), BF16),
                            pltpu.VMEM((GATHER_GROUP, capacity + max(GATHER_ROWS), 1), F32)]),
        out_shape=[jax.ShapeDtypeStruct((b, n_e, capacity, D_MODEL), BF16),
                   jax.ShapeDtypeStruct((b, n_e, capacity // LANES, LANES), F32)],
        compiler_params=_params("arbitrary", "arbitrary"),
        name="gather",
    )(before, fits, h2, tiles(slot), tiles(w))


def _ffn_kernel(xe_ref, gate_ref, wg_ref, wu_ref, wd_ref, ye_ref, acc_s):
    ft = pl.program_id(1)
    last = pl.num_programs(1) - 1

    def step(first, final):
        wg = wg_ref[0].astype(BF16)
        wu = wu_ref[0].astype(BF16)
        wd = wd_ref[0].astype(BF16)
        for i in range(xe_ref.shape[0]):
            xe = xe_ref[i, 0]
            act = jax.nn.silu(_dot(xe, wg)) * _dot(xe, wu)
            part = _dot(act.astype(BF16), wd)
            total = part if first else acc_s[i] + part
            if final:
                ye_ref[i] = (total * _rows_to_cols(gate_ref[i, 0])).astype(BF16)
            else:
                acc_s[i] = total

    pl.when(ft == 0)(lambda: step(True, False))
    pl.when(jnp.logical_and(ft > 0, ft < last))(lambda: step(False, False))
    pl.when(ft == last)(lambda: step(False, True))


def _ffn(xe, gate, wg, wu, wd, f_tile):
    b, n_e, cap, _ = xe.shape
    assert D_EXPERT // f_tile >= 2
    return pl.pallas_call(
        _ffn_kernel,
        grid=(n_e, D_EXPERT // f_tile),
        in_specs=[pl.BlockSpec((b, 1, cap, D_MODEL), lambda e, f: (0, e, 0, 0)),
                  pl.BlockSpec((b, 1, cap // LANES, LANES), lambda e, f: (0, e, 0, 0)),
                  pl.BlockSpec((1, D_MODEL, f_tile), lambda e, f: (e, 0, f)),
                  pl.BlockSpec((1, D_MODEL, f_tile), lambda e, f: (e, 0, f)),
                  pl.BlockSpec((1, f_tile, D_MODEL), lambda e, f: (e, f, 0))],
        out_specs=pl.BlockSpec((b, cap, D_MODEL), lambda e, f: (0, e, 0)),
        out_shape=jax.ShapeDtypeStruct((b, n_e * cap, D_MODEL), BF16),
        scratch_shapes=[pltpu.VMEM((b, cap, D_MODEL), F32)],
        compiler_params=_params("arbitrary", "arbitrary"),
        name="ffn",
    )(xe, gate, wg, wu, wd)


COMBINE_WIDE = CHUNK + BF16_ROWS
COMBINE_NARROW = 48


def _combine_kernel(first_ref, narrow_ref, slot_t_ref, first_t_ref, ye_ref, x1_ref, mod_ref, fg_ref,
                    o_ref, acc_s, *, capacity):
    b = pl.program_id(0)
    j = pl.program_id(1)
    n_sub = o_ref.shape[1] // CHUNK
    lane = _iota2((1, LANES), 1).astype(F32)

    def scatter(window):
        k_total = N_EXPERTS * window
        for sb in range(n_sub):
            blk = j * n_sub + sb
            slot_t = slot_t_ref[0, sb * CHUNK:(sb + 1) * CHUNK, :]
            start_row = jnp.minimum(
                jnp.floor(first_t_ref[0, pl.ds(blk, 1), :] * (1.0 / BF16_ROWS)) * float(BF16_ROWS),
                float(capacity - window))
            k_pos = jnp.where(slot_t >= 0.0, slot_t - start_row + lane * float(window), -1.0)
            cols = []
            for c in range(k_total // LANES):
                k_lane = lane + float(LANES * c)
                hit = jnp.zeros((CHUNK, LANES), F32)
                for e in range((LANES * c) // window, (LANES * c + LANES - 1) // window + 1):
                    hit = jnp.where(k_pos[:, e:e + 1] == k_lane, 1.0, hit)
                cols.append(hit.astype(BF16))
            onehot = jnp.concatenate(cols, axis=1)
            rows = []
            for e in range(N_EXPERTS):
                start = jnp.minimum((first_ref[b, blk, e] // BF16_ROWS) * BF16_ROWS, capacity - window)
                rows.append(ye_ref[0, pl.ds(pl.multiple_of(e * capacity + start, BF16_ROWS), window), :])
            acc_s[sb * CHUNK:(sb + 1) * CHUNK, :] = _dot(onehot, jnp.concatenate(rows, axis=0))

    pl.when(narrow_ref[b, j] != 0)(lambda: scatter(COMBINE_NARROW))
    pl.when(narrow_ref[b, j] == 0)(lambda: scatter(COMBINE_WIDE))

    g2 = mod_ref[pl.ds(b, 1), 5 * D_MODEL:6 * D_MODEL]
    x2 = x1_ref[0] + g2 * acc_s[...]
    o_ref[0] = x2 * lax.rsqrt(jnp.mean(x2 * x2, axis=-1, keepdims=True) + EPS) * fg_ref[...]


def _combine(slot_t, first_t, ye, x1, mod, fg, capacity, tm):
    b, n, _ = x1.shape
    n_blocks = n // CHUNK
    assert CHUNK == LANES and N_EXPERTS <= LANES and tm % CHUNK == 0
    for window in (COMBINE_WIDE, COMBINE_NARROW):
        assert (N_EXPERTS * window) % LANES == 0 and window % BF16_ROWS == 0 and window <= capacity
    assert COMBINE_WIDE >= CHUNK + BF16_ROWS - 1 and capacity % BF16_ROWS == 0
    first = first_t[:, :, :N_EXPERTS].astype(jnp.int32)
    after = jnp.concatenate([first[:, 1:], jnp.full((b, 1, N_EXPERTS), capacity, jnp.int32)], axis=1)
    narrow = jnp.all(after - (first // BF16_ROWS) * BF16_ROWS <= COMBINE_NARROW, axis=2)
    narrow = jnp.all(narrow.reshape(b, n // tm, tm // CHUNK), axis=2).astype(jnp.int32)
    tok = pl.BlockSpec((1, tm, D_MODEL), lambda i, j, *_: (i, j, 0))
    return pl.pallas_call(
        functools.partial(_combine_kernel, capacity=capacity),
        grid_spec=pltpu.PrefetchScalarGridSpec(
            num_scalar_prefetch=2,
            grid=(b, n // tm),
            in_specs=[pl.BlockSpec((1, tm, LANES), lambda i, j, *_: (i, j, 0)),
                      pl.BlockSpec((1, n_blocks, LANES), lambda i, j, *_: (i, 0, 0)),
                      pl.BlockSpec((1, N_EXPERTS * capacity, D_MODEL), lambda i, j, *_: (i, 0, 0)),
                      tok,
                      pl.BlockSpec(mod.shape, lambda i, j, *_: (0, 0)),
                      pl.BlockSpec((1, D_MODEL), lambda i, j, *_: (0, 0))],
            out_specs=tok,
            scratch_shapes=[pltpu.VMEM((tm, D_MODEL), F32)]),
        out_shape=jax.ShapeDtypeStruct((b, n, D_MODEL), F32),
        compiler_params=_params("arbitrary", "arbitrary"),
        name="combine",
    )(first, narrow, slot_t, first_t, ye, x1, mod, fg)


def kernel(x, c, ctx, c_ctx, w_mod, b_mod, norm_mix_g, w_in, conv_q, conv_k, b_igate, b_fgate,
           gmlp_ws, gmlp_bs, mlstm_norm_g, w_out, norm_ffn_g, w_router, w_gate_e, w_up_e,
           w_down_e, final_g):
    depth = w_mod.shape[0]
    assert depth == 1, "the context stream is only carried as mLSTM states (single layer)"
    batch, seq, _ = x.shape
    assert seq % GRID_W == 0 and seq % CHUNK == 0 and batch + 1 <= MOD_ROWS
    capacity = EC_FACTOR * seq // N_EXPERTS
    ctx_row = batch
    l = 0

    cond = jnp.concatenate([c, c_ctx[None], jnp.zeros((MOD_ROWS - batch - 1, D_MODEL), F32)], axis=0)
    mod = _adaln(cond, w_mod[l], b_mod[l][None])

    row = lambda a: a[None]
    w_gates = jnp.pad(w_in[l][:, MAIN_W:], ((0, 0), (0, GATE_PAD - N_GATES)))
    cols = lambda *blks: jnp.concatenate(
        [w_in[l][:, k * GROUP_W:(k + 1) * GROUP_W] for k in blks] + [w_gates], axis=1).astype(BF16)
    gate_bias = jnp.pad(jnp.concatenate([b_igate[l].reshape(-1), b_fgate[l].reshape(-1)]),
                        (0, GATE_PAD - N_GATES))[None]
    k_scale = HEAD_DIM ** -0.5

    identity = lambda a: a
    k_c, v_c, gates_c = _inproj(ctx, mod, row(norm_mix_g[l]), cols(K_BLK, V_BLK), conv_k[l][None],
                                (k_scale,), (identity,), tm=ctx.shape[1], ctx_row=ctx_row)
    state = _mlstm(None, k_c, v_c, 0, gates_c, gate_bias, None)

    q_l, k_l, p, gates = _inproj(x, mod, row(norm_mix_g[l]),
                                 cols(Q_BLK, K_BLK, U_BLK, VG_BLK, O_BLK, V_BLK),
                                 jnp.stack([conv_q[l], conv_k[l]]), (1.0, k_scale),
                                 (jax.nn.gelu, lambda a: _layer_norm(jax.nn.gelu(a)), jax.nn.sigmoid, identity),
                                 tm=512, ctx_row=None)
    h_f, h_b = _mlstm(q_l, k_l, p, P_V, gates, gate_bias, state)

    x1, h2, aff_t = _postmix(p, h_f, h_b, x, mod, gmlp_ws[l].astype(BF16), gmlp_bs[l][:, :, None],
                             row(mlstm_norm_g[l]), w_out[l].astype(BF16), row(norm_ffn_g[l]),
                             jnp.pad(w_router[l], ((0, 0), (0, LANES - N_EXPERTS))), tm=512)

    slot, gate_w, slot_t, first_t, first = _route(aff_t, capacity)

    xe, gate = _gather(h2, slot, gate_w, first, capacity)
    ye = _ffn(xe, gate, w_gate_e[l], w_up_e[l], w_down_e[l], f_tile=512)
    return _combine(slot_t, first_t, ye, x1, mod, row(final_g), capacity, tm=512)
```

```python
import functools

import jax
import jax.numpy as jnp
from jax import lax
from jax.experimental import pallas as pl
from jax.experimental.pallas import tpu as pltpu

F32 = jnp.float32
BF16 = jnp.bfloat16

LANES = 128
BF16_ROWS = 16
VMEM_LIMIT = 56 * 1024 * 1024

D_MODEL = 1024
GRID_W = 64
CHUNK = 128
HEADS = 4
GROUP_W = D_MODEL // 2
HEAD_DIM = GROUP_W // HEADS
N_EXPERTS = 16
EC_FACTOR = 2
D_EXPERT = 2 * D_MODEL
EPS = 1e-6
N_GATES = 4 * HEADS
GATE_PAD = LANES
MOD_ROWS = 8

U_BLK, VG_BLK, Q_BLK, O_BLK, K_BLK, V_BLK = 0, 1, 2, 3, 4, 5
MAIN_W = 6 * GROUP_W
P_U, P_VG, P_O, P_V = 0, 1, 2, 3

ROW_GROUP = 2 * CHUNK
MLSTM_SAMPLES = 4


def _params(*sem):
    return pltpu.CompilerParams(dimension_semantics=sem, vmem_limit_bytes=VMEM_LIMIT)


def _dot(a, b):
    return jnp.dot(a, b, preferred_element_type=F32)


def _dot_nt(a, b):
    return lax.dot_general(a, b, (((1,), (1,)), ((), ())), preferred_element_type=F32)


def _split2(a):
    hi = a.astype(BF16)
    lo = (a - hi.astype(F32)).astype(BF16)
    return hi, lo


def _dot3(a, b):
    ah, al = _split2(a)
    bh, bl = _split2(b)
    return _dot(ah, bh) + (_dot(al, bh) + _dot(ah, bl))


def _dot_exact01(tri, x):
    x1 = x.astype(BF16)
    r1 = x - x1.astype(F32)
    x2 = r1.astype(BF16)
    x3 = (r1 - x2.astype(F32)).astype(BF16)
    return _dot(tri, x1) + (_dot(tri, x2) + _dot(tri, x3))


def _iota2(shape, dim):
    return lax.broadcasted_iota(jnp.int32, shape, dim)


def _eye():
    return _iota2((LANES, LANES), 0) == _iota2((LANES, LANES), 1)


def _cols_to_rows(col):
    return jnp.concatenate(
        [jnp.sum(jnp.where(_eye(), col[r * LANES:(r + 1) * LANES, :], 0.0), axis=0, keepdims=True)
         for r in range(col.shape[0] // LANES)], axis=0)


def _rows_to_cols(rows):
    return jnp.concatenate(
        [jnp.sum(jnp.where(_eye(), rows[r:r + 1, :], 0.0), axis=1, keepdims=True)
         for r in range(rows.shape[0])], axis=0)


def _adaln_kernel(cond_ref, w_ref, b_ref, o_ref):
    o_ref[...] = _dot3(jax.nn.silu(cond_ref[...]), w_ref[...]) + b_ref[...]


def _adaln(cond, w, b):
    n_out = w.shape[1]
    tn = D_MODEL
    assert n_out % tn == 0
    return pl.pallas_call(
        _adaln_kernel,
        grid=(n_out // tn,),
        in_specs=[pl.BlockSpec((MOD_ROWS, D_MODEL), lambda j: (0, 0)),
                  pl.BlockSpec((D_MODEL, tn), lambda j: (0, j)),
                  pl.BlockSpec((1, tn), lambda j: (0, j))],
        out_specs=pl.BlockSpec((MOD_ROWS, tn), lambda j: (0, j)),
        out_shape=jax.ShapeDtypeStruct((MOD_ROWS, n_out), F32),
        compiler_params=_params("arbitrary"),
        name="adaln",
    )(cond, w, b)


def _inproj_kernel(x_ref, xprev_ref, xnext_ref, mod_ref, g_ref, taps_ref, w_ref, *outs, ctx_row, scales,
                   plain_acts):
    conv_refs, (p_ref, gate_ref) = outs[:len(scales)], outs[len(scales):]
    row = pl.program_id(0) if ctx_row is None else ctx_row
    j = pl.program_id(1)
    sh = mod_ref[pl.ds(row, 1), 0:D_MODEL]
    sc = mod_ref[pl.ds(row, 1), D_MODEL:2 * D_MODEL]

    def modulated(x):
        y = x * lax.rsqrt(jnp.mean(x * x, axis=-1, keepdims=True) + EPS) * g_ref[...]
        return y * (1.0 + sc) + sh

    n_groups = x_ref.shape[1] // ROW_GROUP
    before = modulated(xprev_ref[0]) * (j > 0).astype(F32)
    for r in range(n_groups):
        rows = slice(r * ROW_GROUP, (r + 1) * ROW_GROUP)
        h = modulated(x_ref[0, rows, :])
        if r == n_groups - 1:
            after = modulated(xnext_ref[0]) * (j < pl.num_programs(1) - 1).astype(F32)
        else:
            after = modulated(x_ref[0, (r + 1) * ROW_GROUP:(r + 1) * ROW_GROUP + 8, :])
        ext = jnp.concatenate([before, h, after], axis=0).astype(BF16)
        before = h[ROW_GROUP - 8:, :]
        y = _dot(ext, w_ref[...])
        n_ext, n_conv = ext.shape[0], len(scales) * GROUP_W
        y_here = y[8:8 + ROW_GROUP]
        for k, act in enumerate(plain_acts):
            lanes = slice(n_conv + k * GROUP_W, n_conv + (k + 1) * GROUP_W)
            p_ref[0, rows, k * GROUP_W:(k + 1) * GROUP_W] = act(y_here[:, lanes])
        gate_ref[0, rows, :] = y_here[:, n_conv + p_ref.shape[2]:]
        y_conv = y[:, 0:n_conv]
        y_before = pltpu.roll(y_conv, 1, axis=0)[8:8 + ROW_GROUP]
        y_after = pltpu.roll(y_conv, n_ext - 1, axis=0)[8:8 + ROW_GROUP]
        for s, scale in enumerate(scales):
            lanes = slice(s * GROUP_W, (s + 1) * GROUP_W)
            z = (y_before[:, lanes] * taps_ref[s, 0:1, :] + y_here[:, lanes] * taps_ref[s, 1:2, :]
                 + y_after[:, lanes] * taps_ref[s, 2:3, :])
            conv_refs[s][0, rows, :] = (jax.nn.silu(z) * scale).astype(BF16)


def _inproj(x, mod, g, w, taps, scales, plain_acts, tm, ctx_row):
    b, n, _ = x.shape
    wn = len(plain_acts) * GROUP_W
    assert w.shape[1] == len(scales) * GROUP_W + wn + GATE_PAD
    assert tm % ROW_GROUP == 0 and n % tm == 0 and wn % LANES == 0
    rows8 = tm // 8
    last8 = n // 8 - 1
    full = lambda a: pl.BlockSpec(a.shape, lambda i, j: (0,) * a.ndim)
    conv_spec = pl.BlockSpec((1, tm, GROUP_W), lambda i, j: (i, j, 0))
    return pl.pallas_call(
        functools.partial(_inproj_kernel, ctx_row=ctx_row, scales=tuple(scales),
                          plain_acts=tuple(plain_acts)),
        grid=(b, n // tm),
        in_specs=[pl.BlockSpec((1, tm, D_MODEL), lambda i, j: (i, j, 0)),
                  pl.BlockSpec((1, 8, D_MODEL), lambda i, j: (i, jnp.maximum(j * rows8 - 1, 0), 0)),
                  pl.BlockSpec((1, 8, D_MODEL), lambda i, j: (i, jnp.minimum((j + 1) * rows8, last8), 0)),
                  full(mod), full(g), full(taps), full(w)],
        out_specs=[conv_spec] * len(scales)
        + [pl.BlockSpec((1, tm, wn), lambda i, j: (i, j, 0)),
           pl.BlockSpec((1, tm, GATE_PAD), lambda i, j: (i, j, 0))],
        out_shape=[jax.ShapeDtypeStruct((b, n, GROUP_W), BF16)] * len(scales)
        + [jax.ShapeDtypeStruct((b, n, wn), F32), jax.ShapeDtypeStruct((b, n, GATE_PAD), F32)],
        compiler_params=_params("arbitrary", "arbitrary"),
        name="inproj",
    )(x, x, x, mod, g, taps, w)


def _mlstm_chunk(per_dir, bias_ref, c_s, n_s, m_s, h_refs):
    with_h = h_refs[0] is not None
    i0 = _iota2((CHUNK, CHUNK), 0)
    i1 = _iota2((CHUNK, CHUNK), 1)

    units = []
    for smp, d in [(smp, d) for smp in range(c_s.shape[0]) for d in range(2)]:
        q_ref, k_ref, v_ref, g_ref = per_dir[d]
        sees_ts = (i1 <= i0) if d == 0 else (i1 >= i0)
        sees_st = (i0 <= i1) if d == 0 else (i0 >= i1)
        tri = jnp.where(sees_ts, 1.0, 0.0).astype(BF16)
        gates = g_ref[smp] + bias_ref[...]
        bcum = _dot_exact01(tri, jax.nn.log_sigmoid(gates))
        for hd in range(HEADS):
            lanes = slice(hd * HEAD_DIM, (hd + 1) * HEAD_DIM)
            c_prev = c_s[smp, d, hd]
            n_prev = n_s[smp, d, hd]
            k = k_ref[smp, :, lanes]
            st = dict(smp=smp, d=d, hd=hd, lanes=lanes, c_prev=c_prev, n_prev=n_prev, k=k,
                      sees_st=sees_st, gates=gates, bcum=bcum, v=v_ref[smp, :, lanes])
            if with_h:
                lhs = jnp.concatenate([k, c_prev.astype(BF16),
                                       jnp.broadcast_to(n_prev, (BF16_ROWS, HEAD_DIM)).astype(BF16)], axis=0)
                st["prod"] = _dot_nt(lhs, q_ref[smp, :, lanes])
            units.append(st)

    rows = {}
    for st in units:
        smp, d, hd = st["smp"], st["d"], st["hd"]
        if (smp, d) not in rows:
            rows[smp, d] = (st["gates"].T, st["bcum"].T)
        gates_t, bcum_t = rows[smp, d]
        ci = d * HEADS + hd
        cf = 2 * HEADS + d * HEADS + hd
        last = CHUNK - 1 if d == 0 else 0
        li_row = gates_t[ci:ci + 1, :]
        bc_row = bcum_t[cf:cf + 1, :]
        b_last = bc_row[:, last:last + 1]
        m_prev = m_s[smp, d, hd]
        v_t = st["v"].T
        a_row = b_last - bc_row + li_row
        m_new = jnp.maximum(b_last + m_prev, jnp.max(a_row, axis=-1, keepdims=True))
        w_row = jnp.exp(a_row - m_new)
        lhs = jnp.concatenate([v_t * w_row, jnp.broadcast_to(w_row, (BF16_ROWS, CHUNK))], axis=0)
        st["upd"] = _dot(lhs.astype(BF16), st["k"])
        st.update(bc_row=bc_row, m_prev=m_prev, m_new=m_new, v_t=v_t,
                  decay=jnp.exp(b_last + m_prev - m_new),
                  u_col=st["gates"][:, ci:ci + 1] - st["bcum"][:, cf:cf + 1])

    if with_h:
        for st in units:
            prod, bc_row = st["prod"], st["bc_row"]
            g = bc_row + st["m_prev"]
            dmat = jnp.where(st["sees_st"], st["u_col"] + bc_row, -jnp.inf)
            m_t = jnp.maximum(g, jnp.max(dmat, axis=0, keepdims=True))
            inter = jnp.exp(g - m_t)
            s = prod[0:CHUNK] * jnp.exp(dmat - m_t)
            st["pv"] = _dot(st["v_t"].astype(BF16), s.astype(BF16))
            st["num0"] = inter * prod[CHUNK:2 * CHUNK]
            den = inter * prod[2 * CHUNK:2 * CHUNK + 1] + jnp.sum(s, axis=0, keepdims=True)
            st["scale"] = 1.0 / jnp.maximum(jnp.abs(den), jnp.exp(-m_t))

    for st in units:
        smp, d, hd = st["smp"], st["d"], st["hd"]
        if with_h:
            h_refs[d][smp, :, st["lanes"]] = ((st["num0"] + st["pv"]) * st["scale"]).T
        c_s[smp, d, hd] = st["decay"] * st["c_prev"] + st["upd"][0:HEAD_DIM]
        n_s[smp, d, hd] = st["decay"] * st["n_prev"] + st["upd"][HEAD_DIM:HEAD_DIM + 1]
        m_s[smp, d, hd] = st["m_new"]


def _mlstm_kernel(*refs, n_chunks, with_h):
    refs = list(refs)
    take = lambda n: [refs.pop(0) for _ in range(n)]
    per_dir = []
    for _ in range(2):
        q_ref = take(1)[0] if with_h else None
        k_ref, v_ref, g_ref = take(3)
        per_dir.append((q_ref, k_ref, v_ref, g_ref))
    bias_ref = take(1)[0]
    if with_h:
        c0_ref, n0_ref, m0_ref = take(3)
        h_refs = take(2)
        c_out = n_out = m_out = None
    else:
        h_refs = [None, None]
        c_out, n_out, m_out = take(3)
    c_s, n_s, m_s = take(3)

    j = pl.program_id(1)

    @pl.when(j == 0)
    def _():
        if with_h:
            c_s[...] = c0_ref[...]
            n_s[...] = n0_ref[...]
            m_s[...] = m0_ref[...]
        else:
            c_s[...] = jnp.zeros_like(c_s)
            n_s[...] = jnp.zeros_like(n_s)
            m_s[...] = jnp.zeros_like(m_s)

    _mlstm_chunk(per_dir, bias_ref, c_s, n_s, m_s, h_refs)

    if not with_h:
        @pl.when(j == n_chunks - 1)
        def _():
            c_out[...] = c_s[...]
            n_out[...] = n_s[...]
            m_out[...] = m_s[...]


def _mlstm(q, k, p, v_blk, gates, bias, state):
    b, n, _ = k.shape
    n_chunks = n // CHUNK
    with_h = q is not None
    ns = MLSTM_SAMPLES if b % MLSTM_SAMPLES == 0 else 1

    in_specs, args = [], []
    for d in range(2):
        c = (lambda j: j) if d == 0 else (lambda j: n_chunks - 1 - j)
        tok = pl.BlockSpec((ns, CHUNK, GROUP_W), lambda i, j, c=c: (i, c(j), 0))
        if with_h:
            in_specs.append(tok)
            args.append(q)
        in_specs += [tok,
                     pl.BlockSpec((ns, CHUNK, GROUP_W), lambda i, j, c=c: (i, c(j), v_blk)),
                     pl.BlockSpec((ns, CHUNK, GATE_PAD), lambda i, j, c=c: (i, c(j), 0))]
        args += [k, p, gates]
    in_specs.append(pl.BlockSpec((1, GATE_PAD), lambda i, j: (0, 0)))
    args.append(bias)

    c_shape = (2, HEADS, HEAD_DIM, HEAD_DIM)
    v_shape = (2, HEADS, 1, HEAD_DIM)
    c_spec = pl.BlockSpec((ns,) + c_shape, lambda i, j: (i, 0, 0, 0, 0))
    v_spec = pl.BlockSpec((ns,) + v_shape, lambda i, j: (i, 0, 0, 0, 0))
    if with_h:
        in_specs += [c_spec, v_spec, v_spec]
        args += list(state)
        out_specs = [pl.BlockSpec((ns, CHUNK, GROUP_W), lambda i, j: (i, j, 0)),
                     pl.BlockSpec((ns, CHUNK, GROUP_W), lambda i, j: (i, n_chunks - 1 - j, 0))]
        out_shape = [jax.ShapeDtypeStruct((b, n, GROUP_W), F32)] * 2
    else:
        out_specs = [c_spec, v_spec, v_spec]
        out_shape = [jax.ShapeDtypeStruct((b,) + c_shape, F32),
                     jax.ShapeDtypeStruct((b,) + v_shape, F32),
                     jax.ShapeDtypeStruct((b,) + v_shape, F32)]
    return pl.pallas_call(
        functools.partial(_mlstm_kernel, n_chunks=n_chunks, with_h=with_h),
        grid=(b // ns, n_chunks),
        in_specs=in_specs,
        out_specs=out_specs,
        out_shape=out_shape,
        scratch_shapes=[pltpu.VMEM((ns,) + c_shape, F32), pltpu.VMEM((ns,) + v_shape, F32),
                        pltpu.VMEM((ns,) + v_shape, F32)],
        compiler_params=_params("arbitrary", "arbitrary"),
        name="mlstm" if with_h else "mlstm_ctx_state",
    )(*args)


def _layer_norm(x):
    mu = jnp.mean(x, axis=-1, keepdims=True)
    var = jnp.mean(jnp.square(x - mu), axis=-1, keepdims=True)
    return (x - mu) * lax.rsqrt(var + EPS)


def _postmix_kernel(u_ref, vg_ref, o_ref, hf_ref, hb_ref, x_ref, mod_ref, ws_ref, bs_ref, ng_ref,
                    wout_ref, fg_ref, wr_ref, x1_ref, h2_ref, aff_ref, ycat_s):
    tm = x_ref.shape[1]
    b = pl.program_id(0)
    mod = lambda k: mod_ref[pl.ds(b, 1), k * D_MODEL:(k + 1) * D_MODEL]
    wr_hi, wr_lo = _split2(wr_ref[...])
    expert_lane = _iota2((ROW_GROUP, LANES), 1) < N_EXPERTS

    for r in range(tm // ROW_GROUP):
        rows = slice(r * ROW_GROUP, (r + 1) * ROW_GROUP)

        for c in range(r * ROW_GROUP // CHUNK, (r + 1) * ROW_GROUP // CHUNK):
            crows = slice(c * CHUNK, (c + 1) * CHUNK)
            u = u_ref[0, crows, :]
            v = vg_ref[0, crows, :].astype(BF16)
            for hd in range(HEADS):
                lanes = slice(hd * HEAD_DIM, (hd + 1) * HEAD_DIM)
                s = _dot(ws_ref[hd], v[:, lanes]) + bs_ref[hd]
                ycat_s[crows, lanes] = (u[:, lanes] * s).astype(BF16)

        hsum = hf_ref[0, rows, :] + hb_ref[0, rows, :]
        o = o_ref[0, rows, :]
        for hd in range(HEADS):
            lanes = slice(hd * HEAD_DIM, (hd + 1) * HEAD_DIM)
            hn = _layer_norm(hsum[:, lanes]) * ng_ref[:, lanes]
            ycat_s[rows, GROUP_W + hd * HEAD_DIM:GROUP_W + (hd + 1) * HEAD_DIM] = (
                o[:, lanes] * hn).astype(BF16)

        y = _dot(ycat_s[rows, :], wout_ref[...])
        x1 = x_ref[0, rows, :] + mod(2) * y
        x1_ref[0, rows, :] = x1

        n2 = x1 * lax.rsqrt(jnp.mean(x1 * x1, axis=-1, keepdims=True) + EPS) * fg_ref[...]
        h2 = n2 * (1.0 + mod(4)) + mod(3)
        h2_hi, h2_lo = _split2(h2)
        h2_ref[0, rows, :] = h2_hi

        logits = _dot(h2_hi, wr_hi) + (_dot(h2_lo, wr_hi) + _dot(h2_hi, wr_lo))
        logits = jnp.where(expert_lane, logits, -jnp.inf)
        e = jnp.exp(logits - jnp.max(logits, axis=-1, keepdims=True))
        aff = e / jnp.sum(e, axis=-1, keepdims=True)
        aff_ref[0, :, rows] = aff.T[0:N_EXPERTS, :]


def _postmix(p, hf, hb, x, mod, ws, bs, ng, wout, fg, wr, tm):
    b, n, _ = x.shape
    tok = lambda blk: pl.BlockSpec((1, tm, GROUP_W), lambda i, j: (i, j, blk))
    full = lambda a: pl.BlockSpec(a.shape, lambda i, j: (0,) * a.ndim)
    return pl.pallas_call(
        _postmix_kernel,
        grid=(b, n // tm),
        in_specs=[tok(P_U), tok(P_VG), tok(P_O), tok(0), tok(0),
                  pl.BlockSpec((1, tm, D_MODEL), lambda i, j: (i, j, 0)),
                  full(mod), full(ws), full(bs), full(ng), full(wout), full(fg), full(wr)],
        out_specs=[pl.BlockSpec((1, tm, D_MODEL), lambda i, j: (i, j, 0)),
                   pl.BlockSpec((1, tm, D_MODEL), lambda i, j: (i, j, 0)),
                   pl.BlockSpec((1, N_EXPERTS, tm), lambda i, j: (i, 0, j))],
        out_shape=[jax.ShapeDtypeStruct((b, n, D_MODEL), F32),
                   jax.ShapeDtypeStruct((b, n, D_MODEL), BF16),
                   jax.ShapeDtypeStruct((b, N_EXPERTS, n), F32)],
        scratch_shapes=[pltpu.VMEM((tm, D_MODEL), BF16)],
        compiler_params=_params("arbitrary", "arbitrary"),
        name="postmix",
    )(p, p, p, hf, hb, x, mod, ws, bs, ng, wout, fg, wr)


def _cumsum_lanes(x, upper):
    carry = jnp.zeros((x.shape[0], 1), F32)
    outs, before = [], []
    for j in range(x.shape[1] // LANES):
        before.append(carry)
        c = _dot(x[:, j * LANES:(j + 1) * LANES].astype(BF16), upper) + carry
        outs.append(c)
        carry = c[:, LANES - 1:LANES]
    return jnp.concatenate(outs, axis=1), before


def _rows_to_lanes(x, fill):
    pad = jnp.full((LANES - x.shape[0], LANES), fill, F32)
    return jnp.concatenate([x, pad], axis=0).T


def _route_kernel(aff_ref, slot_ref, w_ref, slot_t_ref, first_t_ref, first_ref, *, capacity, n_e):
    aff = aff_ref[...]
    cap = float(capacity)
    thr_bits = jnp.zeros((aff.shape[0], 1), jnp.int32)
    for bit in range(30, -1, -1):
        cand = thr_bits | (1 << bit)
        cnt = jnp.sum(jnp.where(aff >= pltpu.bitcast(cand, F32), 1.0, 0.0), axis=-1, keepdims=True)
        thr_bits = jnp.where(cnt >= cap, cand, thr_bits)
    thr = pltpu.bitcast(thr_bits, F32)
    upper = jnp.where(_iota2((LANES, LANES), 0) <= _iota2((LANES, LANES), 1), 1.0, 0.0).astype(BF16)
    above = jnp.where(aff > thr, 1.0, 0.0)
    tied = jnp.where(aff == thr, 1.0, 0.0)
    need = cap - jnp.sum(above, axis=-1, keepdims=True)
    sel = above + tied * jnp.where(_cumsum_lanes(tied, upper)[0] <= need, 1.0, 0.0)
    count, before = _cumsum_lanes(sel, upper)
    slot = jnp.where(sel > 0.0, count - 1.0, -1.0)
    slot_ref[...] = slot
    w_ref[...] = jnp.where(sel > 0.0, aff, 0.0)

    n_blocks = len(before)
    lane = _iota2((1, LANES), 1)
    first = jnp.zeros((aff.shape[0], LANES), F32)
    for j in range(n_blocks):
        first = jnp.where(lane == j, before[j], first)
    first_ref[...] = first
    for smp in range(aff.shape[0] // n_e):
        rows = slice(smp * n_e, (smp + 1) * n_e)
        for j in range(n_blocks):
            slot_t_ref[smp, j * LANES:(j + 1) * LANES, :] = _rows_to_lanes(
                slot[rows, j * LANES:(j + 1) * LANES], -1.0)
        first_t_ref[smp] = _rows_to_lanes(first[rows, :], 0.0)[0:n_blocks, :]


def _route(aff_t, capacity):
    b, e, n = aff_t.shape
    assert n // LANES <= LANES and e <= LANES and e % 8 == 0
    rows = pl.BlockSpec((b * e, n), lambda i: (0, 0))
    slot, w, slot_t, first_t, first = pl.pallas_call(
        functools.partial(_route_kernel, capacity=capacity, n_e=e),
        grid=(1,),
        in_specs=[rows],
        out_specs=[rows, rows,
                   pl.BlockSpec((b, n, LANES), lambda i: (0, 0, 0)),
                   pl.BlockSpec((b, n // LANES, LANES), lambda i: (0, 0, 0)),
                   pl.BlockSpec((b * e, LANES), lambda i: (0, 0))],
        out_shape=[jax.ShapeDtypeStruct((b * e, n), F32)] * 2
        + [jax.ShapeDtypeStruct((b, n, LANES), F32), jax.ShapeDtypeStruct((b, n // LANES, LANES), F32),
           jax.ShapeDtypeStruct((b * e, LANES), F32)],
        compiler_params=_params("arbitrary"),
        name="route",
    )(aff_t.reshape(b * e, n))
    return slot.reshape(b, e, n), w.reshape(b, e, n), slot_t, first_t, first.reshape(b, e, LANES)


GATHER_TILE = 256
GATHER_ROWS = (64, 128)
GATHER_GROUP = 4


def _gather_kernel(before_ref, fits_ref, h2_ref, slot_ref, w_ref, xe_ref, gate_ref, xe_s, gate_s):
    b = pl.program_id(0)
    g = pl.program_id(1)
    n_g, cap = xe_ref.shape[1], xe_ref.shape[2]
    n = h2_ref.shape[1]
    n_tiles = n // GATHER_TILE
    head = BF16_ROWS

    def windowed(n_rows):
        xe_s[:, 0:head, :] = jnp.zeros((n_g, head, D_MODEL), BF16)
        gate_s[...] = jnp.zeros_like(gate_s)
        row_id = _iota2((n_rows, 1), 0).astype(F32)
        for j in range(n_tiles):
            starts, onehots = [], []
            for e in range(n_g):
                start = pl.multiple_of((before_ref[b, g * n_g + e, j] // head) * head, head)
                hit = (slot_ref[0, e, j:j + 1, :] - start.astype(F32)) == row_id
                onehots.append(jnp.where(hit, 1.0, 0.0).astype(BF16))
                gate_s[e, pl.ds(start, n_rows), :] += jnp.sum(
                    jnp.where(hit, w_ref[0, e, j:j + 1, :], 0.0), axis=1, keepdims=True)
                starts.append(start)
            rows = _dot(jnp.concatenate(onehots, axis=0),
                        h2_ref[0, j * GATHER_TILE:(j + 1) * GATHER_TILE, :]).astype(BF16)
            for e, start in enumerate(starts):
                r0 = e * n_rows
                xe_s[e, pl.ds(start, head), :] += rows[r0:r0 + head]
                xe_s[e, pl.ds(start + head, n_rows - head), :] = rows[r0 + head:r0 + n_rows]
        xe_ref[0] = xe_s[:, 0:cap, :]
        for e in range(n_g):
            gate_ref[0, e] = _cols_to_rows(gate_s[e, 0:cap, :])

    for level, n_rows in enumerate(GATHER_ROWS):
        pl.when(fits_ref[b, g] == level)(functools.partial(windowed, n_rows))

    @pl.when(fits_ref[b, g] == len(GATHER_ROWS))
    def _():
        slot_id = _iota2((cap, 1), 0).astype(F32)
        for e in range(n_g):
            xe = jnp.zeros((cap, D_MODEL), F32)
            gate = jnp.zeros((cap, 1), F32)
            for j in range(n_tiles):
                hit = slot_ref[0, e, j:j + 1, :] == slot_id
                xe = xe + _dot(jnp.where(hit, 1.0, 0.0).astype(BF16),
                               h2_ref[0, j * GATHER_TILE:(j + 1) * GATHER_TILE, :])
                gate = gate + jnp.sum(jnp.where(hit, w_ref[0, e, j:j + 1, :], 0.0), axis=1,
                                      keepdims=True)
            xe_ref[0, e] = xe.astype(BF16)
            gate_ref[0, e] = _cols_to_rows(gate)


def _gather(h2, slot, w, first, capacity):
    b, n, _ = h2.shape
    n_e = slot.shape[1]
    n_tiles = n // GATHER_TILE
    assert n_e % GATHER_GROUP == 0 and GATHER_TILE % LANES == 0 and capacity % LANES == 0
    assert all(r % BF16_ROWS == 0 and r >= 2 * BF16_ROWS for r in GATHER_ROWS)
    assert list(GATHER_ROWS) == sorted(GATHER_ROWS)
    before = first[:, :, 0:n // LANES:GATHER_TILE // LANES].astype(jnp.int32)
    after = jnp.concatenate([before[:, :, 1:], jnp.full((b, n_e, 1), capacity, jnp.int32)], axis=2)
    extent = jnp.max(after - (before // BF16_ROWS) * BF16_ROWS, axis=2)
    extent = jnp.max(extent.reshape(b, n_e // GATHER_GROUP, GATHER_GROUP), axis=2)
    fits = sum((extent >= n_rows).astype(jnp.int32) for n_rows in GATHER_ROWS)
    tiles = lambda a: a.reshape(b, n_e, n_tiles, GATHER_TILE)
    rows = pl.BlockSpec((1, GATHER_GROUP, n_tiles, GATHER_TILE), lambda i, g, *_: (i, g, 0, 0))
    return pl.pallas_call(
        _gather_kernel,
        grid_spec=pltpu.PrefetchScalarGridSpec(
            num_scalar_prefetch=2,
            grid=(b, n_e // GATHER_GROUP),
            in_specs=[pl.BlockSpec((1, n, D_MODEL), lambda i, g, *_: (i, 0, 0)), rows, rows],
            out_specs=[pl.BlockSpec((1, GATHER_GROUP, capacity, D_MODEL), lambda i, g, *_: (i, g, 0, 0)),
                       pl.BlockSpec((1, GATHER_GROUP, capacity // LANES, LANES),
                                    lambda i, g, *_: (i, g, 0, 0))],
            scratch_shapes=[pltpu.VMEM((GATHER_GROUP, capacity + max(GATHER_ROWS), D_MODEL), BF16),
                            pltpu.VMEM((GATHER_GROUP, capacity + max(GATHER_ROWS), 1), F32)]),
        out_shape=[jax.ShapeDtypeStruct((b, n_e, capacity, D_MODEL), BF16),
                   jax.ShapeDtypeStruct((b, n_e, capacity // LANES, LANES), F32)],
        compiler_params=_params("arbitrary", "arbitrary"),
        name="gather",
    )(before, fits, h2, tiles(slot), tiles(w))


def _ffn_kernel(xe_ref, gate_ref, wg_ref, wu_ref, wd_ref, ye_ref, acc_s):
    ft = pl.program_id(1)
    last = pl.num_programs(1) - 1

    def step(first, final):
        wg = wg_ref[0].astype(BF16)
        wu = wu_ref[0].astype(BF16)
        wd = wd_ref[0].astype(BF16)
        for i in range(xe_ref.shape[0]):
            xe = xe_ref[i, 0]
            act = jax.nn.silu(_dot(xe, wg)) * _dot(xe, wu)
            part = _dot(act.astype(BF16), wd)
            total = part if first else acc_s[i] + part
            if final:
                ye_ref[i] = (total * _rows_to_cols(gate_ref[i, 0])).astype(BF16)
            else:
                acc_s[i] = total

    pl.when(ft == 0)(lambda: step(True, False))
    pl.when(jnp.logical_and(ft > 0, ft < last))(lambda: step(False, False))
    pl.when(ft == last)(lambda: step(False, True))


def _ffn(xe, gate, wg, wu, wd, f_tile):
    b, n_e, cap, _ = xe.shape
    assert D_EXPERT // f_tile >= 2
    return pl.pallas_call(
        _ffn_kernel,
        grid=(n_e, D_EXPERT // f_tile),
        in_specs=[pl.BlockSpec((b, 1, cap, D_MODEL), lambda e, f: (0, e, 0, 0)),
                  pl.BlockSpec((b, 1, cap // LANES, LANES), lambda e, f: (0, e, 0, 0)),
                  pl.BlockSpec((1, D_MODEL, f_tile), lambda e, f: (e, 0, f)),
                  pl.BlockSpec((1, D_MODEL, f_tile), lambda e, f: (e, 0, f)),
                  pl.BlockSpec((1, f_tile, D_MODEL), lambda e, f: (e, f, 0))],
        out_specs=pl.BlockSpec((b, cap, D_MODEL), lambda e, f: (0, e, 0)),
        out_shape=jax.ShapeDtypeStruct((b, n_e * cap, D_MODEL), BF16),
        scratch_shapes=[pltpu.VMEM((b, cap, D_MODEL), F32)],
        compiler_params=_params("arbitrary", "arbitrary"),
        name="ffn",
    )(xe, gate, wg, wu, wd)


COMBINE_WIDE = CHUNK + BF16_ROWS
COMBINE_NARROW = 48


def _combine_kernel(first_ref, narrow_ref, slot_t_ref, first_t_ref, ye_ref, x1_ref, mod_ref, fg_ref,
                    o_ref, acc_s, *, capacity):
    b = pl.program_id(0)
    j = pl.program_id(1)
    n_sub = o_ref.shape[1] // CHUNK
    lane = _iota2((1, LANES), 1).astype(F32)

    def scatter(window):
        k_total = N_EXPERTS * window
        for sb in range(n_sub):
            blk = j * n_sub + sb
            slot_t = slot_t_ref[0, sb * CHUNK:(sb + 1) * CHUNK, :]
            start_row = jnp.minimum(
                jnp.floor(first_t_ref[0, pl.ds(blk, 1), :] * (1.0 / BF16_ROWS)) * float(BF16_ROWS),
                float(capacity - window))
            k_pos = jnp.where(slot_t >= 0.0, slot_t - start_row + lane * float(window), -1.0)
            cols = []
            for c in range(k_total // LANES):
                k_lane = lane + float(LANES * c)
                hit = jnp.zeros((CHUNK, LANES), F32)
                for e in range((LANES * c) // window, (LANES * c + LANES - 1) // window + 1):
                    hit = jnp.where(k_pos[:, e:e + 1] == k_lane, 1.0, hit)
                cols.append(hit.astype(BF16))
            onehot = jnp.concatenate(cols, axis=1)
            rows = []
            for e in range(N_EXPERTS):
                start = jnp.minimum((first_ref[b, blk, e] // BF16_ROWS) * BF16_ROWS, capacity - window)
                rows.append(ye_ref[0, pl.ds(pl.multiple_of(e * capacity + start, BF16_ROWS), window), :])
            acc_s[sb * CHUNK:(sb + 1) * CHUNK, :] = _dot(onehot, jnp.concatenate(rows, axis=0))

    pl.when(narrow_ref[b, j] != 0)(lambda: scatter(COMBINE_NARROW))
    pl.when(narrow_ref[b, j] == 0)(lambda: scatter(COMBINE_WIDE))

    g2 = mod_ref[pl.ds(b, 1), 5 * D_MODEL:6 * D_MODEL]
    x2 = x1_ref[0] + g2 * acc_s[...]
    o_ref[0] = x2 * lax.rsqrt(jnp.mean(x2 * x2, axis=-1, keepdims=True) + EPS) * fg_ref[...]


def _combine(slot_t, first_t, ye, x1, mod, fg, capacity, tm):
    b, n, _ = x1.shape
    n_blocks = n // CHUNK
    assert CHUNK == LANES and N_EXPERTS <= LANES and tm % CHUNK == 0
    for window in (COMBINE_WIDE, COMBINE_NARROW):
        assert (N_EXPERTS * window) % LANES == 0 and window % BF16_ROWS == 0 and window <= capacity
    assert COMBINE_WIDE >= CHUNK + BF16_ROWS - 1 and capacity % BF16_ROWS == 0
    first = first_t[:, :, :N_EXPERTS].astype(jnp.int32)
    after = jnp.concatenate([first[:, 1:], jnp.full((b, 1, N_EXPERTS), capacity, jnp.int32)], axis=1)
    narrow = jnp.all(after - (first // BF16_ROWS) * BF16_ROWS <= COMBINE_NARROW, axis=2)
    narrow = jnp.all(narrow.reshape(b, n // tm, tm // CHUNK), axis=2).astype(jnp.int32)
    tok = pl.BlockSpec((1, tm, D_MODEL), lambda i, j, *_: (i, j, 0))
    return pl.pallas_call(
        functools.partial(_combine_kernel, capacity=capacity),
        grid_spec=pltpu.PrefetchScalarGridSpec(
            num_scalar_prefetch=2,
            grid=(b, n // tm),
            in_specs=[pl.BlockSpec((1, tm, LANES), lambda i, j, *_: (i, j, 0)),
                      pl.BlockSpec((1, n_blocks, LANES), lambda i, j, *_: (i, 0, 0)),
                      pl.BlockSpec((1, N_EXPERTS * capacity, D_MODEL), lambda i, j, *_: (i, 0, 0)),
                      tok,
                      pl.BlockSpec(mod.shape, lambda i, j, *_: (0, 0)),
                      pl.BlockSpec((1, D_MODEL), lambda i, j, *_: (0, 0))],
            out_specs=tok,
            scratch_shapes=[pltpu.VMEM((tm, D_MODEL), F32)]),
        out_shape=jax.ShapeDtypeStruct((b, n, D_MODEL), F32),
        compiler_params=_params("arbitrary", "arbitrary"),
        name="combine",
    )(first, narrow, slot_t, first_t, ye, x1, mod, fg)


def kernel(x, c, ctx, c_ctx, w_mod, b_mod, norm_mix_g, w_in, conv_q, conv_k, b_igate, b_fgate,
           gmlp_ws, gmlp_bs, mlstm_norm_g, w_out, norm_ffn_g, w_router, w_gate_e, w_up_e,
           w_down_e, final_g):
    depth = w_mod.shape[0]
    assert depth == 1, "the context stream is only carried as mLSTM states (single layer)"
    batch, seq, _ = x.shape
    assert seq % GRID_W == 0 and seq % CHUNK == 0 and batch + 1 <= MOD_ROWS
    capacity = EC_FACTOR * seq // N_EXPERTS
    ctx_row = batch
    l = 0

    cond = jnp.concatenate([c, c_ctx[None], jnp.zeros((MOD_ROWS - batch - 1, D_MODEL), F32)], axis=0)
    mod = _adaln(cond, w_mod[l], b_mod[l][None])

    row = lambda a: a[None]
    w_gates = jnp.pad(w_in[l][:, MAIN_W:], ((0, 0), (0, GATE_PAD - N_GATES)))
    cols = lambda *blks: jnp.concatenate(
        [w_in[l][:, k * GROUP_W:(k + 1) * GROUP_W] for k in blks] + [w_gates], axis=1).astype(BF16)
    gate_bias = jnp.pad(jnp.concatenate([b_igate[l].reshape(-1), b_fgate[l].reshape(-1)]),
                        (0, GATE_PAD - N_GATES))[None]
    k_scale = HEAD_DIM ** -0.5

    identity = lambda a: a
    k_c, v_c, gates_c = _inproj(ctx, mod, row(norm_mix_g[l]), cols(K_BLK, V_BLK), conv_k[l][None],
                                (k_scale,), (identity,), tm=ctx.shape[1], ctx_row=ctx_row)
    state = _mlstm(None, k_c, v_c, 0, gates_c, gate_bias, None)

    q_l, k_l, p, gates = _inproj(x, mod, row(norm_mix_g[l]),
                                 cols(Q_BLK, K_BLK, U_BLK, VG_BLK, O_BLK, V_BLK),
                                 jnp.stack([conv_q[l], conv_k[l]]), (1.0, k_scale),
                                 (jax.nn.gelu, lambda a: _layer_norm(jax.nn.gelu(a)), jax.nn.sigmoid, identity),
                                 tm=1024, ctx_row=None)
    h_f, h_b = _mlstm(q_l, k_l, p, P_V, gates, gate_bias, state)

    x1, h2, aff_t = _postmix(p, h_f, h_b, x, mod, gmlp_ws[l].astype(BF16), gmlp_bs[l][:, :, None],
                             row(mlstm_norm_g[l]), w_out[l].astype(BF16), row(norm_ffn_g[l]),
                             jnp.pad(w_router[l], ((0, 0), (0, LANES - N_EXPERTS))), tm=1024)

    slot, gate_w, slot_t, first_t, first = _route(aff_t, capacity)

    xe, gate = _gather(h2, slot, gate_w, first, capacity)
    ye = _ffn(xe, gate, w_gate_e[l], w_up_e[l], w_down_e[l], f_tile=512)
    return _combine(slot_t, first_t, ye, x1, mod, row(final_g), capacity, tm=512)
```

```python
import functools

import jax
import jax.numpy as jnp
from jax import lax
from jax.experimental import pallas as pl
from jax.experimental.pallas import tpu as pltpu

F32 = jnp.float32
BF16 = jnp.bfloat16

LANES = 128
BF16_ROWS = 16
VMEM_LIMIT = 56 * 1024 * 1024

D_MODEL = 1024
GRID_W = 64
CHUNK = 128
HEADS = 4
GROUP_W = D_MODEL // 2
HEAD_DIM = GROUP_W // HEADS
N_EXPERTS = 16
EC_FACTOR = 2
D_EXPERT = 2 * D_MODEL
EPS = 1e-6
N_GATES = 4 * HEADS
GATE_PAD = LANES
MOD_ROWS = 8

U_BLK, VG_BLK, Q_BLK, O_BLK, K_BLK, V_BLK = 0, 1, 2, 3, 4, 5
MAIN_W = 6 * GROUP_W
P_U, P_VG, P_O, P_V = 0, 1, 2, 3

ROW_GROUP = 2 * CHUNK
MLSTM_SAMPLES = 4


def _params(*sem):
    return pltpu.CompilerParams(dimension_semantics=sem, vmem_limit_bytes=VMEM_LIMIT)


def _dot(a, b):
    return jnp.dot(a, b, preferred_element_type=F32)


def _dot_nt(a, b):
    return lax.dot_general(a, b, (((1,), (1,)), ((), ())), preferred_element_type=F32)


def _split2(a):
    hi = a.astype(BF16)
    lo = (a - hi.astype(F32)).astype(BF16)
    return hi, lo


def _dot3(a, b):
    ah, al = _split2(a)
    bh, bl = _split2(b)
    return _dot(ah, bh) + (_dot(al, bh) + _dot(ah, bl))


def _dot_exact01(tri, x):
    x1 = x.astype(BF16)
    r1 = x - x1.astype(F32)
    x2 = r1.astype(BF16)
    x3 = (r1 - x2.astype(F32)).astype(BF16)
    return _dot(tri, x1) + (_dot(tri, x2) + _dot(tri, x3))


def _iota2(shape, dim):
    return lax.broadcasted_iota(jnp.int32, shape, dim)


def _eye():
    return _iota2((LANES, LANES), 0) == _iota2((LANES, LANES), 1)


def _cols_to_rows(col):
    return jnp.concatenate(
        [jnp.sum(jnp.where(_eye(), col[r * LANES:(r + 1) * LANES, :], 0.0), axis=0, keepdims=True)
         for r in range(col.shape[0] // LANES)], axis=0)


def _rows_to_cols(rows):
    return jnp.concatenate(
        [jnp.sum(jnp.where(_eye(), rows[r:r + 1, :], 0.0), axis=1, keepdims=True)
         for r in range(rows.shape[0])], axis=0)


def _adaln_kernel(cond_ref, w_ref, b_ref, o_ref):
    o_ref[...] = _dot3(jax.nn.silu(cond_ref[...]), w_ref[...]) + b_ref[...]


def _adaln(cond, w, b):
    n_out = w.shape[1]
    tn = D_MODEL
    assert n_out % tn == 0
    return pl.pallas_call(
        _adaln_kernel,
        grid=(n_out // tn,),
        in_specs=[pl.BlockSpec((MOD_ROWS, D_MODEL), lambda j: (0, 0)),
                  pl.BlockSpec((D_MODEL, tn), lambda j: (0, j)),
                  pl.BlockSpec((1, tn), lambda j: (0, j))],
        out_specs=pl.BlockSpec((MOD_ROWS, tn), lambda j: (0, j)),
        out_shape=jax.ShapeDtypeStruct((MOD_ROWS, n_out), F32),
        compiler_params=_params("arbitrary"),
        name="adaln",
    )(cond, w, b)


def _inproj_kernel(x_ref, xprev_ref, xnext_ref, mod_ref, g_ref, taps_ref, w_ref, *outs, ctx_row, scales,
                   plain_acts):
    conv_refs, (p_ref, gate_ref) = outs[:len(scales)], outs[len(scales):]
    row = pl.program_id(0) if ctx_row is None else ctx_row
    j = pl.program_id(1)
    sh = mod_ref[pl.ds(row, 1), 0:D_MODEL]
    sc = mod_ref[pl.ds(row, 1), D_MODEL:2 * D_MODEL]

    def modulated(x):
        y = x * lax.rsqrt(jnp.mean(x * x, axis=-1, keepdims=True) + EPS) * g_ref[...]
        return y * (1.0 + sc) + sh

    n_groups = x_ref.shape[1] // ROW_GROUP
    before = modulated(xprev_ref[0]) * (j > 0).astype(F32)
    for r in range(n_groups):
        rows = slice(r * ROW_GROUP, (r + 1) * ROW_GROUP)
        h = modulated(x_ref[0, rows, :])
        if r == n_groups - 1:
            after = modulated(xnext_ref[0]) * (j < pl.num_programs(1) - 1).astype(F32)
        else:
            after = modulated(x_ref[0, (r + 1) * ROW_GROUP:(r + 1) * ROW_GROUP + 8, :])
        ext = jnp.concatenate([before, h, after], axis=0).astype(BF16)
        before = h[ROW_GROUP - 8:, :]
        y = _dot(ext, w_ref[...])
        n_ext, n_conv = ext.shape[0], len(scales) * GROUP_W
        y_here = y[8:8 + ROW_GROUP]
        for k, act in enumerate(plain_acts):
            lanes = slice(n_conv + k * GROUP_W, n_conv + (k + 1) * GROUP_W)
            p_ref[0, rows, k * GROUP_W:(k + 1) * GROUP_W] = act(y_here[:, lanes])
        gate_ref[0, rows, :] = y_here[:, n_conv + p_ref.shape[2]:]
        y_conv = y[:, 0:n_conv]
        y_before = pltpu.roll(y_conv, 1, axis=0)[8:8 + ROW_GROUP]
        y_after = pltpu.roll(y_conv, n_ext - 1, axis=0)[8:8 + ROW_GROUP]
        for s, scale in enumerate(scales):
            lanes = slice(s * GROUP_W, (s + 1) * GROUP_W)
            z = (y_before[:, lanes] * taps_ref[s, 0:1, :] + y_here[:, lanes] * taps_ref[s, 1:2, :]
                 + y_after[:, lanes] * taps_ref[s, 2:3, :])
            conv_refs[s][0, rows, :] = (jax.nn.silu(z) * scale).astype(BF16)


def _inproj(x, mod, g, w, taps, scales, plain_acts, tm, ctx_row):
    b, n, _ = x.shape
    wn = len(plain_acts) * GROUP_W
    assert w.shape[1] == len(scales) * GROUP_W + wn + GATE_PAD
    assert tm % ROW_GROUP == 0 and n % tm == 0 and wn % LANES == 0
    rows8 = tm // 8
    last8 = n // 8 - 1
    full = lambda a: pl.BlockSpec(a.shape, lambda i, j: (0,) * a.ndim)
    conv_spec = pl.BlockSpec((1, tm, GROUP_W), lambda i, j: (i, j, 0))
    return pl.pallas_call(
        functools.partial(_inproj_kernel, ctx_row=ctx_row, scales=tuple(scales),
                          plain_acts=tuple(plain_acts)),
        grid=(b, n // tm),
        in_specs=[pl.BlockSpec((1, tm, D_MODEL), lambda i, j: (i, j, 0)),
                  pl.BlockSpec((1, 8, D_MODEL), lambda i, j: (i, jnp.maximum(j * rows8 - 1, 0), 0)),
                  pl.BlockSpec((1, 8, D_MODEL), lambda i, j: (i, jnp.minimum((j + 1) * rows8, last8), 0)),
                  full(mod), full(g), full(taps), full(w)],
        out_specs=[conv_spec] * len(scales)
        + [pl.BlockSpec((1, tm, wn), lambda i, j: (i, j, 0)),
           pl.BlockSpec((1, tm, GATE_PAD), lambda i, j: (i, j, 0))],
        out_shape=[jax.ShapeDtypeStruct((b, n, GROUP_W), BF16)] * len(scales)
        + [jax.ShapeDtypeStruct((b, n, wn), F32), jax.ShapeDtypeStruct((b, n, GATE_PAD), F32)],
        compiler_params=_params("arbitrary", "arbitrary"),
        name="inproj",
    )(x, x, x, mod, g, taps, w)


def _mlstm_chunk(per_dir, bias_ref, c_s, n_s, m_s, h_refs):
    with_h = h_refs[0] is not None
    i0 = _iota2((CHUNK, CHUNK), 0)
    i1 = _iota2((CHUNK, CHUNK), 1)

    units = []
    for smp, d in [(smp, d) for smp in range(c_s.shape[0]) for d in range(2)]:
        q_ref, k_ref, v_ref, g_ref = per_dir[d]
        sees_ts = (i1 <= i0) if d == 0 else (i1 >= i0)
        sees_st = (i0 <= i1) if d == 0 else (i0 >= i1)
        tri = jnp.where(sees_ts, 1.0, 0.0).astype(BF16)
        gates = g_ref[smp] + bias_ref[...]
        bcum = _dot_exact01(tri, jax.nn.log_sigmoid(gates))
        for hd in range(HEADS):
            lanes = slice(hd * HEAD_DIM, (hd + 1) * HEAD_DIM)
            c_prev = c_s[smp, d, hd]
            n_prev = n_s[smp, d, hd]
            k = k_ref[smp, :, lanes]
            st = dict(smp=smp, d=d, hd=hd, lanes=lanes, c_prev=c_prev, n_prev=n_prev, k=k,
                      sees_st=sees_st, gates=gates, bcum=bcum, v=v_ref[smp, :, lanes])
            if with_h:
                lhs = jnp.concatenate([k, c_prev.astype(BF16),
                                       jnp.broadcast_to(n_prev, (BF16_ROWS, HEAD_DIM)).astype(BF16)], axis=0)
                st["prod"] = _dot_nt(lhs, q_ref[smp, :, lanes])
            units.append(st)

    rows = {}
    for st in units:
        smp, d, hd = st["smp"], st["d"], st["hd"]
        if (smp, d) not in rows:
            rows[smp, d] = (st["gates"].T, st["bcum"].T)
        gates_t, bcum_t = rows[smp, d]
        ci = d * HEADS + hd
        cf = 2 * HEADS + d * HEADS + hd
        last = CHUNK - 1 if d == 0 else 0
        li_row = gates_t[ci:ci + 1, :]
        bc_row = bcum_t[cf:cf + 1, :]
        b_last = bc_row[:, last:last + 1]
        m_prev = m_s[smp, d, hd]
        v_t = st["v"].T
        a_row = b_last - bc_row + li_row
        m_new = jnp.maximum(b_last + m_prev, jnp.max(a_row, axis=-1, keepdims=True))
        w_row = jnp.exp(a_row - m_new)
        lhs = jnp.concatenate([v_t * w_row, jnp.broadcast_to(w_row, (BF16_ROWS, CHUNK))], axis=0)
        st["upd"] = _dot(lhs.astype(BF16), st["k"])
        st.update(bc_row=bc_row, m_prev=m_prev, m_new=m_new, v_t=v_t,
                  decay=jnp.exp(b_last + m_prev - m_new),
                  u_col=st["gates"][:, ci:ci + 1] - st["bcum"][:, cf:cf + 1])

    if with_h:
        for st in units:
            prod, bc_row = st["prod"], st["bc_row"]
            g = bc_row + st["m_prev"]
            dmat = jnp.where(st["sees_st"], st["u_col"] + bc_row, -jnp.inf)
            m_t = jnp.maximum(g, jnp.max(dmat, axis=0, keepdims=True))
            inter = jnp.exp(g - m_t)
            s = prod[0:CHUNK] * jnp.exp(dmat - m_t)
            st["pv"] = _dot(st["v_t"].astype(BF16), s.astype(BF16))
            st["num0"] = inter * prod[CHUNK:2 * CHUNK]
            den = inter * prod[2 * CHUNK:2 * CHUNK + 1] + jnp.sum(s, axis=0, keepdims=True)
            st["scale"] = 1.0 / jnp.maximum(jnp.abs(den), jnp.exp(-m_t))

    for st in units:
        smp, d, hd = st["smp"], st["d"], st["hd"]
        if with_h:
            h_refs[d][smp, :, st["lanes"]] = ((st["num0"] + st["pv"]) * st["scale"]).T
        c_s[smp, d, hd] = st["decay"] * st["c_prev"] + st["upd"][0:HEAD_DIM]
        n_s[smp, d, hd] = st["decay"] * st["n_prev"] + st["upd"][HEAD_DIM:HEAD_DIM + 1]
        m_s[smp, d, hd] = st["m_new"]


def _mlstm_kernel(*refs, n_chunks, with_h):
    refs = list(refs)
    take = lambda n: [refs.pop(0) for _ in range(n)]
    per_dir = []
    for _ in range(2):
        q_ref = take(1)[0] if with_h else None
        k_ref, v_ref, g_ref = take(3)
        per_dir.append((q_ref, k_ref, v_ref, g_ref))
    bias_ref = take(1)[0]
    if with_h:
        c0_ref, n0_ref, m0_ref = take(3)
        h_refs = take(2)
        c_out = n_out = m_out = None
    else:
        h_refs = [None, None]
        c_out, n_out, m_out = take(3)
    c_s, n_s, m_s = take(3)

    j = pl.program_id(1)

    @pl.when(j == 0)
    def _():
        if with_h:
            c_s[...] = c0_ref[...]
            n_s[...] = n0_ref[...]
            m_s[...] = m0_ref[...]
        else:
            c_s[...] = jnp.zeros_like(c_s)
            n_s[...] = jnp.zeros_like(n_s)
            m_s[...] = jnp.zeros_like(m_s)

    _mlstm_chunk(per_dir, bias_ref, c_s, n_s, m_s, h_refs)

    if not with_h:
        @pl.when(j == n_chunks - 1)
        def _():
            c_out[...] = c_s[...]
            n_out[...] = n_s[...]
            m_out[...] = m_s[...]


def _mlstm(q, k, p, v_blk, gates, bias, state):
    b, n, _ = k.shape
    n_chunks = n // CHUNK
    with_h = q is not None
    ns = MLSTM_SAMPLES if b % MLSTM_SAMPLES == 0 else 1

    in_specs, args = [], []
    for d in range(2):
        c = (lambda j: j) if d == 0 else (lambda j: n_chunks - 1 - j)
        tok = pl.BlockSpec((ns, CHUNK, GROUP_W), lambda i, j, c=c: (i, c(j), 0))
        if with_h:
            in_specs.append(tok)
            args.append(q)
        in_specs += [tok,
                     pl.BlockSpec((ns, CHUNK, GROUP_W), lambda i, j, c=c: (i, c(j), v_blk)),
                     pl.BlockSpec((ns, CHUNK, GATE_PAD), lambda i, j, c=c: (i, c(j), 0))]
        args += [k, p, gates]
    in_specs.append(pl.BlockSpec((1, GATE_PAD), lambda i, j: (0, 0)))
    args.append(bias)

    c_shape = (2, HEADS, HEAD_DIM, HEAD_DIM)
    v_shape = (2, HEADS, 1, HEAD_DIM)
    c_spec = pl.BlockSpec((ns,) + c_shape, lambda i, j: (i, 0, 0, 0, 0))
    v_spec = pl.BlockSpec((ns,) + v_shape, lambda i, j: (i, 0, 0, 0, 0))
    if with_h:
        in_specs += [c_spec, v_spec, v_spec]
        args += list(state)
        out_specs = [pl.BlockSpec((ns, CHUNK, GROUP_W), lambda i, j: (i, j, 0)),
                     pl.BlockSpec((ns, CHUNK, GROUP_W), lambda i, j: (i, n_chunks - 1 - j, 0))]
        out_shape = [jax.ShapeDtypeStruct((b, n, GROUP_W), F32)] * 2
    else:
        out_specs = [c_spec, v_spec, v_spec]
        out_shape = [jax.ShapeDtypeStruct((b,) + c_shape, F32),
                     jax.ShapeDtypeStruct((b,) + v_shape, F32),
                     jax.ShapeDtypeStruct((b,) + v_shape, F32)]
    return pl.pallas_call(
        functools.partial(_mlstm_kernel, n_chunks=n_chunks, with_h=with_h),
        grid=(b // ns, n_chunks),
        in_specs=in_specs,
        out_specs=out_specs,
        out_shape=out_shape,
        scratch_shapes=[pltpu.VMEM((ns,) + c_shape, F32), pltpu.VMEM((ns,) + v_shape, F32),
                        pltpu.VMEM((ns,) + v_shape, F32)],
        compiler_params=_params("arbitrary", "arbitrary"),
        name="mlstm" if with_h else "mlstm_ctx_state",
    )(*args)


def _layer_norm(x):
    mu = jnp.mean(x, axis=-1, keepdims=True)
    var = jnp.mean(jnp.square(x - mu), axis=-1, keepdims=True)
    return (x - mu) * lax.rsqrt(var + EPS)


def _postmix_kernel(u_ref, vg_ref, o_ref, hf_ref, hb_ref, x_ref, mod_ref, ws_ref, bs_ref, ng_ref,
                    wout_ref, fg_ref, wr_ref, x1_ref, h2_ref, aff_ref, ycat_s):
    tm = x_ref.shape[1]
    b = pl.program_id(0)
    mod = lambda k: mod_ref[pl.ds(b, 1), k * D_MODEL:(k + 1) * D_MODEL]
    wr_hi, wr_lo = _split2(wr_ref[...])
    expert_lane = _iota2((ROW_GROUP, LANES), 1) < N_EXPERTS

    for r in range(tm // ROW_GROUP):
        rows = slice(r * ROW_GROUP, (r + 1) * ROW_GROUP)

        for c in range(r * ROW_GROUP // CHUNK, (r + 1) * ROW_GROUP // CHUNK):
            crows = slice(c * CHUNK, (c + 1) * CHUNK)
            u = u_ref[0, crows, :]
            v = vg_ref[0, crows, :].astype(BF16)
            for hd in range(HEADS):
                lanes = slice(hd * HEAD_DIM, (hd + 1) * HEAD_DIM)
                s = _dot(ws_ref[hd], v[:, lanes]) + bs_ref[hd]
                ycat_s[crows, lanes] = (u[:, lanes] * s).astype(BF16)

        hsum = hf_ref[0, rows, :] + hb_ref[0, rows, :]
        o = o_ref[0, rows, :]
        for hd in range(HEADS):
            lanes = slice(hd * HEAD_DIM, (hd + 1) * HEAD_DIM)
            hn = _layer_norm(hsum[:, lanes]) * ng_ref[:, lanes]
            ycat_s[rows, GROUP_W + hd * HEAD_DIM:GROUP_W + (hd + 1) * HEAD_DIM] = (
                o[:, lanes] * hn).astype(BF16)

        y = _dot(ycat_s[rows, :], wout_ref[...])
        x1 = x_ref[0, rows, :] + mod(2) * y
        x1_ref[0, rows, :] = x1

        n2 = x1 * lax.rsqrt(jnp.mean(x1 * x1, axis=-1, keepdims=True) + EPS) * fg_ref[...]
        h2 = n2 * (1.0 + mod(4)) + mod(3)
        h2_hi, h2_lo = _split2(h2)
        h2_ref[0, rows, :] = h2_hi

        logits = _dot(h2_hi, wr_hi) + (_dot(h2_lo, wr_hi) + _dot(h2_hi, wr_lo))
        logits = jnp.where(expert_lane, logits, -jnp.inf)
        e = jnp.exp(logits - jnp.max(logits, axis=-1, keepdims=True))
        aff = e / jnp.sum(e, axis=-1, keepdims=True)
        aff_ref[0, :, rows] = aff.T[0:N_EXPERTS, :]


def _postmix(p, hf, hb, x, mod, ws, bs, ng, wout, fg, wr, tm):
    b, n, _ = x.shape
    tok = lambda blk: pl.BlockSpec((1, tm, GROUP_W), lambda i, j: (i, j, blk))
    full = lambda a: pl.BlockSpec(a.shape, lambda i, j: (0,) * a.ndim)
    return pl.pallas_call(
        _postmix_kernel,
        grid=(b, n // tm),
        in_specs=[tok(P_U), tok(P_VG), tok(P_O), tok(0), tok(0),
                  pl.BlockSpec((1, tm, D_MODEL), lambda i, j: (i, j, 0)),
                  full(mod), full(ws), full(bs), full(ng), full(wout), full(fg), full(wr)],
        out_specs=[pl.BlockSpec((1, tm, D_MODEL), lambda i, j: (i, j, 0)),
                   pl.BlockSpec((1, tm, D_MODEL), lambda i, j: (i, j, 0)),
                   pl.BlockSpec((1, N_EXPERTS, tm), lambda i, j: (i, 0, j))],
        out_shape=[jax.ShapeDtypeStruct((b, n, D_MODEL), F32),
                   jax.ShapeDtypeStruct((b, n, D_MODEL), BF16),
                   jax.ShapeDtypeStruct((b, N_EXPERTS, n), F32)],
        scratch_shapes=[pltpu.VMEM((tm, D_MODEL), BF16)],
        compiler_params=_params("arbitrary", "arbitrary"),
        name="postmix",
    )(p, p, p, hf, hb, x, mod, ws, bs, ng, wout, fg, wr)


def _cumsum_lanes(x, upper):
    carry = jnp.zeros((x.shape[0], 1), F32)
    outs, before = [], []
    for j in range(x.shape[1] // LANES):
        before.append(carry)
        c = _dot(x[:, j * LANES:(j + 1) * LANES].astype(BF16), upper) + carry
        outs.append(c)
        carry = c[:, LANES - 1:LANES]
    return jnp.concatenate(outs, axis=1), before


def _rows_to_lanes(x, fill):
    pad = jnp.full((LANES - x.shape[0], LANES), fill, F32)
    return jnp.concatenate([x, pad], axis=0).T


def _route_kernel(aff_ref, slot_ref, w_ref, slot_t_ref, first_t_ref, first_ref, *, capacity, n_e):
    aff = aff_ref[...]
    cap = float(capacity)
    thr_bits = jnp.zeros((aff.shape[0], 1), jnp.int32)
    for bit in range(30, -1, -1):
        cand = thr_bits | (1 << bit)
        cnt = jnp.sum(jnp.where(aff >= pltpu.bitcast(cand, F32), 1.0, 0.0), axis=-1, keepdims=True)
        thr_bits = jnp.where(cnt >= cap, cand, thr_bits)
    thr = pltpu.bitcast(thr_bits, F32)
    upper = jnp.where(_iota2((LANES, LANES), 0) <= _iota2((LANES, LANES), 1), 1.0, 0.0).astype(BF16)
    above = jnp.where(aff > thr, 1.0, 0.0)
    tied = jnp.where(aff == thr, 1.0, 0.0)
    need = cap - jnp.sum(above, axis=-1, keepdims=True)
    sel = above + tied * jnp.where(_cumsum_lanes(tied, upper)[0] <= need, 1.0, 0.0)
    count, before = _cumsum_lanes(sel, upper)
    slot = jnp.where(sel > 0.0, count - 1.0, -1.0)
    slot_ref[...] = slot
    w_ref[...] = jnp.where(sel > 0.0, aff, 0.0)

    n_blocks = len(before)
    lane = _iota2((1, LANES), 1)
    first = jnp.zeros((aff.shape[0], LANES), F32)
    for j in range(n_blocks):
        first = jnp.where(lane == j, before[j], first)
    first_ref[...] = first
    for smp in range(aff.shape[0] // n_e):
        rows = slice(smp * n_e, (smp + 1) * n_e)
        for j in range(n_blocks):
            slot_t_ref[smp, j * LANES:(j + 1) * LANES, :] = _rows_to_lanes(
                slot[rows, j * LANES:(j + 1) * LANES], -1.0)
        first_t_ref[smp] = _rows_to_lanes(first[rows, :], 0.0)[0:n_blocks, :]


def _route(aff_t, capacity):
    b, e, n = aff_t.shape
    assert n // LANES <= LANES and e <= LANES and e % 8 == 0
    rows = pl.BlockSpec((b * e, n), lambda i: (0, 0))
    slot, w, slot_t, first_t, first = pl.pallas_call(
        functools.partial(_route_kernel, capacity=capacity, n_e=e),
        grid=(1,),
        in_specs=[rows],
        out_specs=[rows, rows,
                   pl.BlockSpec((b, n, LANES), lambda i: (0, 0, 0)),
                   pl.BlockSpec((b, n // LANES, LANES), lambda i: (0, 0, 0)),
                   pl.BlockSpec((b * e, LANES), lambda i: (0, 0))],
        out_shape=[jax.ShapeDtypeStruct((b * e, n), F32)] * 2
        + [jax.ShapeDtypeStruct((b, n, LANES), F32), jax.ShapeDtypeStruct((b, n // LANES, LANES), F32),
           jax.ShapeDtypeStruct((b * e, LANES), F32)],
        compiler_params=_params("arbitrary"),
        name="route",
    )(aff_t.reshape(b * e, n))
    return slot.reshape(b, e, n), w.reshape(b, e, n), slot_t, first_t, first.reshape(b, e, LANES)


GATHER_TILE = 256
GATHER_ROWS = (64, 128)
GATHER_GROUP = 4


def _gather_kernel(before_ref, fits_ref, h2_ref, slot_ref, w_ref, xe_ref, gate_ref, xe_s, gate_s):
    b = pl.program_id(0)
    g = pl.program_id(1)
    n_g, cap = xe_ref.shape[1], xe_ref.shape[2]
    n = h2_ref.shape[1]
    n_tiles = n // GATHER_TILE
    head = BF16_ROWS

    def windowed(n_rows):
        xe_s[:, 0:head, :] = jnp.zeros((n_g, head, D_MODEL), BF16)
        gate_s[...] = jnp.zeros_like(gate_s)
        row_id = _iota2((n_rows, 1), 0).astype(F32)
        for j in range(n_tiles):
            starts, onehots = [], []
            for e in range(n_g):
                start = pl.multiple_of((before_ref[b, g * n_g + e, j] // head) * head, head)
                hit = (slot_ref[0, e, j:j + 1, :] - start.astype(F32)) == row_id
                onehots.append(jnp.where(hit, 1.0, 0.0).astype(BF16))
                gate_s[e, pl.ds(start, n_rows), :] += jnp.sum(
                    jnp.where(hit, w_ref[0, e, j:j + 1, :], 0.0), axis=1, keepdims=True)
                starts.append(start)
            rows = _dot(jnp.concatenate(onehots, axis=0),
                        h2_ref[0, j * GATHER_TILE:(j + 1) * GATHER_TILE, :]).astype(BF16)
            for e, start in enumerate(starts):
                r0 = e * n_rows
                xe_s[e, pl.ds(start, head), :] += rows[r0:r0 + head]
                xe_s[e, pl.ds(start + head, n_rows - head), :] = rows[r0 + head:r0 + n_rows]
        xe_ref[0] = xe_s[:, 0:cap, :]
        for e in range(n_g):
            gate_ref[0, e] = _cols_to_rows(gate_s[e, 0:cap, :])

    for level, n_rows in enumerate(GATHER_ROWS):
        pl.when(fits_ref[b, g] == level)(functools.partial(windowed, n_rows))

    @pl.when(fits_ref[b, g] == len(GATHER_ROWS))
    def _():
        slot_id = _iota2((cap, 1), 0).astype(F32)
        for e in range(n_g):
            xe = jnp.zeros((cap, D_MODEL), F32)
            gate = jnp.zeros((cap, 1), F32)
            for j in range(n_tiles):
                hit = slot_ref[0, e, j:j + 1, :] == slot_id
                xe = xe + _dot(jnp.where(hit, 1.0, 0.0).astype(BF16),
                               h2_ref[0, j * GATHER_TILE:(j + 1) * GATHER_TILE, :])
                gate = gate + jnp.sum(jnp.where(hit, w_ref[0, e, j:j + 1, :], 0.0), axis=1,
                                      keepdims=True)
            xe_ref[0, e] = xe.astype(BF16)
            gate_ref[0, e] = _cols_to_rows(gate)


def _gather(h2, slot, w, first, capacity):
    b, n, _ = h2.shape
    n_e = slot.shape[1]
    n_tiles = n // GATHER_TILE
    assert n_e % GATHER_GROUP == 0 and GATHER_TILE % LANES == 0 and capacity % LANES == 0
    assert all(r % BF16_ROWS == 0 and r >= 2 * BF16_ROWS for r in GATHER_ROWS)
    assert list(GATHER_ROWS) == sorted(GATHER_ROWS)
    before = first[:, :, 0:n // LANES:GATHER_TILE // LANES].astype(jnp.int32)
    after = jnp.concatenate([before[:, :, 1:], jnp.full((b, n_e, 1), capacity, jnp.int32)], axis=2)
    extent = jnp.max(after - (before // BF16_ROWS) * BF16_ROWS, axis=2)
    extent = jnp.max(extent.reshape(b, n_e // GATHER_GROUP, GATHER_GROUP), axis=2)
    fits = sum((extent >= n_rows).astype(jnp.int32) for n_rows in GATHER_ROWS)
    tiles = lambda a: a.reshape(b, n_e, n_tiles, GATHER_TILE)
    rows = pl.BlockSpec((1, GATHER_GROUP, n_tiles, GATHER_TILE), lambda i, g, *_: (i, g, 0, 0))
    return pl.pallas_call(
        _gather_kernel,
        grid_spec=pltpu.PrefetchScalarGridSpec(
            num_scalar_prefetch=2,
            grid=(b, n_e // GATHER_GROUP),
            in_specs=[pl.BlockSpec((1, n, D_MODEL), lambda i, g, *_: (i, 0, 0)), rows, rows],
            out_specs=[pl.BlockSpec((1, GATHER_GROUP, capacity, D_MODEL), lambda i, g, *_: (i, g, 0, 0)),
                       pl.BlockSpec((1, GATHER_GROUP, capacity // LANES, LANES),
                                    lambda i, g, *_: (i, g, 0, 0))],
            scratch_shapes=[pltpu.VMEM((GATHER_GROUP, capacity + max(GATHER_ROWS), D_MODEL), BF16),
                            pltpu.VMEM((GATHER_GROUP, capacity + max(GATHER_ROWS), 1), F32)]),
        out_shape=[jax.ShapeDtypeStruct((b, n_e, capacity, D_MODEL), BF16),
                   jax.ShapeDtypeStruct((b, n_e, capacity // LANES, LANES), F32)],
        compiler_params=_params("arbitrary", "arbitrary"),
        name="gather",
    )(before, fits, h2, tiles(slot), tiles(w))


def _ffn_kernel(xe_ref, gate_ref, wg_ref, wu_ref, wd_ref, ye_ref, acc_s):
    ft = pl.program_id(1)
    last = pl.num_programs(1) - 1

    def step(first, final):
        wg = wg_ref[0].astype(BF16)
        wu = wu_ref[0].astype(BF16)
        wd = wd_ref[0].astype(BF16)
        for i in range(xe_ref.shape[0]):
            xe = xe_ref[i, 0]
            act = jax.nn.silu(_dot(xe, wg)) * _dot(xe, wu)
            part = _dot(act.astype(BF16), wd)
            total = part if first else acc_s[i] + part
            if final:
                ye_ref[i] = (total * _rows_to_cols(gate_ref[i, 0])).astype(BF16)
            else:
                acc_s[i] = total

    pl.when(ft == 0)(lambda: step(True, False))
    pl.when(jnp.logical_and(ft > 0, ft < last))(lambda: step(False, False))
    pl.when(ft == last)(lambda: step(False, True))


def _ffn(xe, gate, wg, wu, wd, f_tile):
    b, n_e, cap, _ = xe.shape
    assert D_EXPERT // f_tile >= 2
    return pl.pallas_call(
        _ffn_kernel,
        grid=(n_e, D_EXPERT // f_tile),
        in_specs=[pl.BlockSpec((b, 1, cap, D_MODEL), lambda e, f: (0, e, 0, 0)),
                  pl.BlockSpec((b, 1, cap // LANES, LANES), lambda e, f: (0, e, 0, 0)),
                  pl.BlockSpec((1, D_MODEL, f_tile), lambda e, f: (e, 0, f)),
                  pl.BlockSpec((1, D_MODEL, f_tile), lambda e, f: (e, 0, f)),
                  pl.BlockSpec((1, f_tile, D_MODEL), lambda e, f: (e, f, 0))],
        out_specs=pl.BlockSpec((b, cap, D_MODEL), lambda e, f: (0, e, 0)),
        out_shape=jax.ShapeDtypeStruct((b, n_e * cap, D_MODEL), BF16),
        scratch_shapes=[pltpu.VMEM((b, cap, D_MODEL), F32)],
        compiler_params=_params("arbitrary", "arbitrary"),
        name="ffn",
    )(xe, gate, wg, wu, wd)


COMBINE_WIDE = CHUNK + BF16_ROWS
COMBINE_NARROW = 48


def _combine_kernel(first_ref, narrow_ref, slot_t_ref, first_t_ref, ye_ref, x1_ref, mod_ref, fg_ref,
                    o_ref, acc_s, *, capacity):
    b = pl.program_id(0)
    j = pl.program_id(1)
    n_sub = o_ref.shape[1] // CHUNK
    lane = _iota2((1, LANES), 1).astype(F32)

    def scatter(window):
        k_total = N_EXPERTS * window
        for sb in range(n_sub):
            blk = j * n_sub + sb
            slot_t = slot_t_ref[0, sb * CHUNK:(sb + 1) * CHUNK, :]
            start_row = jnp.minimum(
                jnp.floor(first_t_ref[0, pl.ds(blk, 1), :] * (1.0 / BF16_ROWS)) * float(BF16_ROWS),
                float(capacity - window))
            k_pos = jnp.where(slot_t >= 0.0, slot_t - start_row + lane * float(window), -1.0)
            cols = []
            for c in range(k_total // LANES):
                k_lane = lane + float(LANES * c)
                hit = jnp.zeros((CHUNK, LANES), F32)
                for e in range((LANES * c) // window, (LANES * c + LANES - 1) // window + 1):
                    hit = jnp.where(k_pos[:, e:e + 1] == k_lane, 1.0, hit)
                cols.append(hit.astype(BF16))
            onehot = jnp.concatenate(cols, axis=1)
            rows = []
            for e in range(N_EXPERTS):
                start = jnp.minimum((first_ref[b, blk, e] // BF16_ROWS) * BF16_ROWS, capacity - window)
                rows.append(ye_ref[0, pl.ds(pl.multiple_of(e * capacity + start, BF16_ROWS), window), :])
            acc_s[sb * CHUNK:(sb + 1) * CHUNK, :] = _dot(onehot, jnp.concatenate(rows, axis=0))

    pl.when(narrow_ref[b, j] != 0)(lambda: scatter(COMBINE_NARROW))
    pl.when(narrow_ref[b, j] == 0)(lambda: scatter(COMBINE_WIDE))

    g2 = mod_ref[pl.ds(b, 1), 5 * D_MODEL:6 * D_MODEL]
    x2 = x1_ref[0] + g2 * acc_s[...]
    o_ref[0] = x2 * lax.rsqrt(jnp.mean(x2 * x2, axis=-1, keepdims=True) + EPS) * fg_ref[...]


def _combine(slot_t, first_t, ye, x1, mod, fg, capacity, tm):
    b, n, _ = x1.shape
    n_blocks = n // CHUNK
    assert CHUNK == LANES and N_EXPERTS <= LANES and tm % CHUNK == 0
    for window in (COMBINE_WIDE, COMBINE_NARROW):
        assert (N_EXPERTS * window) % LANES == 0 and window % BF16_ROWS == 0 and window <= capacity
    assert COMBINE_WIDE >= CHUNK + BF16_ROWS - 1 and capacity % BF16_ROWS == 0
    first = first_t[:, :, :N_EXPERTS].astype(jnp.int32)
    after = jnp.concatenate([first[:, 1:], jnp.full((b, 1, N_EXPERTS), capacity, jnp.int32)], axis=1)
    narrow = jnp.all(after - (first // BF16_ROWS) * BF16_ROWS <= COMBINE_NARROW, axis=2)
    narrow = jnp.all(narrow.reshape(b, n // tm, tm // CHUNK), axis=2).astype(jnp.int32)
    tok = pl.BlockSpec((1, tm, D_MODEL), lambda i, j, *_: (i, j, 0))
    return pl.pallas_call(
        functools.partial(_combine_kernel, capacity=capacity),
        grid_spec=pltpu.PrefetchScalarGridSpec(
            num_scalar_prefetch=2,
            grid=(b, n // tm),
            in_specs=[pl.BlockSpec((1, tm, LANES), lambda i, j, *_: (i, j, 0)),
                      pl.BlockSpec((1, n_blocks, LANES), lambda i, j, *_: (i, 0, 0)),
                      pl.BlockSpec((1, N_EXPERTS * capacity, D_MODEL), lambda i, j, *_: (i, 0, 0)),
                      tok,
                      pl.BlockSpec(mod.shape, lambda i, j, *_: (0, 0)),
                      pl.BlockSpec((1, D_MODEL), lambda i, j, *_: (0, 0))],
            out_specs=tok,
            scratch_shapes=[pltpu.VMEM((tm, D_MODEL), F32)]),
        out_shape=jax.ShapeDtypeStruct((b, n, D_MODEL), F32),
        compiler_params=_params("arbitrary", "arbitrary"),
        name="combine",
    )(first, narrow, slot_t, first_t, ye, x1, mod, fg)


def kernel(x, c, ctx, c_ctx, w_mod, b_mod, norm_mix_g, w_in, conv_q, conv_k, b_igate, b_fgate,
           gmlp_ws, gmlp_bs, mlstm_norm_g, w_out, norm_ffn_g, w_router, w_gate_e, w_up_e,
           w_down_e, final_g):
    depth = w_mod.shape[0]
    assert depth == 1, "the context stream is only carried as mLSTM states (single layer)"
    batch, seq, _ = x.shape
    assert seq % GRID_W == 0 and seq % CHUNK == 0 and batch + 1 <= MOD_ROWS
    capacity = EC_FACTOR * seq // N_EXPERTS
    ctx_row = batch
    l = 0

    cond = jnp.concatenate([c, c_ctx[None], jnp.zeros((MOD_ROWS - batch - 1, D_MODEL), F32)], axis=0)
    mod = _adaln(cond, w_mod[l], b_mod[l][None])

    row = lambda a: a[None]
    w_gates = jnp.pad(w_in[l][:, MAIN_W:], ((0, 0), (0, GATE_PAD - N_GATES)))
    cols = lambda *blks: jnp.concatenate(
        [w_in[l][:, k * GROUP_W:(k + 1) * GROUP_W] for k in blks] + [w_gates], axis=1).astype(BF16)
    gate_bias = jnp.pad(jnp.concatenate([b_igate[l].reshape(-1), b_fgate[l].reshape(-1)]),
                        (0, GATE_PAD - N_GATES))[None]
    k_scale = HEAD_DIM ** -0.5

    identity = lambda a: a
    k_c, v_c, gates_c = _inproj(ctx, mod, row(norm_mix_g[l]), cols(K_BLK, V_BLK), conv_k[l][None],
                                (k_scale,), (identity,), tm=ctx.shape[1], ctx_row=ctx_row)
    state = _mlstm(None, k_c, v_c, 0, gates_c, gate_bias, None)

    q_l, k_l, p, gates = _inproj(x, mod, row(norm_mix_g[l]),
                                 cols(Q_BLK, K_BLK, U_BLK, VG_BLK, O_BLK, V_BLK),
                                 jnp.stack([conv_q[l], conv_k[l]]), (1.0, k_scale),
                                 (jax.nn.gelu, lambda a: _layer_norm(jax.nn.gelu(a)), jax.nn.sigmoid, identity),
                                 tm=512, ctx_row=None)
    h_f, h_b = _mlstm(q_l, k_l, p, P_V, gates, gate_bias, state)

    x1, h2, aff_t = _postmix(p, h_f, h_b, x, mod, gmlp_ws[l].astype(BF16), gmlp_bs[l][:, :, None],
                             row(mlstm_norm_g[l]), w_out[l].astype(BF16), row(norm_ffn_g[l]),
                             jnp.pad(w_router[l], ((0, 0), (0, LANES - N_EXPERTS))), tm=1024)

    slot, gate_w, slot_t, first_t, first = _route(aff_t, capacity)

    xe, gate = _gather(h2, slot, gate_w, first, capacity)
    ye = _ffn(xe, gate, w_gate_e[l], w_up_e[l], w_down_e[l], f_tile=512)
    return _combine(slot_t, first_t, ye, x1, mod, row(final_g), capacity, tm=512)
```

```python
import functools

import jax
import jax.numpy as jnp
from jax import lax
from jax.experimental import pallas as pl
from jax.experimental.pallas import tpu as pltpu

F32 = jnp.float32
BF16 = jnp.bfloat16

LANES = 128
BF16_ROWS = 16
VMEM_LIMIT = 56 * 1024 * 1024

D_MODEL = 1024
GRID_W = 64
CHUNK = 128
HEADS = 4
GROUP_W = D_MODEL // 2
HEAD_DIM = GROUP_W // HEADS
N_EXPERTS = 16
EC_FACTOR = 2
D_EXPERT = 2 * D_MODEL
EPS = 1e-6
N_GATES = 4 * HEADS
GATE_PAD = LANES
MOD_ROWS = 8

U_BLK, VG_BLK, Q_BLK, O_BLK, K_BLK, V_BLK = 0, 1, 2, 3, 4, 5
MAIN_W = 6 * GROUP_W
P_U, P_VG, P_O, P_V = 0, 1, 2, 3

ROW_GROUP = 2 * CHUNK
MLSTM_SAMPLES = 4


def _params(*sem):
    return pltpu.CompilerParams(dimension_semantics=sem, vmem_limit_bytes=VMEM_LIMIT)


def _dot(a, b):
    return jnp.dot(a, b, preferred_element_type=F32)


def _dot_nt(a, b):
    return lax.dot_general(a, b, (((1,), (1,)), ((), ())), preferred_element_type=F32)


def _split2(a):
    hi = a.astype(BF16)
    lo = (a - hi.astype(F32)).astype(BF16)
    return hi, lo


def _dot3(a, b):
    ah, al = _split2(a)
    bh, bl = _split2(b)
    return _dot(ah, bh) + (_dot(al, bh) + _dot(ah, bl))


def _dot_exact01(tri, x):
    x1 = x.astype(BF16)
    r1 = x - x1.astype(F32)
    x2 = r1.astype(BF16)
    x3 = (r1 - x2.astype(F32)).astype(BF16)
    return _dot(tri, x1) + (_dot(tri, x2) + _dot(tri, x3))


def _iota2(shape, dim):
    return lax.broadcasted_iota(jnp.int32, shape, dim)


def _eye():
    return _iota2((LANES, LANES), 0) == _iota2((LANES, LANES), 1)


def _cols_to_rows(col):
    return jnp.concatenate(
        [jnp.sum(jnp.where(_eye(), col[r * LANES:(r + 1) * LANES, :], 0.0), axis=0, keepdims=True)
         for r in range(col.shape[0] // LANES)], axis=0)


def _rows_to_cols(rows):
    return jnp.concatenate(
        [jnp.sum(jnp.where(_eye(), rows[r:r + 1, :], 0.0), axis=1, keepdims=True)
         for r in range(rows.shape[0])], axis=0)


def _adaln_kernel(cond_ref, w_ref, b_ref, o_ref):
    o_ref[...] = _dot3(jax.nn.silu(cond_ref[...]), w_ref[...]) + b_ref[...]


def _adaln(cond, w, b):
    n_out = w.shape[1]
    tn = D_MODEL
    assert n_out % tn == 0
    return pl.pallas_call(
        _adaln_kernel,
        grid=(n_out // tn,),
        in_specs=[pl.BlockSpec((MOD_ROWS, D_MODEL), lambda j: (0, 0)),
                  pl.BlockSpec((D_MODEL, tn), lambda j: (0, j)),
                  pl.BlockSpec((1, tn), lambda j: (0, j))],
        out_specs=pl.BlockSpec((MOD_ROWS, tn), lambda j: (0, j)),
        out_shape=jax.ShapeDtypeStruct((MOD_ROWS, n_out), F32),
        compiler_params=_params("arbitrary"),
        name="adaln",
    )(cond, w, b)


def _inproj_kernel(x_ref, xprev_ref, xnext_ref, mod_ref, g_ref, taps_ref, w_ref, *outs, ctx_row, scales,
                   plain_acts, raw_blocks):
    conv_refs, (p_ref, gate_ref) = outs[:len(scales)], outs[len(scales):len(scales) + 2]
    row = pl.program_id(0) if ctx_row is None else ctx_row
    j = pl.program_id(1)
    if raw_blocks is None:
        w_bf = w_ref
    else:
        w_bf = outs[-1]

        @pl.when(jnp.logical_and(pl.program_id(0) == 0, j == 0))
        def _():
            for k, blk in enumerate(raw_blocks):
                w_bf[:, k * GROUP_W:(k + 1) * GROUP_W] = (
                    w_ref[:, blk * GROUP_W:(blk + 1) * GROUP_W].astype(BF16))
            g0 = len(raw_blocks) * GROUP_W
            n_gate = w_ref.shape[1] - MAIN_W
            w_bf[:, g0:g0 + GATE_PAD] = jnp.zeros((D_MODEL, GATE_PAD), BF16)
            w_bf[:, g0:g0 + n_gate] = w_ref[:, MAIN_W:MAIN_W + n_gate].astype(BF16)
    sh = mod_ref[pl.ds(row, 1), 0:D_MODEL]
    sc = mod_ref[pl.ds(row, 1), D_MODEL:2 * D_MODEL]

    def modulated(x):
        y = x * lax.rsqrt(jnp.mean(x * x, axis=-1, keepdims=True) + EPS) * g_ref[...]
        return y * (1.0 + sc) + sh

    n_groups = x_ref.shape[1] // ROW_GROUP
    before = modulated(xprev_ref[0]) * (j > 0).astype(F32)
    for r in range(n_groups):
        rows = slice(r * ROW_GROUP, (r + 1) * ROW_GROUP)
        h = modulated(x_ref[0, rows, :])
        if r == n_groups - 1:
            after = modulated(xnext_ref[0]) * (j < pl.num_programs(1) - 1).astype(F32)
        else:
            after = modulated(x_ref[0, (r + 1) * ROW_GROUP:(r + 1) * ROW_GROUP + 8, :])
        ext = jnp.concatenate([before, h, after], axis=0).astype(BF16)
        before = h[ROW_GROUP - 8:, :]
        y = _dot(ext, w_bf[...])
        n_ext, n_conv = ext.shape[0], len(scales) * GROUP_W
        y_here = y[8:8 + ROW_GROUP]
        for k, act in enumerate(plain_acts):
            lanes = slice(n_conv + k * GROUP_W, n_conv + (k + 1) * GROUP_W)
            p_ref[0, rows, k * GROUP_W:(k + 1) * GROUP_W] = act(y_here[:, lanes])
        gate_ref[0, rows, :] = y_here[:, n_conv + p_ref.shape[2]:]
        y_conv = y[:, 0:n_conv]
        y_before = pltpu.roll(y_conv, 1, axis=0)[8:8 + ROW_GROUP]
        y_after = pltpu.roll(y_conv, n_ext - 1, axis=0)[8:8 + ROW_GROUP]
        for s, scale in enumerate(scales):
            lanes = slice(s * GROUP_W, (s + 1) * GROUP_W)
            z = (y_before[:, lanes] * taps_ref[s, 0:1, :] + y_here[:, lanes] * taps_ref[s, 1:2, :]
                 + y_after[:, lanes] * taps_ref[s, 2:3, :])
            conv_refs[s][0, rows, :] = (jax.nn.silu(z) * scale).astype(BF16)


def _inproj(x, mod, g, w, taps, scales, plain_acts, tm, ctx_row, raw_blocks=None):
    b, n, _ = x.shape
    wn = len(plain_acts) * GROUP_W
    w_cols = len(scales) * GROUP_W + wn + GATE_PAD
    if raw_blocks is None:
        assert w.shape[1] == w_cols
        scratch = []
    else:
        assert len(raw_blocks) == len(scales) + len(plain_acts) and 0 < w.shape[1] - MAIN_W <= GATE_PAD
        scratch = [pltpu.VMEM((D_MODEL, w_cols), BF16)]
    assert tm % ROW_GROUP == 0 and n % tm == 0 and wn % LANES == 0
    rows8 = tm // 8
    last8 = n // 8 - 1
    full = lambda a: pl.BlockSpec(a.shape, lambda i, j: (0,) * a.ndim)
    conv_spec = pl.BlockSpec((1, tm, GROUP_W), lambda i, j: (i, j, 0))
    return pl.pallas_call(
        functools.partial(_inproj_kernel, ctx_row=ctx_row, scales=tuple(scales),
                          plain_acts=tuple(plain_acts), raw_blocks=raw_blocks),
        grid=(b, n // tm),
        in_specs=[pl.BlockSpec((1, tm, D_MODEL), lambda i, j: (i, j, 0)),
                  pl.BlockSpec((1, 8, D_MODEL), lambda i, j: (i, jnp.maximum(j * rows8 - 1, 0), 0)),
                  pl.BlockSpec((1, 8, D_MODEL), lambda i, j: (i, jnp.minimum((j + 1) * rows8, last8), 0)),
                  full(mod), full(g), full(taps), full(w)],
        out_specs=[conv_spec] * len(scales)
        + [pl.BlockSpec((1, tm, wn), lambda i, j: (i, j, 0)),
           pl.BlockSpec((1, tm, GATE_PAD), lambda i, j: (i, j, 0))],
        out_shape=[jax.ShapeDtypeStruct((b, n, GROUP_W), BF16)] * len(scales)
        + [jax.ShapeDtypeStruct((b, n, wn), F32), jax.ShapeDtypeStruct((b, n, GATE_PAD), F32)],
        scratch_shapes=scratch,
        compiler_params=_params("arbitrary", "arbitrary"),
        name="inproj",
    )(x, x, x, mod, g, taps, w)


def _mlstm_chunk(per_dir, bias_ref, c_s, n_s, m_s, h_refs):
    with_h = h_refs[0] is not None
    i0 = _iota2((CHUNK, CHUNK), 0)
    i1 = _iota2((CHUNK, CHUNK), 1)

    units = []
    for smp, d in [(smp, d) for smp in range(c_s.shape[0]) for d in range(2)]:
        q_ref, k_ref, v_ref, g_ref = per_dir[d]
        sees_ts = (i1 <= i0) if d == 0 else (i1 >= i0)
        sees_st = (i0 <= i1) if d == 0 else (i0 >= i1)
        tri = jnp.where(sees_ts, 1.0, 0.0).astype(BF16)
        gates = g_ref[smp] + bias_ref[...]
        bcum = _dot_exact01(tri, jax.nn.log_sigmoid(gates))
        for hd in range(HEADS):
            lanes = slice(hd * HEAD_DIM, (hd + 1) * HEAD_DIM)
            c_prev = c_s[smp, d, hd]
            n_prev = n_s[smp, d, hd]
            k = k_ref[smp, :, lanes]
            st = dict(smp=smp, d=d, hd=hd, lanes=lanes, c_prev=c_prev, n_prev=n_prev, k=k,
                      sees_st=sees_st, gates=gates, bcum=bcum, v=v_ref[smp, :, lanes])
            if with_h:
                lhs = jnp.concatenate([k, c_prev.astype(BF16),
                                       jnp.broadcast_to(n_prev, (BF16_ROWS, HEAD_DIM)).astype(BF16)], axis=0)
                st["prod"] = _dot_nt(lhs, q_ref[smp, :, lanes])
            units.append(st)

    rows = {}
    for st in units:
        smp, d, hd = st["smp"], st["d"], st["hd"]
        if (smp, d) not in rows:
            rows[smp, d] = (st["gates"].T, st["bcum"].T)
        gates_t, bcum_t = rows[smp, d]
        ci = d * HEADS + hd
        cf = 2 * HEADS + d * HEADS + hd
        last = CHUNK - 1 if d == 0 else 0
        li_row = gates_t[ci:ci + 1, :]
        bc_row = bcum_t[cf:cf + 1, :]
        b_last = bc_row[:, last:last + 1]
        m_prev = m_s[smp, d, hd]
        v_t = st["v"].T
        a_row = b_last - bc_row + li_row
        m_new = jnp.maximum(b_last + m_prev, jnp.max(a_row, axis=-1, keepdims=True))
        w_row = jnp.exp(a_row - m_new)
        lhs = jnp.concatenate([v_t * w_row, jnp.broadcast_to(w_row, (BF16_ROWS, CHUNK))], axis=0)
        st["upd"] = _dot(lhs.astype(BF16), st["k"])
        st.update(bc_row=bc_row, m_prev=m_prev, m_new=m_new, v_t=v_t,
                  decay=jnp.exp(b_last + m_prev - m_new),
                  u_col=st["gates"][:, ci:ci + 1] - st["bcum"][:, cf:cf + 1])

    if with_h:
        for st in units:
            prod, bc_row = st["prod"], st["bc_row"]
            g = bc_row + st["m_prev"]
            dmat = jnp.where(st["sees_st"], st["u_col"] + bc_row, -jnp.inf)
            m_t = jnp.maximum(g, jnp.max(dmat, axis=0, keepdims=True))
            inter = jnp.exp(g - m_t)
            s = prod[0:CHUNK] * jnp.exp(dmat - m_t)
            st["pv"] = _dot(st["v_t"].astype(BF16), s.astype(BF16))
            st["num0"] = inter * prod[CHUNK:2 * CHUNK]
            den = inter * prod[2 * CHUNK:2 * CHUNK + 1] + jnp.sum(s, axis=0, keepdims=True)
            st["scale"] = 1.0 / jnp.maximum(jnp.abs(den), jnp.exp(-m_t))

    for st in units:
        smp, d, hd = st["smp"], st["d"], st["hd"]
        if with_h:
            h_refs[d][smp, :, st["lanes"]] = ((st["num0"] + st["pv"]) * st["scale"]).T
        c_s[smp, d, hd] = st["decay"] * st["c_prev"] + st["upd"][0:HEAD_DIM]
        n_s[smp, d, hd] = st["decay"] * st["n_prev"] + st["upd"][HEAD_DIM:HEAD_DIM + 1]
        m_s[smp, d, hd] = st["m_new"]


def _mlstm_kernel(*refs, n_chunks, with_h):
    refs = list(refs)
    take = lambda n: [refs.pop(0) for _ in range(n)]
    per_dir = []
    for _ in range(2):
        q_ref = take(1)[0] if with_h else None
        k_ref, v_ref, g_ref = take(3)
        per_dir.append((q_ref, k_ref, v_ref, g_ref))
    bias_ref = take(1)[0]
    if with_h:
        c0_ref, n0_ref, m0_ref = take(3)
        h_refs = take(2)
        c_out = n_out = m_out = None
    else:
        h_refs = [None, None]
        c_out, n_out, m_out = take(3)
    c_s, n_s, m_s = take(3)

    j = pl.program_id(1)

    @pl.when(j == 0)
    def _():
        if with_h:
            c_s[...] = c0_ref[...]
            n_s[...] = n0_ref[...]
            m_s[...] = m0_ref[...]
        else:
            c_s[...] = jnp.zeros_like(c_s)
            n_s[...] = jnp.zeros_like(n_s)
            m_s[...] = jnp.zeros_like(m_s)

    _mlstm_chunk(per_dir, bias_ref, c_s, n_s, m_s, h_refs)

    if not with_h:
        @pl.when(j == n_chunks - 1)
        def _():
            c_out[...] = c_s[...]
            n_out[...] = n_s[...]
            m_out[...] = m_s[...]


def _mlstm(q, k, p, v_blk, gates, bias, state):
    b, n, _ = k.shape
    n_chunks = n // CHUNK
    with_h = q is not None
    ns = MLSTM_SAMPLES if b % MLSTM_SAMPLES == 0 else 1

    in_specs, args = [], []
    for d in range(2):
        c = (lambda j: j) if d == 0 else (lambda j: n_chunks - 1 - j)
        tok = pl.BlockSpec((ns, CHUNK, GROUP_W), lambda i, j, c=c: (i, c(j), 0))
        if with_h:
            in_specs.append(tok)
            args.append(q)
        in_specs += [tok,
                     pl.BlockSpec((ns, CHUNK, GROUP_W), lambda i, j, c=c: (i, c(j), v_blk)),
                     pl.BlockSpec((ns, CHUNK, GATE_PAD), lambda i, j, c=c: (i, c(j), 0))]
        args += [k, p, gates]
    in_specs.append(pl.BlockSpec((1, GATE_PAD), lambda i, j: (0, 0)))
    args.append(bias)

    c_shape = (2, HEADS, HEAD_DIM, HEAD_DIM)
    v_shape = (2, HEADS, 1, HEAD_DIM)
    c_spec = pl.BlockSpec((ns,) + c_shape, lambda i, j: (i, 0, 0, 0, 0))
    v_spec = pl.BlockSpec((ns,) + v_shape, lambda i, j: (i, 0, 0, 0, 0))
    if with_h:
        in_specs += [c_spec, v_spec, v_spec]
        args += list(state)
        out_specs = [pl.BlockSpec((ns, CHUNK, GROUP_W), lambda i, j: (i, j, 0)),
                     pl.BlockSpec((ns, CHUNK, GROUP_W), lambda i, j: (i, n_chunks - 1 - j, 0))]
        out_shape = [jax.ShapeDtypeStruct((b, n, GROUP_W), F32)] * 2
    else:
        out_specs = [c_spec, v_spec, v_spec]
        out_shape = [jax.ShapeDtypeStruct((b,) + c_shape, F32),
                     jax.ShapeDtypeStruct((b,) + v_shape, F32),
                     jax.ShapeDtypeStruct((b,) + v_shape, F32)]
    return pl.pallas_call(
        functools.partial(_mlstm_kernel, n_chunks=n_chunks, with_h=with_h),
        grid=(b // ns, n_chunks),
        in_specs=in_specs,
        out_specs=out_specs,
        out_shape=out_shape,
        scratch_shapes=[pltpu.VMEM((ns,) + c_shape, F32), pltpu.VMEM((ns,) + v_shape, F32),
                        pltpu.VMEM((ns,) + v_shape, F32)],
        compiler_params=_params("arbitrary", "arbitrary"),
        name="mlstm" if with_h else "mlstm_ctx_state",
    )(*args)


def _layer_norm(x):
    mu = jnp.mean(x, axis=-1, keepdims=True)
    var = jnp.mean(jnp.square(x - mu), axis=-1, keepdims=True)
    return (x - mu) * lax.rsqrt(var + EPS)


def _postmix_kernel(u_ref, vg_ref, o_ref, hf_ref, hb_ref, x_ref, mod_ref, ws_ref, bs_ref, ng_ref,
                    wout_ref, fg_ref, wr_ref, x1_ref, h2_ref, aff_ref, ycat_s):
    tm = x_ref.shape[1]
    b = pl.program_id(0)
    mod = lambda k: mod_ref[pl.ds(b, 1), k * D_MODEL:(k + 1) * D_MODEL]
    wr_hi, wr_lo = _split2(wr_ref[...])
    expert_lane = _iota2((ROW_GROUP, LANES), 1) < N_EXPERTS

    for r in range(tm // ROW_GROUP):
        rows = slice(r * ROW_GROUP, (r + 1) * ROW_GROUP)

        for c in range(r * ROW_GROUP // CHUNK, (r + 1) * ROW_GROUP // CHUNK):
            crows = slice(c * CHUNK, (c + 1) * CHUNK)
            u = u_ref[0, crows, :]
            v = vg_ref[0, crows, :].astype(BF16)
            for hd in range(HEADS):
                lanes = slice(hd * HEAD_DIM, (hd + 1) * HEAD_DIM)
                s = _dot(ws_ref[hd], v[:, lanes]) + bs_ref[hd]
                ycat_s[crows, lanes] = (u[:, lanes] * s).astype(BF16)

        hsum = hf_ref[0, rows, :] + hb_ref[0, rows, :]
        o = o_ref[0, rows, :]
        for hd in range(HEADS):
            lanes = slice(hd * HEAD_DIM, (hd + 1) * HEAD_DIM)
            hn = _layer_norm(hsum[:, lanes]) * ng_ref[:, lanes]
            ycat_s[rows, GROUP_W + hd * HEAD_DIM:GROUP_W + (hd + 1) * HEAD_DIM] = (
                o[:, lanes] * hn).astype(BF16)

        y = _dot(ycat_s[rows, :], wout_ref[...])
        x1 = x_ref[0, rows, :] + mod(2) * y
        x1_ref[0, rows, :] = x1

        n2 = x1 * lax.rsqrt(jnp.mean(x1 * x1, axis=-1, keepdims=True) + EPS) * fg_ref[...]
        h2 = n2 * (1.0 + mod(4)) + mod(3)
        h2_hi, h2_lo = _split2(h2)
        h2_ref[0, rows, :] = h2_hi

        logits = _dot(h2_hi, wr_hi) + (_dot(h2_lo, wr_hi) + _dot(h2_hi, wr_lo))
        logits = jnp.where(expert_lane, logits, -jnp.inf)
        e = jnp.exp(logits - jnp.max(logits, axis=-1, keepdims=True))
        aff = e / jnp.sum(e, axis=-1, keepdims=True)
        aff_ref[0, :, rows] = aff.T[0:N_EXPERTS, :]


def _postmix(p, hf, hb, x, mod, ws, bs, ng, wout, fg, wr, tm):
    b, n, _ = x.shape
    tok = lambda blk: pl.BlockSpec((1, tm, GROUP_W), lambda i, j: (i, j, blk))
    full = lambda a: pl.BlockSpec(a.shape, lambda i, j: (0,) * a.ndim)
    return pl.pallas_call(
        _postmix_kernel,
        grid=(b, n // tm),
        in_specs=[tok(P_U), tok(P_VG), tok(P_O), tok(0), tok(0),
                  pl.BlockSpec((1, tm, D_MODEL), lambda i, j: (i, j, 0)),
                  full(mod), full(ws), full(bs), full(ng), full(wout), full(fg), full(wr)],
        out_specs=[pl.BlockSpec((1, tm, D_MODEL), lambda i, j: (i, j, 0)),
                   pl.BlockSpec((1, tm, D_MODEL), lambda i, j: (i, j, 0)),
                   pl.BlockSpec((1, N_EXPERTS, tm), lambda i, j: (i, 0, j))],
        out_shape=[jax.ShapeDtypeStruct((b, n, D_MODEL), F32),
                   jax.ShapeDtypeStruct((b, n, D_MODEL), BF16),
                   jax.ShapeDtypeStruct((b, N_EXPERTS, n), F32)],
        scratch_shapes=[pltpu.VMEM((tm, D_MODEL), BF16)],
        compiler_params=_params("arbitrary", "arbitrary"),
        name="postmix",
    )(p, p, p, hf, hb, x, mod, ws, bs, ng, wout, fg, wr)


def _cumsum_lanes(x, upper):
    carry = jnp.zeros((x.shape[0], 1), F32)
    outs, before = [], []
    for j in range(x.shape[1] // LANES):
        before.append(carry)
        c = _dot(x[:, j * LANES:(j + 1) * LANES].astype(BF16), upper) + carry
        outs.append(c)
        carry = c[:, LANES - 1:LANES]
    return jnp.concatenate(outs, axis=1), before


def _rows_to_lanes(x, fill):
    pad = jnp.full((LANES - x.shape[0], LANES), fill, F32)
    return jnp.concatenate([x, pad], axis=0).T


def _route_kernel(aff_ref, slot_ref, w_ref, slot_t_ref, first_t_ref, first_ref, *, capacity, n_e):
    aff = aff_ref[...]
    cap = float(capacity)
    thr_bits = jnp.zeros((aff.shape[0], 1), jnp.int32)
    for bit in range(30, -1, -1):
        cand = thr_bits | (1 << bit)
        cnt = jnp.sum(jnp.where(aff >= pltpu.bitcast(cand, F32), 1.0, 0.0), axis=-1, keepdims=True)
        thr_bits = jnp.where(cnt >= cap, cand, thr_bits)
    thr = pltpu.bitcast(thr_bits, F32)
    upper = jnp.where(_iota2((LANES, LANES), 0) <= _iota2((LANES, LANES), 1), 1.0, 0.0).astype(BF16)
    above = jnp.where(aff > thr, 1.0, 0.0)
    tied = jnp.where(aff == thr, 1.0, 0.0)
    need = cap - jnp.sum(above, axis=-1, keepdims=True)
    sel = above + tied * jnp.where(_cumsum_lanes(tied, upper)[0] <= need, 1.0, 0.0)
    count, before = _cumsum_lanes(sel, upper)
    slot = jnp.where(sel > 0.0, count - 1.0, -1.0)
    slot_ref[...] = slot
    w_ref[...] = jnp.where(sel > 0.0, aff, 0.0)

    n_blocks = len(before)
    lane = _iota2((1, LANES), 1)
    first = jnp.zeros((aff.shape[0], LANES), F32)
    for j in range(n_blocks):
        first = jnp.where(lane == j, before[j], first)
    first_ref[...] = first
    for smp in range(aff.shape[0] // n_e):
        rows = slice(smp * n_e, (smp + 1) * n_e)
        for j in range(n_blocks):
            slot_t_ref[smp, j * LANES:(j + 1) * LANES, :] = _rows_to_lanes(
                slot[rows, j * LANES:(j + 1) * LANES], -1.0)
        first_t_ref[smp] = _rows_to_lanes(first[rows, :], 0.0)[0:n_blocks, :]


def _route(aff_t, capacity):
    b, e, n = aff_t.shape
    assert n // LANES <= LANES and e <= LANES and e % 8 == 0
    rows = pl.BlockSpec((b * e, n), lambda i: (0, 0))
    slot, w, slot_t, first_t, first = pl.pallas_call(
        functools.partial(_route_kernel, capacity=capacity, n_e=e),
        grid=(1,),
        in_specs=[rows],
        out_specs=[rows, rows,
                   pl.BlockSpec((b, n, LANES), lambda i: (0, 0, 0)),
                   pl.BlockSpec((b, n // LANES, LANES), lambda i: (0, 0, 0)),
                   pl.BlockSpec((b * e, LANES), lambda i: (0, 0))],
        out_shape=[jax.ShapeDtypeStruct((b * e, n), F32)] * 2
        + [jax.ShapeDtypeStruct((b, n, LANES), F32), jax.ShapeDtypeStruct((b, n // LANES, LANES), F32),
           jax.ShapeDtypeStruct((b * e, LANES), F32)],
        compiler_params=_params("arbitrary"),
        name="route",
    )(aff_t.reshape(b * e, n))
    return slot.reshape(b, e, n), w.reshape(b, e, n), slot_t, first_t, first.reshape(b, e, LANES)


GATHER_TILE = 256
GATHER_ROWS = (64, 128)
GATHER_GROUP = 4


def _gather_kernel(before_ref, fits_ref, h2_ref, slot_ref, w_ref, xe_ref, gate_ref, xe_s, gate_s):
    b = pl.program_id(0)
    g = pl.program_id(1)
    n_g, cap = xe_ref.shape[1], xe_ref.shape[2]
    n = h2_ref.shape[1]
    n_tiles = n // GATHER_TILE
    head = BF16_ROWS

    def windowed(n_rows):
        xe_s[:, 0:head, :] = jnp.zeros((n_g, head, D_MODEL), BF16)
        gate_s[...] = jnp.zeros_like(gate_s)
        row_id = _iota2((n_rows, 1), 0).astype(F32)
        for j in range(n_tiles):
            starts, onehots = [], []
            for e in range(n_g):
                start = pl.multiple_of((before_ref[b, g * n_g + e, j] // head) * head, head)
                hit = (slot_ref[0, e, j:j + 1, :] - start.astype(F32)) == row_id
                onehots.append(jnp.where(hit, 1.0, 0.0).astype(BF16))
                gate_s[e, pl.ds(start, n_rows), :] += jnp.sum(
                    jnp.where(hit, w_ref[0, e, j:j + 1, :], 0.0), axis=1, keepdims=True)
                starts.append(start)
            rows = _dot(jnp.concatenate(onehots, axis=0),
                        h2_ref[0, j * GATHER_TILE:(j + 1) * GATHER_TILE, :]).astype(BF16)
            for e, start in enumerate(starts):
                r0 = e * n_rows
                xe_s[e, pl.ds(start, head), :] += rows[r0:r0 + head]
                xe_s[e, pl.ds(start + head, n_rows - head), :] = rows[r0 + head:r0 + n_rows]
        xe_ref[0] = xe_s[:, 0:cap, :]
        for e in range(n_g):
            gate_ref[0, e] = _cols_to_rows(gate_s[e, 0:cap, :])

    for level, n_rows in enumerate(GATHER_ROWS):
        pl.when(fits_ref[b, g] == level)(functools.partial(windowed, n_rows))

    @pl.when(fits_ref[b, g] == len(GATHER_ROWS))
    def _():
        slot_id = _iota2((cap, 1), 0).astype(F32)
        for e in range(n_g):
            xe = jnp.zeros((cap, D_MODEL), F32)
            gate = jnp.zeros((cap, 1), F32)
            for j in range(n_tiles):
                hit = slot_ref[0, e, j:j + 1, :] == slot_id
                xe = xe + _dot(jnp.where(hit, 1.0, 0.0).astype(BF16),
                               h2_ref[0, j * GATHER_TILE:(j + 1) * GATHER_TILE, :])
                gate = gate + jnp.sum(jnp.where(hit, w_ref[0, e, j:j + 1, :], 0.0), axis=1,
                                      keepdims=True)
            xe_ref[0, e] = xe.astype(BF16)
            gate_ref[0, e] = _cols_to_rows(gate)


def _gather(h2, slot, w, first, capacity):
    b, n, _ = h2.shape
    n_e = slot.shape[1]
    n_tiles = n // GATHER_TILE
    assert n_e % GATHER_GROUP == 0 and GATHER_TILE % LANES == 0 and capacity % LANES == 0
    assert all(r % BF16_ROWS == 0 and r >= 2 * BF16_ROWS for r in GATHER_ROWS)
    assert list(GATHER_ROWS) == sorted(GATHER_ROWS)
    before = first[:, :, 0:n // LANES:GATHER_TILE // LANES].astype(jnp.int32)
    after = jnp.concatenate([before[:, :, 1:], jnp.full((b, n_e, 1), capacity, jnp.int32)], axis=2)
    extent = jnp.max(after - (before // BF16_ROWS) * BF16_ROWS, axis=2)
    extent = jnp.max(extent.reshape(b, n_e // GATHER_GROUP, GATHER_GROUP), axis=2)
    fits = sum((extent >= n_rows).astype(jnp.int32) for n_rows in GATHER_ROWS)
    tiles = lambda a: a.reshape(b, n_e, n_tiles, GATHER_TILE)
    rows = pl.BlockSpec((1, GATHER_GROUP, n_tiles, GATHER_TILE), lambda i, g, *_: (i, g, 0, 0))
    return pl.pallas_call(
        _gather_kernel,
        grid_spec=pltpu.PrefetchScalarGridSpec(
            num_scalar_prefetch=2,
            grid=(b, n_e // GATHER_GROUP),
            in_specs=[pl.BlockSpec((1, n, D_MODEL), lambda i, g, *_: (i, 0, 0)), rows, rows],
            out_specs=[pl.BlockSpec((1, GATHER_GROUP, capacity, D_MODEL), lambda i, g, *_: (i, g, 0, 0)),
                       pl.BlockSpec((1, GATHER_GROUP, capacity // LANES, LANES),
                                    lambda i, g, *_: (i, g, 0, 0))],
            scratch_shapes=[pltpu.VMEM((GATHER_GROUP, capacity + max(GATHER_ROWS), D_MODEL), BF16),
                            pltpu.VMEM((GATHER_GROUP, capacity + max(GATHER_ROWS), 1), F32)]),
        out_shape=[jax.ShapeDtypeStruct((b, n_e, capacity, D_MODEL), BF16),
                   jax.ShapeDtypeStruct((b, n_e, capacity // LANES, LANES), F32)],
        compiler_params=_params("arbitrary", "arbitrary"),
        name="gather",
    )(before, fits, h2, tiles(slot), tiles(w))


def _ffn_kernel(xe_ref, gate_ref, wg_ref, wu_ref, wd_ref, ye_ref, acc_s):
    ft = pl.program_id(1)
    last = pl.num_programs(1) - 1

    def step(first, final):
        wg = wg_ref[0].astype(BF16)
        wu = wu_ref[0].astype(BF16)
        wd = wd_ref[0].astype(BF16)
        for i in range(xe_ref.shape[0]):
            xe = xe_ref[i, 0]
            act = jax.nn.silu(_dot(xe, wg)) * _dot(xe, wu)
            part = _dot(act.astype(BF16), wd)
            total = part if first else acc_s[i] + part
            if final:
                ye_ref[i] = (total * _rows_to_cols(gate_ref[i, 0])).astype(BF16)
            else:
                acc_s[i] = total

    pl.when(ft == 0)(lambda: step(True, False))
    pl.when(jnp.logical_and(ft > 0, ft < last))(lambda: step(False, False))
    pl.when(ft == last)(lambda: step(False, True))


def _ffn(xe, gate, wg, wu, wd, f_tile):
    b, n_e, cap, _ = xe.shape
    assert D_EXPERT // f_tile >= 2
    return pl.pallas_call(
        _ffn_kernel,
        grid=(n_e, D_EXPERT // f_tile),
        in_specs=[pl.BlockSpec((b, 1, cap, D_MODEL), lambda e, f: (0, e, 0, 0)),
                  pl.BlockSpec((b, 1, cap // LANES, LANES), lambda e, f: (0, e, 0, 0)),
                  pl.BlockSpec((1, D_MODEL, f_tile), lambda e, f: (e, 0, f)),
                  pl.BlockSpec((1, D_MODEL, f_tile), lambda e, f: (e, 0, f)),
                  pl.BlockSpec((1, f_tile, D_MODEL), lambda e, f: (e, f, 0))],
        out_specs=pl.BlockSpec((b, cap, D_MODEL), lambda e, f: (0, e, 0)),
        out_shape=jax.ShapeDtypeStruct((b, n_e * cap, D_MODEL), BF16),
        scratch_shapes=[pltpu.VMEM((b, cap, D_MODEL), F32)],
        compiler_params=_params("arbitrary", "arbitrary"),
        name="ffn",
    )(xe, gate, wg, wu, wd)


COMBINE_WIDE = CHUNK + BF16_ROWS
COMBINE_NARROW = 48


def _combine_kernel(first_ref, narrow_ref, slot_t_ref, first_t_ref, ye_ref, x1_ref, mod_ref, fg_ref,
                    o_ref, acc_s, *, capacity):
    b = pl.program_id(0)
    j = pl.program_id(1)
    n_sub = o_ref.shape[1] // CHUNK
    lane = _iota2((1, LANES), 1).astype(F32)

    def scatter(window):
        k_total = N_EXPERTS * window
        for sb in range(n_sub):
            blk = j * n_sub + sb
            slot_t = slot_t_ref[0, sb * CHUNK:(sb + 1) * CHUNK, :]
            start_row = jnp.minimum(
                jnp.floor(first_t_ref[0, pl.ds(blk, 1), :] * (1.0 / BF16_ROWS)) * float(BF16_ROWS),
                float(capacity - window))
            k_pos = jnp.where(slot_t >= 0.0, slot_t - start_row + lane * float(window), -1.0)
            cols = []
            for c in range(k_total // LANES):
                k_lane = lane + float(LANES * c)
                hit = jnp.zeros((CHUNK, LANES), F32)
                for e in range((LANES * c) // window, (LANES * c + LANES - 1) // window + 1):
                    hit = jnp.where(k_pos[:, e:e + 1] == k_lane, 1.0, hit)
                cols.append(hit.astype(BF16))
            onehot = jnp.concatenate(cols, axis=1)
            rows = []
            for e in range(N_EXPERTS):
                start = jnp.minimum((first_ref[b, blk, e] // BF16_ROWS) * BF16_ROWS, capacity - window)
                rows.append(ye_ref[0, pl.ds(pl.multiple_of(e * capacity + start, BF16_ROWS), window), :])
            acc_s[sb * CHUNK:(sb + 1) * CHUNK, :] = _dot(onehot, jnp.concatenate(rows, axis=0))

    pl.when(narrow_ref[b, j] != 0)(lambda: scatter(COMBINE_NARROW))
    pl.when(narrow_ref[b, j] == 0)(lambda: scatter(COMBINE_WIDE))

    g2 = mod_ref[pl.ds(b, 1), 5 * D_MODEL:6 * D_MODEL]
    x2 = x1_ref[0] + g2 * acc_s[...]
    o_ref[0] = x2 * lax.rsqrt(jnp.mean(x2 * x2, axis=-1, keepdims=True) + EPS) * fg_ref[...]


def _combine(slot_t, first_t, ye, x1, mod, fg, capacity, tm):
    b, n, _ = x1.shape
    n_blocks = n // CHUNK
    assert CHUNK == LANES and N_EXPERTS <= LANES and tm % CHUNK == 0
    for window in (COMBINE_WIDE, COMBINE_NARROW):
        assert (N_EXPERTS * window) % LANES == 0 and window % BF16_ROWS == 0 and window <= capacity
    assert COMBINE_WIDE >= CHUNK + BF16_ROWS - 1 and capacity % BF16_ROWS == 0
    first = first_t[:, :, :N_EXPERTS].astype(jnp.int32)
    after = jnp.concatenate([first[:, 1:], jnp.full((b, 1, N_EXPERTS), capacity, jnp.int32)], axis=1)
    narrow = jnp.all(after - (first // BF16_ROWS) * BF16_ROWS <= COMBINE_NARROW, axis=2)
    narrow = jnp.all(narrow.reshape(b, n // tm, tm // CHUNK), axis=2).astype(jnp.int32)
    tok = pl.BlockSpec((1, tm, D_MODEL), lambda i, j, *_: (i, j, 0))
    return pl.pallas_call(
        functools.partial(_combine_kernel, capacity=capacity),
        grid_spec=pltpu.PrefetchScalarGridSpec(
            num_scalar_prefetch=2,
            grid=(b, n // tm),
            in_specs=[pl.BlockSpec((1, tm, LANES), lambda i, j, *_: (i, j, 0)),
                      pl.BlockSpec((1, n_blocks, LANES), lambda i, j, *_: (i, 0, 0)),
                      pl.BlockSpec((1, N_EXPERTS * capacity, D_MODEL), lambda i, j, *_: (i, 0, 0)),
                      tok,
                      pl.BlockSpec(mod.shape, lambda i, j, *_: (0, 0)),
                      pl.BlockSpec((1, D_MODEL), lambda i, j, *_: (0, 0))],
            out_specs=tok,
            scratch_shapes=[pltpu.VMEM((tm, D_MODEL), F32)]),
        out_shape=jax.ShapeDtypeStruct((b, n, D_MODEL), F32),
        compiler_params=_params("arbitrary", "arbitrary"),
        name="combine",
    )(first, narrow, slot_t, first_t, ye, x1, mod, fg)


def kernel(x, c, ctx, c_ctx, w_mod, b_mod, norm_mix_g, w_in, conv_q, conv_k, b_igate, b_fgate,
           gmlp_ws, gmlp_bs, mlstm_norm_g, w_out, norm_ffn_g, w_router, w_gate_e, w_up_e,
           w_down_e, final_g):
    depth = w_mod.shape[0]
    assert depth == 1, "the context stream is only carried as mLSTM states (single layer)"
    batch, seq, _ = x.shape
    assert seq % GRID_W == 0 and seq % CHUNK == 0 and batch + 1 <= MOD_ROWS
    capacity = EC_FACTOR * seq // N_EXPERTS
    ctx_row = batch
    l = 0

    cond = jnp.concatenate([c, c_ctx[None], jnp.zeros((MOD_ROWS - batch - 1, D_MODEL), F32)], axis=0)
    mod = _adaln(cond, w_mod[l], b_mod[l][None])

    row = lambda a: a[None]
    w_gates = jnp.pad(w_in[l][:, MAIN_W:], ((0, 0), (0, GATE_PAD - N_GATES)))
    cols = lambda *blks: jnp.concatenate(
        [w_in[l][:, k * GROUP_W:(k + 1) * GROUP_W] for k in blks] + [w_gates], axis=1).astype(BF16)
    gate_bias = jnp.pad(jnp.concatenate([b_igate[l].reshape(-1), b_fgate[l].reshape(-1)]),
                        (0, GATE_PAD - N_GATES))[None]
    k_scale = HEAD_DIM ** -0.5

    identity = lambda a: a
    k_c, v_c, gates_c = _inproj(ctx, mod, row(norm_mix_g[l]), cols(K_BLK, V_BLK), conv_k[l][None],
                                (k_scale,), (identity,), tm=ctx.shape[1], ctx_row=ctx_row)
    state = _mlstm(None, k_c, v_c, 0, gates_c, gate_bias, None)

    q_l, k_l, p, gates = _inproj(x, mod, row(norm_mix_g[l]), w_in[l],
                                 jnp.stack([conv_q[l], conv_k[l]]), (1.0, k_scale),
                                 (jax.nn.gelu, lambda a: _layer_norm(jax.nn.gelu(a)), jax.nn.sigmoid, identity),
                                 tm=512, ctx_row=None,
                                 raw_blocks=(Q_BLK, K_BLK, U_BLK, VG_BLK, O_BLK, V_BLK))
    h_f, h_b = _mlstm(q_l, k_l, p, P_V, gates, gate_bias, state)

    x1, h2, aff_t = _postmix(p, h_f, h_b, x, mod, gmlp_ws[l].astype(BF16), gmlp_bs[l][:, :, None],
                             row(mlstm_norm_g[l]), w_out[l].astype(BF16), row(norm_ffn_g[l]),
                             jnp.pad(w_router[l], ((0, 0), (0, LANES - N_EXPERTS))), tm=1024)

    slot, gate_w, slot_t, first_t, first = _route(aff_t, capacity)

    xe, gate = _gather(h2, slot, gate_w, first, capacity)
    ye = _ffn(xe, gate, w_gate_e[l], w_up_e[l], w_down_e[l], f_tile=512)
    return _combine(slot_t, first_t, ye, x1, mod, row(final_g), capacity, tm=512)
```

```python
import functools

import jax
import jax.numpy as jnp
from jax import lax
from jax.experimental import pallas as pl
from jax.experimental.pallas import tpu as pltpu

F32 = jnp.float32
BF16 = jnp.bfloat16

LANES = 128
BF16_ROWS = 16
VMEM_LIMIT = 56 * 1024 * 1024

D_MODEL = 1024
GRID_W = 64
CHUNK = 128
HEADS = 4
GROUP_W = D_MODEL // 2
HEAD_DIM = GROUP_W // HEADS
N_EXPERTS = 16
EC_FACTOR = 2
D_EXPERT = 2 * D_MODEL
EPS = 1e-6
N_GATES = 4 * HEADS
GATE_PAD = LANES
MOD_ROWS = 8

U_BLK, VG_BLK, Q_BLK, O_BLK, K_BLK, V_BLK = 0, 1, 2, 3, 4, 5
MAIN_W = 6 * GROUP_W
P_U, P_VG, P_O, P_V = 0, 1, 2, 3

ROW_GROUP = 2 * CHUNK
MLSTM_SAMPLES = 4


def _params(*sem):
    return pltpu.CompilerParams(dimension_semantics=sem, vmem_limit_bytes=VMEM_LIMIT)


def _dot(a, b):
    return jnp.dot(a, b, preferred_element_type=F32)


def _dot_nt(a, b):
    return lax.dot_general(a, b, (((1,), (1,)), ((), ())), preferred_element_type=F32)


def _split2(a):
    hi = a.astype(BF16)
    lo = (a - hi.astype(F32)).astype(BF16)
    return hi, lo


def _dot3(a, b):
    ah, al = _split2(a)
    bh, bl = _split2(b)
    return _dot(ah, bh) + (_dot(al, bh) + _dot(ah, bl))


def _dot_exact01(tri, x):
    x1 = x.astype(BF16)
    r1 = x - x1.astype(F32)
    x2 = r1.astype(BF16)
    x3 = (r1 - x2.astype(F32)).astype(BF16)
    return _dot(tri, x1) + (_dot(tri, x2) + _dot(tri, x3))


def _iota2(shape, dim):
    return lax.broadcasted_iota(jnp.int32, shape, dim)


def _eye():
    return _iota2((LANES, LANES), 0) == _iota2((LANES, LANES), 1)


def _cols_to_rows(col):
    return jnp.concatenate(
        [jnp.sum(jnp.where(_eye(), col[r * LANES:(r + 1) * LANES, :], 0.0), axis=0, keepdims=True)
         for r in range(col.shape[0] // LANES)], axis=0)


def _rows_to_cols(rows):
    return jnp.concatenate(
        [jnp.sum(jnp.where(_eye(), rows[r:r + 1, :], 0.0), axis=1, keepdims=True)
         for r in range(rows.shape[0])], axis=0)


def _adaln_kernel(cond_ref, w_ref, b_ref, o_ref):
    o_ref[...] = _dot3(jax.nn.silu(cond_ref[...]), w_ref[...]) + b_ref[...]


def _adaln(cond, w, b):
    n_out = w.shape[1]
    tn = D_MODEL
    assert n_out % tn == 0
    return pl.pallas_call(
        _adaln_kernel,
        grid=(n_out // tn,),
        in_specs=[pl.BlockSpec((MOD_ROWS, D_MODEL), lambda j: (0, 0)),
                  pl.BlockSpec((D_MODEL, tn), lambda j: (0, j)),
                  pl.BlockSpec((1, tn), lambda j: (0, j))],
        out_specs=pl.BlockSpec((MOD_ROWS, tn), lambda j: (0, j)),
        out_shape=jax.ShapeDtypeStruct((MOD_ROWS, n_out), F32),
        compiler_params=_params("arbitrary"),
        name="adaln",
    )(cond, w, b)


def _inproj_kernel(x_ref, xprev_ref, xnext_ref, mod_ref, g_ref, taps_ref, w_ref, *outs, ctx_row, scales,
                   plain_acts, raw_blocks):
    conv_refs, (p_ref, gate_ref), w_bf = outs[:len(scales)], outs[len(scales):len(scales) + 2], outs[-1]
    row = pl.program_id(0) if ctx_row is None else ctx_row
    j = pl.program_id(1)

    @pl.when(jnp.logical_and(pl.program_id(0) == 0, j == 0))
    def _():
        for k, blk in enumerate(raw_blocks):
            w_bf[:, k * GROUP_W:(k + 1) * GROUP_W] = w_ref[:, blk * GROUP_W:(blk + 1) * GROUP_W].astype(BF16)
        g0 = len(raw_blocks) * GROUP_W
        n_gate = w_ref.shape[1] - MAIN_W
        w_bf[:, g0:g0 + GATE_PAD] = jnp.zeros((D_MODEL, GATE_PAD), BF16)
        w_bf[:, g0:g0 + n_gate] = w_ref[:, MAIN_W:MAIN_W + n_gate].astype(BF16)
    sh = mod_ref[pl.ds(row, 1), 0:D_MODEL]
    sc = mod_ref[pl.ds(row, 1), D_MODEL:2 * D_MODEL]

    def modulated(x):
        y = x * lax.rsqrt(jnp.mean(x * x, axis=-1, keepdims=True) + EPS) * g_ref[...]
        return y * (1.0 + sc) + sh

    n_groups = x_ref.shape[1] // ROW_GROUP
    before = modulated(xprev_ref[0]) * (j > 0).astype(F32)
    for r in range(n_groups):
        rows = slice(r * ROW_GROUP, (r + 1) * ROW_GROUP)
        h = modulated(x_ref[0, rows, :])
        if r == n_groups - 1:
            after = modulated(xnext_ref[0]) * (j < pl.num_programs(1) - 1).astype(F32)
        else:
            after = modulated(x_ref[0, (r + 1) * ROW_GROUP:(r + 1) * ROW_GROUP + 8, :])
        ext = jnp.concatenate([before, h, after], axis=0).astype(BF16)
        before = h[ROW_GROUP - 8:, :]
        y = _dot(ext, w_bf[...])
        n_ext, n_conv = ext.shape[0], len(scales) * GROUP_W
        y_here = y[8:8 + ROW_GROUP]
        for k, act in enumerate(plain_acts):
            lanes = slice(n_conv + k * GROUP_W, n_conv + (k + 1) * GROUP_W)
            p_ref[0, rows, k * GROUP_W:(k + 1) * GROUP_W] = act(y_here[:, lanes])
        gate_ref[0, rows, :] = y_here[:, n_conv + p_ref.shape[2]:]
        y_conv = y[:, 0:n_conv]
        y_before = pltpu.roll(y_conv, 1, axis=0)[8:8 + ROW_GROUP]
        y_after = pltpu.roll(y_conv, n_ext - 1, axis=0)[8:8 + ROW_GROUP]
        for s, scale in enumerate(scales):
            lanes = slice(s * GROUP_W, (s + 1) * GROUP_W)
            z = (y_before[:, lanes] * taps_ref[s, 0:1, :] + y_here[:, lanes] * taps_ref[s, 1:2, :]
                 + y_after[:, lanes] * taps_ref[s, 2:3, :])
            conv_refs[s][0, rows, :] = (jax.nn.silu(z) * scale).astype(BF16)


def _inproj(x, mod, g, w, raw_blocks, taps, scales, plain_acts, tm, ctx_row):
    b, n, _ = x.shape
    wn = len(plain_acts) * GROUP_W
    w_cols = len(scales) * GROUP_W + wn + GATE_PAD
    assert len(raw_blocks) == len(scales) + len(plain_acts) and 0 < w.shape[1] - MAIN_W <= GATE_PAD
    assert tm % ROW_GROUP == 0 and n % tm == 0 and wn % LANES == 0
    rows8 = tm // 8
    last8 = n // 8 - 1
    full = lambda a: pl.BlockSpec(a.shape, lambda i, j: (0,) * a.ndim)
    conv_spec = pl.BlockSpec((1, tm, GROUP_W), lambda i, j: (i, j, 0))
    return pl.pallas_call(
        functools.partial(_inproj_kernel, ctx_row=ctx_row, scales=tuple(scales),
                          plain_acts=tuple(plain_acts), raw_blocks=tuple(raw_blocks)),
        grid=(b, n // tm),
        in_specs=[pl.BlockSpec((1, tm, D_MODEL), lambda i, j: (i, j, 0)),
                  pl.BlockSpec((1, 8, D_MODEL), lambda i, j: (i, jnp.maximum(j * rows8 - 1, 0), 0)),
                  pl.BlockSpec((1, 8, D_MODEL), lambda i, j: (i, jnp.minimum((j + 1) * rows8, last8), 0)),
                  full(mod), full(g), full(taps), full(w)],
        out_specs=[conv_spec] * len(scales)
        + [pl.BlockSpec((1, tm, wn), lambda i, j: (i, j, 0)),
           pl.BlockSpec((1, tm, GATE_PAD), lambda i, j: (i, j, 0))],
        out_shape=[jax.ShapeDtypeStruct((b, n, GROUP_W), BF16)] * len(scales)
        + [jax.ShapeDtypeStruct((b, n, wn), F32), jax.ShapeDtypeStruct((b, n, GATE_PAD), F32)],
        scratch_shapes=[pltpu.VMEM((D_MODEL, w_cols), BF16)],
        compiler_params=_params("arbitrary", "arbitrary"),
        name="inproj",
    )(x, x, x, mod, g, taps, w)


def _mlstm_chunk(per_dir, bias_ref, c_s, n_s, m_s, h_refs):
    with_h = h_refs[0] is not None
    i0 = _iota2((CHUNK, CHUNK), 0)
    i1 = _iota2((CHUNK, CHUNK), 1)

    units = []
    for smp, d in [(smp, d) for smp in range(c_s.shape[0]) for d in range(2)]:
        q_ref, k_ref, v_ref, g_ref = per_dir[d]
        sees_ts = (i1 <= i0) if d == 0 else (i1 >= i0)
        sees_st = (i0 <= i1) if d == 0 else (i0 >= i1)
        tri = jnp.where(sees_ts, 1.0, 0.0).astype(BF16)
        gates = g_ref[smp] + bias_ref[...]
        bcum = _dot_exact01(tri, jax.nn.log_sigmoid(gates))
        for hd in range(HEADS):
            lanes = slice(hd * HEAD_DIM, (hd + 1) * HEAD_DIM)
            c_prev = c_s[smp, d, hd]
            n_prev = n_s[smp, d, hd]
            k = k_ref[smp, :, lanes]
            st = dict(smp=smp, d=d, hd=hd, lanes=lanes, c_prev=c_prev, n_prev=n_prev, k=k,
                      sees_st=sees_st, gates=gates, bcum=bcum, v=v_ref[smp, :, lanes])
            if with_h:
                lhs = jnp.concatenate([k, c_prev.astype(BF16),
                                       jnp.broadcast_to(n_prev, (BF16_ROWS, HEAD_DIM)).astype(BF16)], axis=0)
                st["prod"] = _dot_nt(lhs, q_ref[smp, :, lanes])
            units.append(st)

    rows = {}
    for st in units:
        smp, d, hd = st["smp"], st["d"], st["hd"]
        if (smp, d) not in rows:
            rows[smp, d] = (st["gates"].T, st["bcum"].T)
        gates_t, bcum_t = rows[smp, d]
        ci = d * HEADS + hd
        cf = 2 * HEADS + d * HEADS + hd
        last = CHUNK - 1 if d == 0 else 0
        li_row = gates_t[ci:ci + 1, :]
        bc_row = bcum_t[cf:cf + 1, :]
        b_last = bc_row[:, last:last + 1]
        m_prev = m_s[smp, d, hd]
        v_t = st["v"].T
        a_row = b_last - bc_row + li_row
        m_new = jnp.maximum(b_last + m_prev, jnp.max(a_row, axis=-1, keepdims=True))
        w_row = jnp.exp(a_row - m_new)
        lhs = jnp.concatenate([v_t * w_row, jnp.broadcast_to(w_row, (BF16_ROWS, CHUNK))], axis=0)
        st["upd"] = _dot(lhs.astype(BF16), st["k"])
        st.update(bc_row=bc_row, m_prev=m_prev, m_new=m_new, v_t=v_t,
                  decay=jnp.exp(b_last + m_prev - m_new),
                  u_col=st["gates"][:, ci:ci + 1] - st["bcum"][:, cf:cf + 1])

    if with_h:
        for st in units:
            prod, bc_row = st["prod"], st["bc_row"]
            g = bc_row + st["m_prev"]
            dmat = jnp.where(st["sees_st"], st["u_col"] + bc_row, -jnp.inf)
            m_t = jnp.maximum(g, jnp.max(dmat, axis=0, keepdims=True))
            inter = jnp.exp(g - m_t)
            s = prod[0:CHUNK] * jnp.exp(dmat - m_t)
            st["pv"] = _dot(st["v_t"].astype(BF16), s.astype(BF16))
            st["num0"] = inter * prod[CHUNK:2 * CHUNK]
            den = inter * prod[2 * CHUNK:2 * CHUNK + 1] + jnp.sum(s, axis=0, keepdims=True)
            st["scale"] = 1.0 / jnp.maximum(jnp.abs(den), jnp.exp(-m_t))

    for st in units:
        smp, d, hd = st["smp"], st["d"], st["hd"]
        if with_h:
            h_refs[d][smp, :, st["lanes"]] = ((st["num0"] + st["pv"]) * st["scale"]).T
        c_s[smp, d, hd] = st["decay"] * st["c_prev"] + st["upd"][0:HEAD_DIM]
        n_s[smp, d, hd] = st["decay"] * st["n_prev"] + st["upd"][HEAD_DIM:HEAD_DIM + 1]
        m_s[smp, d, hd] = st["m_new"]


def _mlstm_kernel(*refs, n_chunks, with_h):
    refs = list(refs)
    take = lambda n: [refs.pop(0) for _ in range(n)]
    per_dir = []
    for _ in range(2):
        q_ref = take(1)[0] if with_h else None
        k_ref, v_ref, g_ref = take(3)
        per_dir.append((q_ref, k_ref, v_ref, g_ref))
    bias_ref = take(1)[0]
    if with_h:
        c0_ref, n0_ref, m0_ref = take(3)
        h_refs = take(2)
        c_out = n_out = m_out = None
    else:
        h_refs = [None, None]
        c_out, n_out, m_out = take(3)
    c_s, n_s, m_s = take(3)

    j = pl.program_id(1)

    @pl.when(j == 0)
    def _():
        if with_h:
            c_s[...] = c0_ref[...]
            n_s[...] = n0_ref[...]
            m_s[...] = m0_ref[...]
        else:
            c_s[...] = jnp.zeros_like(c_s)
            n_s[...] = jnp.zeros_like(n_s)
            m_s[...] = jnp.zeros_like(m_s)

    _mlstm_chunk(per_dir, bias_ref, c_s, n_s, m_s, h_refs)

    if not with_h:
        @pl.when(j == n_chunks - 1)
        def _():
            c_out[...] = c_s[...]
            n_out[...] = n_s[...]
            m_out[...] = m_s[...]


def _mlstm(q, k, p, v_blk, gates, bias, state):
    b, n, _ = k.shape
    n_chunks = n // CHUNK
    with_h = q is not None
    ns = MLSTM_SAMPLES if b % MLSTM_SAMPLES == 0 else 1

    in_specs, args = [], []
    for d in range(2):
        c = (lambda j: j) if d == 0 else (lambda j: n_chunks - 1 - j)
        tok = pl.BlockSpec((ns, CHUNK, GROUP_W), lambda i, j, c=c: (i, c(j), 0))
        if with_h:
            in_specs.append(tok)
            args.append(q)
        in_specs += [tok,
                     pl.BlockSpec((ns, CHUNK, GROUP_W), lambda i, j, c=c: (i, c(j), v_blk)),
                     pl.BlockSpec((ns, CHUNK, GATE_PAD), lambda i, j, c=c: (i, c(j), 0))]
        args += [k, p, gates]
    in_specs.append(pl.BlockSpec((1, GATE_PAD), lambda i, j: (0, 0)))
    args.append(bias)

    c_shape = (2, HEADS, HEAD_DIM, HEAD_DIM)
    v_shape = (2, HEADS, 1, HEAD_DIM)
    c_spec = pl.BlockSpec((ns,) + c_shape, lambda i, j: (i, 0, 0, 0, 0))
    v_spec = pl.BlockSpec((ns,) + v_shape, lambda i, j: (i, 0, 0, 0, 0))
    if with_h:
        in_specs += [c_spec, v_spec, v_spec]
        args += list(state)
        out_specs = [pl.BlockSpec((ns, CHUNK, GROUP_W), lambda i, j: (i, j, 0)),
                     pl.BlockSpec((ns, CHUNK, GROUP_W), lambda i, j: (i, n_chunks - 1 - j, 0))]
        out_shape = [jax.ShapeDtypeStruct((b, n, GROUP_W), F32)] * 2
    else:
        out_specs = [c_spec, v_spec, v_spec]
        out_shape = [jax.ShapeDtypeStruct((b,) + c_shape, F32),
                     jax.ShapeDtypeStruct((b,) + v_shape, F32),
                     jax.ShapeDtypeStruct((b,) + v_shape, F32)]
    return pl.pallas_call(
        functools.partial(_mlstm_kernel, n_chunks=n_chunks, with_h=with_h),
        grid=(b // ns, n_chunks),
        in_specs=in_specs,
        out_specs=out_specs,
        out_shape=out_shape,
        scratch_shapes=[pltpu.VMEM((ns,) + c_shape, F32), pltpu.VMEM((ns,) + v_shape, F32),
                        pltpu.VMEM((ns,) + v_shape, F32)],
        compiler_params=_params("arbitrary", "arbitrary"),
        name="mlstm" if with_h else "mlstm_ctx_state",
    )(*args)


def _layer_norm(x):
    mu = jnp.mean(x, axis=-1, keepdims=True)
    var = jnp.mean(jnp.square(x - mu), axis=-1, keepdims=True)
    return (x - mu) * lax.rsqrt(var + EPS)


def _postmix_kernel(u_ref, vg_ref, o_ref, hf_ref, hb_ref, x_ref, mod_ref, ws_ref, bs_ref, ng_ref,
                    wout_ref, fg_ref, wr_ref, x1_ref, h2_ref, aff_ref, ycat_s):
    tm = x_ref.shape[1]
    b = pl.program_id(0)
    mod = lambda k: mod_ref[pl.ds(b, 1), k * D_MODEL:(k + 1) * D_MODEL]
    wr_hi, wr_lo = _split2(wr_ref[...])
    expert_lane = _iota2((ROW_GROUP, LANES), 1) < N_EXPERTS

    for r in range(tm // ROW_GROUP):
        rows = slice(r * ROW_GROUP, (r + 1) * ROW_GROUP)

        for c in range(r * ROW_GROUP // CHUNK, (r + 1) * ROW_GROUP // CHUNK):
            crows = slice(c * CHUNK, (c + 1) * CHUNK)
            u = u_ref[0, crows, :]
            v = vg_ref[0, crows, :].astype(BF16)
            for hd in range(HEADS):
                lanes = slice(hd * HEAD_DIM, (hd + 1) * HEAD_DIM)
                s = _dot(ws_ref[hd], v[:, lanes]) + bs_ref[hd]
                ycat_s[crows, lanes] = (u[:, lanes] * s).astype(BF16)

        hsum = hf_ref[0, rows, :] + hb_ref[0, rows, :]
        o = o_ref[0, rows, :]
        for hd in range(HEADS):
            lanes = slice(hd * HEAD_DIM, (hd + 1) * HEAD_DIM)
            hn = _layer_norm(hsum[:, lanes]) * ng_ref[:, lanes]
            ycat_s[rows, GROUP_W + hd * HEAD_DIM:GROUP_W + (hd + 1) * HEAD_DIM] = (
                o[:, lanes] * hn).astype(BF16)

        y = _dot(ycat_s[rows, :], wout_ref[...])
        x1 = x_ref[0, rows, :] + mod(2) * y
        x1_ref[0, rows, :] = x1

        n2 = x1 * lax.rsqrt(jnp.mean(x1 * x1, axis=-1, keepdims=True) + EPS) * fg_ref[...]
        h2 = n2 * (1.0 + mod(4)) + mod(3)
        h2_hi, h2_lo = _split2(h2)
        h2_ref[0, rows, :] = h2_hi

        logits = _dot(h2_hi, wr_hi) + (_dot(h2_lo, wr_hi) + _dot(h2_hi, wr_lo))
        logits = jnp.where(expert_lane, logits, -jnp.inf)
        e = jnp.exp(logits - jnp.max(logits, axis=-1, keepdims=True))
        aff = e / jnp.sum(e, axis=-1, keepdims=True)
        aff_ref[0, :, rows] = aff.T[0:N_EXPERTS, :]


def _postmix(p, hf, hb, x, mod, ws, bs, ng, wout, fg, wr, tm):
    b, n, _ = x.shape
    tok = lambda blk: pl.BlockSpec((1, tm, GROUP_W), lambda i, j: (i, j, blk))
    full = lambda a: pl.BlockSpec(a.shape, lambda i, j: (0,) * a.ndim)
    return pl.pallas_call(
        _postmix_kernel,
        grid=(b, n // tm),
        in_specs=[tok(P_U), tok(P_VG), tok(P_O), tok(0), tok(0),
                  pl.BlockSpec((1, tm, D_MODEL), lambda i, j: (i, j, 0)),
                  full(mod), full(ws), full(bs), full(ng), full(wout), full(fg), full(wr)],
        out_specs=[pl.BlockSpec((1, tm, D_MODEL), lambda i, j: (i, j, 0)),
                   pl.BlockSpec((1, tm, D_MODEL), lambda i, j: (i, j, 0)),
                   pl.BlockSpec((1, N_EXPERTS, tm), lambda i, j: (i, 0, j))],
        out_shape=[jax.ShapeDtypeStruct((b, n, D_MODEL), F32),
                   jax.ShapeDtypeStruct((b, n, D_MODEL), BF16),
                   jax.ShapeDtypeStruct((b, N_EXPERTS, n), F32)],
        scratch_shapes=[pltpu.VMEM((tm, D_MODEL), BF16)],
        compiler_params=_params("arbitrary", "arbitrary"),
        name="postmix",
    )(p, p, p, hf, hb, x, mod, ws, bs, ng, wout, fg, wr)


def _cumsum_lanes(x, upper):
    carry = jnp.zeros((x.shape[0], 1), F32)
    outs, before = [], []
    for j in range(x.shape[1] // LANES):
        before.append(carry)
        c = _dot(x[:, j * LANES:(j + 1) * LANES].astype(BF16), upper) + carry
        outs.append(c)
        carry = c[:, LANES - 1:LANES]
    return jnp.concatenate(outs, axis=1), before


def _rows_to_lanes(x, fill):
    pad = jnp.full((LANES - x.shape[0], LANES), fill, F32)
    return jnp.concatenate([x, pad], axis=0).T


def _route_kernel(aff_ref, slot_ref, w_ref, slot_t_ref, first_t_ref, first_ref, *, capacity, n_e):
    aff = aff_ref[...]
    cap = float(capacity)
    thr_bits = jnp.zeros((aff.shape[0], 1), jnp.int32)
    for bit in range(30, -1, -1):
        cand = thr_bits | (1 << bit)
        cnt = jnp.sum(jnp.where(aff >= pltpu.bitcast(cand, F32), 1.0, 0.0), axis=-1, keepdims=True)
        thr_bits = jnp.where(cnt >= cap, cand, thr_bits)
    thr = pltpu.bitcast(thr_bits, F32)
    upper = jnp.where(_iota2((LANES, LANES), 0) <= _iota2((LANES, LANES), 1), 1.0, 0.0).astype(BF16)
    above = jnp.where(aff > thr, 1.0, 0.0)
    tied = jnp.where(aff == thr, 1.0, 0.0)
    need = cap - jnp.sum(above, axis=-1, keepdims=True)
    sel = above + tied * jnp.where(_cumsum_lanes(tied, upper)[0] <= need, 1.0, 0.0)
    count, before = _cumsum_lanes(sel, upper)
    slot = jnp.where(sel > 0.0, count - 1.0, -1.0)
    slot_ref[...] = slot
    w_ref[...] = jnp.where(sel > 0.0, aff, 0.0)

    n_blocks = len(before)
    lane = _iota2((1, LANES), 1)
    first = jnp.zeros((aff.shape[0], LANES), F32)
    for j in range(n_blocks):
        first = jnp.where(lane == j, before[j], first)
    first_ref[...] = first
    for smp in range(aff.shape[0] // n_e):
        rows = slice(smp * n_e, (smp + 1) * n_e)
        for j in range(n_blocks):
            slot_t_ref[smp, j * LANES:(j + 1) * LANES, :] = _rows_to_lanes(
                slot[rows, j * LANES:(j + 1) * LANES], -1.0)
        first_t_ref[smp] = _rows_to_lanes(first[rows, :], 0.0)[0:n_blocks, :]


def _route(aff_t, capacity):
    b, e, n = aff_t.shape
    assert n // LANES <= LANES and e <= LANES and e % 8 == 0
    rows = pl.BlockSpec((b * e, n), lambda i: (0, 0))
    slot, w, slot_t, first_t, first = pl.pallas_call(
        functools.partial(_route_kernel, capacity=capacity, n_e=e),
        grid=(1,),
        in_specs=[rows],
        out_specs=[rows, rows,
                   pl.BlockSpec((b, n, LANES), lambda i: (0, 0, 0)),
                   pl.BlockSpec((b, n // LANES, LANES), lambda i: (0, 0, 0)),
                   pl.BlockSpec((b * e, LANES), lambda i: (0, 0))],
        out_shape=[jax.ShapeDtypeStruct((b * e, n), F32)] * 2
        + [jax.ShapeDtypeStruct((b, n, LANES), F32), jax.ShapeDtypeStruct((b, n // LANES, LANES), F32),
           jax.ShapeDtypeStruct((b * e, LANES), F32)],
        compiler_params=_params("arbitrary"),
        name="route",
    )(aff_t.reshape(b * e, n))
    return slot.reshape(b, e, n), w.reshape(b, e, n), slot_t, first_t, first.reshape(b, e, LANES)


GATHER_TILE = 256
GATHER_ROWS = (64, 128)
GATHER_GROUP = 4


def _gather_kernel(before_ref, fits_ref, h2_ref, slot_ref, w_ref, xe_ref, gate_ref, xe_s, gate_s):
    b = pl.program_id(0)
    g = pl.program_id(1)
    n_g, cap = xe_ref.shape[1], xe_ref.shape[2]
    n = h2_ref.shape[1]
    n_tiles = n // GATHER_TILE
    head = BF16_ROWS

    def windowed(n_rows):
        xe_s[:, 0:head, :] = jnp.zeros((n_g, head, D_MODEL), BF16)
        gate_s[...] = jnp.zeros_like(gate_s)
        row_id = _iota2((n_rows, 1), 0).astype(F32)
        for j in range(n_tiles):
            starts, onehots = [], []
            for e in range(n_g):
                start = pl.multiple_of((before_ref[b, g * n_g + e, j] // head) * head, head)
                hit = (slot_ref[0, e, j:j + 1, :] - start.astype(F32)) == row_id
                onehots.append(jnp.where(hit, 1.0, 0.0).astype(BF16))
                gate_s[e, pl.ds(start, n_rows), :] += jnp.sum(
                    jnp.where(hit, w_ref[0, e, j:j + 1, :], 0.0), axis=1, keepdims=True)
                starts.append(start)
            rows = _dot(jnp.concatenate(onehots, axis=0),
                        h2_ref[0, j * GATHER_TILE:(j + 1) * GATHER_TILE, :]).astype(BF16)
            for e, start in enumerate(starts):
                r0 = e * n_rows
                xe_s[e, pl.ds(start, head), :] += rows[r0:r0 + head]
                xe_s[e, pl.ds(start + head, n_rows - head), :] = rows[r0 + head:r0 + n_rows]
        xe_ref[0] = xe_s[:, 0:cap, :]
        for e in range(n_g):
            gate_ref[0, e] = _cols_to_rows(gate_s[e, 0:cap, :])

    for level, n_rows in enumerate(GATHER_ROWS):
        pl.when(fits_ref[b, g] == level)(functools.partial(windowed, n_rows))

    @pl.when(fits_ref[b, g] == len(GATHER_ROWS))
    def _():
        slot_id = _iota2((cap, 1), 0).astype(F32)
        for e in range(n_g):
            xe = jnp.zeros((cap, D_MODEL), F32)
            gate = jnp.zeros((cap, 1), F32)
            for j in range(n_tiles):
                hit = slot_ref[0, e, j:j + 1, :] == slot_id
                xe = xe + _dot(jnp.where(hit, 1.0, 0.0).astype(BF16),
                               h2_ref[0, j * GATHER_TILE:(j + 1) * GATHER_TILE, :])
                gate = gate + jnp.sum(jnp.where(hit, w_ref[0, e, j:j + 1, :], 0.0), axis=1,
                                      keepdims=True)
            xe_ref[0, e] = xe.astype(BF16)
            gate_ref[0, e] = _cols_to_rows(gate)


def _gather(h2, slot, w, first, capacity):
    b, n, _ = h2.shape
    n_e = slot.shape[1]
    n_tiles = n // GATHER_TILE
    assert n_e % GATHER_GROUP == 0 and GATHER_TILE % LANES == 0 and capacity % LANES == 0
    assert all(r % BF16_ROWS == 0 and r >= 2 * BF16_ROWS for r in GATHER_ROWS)
    assert list(GATHER_ROWS) == sorted(GATHER_ROWS)
    before = first[:, :, 0:n // LANES:GATHER_TILE // LANES].astype(jnp.int32)
    after = jnp.concatenate([before[:, :, 1:], jnp.full((b, n_e, 1), capacity, jnp.int32)], axis=2)
    extent = jnp.max(after - (before // BF16_ROWS) * BF16_ROWS, axis=2)
    extent = jnp.max(extent.reshape(b, n_e // GATHER_GROUP, GATHER_GROUP), axis=2)
    fits = sum((extent >= n_rows).astype(jnp.int32) for n_rows in GATHER_ROWS)
    tiles = lambda a: a.reshape(b, n_e, n_tiles, GATHER_TILE)
    rows = pl.BlockSpec((1, GATHER_GROUP, n_tiles, GATHER_TILE), lambda i, g, *_: (i, g, 0, 0))
    return pl.pallas_call(
        _gather_kernel,
        grid_spec=pltpu.PrefetchScalarGridSpec(
            num_scalar_prefetch=2,
            grid=(b, n_e // GATHER_GROUP),
            in_specs=[pl.BlockSpec((1, n, D_MODEL), lambda i, g, *_: (i, 0, 0)), rows, rows],
            out_specs=[pl.BlockSpec((1, GATHER_GROUP, capacity, D_MODEL), lambda i, g, *_: (i, g, 0, 0)),
                       pl.BlockSpec((1, GATHER_GROUP, capacity // LANES, LANES),
                                    lambda i, g, *_: (i, g, 0, 0))],
            scratch_shapes=[pltpu.VMEM((GATHER_GROUP, capacity + max(GATHER_ROWS), D_MODEL), BF16),
                            pltpu.VMEM((GATHER_GROUP, capacity + max(GATHER_ROWS), 1), F32)]),
        out_shape=[jax.ShapeDtypeStruct((b, n_e, capacity, D_MODEL), BF16),
                   jax.ShapeDtypeStruct((b, n_e, capacity // LANES, LANES), F32)],
        compiler_params=_params("arbitrary", "arbitrary"),
        name="gather",
    )(before, fits, h2, tiles(slot), tiles(w))


def _ffn_kernel(xe_ref, gate_ref, wg_ref, wu_ref, wd_ref, ye_ref, acc_s):
    ft = pl.program_id(1)
    last = pl.num_programs(1) - 1

    def step(first, final):
        wg = wg_ref[0].astype(BF16)
        wu = wu_ref[0].astype(BF16)
        wd = wd_ref[0].astype(BF16)
        for i in range(xe_ref.shape[0]):
            xe = xe_ref[i, 0]
            act = jax.nn.silu(_dot(xe, wg)) * _dot(xe, wu)
            part = _dot(act.astype(BF16), wd)
            total = part if first else acc_s[i] + part
            if final:
                ye_ref[i] = (total * _rows_to_cols(gate_ref[i, 0])).astype(BF16)
            else:
                acc_s[i] = total

    pl.when(ft == 0)(lambda: step(True, False))
    pl.when(jnp.logical_and(ft > 0, ft < last))(lambda: step(False, False))
    pl.when(ft == last)(lambda: step(False, True))


def _ffn(xe, gate, wg, wu, wd, f_tile):
    b, n_e, cap, _ = xe.shape
    assert D_EXPERT // f_tile >= 2
    return pl.pallas_call(
        _ffn_kernel,
        grid=(n_e, D_EXPERT // f_tile),
        in_specs=[pl.BlockSpec((b, 1, cap, D_MODEL), lambda e, f: (0, e, 0, 0)),
                  pl.BlockSpec((b, 1, cap // LANES, LANES), lambda e, f: (0, e, 0, 0)),
                  pl.BlockSpec((1, D_MODEL, f_tile), lambda e, f: (e, 0, f)),
                  pl.BlockSpec((1, D_MODEL, f_tile), lambda e, f: (e, 0, f)),
                  pl.BlockSpec((1, f_tile, D_MODEL), lambda e, f: (e, f, 0))],
        out_specs=pl.BlockSpec((b, cap, D_MODEL), lambda e, f: (0, e, 0)),
        out_shape=jax.ShapeDtypeStruct((b, n_e * cap, D_MODEL), BF16),
        scratch_shapes=[pltpu.VMEM((b, cap, D_MODEL), F32)],
        compiler_params=_params("arbitrary", "arbitrary"),
        name="ffn",
    )(xe, gate, wg, wu, wd)


COMBINE_WIDE = CHUNK + BF16_ROWS
COMBINE_NARROW = 48


def _combine_kernel(first_ref, narrow_ref, slot_t_ref, first_t_ref, ye_ref, x1_ref, mod_ref, fg_ref,
                    o_ref, acc_s, *, capacity):
    b = pl.program_id(0)
    j = pl.program_id(1)
    n_sub = o_ref.shape[1] // CHUNK
    lane = _iota2((1, LANES), 1).astype(F32)

    def scatter(window):
        k_total = N_EXPERTS * window
        for sb in range(n_sub):
            blk = j * n_sub + sb
            slot_t = slot_t_ref[0, sb * CHUNK:(sb + 1) * CHUNK, :]
            start_row = jnp.minimum(
                jnp.floor(first_t_ref[0, pl.ds(blk, 1), :] * (1.0 / BF16_ROWS)) * float(BF16_ROWS),
                float(capacity - window))
            k_pos = jnp.where(slot_t >= 0.0, slot_t - start_row + lane * float(window), -1.0)
            cols = []
            for c in range(k_total // LANES):
                k_lane = lane + float(LANES * c)
                hit = jnp.zeros((CHUNK, LANES), F32)
                for e in range((LANES * c) // window, (LANES * c + LANES - 1) // window + 1):
                    hit = jnp.where(k_pos[:, e:e + 1] == k_lane, 1.0, hit)
                cols.append(hit.astype(BF16))
            onehot = jnp.concatenate(cols, axis=1)
            rows = []
            for e in range(N_EXPERTS):
                start = jnp.minimum((first_ref[b, blk, e] // BF16_ROWS) * BF16_ROWS, capacity - window)
                rows.append(ye_ref[0, pl.ds(pl.multiple_of(e * capacity + start, BF16_ROWS), window), :])
            acc_s[sb * CHUNK:(sb + 1) * CHUNK, :] = _dot(onehot, jnp.concatenate(rows, axis=0))

    pl.when(narrow_ref[b, j] != 0)(lambda: scatter(COMBINE_NARROW))
    pl.when(narrow_ref[b, j] == 0)(lambda: scatter(COMBINE_WIDE))

    g2 = mod_ref[pl.ds(b, 1), 5 * D_MODEL:6 * D_MODEL]
    x2 = x1_ref[0] + g2 * acc_s[...]
    o_ref[0] = x2 * lax.rsqrt(jnp.mean(x2 * x2, axis=-1, keepdims=True) + EPS) * fg_ref[...]


def _combine(slot_t, first_t, ye, x1, mod, fg, capacity, tm):
    b, n, _ = x1.shape
    n_blocks = n // CHUNK
    assert CHUNK == LANES and N_EXPERTS <= LANES and tm % CHUNK == 0
    for window in (COMBINE_WIDE, COMBINE_NARROW):
        assert (N_EXPERTS * window) % LANES == 0 and window % BF16_ROWS == 0 and window <= capacity
    assert COMBINE_WIDE >= CHUNK + BF16_ROWS - 1 and capacity % BF16_ROWS == 0
    first = first_t[:, :, :N_EXPERTS].astype(jnp.int32)
    after = jnp.concatenate([first[:, 1:], jnp.full((b, 1, N_EXPERTS), capacity, jnp.int32)], axis=1)
    narrow = jnp.all(after - (first // BF16_ROWS) * BF16_ROWS <= COMBINE_NARROW, axis=2)
    narrow = jnp.all(narrow.reshape(b, n // tm, tm // CHUNK), axis=2).astype(jnp.int32)
    tok = pl.BlockSpec((1, tm, D_MODEL), lambda i, j, *_: (i, j, 0))
    return pl.pallas_call(
        functools.partial(_combine_kernel, capacity=capacity),
        grid_spec=pltpu.PrefetchScalarGridSpec(
            num_scalar_prefetch=2,
            grid=(b, n // tm),
            in_specs=[pl.BlockSpec((1, tm, LANES), lambda i, j, *_: (i, j, 0)),
                      pl.BlockSpec((1, n_blocks, LANES), lambda i, j, *_: (i, 0, 0)),
                      pl.BlockSpec((1, N_EXPERTS * capacity, D_MODEL), lambda i, j, *_: (i, 0, 0)),
                      tok,
                      pl.BlockSpec(mod.shape, lambda i, j, *_: (0, 0)),
                      pl.BlockSpec((1, D_MODEL), lambda i, j, *_: (0, 0))],
            out_specs=tok,
            scratch_shapes=[pltpu.VMEM((tm, D_MODEL), F32)]),
        out_shape=jax.ShapeDtypeStruct((b, n, D_MODEL), F32),
        compiler_params=_params("arbitrary", "arbitrary"),
        name="combine",
    )(first, narrow, slot_t, first_t, ye, x1, mod, fg)


def kernel(x, c, ctx, c_ctx, w_mod, b_mod, norm_mix_g, w_in, conv_q, conv_k, b_igate, b_fgate,
           gmlp_ws, gmlp_bs, mlstm_norm_g, w_out, norm_ffn_g, w_router, w_gate_e, w_up_e,
           w_down_e, final_g):
    depth = w_mod.shape[0]
    assert depth == 1, "the context stream is only carried as mLSTM states (single layer)"
    batch, seq, _ = x.shape
    assert seq % GRID_W == 0 and seq % CHUNK == 0 and batch + 1 <= MOD_ROWS
    capacity = EC_FACTOR * seq // N_EXPERTS
    ctx_row = batch
    l = 0

    cond = jnp.concatenate([c, c_ctx[None], jnp.zeros((MOD_ROWS - batch - 1, D_MODEL), F32)], axis=0)
    mod = _adaln(cond, w_mod[l], b_mod[l][None])

    row = lambda a: a[None]
    gate_bias = jnp.pad(jnp.concatenate([b_igate[l].reshape(-1), b_fgate[l].reshape(-1)]),
                        (0, GATE_PAD - N_GATES))[None]
    k_scale = HEAD_DIM ** -0.5

    identity = lambda a: a
    k_c, v_c, gates_c = _inproj(ctx, mod, row(norm_mix_g[l]), w_in[l], (K_BLK, V_BLK), conv_k[l][None],
                                (k_scale,), (identity,), tm=ctx.shape[1], ctx_row=ctx_row)
    state = _mlstm(None, k_c, v_c, 0, gates_c, gate_bias, None)

    q_l, k_l, p, gates = _inproj(x, mod, row(norm_mix_g[l]), w_in[l],
                                 (Q_BLK, K_BLK, U_BLK, VG_BLK, O_BLK, V_BLK),
                                 jnp.stack([conv_q[l], conv_k[l]]), (1.0, k_scale),
                                 (jax.nn.gelu, lambda a: _layer_norm(jax.nn.gelu(a)), jax.nn.sigmoid, identity),
                                 tm=512, ctx_row=None)
    h_f, h_b = _mlstm(q_l, k_l, p, P_V, gates, gate_bias, state)

    x1, h2, aff_t = _postmix(p, h_f, h_b, x, mod, gmlp_ws[l].astype(BF16), gmlp_bs[l][:, :, None],
                             row(mlstm_norm_g[l]), w_out[l].astype(BF16), row(norm_ffn_g[l]),
                             jnp.pad(w_router[l], ((0, 0), (0, LANES - N_EXPERTS))), tm=1024)

    slot, gate_w, slot_t, first_t, first = _route(aff_t, capacity)

    xe, gate = _gather(h2, slot, gate_w, first, capacity)
    ye = _ffn(xe, gate, w_gate_e[l], w_up_e[l], w_down_e[l], f_tile=512)
    return _combine(slot_t, first_t, ye, x1, mod, row(final_g), capacity, tm=512)
```

```python
import functools

import jax
import jax.numpy as jnp
from jax import lax
from jax.experimental import pallas as pl
from jax.experimental.pallas import tpu as pltpu

F32 = jnp.float32
BF16 = jnp.bfloat16

LANES = 128
BF16_ROWS = 16
VMEM_LIMIT = 56 * 1024 * 1024

D_MODEL = 1024
GRID_W = 64
CHUNK = 128
HEADS = 4
GROUP_W = D_MODEL // 2
HEAD_DIM = GROUP_W // HEADS
N_EXPERTS = 16
EC_FACTOR = 2
D_EXPERT = 2 * D_MODEL
EPS = 1e-6
N_GATES = 4 * HEADS
GATE_PAD = LANES
MOD_ROWS = 8

U_BLK, VG_BLK, Q_BLK, O_BLK, K_BLK, V_BLK = 0, 1, 2, 3, 4, 5
MAIN_W = 6 * GROUP_W
P_U, P_VG, P_O, P_V = 0, 1, 2, 3

ROW_GROUP = 2 * CHUNK
MLSTM_SAMPLES = 4


def _params(*sem):
    return pltpu.CompilerParams(dimension_semantics=sem, vmem_limit_bytes=VMEM_LIMIT)


def _dot(a, b):
    return jnp.dot(a, b, preferred_element_type=F32)


def _dot_nt(a, b):
    return lax.dot_general(a, b, (((1,), (1,)), ((), ())), preferred_element_type=F32)


def _split2(a):
    hi = a.astype(BF16)
    lo = (a - hi.astype(F32)).astype(BF16)
    return hi, lo


def _dot3(a, b):
    ah, al = _split2(a)
    bh, bl = _split2(b)
    return _dot(ah, bh) + (_dot(al, bh) + _dot(ah, bl))


def _dot_exact01(tri, x):
    x1 = x.astype(BF16)
    r1 = x - x1.astype(F32)
    x2 = r1.astype(BF16)
    x3 = (r1 - x2.astype(F32)).astype(BF16)
    return _dot(tri, x1) + (_dot(tri, x2) + _dot(tri, x3))


def _iota2(shape, dim):
    return lax.broadcasted_iota(jnp.int32, shape, dim)


def _eye():
    return _iota2((LANES, LANES), 0) == _iota2((LANES, LANES), 1)


def _cols_to_rows(col):
    return jnp.concatenate(
        [jnp.sum(jnp.where(_eye(), col[r * LANES:(r + 1) * LANES, :], 0.0), axis=0, keepdims=True)
         for r in range(col.shape[0] // LANES)], axis=0)


def _rows_to_cols(rows):
    return jnp.concatenate(
        [jnp.sum(jnp.where(_eye(), rows[r:r + 1, :], 0.0), axis=1, keepdims=True)
         for r in range(rows.shape[0])], axis=0)


def _adaln_kernel(cond_ref, w_ref, b_ref, o_ref):
    o_ref[...] = _dot3(jax.nn.silu(cond_ref[...]), w_ref[...]) + b_ref[...]


def _adaln(cond, w, b):
    n_out = w.shape[1]
    tn = D_MODEL
    assert n_out % tn == 0
    return pl.pallas_call(
        _adaln_kernel,
        grid=(n_out // tn,),
        in_specs=[pl.BlockSpec((MOD_ROWS, D_MODEL), lambda j: (0, 0)),
                  pl.BlockSpec((D_MODEL, tn), lambda j: (0, j)),
                  pl.BlockSpec((1, tn), lambda j: (0, j))],
        out_specs=pl.BlockSpec((MOD_ROWS, tn), lambda j: (0, j)),
        out_shape=jax.ShapeDtypeStruct((MOD_ROWS, n_out), F32),
        compiler_params=_params("arbitrary"),
        name="adaln",
    )(cond, w, b)


def _inproj_kernel(x_ref, xprev_ref, xnext_ref, mod_ref, g_ref, taps_ref, w_ref, *outs, ctx_row, scales,
                   plain_acts, raw_blocks):
    conv_refs, (p_ref, gate_ref), w_bf = outs[:len(scales)], outs[len(scales):len(scales) + 2], outs[-1]
    row = pl.program_id(0) if ctx_row is None else ctx_row
    j = pl.program_id(1)

    @pl.when(jnp.logical_and(pl.program_id(0) == 0, j == 0))
    def _():
        for k, blk in enumerate(raw_blocks):
            w_bf[:, k * GROUP_W:(k + 1) * GROUP_W] = w_ref[blk * GROUP_W:(blk + 1) * GROUP_W, :].T.astype(BF16)
        g0 = len(raw_blocks) * GROUP_W
        n_gate = w_ref.shape[0] - MAIN_W
        gate_rows = jnp.concatenate([w_ref[MAIN_W:MAIN_W + n_gate, :],
                                     jnp.zeros((GATE_PAD - n_gate, D_MODEL), F32)], axis=0)
        w_bf[:, g0:g0 + GATE_PAD] = gate_rows.T.astype(BF16)
    sh = mod_ref[pl.ds(row, 1), 0:D_MODEL]
    sc = mod_ref[pl.ds(row, 1), D_MODEL:2 * D_MODEL]

    def modulated(x):
        y = x * lax.rsqrt(jnp.mean(x * x, axis=-1, keepdims=True) + EPS) * g_ref[...]
        return y * (1.0 + sc) + sh

    n_groups = x_ref.shape[1] // ROW_GROUP
    before = modulated(xprev_ref[0]) * (j > 0).astype(F32)
    for r in range(n_groups):
        rows = slice(r * ROW_GROUP, (r + 1) * ROW_GROUP)
        h = modulated(x_ref[0, rows, :])
        if r == n_groups - 1:
            after = modulated(xnext_ref[0]) * (j < pl.num_programs(1) - 1).astype(F32)
        else:
            after = modulated(x_ref[0, (r + 1) * ROW_GROUP:(r + 1) * ROW_GROUP + 8, :])
        ext = jnp.concatenate([before, h, after], axis=0).astype(BF16)
        before = h[ROW_GROUP - 8:, :]
        y = _dot(ext, w_bf[...])
        n_ext, n_conv = ext.shape[0], len(scales) * GROUP_W
        y_here = y[8:8 + ROW_GROUP]
        for k, act in enumerate(plain_acts):
            lanes = slice(n_conv + k * GROUP_W, n_conv + (k + 1) * GROUP_W)
            p_ref[0, rows, k * GROUP_W:(k + 1) * GROUP_W] = act(y_here[:, lanes])
        gate_ref[0, rows, :] = y_here[:, n_conv + p_ref.shape[2]:]
        y_conv = y[:, 0:n_conv]
        y_before = pltpu.roll(y_conv, 1, axis=0)[8:8 + ROW_GROUP]
        y_after = pltpu.roll(y_conv, n_ext - 1, axis=0)[8:8 + ROW_GROUP]
        for s, scale in enumerate(scales):
            lanes = slice(s * GROUP_W, (s + 1) * GROUP_W)
            z = (y_before[:, lanes] * taps_ref[s, 0:1, :] + y_here[:, lanes] * taps_ref[s, 1:2, :]
                 + y_after[:, lanes] * taps_ref[s, 2:3, :])
            conv_refs[s][0, rows, :] = (jax.nn.silu(z) * scale).astype(BF16)


def _inproj(x, mod, g, w, raw_blocks, taps, scales, plain_acts, tm, ctx_row):
    b, n, _ = x.shape
    wn = len(plain_acts) * GROUP_W
    w_cols = len(scales) * GROUP_W + wn + GATE_PAD
    assert len(raw_blocks) == len(scales) + len(plain_acts) and 0 < w.shape[1] - MAIN_W <= GATE_PAD
    assert w.shape[1] % 8 == 0
    w = w.T
    assert tm % ROW_GROUP == 0 and n % tm == 0 and wn % LANES == 0
    rows8 = tm // 8
    last8 = n // 8 - 1
    full = lambda a: pl.BlockSpec(a.shape, lambda i, j: (0,) * a.ndim)
    conv_spec = pl.BlockSpec((1, tm, GROUP_W), lambda i, j: (i, j, 0))
    return pl.pallas_call(
        functools.partial(_inproj_kernel, ctx_row=ctx_row, scales=tuple(scales),
                          plain_acts=tuple(plain_acts), raw_blocks=tuple(raw_blocks)),
        grid=(b, n // tm),
        in_specs=[pl.BlockSpec((1, tm, D_MODEL), lambda i, j: (i, j, 0)),
                  pl.BlockSpec((1, 8, D_MODEL), lambda i, j: (i, jnp.maximum(j * rows8 - 1, 0), 0)),
                  pl.BlockSpec((1, 8, D_MODEL), lambda i, j: (i, jnp.minimum((j + 1) * rows8, last8), 0)),
                  full(mod), full(g), full(taps), full(w)],
        out_specs=[conv_spec] * len(scales)
        + [pl.BlockSpec((1, tm, wn), lambda i, j: (i, j, 0)),
           pl.BlockSpec((1, tm, GATE_PAD), lambda i, j: (i, j, 0))],
        out_shape=[jax.ShapeDtypeStruct((b, n, GROUP_W), BF16)] * len(scales)
        + [jax.ShapeDtypeStruct((b, n, wn), F32), jax.ShapeDtypeStruct((b, n, GATE_PAD), F32)],
        scratch_shapes=[pltpu.VMEM((D_MODEL, w_cols), BF16)],
        compiler_params=_params("arbitrary", "arbitrary"),
        name="inproj",
    )(x, x, x, mod, g, taps, w)


def _mlstm_chunk(per_dir, bias_ref, c_s, n_s, m_s, h_refs):
    with_h = h_refs[0] is not None
    i0 = _iota2((CHUNK, CHUNK), 0)
    i1 = _iota2((CHUNK, CHUNK), 1)

    units = []
    for smp, d in [(smp, d) for smp in range(c_s.shape[0]) for d in range(2)]:
        q_ref, k_ref, v_ref, g_ref = per_dir[d]
        sees_ts = (i1 <= i0) if d == 0 else (i1 >= i0)
        sees_st = (i0 <= i1) if d == 0 else (i0 >= i1)
        tri = jnp.where(sees_ts, 1.0, 0.0).astype(BF16)
        gates = g_ref[smp] + bias_ref[...]
        bcum = _dot_exact01(tri, jax.nn.log_sigmoid(gates))
        for hd in range(HEADS):
            lanes = slice(hd * HEAD_DIM, (hd + 1) * HEAD_DIM)
            c_prev = c_s[smp, d, hd]
            n_prev = n_s[smp, d, hd]
            k = k_ref[smp, :, lanes]
            st = dict(smp=smp, d=d, hd=hd, lanes=lanes, c_prev=c_prev, n_prev=n_prev, k=k,
                      sees_st=sees_st, gates=gates, bcum=bcum, v=v_ref[smp, :, lanes])
            if with_h:
                lhs = jnp.concatenate([k, c_prev.astype(BF16),
                                       jnp.broadcast_to(n_prev, (BF16_ROWS, HEAD_DIM)).astype(BF16)], axis=0)
                st["prod"] = _dot_nt(lhs, q_ref[smp, :, lanes])
            units.append(st)

    rows = {}
    for st in units:
        smp, d, hd = st["smp"], st["d"], st["hd"]
        if (smp, d) not in rows:
            rows[smp, d] = (st["gates"].T, st["bcum"].T)
        gates_t, bcum_t = rows[smp, d]
        ci = d * HEADS + hd
        cf = 2 * HEADS + d * HEADS + hd
        last = CHUNK - 1 if d == 0 else 0
        li_row = gates_t[ci:ci + 1, :]
        bc_row = bcum_t[cf:cf + 1, :]
        b_last = bc_row[:, last:last + 1]
        m_prev = m_s[smp, d, hd]
        v_t = st["v"].T
        a_row = b_last - bc_row + li_row
        m_new = jnp.maximum(b_last + m_prev, jnp.max(a_row, axis=-1, keepdims=True))
        w_row = jnp.exp(a_row - m_new)
        lhs = jnp.concatenate([v_t * w_row, jnp.broadcast_to(w_row, (BF16_ROWS, CHUNK))], axis=0)
        st["upd"] = _dot(lhs.astype(BF16), st["k"])
        st.update(bc_row=bc_row, m_prev=m_prev, m_new=m_new, v_t=v_t,
                  decay=jnp.exp(b_last + m_prev - m_new),
                  u_col=st["gates"][:, ci:ci + 1] - st["bcum"][:, cf:cf + 1])

    if with_h:
        for st in units:
            prod, bc_row = st["prod"], st["bc_row"]
            g = bc_row + st["m_prev"]
            dmat = jnp.where(st["sees_st"], st["u_col"] + bc_row, -jnp.inf)
            m_t = jnp.maximum(g, jnp.max(dmat, axis=0, keepdims=True))
            inter = jnp.exp(g - m_t)
            s = prod[0:CHUNK] * jnp.exp(dmat - m_t)
            st["pv"] = _dot(st["v_t"].astype(BF16), s.astype(BF16))
            st["num0"] = inter * prod[CHUNK:2 * CHUNK]
            den = inter * prod[2 * CHUNK:2 * CHUNK + 1] + jnp.sum(s, axis=0, keepdims=True)
            st["scale"] = 1.0 / jnp.maximum(jnp.abs(den), jnp.exp(-m_t))

    for st in units:
        smp, d, hd = st["smp"], st["d"], st["hd"]
        if with_h:
            h_refs[d][smp, :, st["lanes"]] = ((st["num0"] + st["pv"]) * st["scale"]).T
        c_s[smp, d, hd] = st["decay"] * st["c_prev"] + st["upd"][0:HEAD_DIM]
        n_s[smp, d, hd] = st["decay"] * st["n_prev"] + st["upd"][HEAD_DIM:HEAD_DIM + 1]
        m_s[smp, d, hd] = st["m_new"]


def _mlstm_kernel(*refs, n_chunks, with_h):
    refs = list(refs)
    take = lambda n: [refs.pop(0) for _ in range(n)]
    per_dir = []
    for _ in range(2):
        q_ref = take(1)[0] if with_h else None
        k_ref, v_ref, g_ref = take(3)
        per_dir.append((q_ref, k_ref, v_ref, g_ref))
    bias_ref = take(1)[0]
    if with_h:
        c0_ref, n0_ref, m0_ref = take(3)
        h_refs = take(2)
        c_out = n_out = m_out = None
    else:
        h_refs = [None, None]
        c_out, n_out, m_out = take(3)
    c_s, n_s, m_s = take(3)

    j = pl.program_id(1)

    @pl.when(j == 0)
    def _():
        if with_h:
            c_s[...] = c0_ref[...]
            n_s[...] = n0_ref[...]
            m_s[...] = m0_ref[...]
        else:
            c_s[...] = jnp.zeros_like(c_s)
            n_s[...] = jnp.zeros_like(n_s)
            m_s[...] = jnp.zeros_like(m_s)

    _mlstm_chunk(per_dir, bias_ref, c_s, n_s, m_s, h_refs)

    if not with_h:
        @pl.when(j == n_chunks - 1)
        def _():
            c_out[...] = c_s[...]
            n_out[...] = n_s[...]
            m_out[...] = m_s[...]


def _mlstm(q, k, p, v_blk, gates, bias, state):
    b, n, _ = k.shape
    n_chunks = n // CHUNK
    with_h = q is not None
    ns = MLSTM_SAMPLES if b % MLSTM_SAMPLES == 0 else 1

    in_specs, args = [], []
    for d in range(2):
        c = (lambda j: j) if d == 0 else (lambda j: n_chunks - 1 - j)
        tok = pl.BlockSpec((ns, CHUNK, GROUP_W), lambda i, j, c=c: (i, c(j), 0))
        if with_h:
            in_specs.append(tok)
            args.append(q)
        in_specs += [tok,
                     pl.BlockSpec((ns, CHUNK, GROUP_W), lambda i, j, c=c: (i, c(j), v_blk)),
                     pl.BlockSpec((ns, CHUNK, GATE_PAD), lambda i, j, c=c: (i, c(j), 0))]
        args += [k, p, gates]
    in_specs.append(pl.BlockSpec((1, GATE_PAD), lambda i, j: (0, 0)))
    args.append(bias)

    c_shape = (2, HEADS, HEAD_DIM, HEAD_DIM)
    v_shape = (2, HEADS, 1, HEAD_DIM)
    c_spec = pl.BlockSpec((ns,) + c_shape, lambda i, j: (i, 0, 0, 0, 0))
    v_spec = pl.BlockSpec((ns,) + v_shape, lambda i, j: (i, 0, 0, 0, 0))
    if with_h:
        in_specs += [c_spec, v_spec, v_spec]
        args += list(state)
        out_specs = [pl.BlockSpec((ns, CHUNK, GROUP_W), lambda i, j: (i, j, 0)),
                     pl.BlockSpec((ns, CHUNK, GROUP_W), lambda i, j: (i, n_chunks - 1 - j, 0))]
        out_shape = [jax.ShapeDtypeStruct((b, n, GROUP_W), F32)] * 2
    else:
        out_specs = [c_spec, v_spec, v_spec]
        out_shape = [jax.ShapeDtypeStruct((b,) + c_shape, F32),
                     jax.ShapeDtypeStruct((b,) + v_shape, F32),
                     jax.ShapeDtypeStruct((b,) + v_shape, F32)]
    return pl.pallas_call(
        functools.partial(_mlstm_kernel, n_chunks=n_chunks, with_h=with_h),
        grid=(b // ns, n_chunks),
        in_specs=in_specs,
        out_specs=out_specs,
        out_shape=out_shape,
        scratch_shapes=[pltpu.VMEM((ns,) + c_shape, F32), pltpu.VMEM((ns,) + v_shape, F32),
                        pltpu.VMEM((ns,) + v_shape, F32)],
        compiler_params=_params("arbitrary", "arbitrary"),
        name="mlstm" if with_h else "mlstm_ctx_state",
    )(*args)


def _layer_norm(x):
    mu = jnp.mean(x, axis=-1, keepdims=True)
    var = jnp.mean(jnp.square(x - mu), axis=-1, keepdims=True)
    return (x - mu) * lax.rsqrt(var + EPS)


def _postmix_kernel(u_ref, vg_ref, o_ref, hf_ref, hb_ref, x_ref, mod_ref, ws_ref, bs_ref, ng_ref,
                    wout_ref, fg_ref, wr_ref, x1_ref, h2_ref, aff_ref, ycat_s):
    tm = x_ref.shape[1]
    b = pl.program_id(0)
    mod = lambda k: mod_ref[pl.ds(b, 1), k * D_MODEL:(k + 1) * D_MODEL]
    wr_hi, wr_lo = _split2(wr_ref[...])
    expert_lane = _iota2((ROW_GROUP, LANES), 1) < N_EXPERTS

    for r in range(tm // ROW_GROUP):
        rows = slice(r * ROW_GROUP, (r + 1) * ROW_GROUP)

        for c in range(r * ROW_GROUP // CHUNK, (r + 1) * ROW_GROUP // CHUNK):
            crows = slice(c * CHUNK, (c + 1) * CHUNK)
            u = u_ref[0, crows, :]
            v = vg_ref[0, crows, :].astype(BF16)
            for hd in range(HEADS):
                lanes = slice(hd * HEAD_DIM, (hd + 1) * HEAD_DIM)
                s = _dot(ws_ref[hd], v[:, lanes]) + bs_ref[hd]
                ycat_s[crows, lanes] = (u[:, lanes] * s).astype(BF16)

        hsum = hf_ref[0, rows, :] + hb_ref[0, rows, :]
        o = o_ref[0, rows, :]
        for hd in range(HEADS):
            lanes = slice(hd * HEAD_DIM, (hd + 1) * HEAD_DIM)
            hn = _layer_norm(hsum[:, lanes]) * ng_ref[:, lanes]
            ycat_s[rows, GROUP_W + hd * HEAD_DIM:GROUP_W + (hd + 1) * HEAD_DIM] = (
                o[:, lanes] * hn).astype(BF16)

        y = _dot(ycat_s[rows, :], wout_ref[...])
        x1 = x_ref[0, rows, :] + mod(2) * y
        x1_ref[0, rows, :] = x1

        n2 = x1 * lax.rsqrt(jnp.mean(x1 * x1, axis=-1, keepdims=True) + EPS) * fg_ref[...]
        h2 = n2 * (1.0 + mod(4)) + mod(3)
        h2_hi, h2_lo = _split2(h2)
        h2_ref[0, rows, :] = h2_hi

        logits = _dot(h2_hi, wr_hi) + (_dot(h2_lo, wr_hi) + _dot(h2_hi, wr_lo))
        logits = jnp.where(expert_lane, logits, -jnp.inf)
        e = jnp.exp(logits - jnp.max(logits, axis=-1, keepdims=True))
        aff = e / jnp.sum(e, axis=-1, keepdims=True)
        aff_ref[0, :, rows] = aff.T[0:N_EXPERTS, :]


def _postmix(p, hf, hb, x, mod, ws, bs, ng, wout, fg, wr, tm):
    b, n, _ = x.shape
    tok = lambda blk: pl.BlockSpec((1, tm, GROUP_W), lambda i, j: (i, j, blk))
    full = lambda a: pl.BlockSpec(a.shape, lambda i, j: (0,) * a.ndim)
    return pl.pallas_call(
        _postmix_kernel,
        grid=(b, n // tm),
        in_specs=[tok(P_U), tok(P_VG), tok(P_O), tok(0), tok(0),
                  pl.BlockSpec((1, tm, D_MODEL), lambda i, j: (i, j, 0)),
                  full(mod), full(ws), full(bs), full(ng), full(wout), full(fg), full(wr)],
        out_specs=[pl.BlockSpec((1, tm, D_MODEL), lambda i, j: (i, j, 0)),
                   pl.BlockSpec((1, tm, D_MODEL), lambda i, j: (i, j, 0)),
                   pl.BlockSpec((1, N_EXPERTS, tm), lambda i, j: (i, 0, j))],
        out_shape=[jax.ShapeDtypeStruct((b, n, D_MODEL), F32),
                   jax.ShapeDtypeStruct((b, n, D_MODEL), BF16),
                   jax.ShapeDtypeStruct((b, N_EXPERTS, n), F32)],
        scratch_shapes=[pltpu.VMEM((tm, D_MODEL), BF16)],
        compiler_params=_params("arbitrary", "arbitrary"),
        name="postmix",
    )(p, p, p, hf, hb, x, mod, ws, bs, ng, wout, fg, wr)


def _cumsum_lanes(x, upper):
    carry = jnp.zeros((x.shape[0], 1), F32)
    outs, before = [], []
    for j in range(x.shape[1] // LANES):
        before.append(carry)
        c = _dot(x[:, j * LANES:(j + 1) * LANES].astype(BF16), upper) + carry
        outs.append(c)
        carry = c[:, LANES - 1:LANES]
    return jnp.concatenate(outs, axis=1), before


def _rows_to_lanes(x, fill):
    pad = jnp.full((LANES - x.shape[0], LANES), fill, F32)
    return jnp.concatenate([x, pad], axis=0).T


def _route_kernel(aff_ref, slot_ref, w_ref, slot_t_ref, first_t_ref, first_ref, *, capacity, n_e):
    aff = aff_ref[...]
    cap = float(capacity)
    thr_bits = jnp.zeros((aff.shape[0], 1), jnp.int32)
    for bit in range(30, -1, -1):
        cand = thr_bits | (1 << bit)
        cnt = jnp.sum(jnp.where(aff >= pltpu.bitcast(cand, F32), 1.0, 0.0), axis=-1, keepdims=True)
        thr_bits = jnp.where(cnt >= cap, cand, thr_bits)
    thr = pltpu.bitcast(thr_bits, F32)
    upper = jnp.where(_iota2((LANES, LANES), 0) <= _iota2((LANES, LANES), 1), 1.0, 0.0).astype(BF16)
    above = jnp.where(aff > thr, 1.0, 0.0)
    tied = jnp.where(aff == thr, 1.0, 0.0)
    need = cap - jnp.sum(above, axis=-1, keepdims=True)
    sel = above + tied * jnp.where(_cumsum_lanes(tied, upper)[0] <= need, 1.0, 0.0)
    count, before = _cumsum_lanes(sel, upper)
    slot = jnp.where(sel > 0.0, count - 1.0, -1.0)
    slot_ref[...] = slot
    w_ref[...] = jnp.where(sel > 0.0, aff, 0.0)

    n_blocks = len(before)
    lane = _iota2((1, LANES), 1)
    first = jnp.zeros((aff.shape[0], LANES), F32)
    for j in range(n_blocks):
        first = jnp.where(lane == j, before[j], first)
    first_ref[...] = first
    for smp in range(aff.shape[0] // n_e):
        rows = slice(smp * n_e, (smp + 1) * n_e)
        for j in range(n_blocks):
            slot_t_ref[smp, j * LANES:(j + 1) * LANES, :] = _rows_to_lanes(
                slot[rows, j * LANES:(j + 1) * LANES], -1.0)
        first_t_ref[smp] = _rows_to_lanes(first[rows, :], 0.0)[0:n_blocks, :]


def _route(aff_t, capacity):
    b, e, n = aff_t.shape
    assert n // LANES <= LANES and e <= LANES and e % 8 == 0
    rows = pl.BlockSpec((b * e, n), lambda i: (0, 0))
    slot, w, slot_t, first_t, first = pl.pallas_call(
        functools.partial(_route_kernel, capacity=capacity, n_e=e),
        grid=(1,),
        in_specs=[rows],
        out_specs=[rows, rows,
                   pl.BlockSpec((b, n, LANES), lambda i: (0, 0, 0)),
                   pl.BlockSpec((b, n // LANES, LANES), lambda i: (0, 0, 0)),
                   pl.BlockSpec((b * e, LANES), lambda i: (0, 0))],
        out_shape=[jax.ShapeDtypeStruct((b * e, n), F32)] * 2
        + [jax.ShapeDtypeStruct((b, n, LANES), F32), jax.ShapeDtypeStruct((b, n // LANES, LANES), F32),
           jax.ShapeDtypeStruct((b * e, LANES), F32)],
        compiler_params=_params("arbitrary"),
        name="route",
    )(aff_t.reshape(b * e, n))
    return slot.reshape(b, e, n), w.reshape(b, e, n), slot_t, first_t, first.reshape(b, e, LANES)


GATHER_TILE = 256
GATHER_ROWS = (64, 128)
GATHER_GROUP = 4


def _gather_kernel(before_ref, fits_ref, h2_ref, slot_ref, w_ref, xe_ref, gate_ref, xe_s, gate_s):
    b = pl.program_id(0)
    g = pl.program_id(1)
    n_g, cap = xe_ref.shape[1], xe_ref.shape[2]
    n = h2_ref.shape[1]
    n_tiles = n // GATHER_TILE
    head = BF16_ROWS

    def windowed(n_rows):
        xe_s[:, 0:head, :] = jnp.zeros((n_g, head, D_MODEL), BF16)
        gate_s[...] = jnp.zeros_like(gate_s)
        row_id = _iota2((n_rows, 1), 0).astype(F32)
        for j in range(n_tiles):
            starts, onehots = [], []
            for e in range(n_g):
                start = pl.multiple_of((before_ref[b, g * n_g + e, j] // head) * head, head)
                hit = (slot_ref[0, e, j:j + 1, :] - start.astype(F32)) == row_id
                onehots.append(jnp.where(hit, 1.0, 0.0).astype(BF16))
                gate_s[e, pl.ds(start, n_rows), :] += jnp.sum(
                    jnp.where(hit, w_ref[0, e, j:j + 1, :], 0.0), axis=1, keepdims=True)
                starts.append(start)
            rows = _dot(jnp.concatenate(onehots, axis=0),
                        h2_ref[0, j * GATHER_TILE:(j + 1) * GATHER_TILE, :]).astype(BF16)
            for e, start in enumerate(starts):
                r0 = e * n_rows
                xe_s[e, pl.ds(start, head), :] += rows[r0:r0 + head]
                xe_s[e, pl.ds(start + head, n_rows - head), :] = rows[r0 + head:r0 + n_rows]
        xe_ref[0] = xe_s[:, 0:cap, :]
        for e in range(n_g):
            gate_ref[0, e] = _cols_to_rows(gate_s[e, 0:cap, :])

    for level, n_rows in enumerate(GATHER_ROWS):
        pl.when(fits_ref[b, g] == level)(functools.partial(windowed, n_rows))

    @pl.when(fits_ref[b, g] == len(GATHER_ROWS))
    def _():
        slot_id = _iota2((cap, 1), 0).astype(F32)
        for e in range(n_g):
            xe = jnp.zeros((cap, D_MODEL), F32)
            gate = jnp.zeros((cap, 1), F32)
            for j in range(n_tiles):
                hit = slot_ref[0, e, j:j + 1, :] == slot_id
                xe = xe + _dot(jnp.where(hit, 1.0, 0.0).astype(BF16),
                               h2_ref[0, j * GATHER_TILE:(j + 1) * GATHER_TILE, :])
                gate = gate + jnp.sum(jnp.where(hit, w_ref[0, e, j:j + 1, :], 0.0), axis=1,
                                      keepdims=True)
            xe_ref[0, e] = xe.astype(BF16)
            gate_ref[0, e] = _cols_to_rows(gate)


def _gather(h2, slot, w, first, capacity):
    b, n, _ = h2.shape
    n_e = slot.shape[1]
    n_tiles = n // GATHER_TILE
    assert n_e % GATHER_GROUP == 0 and GATHER_TILE % LANES == 0 and capacity % LANES == 0
    assert all(r % BF16_ROWS == 0 and r >= 2 * BF16_ROWS for r in GATHER_ROWS)
    assert list(GATHER_ROWS) == sorted(GATHER_ROWS)
    before = first[:, :, 0:n // LANES:GATHER_TILE // LANES].astype(jnp.int32)
    after = jnp.concatenate([before[:, :, 1:], jnp.full((b, n_e, 1), capacity, jnp.int32)], axis=2)
    extent = jnp.max(after - (before // BF16_ROWS) * BF16_ROWS, axis=2)
    extent = jnp.max(extent.reshape(b, n_e // GATHER_GROUP, GATHER_GROUP), axis=2)
    fits = sum((extent >= n_rows).astype(jnp.int32) for n_rows in GATHER_ROWS)
    tiles = lambda a: a.reshape(b, n_e, n_tiles, GATHER_TILE)
    rows = pl.BlockSpec((1, GATHER_GROUP, n_tiles, GATHER_TILE), lambda i, g, *_: (i, g, 0, 0))
    return pl.pallas_call(
        _gather_kernel,
        grid_spec=pltpu.PrefetchScalarGridSpec(
            num_scalar_prefetch=2,
            grid=(b, n_e // GATHER_GROUP),
            in_specs=[pl.BlockSpec((1, n, D_MODEL), lambda i, g, *_: (i, 0, 0)), rows, rows],
            out_specs=[pl.BlockSpec((1, GATHER_GROUP, capacity, D_MODEL), lambda i, g, *_: (i, g, 0, 0)),
                       pl.BlockSpec((1, GATHER_GROUP, capacity // LANES, LANES),
                                    lambda i, g, *_: (i, g, 0, 0))],
            scratch_shapes=[pltpu.VMEM((GATHER_GROUP, capacity + max(GATHER_ROWS), D_MODEL), BF16),
                            pltpu.VMEM((GATHER_GROUP, capacity + max(GATHER_ROWS), 1), F32)]),
        out_shape=[jax.ShapeDtypeStruct((b, n_e, capacity, D_MODEL), BF16),
                   jax.ShapeDtypeStruct((b, n_e, capacity // LANES, LANES), F32)],
        compiler_params=_params("arbitrary", "arbitrary"),
        name="gather",
    )(before, fits, h2, tiles(slot), tiles(w))


def _ffn_kernel(xe_ref, gate_ref, wg_ref, wu_ref, wd_ref, ye_ref, acc_s):
    ft = pl.program_id(1)
    last = pl.num_programs(1) - 1

    def step(first, final):
        wg = wg_ref[0].astype(BF16)
        wu = wu_ref[0].astype(BF16)
        wd = wd_ref[0].astype(BF16)
        for i in range(xe_ref.shape[0]):
            xe = xe_ref[i, 0]
            act = jax.nn.silu(_dot(xe, wg)) * _dot(xe, wu)
            part = _dot(act.astype(BF16), wd)
            total = part if first else acc_s[i] + part
            if final:
                ye_ref[i] = (total * _rows_to_cols(gate_ref[i, 0])).astype(BF16)
            else:
                acc_s[i] = total

    pl.when(ft == 0)(lambda: step(True, False))
    pl.when(jnp.logical_and(ft > 0, ft < last))(lambda: step(False, False))
    pl.when(ft == last)(lambda: step(False, True))


def _ffn(xe, gate, wg, wu, wd, f_tile):
    b, n_e, cap, _ = xe.shape
    assert D_EXPERT // f_tile >= 2
    return pl.pallas_call(
        _ffn_kernel,
        grid=(n_e, D_EXPERT // f_tile),
        in_specs=[pl.BlockSpec((b, 1, cap, D_MODEL), lambda e, f: (0, e, 0, 0)),
                  pl.BlockSpec((b, 1, cap // LANES, LANES), lambda e, f: (0, e, 0, 0)),
                  pl.BlockSpec((1, D_MODEL, f_tile), lambda e, f: (e, 0, f)),
                  pl.BlockSpec((1, D_MODEL, f_tile), lambda e, f: (e, 0, f)),
                  pl.BlockSpec((1, f_tile, D_MODEL), lambda e, f: (e, f, 0))],
        out_specs=pl.BlockSpec((b, cap, D_MODEL), lambda e, f: (0, e, 0)),
        out_shape=jax.ShapeDtypeStruct((b, n_e * cap, D_MODEL), BF16),
        scratch_shapes=[pltpu.VMEM((b, cap, D_MODEL), F32)],
        compiler_params=_params("arbitrary", "arbitrary"),
        name="ffn",
    )(xe, gate, wg, wu, wd)


COMBINE_WIDE = CHUNK + BF16_ROWS
COMBINE_NARROW = 48


def _combine_kernel(first_ref, narrow_ref, slot_t_ref, first_t_ref, ye_ref, x1_ref, mod_ref, fg_ref,
                    o_ref, acc_s, *, capacity):
    b = pl.program_id(0)
    j = pl.program_id(1)
    n_sub = o_ref.shape[1] // CHUNK
    lane = _iota2((1, LANES), 1).astype(F32)

    def scatter(window):
        k_total = N_EXPERTS * window
        for sb in range(n_sub):
            blk = j * n_sub + sb
            slot_t = slot_t_ref[0, sb * CHUNK:(sb + 1) * CHUNK, :]
            start_row = jnp.minimum(
                jnp.floor(first_t_ref[0, pl.ds(blk, 1), :] * (1.0 / BF16_ROWS)) * float(BF16_ROWS),
                float(capacity - window))
            k_pos = jnp.where(slot_t >= 0.0, slot_t - start_row + lane * float(window), -1.0)
            cols = []
            for c in range(k_total // LANES):
                k_lane = lane + float(LANES * c)
                hit = jnp.zeros((CHUNK, LANES), F32)
                for e in range((LANES * c) // window, (LANES * c + LANES - 1) // window + 1):
                    hit = jnp.where(k_pos[:, e:e + 1] == k_lane, 1.0, hit)
                cols.append(hit.astype(BF16))
            onehot = jnp.concatenate(cols, axis=1)
            rows = []
            for e in range(N_EXPERTS):
                start = jnp.minimum((first_ref[b, blk, e] // BF16_ROWS) * BF16_ROWS, capacity - window)
                rows.append(ye_ref[0, pl.ds(pl.multiple_of(e * capacity + start, BF16_ROWS), window), :])
            acc_s[sb * CHUNK:(sb + 1) * CHUNK, :] = _dot(onehot, jnp.concatenate(rows, axis=0))

    pl.when(narrow_ref[b, j] != 0)(lambda: scatter(COMBINE_NARROW))
    pl.when(narrow_ref[b, j] == 0)(lambda: scatter(COMBINE_WIDE))

    g2 = mod_ref[pl.ds(b, 1), 5 * D_MODEL:6 * D_MODEL]
    x2 = x1_ref[0] + g2 * acc_s[...]
    o_ref[0] = x2 * lax.rsqrt(jnp.mean(x2 * x2, axis=-1, keepdims=True) + EPS) * fg_ref[...]


def _combine(slot_t, first_t, ye, x1, mod, fg, capacity, tm):
    b, n, _ = x1.shape
    n_blocks = n // CHUNK
    assert CHUNK == LANES and N_EXPERTS <= LANES and tm % CHUNK == 0
    for window in (COMBINE_WIDE, COMBINE_NARROW):
        assert (N_EXPERTS * window) % LANES == 0 and window % BF16_ROWS == 0 and window <= capacity
    assert COMBINE_WIDE >= CHUNK + BF16_ROWS - 1 and capacity % BF16_ROWS == 0
    first = first_t[:, :, :N_EXPERTS].astype(jnp.int32)
    after = jnp.concatenate([first[:, 1:], jnp.full((b, 1, N_EXPERTS), capacity, jnp.int32)], axis=1)
    narrow = jnp.all(after - (first // BF16_ROWS) * BF16_ROWS <= COMBINE_NARROW, axis=2)
    narrow = jnp.all(narrow.reshape(b, n // tm, tm // CHUNK), axis=2).astype(jnp.int32)
    tok = pl.BlockSpec((1, tm, D_MODEL), lambda i, j, *_: (i, j, 0))
    return pl.pallas_call(
        functools.partial(_combine_kernel, capacity=capacity),
        grid_spec=pltpu.PrefetchScalarGridSpec(
            num_scalar_prefetch=2,
            grid=(b, n // tm),
            in_specs=[pl.BlockSpec((1, tm, LANES), lambda i, j, *_: (i, j, 0)),
                      pl.BlockSpec((1, n_blocks, LANES), lambda i, j, *_: (i, 0, 0)),
                      pl.BlockSpec((1, N_EXPERTS * capacity, D_MODEL), lambda i, j, *_: (i, 0, 0)),
                      tok,
                      pl.BlockSpec(mod.shape, lambda i, j, *_: (0, 0)),
                      pl.BlockSpec((1, D_MODEL), lambda i, j, *_: (0, 0))],
            out_specs=tok,
            scratch_shapes=[pltpu.VMEM((tm, D_MODEL), F32)]),
        out_shape=jax.ShapeDtypeStruct((b, n, D_MODEL), F32),
        compiler_params=_params("arbitrary", "arbitrary"),
        name="combine",
    )(first, narrow, slot_t, first_t, ye, x1, mod, fg)


def kernel(x, c, ctx, c_ctx, w_mod, b_mod, norm_mix_g, w_in, conv_q, conv_k, b_igate, b_fgate,
           gmlp_ws, gmlp_bs, mlstm_norm_g, w_out, norm_ffn_g, w_router, w_gate_e, w_up_e,
           w_down_e, final_g):
    depth = w_mod.shape[0]
    assert depth == 1, "the context stream is only carried as mLSTM states (single layer)"
    batch, seq, _ = x.shape
    assert seq % GRID_W == 0 and seq % CHUNK == 0 and batch + 1 <= MOD_ROWS
    capacity = EC_FACTOR * seq // N_EXPERTS
    ctx_row = batch
    l = 0

    cond = jnp.concatenate([c, c_ctx[None], jnp.zeros((MOD_ROWS - batch - 1, D_MODEL), F32)], axis=0)
    mod = _adaln(cond, w_mod[l], b_mod[l][None])

    row = lambda a: a[None]
    gate_bias = jnp.pad(jnp.concatenate([b_igate[l].reshape(-1), b_fgate[l].reshape(-1)]),
                        (0, GATE_PAD - N_GATES))[None]
    k_scale = HEAD_DIM ** -0.5

    identity = lambda a: a
    k_c, v_c, gates_c = _inproj(ctx, mod, row(norm_mix_g[l]), w_in[l], (K_BLK, V_BLK), conv_k[l][None],
                                (k_scale,), (identity,), tm=ctx.shape[1], ctx_row=ctx_row)
    state = _mlstm(None, k_c, v_c, 0, gates_c, gate_bias, None)

    q_l, k_l, p, gates = _inproj(x, mod, row(norm_mix_g[l]), w_in[l],
                                 (Q_BLK, K_BLK, U_BLK, VG_BLK, O_BLK, V_BLK),
                                 jnp.stack([conv_q[l], conv_k[l]]), (1.0, k_scale),
                                 (jax.nn.gelu, lambda a: _layer_norm(jax.nn.gelu(a)), jax.nn.sigmoid, identity),
                                 tm=512, ctx_row=None)
    h_f, h_b = _mlstm(q_l, k_l, p, P_V, gates, gate_bias, state)

    x1, h2, aff_t = _postmix(p, h_f, h_b, x, mod, gmlp_ws[l].astype(BF16), gmlp_bs[l][:, :, None],
                             row(mlstm_norm_g[l]), w_out[l].astype(BF16), row(norm_ffn_g[l]),
                             jnp.pad(w_router[l], ((0, 0), (0, LANES - N_EXPERTS))), tm=1024)

    slot, gate_w, slot_t, first_t, first = _route(aff_t, capacity)

    xe, gate = _gather(h2, slot, gate_w, first, capacity)
    ye = _ffn(xe, gate, w_gate_e[l], w_up_e[l], w_down_e[l], f_tile=512)
    return _combine(slot_t, first_t, ye, x1, mod, row(final_g), capacity, tm=512)
```

```python
import functools

import jax
import jax.numpy as jnp
from jax import lax
from jax.experimental import pallas as pl
from jax.experimental.pallas import tpu as pltpu

F32 = jnp.float32
BF16 = jnp.bfloat16

LANES = 128
BF16_ROWS = 16
VMEM_LIMIT = 56 * 1024 * 1024

D_MODEL = 1024
GRID_W = 64
CHUNK = 128
HEADS = 4
GROUP_W = D_MODEL // 2
HEAD_DIM = GROUP_W // HEADS
N_EXPERTS = 16
EC_FACTOR = 2
D_EXPERT = 2 * D_MODEL
EPS = 1e-6
N_GATES = 4 * HEADS
GATE_PAD = LANES
MOD_ROWS = 8

U_BLK, VG_BLK, Q_BLK, O_BLK, K_BLK, V_BLK = 0, 1, 2, 3, 4, 5
MAIN_W = 6 * GROUP_W
P_U, P_VG, P_O, P_V = 0, 1, 2, 3

ROW_GROUP = 2 * CHUNK
MLSTM_SAMPLES = 4


def _params(*sem):
    return pltpu.CompilerParams(dimension_semantics=sem, vmem_limit_bytes=VMEM_LIMIT)


def _dot(a, b):
    return jnp.dot(a, b, preferred_element_type=F32)


def _dot_nt(a, b):
    return lax.dot_general(a, b, (((1,), (1,)), ((), ())), preferred_element_type=F32)


def _split2(a):
    hi = a.astype(BF16)
    lo = (a - hi.astype(F32)).astype(BF16)
    return hi, lo


def _dot3(a, b):
    ah, al = _split2(a)
    bh, bl = _split2(b)
    return _dot(ah, bh) + (_dot(al, bh) + _dot(ah, bl))


def _dot_exact01(tri, x):
    x1 = x.astype(BF16)
    r1 = x - x1.astype(F32)
    x2 = r1.astype(BF16)
    x3 = (r1 - x2.astype(F32)).astype(BF16)
    return _dot(tri, x1) + (_dot(tri, x2) + _dot(tri, x3))


def _iota2(shape, dim):
    return lax.broadcasted_iota(jnp.int32, shape, dim)


def _eye():
    return _iota2((LANES, LANES), 0) == _iota2((LANES, LANES), 1)


def _cols_to_rows(col):
    return jnp.concatenate(
        [jnp.sum(jnp.where(_eye(), col[r * LANES:(r + 1) * LANES, :], 0.0), axis=0, keepdims=True)
         for r in range(col.shape[0] // LANES)], axis=0)


def _rows_to_cols(rows):
    return jnp.concatenate(
        [jnp.sum(jnp.where(_eye(), rows[r:r + 1, :], 0.0), axis=1, keepdims=True)
         for r in range(rows.shape[0])], axis=0)


def _adaln_kernel(cond_ref, w_ref, b_ref, o_ref):
    o_ref[...] = _dot3(jax.nn.silu(cond_ref[...]), w_ref[...]) + b_ref[...]


def _adaln(cond, w, b):
    n_out = w.shape[1]
    tn = D_MODEL
    assert n_out % tn == 0
    return pl.pallas_call(
        _adaln_kernel,
        grid=(n_out // tn,),
        in_specs=[pl.BlockSpec((MOD_ROWS, D_MODEL), lambda j: (0, 0)),
                  pl.BlockSpec((D_MODEL, tn), lambda j: (0, j)),
                  pl.BlockSpec((1, tn), lambda j: (0, j))],
        out_specs=pl.BlockSpec((MOD_ROWS, tn), lambda j: (0, j)),
        out_shape=jax.ShapeDtypeStruct((MOD_ROWS, n_out), F32),
        compiler_params=_params("arbitrary"),
        name="adaln",
    )(cond, w, b)


def _inproj_kernel(x_ref, xprev_ref, xnext_ref, mod_ref, g_ref, taps_ref, w_ref, *outs, ctx_row, scales,
                   plain_acts, raw_blocks):
    conv_refs, (p_ref, gate_ref), w_bf = outs[:len(scales)], outs[len(scales):len(scales) + 2], outs[-1]
    row = pl.program_id(0) if ctx_row is None else ctx_row
    j = pl.program_id(1)

    @pl.when(jnp.logical_and(pl.program_id(0) == 0, j == 0))
    def _():
        for k, blk in enumerate(raw_blocks):
            w_bf[:, k * GROUP_W:(k + 1) * GROUP_W] = w_ref[blk * GROUP_W:(blk + 1) * GROUP_W, :].T.astype(BF16)
        g0 = len(raw_blocks) * GROUP_W
        n_gate = w_ref.shape[0] - MAIN_W
        gate_rows = jnp.concatenate([w_ref[MAIN_W:MAIN_W + n_gate, :],
                                     jnp.zeros((GATE_PAD - n_gate, D_MODEL), F32)], axis=0)
        w_bf[:, g0:g0 + GATE_PAD] = gate_rows.T.astype(BF16)
    sh = mod_ref[pl.ds(row, 1), 0:D_MODEL]
    sc = mod_ref[pl.ds(row, 1), D_MODEL:2 * D_MODEL]

    def modulated(x):
        y = x * lax.rsqrt(jnp.mean(x * x, axis=-1, keepdims=True) + EPS) * g_ref[...]
        return y * (1.0 + sc) + sh

    n_groups = x_ref.shape[1] // ROW_GROUP
    before = modulated(xprev_ref[0]) * (j > 0).astype(F32)
    for r in range(n_groups):
        rows = slice(r * ROW_GROUP, (r + 1) * ROW_GROUP)
        h = modulated(x_ref[0, rows, :])
        if r == n_groups - 1:
            after = modulated(xnext_ref[0]) * (j < pl.num_programs(1) - 1).astype(F32)
        else:
            after = modulated(x_ref[0, (r + 1) * ROW_GROUP:(r + 1) * ROW_GROUP + 8, :])
        ext = jnp.concatenate([before, h, after], axis=0).astype(BF16)
        before = h[ROW_GROUP - 8:, :]
        y = _dot(ext, w_bf[...])
        n_ext, n_conv = ext.shape[0], len(scales) * GROUP_W
        y_here = y[8:8 + ROW_GROUP]
        for k, act in enumerate(plain_acts):
            lanes = slice(n_conv + k * GROUP_W, n_conv + (k + 1) * GROUP_W)
            p_ref[0, rows, k * GROUP_W:(k + 1) * GROUP_W] = act(y_here[:, lanes])
        gate_ref[0, rows, :] = y_here[:, n_conv + p_ref.shape[2]:]
        y_conv = y[:, 0:n_conv]
        y_before = pltpu.roll(y_conv, 1, axis=0)[8:8 + ROW_GROUP]
        y_after = pltpu.roll(y_conv, n_ext - 1, axis=0)[8:8 + ROW_GROUP]
        for s, scale in enumerate(scales):
            lanes = slice(s * GROUP_W, (s + 1) * GROUP_W)
            z = (y_before[:, lanes] * taps_ref[s, 0:1, :] + y_here[:, lanes] * taps_ref[s, 1:2, :]
                 + y_after[:, lanes] * taps_ref[s, 2:3, :])
            conv_refs[s][0, rows, :] = (jax.nn.silu(z) * scale).astype(BF16)


def _inproj(x, mod, g, w, raw_blocks, taps, scales, plain_acts, tm, ctx_row):
    b, n, _ = x.shape
    wn = len(plain_acts) * GROUP_W
    w_cols = len(scales) * GROUP_W + wn + GATE_PAD
    assert len(raw_blocks) == len(scales) + len(plain_acts) and 0 < w.shape[1] - MAIN_W <= GATE_PAD
    assert w.shape[1] % 8 == 0
    w = w.T
    assert tm % ROW_GROUP == 0 and n % tm == 0 and wn % LANES == 0
    rows8 = tm // 8
    last8 = n // 8 - 1
    full = lambda a: pl.BlockSpec(a.shape, lambda i, j: (0,) * a.ndim)
    conv_spec = pl.BlockSpec((1, tm, GROUP_W), lambda i, j: (i, j, 0))
    return pl.pallas_call(
        functools.partial(_inproj_kernel, ctx_row=ctx_row, scales=tuple(scales),
                          plain_acts=tuple(plain_acts), raw_blocks=tuple(raw_blocks)),
        grid=(b, n // tm),
        in_specs=[pl.BlockSpec((1, tm, D_MODEL), lambda i, j: (i, j, 0)),
                  pl.BlockSpec((1, 8, D_MODEL), lambda i, j: (i, jnp.maximum(j * rows8 - 1, 0), 0)),
                  pl.BlockSpec((1, 8, D_MODEL), lambda i, j: (i, jnp.minimum((j + 1) * rows8, last8), 0)),
                  full(mod), full(g), full(taps), full(w)],
        out_specs=[conv_spec] * len(scales)
        + [pl.BlockSpec((1, tm, wn), lambda i, j: (i, j, 0)),
           pl.BlockSpec((1, tm, GATE_PAD), lambda i, j: (i, j, 0))],
        out_shape=[jax.ShapeDtypeStruct((b, n, GROUP_W), BF16)] * len(scales)
        + [jax.ShapeDtypeStruct((b, n, wn), F32), jax.ShapeDtypeStruct((b, n, GATE_PAD), F32)],
        scratch_shapes=[pltpu.VMEM((D_MODEL, w_cols), BF16)],
        compiler_params=_params("arbitrary", "arbitrary"),
        name="inproj",
    )(x, x, x, mod, g, taps, w)


def _mlstm_chunk(per_dir, bias_ref, c_s, n_s, m_s, h_refs):
    with_h = h_refs[0] is not None
    i0 = _iota2((CHUNK, CHUNK), 0)
    i1 = _iota2((CHUNK, CHUNK), 1)

    units = []
    for smp, d in [(smp, d) for smp in range(c_s.shape[0]) for d in range(2)]:
        q_ref, k_ref, v_ref, g_ref = per_dir[d]
        sees_ts = (i1 <= i0) if d == 0 else (i1 >= i0)
        sees_st = (i0 <= i1) if d == 0 else (i0 >= i1)
        tri = jnp.where(sees_ts, 1.0, 0.0).astype(BF16)
        gates = g_ref[smp] + bias_ref[...]
        bcum = _dot_exact01(tri, jax.nn.log_sigmoid(gates))
        for hd in range(HEADS):
            lanes = slice(hd * HEAD_DIM, (hd + 1) * HEAD_DIM)
            c_prev = c_s[smp, d, hd]
            n_prev = n_s[smp, d, hd]
            k = k_ref[smp, :, lanes]
            st = dict(smp=smp, d=d, hd=hd, lanes=lanes, c_prev=c_prev, n_prev=n_prev, k=k,
                      sees_st=sees_st, gates=gates, bcum=bcum, v=v_ref[smp, :, lanes])
            if with_h:
                lhs = jnp.concatenate([k, c_prev.astype(BF16),
                                       jnp.broadcast_to(n_prev, (BF16_ROWS, HEAD_DIM)).astype(BF16)], axis=0)
                st["prod"] = _dot_nt(lhs, q_ref[smp, :, lanes])
            units.append(st)

    rows = {}
    for st in units:
        smp, d, hd = st["smp"], st["d"], st["hd"]
        if (smp, d) not in rows:
            rows[smp, d] = (st["gates"].T, st["bcum"].T)
        gates_t, bcum_t = rows[smp, d]
        ci = d * HEADS + hd
        cf = 2 * HEADS + d * HEADS + hd
        last = CHUNK - 1 if d == 0 else 0
        li_row = gates_t[ci:ci + 1, :]
        bc_row = bcum_t[cf:cf + 1, :]
        b_last = bc_row[:, last:last + 1]
        m_prev = m_s[smp, d, hd]
        v_t = st["v"].T
        a_row = b_last - bc_row + li_row
        m_new = jnp.maximum(b_last + m_prev, jnp.max(a_row, axis=-1, keepdims=True))
        w_row = jnp.exp(a_row - m_new)
        lhs = jnp.concatenate([v_t * w_row, jnp.broadcast_to(w_row, (BF16_ROWS, CHUNK))], axis=0)
        st["upd"] = _dot(lhs.astype(BF16), st["k"])
        st.update(bc_row=bc_row, m_prev=m_prev, m_new=m_new, v_t=v_t,
                  decay=jnp.exp(b_last + m_prev - m_new),
                  u_col=st["gates"][:, ci:ci + 1] - st["bcum"][:, cf:cf + 1])

    if with_h:
        for st in units:
            prod, bc_row = st["prod"], st["bc_row"]
            g = bc_row + st["m_prev"]
            dmat = jnp.where(st["sees_st"], st["u_col"] + bc_row, -jnp.inf)
            m_t = jnp.maximum(g, jnp.max(dmat, axis=0, keepdims=True))
            inter = jnp.exp(g - m_t)
            s = prod[0:CHUNK] * jnp.exp(dmat - m_t)
            st["pv"] = _dot(st["v_t"].astype(BF16), s.astype(BF16))
            st["num0"] = inter * prod[CHUNK:2 * CHUNK]
            den = inter * prod[2 * CHUNK:2 * CHUNK + 1] + jnp.sum(s, axis=0, keepdims=True)
            st["scale"] = 1.0 / jnp.maximum(jnp.abs(den), jnp.exp(-m_t))

    for st in units:
        smp, d, hd = st["smp"], st["d"], st["hd"]
        if with_h:
            h_refs[d][smp, :, st["lanes"]] = ((st["num0"] + st["pv"]) * st["scale"]).T
        c_s[smp, d, hd] = st["decay"] * st["c_prev"] + st["upd"][0:HEAD_DIM]
        n_s[smp, d, hd] = st["decay"] * st["n_prev"] + st["upd"][HEAD_DIM:HEAD_DIM + 1]
        m_s[smp, d, hd] = st["m_new"]


def _mlstm_kernel(*refs, n_chunks, with_h):
    refs = list(refs)
    take = lambda n: [refs.pop(0) for _ in range(n)]
    per_dir = []
    for _ in range(2):
        q_ref = take(1)[0] if with_h else None
        k_ref, v_ref, g_ref = take(3)
        per_dir.append((q_ref, k_ref, v_ref, g_ref))
    bias_ref = take(1)[0]
    if with_h:
        c0_ref, n0_ref, m0_ref = take(3)
        h_refs = take(2)
        c_out = n_out = m_out = None
    else:
        h_refs = [None, None]
        c_out, n_out, m_out = take(3)
    c_s, n_s, m_s = take(3)

    j = pl.program_id(1)

    @pl.when(j == 0)
    def _():
        if with_h:
            c_s[...] = c0_ref[...]
            n_s[...] = n0_ref[...]
            m_s[...] = m0_ref[...]
        else:
            c_s[...] = jnp.zeros_like(c_s)
            n_s[...] = jnp.zeros_like(n_s)
            m_s[...] = jnp.zeros_like(m_s)

    _mlstm_chunk(per_dir, bias_ref, c_s, n_s, m_s, h_refs)

    if not with_h:
        @pl.when(j == n_chunks - 1)
        def _():
            c_out[...] = c_s[...]
            n_out[...] = n_s[...]
            m_out[...] = m_s[...]


def _mlstm(q, k, p, v_blk, gates, bias, state):
    b, n, _ = k.shape
    n_chunks = n // CHUNK
    with_h = q is not None
    ns = MLSTM_SAMPLES if b % MLSTM_SAMPLES == 0 else 1

    in_specs, args = [], []
    for d in range(2):
        c = (lambda j: j) if d == 0 else (lambda j: n_chunks - 1 - j)
        tok = pl.BlockSpec((ns, CHUNK, GROUP_W), lambda i, j, c=c: (i, c(j), 0))
        if with_h:
            in_specs.append(tok)
            args.append(q)
        in_specs += [tok,
                     pl.BlockSpec((ns, CHUNK, GROUP_W), lambda i, j, c=c: (i, c(j), v_blk)),
                     pl.BlockSpec((ns, CHUNK, GATE_PAD), lambda i, j, c=c: (i, c(j), 0))]
        args += [k, p, gates]
    in_specs.append(pl.BlockSpec((1, GATE_PAD), lambda i, j: (0, 0)))
    args.append(bias)

    c_shape = (2, HEADS, HEAD_DIM, HEAD_DIM)
    v_shape = (2, HEADS, 1, HEAD_DIM)
    c_spec = pl.BlockSpec((ns,) + c_shape, lambda i, j: (i, 0, 0, 0, 0))
    v_spec = pl.BlockSpec((ns,) + v_shape, lambda i, j: (i, 0, 0, 0, 0))
    if with_h:
        in_specs += [c_spec, v_spec, v_spec]
        args += list(state)
        out_specs = [pl.BlockSpec((ns, CHUNK, GROUP_W), lambda i, j: (i, j, 0)),
                     pl.BlockSpec((ns, CHUNK, GROUP_W), lambda i, j: (i, n_chunks - 1 - j, 0))]
        out_shape = [jax.ShapeDtypeStruct((b, n, GROUP_W), F32)] * 2
    else:
        out_specs = [c_spec, v_spec, v_spec]
        out_shape = [jax.ShapeDtypeStruct((b,) + c_shape, F32),
                     jax.ShapeDtypeStruct((b,) + v_shape, F32),
                     jax.ShapeDtypeStruct((b,) + v_shape, F32)]
    return pl.pallas_call(
        functools.partial(_mlstm_kernel, n_chunks=n_chunks, with_h=with_h),
        grid=(b // ns, n_chunks),
        in_specs=in_specs,
        out_specs=out_specs,
        out_shape=out_shape,
        scratch_shapes=[pltpu.VMEM((ns,) + c_shape, F32), pltpu.VMEM((ns,) + v_shape, F32),
                        pltpu.VMEM((ns,) + v_shape, F32)],
        compiler_params=_params("arbitrary", "arbitrary"),
        name="mlstm" if with_h else "mlstm_ctx_state",
    )(*args)


def _layer_norm(x):
    mu = jnp.mean(x, axis=-1, keepdims=True)
    var = jnp.mean(jnp.square(x - mu), axis=-1, keepdims=True)
    return (x - mu) * lax.rsqrt(var + EPS)


def _postmix_kernel(u_ref, vg_ref, o_ref, hf_ref, hb_ref, x_ref, mod_ref, ws_ref, bs_ref, ng_ref,
                    wout_ref, fg_ref, wr_ref, x1_ref, h2_ref, aff_ref, ycat_s):
    tm = x_ref.shape[1]
    b = pl.program_id(0)
    mod = lambda k: mod_ref[pl.ds(b, 1), k * D_MODEL:(k + 1) * D_MODEL]
    wr_hi, wr_lo = _split2(wr_ref[...])
    expert_lane = _iota2((ROW_GROUP, LANES), 1) < N_EXPERTS

    for r in range(tm // ROW_GROUP):
        rows = slice(r * ROW_GROUP, (r + 1) * ROW_GROUP)

        for c in range(r * ROW_GROUP // CHUNK, (r + 1) * ROW_GROUP // CHUNK):
            crows = slice(c * CHUNK, (c + 1) * CHUNK)
            u = u_ref[0, crows, :]
            v = vg_ref[0, crows, :].astype(BF16)
            for hd in range(HEADS):
                lanes = slice(hd * HEAD_DIM, (hd + 1) * HEAD_DIM)
                s = _dot(ws_ref[hd], v[:, lanes]) + bs_ref[hd]
                ycat_s[crows, lanes] = (u[:, lanes] * s).astype(BF16)

        hsum = hf_ref[0, rows, :] + hb_ref[0, rows, :]
        o = o_ref[0, rows, :]
        for hd in range(HEADS):
            lanes = slice(hd * HEAD_DIM, (hd + 1) * HEAD_DIM)
            hn = _layer_norm(hsum[:, lanes]) * ng_ref[:, lanes]
            ycat_s[rows, GROUP_W + hd * HEAD_DIM:GROUP_W + (hd + 1) * HEAD_DIM] = (
                o[:, lanes] * hn).astype(BF16)

        y = _dot(ycat_s[rows, :], wout_ref[...])
        x1 = x_ref[0, rows, :] + mod(2) * y
        x1_ref[0, rows, :] = x1

        n2 = x1 * lax.rsqrt(jnp.mean(x1 * x1, axis=-1, keepdims=True) + EPS) * fg_ref[...]
        h2 = n2 * (1.0 + mod(4)) + mod(3)
        h2_hi, h2_lo = _split2(h2)
        h2_ref[0, rows, :] = h2_hi

        logits = _dot(h2_hi, wr_hi) + (_dot(h2_lo, wr_hi) + _dot(h2_hi, wr_lo))
        logits = jnp.where(expert_lane, logits, -jnp.inf)
        e = jnp.exp(logits - jnp.max(logits, axis=-1, keepdims=True))
        aff = e / jnp.sum(e, axis=-1, keepdims=True)
        aff_ref[0, :, rows] = aff.T[0:N_EXPERTS, :]


def _postmix(p, hf, hb, x, mod, ws, bs, ng, wout, fg, wr, tm):
    b, n, _ = x.shape
    tok = lambda blk: pl.BlockSpec((1, tm, GROUP_W), lambda i, j: (i, j, blk))
    full = lambda a: pl.BlockSpec(a.shape, lambda i, j: (0,) * a.ndim)
    return pl.pallas_call(
        _postmix_kernel,
        grid=(b, n // tm),
        in_specs=[tok(P_U), tok(P_VG), tok(P_O), tok(0), tok(0),
                  pl.BlockSpec((1, tm, D_MODEL), lambda i, j: (i, j, 0)),
                  full(mod), full(ws), full(bs), full(ng), full(wout), full(fg), full(wr)],
        out_specs=[pl.BlockSpec((1, tm, D_MODEL), lambda i, j: (i, j, 0)),
                   pl.BlockSpec((1, tm, D_MODEL), lambda i, j: (i, j, 0)),
                   pl.BlockSpec((1, N_EXPERTS, tm), lambda i, j: (i, 0, j))],
        out_shape=[jax.ShapeDtypeStruct((b, n, D_MODEL), F32),
                   jax.ShapeDtypeStruct((b, n, D_MODEL), BF16),
                   jax.ShapeDtypeStruct((b, N_EXPERTS, n), F32)],
        scratch_shapes=[pltpu.VMEM((tm, D_MODEL), BF16)],
        compiler_params=_params("arbitrary", "arbitrary"),
        name="postmix",
    )(p, p, p, hf, hb, x, mod, ws, bs, ng, wout, fg, wr)


def _cumsum_lanes(x, upper):
    carry = jnp.zeros((x.shape[0], 1), F32)
    outs, before = [], []
    for j in range(x.shape[1] // LANES):
        before.append(carry)
        c = _dot(x[:, j * LANES:(j + 1) * LANES].astype(BF16), upper) + carry
        outs.append(c)
        carry = c[:, LANES - 1:LANES]
    return jnp.concatenate(outs, axis=1), before


def _rows_to_lanes(x, fill):
    pad = jnp.full((LANES - x.shape[0], LANES), fill, F32)
    return jnp.concatenate([x, pad], axis=0).T


def _route_kernel(aff_ref, slot_ref, w_ref, slot_t_ref, first_t_ref, first_ref, *, capacity, n_e):
    aff = aff_ref[...]
    cap = float(capacity)
    thr_bits = jnp.zeros((aff.shape[0], 1), jnp.int32)
    for bit in range(30, -1, -1):
        cand = thr_bits | (1 << bit)
        cnt = jnp.sum(jnp.where(aff >= pltpu.bitcast(cand, F32), 1.0, 0.0), axis=-1, keepdims=True)
        thr_bits = jnp.where(cnt >= cap, cand, thr_bits)
    thr = pltpu.bitcast(thr_bits, F32)
    upper = jnp.where(_iota2((LANES, LANES), 0) <= _iota2((LANES, LANES), 1), 1.0, 0.0).astype(BF16)
    above = jnp.where(aff > thr, 1.0, 0.0)
    tied = jnp.where(aff == thr, 1.0, 0.0)
    need = cap - jnp.sum(above, axis=-1, keepdims=True)
    sel = above + tied * jnp.where(_cumsum_lanes(tied, upper)[0] <= need, 1.0, 0.0)
    count, before = _cumsum_lanes(sel, upper)
    slot = jnp.where(sel > 0.0, count - 1.0, -1.0)
    slot_ref[...] = slot
    w_ref[...] = jnp.where(sel > 0.0, aff, 0.0)

    n_blocks = len(before)
    lane = _iota2((1, LANES), 1)
    first = jnp.zeros((aff.shape[0], LANES), F32)
    for j in range(n_blocks):
        first = jnp.where(lane == j, before[j], first)
    first_ref[...] = first
    for smp in range(aff.shape[0] // n_e):
        rows = slice(smp * n_e, (smp + 1) * n_e)
        for j in range(n_blocks):
            slot_t_ref[smp, j * LANES:(j + 1) * LANES, :] = _rows_to_lanes(
                slot[rows, j * LANES:(j + 1) * LANES], -1.0)
        first_t_ref[smp] = _rows_to_lanes(first[rows, :], 0.0)[0:n_blocks, :]


def _route(aff_t, capacity):
    b, e, n = aff_t.shape
    assert n // LANES <= LANES and e <= LANES and e % 8 == 0
    rows = pl.BlockSpec((b * e, n), lambda i: (0, 0))
    slot, w, slot_t, first_t, first = pl.pallas_call(
        functools.partial(_route_kernel, capacity=capacity, n_e=e),
        grid=(1,),
        in_specs=[rows],
        out_specs=[rows, rows,
                   pl.BlockSpec((b, n, LANES), lambda i: (0, 0, 0)),
                   pl.BlockSpec((b, n // LANES, LANES), lambda i: (0, 0, 0)),
                   pl.BlockSpec((b * e, LANES), lambda i: (0, 0))],
        out_shape=[jax.ShapeDtypeStruct((b * e, n), F32)] * 2
        + [jax.ShapeDtypeStruct((b, n, LANES), F32), jax.ShapeDtypeStruct((b, n // LANES, LANES), F32),
           jax.ShapeDtypeStruct((b * e, LANES), F32)],
        compiler_params=_params("arbitrary"),
        name="route",
    )(aff_t.reshape(b * e, n))
    return slot.reshape(b, e, n), w.reshape(b, e, n), slot_t, first_t, first.reshape(b, e, LANES)


GATHER_TILE = 256
GATHER_ROWS = (64, 128)
GATHER_GROUP = 4


def _gather_kernel(before_ref, fits_ref, h2_ref, slot_ref, w_ref, xe_ref, gate_ref, xe_s, gate_s):
    b = pl.program_id(0)
    g = pl.program_id(1)
    n_g, cap = xe_ref.shape[1], xe_ref.shape[2]
    n = h2_ref.shape[1]
    n_tiles = n // GATHER_TILE
    head = BF16_ROWS
    row0 = (g * n_g) % 8

    def tile_row(ref, e, j):
        return ref[pl.ds(row0 + e, 1), j * GATHER_TILE:(j + 1) * GATHER_TILE]

    def windowed(n_rows):
        xe_s[:, 0:head, :] = jnp.zeros((n_g, head, D_MODEL), BF16)
        gate_s[...] = jnp.zeros_like(gate_s)
        row_id = _iota2((n_rows, 1), 0).astype(F32)
        for j in range(n_tiles):
            starts, onehots = [], []
            for e in range(n_g):
                start = pl.multiple_of((before_ref[b, g * n_g + e, j] // head) * head, head)
                hit = (tile_row(slot_ref, e, j) - start.astype(F32)) == row_id
                onehots.append(jnp.where(hit, 1.0, 0.0).astype(BF16))
                gate_s[e, pl.ds(start, n_rows), :] += jnp.sum(
                    jnp.where(hit, tile_row(w_ref, e, j), 0.0), axis=1, keepdims=True)
                starts.append(start)
            rows = _dot(jnp.concatenate(onehots, axis=0),
                        h2_ref[0, j * GATHER_TILE:(j + 1) * GATHER_TILE, :]).astype(BF16)
            for e, start in enumerate(starts):
                r0 = e * n_rows
                xe_s[e, pl.ds(start, head), :] += rows[r0:r0 + head]
                xe_s[e, pl.ds(start + head, n_rows - head), :] = rows[r0 + head:r0 + n_rows]
        xe_ref[0] = xe_s[:, 0:cap, :]
        for e in range(n_g):
            gate_ref[0, e] = _cols_to_rows(gate_s[e, 0:cap, :])

    for level, n_rows in enumerate(GATHER_ROWS):
        pl.when(fits_ref[b, g] == level)(functools.partial(windowed, n_rows))

    @pl.when(fits_ref[b, g] == len(GATHER_ROWS))
    def _():
        slot_id = _iota2((cap, 1), 0).astype(F32)
        for e in range(n_g):
            xe = jnp.zeros((cap, D_MODEL), F32)
            gate = jnp.zeros((cap, 1), F32)
            for j in range(n_tiles):
                hit = tile_row(slot_ref, e, j) == slot_id
                xe = xe + _dot(jnp.where(hit, 1.0, 0.0).astype(BF16),
                               h2_ref[0, j * GATHER_TILE:(j + 1) * GATHER_TILE, :])
                gate = gate + jnp.sum(jnp.where(hit, tile_row(w_ref, e, j), 0.0), axis=1,
                                      keepdims=True)
            xe_ref[0, e] = xe.astype(BF16)
            gate_ref[0, e] = _cols_to_rows(gate)


def _gather(h2, slot, w, first, capacity):
    b, n, _ = h2.shape
    n_e = slot.shape[1]
    n_tiles = n // GATHER_TILE
    assert n_e % GATHER_GROUP == 0 and GATHER_TILE % LANES == 0 and capacity % LANES == 0
    assert all(r % BF16_ROWS == 0 and r >= 2 * BF16_ROWS for r in GATHER_ROWS)
    assert list(GATHER_ROWS) == sorted(GATHER_ROWS)
    before = first[:, :, 0:n // LANES:GATHER_TILE // LANES].astype(jnp.int32)
    after = jnp.concatenate([before[:, :, 1:], jnp.full((b, n_e, 1), capacity, jnp.int32)], axis=2)
    extent = jnp.max(after - (before // BF16_ROWS) * BF16_ROWS, axis=2)
    extent = jnp.max(extent.reshape(b, n_e // GATHER_GROUP, GATHER_GROUP), axis=2)
    fits = sum((extent >= n_rows).astype(jnp.int32) for n_rows in GATHER_ROWS)
    assert 8 % GATHER_GROUP == 0 and n_e % 8 == 0
    tiles = lambda a: a.reshape(b * n_e, n)
    rows = pl.BlockSpec((8, n), lambda i, g, *_: (i * (n_e // 8) + (g * GATHER_GROUP) // 8, 0))
    return pl.pallas_call(
        _gather_kernel,
        grid_spec=pltpu.PrefetchScalarGridSpec(
            num_scalar_prefetch=2,
            grid=(b, n_e // GATHER_GROUP),
            in_specs=[pl.BlockSpec((1, n, D_MODEL), lambda i, g, *_: (i, 0, 0)), rows, rows],
            out_specs=[pl.BlockSpec((1, GATHER_GROUP, capacity, D_MODEL), lambda i, g, *_: (i, g, 0, 0)),
                       pl.BlockSpec((1, GATHER_GROUP, capacity // LANES, LANES),
                                    lambda i, g, *_: (i, g, 0, 0))],
            scratch_shapes=[pltpu.VMEM((GATHER_GROUP, capacity + max(GATHER_ROWS), D_MODEL), BF16),
                            pltpu.VMEM((GATHER_GROUP, capacity + max(GATHER_ROWS), 1), F32)]),
        out_shape=[jax.ShapeDtypeStruct((b, n_e, capacity, D_MODEL), BF16),
                   jax.ShapeDtypeStruct((b, n_e, capacity // LANES, LANES), F32)],
        compiler_params=_params("arbitrary", "arbitrary"),
        name="gather",
    )(before, fits, h2, tiles(slot), tiles(w))


def _ffn_kernel(xe_ref, gate_ref, wg_ref, wu_ref, wd_ref, ye_ref, acc_s):
    ft = pl.program_id(1)
    last = pl.num_programs(1) - 1

    def step(first, final):
        wg = wg_ref[0].astype(BF16)
        wu = wu_ref[0].astype(BF16)
        wd = wd_ref[0].astype(BF16)
        for i in range(xe_ref.shape[0]):
            xe = xe_ref[i, 0]
            act = jax.nn.silu(_dot(xe, wg)) * _dot(xe, wu)
            part = _dot(act.astype(BF16), wd)
            total = part if first else acc_s[i] + part
            if final:
                ye_ref[i] = (total * _rows_to_cols(gate_ref[i, 0])).astype(BF16)
            else:
                acc_s[i] = total

    pl.when(ft == 0)(lambda: step(True, False))
    pl.when(jnp.logical_and(ft > 0, ft < last))(lambda: step(False, False))
    pl.when(ft == last)(lambda: step(False, True))


def _ffn(xe, gate, wg, wu, wd, f_tile):
    b, n_e, cap, _ = xe.shape
    assert D_EXPERT // f_tile >= 2
    return pl.pallas_call(
        _ffn_kernel,
        grid=(n_e, D_EXPERT // f_tile),
        in_specs=[pl.BlockSpec((b, 1, cap, D_MODEL), lambda e, f: (0, e, 0, 0)),
                  pl.BlockSpec((b, 1, cap // LANES, LANES), lambda e, f: (0, e, 0, 0)),
                  pl.BlockSpec((1, D_MODEL, f_tile), lambda e, f: (e, 0, f)),
                  pl.BlockSpec((1, D_MODEL, f_tile), lambda e, f: (e, 0, f)),
                  pl.BlockSpec((1, f_tile, D_MODEL), lambda e, f: (e, f, 0))],
        out_specs=pl.BlockSpec((b, cap, D_MODEL), lambda e, f: (0, e, 0)),
        out_shape=jax.ShapeDtypeStruct((b, n_e * cap, D_MODEL), BF16),
        scratch_shapes=[pltpu.VMEM((b, cap, D_MODEL), F32)],
        compiler_params=_params("arbitrary", "arbitrary"),
        name="ffn",
    )(xe, gate, wg, wu, wd)


COMBINE_WIDE = CHUNK + BF16_ROWS
COMBINE_NARROW = 48


def _combine_kernel(first_ref, narrow_ref, slot_t_ref, first_t_ref, ye_ref, x1_ref, mod_ref, fg_ref,
                    o_ref, acc_s, *, capacity):
    b = pl.program_id(0)
    j = pl.program_id(1)
    n_sub = o_ref.shape[1] // CHUNK
    lane = _iota2((1, LANES), 1).astype(F32)

    def scatter(window):
        k_total = N_EXPERTS * window
        for sb in range(n_sub):
            blk = j * n_sub + sb
            slot_t = slot_t_ref[0, sb * CHUNK:(sb + 1) * CHUNK, :]
            start_row = jnp.minimum(
                jnp.floor(first_t_ref[0, pl.ds(blk, 1), :] * (1.0 / BF16_ROWS)) * float(BF16_ROWS),
                float(capacity - window))
            k_pos = jnp.where(slot_t >= 0.0, slot_t - start_row + lane * float(window), -1.0)
            cols = []
            for c in range(k_total // LANES):
                k_lane = lane + float(LANES * c)
                hit = jnp.zeros((CHUNK, LANES), F32)
                for e in range((LANES * c) // window, (LANES * c + LANES - 1) // window + 1):
                    hit = jnp.where(k_pos[:, e:e + 1] == k_lane, 1.0, hit)
                cols.append(hit.astype(BF16))
            onehot = jnp.concatenate(cols, axis=1)
            rows = []
            for e in range(N_EXPERTS):
                start = jnp.minimum((first_ref[b, blk, e] // BF16_ROWS) * BF16_ROWS, capacity - window)
                rows.append(ye_ref[0, pl.ds(pl.multiple_of(e * capacity + start, BF16_ROWS), window), :])
            acc_s[sb * CHUNK:(sb + 1) * CHUNK, :] = _dot(onehot, jnp.concatenate(rows, axis=0))

    pl.when(narrow_ref[b, j] != 0)(lambda: scatter(COMBINE_NARROW))
    pl.when(narrow_ref[b, j] == 0)(lambda: scatter(COMBINE_WIDE))

    g2 = mod_ref[pl.ds(b, 1), 5 * D_MODEL:6 * D_MODEL]
    x2 = x1_ref[0] + g2 * acc_s[...]
    o_ref[0] = x2 * lax.rsqrt(jnp.mean(x2 * x2, axis=-1, keepdims=True) + EPS) * fg_ref[...]


def _combine(slot_t, first_t, ye, x1, mod, fg, capacity, tm):
    b, n, _ = x1.shape
    n_blocks = n // CHUNK
    assert CHUNK == LANES and N_EXPERTS <= LANES and tm % CHUNK == 0
    for window in (COMBINE_WIDE, COMBINE_NARROW):
        assert (N_EXPERTS * window) % LANES == 0 and window % BF16_ROWS == 0 and window <= capacity
    assert COMBINE_WIDE >= CHUNK + BF16_ROWS - 1 and capacity % BF16_ROWS == 0
    first = first_t[:, :, :N_EXPERTS].astype(jnp.int32)
    after = jnp.concatenate([first[:, 1:], jnp.full((b, 1, N_EXPERTS), capacity, jnp.int32)], axis=1)
    narrow = jnp.all(after - (first // BF16_ROWS) * BF16_ROWS <= COMBINE_NARROW, axis=2)
    narrow = jnp.all(narrow.reshape(b, n // tm, tm // CHUNK), axis=2).astype(jnp.int32)
    tok = pl.BlockSpec((1, tm, D_MODEL), lambda i, j, *_: (i, j, 0))
    return pl.pallas_call(
        functools.partial(_combine_kernel, capacity=capacity),
        grid_spec=pltpu.PrefetchScalarGridSpec(
            num_scalar_prefetch=2,
            grid=(b, n // tm),
            in_specs=[pl.BlockSpec((1, tm, LANES), lambda i, j, *_: (i, j, 0)),
                      pl.BlockSpec((1, n_blocks, LANES), lambda i, j, *_: (i, 0, 0)),
                      pl.BlockSpec((1, N_EXPERTS * capacity, D_MODEL), lambda i, j, *_: (i, 0, 0)),
                      tok,
                      pl.BlockSpec(mod.shape, lambda i, j, *_: (0, 0)),
                      pl.BlockSpec((1, D_MODEL), lambda i, j, *_: (0, 0))],
            out_specs=tok,
            scratch_shapes=[pltpu.VMEM((tm, D_MODEL), F32)]),
        out_shape=jax.ShapeDtypeStruct((b, n, D_MODEL), F32),
        compiler_params=_params("arbitrary", "arbitrary"),
        name="combine",
    )(first, narrow, slot_t, first_t, ye, x1, mod, fg)


def kernel(x, c, ctx, c_ctx, w_mod, b_mod, norm_mix_g, w_in, conv_q, conv_k, b_igate, b_fgate,
           gmlp_ws, gmlp_bs, mlstm_norm_g, w_out, norm_ffn_g, w_router, w_gate_e, w_up_e,
           w_down_e, final_g):
    depth = w_mod.shape[0]
    assert depth == 1, "the context stream is only carried as mLSTM states (single layer)"
    batch, seq, _ = x.shape
    assert seq % GRID_W == 0 and seq % CHUNK == 0 and batch + 1 <= MOD_ROWS
    capacity = EC_FACTOR * seq // N_EXPERTS
    ctx_row = batch
    l = 0

    cond = jnp.concatenate([c, c_ctx[None], jnp.zeros((MOD_ROWS - batch - 1, D_MODEL), F32)], axis=0)
    mod = _adaln(cond, w_mod[l], b_mod[l][None])

    row = lambda a: a[None]
    gate_bias = jnp.pad(jnp.concatenate([b_igate[l].reshape(-1), b_fgate[l].reshape(-1)]),
                        (0, GATE_PAD - N_GATES))[None]
    k_scale = HEAD_DIM ** -0.5

    identity = lambda a: a
    k_c, v_c, gates_c = _inproj(ctx, mod, row(norm_mix_g[l]), w_in[l], (K_BLK, V_BLK), conv_k[l][None],
                                (k_scale,), (identity,), tm=ctx.shape[1], ctx_row=ctx_row)
    state = _mlstm(None, k_c, v_c, 0, gates_c, gate_bias, None)

    q_l, k_l, p, gates = _inproj(x, mod, row(norm_mix_g[l]), w_in[l],
                                 (Q_BLK, K_BLK, U_BLK, VG_BLK, O_BLK, V_BLK),
                                 jnp.stack([conv_q[l], conv_k[l]]), (1.0, k_scale),
                                 (jax.nn.gelu, lambda a: _layer_norm(jax.nn.gelu(a)), jax.nn.sigmoid, identity),
                                 tm=512, ctx_row=None)
    h_f, h_b = _mlstm(q_l, k_l, p, P_V, gates, gate_bias, state)

    x1, h2, aff_t = _postmix(p, h_f, h_b, x, mod, gmlp_ws[l].astype(BF16), gmlp_bs[l][:, :, None],
                             row(mlstm_norm_g[l]), w_out[l].astype(BF16), row(norm_ffn_g[l]),
                             jnp.pad(w_router[l], ((0, 0), (0, LANES - N_EXPERTS))), tm=1024)

    slot, gate_w, slot_t, first_t, first = _route(aff_t, capacity)

    xe, gate = _gather(h2, slot, gate_w, first, capacity)
    ye = _ffn(xe, gate, w_gate_e[l], w_up_e[l], w_down_e[l], f_tile=512)
    return _combine(slot_t, first_t, ye, x1, mod, row(final_g), capacity, tm=512)
```

```python
import functools

import jax
import jax.numpy as jnp
from jax import lax
from jax.experimental import pallas as pl
from jax.experimental.pallas import tpu as pltpu

F32 = jnp.float32
BF16 = jnp.bfloat16

LANES = 128
BF16_ROWS = 16
VMEM_LIMIT = 56 * 1024 * 1024

D_MODEL = 1024
GRID_W = 64
CHUNK = 128
HEADS = 4
GROUP_W = D_MODEL // 2
HEAD_DIM = GROUP_W // HEADS
N_EXPERTS = 16
EC_FACTOR = 2
D_EXPERT = 2 * D_MODEL
EPS = 1e-6
N_GATES = 4 * HEADS
GATE_PAD = LANES
MOD_ROWS = 8

U_BLK, VG_BLK, Q_BLK, O_BLK, K_BLK, V_BLK = 0, 1, 2, 3, 4, 5
MAIN_W = 6 * GROUP_W
P_U, P_VG, P_O, P_V = 0, 1, 2, 3

ROW_GROUP = 2 * CHUNK
MLSTM_SAMPLES = 4


def _params(*sem):
    return pltpu.CompilerParams(dimension_semantics=sem, vmem_limit_bytes=VMEM_LIMIT)


def _dot(a, b):
    return jnp.dot(a, b, preferred_element_type=F32)


def _dot_nt(a, b):
    return lax.dot_general(a, b, (((1,), (1,)), ((), ())), preferred_element_type=F32)


def _split2(a):
    hi = a.astype(BF16)
    lo = (a - hi.astype(F32)).astype(BF16)
    return hi, lo


def _dot3(a, b):
    ah, al = _split2(a)
    bh, bl = _split2(b)
    return _dot(ah, bh) + (_dot(al, bh) + _dot(ah, bl))


def _dot_exact01(tri, x):
    x1 = x.astype(BF16)
    r1 = x - x1.astype(F32)
    x2 = r1.astype(BF16)
    x3 = (r1 - x2.astype(F32)).astype(BF16)
    return _dot(tri, x1) + (_dot(tri, x2) + _dot(tri, x3))


def _iota2(shape, dim):
    return lax.broadcasted_iota(jnp.int32, shape, dim)


def _eye():
    return _iota2((LANES, LANES), 0) == _iota2((LANES, LANES), 1)


def _cols_to_rows(col):
    return jnp.concatenate(
        [jnp.sum(jnp.where(_eye(), col[r * LANES:(r + 1) * LANES, :], 0.0), axis=0, keepdims=True)
         for r in range(col.shape[0] // LANES)], axis=0)


def _rows_to_cols(rows):
    return jnp.concatenate(
        [jnp.sum(jnp.where(_eye(), rows[r:r + 1, :], 0.0), axis=1, keepdims=True)
         for r in range(rows.shape[0])], axis=0)


def _adaln_kernel(cond_ref, w_ref, b_ref, o_ref):
    o_ref[...] = _dot3(jax.nn.silu(cond_ref[...]), w_ref[...]) + b_ref[...]


def _adaln(cond, w, b):
    n_out = w.shape[1]
    tn = D_MODEL
    assert n_out % tn == 0
    return pl.pallas_call(
        _adaln_kernel,
        grid=(n_out // tn,),
        in_specs=[pl.BlockSpec((MOD_ROWS, D_MODEL), lambda j: (0, 0)),
                  pl.BlockSpec((D_MODEL, tn), lambda j: (0, j)),
                  pl.BlockSpec((1, tn), lambda j: (0, j))],
        out_specs=pl.BlockSpec((MOD_ROWS, tn), lambda j: (0, j)),
        out_shape=jax.ShapeDtypeStruct((MOD_ROWS, n_out), F32),
        compiler_params=_params("arbitrary"),
        name="adaln",
    )(cond, w, b)


def _inproj_kernel(x_ref, xprev_ref, xnext_ref, mod_ref, g_ref, taps_ref, w_ref, *outs, ctx_row, scales,
                   plain_acts, raw_blocks):
    conv_refs, (p_ref, gate_ref), w_bf = outs[:len(scales)], outs[len(scales):len(scales) + 2], outs[-1]
    row = pl.program_id(0) if ctx_row is None else ctx_row
    j = pl.program_id(1)

    @pl.when(jnp.logical_and(pl.program_id(0) == 0, j == 0))
    def _():
        for k, blk in enumerate(raw_blocks):
            w_bf[:, k * GROUP_W:(k + 1) * GROUP_W] = w_ref[blk * GROUP_W:(blk + 1) * GROUP_W, :].T.astype(BF16)
        g0 = len(raw_blocks) * GROUP_W
        n_gate = w_ref.shape[0] - MAIN_W
        gate_rows = jnp.concatenate([w_ref[MAIN_W:MAIN_W + n_gate, :],
                                     jnp.zeros((GATE_PAD - n_gate, D_MODEL), F32)], axis=0)
        w_bf[:, g0:g0 + GATE_PAD] = gate_rows.T.astype(BF16)
    sh = mod_ref[pl.ds(row, 1), 0:D_MODEL]
    sc = mod_ref[pl.ds(row, 1), D_MODEL:2 * D_MODEL]

    def modulated(x):
        y = x * lax.rsqrt(jnp.mean(x * x, axis=-1, keepdims=True) + EPS) * g_ref[...]
        return y * (1.0 + sc) + sh

    n_groups = x_ref.shape[1] // ROW_GROUP
    before = modulated(xprev_ref[0]) * (j > 0).astype(F32)
    for r in range(n_groups):
        rows = slice(r * ROW_GROUP, (r + 1) * ROW_GROUP)
        h = modulated(x_ref[0, rows, :])
        if r == n_groups - 1:
            after = modulated(xnext_ref[0]) * (j < pl.num_programs(1) - 1).astype(F32)
        else:
            after = modulated(x_ref[0, (r + 1) * ROW_GROUP:(r + 1) * ROW_GROUP + 8, :])
        ext = jnp.concatenate([before, h, after], axis=0).astype(BF16)
        before = h[ROW_GROUP - 8:, :]
        y = _dot(ext, w_bf[...])
        n_ext, n_conv = ext.shape[0], len(scales) * GROUP_W
        y_here = y[8:8 + ROW_GROUP]
        for k, act in enumerate(plain_acts):
            lanes = slice(n_conv + k * GROUP_W, n_conv + (k + 1) * GROUP_W)
            p_ref[0, rows, k * GROUP_W:(k + 1) * GROUP_W] = act(y_here[:, lanes])
        gate_ref[0, rows, :] = y_here[:, n_conv + p_ref.shape[2]:]
        y_conv = y[:, 0:n_conv]
        y_before = pltpu.roll(y_conv, 1, axis=0)[8:8 + ROW_GROUP]
        y_after = pltpu.roll(y_conv, n_ext - 1, axis=0)[8:8 + ROW_GROUP]
        for s, scale in enumerate(scales):
            lanes = slice(s * GROUP_W, (s + 1) * GROUP_W)
            z = (y_before[:, lanes] * taps_ref[s, 0:1, :] + y_here[:, lanes] * taps_ref[s, 1:2, :]
                 + y_after[:, lanes] * taps_ref[s, 2:3, :])
            conv_refs[s][0, rows, :] = (jax.nn.silu(z) * scale).astype(BF16)


def _inproj(x, mod, g, w, raw_blocks, taps, scales, plain_acts, tm, ctx_row):
    b, n, _ = x.shape
    wn = len(plain_acts) * GROUP_W
    w_cols = len(scales) * GROUP_W + wn + GATE_PAD
    assert len(raw_blocks) == len(scales) + len(plain_acts) and 0 < w.shape[1] - MAIN_W <= GATE_PAD
    assert w.shape[1] % 8 == 0
    w = w.T
    assert tm % ROW_GROUP == 0 and n % tm == 0 and wn % LANES == 0
    rows8 = tm // 8
    last8 = n // 8 - 1
    full = lambda a: pl.BlockSpec(a.shape, lambda i, j: (0,) * a.ndim)
    conv_spec = pl.BlockSpec((1, tm, GROUP_W), lambda i, j: (i, j, 0))
    return pl.pallas_call(
        functools.partial(_inproj_kernel, ctx_row=ctx_row, scales=tuple(scales),
                          plain_acts=tuple(plain_acts), raw_blocks=tuple(raw_blocks)),
        grid=(b, n // tm),
        in_specs=[pl.BlockSpec((1, tm, D_MODEL), lambda i, j: (i, j, 0)),
                  pl.BlockSpec((1, 8, D_MODEL), lambda i, j: (i, jnp.maximum(j * rows8 - 1, 0), 0)),
                  pl.BlockSpec((1, 8, D_MODEL), lambda i, j: (i, jnp.minimum((j + 1) * rows8, last8), 0)),
                  full(mod), full(g), full(taps), full(w)],
        out_specs=[conv_spec] * len(scales)
        + [pl.BlockSpec((1, tm, wn), lambda i, j: (i, j, 0)),
           pl.BlockSpec((1, tm, GATE_PAD), lambda i, j: (i, j, 0))],
        out_shape=[jax.ShapeDtypeStruct((b, n, GROUP_W), BF16)] * len(scales)
        + [jax.ShapeDtypeStruct((b, n, wn), F32), jax.ShapeDtypeStruct((b, n, GATE_PAD), F32)],
        scratch_shapes=[pltpu.VMEM((D_MODEL, w_cols), BF16)],
        compiler_params=_params("arbitrary", "arbitrary"),
        name="inproj",
    )(x, x, x, mod, g, taps, w)


def _mlstm_chunk(per_dir, bias_ref, c_s, n_s, m_s, h_refs):
    with_h = h_refs[0] is not None
    i0 = _iota2((CHUNK, CHUNK), 0)
    i1 = _iota2((CHUNK, CHUNK), 1)

    units = []
    for smp, d in [(smp, d) for smp in range(c_s.shape[0]) for d in range(2)]:
        q_ref, k_ref, v_ref, g_ref = per_dir[d]
        sees_ts = (i1 <= i0) if d == 0 else (i1 >= i0)
        sees_st = (i0 <= i1) if d == 0 else (i0 >= i1)
        tri = jnp.where(sees_ts, 1.0, 0.0).astype(BF16)
        gates = g_ref[smp] + bias_ref[...]
        bcum = _dot_exact01(tri, jax.nn.log_sigmoid(gates))
        for hd in range(HEADS):
            lanes = slice(hd * HEAD_DIM, (hd + 1) * HEAD_DIM)
            c_prev = c_s[smp, d, hd]
            n_prev = n_s[smp, d, hd]
            k = k_ref[smp, :, lanes]
            st = dict(smp=smp, d=d, hd=hd, lanes=lanes, c_prev=c_prev, n_prev=n_prev, k=k,
                      sees_st=sees_st, gates=gates, bcum=bcum, v=v_ref[smp, :, lanes])
            if with_h:
                lhs = jnp.concatenate([k, c_prev.astype(BF16),
                                       jnp.broadcast_to(n_prev, (BF16_ROWS, HEAD_DIM)).astype(BF16)], axis=0)
                st["prod"] = _dot_nt(lhs, q_ref[smp, :, lanes])
            units.append(st)

    rows = {}
    for st in units:
        smp, d, hd = st["smp"], st["d"], st["hd"]
        if (smp, d) not in rows:
            rows[smp, d] = (st["gates"].T, st["bcum"].T)
        gates_t, bcum_t = rows[smp, d]
        ci = d * HEADS + hd
        cf = 2 * HEADS + d * HEADS + hd
        last = CHUNK - 1 if d == 0 else 0
        li_row = gates_t[ci:ci + 1, :]
        bc_row = bcum_t[cf:cf + 1, :]
        b_last = bc_row[:, last:last + 1]
        m_prev = m_s[smp, d, hd]
        v_t = st["v"].T
        a_row = b_last - bc_row + li_row
        m_new = jnp.maximum(b_last + m_prev, jnp.max(a_row, axis=-1, keepdims=True))
        w_row = jnp.exp(a_row - m_new)
        lhs = jnp.concatenate([v_t * w_row, jnp.broadcast_to(w_row, (BF16_ROWS, CHUNK))], axis=0)
        st["upd"] = _dot(lhs.astype(BF16), st["k"])
        st.update(bc_row=bc_row, m_prev=m_prev, m_new=m_new, v_t=v_t,
                  decay=jnp.exp(b_last + m_prev - m_new),
                  u_col=st["gates"][:, ci:ci + 1] - st["bcum"][:, cf:cf + 1])

    if with_h:
        for st in units:
            prod, bc_row = st["prod"], st["bc_row"]
            g = bc_row + st["m_prev"]
            dmat = jnp.where(st["sees_st"], st["u_col"] + bc_row, -jnp.inf)
            m_t = jnp.maximum(g, jnp.max(dmat, axis=0, keepdims=True))
            inter = jnp.exp(g - m_t)
            s = prod[0:CHUNK] * jnp.exp(dmat - m_t)
            st["pv"] = _dot(st["v_t"].astype(BF16), s.astype(BF16))
            st["num0"] = inter * prod[CHUNK:2 * CHUNK]
            den = inter * prod[2 * CHUNK:2 * CHUNK + 1] + jnp.sum(s, axis=0, keepdims=True)
            st["scale"] = 1.0 / jnp.maximum(jnp.abs(den), jnp.exp(-m_t))

    for st in units:
        smp, d, hd = st["smp"], st["d"], st["hd"]
        if with_h:
            h_refs[d][smp, :, st["lanes"]] = ((st["num0"] + st["pv"]) * st["scale"]).T
        c_s[smp, d, hd] = st["decay"] * st["c_prev"] + st["upd"][0:HEAD_DIM]
        n_s[smp, d, hd] = st["decay"] * st["n_prev"] + st["upd"][HEAD_DIM:HEAD_DIM + 1]
        m_s[smp, d, hd] = st["m_new"]


def _mlstm_kernel(*refs, n_chunks, with_h):
    refs = list(refs)
    take = lambda n: [refs.pop(0) for _ in range(n)]
    per_dir = []
    for _ in range(2):
        q_ref = take(1)[0] if with_h else None
        k_ref, v_ref, g_ref = take(3)
        per_dir.append((q_ref, k_ref, v_ref, g_ref))
    bias_ref = take(1)[0]
    if with_h:
        c0_ref, n0_ref, m0_ref = take(3)
        h_refs = take(2)
        c_out = n_out = m_out = None
    else:
        h_refs = [None, None]
        c_out, n_out, m_out = take(3)
    c_s, n_s, m_s = take(3)

    j = pl.program_id(1)

    @pl.when(j == 0)
    def _():
        if with_h:
            c_s[...] = c0_ref[...]
            n_s[...] = n0_ref[...]
            m_s[...] = m0_ref[...]
        else:
            c_s[...] = jnp.zeros_like(c_s)
            n_s[...] = jnp.zeros_like(n_s)
            m_s[...] = jnp.zeros_like(m_s)

    _mlstm_chunk(per_dir, bias_ref, c_s, n_s, m_s, h_refs)

    if not with_h:
        @pl.when(j == n_chunks - 1)
        def _():
            c_out[...] = c_s[...]
            n_out[...] = n_s[...]
            m_out[...] = m_s[...]


def _mlstm(q, k, p, v_blk, gates, bias, state):
    b, n, _ = k.shape
    n_chunks = n // CHUNK
    with_h = q is not None
    ns = MLSTM_SAMPLES if b % MLSTM_SAMPLES == 0 else 1

    in_specs, args = [], []
    for d in range(2):
        c = (lambda j: j) if d == 0 else (lambda j: n_chunks - 1 - j)
        tok = pl.BlockSpec((ns, CHUNK, GROUP_W), lambda i, j, c=c: (i, c(j), 0))
        if with_h:
            in_specs.append(tok)
            args.append(q)
        in_specs += [tok,
                     pl.BlockSpec((ns, CHUNK, GROUP_W), lambda i, j, c=c: (i, c(j), v_blk)),
                     pl.BlockSpec((ns, CHUNK, GATE_PAD), lambda i, j, c=c: (i, c(j), 0))]
        args += [k, p, gates]
    in_specs.append(pl.BlockSpec((1, GATE_PAD), lambda i, j: (0, 0)))
    args.append(bias)

    c_shape = (2, HEADS, HEAD_DIM, HEAD_DIM)
    v_shape = (2, HEADS, 1, HEAD_DIM)
    c_spec = pl.BlockSpec((ns,) + c_shape, lambda i, j: (i, 0, 0, 0, 0))
    v_spec = pl.BlockSpec((ns,) + v_shape, lambda i, j: (i, 0, 0, 0, 0))
    if with_h:
        in_specs += [c_spec, v_spec, v_spec]
        args += list(state)
        out_specs = [pl.BlockSpec((ns, CHUNK, GROUP_W), lambda i, j: (i, j, 0)),
                     pl.BlockSpec((ns, CHUNK, GROUP_W), lambda i, j: (i, n_chunks - 1 - j, 0))]
        out_shape = [jax.ShapeDtypeStruct((b, n, GROUP_W), F32)] * 2
    else:
        out_specs = [c_spec, v_spec, v_spec]
        out_shape = [jax.ShapeDtypeStruct((b,) + c_shape, F32),
                     jax.ShapeDtypeStruct((b,) + v_shape, F32),
                     jax.ShapeDtypeStruct((b,) + v_shape, F32)]
    return pl.pallas_call(
        functools.partial(_mlstm_kernel, n_chunks=n_chunks, with_h=with_h),
        grid=(b // ns, n_chunks),
        in_specs=in_specs,
        out_specs=out_specs,
        out_shape=out_shape,
        scratch_shapes=[pltpu.VMEM((ns,) + c_shape, F32), pltpu.VMEM((ns,) + v_shape, F32),
                        pltpu.VMEM((ns,) + v_shape, F32)],
        compiler_params=_params("arbitrary", "arbitrary"),
        name="mlstm" if with_h else "mlstm_ctx_state",
    )(*args)


def _layer_norm(x):
    mu = jnp.mean(x, axis=-1, keepdims=True)
    var = jnp.mean(jnp.square(x - mu), axis=-1, keepdims=True)
    return (x - mu) * lax.rsqrt(var + EPS)


def _postmix_kernel(u_ref, vg_ref, o_ref, hf_ref, hb_ref, x_ref, mod_ref, ws_ref, bs_ref, ng_ref,
                    wout_ref, fg_ref, wr_ref, x1_ref, h2_ref, aff_ref, ycat_s, ws_s, wout_s):
    @pl.when(jnp.logical_and(pl.program_id(0) == 0, pl.program_id(1) == 0))
    def _():
        ws_s[...] = ws_ref[...].astype(BF16)
        wout_s[...] = wout_ref[...].astype(BF16)

    tm = x_ref.shape[1]
    b = pl.program_id(0)
    mod = lambda k: mod_ref[pl.ds(b, 1), k * D_MODEL:(k + 1) * D_MODEL]
    wr_hi, wr_lo = _split2(wr_ref[...])
    expert_lane = _iota2((ROW_GROUP, LANES), 1) < N_EXPERTS

    for r in range(tm // ROW_GROUP):
        rows = slice(r * ROW_GROUP, (r + 1) * ROW_GROUP)

        for c in range(r * ROW_GROUP // CHUNK, (r + 1) * ROW_GROUP // CHUNK):
            crows = slice(c * CHUNK, (c + 1) * CHUNK)
            u = u_ref[0, crows, :]
            v = vg_ref[0, crows, :].astype(BF16)
            for hd in range(HEADS):
                lanes = slice(hd * HEAD_DIM, (hd + 1) * HEAD_DIM)
                s = _dot(ws_s[hd], v[:, lanes]) + bs_ref[hd]
                ycat_s[crows, lanes] = (u[:, lanes] * s).astype(BF16)

        hsum = hf_ref[0, rows, :] + hb_ref[0, rows, :]
        o = o_ref[0, rows, :]
        for hd in range(HEADS):
            lanes = slice(hd * HEAD_DIM, (hd + 1) * HEAD_DIM)
            hn = _layer_norm(hsum[:, lanes]) * ng_ref[:, lanes]
            ycat_s[rows, GROUP_W + hd * HEAD_DIM:GROUP_W + (hd + 1) * HEAD_DIM] = (
                o[:, lanes] * hn).astype(BF16)

        y = _dot(ycat_s[rows, :], wout_s[...])
        x1 = x_ref[0, rows, :] + mod(2) * y
        x1_ref[0, rows, :] = x1

        n2 = x1 * lax.rsqrt(jnp.mean(x1 * x1, axis=-1, keepdims=True) + EPS) * fg_ref[...]
        h2 = n2 * (1.0 + mod(4)) + mod(3)
        h2_hi, h2_lo = _split2(h2)
        h2_ref[0, rows, :] = h2_hi

        logits = _dot(h2_hi, wr_hi) + (_dot(h2_lo, wr_hi) + _dot(h2_hi, wr_lo))
        logits = jnp.where(expert_lane, logits, -jnp.inf)
        e = jnp.exp(logits - jnp.max(logits, axis=-1, keepdims=True))
        aff = e / jnp.sum(e, axis=-1, keepdims=True)
        aff_ref[0, :, rows] = aff.T[0:N_EXPERTS, :]


def _postmix(p, hf, hb, x, mod, ws, bs, ng, wout, fg, wr, tm):
    b, n, _ = x.shape
    tok = lambda blk: pl.BlockSpec((1, tm, GROUP_W), lambda i, j: (i, j, blk))
    full = lambda a: pl.BlockSpec(a.shape, lambda i, j: (0,) * a.ndim)
    return pl.pallas_call(
        _postmix_kernel,
        grid=(b, n // tm),
        in_specs=[tok(P_U), tok(P_VG), tok(P_O), tok(0), tok(0),
                  pl.BlockSpec((1, tm, D_MODEL), lambda i, j: (i, j, 0)),
                  full(mod), full(ws), full(bs), full(ng), full(wout), full(fg), full(wr)],
        out_specs=[pl.BlockSpec((1, tm, D_MODEL), lambda i, j: (i, j, 0)),
                   pl.BlockSpec((1, tm, D_MODEL), lambda i, j: (i, j, 0)),
                   pl.BlockSpec((1, N_EXPERTS, tm), lambda i, j: (i, 0, j))],
        out_shape=[jax.ShapeDtypeStruct((b, n, D_MODEL), F32),
                   jax.ShapeDtypeStruct((b, n, D_MODEL), BF16),
                   jax.ShapeDtypeStruct((b, N_EXPERTS, n), F32)],
        scratch_shapes=[pltpu.VMEM((tm, D_MODEL), BF16), pltpu.VMEM(ws.shape, BF16),
                        pltpu.VMEM(wout.shape, BF16)],
        compiler_params=_params("arbitrary", "arbitrary"),
        name="postmix",
    )(p, p, p, hf, hb, x, mod, ws, bs, ng, wout, fg, wr)


def _cumsum_lanes(x, upper):
    carry = jnp.zeros((x.shape[0], 1), F32)
    outs, before = [], []
    for j in range(x.shape[1] // LANES):
        before.append(carry)
        c = _dot(x[:, j * LANES:(j + 1) * LANES].astype(BF16), upper) + carry
        outs.append(c)
        carry = c[:, LANES - 1:LANES]
    return jnp.concatenate(outs, axis=1), before


def _rows_to_lanes(x, fill):
    pad = jnp.full((LANES - x.shape[0], LANES), fill, F32)
    return jnp.concatenate([x, pad], axis=0).T


def _route_kernel(aff_ref, slot_ref, w_ref, slot_t_ref, first_t_ref, first_ref, *, capacity, n_e):
    aff = aff_ref[...]
    cap = float(capacity)
    thr_bits = jnp.zeros((aff.shape[0], 1), jnp.int32)
    for bit in range(30, -1, -1):
        cand = thr_bits | (1 << bit)
        cnt = jnp.sum(jnp.where(aff >= pltpu.bitcast(cand, F32), 1.0, 0.0), axis=-1, keepdims=True)
        thr_bits = jnp.where(cnt >= cap, cand, thr_bits)
    thr = pltpu.bitcast(thr_bits, F32)
    upper = jnp.where(_iota2((LANES, LANES), 0) <= _iota2((LANES, LANES), 1), 1.0, 0.0).astype(BF16)
    above = jnp.where(aff > thr, 1.0, 0.0)
    tied = jnp.where(aff == thr, 1.0, 0.0)
    need = cap - jnp.sum(above, axis=-1, keepdims=True)
    sel = above + tied * jnp.where(_cumsum_lanes(tied, upper)[0] <= need, 1.0, 0.0)
    count, before = _cumsum_lanes(sel, upper)
    slot = jnp.where(sel > 0.0, count - 1.0, -1.0)
    slot_ref[...] = slot
    w_ref[...] = jnp.where(sel > 0.0, aff, 0.0)

    n_blocks = len(before)
    lane = _iota2((1, LANES), 1)
    first = jnp.zeros((aff.shape[0], LANES), F32)
    for j in range(n_blocks):
        first = jnp.where(lane == j, before[j], first)
    first_ref[...] = first
    for smp in range(aff.shape[0] // n_e):
        rows = slice(smp * n_e, (smp + 1) * n_e)
        for j in range(n_blocks):
            slot_t_ref[smp, j * LANES:(j + 1) * LANES, :] = _rows_to_lanes(
                slot[rows, j * LANES:(j + 1) * LANES], -1.0)
        first_t_ref[smp] = _rows_to_lanes(first[rows, :], 0.0)[0:n_blocks, :]


def _route(aff_t, capacity):
    b, e, n = aff_t.shape
    assert n // LANES <= LANES and e <= LANES and e % 8 == 0
    rows = pl.BlockSpec((b * e, n), lambda i: (0, 0))
    slot, w, slot_t, first_t, first = pl.pallas_call(
        functools.partial(_route_kernel, capacity=capacity, n_e=e),
        grid=(1,),
        in_specs=[rows],
        out_specs=[rows, rows,
                   pl.BlockSpec((b, n, LANES), lambda i: (0, 0, 0)),
                   pl.BlockSpec((b, n // LANES, LANES), lambda i: (0, 0, 0)),
                   pl.BlockSpec((b * e, LANES), lambda i: (0, 0))],
        out_shape=[jax.ShapeDtypeStruct((b * e, n), F32)] * 2
        + [jax.ShapeDtypeStruct((b, n, LANES), F32), jax.ShapeDtypeStruct((b, n // LANES, LANES), F32),
           jax.ShapeDtypeStruct((b * e, LANES), F32)],
        compiler_params=_params("arbitrary"),
        name="route",
    )(aff_t.reshape(b * e, n))
    return slot.reshape(b, e, n), w.reshape(b, e, n), slot_t, first_t, first.reshape(b, e, LANES)


GATHER_TILE = 256
GATHER_ROWS = (64, 128)
GATHER_GROUP = 4


def _gather_kernel(before_ref, fits_ref, h2_ref, slot_ref, w_ref, xe_ref, gate_ref, xe_s, gate_s):
    b = pl.program_id(0)
    g = pl.program_id(1)
    n_g, cap = xe_ref.shape[1], xe_ref.shape[2]
    n = h2_ref.shape[1]
    n_tiles = n // GATHER_TILE
    head = BF16_ROWS
    row0 = (g * n_g) % 8

    def tile_row(ref, e, j):
        return ref[pl.ds(row0 + e, 1), j * GATHER_TILE:(j + 1) * GATHER_TILE]

    def windowed(n_rows):
        xe_s[:, 0:head, :] = jnp.zeros((n_g, head, D_MODEL), BF16)
        gate_s[...] = jnp.zeros_like(gate_s)
        row_id = _iota2((n_rows, 1), 0).astype(F32)
        for j in range(n_tiles):
            starts, onehots = [], []
            for e in range(n_g):
                start = pl.multiple_of((before_ref[b, g * n_g + e, j] // head) * head, head)
                hit = (tile_row(slot_ref, e, j) - start.astype(F32)) == row_id
                onehots.append(jnp.where(hit, 1.0, 0.0).astype(BF16))
                gate_s[e, pl.ds(start, n_rows), :] += jnp.sum(
                    jnp.where(hit, tile_row(w_ref, e, j), 0.0), axis=1, keepdims=True)
                starts.append(start)
            rows = _dot(jnp.concatenate(onehots, axis=0),
                        h2_ref[0, j * GATHER_TILE:(j + 1) * GATHER_TILE, :]).astype(BF16)
            for e, start in enumerate(starts):
                r0 = e * n_rows
                xe_s[e, pl.ds(start, head), :] += rows[r0:r0 + head]
                xe_s[e, pl.ds(start + head, n_rows - head), :] = rows[r0 + head:r0 + n_rows]
        xe_ref[0] = xe_s[:, 0:cap, :]
        for e in range(n_g):
            gate_ref[0, e] = _cols_to_rows(gate_s[e, 0:cap, :])

    for level, n_rows in enumerate(GATHER_ROWS):
        pl.when(fits_ref[b, g] == level)(functools.partial(windowed, n_rows))

    @pl.when(fits_ref[b, g] == len(GATHER_ROWS))
    def _():
        slot_id = _iota2((cap, 1), 0).astype(F32)
        for e in range(n_g):
            xe = jnp.zeros((cap, D_MODEL), F32)
            gate = jnp.zeros((cap, 1), F32)
            for j in range(n_tiles):
                hit = tile_row(slot_ref, e, j) == slot_id
                xe = xe + _dot(jnp.where(hit, 1.0, 0.0).astype(BF16),
                               h2_ref[0, j * GATHER_TILE:(j + 1) * GATHER_TILE, :])
                gate = gate + jnp.sum(jnp.where(hit, tile_row(w_ref, e, j), 0.0), axis=1,
                                      keepdims=True)
            xe_ref[0, e] = xe.astype(BF16)
            gate_ref[0, e] = _cols_to_rows(gate)


def _gather(h2, slot, w, first, capacity):
    b, n, _ = h2.shape
    n_e = slot.shape[1]
    n_tiles = n // GATHER_TILE
    assert n_e % GATHER_GROUP == 0 and GATHER_TILE % LANES == 0 and capacity % LANES == 0
    assert all(r % BF16_ROWS == 0 and r >= 2 * BF16_ROWS for r in GATHER_ROWS)
    assert list(GATHER_ROWS) == sorted(GATHER_ROWS)
    before = first[:, :, 0:n // LANES:GATHER_TILE // LANES].astype(jnp.int32)
    after = jnp.concatenate([before[:, :, 1:], jnp.full((b, n_e, 1), capacity, jnp.int32)], axis=2)
    extent = jnp.max(after - (before // BF16_ROWS) * BF16_ROWS, axis=2)
    extent = jnp.max(extent.reshape(b, n_e // GATHER_GROUP, GATHER_GROUP), axis=2)
    fits = sum((extent >= n_rows).astype(jnp.int32) for n_rows in GATHER_ROWS)
    assert 8 % GATHER_GROUP == 0 and n_e % 8 == 0
    tiles = lambda a: a.reshape(b * n_e, n)
    rows = pl.BlockSpec((8, n), lambda i, g, *_: (i * (n_e // 8) + (g * GATHER_GROUP) // 8, 0))
    return pl.pallas_call(
        _gather_kernel,
        grid_spec=pltpu.PrefetchScalarGridSpec(
            num_scalar_prefetch=2,
            grid=(b, n_e // GATHER_GROUP),
            in_specs=[pl.BlockSpec((1, n, D_MODEL), lambda i, g, *_: (i, 0, 0)), rows, rows],
            out_specs=[pl.BlockSpec((1, GATHER_GROUP, capacity, D_MODEL), lambda i, g, *_: (i, g, 0, 0)),
                       pl.BlockSpec((1, GATHER_GROUP, capacity // LANES, LANES),
                                    lambda i, g, *_: (i, g, 0, 0))],
            scratch_shapes=[pltpu.VMEM((GATHER_GROUP, capacity + max(GATHER_ROWS), D_MODEL), BF16),
                            pltpu.VMEM((GATHER_GROUP, capacity + max(GATHER_ROWS), 1), F32)]),
        out_shape=[jax.ShapeDtypeStruct((b, n_e, capacity, D_MODEL), BF16),
                   jax.ShapeDtypeStruct((b, n_e, capacity // LANES, LANES), F32)],
        compiler_params=_params("arbitrary", "arbitrary"),
        name="gather",
    )(before, fits, h2, tiles(slot), tiles(w))


def _ffn_kernel(xe_ref, gate_ref, wg_ref, wu_ref, wd_ref, ye_ref, acc_s):
    ft = pl.program_id(1)
    last = pl.num_programs(1) - 1

    def step(first, final):
        wg = wg_ref[0].astype(BF16)
        wu = wu_ref[0].astype(BF16)
        wd = wd_ref[0].astype(BF16)
        for i in range(xe_ref.shape[0]):
            xe = xe_ref[i, 0]
            act = jax.nn.silu(_dot(xe, wg)) * _dot(xe, wu)
            part = _dot(act.astype(BF16), wd)
            total = part if first else acc_s[i] + part
            if final:
                ye_ref[i] = (total * _rows_to_cols(gate_ref[i, 0])).astype(BF16)
            else:
                acc_s[i] = total

    pl.when(ft == 0)(lambda: step(True, False))
    pl.when(jnp.logical_and(ft > 0, ft < last))(lambda: step(False, False))
    pl.when(ft == last)(lambda: step(False, True))


def _ffn(xe, gate, wg, wu, wd, f_tile):
    b, n_e, cap, _ = xe.shape
    assert D_EXPERT // f_tile >= 2
    return pl.pallas_call(
        _ffn_kernel,
        grid=(n_e, D_EXPERT // f_tile),
        in_specs=[pl.BlockSpec((b, 1, cap, D_MODEL), lambda e, f: (0, e, 0, 0)),
                  pl.BlockSpec((b, 1, cap // LANES, LANES), lambda e, f: (0, e, 0, 0)),
                  pl.BlockSpec((1, D_MODEL, f_tile), lambda e, f: (e, 0, f)),
                  pl.BlockSpec((1, D_MODEL, f_tile), lambda e, f: (e, 0, f)),
                  pl.BlockSpec((1, f_tile, D_MODEL), lambda e, f: (e, f, 0))],
        out_specs=pl.BlockSpec((b, cap, D_MODEL), lambda e, f: (0, e, 0)),
        out_shape=jax.ShapeDtypeStruct((b, n_e * cap, D_MODEL), BF16),
        scratch_shapes=[pltpu.VMEM((b, cap, D_MODEL), F32)],
        compiler_params=_params("arbitrary", "arbitrary"),
        name="ffn",
    )(xe, gate, wg, wu, wd)


COMBINE_WIDE = CHUNK + BF16_ROWS
COMBINE_NARROW = 48


def _combine_kernel(first_ref, narrow_ref, slot_t_ref, first_t_ref, ye_ref, x1_ref, mod_ref, fg_ref,
                    o_ref, acc_s, *, capacity):
    b = pl.program_id(0)
    j = pl.program_id(1)
    n_sub = o_ref.shape[1] // CHUNK
    lane = _iota2((1, LANES), 1).astype(F32)

    def scatter(window):
        k_total = N_EXPERTS * window
        for sb in range(n_sub):
            blk = j * n_sub + sb
            slot_t = slot_t_ref[0, sb * CHUNK:(sb + 1) * CHUNK, :]
            start_row = jnp.minimum(
                jnp.floor(first_t_ref[0, pl.ds(blk, 1), :] * (1.0 / BF16_ROWS)) * float(BF16_ROWS),
                float(capacity - window))
            k_pos = jnp.where(slot_t >= 0.0, slot_t - start_row + lane * float(window), -1.0)
            cols = []
            for c in range(k_total // LANES):
                k_lane = lane + float(LANES * c)
                hit = jnp.zeros((CHUNK, LANES), F32)
                for e in range((LANES * c) // window, (LANES * c + LANES - 1) // window + 1):
                    hit = jnp.where(k_pos[:, e:e + 1] == k_lane, 1.0, hit)
                cols.append(hit.astype(BF16))
            onehot = jnp.concatenate(cols, axis=1)
            rows = []
            for e in range(N_EXPERTS):
                start = jnp.minimum((first_ref[b, blk, e] // BF16_ROWS) * BF16_ROWS, capacity - window)
                rows.append(ye_ref[0, pl.ds(pl.multiple_of(e * capacity + start, BF16_ROWS), window), :])
            acc_s[sb * CHUNK:(sb + 1) * CHUNK, :] = _dot(onehot, jnp.concatenate(rows, axis=0))

    pl.when(narrow_ref[b, j] != 0)(lambda: scatter(COMBINE_NARROW))
    pl.when(narrow_ref[b, j] == 0)(lambda: scatter(COMBINE_WIDE))

    g2 = mod_ref[pl.ds(b, 1), 5 * D_MODEL:6 * D_MODEL]
    x2 = x1_ref[0] + g2 * acc_s[...]
    o_ref[0] = x2 * lax.rsqrt(jnp.mean(x2 * x2, axis=-1, keepdims=True) + EPS) * fg_ref[...]


def _combine(slot_t, first_t, ye, x1, mod, fg, capacity, tm):
    b, n, _ = x1.shape
    n_blocks = n // CHUNK
    assert CHUNK == LANES and N_EXPERTS <= LANES and tm % CHUNK == 0
    for window in (COMBINE_WIDE, COMBINE_NARROW):
        assert (N_EXPERTS * window) % LANES == 0 and window % BF16_ROWS == 0 and window <= capacity
    assert COMBINE_WIDE >= CHUNK + BF16_ROWS - 1 and capacity % BF16_ROWS == 0
    first = first_t[:, :, :N_EXPERTS].astype(jnp.int32)
    after = jnp.concatenate([first[:, 1:], jnp.full((b, 1, N_EXPERTS), capacity, jnp.int32)], axis=1)
    narrow = jnp.all(after - (first // BF16_ROWS) * BF16_ROWS <= COMBINE_NARROW, axis=2)
    narrow = jnp.all(narrow.reshape(b, n // tm, tm // CHUNK), axis=2).astype(jnp.int32)
    tok = pl.BlockSpec((1, tm, D_MODEL), lambda i, j, *_: (i, j, 0))
    return pl.pallas_call(
        functools.partial(_combine_kernel, capacity=capacity),
        grid_spec=pltpu.PrefetchScalarGridSpec(
            num_scalar_prefetch=2,
            grid=(b, n // tm),
            in_specs=[pl.BlockSpec((1, tm, LANES), lambda i, j, *_: (i, j, 0)),
                      pl.BlockSpec((1, n_blocks, LANES), lambda i, j, *_: (i, 0, 0)),
                      pl.BlockSpec((1, N_EXPERTS * capacity, D_MODEL), lambda i, j, *_: (i, 0, 0)),
                      tok,
                      pl.BlockSpec(mod.shape, lambda i, j, *_: (0, 0)),
                      pl.BlockSpec((1, D_MODEL), lambda i, j, *_: (0, 0))],
            out_specs=tok,
            scratch_shapes=[pltpu.VMEM((tm, D_MODEL), F32)]),
        out_shape=jax.ShapeDtypeStruct((b, n, D_MODEL), F32),
        compiler_params=_params("arbitrary", "arbitrary"),
        name="combine",
    )(first, narrow, slot_t, first_t, ye, x1, mod, fg)


def kernel(x, c, ctx, c_ctx, w_mod, b_mod, norm_mix_g, w_in, conv_q, conv_k, b_igate, b_fgate,
           gmlp_ws, gmlp_bs, mlstm_norm_g, w_out, norm_ffn_g, w_router, w_gate_e, w_up_e,
           w_down_e, final_g):
    depth = w_mod.shape[0]
    assert depth == 1, "the context stream is only carried as mLSTM states (single layer)"
    batch, seq, _ = x.shape
    assert seq % GRID_W == 0 and seq % CHUNK == 0 and batch + 1 <= MOD_ROWS
    capacity = EC_FACTOR * seq // N_EXPERTS
    ctx_row = batch
    l = 0

    cond = jnp.concatenate([c, c_ctx[None], jnp.zeros((MOD_ROWS - batch - 1, D_MODEL), F32)], axis=0)
    mod = _adaln(cond, w_mod[l], b_mod[l][None])

    row = lambda a: a[None]
    gate_bias = jnp.pad(jnp.concatenate([b_igate[l].reshape(-1), b_fgate[l].reshape(-1)]),
                        (0, GATE_PAD - N_GATES))[None]
    k_scale = HEAD_DIM ** -0.5

    identity = lambda a: a
    k_c, v_c, gates_c = _inproj(ctx, mod, row(norm_mix_g[l]), w_in[l], (K_BLK, V_BLK), conv_k[l][None],
                                (k_scale,), (identity,), tm=ctx.shape[1], ctx_row=ctx_row)
    state = _mlstm(None, k_c, v_c, 0, gates_c, gate_bias, None)

    q_l, k_l, p, gates = _inproj(x, mod, row(norm_mix_g[l]), w_in[l],
                                 (Q_BLK, K_BLK, U_BLK, VG_BLK, O_BLK, V_BLK),
                                 jnp.stack([conv_q[l], conv_k[l]]), (1.0, k_scale),
                                 (jax.nn.gelu, lambda a: _layer_norm(jax.nn.gelu(a)), jax.nn.sigmoid, identity),
                                 tm=512, ctx_row=None)
    h_f, h_b = _mlstm(q_l, k_l, p, P_V, gates, gate_bias, state)

    x1, h2, aff_t = _postmix(p, h_f, h_b, x, mod, gmlp_ws[l], gmlp_bs[l][:, :, None],
                             row(mlstm_norm_g[l]), w_out[l], row(norm_ffn_g[l]),
                             jnp.pad(w_router[l], ((0, 0), (0, LANES - N_EXPERTS))), tm=1024)

    slot, gate_w, slot_t, first_t, first = _route(aff_t, capacity)

    xe, gate = _gather(h2, slot, gate_w, first, capacity)
    ye = _ffn(xe, gate, w_gate_e[l], w_up_e[l], w_down_e[l], f_tile=512)
    return _combine(slot_t, first_t, ye, x1, mod, row(final_g), capacity, tm=512)
```
